```python
import math
import jax
import jax.numpy as jnp
from jax import lax
import numpy as np

D_MODEL = 2048
BATCH = 4
SEQ = 2048
DEPTH = 1

GRID_W = 64
CTX_LEN = 256
EXPAND = 2
MIX_WIDTH = EXPAND * D_MODEL
SSD_WIDTH = MIX_WIDTH // 2
CM_WIDTH = MIX_WIDTH - SSD_WIDTH
SSD_HEAD_DIM = 64
SSD_HEADS = SSD_WIDTH // SSD_HEAD_DIM
SSD_GROUPS = 4
SSD_STATE = 128
SSD_CHUNK = 128
CONV_K = 3
BC_WIDTH = SSD_GROUPS * SSD_STATE
CONV_DIM = SSD_WIDTH + 2 * BC_WIDTH
CM_HEADS = 8
CM_HEAD_DIM = CM_WIDTH // CM_HEADS
CM_CHUNK = 128
IN_COLS = SSD_WIDTH + CONV_DIM + 2 * SSD_HEADS + 2 * CM_WIDTH
N_EXPERT_GROUPS = 4
EXPERTS_PER_GROUP = 8
EXPERT_TOP_K = 2
EXPERT_FF = D_MODEL // 4
N_MOD = 6
NORM_EPS = 1e-6
DT_MIN = 1e-3
DT_MAX = 1e-1

kernel_name = 'hybrid_bissd_chunkgmlp_hmoe_block'


def rmsnorm(x, g):
    xf = x.astype(jnp.float32)
    y = xf * lax.rsqrt(jnp.mean(xf * xf, axis=-1, keepdims=True) + NORM_EPS)
    return (y * g).astype(x.dtype)


def layernorm(x, g, b):
    xf = x.astype(jnp.float32)
    mu = jnp.mean(xf, axis=-1, keepdims=True)
    xc = xf - mu
    y = xc * lax.rsqrt(jnp.mean(xc * xc, axis=-1, keepdims=True) + NORM_EPS)
    return (y * g + b).astype(x.dtype)


def modulate(h, shift, scale):
    return h * (1.0 + scale) + shift


def dwconv_centred(x, w, b):
    k = w.shape[0]
    pad = k // 2
    n = x.shape[-2]
    xp = jnp.pad(x, [(0, 0)] * (x.ndim - 2) + [(pad, pad), (0, 0)])
    out = b
    for j in range(k):
        out = out + xp[..., j:j + n, :] * w[j]
    return out


def split_in_proj(p):
    s0 = SSD_WIDTH
    s1 = s0 + CONV_DIM
    s2 = s1 + SSD_HEADS
    s3 = s2 + SSD_HEADS
    s4 = s3 + CM_WIDTH
    return jnp.split(p, [s0, s1, s2, s3, s4], axis=-1)


def ssd_prepare(xbc, dt_f, dt_b, conv_w, conv_b, dt_bias_f, dt_bias_b, rows):
    bsz, n, _ = xbc.shape
    if rows is None:
        xbc = dwconv_centred(xbc, conv_w, conv_b)
    else:
        xbc = dwconv_centred(xbc.reshape(bsz, rows, GRID_W, CONV_DIM), conv_w, conv_b).reshape(bsz, n, CONV_DIM)
    xbc = jax.nn.silu(xbc)
    xs, bm, cm = jnp.split(xbc, [SSD_WIDTH, SSD_WIDTH + BC_WIDTH], axis=-1)
    xs = xs.reshape(bsz, n, SSD_HEADS, SSD_HEAD_DIM)
    bm = bm.reshape(bsz, n, SSD_GROUPS, SSD_STATE)
    cm = cm.reshape(bsz, n, SSD_GROUPS, SSD_STATE)
    dtf = jax.nn.softplus(dt_f.astype(jnp.float32) + dt_bias_f.astype(jnp.float32))
    dtb = jax.nn.softplus(dt_b.astype(jnp.float32) + dt_bias_b.astype(jnp.float32))
    return xs, bm, cm, dtf, dtb


def ssd_scan(xs, dt, a, bm, cm, h0, with_output):
    bsz, n, nh, hp = xs.shape
    g, ns = bm.shape[-2:]
    r = nh // g
    nc = n // SSD_CHUNK
    q = SSD_CHUNK
    dt = dt.reshape(bsz, nc, q, g, r)
    x_dt = xs.reshape(bsz, nc, q, g, r, hp) * dt[..., None]
    bc = bm.reshape(bsz, nc, q, g, ns)
    cc = cm.reshape(bsz, nc, q, g, ns)
    a_cum = jnp.cumsum(dt * a.reshape(g, r), axis=2)
    a_tot = a_cum[:, :, -1]
    decay_end = jnp.exp(a_tot[:, :, None] - a_cum)
    states = jnp.einsum('bcqgn,bcqgrp->bcgrpn', bc, x_dt * decay_end[..., None])

    def step(h, inp):
        s_c, a_c = inp
        return h * jnp.exp(a_c)[..., None, None] + s_c, h

    h_last, h_prev = lax.scan(step, h0.reshape(bsz, g, r, hp, ns).astype(jnp.float32),
                              (jnp.moveaxis(states, 1, 0), jnp.moveaxis(a_tot, 1, 0)))
    h_final = h_last.reshape(bsz, nh, hp, ns)
    if not with_output:
        return None, h_final
    h_prev = jnp.moveaxis(h_prev, 0, 1)
    y_off = jnp.einsum('bcqgn,bcgrpn->bcqgrp', cc, h_prev) * jnp.exp(a_cum)[..., None]
    seg = a_cum[:, :, :, None] - a_cum[:, :, None, :]
    lower = np.tri(q, dtype=bool)[:, :, None, None]
    decay = jnp.exp(jnp.where(lower, seg, -jnp.inf))
    cb = jnp.einsum('bcign,bcjgn->bcijg', cc, bc)
    y_diag = jnp.einsum('bcijgr,bcjgrp->bcigrp', cb[..., None] * decay, x_dt)
    y = (y_diag + y_off).reshape(bsz, n, nh, hp)
    return y, h_final


def bi_ssd(xs, dtf, dtb, a_f, a_b, bm, cm, h0_f, h0_b, with_output):
    y_f, h_f = ssd_scan(xs, dtf, a_f, bm, cm, h0_f, with_output)
    rev = lambda t: jnp.flip(t, axis=1)
    y_b, h_b = ssd_scan(rev(xs), rev(dtb), a_b, rev(bm), rev(cm), h0_b, with_output)
    y = y_f + rev(y_b) if with_output else None
    return y, h_f, h_b


def ssd_finish(y, xs, z, d_skip, norm_g):
    bsz, n = z.shape[:2]
    y = (y + xs * d_skip[:, None]).reshape(bsz, n, SSD_WIDTH).astype(z.dtype)
    return rmsnorm(y * jax.nn.silu(z), norm_g)


def chunk_gmlp(u, v, ln_g, ln_b, w_s, b_s):
    bsz, n, _ = u.shape
    nc = n // CM_CHUNK
    u = jax.nn.gelu(u)
    v = layernorm(jax.nn.gelu(v), ln_g, ln_b).reshape(bsz, nc, CM_CHUNK, CM_HEADS, CM_HEAD_DIM)
    s = jnp.einsum('gij,bcjgd->bcigd', w_s, v) + b_s.T[:, :, None]
    return u * s.reshape(bsz, n, CM_WIDTH)


def hier_moe(t, w_rg, b_rg, w_re, b_re, w_gate, w_up, w_down):
    p_g = jax.nn.softmax((t @ w_rg).astype(jnp.float32) + b_rg, axis=-1)
    top_pg, top_g = lax.top_k(p_g, 1)
    onehot_g = jax.nn.one_hot(top_g[:, 0], N_EXPERT_GROUPS, dtype=jnp.float32)
    logits_e = jnp.einsum('td,gde->tge', t, w_re).astype(jnp.float32) + b_re
    logits_sel = jnp.einsum('tge,tg->te', logits_e, onehot_g)
    top_le, top_e = lax.top_k(logits_sel, EXPERT_TOP_K)
    p_e = jax.nn.softmax(top_le, axis=-1) * top_pg
    w_e = jnp.einsum('tk,tke->te', p_e, jax.nn.one_hot(top_e, EXPERTS_PER_GROUP, dtype=jnp.float32))
    comb = (onehot_g[:, :, None] * w_e[:, None, :]).astype(t.dtype)
    out = jnp.zeros_like(t)
    for g in range(N_EXPERT_GROUPS):
        hid = jax.nn.silu(jnp.einsum('td,edf->tef', t, w_gate[g])) * jnp.einsum('td,edf->tef', t, w_up[g])
        out = out + jnp.einsum('tef,efd->td', hid * comb[:, g, :, None], w_down[g])
    return out


def setup_inputs(seed: int = 0) -> dict:
    key = jax.random.key(seed)
    ks = jax.random.split(key, 32)
    f32 = jnp.float32

    def nrm(k, shape, scale):
        return jax.random.normal(k, shape, f32) * scale

    L = DEPTH
    G, E, F = N_EXPERT_GROUPS, EXPERTS_PER_GROUP, EXPERT_FF
    log_lo, log_hi = math.log(DT_MIN), math.log(DT_MAX)
    dt0 = jnp.exp(jax.random.uniform(ks[8], (L, 2, SSD_HEADS), f32) * (log_hi - log_lo) + log_lo)
    dt_bias = dt0 + jnp.log(-jnp.expm1(-dt0))
    a_log = jnp.log(jax.random.uniform(ks[9], (L, 2, SSD_HEADS), f32, 1.0, 16.0))
    return {
        'x': nrm(ks[0], (BATCH, SEQ, D_MODEL), 1.0),
        'c': nrm(ks[1], (BATCH, D_MODEL), 1.0),
        'ctx': nrm(ks[2], (BATCH, CTX_LEN, D_MODEL), 1.0),
        'c_ctx': nrm(ks[3], (D_MODEL,), 1.0),
        'w_mod': nrm(ks[4], (L, D_MODEL, N_MOD * D_MODEL), 0.5 * D_MODEL ** -0.5),
        'b_mod': nrm(ks[5], (L, N_MOD * D_MODEL), 0.02),
        'norm1_g': 1.0 + nrm(ks[6], (L, D_MODEL), 0.02),
        'w_in': nrm(ks[7], (L, D_MODEL, IN_COLS), D_MODEL ** -0.5),
        'conv_w': nrm(ks[10], (L, CONV_K, CONV_DIM), CONV_K ** -0.5),
        'conv_b': nrm(ks[11], (L, CONV_DIM), 0.02),
        'dt_bias_f': dt_bias[:, 0],
        'dt_bias_b': dt_bias[:, 1],
        'a_log_f': a_log[:, 0],
        'a_log_b': a_log[:, 1],
        'd_skip': 1.0 + nrm(ks[12], (L, SSD_HEADS), 0.1),
        'ssd_norm_g': 1.0 + nrm(ks[13], (L, SSD_WIDTH), 0.02),
        'cm_ln_g': 1.0 + nrm(ks[14], (L, CM_WIDTH), 0.02),
        'cm_ln_b': nrm(ks[15], (L, CM_WIDTH), 0.02),
        'w_spatial': nrm(ks[16], (L, CM_HEADS, CM_CHUNK, CM_CHUNK), CM_CHUNK ** -0.5),
        'b_spatial': 1.0 + nrm(ks[17], (L, CM_HEADS, CM_CHUNK), 0.02),
        'w_out': nrm(ks[18], (L, MIX_WIDTH, D_MODEL), MIX_WIDTH ** -0.5),
        'norm2_g': 1.0 + nrm(ks[19], (L, D_MODEL), 0.02),
        'w_router_group': nrm(ks[20], (L, D_MODEL, G), D_MODEL ** -0.5),
        'b_router_group': nrm(ks[21], (L, G), 0.01),
        'w_router_expert': nrm(ks[22], (L, G, D_MODEL, E), D_MODEL ** -0.5),
        'b_router_expert': nrm(ks[23], (L, G, E), 0.01),
        'w_exp_gate': nrm(ks[24], (L, G, E, D_MODEL, F), D_MODEL ** -0.5),
        'w_exp_up': nrm(ks[25], (L, G, E, D_MODEL, F), D_MODEL ** -0.5),
        'w_exp_down': nrm(ks[26], (L, G, E, F, D_MODEL), F ** -0.5),
        'normf_g': 1.0 + nrm(ks[27], (D_MODEL,), 0.02),
    }


def reference(x, c, ctx, c_ctx, w_mod, b_mod, norm1_g, w_in, conv_w, conv_b, dt_bias_f, dt_bias_b,
              a_log_f, a_log_b, d_skip, ssd_norm_g, cm_ln_g, cm_ln_b, w_spatial, b_spatial, w_out,
              norm2_g, w_router_group, b_router_group, w_router_expert, b_router_expert,
              w_exp_gate, w_exp_up, w_exp_down, normf_g):
    bsz, seq_len, _ = x.shape
    rows = seq_len // GRID_W
    h_zero = jnp.zeros((bsz, SSD_HEADS, SSD_HEAD_DIM, SSD_STATE), jnp.float32)
    for i in range(DEPTH):
        last = i == DEPTH - 1
        mod_x = (jax.nn.silu(c) @ w_mod[i] + b_mod[i])[:, None, :]
        mod_c = jax.nn.silu(c_ctx) @ w_mod[i] + b_mod[i]
        sh1, sc1, g1, sh2, sc2, g2 = jnp.split(mod_x, N_MOD, axis=-1)
        csh1, csc1, cg1, csh2, csc2, cg2 = jnp.split(mod_c, N_MOD, axis=-1)
        a_f = -jnp.exp(a_log_f[i].astype(jnp.float32))
        a_b = -jnp.exp(a_log_b[i].astype(jnp.float32))
        p_x = modulate(rmsnorm(x, norm1_g[i]), sh1, sc1) @ w_in[i]
        p_c = modulate(rmsnorm(ctx, norm1_g[i]), csh1, csc1) @ w_in[i]
        z_x, xbc_x, dtf_x, dtb_x, u_x, v_x = split_in_proj(p_x)
        z_c, xbc_c, dtf_c, dtb_c, u_c, v_c = split_in_proj(p_c)
        xs_c, bm_c, cm_c, dtf_c, dtb_c = ssd_prepare(xbc_c, dtf_c, dtb_c, conv_w[i], conv_b[i],
                                                      dt_bias_f[i], dt_bias_b[i], None)
        y_c, hc_f, hc_b = bi_ssd(xs_c, dtf_c, dtb_c, a_f, a_b, bm_c, cm_c, h_zero, h_zero, not last)
        xs_x, bm_x, cm_x, dtf_x, dtb_x = ssd_prepare(xbc_x, dtf_x, dtb_x, conv_w[i], conv_b[i],
                                                      dt_bias_f[i], dt_bias_b[i], rows)
        y_x, _, _ = bi_ssd(xs_x, dtf_x, dtb_x, a_f, a_b, bm_x, cm_x, hc_f, hc_b, True)
        mix_x = jnp.concatenate([ssd_finish(y_x, xs_x, z_x, d_skip[i], ssd_norm_g[i]),
                                 chunk_gmlp(u_x, v_x, cm_ln_g[i], cm_ln_b[i], w_spatial[i], b_spatial[i])],
                                axis=-1)
        x = x + g1 * (mix_x @ w_out[i])
        moe_w = (w_router_group[i], b_router_group[i], w_router_expert[i], b_router_expert[i],
                 w_exp_gate[i], w_exp_up[i], w_exp_down[i])
        h2 = modulate(rmsnorm(x, norm2_g[i]), sh2, sc2)
        x = x + g2 * hier_moe(h2.reshape(-1, D_MODEL), *moe_w).reshape(x.shape)
        if not last:
            mix_c = jnp.concatenate([ssd_finish(y_c, xs_c, z_c, d_skip[i], ssd_norm_g[i]),
                                     chunk_gmlp(u_c, v_c, cm_ln_g[i], cm_ln_b[i], w_spatial[i], b_spatial[i])],
                                    axis=-1)
            ctx = ctx + cg1 * (mix_c @ w_out[i])
            h2c = modulate(rmsnorm(ctx, norm2_g[i]), csh2, csc2)
            ctx = ctx + cg2 * hier_moe(h2c.reshape(-1, D_MODEL), *moe_w).reshape(ctx.shape)
    return rmsnorm(x, normf_g)
```

```python
import functools

import jax
import jax.numpy as jnp
from jax import lax
from jax.experimental import pallas as pl
from jax.experimental.pallas import tpu as pltpu

F32 = jnp.float32
BF16 = jnp.bfloat16
HIGHEST = lax.Precision.HIGHEST

D_MODEL = 2048
GRID_W = 64
SSD_WIDTH = 2048
CM_WIDTH = 2048
SSD_HEADS = 32
SSD_HEAD_DIM = 64
SSD_GROUPS = 4
HEADS_PER_GROUP = SSD_HEADS // SSD_GROUPS
GROUP_WIDTH = HEADS_PER_GROUP * SSD_HEAD_DIM
SSD_STATE = 128
CHUNK = 128
BC_WIDTH = SSD_GROUPS * SSD_STATE
CM_HEADS = 8
CM_HEAD_DIM = CM_WIDTH // CM_HEADS
N_GROUPS = 4
N_EXPERTS = 32
EXPERTS_PER_GROUP = 8
EXPERT_FF = 512
N_MOD = 6
NORM_EPS = 1e-6
LANES = 128
NEG_BIG = -1e30
MOD_ROWS = 8
VMEM_LIMIT = 56 * 1024 * 1024


def _params(*sem):
    return pltpu.CompilerParams(dimension_semantics=sem, vmem_limit_bytes=VMEM_LIMIT)


def _silu(v):
    return v * jax.nn.sigmoid(v)


def _dot(a, b):
    return jnp.dot(a, b, preferred_element_type=F32)


def _mod_kernel(cc_ref, w_ref, b_ref, o_ref):
    a = _silu(cc_ref[...])
    hi = a.astype(BF16).astype(F32)
    lhs = jnp.concatenate([hi, a - hi], axis=0).astype(BF16)
    r = _dot(lhs, w_ref[...].astype(BF16))
    o_ref[...] = r[:MOD_ROWS] + r[MOD_ROWS:] + b_ref[...]


def _modulation(cc, w_mod, b_mod):
    n = w_mod.shape[1]
    tn = 1024
    return pl.pallas_call(
        _mod_kernel,
        grid=(n // tn,),
        in_specs=[
            pl.BlockSpec((MOD_ROWS, D_MODEL), lambda j: (0, 0)),
            pl.BlockSpec((D_MODEL, tn), lambda j: (0, j)),
            pl.BlockSpec((1, tn), lambda j: (0, j)),
        ],
        out_specs=pl.BlockSpec((MOD_ROWS, tn), lambda j: (0, j)),
        out_shape=jax.ShapeDtypeStruct((MOD_ROWS, n), F32),
        compiler_params=_params("arbitrary"),
        name="modulation",
    )(cc, w_mod, b_mod)


def _in_proj_kernel(x_ref, g_ref, sh_ref, sc_ref, w_ref, wdt_ref, o_ref, dt_ref, h_scr):
    tm = x_ref.shape[0]
    rc = 256

    @pl.when(pl.program_id(1) == 0)
    def _():
        gain = g_ref[...]
        scale = 1.0 + sc_ref[...]
        shift = sh_ref[...]

        def body(k, carry):
            r = pl.ds(pl.multiple_of(k * rc, rc), rc)
            xv = x_ref[r, :]
            ms = jnp.mean(xv * xv, axis=-1, keepdims=True)
            h = (xv * lax.rsqrt(ms + NORM_EPS) * gain) * scale + shift
            h_scr[r, :] = h.astype(BF16)
            return carry

        lax.fori_loop(0, tm // rc, body, 0)
        dt_ref[...] = _dot(h_scr[...], wdt_ref[...])

    o_ref[...] = _dot(h_scr[...], w_ref[...]).astype(o_ref.dtype)


def _in_proj(x2, mod3, mod_row_of_tile, norm_g, w_main, w_dt, tm, tn):
    m = x2.shape[0]
    n = w_main.shape[1]
    return pl.pallas_call(
        _in_proj_kernel,
        grid=(m // tm, n // tn),
        in_specs=[
            pl.BlockSpec((tm, D_MODEL), lambda i, j: (i, 0)),
            pl.BlockSpec((1, D_MODEL), lambda i, j: (0, 0)),
            pl.BlockSpec((None, 1, D_MODEL), lambda i, j: (mod_row_of_tile(i), 0, 0)),
            pl.BlockSpec((None, 1, D_MODEL), lambda i, j: (mod_row_of_tile(i), 0, 1)),
            pl.BlockSpec((D_MODEL, tn), lambda i, j: (0, j)),
            pl.BlockSpec((D_MODEL, LANES), lambda i, j: (0, 0)),
        ],
        out_specs=[
            pl.BlockSpec((tm, tn), lambda i, j: (i, j)),
            pl.BlockSpec((tm, LANES), lambda i, j: (i, 0)),
        ],
        out_shape=[
            jax.ShapeDtypeStruct((m, n), BF16),
            jax.ShapeDtypeStruct((m, LANES), F32),
        ],
        scratch_shapes=[pltpu.VMEM((tm, D_MODEL), BF16)],
        compiler_params=_params("parallel", "arbitrary"),
        name="in_proj",
    )(x2, norm_g, mod3, mod3, w_main, w_dt)


def _split3(v):
    hi = v.astype(BF16)
    rem = v - hi.astype(F32)
    mid = rem.astype(BF16)
    lo = (rem - mid.astype(F32)).astype(BF16)
    return hi, mid, lo


def _expand_heads(v, sel):
    hi, mid, lo = _split3(v)
    return _dot(hi, sel) + _dot(mid, sel) + _dot(lo, sel)


def _ssd_kernel(xs_ref, b_ref, c_ref, dt_ref, cwx_ref, cbx_ref, cwb_ref, cbb_ref, cwc_ref, cbc_ref,
                dtbias_ref, arow_ref, dskip_ref, h0f_ref, h0b_ref, *rest,
                n_chunks, conv_tile, conv_period, with_output):
    if with_output:
        y_ref, xs_s, b_s, c_s, cum_s, dts_s, hf_s, hb_s, y_s = rest
    else:
        hf_out, hb_out, xs_s, b_s, c_s, cum_s, dts_s, hf_s, hb_s = rest
    seq = n_chunks * CHUNK
    hpg = HEADS_PER_GROUP

    def conv_silu(src_ref, dst_ref, w_ref, bias_ref):
        cols = src_ref.shape[1]
        w0, w1, w2 = w_ref[0:1, :], w_ref[1:2, :], w_ref[2:3, :]
        bias = bias_ref[...]
        row = lax.broadcasted_iota(jnp.int32, (conv_tile, cols), 0) % conv_period
        first = row == 0
        last = row == conv_period - 1

        def body(k, carry):
            r = pl.ds(pl.multiple_of(k * conv_tile, conv_tile), conv_tile)
            v = src_ref[r, :].astype(F32)
            prev = jnp.where(first, 0.0, pltpu.roll(v, 1, 0))
            nxt = jnp.where(last, 0.0, pltpu.roll(v, conv_tile - 1, 0))
            y = bias + prev * w0 + v * w1 + nxt * w2
            dst_ref[r, :] = _silu(y).astype(BF16)
            return carry

        lax.fori_loop(0, seq // conv_tile, body, 0)

    conv_silu(xs_ref, xs_s, cwx_ref, cbx_ref)
    conv_silu(b_ref, b_s, cwb_ref, cbb_ref)
    conv_silu(c_ref, c_s, cwc_ref, cbc_ref)

    ii = lax.broadcasted_iota(jnp.int32, (CHUNK, CHUNK), 0)
    jj = lax.broadcasted_iota(jnp.int32, (CHUNK, CHUNK), 1)
    lower = ii >= jj
    upper = jj >= ii
    tri_lo = lower.astype(F32)
    tri_up = upper.astype(F32)
    arow = arow_ref[...]
    dtbias = dtbias_ref[...]

    def dt_body(k, carry):
        r = pl.ds(pl.multiple_of(k * CHUNK, CHUNK), CHUNK)
        raw = dt_ref[r, :] + dtbias
        dts = jnp.maximum(raw, 0.0) + jnp.log1p(jnp.exp(-jnp.abs(raw)))
        adt = dts * arow
        cf = jnp.dot(tri_lo, adt, precision=HIGHEST, preferred_element_type=F32)
        cr = jnp.dot(tri_up, adt, precision=HIGHEST, preferred_element_type=F32)
        dts_s[r, :] = dts
        cum_s[r, :] = jnp.where(jj < hpg, cf, cr)
        return carry

    lax.fori_loop(0, n_chunks, dt_body, 0)

    hf_s[...] = h0f_ref[...]
    hb_s[...] = h0b_ref[...]

    head_of_col = lax.broadcasted_iota(jnp.int32, (LANES, GROUP_WIDTH), 1) // SSD_HEAD_DIM
    sel_row = lax.broadcasted_iota(jnp.int32, (LANES, GROUP_WIDTH), 0)
    sel_f = (sel_row == head_of_col).astype(BF16)
    sel_b = (sel_row == head_of_col + hpg).astype(BF16)
    low_half = jj < SSD_HEAD_DIM

    def direction_terms(r, sel):
        cum = cum_s[r, :]
        dts = dts_s[r, :]
        tot = jnp.sum(dts * arow, axis=0, keepdims=True)
        wst = dts * jnp.exp(tot - cum)
        eoff = jnp.exp(cum)
        edec = jnp.broadcast_to(jnp.exp(tot), (8, LANES))
        ex = _expand_heads(jnp.concatenate([wst, eoff, edec], axis=0), sel)
        return cum, dts, ex[:CHUNK], ex[CHUNK:2 * CHUNK], ex[2 * CHUNK:2 * CHUNK + 1]

    def state_step(h_s, r, sel):
        cum, dts, w_state, e_off, e_dec = direction_terms(r, sel)
        xc = xs_s[r, :]
        xw = (xc.astype(F32) * w_state).astype(BF16)
        s_new = lax.dot_general(b_s[r, :], xw, (((0,), (0,)), ((), ())), preferred_element_type=F32)
        h = h_s[...]
        y_off = _dot(c_s[r, :], h.astype(BF16)) * e_off
        h_s[...] = h * e_dec + s_new
        return cum, dts, xc, y_off

    def fwd_body(k, carry):
        r = pl.ds(pl.multiple_of(k * CHUNK, CHUNK), CHUNK)
        cum, dts, xc, y_off = state_step(hf_s, r, sel_f)
        if with_output:
            g = lax.dot_general(c_s[r, :], b_s[r, :], (((1,), (1,)), ((), ())), preferred_element_type=F32)
            cum_t = cum.T
            dts_t = dts.T
            pieces = []
            for pair in range(hpg // 2):
                ms = []
                for h in (2 * pair, 2 * pair + 1):
                    hb = h + hpg
                    mf = jnp.exp(jnp.where(lower, cum[:, h:h + 1] - cum_t[h:h + 1, :], NEG_BIG)) * dts_t[h:h + 1, :]
                    mb = jnp.exp(jnp.where(upper, cum[:, hb:hb + 1] - cum_t[hb:hb + 1, :], NEG_BIG)) * dts_t[hb:hb + 1, :]
                    ms.append((g * (mf + mb)).astype(BF16))
                lhs = jnp.concatenate(ms, axis=1)
                xp = xc[:, pair * LANES:(pair + 1) * LANES]
                zero = jnp.zeros_like(xp)
                rhs = jnp.concatenate([jnp.where(low_half, xp, zero), jnp.where(low_half, zero, xp)], axis=0)
                pieces.append(_dot(lhs, rhs))
            y_diag = jnp.concatenate(pieces, axis=1)
            y_s[r, :] = y_diag + y_off + xc.astype(F32) * dskip_ref[...]
        return carry

    lax.fori_loop(0, n_chunks, fwd_body, 0)

    def bwd_body(k, carry):
        r = pl.ds(pl.multiple_of((n_chunks - 1 - k) * CHUNK, CHUNK), CHUNK)
        _, _, _, y_off = state_step(hb_s, r, sel_b)
        if with_output:
            y_ref[r, :] = (y_s[r, :] + y_off).astype(y_ref.dtype)
        return carry

    lax.fori_loop(0, n_chunks, bwd_body, 0)

    if not with_output:
        hf_out[...] = hf_s[...]
        hb_out[...] = hb_s[...]


def _ssd(pm, dtg, conv_w, conv_b, dtbias_g, arow_g, dskip_g, h0f, h0b, *, batch, seq, conv_tile, conv_period,
         with_output):
    n_chunks = seq // CHUNK
    xs_blk0 = SSD_WIDTH // GROUP_WIDTH
    b_blk0 = (2 * SSD_WIDTH + 2 * CM_WIDTH) // SSD_STATE
    c_blk0 = b_blk0 + SSD_GROUPS
    cw_b0 = SSD_WIDTH // SSD_STATE
    cw_c0 = cw_b0 + SSD_GROUPS
    state_spec = pl.BlockSpec((None, None, SSD_STATE, GROUP_WIDTH), lambda b, g: (b, g, 0, 0))
    in_specs = [
        pl.BlockSpec((seq, GROUP_WIDTH), lambda b, g: (b, xs_blk0 + g)),
        pl.BlockSpec((seq, SSD_STATE), lambda b, g: (b, b_blk0 + g)),
        pl.BlockSpec((seq, SSD_STATE), lambda b, g: (b, c_blk0 + g)),
        pl.BlockSpec((None, seq, LANES), lambda b, g: (g, b, 0)),
        pl.BlockSpec((3, GROUP_WIDTH), lambda b, g: (0, g)),
        pl.BlockSpec((1, GROUP_WIDTH), lambda b, g: (0, g)),
        pl.BlockSpec((3, SSD_STATE), lambda b, g: (0, cw_b0 + g)),
        pl.BlockSpec((1, SSD_STATE), lambda b, g: (0, cw_b0 + g)),
        pl.BlockSpec((3, SSD_STATE), lambda b, g: (0, cw_c0 + g)),
        pl.BlockSpec((1, SSD_STATE), lambda b, g: (0, cw_c0 + g)),
        pl.BlockSpec((None, 1, LANES), lambda b, g: (g, 0, 0)),
        pl.BlockSpec((None, 1, LANES), lambda b, g: (g, 0, 0)),
        pl.BlockSpec((None, 1, GROUP_WIDTH), lambda b, g: (g, 0, 0)),
        state_spec,
        state_spec,
    ]
    scratch = [
        pltpu.VMEM((seq, GROUP_WIDTH), BF16),
        pltpu.VMEM((seq, SSD_STATE), BF16),
        pltpu.VMEM((seq, SSD_STATE), BF16),
        pltpu.VMEM((seq, LANES), F32),
        pltpu.VMEM((seq, LANES), F32),
        pltpu.VMEM((SSD_STATE, GROUP_WIDTH), F32),
        pltpu.VMEM((SSD_STATE, GROUP_WIDTH), F32),
    ]
    if with_output:
        out_specs = pl.BlockSpec((seq, GROUP_WIDTH), lambda b, g: (b, g))
        out_shape = jax.ShapeDtypeStruct((batch * seq, SSD_WIDTH), BF16)
        scratch.append(pltpu.VMEM((seq, GROUP_WIDTH), F32))
    else:
        out_specs = [state_spec, state_spec]
        out_shape = [jax.ShapeDtypeStruct((batch, SSD_GROUPS, SSD_STATE, GROUP_WIDTH), F32)] * 2
    return pl.pallas_call(
        functools.partial(_ssd_kernel, n_chunks=n_chunks, conv_tile=conv_tile, conv_period=conv_period,
                          with_output=with_output),
        grid=(batch, SSD_GROUPS),
        in_specs=in_specs,
        out_specs=out_specs,
        out_shape=out_shape,
        scratch_shapes=scratch,
        compiler_params=_params("parallel", "parallel"),
        name="ssd_out" if with_output else "ssd_ctx",
    )(pm, pm, pm, dtg, conv_w, conv_b, conv_w, conv_b, conv_w, conv_b, dtbias_g, arow_g, dskip_g, h0f, h0b)


def _out_proj_kernel(yd_ref, z_ref, u_ref, v_ref, x_ref, g1_ref, ng_ref, lng_ref, lnb_ref, ws_ref, bst_ref, w_ref,
                     o_ref, mix_scr):
    tm = yd_ref.shape[0]

    @pl.when(pl.program_id(1) == 0)
    def _():
        def body(k, carry):
            r = pl.ds(pl.multiple_of(k * CHUNK, CHUNK), CHUNK)
            a = yd_ref[r, :].astype(F32) * _silu(z_ref[r, :].astype(F32))
            ms = jnp.mean(a * a, axis=-1, keepdims=True)
            mix_scr[r, 0:SSD_WIDTH] = (a * lax.rsqrt(ms + NORM_EPS) * ng_ref[...]).astype(BF16)
            gv = jax.nn.gelu(v_ref[r, :].astype(F32))
            mu = jnp.mean(gv, axis=-1, keepdims=True)
            xc = gv - mu
            var = jnp.mean(xc * xc, axis=-1, keepdims=True)
            ln = ((xc * lax.rsqrt(var + NORM_EPS)) * lng_ref[...] + lnb_ref[...]).astype(BF16)
            gu = jax.nn.gelu(u_ref[r, :].astype(F32))
            for h in range(CM_HEADS):
                c0, c1 = h * CM_HEAD_DIM, (h + 1) * CM_HEAD_DIM
                s = _dot(ws_ref[h], ln[:, c0:c1]) + bst_ref[:, h:h + 1]
                mix_scr[r, SSD_WIDTH + c0:SSD_WIDTH + c1] = (gu[:, c0:c1] * s).astype(BF16)
            return carry

        lax.fori_loop(0, tm // CHUNK, body, 0)

    o_ref[...] = x_ref[...] + g1_ref[...] * _dot(mix_scr[...], w_ref[...])


def _out_proj(yd, pm, x2, mod3, ssd_norm_g, ln_g, ln_b, ws, bst, w_out, *, seq, tm, tn):
    m = x2.shape[0]
    tiles_per_batch = seq // tm
    z_blk, u_blk, v_blk = 0, 2 * SSD_WIDTH // CM_WIDTH, 2 * SSD_WIDTH // CM_WIDTH + 1
    g1_blk0 = 2 * D_MODEL // tn
    row = lambda shape: pl.BlockSpec(shape, lambda i, j: (0, 0))
    return pl.pallas_call(
        _out_proj_kernel,
        grid=(m // tm, D_MODEL // tn),
        in_specs=[
            pl.BlockSpec((tm, SSD_WIDTH), lambda i, j: (i, 0)),
            pl.BlockSpec((tm, SSD_WIDTH), lambda i, j: (i, z_blk)),
            pl.BlockSpec((tm, CM_WIDTH), lambda i, j: (i, u_blk)),
            pl.BlockSpec((tm, CM_WIDTH), lambda i, j: (i, v_blk)),
            pl.BlockSpec((tm, tn), lambda i, j: (i, j)),
            pl.BlockSpec((None, 1, tn), lambda i, j: (i // tiles_per_batch, 0, g1_blk0 + j)),
            row((1, SSD_WIDTH)),
            row((1, CM_WIDTH)),
            row((1, CM_WIDTH)),
            pl.BlockSpec((CM_HEADS, CHUNK, CHUNK), lambda i, j: (0, 0, 0)),
            row((CHUNK, CM_HEADS)),
            pl.BlockSpec((SSD_WIDTH + CM_WIDTH, tn), lambda i, j: (0, j)),
        ],
        out_specs=pl.BlockSpec((tm, tn), lambda i, j: (i, j)),
        out_shape=jax.ShapeDtypeStruct((m, D_MODEL), F32),
        scratch_shapes=[pltpu.VMEM((tm, SSD_WIDTH + CM_WIDTH), BF16)],
        compiler_params=_params("parallel", "arbitrary"),
        name="out_proj",
    )(yd, pm, pm, pm, x2, mod3, ssd_norm_g, ln_g, ln_b, ws, bst, w_out)


def _route(logits):
    lane = lax.broadcasted_iota(jnp.int32, logits.shape, 1)
    lane_f = lane.astype(F32)
    is_group = (lane >= N_EXPERTS) & (lane < N_EXPERTS + N_GROUPS)
    lg = jnp.where(is_group, logits, NEG_BIG)
    mg = jnp.max(lg, axis=1, keepdims=True)
    top_pg = 1.0 / jnp.sum(jnp.exp(lg - mg), axis=1, keepdims=True)
    gi = jnp.min(jnp.where(lg == mg, lane_f, 1e9), axis=1, keepdims=True) - N_EXPERTS
    in_group = (lane < N_EXPERTS) & ((lane // EXPERTS_PER_GROUP).astype(F32) == gi)
    le = jnp.where(in_group, logits, NEG_BIG)
    m1 = jnp.max(le, axis=1, keepdims=True)
    i1 = jnp.min(jnp.where(le == m1, lane_f, 1e9), axis=1, keepdims=True)
    le2 = jnp.where(lane_f == i1, NEG_BIG, le)
    m2 = jnp.max(le2, axis=1, keepdims=True)
    i2 = jnp.min(jnp.where(le2 == m2, lane_f, 1e9), axis=1, keepdims=True)
    e2 = jnp.exp(m2 - m1)
    p1 = 1.0 / (1.0 + e2)
    p2 = e2 * p1
    return (jnp.where(lane_f == i1, p1, 0.0) + jnp.where(lane_f == i2, p2, 0.0)) * top_pg


def _moe_kernel(x1_ref, n2g_ref, sh2_ref, sc2_ref, g2_ref, wr_ref, br_ref, wg_ref, wu_ref, wd_ref, nfg_ref,
                o_ref, h2_scr, comb_scr, acc_scr):
    e = pl.program_id(1)

    @pl.when(e == 0)
    def _():
        x1 = x1_ref[...]
        ms = jnp.mean(x1 * x1, axis=-1, keepdims=True)
        h2 = (x1 * lax.rsqrt(ms + NORM_EPS) * n2g_ref[...]) * (1.0 + sc2_ref[...]) + sh2_ref[...]
        h2_scr[...] = h2.astype(BF16)
        logits = jnp.dot(h2, wr_ref[...], precision=HIGHEST, preferred_element_type=F32) + br_ref[...]
        comb_scr[...] = _route(logits)
        acc_scr[...] = jnp.zeros_like(acc_scr)

    h = h2_scr[...]
    hid = _silu(_dot(h, wg_ref[...])) * _dot(h, wu_ref[...])
    lane = lax.broadcasted_iota(jnp.int32, comb_scr.shape, 1)
    cw = jnp.sum(jnp.where(lane == e, comb_scr[...], 0.0), axis=1, keepdims=True)
    acc_scr[...] += _dot((hid * cw).astype(BF16), wd_ref[...])

    @pl.when(e == pl.num_programs(1) - 1)
    def _():
        y = x1_ref[...] + g2_ref[...] * acc_scr[...]
        ms = jnp.mean(y * y, axis=-1, keepdims=True)
        o_ref[...] = y * lax.rsqrt(ms + NORM_EPS) * nfg_ref[...]


def _moe(x1, mod3, norm2_g, w_router, b_router, wg, wu, wd, normf_g, *, seq, tm):
    m = x1.shape[0]
    tiles_per_batch = seq // tm
    row = lambda shape: pl.BlockSpec(shape, lambda i, e: (0, 0))
    modrow = lambda k: pl.BlockSpec((None, 1, D_MODEL), lambda i, e: (i // tiles_per_batch, 0, k))
    return pl.pallas_call(
        _moe_kernel,
        grid=(m // tm, N_EXPERTS),
        in_specs=[
            pl.BlockSpec((tm, D_MODEL), lambda i, e: (i, 0)),
            row((1, D_MODEL)),
            modrow(3),
            modrow(4),
            modrow(5),
            row((D_MODEL, LANES)),
            row((1, LANES)),
            pl.BlockSpec((None, D_MODEL, EXPERT_FF), lambda i, e: (e, 0, 0)),
            pl.BlockSpec((None, D_MODEL, EXPERT_FF), lambda i, e: (e, 0, 0)),
            pl.BlockSpec((None, EXPERT_FF, D_MODEL), lambda i, e: (e, 0, 0)),
            row((1, D_MODEL)),
        ],
        out_specs=pl.BlockSpec((tm, D_MODEL), lambda i, e: (i, 0)),
        out_shape=jax.ShapeDtypeStruct((m, D_MODEL), F32),
        scratch_shapes=[
            pltpu.VMEM((tm, D_MODEL), BF16),
            pltpu.VMEM((tm, LANES), F32),
            pltpu.VMEM((tm, D_MODEL), F32),
        ],
        compiler_params=_params("parallel", "arbitrary"),
        name="moe",
    )(x1, norm2_g, mod3, mod3, mod3, w_router, b_router, wg, wu, wd, normf_g)


def _group_lanes(fwd, bwd):
    both = jnp.concatenate([fwd.reshape(SSD_GROUPS, HEADS_PER_GROUP), bwd.reshape(SSD_GROUPS, HEADS_PER_GROUP)], axis=1)
    return jnp.pad(both, ((0, 0), (0, LANES - 2 * HEADS_PER_GROUP))).reshape(SSD_GROUPS, 1, LANES)


def _group_dt(dt, rows):
    both = dt[:, :2 * SSD_HEADS].reshape(rows, 2, SSD_GROUPS, HEADS_PER_GROUP)
    both = jnp.transpose(both, (2, 0, 1, 3)).reshape(SSD_GROUPS, rows, 2 * HEADS_PER_GROUP)
    return jnp.pad(both, ((0, 0), (0, 0), (0, LANES - 2 * HEADS_PER_GROUP)))


def kernel(x, c, ctx, c_ctx, w_mod, b_mod, norm1_g, w_in, conv_w, conv_b, dt_bias_f, dt_bias_b, a_log_f, a_log_b,
           d_skip, ssd_norm_g, cm_ln_g, cm_ln_b, w_spatial, b_spatial, w_out, norm2_g, w_router_group,
           b_router_group, w_router_expert, b_router_expert, w_exp_gate, w_exp_up, w_exp_down, normf_g):
    bsz, seq, _ = x.shape
    ctx_len = ctx.shape[1]
    i = 0

    cc = jnp.concatenate([c, c_ctx[None, :], jnp.zeros((MOD_ROWS - bsz - 1, D_MODEL), F32)], axis=0)
    mod = _modulation(cc, w_mod[i], b_mod[i][None, :])
    mod3 = mod.reshape(MOD_ROWS, 1, N_MOD * D_MODEL)

    bc0 = 2 * SSD_WIDTH
    dt0 = bc0 + 2 * BC_WIDTH
    dt1 = dt0 + 2 * SSD_HEADS
    wi = w_in[i]
    w_main = jnp.concatenate([wi[:, :bc0], wi[:, dt1:], wi[:, bc0:dt0]], axis=1).astype(BF16)
    w_dt = jnp.pad(wi[:, dt0:dt1], ((0, 0), (0, LANES - 2 * SSD_HEADS))).astype(BF16)
    g1row = norm1_g[i][None, :]

    x2 = x.reshape(bsz * seq, D_MODEL)
    ctx2 = ctx.reshape(bsz * ctx_len, D_MODEL)
    tm_in = 1024
    pm_x, dt_x = _in_proj(x2, mod3, lambda t: t // (seq // tm_in), g1row, w_main, w_dt, tm_in, 1024)
    pm_c, dt_c = _in_proj(ctx2, mod3, lambda t: bsz, g1row, w_main, w_dt, bsz * ctx_len, 1024)

    dtbias_g = _group_lanes(dt_bias_f[i], dt_bias_b[i])
    arow_g = _group_lanes(-jnp.exp(a_log_f[i].astype(F32)), -jnp.exp(a_log_b[i].astype(F32)))
    dskip_g = jnp.repeat(d_skip[i], SSD_HEAD_DIM).reshape(SSD_GROUPS, 1, GROUP_WIDTH)
    cw = conv_w[i]
    cb = conv_b[i][None, :]
    h_zero = jnp.zeros((bsz, SSD_GROUPS, SSD_STATE, GROUP_WIDTH), F32)

    hc_f, hc_b = _ssd(pm_c, _group_dt(dt_c, bsz * ctx_len), cw, cb, dtbias_g, arow_g, dskip_g, h_zero, h_zero,
                      batch=bsz, seq=ctx_len, conv_tile=ctx_len, conv_period=ctx_len, with_output=False)
    yd = _ssd(pm_x, _group_dt(dt_x, bsz * seq), cw, cb, dtbias_g, arow_g, dskip_g, hc_f, hc_b,
              batch=bsz, seq=seq, conv_tile=CHUNK, conv_period=GRID_W, with_output=True)

    x1 = _out_proj(yd, pm_x, x2, mod3, ssd_norm_g[i][None, :], cm_ln_g[i][None, :], cm_ln_b[i][None, :],
                   w_spatial[i].astype(BF16), b_spatial[i].T, w_out[i].astype(BF16), seq=seq, tm=512, tn=1024)

    w_re = jnp.transpose(w_router_expert[i], (1, 0, 2)).reshape(D_MODEL, N_EXPERTS)
    pad = LANES - N_EXPERTS - N_GROUPS
    w_router = jnp.pad(jnp.concatenate([w_re, w_router_group[i]], axis=1), ((0, 0), (0, pad)))
    b_router = jnp.pad(jnp.concatenate([b_router_expert[i].reshape(-1), b_router_group[i]]), (0, pad))[None, :]
    wg = w_exp_gate[i].reshape(N_EXPERTS, D_MODEL, EXPERT_FF).astype(BF16)
    wu = w_exp_up[i].reshape(N_EXPERTS, D_MODEL, EXPERT_FF).astype(BF16)
    wd = w_exp_down[i].reshape(N_EXPERTS, EXPERT_FF, D_MODEL).astype(BF16)
    out = _moe(x1, mod3, norm2_g[i][None, :], w_router, b_router, wg, wu, wd, normf_g[None, :], seq=seq, tm=512)
    return out.reshape(bsz, seq, D_MODEL)
```

```python
import functools

import jax
import jax.numpy as jnp
from jax import lax
from jax.experimental import pallas as pl
from jax.experimental.pallas import tpu as pltpu

F32 = jnp.float32
BF16 = jnp.bfloat16
HIGHEST = lax.Precision.HIGHEST

D_MODEL = 2048
GRID_W = 64
SSD_WIDTH = 2048
CM_WIDTH = 2048
SSD_HEADS = 32
SSD_HEAD_DIM = 64
SSD_GROUPS = 4
HEADS_PER_GROUP = SSD_HEADS // SSD_GROUPS
GROUP_WIDTH = HEADS_PER_GROUP * SSD_HEAD_DIM
SSD_STATE = 128
CHUNK = 128
BC_WIDTH = SSD_GROUPS * SSD_STATE
CM_HEADS = 8
CM_HEAD_DIM = CM_WIDTH // CM_HEADS
N_GROUPS = 4
N_EXPERTS = 32
EXPERTS_PER_GROUP = 8
EXPERT_FF = 512
N_MOD = 6
NORM_EPS = 1e-6
LANES = 128
NEG_BIG = -1e30
MOD_ROWS = 8
VMEM_LIMIT = 56 * 1024 * 1024


def _params(*sem):
    return pltpu.CompilerParams(dimension_semantics=sem, vmem_limit_bytes=VMEM_LIMIT)


def _silu(v):
    return v * jax.nn.sigmoid(v)


def _dot(a, b):
    return jnp.dot(a, b, preferred_element_type=F32)


def _mod_kernel(cc_ref, w_ref, b_ref, o_ref):
    a = _silu(cc_ref[...])
    hi = a.astype(BF16).astype(F32)
    lhs = jnp.concatenate([hi, a - hi], axis=0).astype(BF16)
    r = _dot(lhs, w_ref[...].astype(BF16))
    o_ref[...] = r[:MOD_ROWS] + r[MOD_ROWS:] + b_ref[...]


def _modulation(cc, w_mod, b_mod):
    n = w_mod.shape[1]
    tn = 1024
    return pl.pallas_call(
        _mod_kernel,
        grid=(n // tn,),
        in_specs=[
            pl.BlockSpec((MOD_ROWS, D_MODEL), lambda j: (0, 0)),
            pl.BlockSpec((D_MODEL, tn), lambda j: (0, j)),
            pl.BlockSpec((1, tn), lambda j: (0, j)),
        ],
        out_specs=pl.BlockSpec((MOD_ROWS, tn), lambda j: (0, j)),
        out_shape=jax.ShapeDtypeStruct((MOD_ROWS, n), F32),
        compiler_params=_params("arbitrary"),
        name="modulation",
    )(cc, w_mod, b_mod)


def _in_proj_kernel(x_ref, g_ref, sh_ref, sc_ref, w_ref, wdt_ref, o_ref, dt_ref, h_scr):
    tm = x_ref.shape[0]
    rc = 256

    @pl.when(pl.program_id(1) == 0)
    def _():
        gain = g_ref[...]
        scale = 1.0 + sc_ref[...]
        shift = sh_ref[...]

        def body(k, carry):
            r = pl.ds(pl.multiple_of(k * rc, rc), rc)
            xv = x_ref[r, :]
            ms = jnp.mean(xv * xv, axis=-1, keepdims=True)
            h = (xv * lax.rsqrt(ms + NORM_EPS) * gain) * scale + shift
            h_scr[r, :] = h.astype(BF16)
            return carry

        lax.fori_loop(0, tm // rc, body, 0)
        dt_ref[...] = _dot(h_scr[...], wdt_ref[...])

    o_ref[...] = _dot(h_scr[...], w_ref[...]).astype(o_ref.dtype)


def _in_proj(x2, mod3, mod_row_of_tile, norm_g, w_main, w_dt, tm, tn):
    m = x2.shape[0]
    n = w_main.shape[1]
    return pl.pallas_call(
        _in_proj_kernel,
        grid=(m // tm, n // tn),
        in_specs=[
            pl.BlockSpec((tm, D_MODEL), lambda i, j: (i, 0)),
            pl.BlockSpec((1, D_MODEL), lambda i, j: (0, 0)),
            pl.BlockSpec((None, 1, D_MODEL), lambda i, j: (mod_row_of_tile(i), 0, 0)),
            pl.BlockSpec((None, 1, D_MODEL), lambda i, j: (mod_row_of_tile(i), 0, 1)),
            pl.BlockSpec((D_MODEL, tn), lambda i, j: (0, j)),
            pl.BlockSpec((D_MODEL, LANES), lambda i, j: (0, 0)),
        ],
        out_specs=[
            pl.BlockSpec((tm, tn), lambda i, j: (i, j)),
            pl.BlockSpec((tm, LANES), lambda i, j: (i, 0)),
        ],
        out_shape=[
            jax.ShapeDtypeStruct((m, n), BF16),
            jax.ShapeDtypeStruct((m, LANES), F32),
        ],
        scratch_shapes=[pltpu.VMEM((tm, D_MODEL), BF16)],
        compiler_params=_params("parallel", "arbitrary"),
        name="in_proj",
    )(x2, norm_g, mod3, mod3, w_main, w_dt)


def _split3(v):
    hi = v.astype(BF16)
    rem = v - hi.astype(F32)
    mid = rem.astype(BF16)
    lo = (rem - mid.astype(F32)).astype(BF16)
    return hi, mid, lo


def _expand_heads(v, sel):
    hi, mid, lo = _split3(v)
    return _dot(hi, sel) + _dot(mid, sel) + _dot(lo, sel)


def _ssd_kernel(xs_ref, b_ref, c_ref, dt_ref, cwx_ref, cbx_ref, cwb_ref, cbb_ref, cwc_ref, cbc_ref,
                dtbias_ref, arow_ref, dskip_ref, h0f_ref, h0b_ref, *rest,
                n_chunks, conv_tile, conv_period, with_output):
    if with_output:
        y_ref, xs_s, b_s, c_s, cum_s, dts_s, hf_s, hb_s, y_s = rest
    else:
        hf_out, hb_out, xs_s, b_s, c_s, cum_s, dts_s, hf_s, hb_s = rest
    seq = n_chunks * CHUNK
    hpg = HEADS_PER_GROUP

    def conv_silu(src_ref, dst_ref, w_ref, bias_ref):
        cols = src_ref.shape[1]
        w0, w1, w2 = w_ref[0:1, :], w_ref[1:2, :], w_ref[2:3, :]
        bias = bias_ref[...]
        row = lax.broadcasted_iota(jnp.int32, (conv_tile, cols), 0) % conv_period
        first = row == 0
        last = row == conv_period - 1

        def body(k, carry):
            r = pl.ds(pl.multiple_of(k * conv_tile, conv_tile), conv_tile)
            v = src_ref[r, :].astype(F32)
            prev = jnp.where(first, 0.0, pltpu.roll(v, 1, 0))
            nxt = jnp.where(last, 0.0, pltpu.roll(v, conv_tile - 1, 0))
            y = bias + prev * w0 + v * w1 + nxt * w2
            dst_ref[r, :] = _silu(y).astype(BF16)
            return carry

        lax.fori_loop(0, seq // conv_tile, body, 0)

    conv_silu(xs_ref, xs_s, cwx_ref, cbx_ref)
    conv_silu(b_ref, b_s, cwb_ref, cbb_ref)
    conv_silu(c_ref, c_s, cwc_ref, cbc_ref)

    ii = lax.broadcasted_iota(jnp.int32, (CHUNK, CHUNK), 0)
    jj = lax.broadcasted_iota(jnp.int32, (CHUNK, CHUNK), 1)
    lower = ii >= jj
    upper = jj >= ii
    tri_lo = lower.astype(F32)
    tri_up = upper.astype(F32)
    arow = arow_ref[...]
    dtbias = dtbias_ref[...]

    def dt_body(k, carry):
        r = pl.ds(pl.multiple_of(k * CHUNK, CHUNK), CHUNK)
        raw = dt_ref[r, :] + dtbias
        dts = jnp.maximum(raw, 0.0) + jnp.log1p(jnp.exp(-jnp.abs(raw)))
        adt = dts * arow
        cf = jnp.dot(tri_lo, adt, precision=HIGHEST, preferred_element_type=F32)
        cr = jnp.dot(tri_up, adt, precision=HIGHEST, preferred_element_type=F32)
        dts_s[r, :] = dts
        cum_s[r, :] = jnp.where(jj < hpg, cf, cr)
        return carry

    lax.fori_loop(0, n_chunks, dt_body, 0)

    hf_s[...] = h0f_ref[...]
    hb_s[...] = h0b_ref[...]

    head_of_col = lax.broadcasted_iota(jnp.int32, (LANES, GROUP_WIDTH), 1) // SSD_HEAD_DIM
    sel_row = lax.broadcasted_iota(jnp.int32, (LANES, GROUP_WIDTH), 0)
    sel_f = (sel_row == head_of_col).astype(BF16)
    sel_b = (sel_row == head_of_col + hpg).astype(BF16)
    low_half = jj < SSD_HEAD_DIM

    def direction_terms(r, sel):
        cum = cum_s[r, :]
        dts = dts_s[r, :]
        tot = jnp.sum(dts * arow, axis=0, keepdims=True)
        wst = dts * jnp.exp(tot - cum)
        eoff = jnp.exp(cum)
        edec = jnp.broadcast_to(jnp.exp(tot), (8, LANES))
        ex = _expand_heads(jnp.concatenate([wst, eoff, edec], axis=0), sel)
        return cum, dts, ex[:CHUNK], ex[CHUNK:2 * CHUNK], ex[2 * CHUNK:2 * CHUNK + 1]

    def state_step(h_s, r, sel):
        cum, dts, w_state, e_off, e_dec = direction_terms(r, sel)
        xc = xs_s[r, :]
        xw = (xc.astype(F32) * w_state).astype(BF16)
        s_new = lax.dot_general(b_s[r, :], xw, (((0,), (0,)), ((), ())), preferred_element_type=F32)
        h = h_s[...]
        y_off = _dot(c_s[r, :], h.astype(BF16)) * e_off
        h_s[...] = h * e_dec + s_new
        return cum, dts, xc, y_off

    def fwd_body(k, carry):
        r = pl.ds(pl.multiple_of(k * CHUNK, CHUNK), CHUNK)
        cum, dts, xc, y_off = state_step(hf_s, r, sel_f)
        if with_output:
            g = lax.dot_general(c_s[r, :], b_s[r, :], (((1,), (1,)), ((), ())), preferred_element_type=F32)
            cum_t = cum.T
            dts_t = dts.T
            pieces = []
            for pair in range(hpg // 2):
                ms = []
                for h in (2 * pair, 2 * pair + 1):
                    hb = h + hpg
                    mf = jnp.exp(jnp.where(lower, cum[:, h:h + 1] - cum_t[h:h + 1, :], NEG_BIG)) * dts_t[h:h + 1, :]
                    mb = jnp.exp(jnp.where(upper, cum[:, hb:hb + 1] - cum_t[hb:hb + 1, :], NEG_BIG)) * dts_t[hb:hb + 1, :]
                    ms.append((g * (mf + mb)).astype(BF16))
                lhs = jnp.concatenate(ms, axis=1)
                xp = xc[:, pair * LANES:(pair + 1) * LANES]
                zero = jnp.zeros_like(xp)
                rhs = jnp.concatenate([jnp.where(low_half, xp, zero), jnp.where(low_half, zero, xp)], axis=0)
                pieces.append(_dot(lhs, rhs))
            y_diag = jnp.concatenate(pieces, axis=1)
            y_s[r, :] = y_diag + y_off + xc.astype(F32) * dskip_ref[...]
        return carry

    lax.fori_loop(0, n_chunks, fwd_body, 0)

    def bwd_body(k, carry):
        r = pl.ds(pl.multiple_of((n_chunks - 1 - k) * CHUNK, CHUNK), CHUNK)
        _, _, _, y_off = state_step(hb_s, r, sel_b)
        if with_output:
            y_ref[r, :] = (y_s[r, :] + y_off).astype(y_ref.dtype)
        return carry

    lax.fori_loop(0, n_chunks, bwd_body, 0)

    if not with_output:
        hf_out[...] = hf_s[...]
        hb_out[...] = hb_s[...]


def _ssd(pm, dtg, conv_w, conv_b, dtbias_g, arow_g, dskip_g, h0f, h0b, *, batch, seq, conv_tile, conv_period,
         with_output):
    n_chunks = seq // CHUNK
    xs_blk0 = SSD_WIDTH // GROUP_WIDTH
    b_blk0 = (2 * SSD_WIDTH + 2 * CM_WIDTH) // SSD_STATE
    c_blk0 = b_blk0 + SSD_GROUPS
    cw_b0 = SSD_WIDTH // SSD_STATE
    cw_c0 = cw_b0 + SSD_GROUPS
    state_spec = pl.BlockSpec((None, None, SSD_STATE, GROUP_WIDTH), lambda b, g: (b, g, 0, 0))
    in_specs = [
        pl.BlockSpec((seq, GROUP_WIDTH), lambda b, g: (b, xs_blk0 + g)),
        pl.BlockSpec((seq, SSD_STATE), lambda b, g: (b, b_blk0 + g)),
        pl.BlockSpec((seq, SSD_STATE), lambda b, g: (b, c_blk0 + g)),
        pl.BlockSpec((None, seq, LANES), lambda b, g: (g, b, 0)),
        pl.BlockSpec((3, GROUP_WIDTH), lambda b, g: (0, g)),
        pl.BlockSpec((1, GROUP_WIDTH), lambda b, g: (0, g)),
        pl.BlockSpec((3, SSD_STATE), lambda b, g: (0, cw_b0 + g)),
        pl.BlockSpec((1, SSD_STATE), lambda b, g: (0, cw_b0 + g)),
        pl.BlockSpec((3, SSD_STATE), lambda b, g: (0, cw_c0 + g)),
        pl.BlockSpec((1, SSD_STATE), lambda b, g: (0, cw_c0 + g)),
        pl.BlockSpec((None, 1, LANES), lambda b, g: (g, 0, 0)),
        pl.BlockSpec((None, 1, LANES), lambda b, g: (g, 0, 0)),
        pl.BlockSpec((None, 1, GROUP_WIDTH), lambda b, g: (g, 0, 0)),
        state_spec,
        state_spec,
    ]
    scratch = [
        pltpu.VMEM((seq, GROUP_WIDTH), BF16),
        pltpu.VMEM((seq, SSD_STATE), BF16),
        pltpu.VMEM((seq, SSD_STATE), BF16),
        pltpu.VMEM((seq, LANES), F32),
        pltpu.VMEM((seq, LANES), F32),
        pltpu.VMEM((SSD_STATE, GROUP_WIDTH), F32),
        pltpu.VMEM((SSD_STATE, GROUP_WIDTH), F32),
    ]
    if with_output:
        out_specs = pl.BlockSpec((seq, GROUP_WIDTH), lambda b, g: (b, g))
        out_shape = jax.ShapeDtypeStruct((batch * seq, SSD_WIDTH), BF16)
        scratch.append(pltpu.VMEM((seq, GROUP_WIDTH), F32))
    else:
        out_specs = [state_spec, state_spec]
        out_shape = [jax.ShapeDtypeStruct((batch, SSD_GROUPS, SSD_STATE, GROUP_WIDTH), F32)] * 2
    return pl.pallas_call(
        functools.partial(_ssd_kernel, n_chunks=n_chunks, conv_tile=conv_tile, conv_period=conv_period,
                          with_output=with_output),
        grid=(batch, SSD_GROUPS),
        in_specs=in_specs,
        out_specs=out_specs,
        out_shape=out_shape,
        scratch_shapes=scratch,
        compiler_params=_params("parallel", "parallel"),
        name="ssd_out" if with_output else "ssd_ctx",
    )(pm, pm, pm, dtg, conv_w, conv_b, conv_w, conv_b, conv_w, conv_b, dtbias_g, arow_g, dskip_g, h0f, h0b)


def _out_proj_kernel(yd_ref, z_ref, u_ref, v_ref, x_ref, g1_ref, ng_ref, lng_ref, lnb_ref, ws_ref, bst_ref, w_ref,
                     o_ref, mix_scr):
    tm = yd_ref.shape[0]

    @pl.when(pl.program_id(1) == 0)
    def _():
        def body(k, carry):
            r = pl.ds(pl.multiple_of(k * CHUNK, CHUNK), CHUNK)
            a = yd_ref[r, :].astype(F32) * _silu(z_ref[r, :].astype(F32))
            ms = jnp.mean(a * a, axis=-1, keepdims=True)
            mix_scr[r, 0:SSD_WIDTH] = (a * lax.rsqrt(ms + NORM_EPS) * ng_ref[...]).astype(BF16)
            gv = jax.nn.gelu(v_ref[r, :].astype(F32))
            mu = jnp.mean(gv, axis=-1, keepdims=True)
            xc = gv - mu
            var = jnp.mean(xc * xc, axis=-1, keepdims=True)
            ln = ((xc * lax.rsqrt(var + NORM_EPS)) * lng_ref[...] + lnb_ref[...]).astype(BF16)
            gu = jax.nn.gelu(u_ref[r, :].astype(F32))
            for h in range(CM_HEADS):
                c0, c1 = h * CM_HEAD_DIM, (h + 1) * CM_HEAD_DIM
                s = _dot(ws_ref[h], ln[:, c0:c1]) + bst_ref[:, h:h + 1]
                mix_scr[r, SSD_WIDTH + c0:SSD_WIDTH + c1] = (gu[:, c0:c1] * s).astype(BF16)
            return carry

        lax.fori_loop(0, tm // CHUNK, body, 0)

    o_ref[...] = x_ref[...] + g1_ref[...] * _dot(mix_scr[...], w_ref[...])


def _out_proj(yd, pm, x2, mod3, ssd_norm_g, ln_g, ln_b, ws, bst, w_out, *, seq, tm, tn):
    m = x2.shape[0]
    tiles_per_batch = seq // tm
    z_blk, u_blk, v_blk = 0, 2 * SSD_WIDTH // CM_WIDTH, 2 * SSD_WIDTH // CM_WIDTH + 1
    g1_blk0 = 2 * D_MODEL // tn
    row = lambda shape: pl.BlockSpec(shape, lambda i, j: (0, 0))
    return pl.pallas_call(
        _out_proj_kernel,
        grid=(m // tm, D_MODEL // tn),
        in_specs=[
            pl.BlockSpec((tm, SSD_WIDTH), lambda i, j: (i, 0)),
            pl.BlockSpec((tm, SSD_WIDTH), lambda i, j: (i, z_blk)),
            pl.BlockSpec((tm, CM_WIDTH), lambda i, j: (i, u_blk)),
            pl.BlockSpec((tm, CM_WIDTH), lambda i, j: (i, v_blk)),
            pl.BlockSpec((tm, tn), lambda i, j: (i, j)),
            pl.BlockSpec((None, 1, tn), lambda i, j: (i // tiles_per_batch, 0, g1_blk0 + j)),
            row((1, SSD_WIDTH)),
            row((1, CM_WIDTH)),
            row((1, CM_WIDTH)),
            pl.BlockSpec((CM_HEADS, CHUNK, CHUNK), lambda i, j: (0, 0, 0)),
            row((CHUNK, CM_HEADS)),
            pl.BlockSpec((SSD_WIDTH + CM_WIDTH, tn), lambda i, j: (0, j)),
        ],
        out_specs=pl.BlockSpec((tm, tn), lambda i, j: (i, j)),
        out_shape=jax.ShapeDtypeStruct((m, D_MODEL), F32),
        scratch_shapes=[pltpu.VMEM((tm, SSD_WIDTH + CM_WIDTH), BF16)],
        compiler_params=_params("parallel", "arbitrary"),
        name="out_proj",
    )(yd, pm, pm, pm, x2, mod3, ssd_norm_g, ln_g, ln_b, ws, bst, w_out)


def _route(logits):
    lane = lax.broadcasted_iota(jnp.int32, logits.shape, 1)
    lane_f = lane.astype(F32)
    is_group = (lane >= N_EXPERTS) & (lane < N_EXPERTS + N_GROUPS)
    lg = jnp.where(is_group, logits, NEG_BIG)
    mg = jnp.max(lg, axis=1, keepdims=True)
    top_pg = 1.0 / jnp.sum(jnp.exp(lg - mg), axis=1, keepdims=True)
    gi = jnp.min(jnp.where(lg == mg, lane_f, 1e9), axis=1, keepdims=True) - N_EXPERTS
    in_group = (lane < N_EXPERTS) & ((lane // EXPERTS_PER_GROUP).astype(F32) == gi)
    le = jnp.where(in_group, logits, NEG_BIG)
    m1 = jnp.max(le, axis=1, keepdims=True)
    i1 = jnp.min(jnp.where(le == m1, lane_f, 1e9), axis=1, keepdims=True)
    le2 = jnp.where(lane_f == i1, NEG_BIG, le)
    m2 = jnp.max(le2, axis=1, keepdims=True)
    i2 = jnp.min(jnp.where(le2 == m2, lane_f, 1e9), axis=1, keepdims=True)
    e2 = jnp.exp(m2 - m1)
    p1 = 1.0 / (1.0 + e2)
    p2 = e2 * p1
    return i1, i2, p1 * top_pg, p2 * top_pg


RINFO_E1, RINFO_E2, RINFO_W1, RINFO_W2, RINFO_R1, RINFO_R2 = range(6)


def _route_kernel(x1_ref, n2g_ref, sh2_ref, sc2_ref, wr_ref, br_ref, h2_ref, rinfo_ref, counts_ref, cnt_scr):
    tm = x1_ref.shape[0]

    @pl.when(pl.program_id(0) == 0)
    def _():
        cnt_scr[...] = jnp.zeros_like(cnt_scr)

    x1 = x1_ref[...]
    ms = jnp.mean(x1 * x1, axis=-1, keepdims=True)
    h2 = (x1 * lax.rsqrt(ms + NORM_EPS) * n2g_ref[...]) * (1.0 + sc2_ref[...]) + sh2_ref[...]
    h2_ref[...] = h2
    logits = jnp.dot(h2, wr_ref[...], precision=HIGHEST, preferred_element_type=F32) + br_ref[...]
    i1, i2, w1, w2 = _route(logits)

    lane = lax.broadcasted_iota(jnp.int32, (tm, LANES), 1)
    lane_f = lane.astype(F32)
    oh1 = jnp.where(lane_f == i1, 1.0, 0.0)
    oh2 = jnp.where(lane_f == i2, 1.0, 0.0)
    before = (lax.broadcasted_iota(jnp.int32, (tm, tm), 0) > lax.broadcasted_iota(jnp.int32, (tm, tm), 1)).astype(BF16)
    carried = cnt_scr[...]
    tot1 = jnp.sum(oh1, axis=0, keepdims=True)
    r1 = jnp.sum(oh1 * (_dot(before, oh1.astype(BF16)) + carried), axis=1, keepdims=True)
    r2 = jnp.sum(oh2 * (_dot(before, oh2.astype(BF16)) + (carried + tot1)), axis=1, keepdims=True)
    counts = carried + tot1 + jnp.sum(oh2, axis=0, keepdims=True)
    cnt_scr[...] = counts
    counts_ref[...] = jnp.broadcast_to(counts, counts_ref.shape)

    info = jnp.zeros((tm, LANES), F32)
    for k, v in ((RINFO_E1, i1), (RINFO_E2, i2), (RINFO_W1, w1), (RINFO_W2, w2), (RINFO_R1, r1), (RINFO_R2, r2)):
        info = jnp.where(lane == k, v, info)
    rinfo_ref[...] = info


def _route_call(x1, mod3, norm2_g, w_router, b_router, *, seq, tm):
    m = x1.shape[0]
    tiles_per_batch = seq // tm
    row = lambda shape: pl.BlockSpec(shape, lambda i: (0, 0))
    modrow = lambda k: pl.BlockSpec((None, 1, D_MODEL), lambda i: (i // tiles_per_batch, 0, k))
    return pl.pallas_call(
        _route_kernel,
        grid=(m // tm,),
        in_specs=[
            pl.BlockSpec((tm, D_MODEL), lambda i: (i, 0)),
            row((1, D_MODEL)),
            modrow(3),
            modrow(4),
            row((D_MODEL, LANES)),
            row((1, LANES)),
        ],
        out_specs=[
            pl.BlockSpec((tm, D_MODEL), lambda i: (i, 0)),
            pl.BlockSpec((tm, LANES), lambda i: (i, 0)),
            row((8, LANES)),
        ],
        out_shape=[
            jax.ShapeDtypeStruct((m, D_MODEL), F32),
            jax.ShapeDtypeStruct((m, LANES), F32),
            jax.ShapeDtypeStruct((8, LANES), F32),
        ],
        scratch_shapes=[pltpu.VMEM((1, LANES), F32)],
        compiler_params=_params("arbitrary"),
        name="route",
    )(x1, norm2_g, mod3, mod3, w_router, b_router)


def _experts_kernel(te_ref, nu_ref, idx_cur, idx_nxt, h2_hbm, wg_ref, wu_ref, wd_ref, y_ref,
                    xbuf, wg_b, wu_b, wd_b, sem):
    j = pl.program_id(0)
    n_used = nu_ref[0]
    tm = xbuf.shape[1]
    slot = j % 2

    def start_gather(idx_ref, s):
        def body(r, carry):
            tok = idx_ref[0, r]
            pltpu.make_async_copy(h2_hbm.at[pl.ds(tok, 1), :], xbuf.at[s, pl.ds(r, 1), :], sem.at[s]).start()
            return carry

        lax.fori_loop(0, tm, body, 0)

    @pl.when(j == 0)
    def _():
        start_gather(idx_cur, 0)

    @pl.when(j + 1 < n_used)
    def _():
        start_gather(idx_nxt, 1 - slot)

    @pl.when((j < n_used) & ((j == 0) | (te_ref[j] != te_ref[jnp.maximum(j - 1, 0)])))
    def _():
        wg_b[...] = wg_ref[...].astype(BF16)
        wu_b[...] = wu_ref[...].astype(BF16)
        wd_b[...] = wd_ref[...].astype(BF16)

    @pl.when(j < n_used)
    def _():
        pltpu.make_async_copy(h2_hbm.at[pl.ds(0, tm), :], xbuf.at[slot], sem.at[slot]).wait()
        xt = xbuf[slot].astype(BF16)
        hid = (_silu(_dot(xt, wg_b[...])) * _dot(xt, wu_b[...])).astype(BF16)
        y_ref[...] = _dot(hid, wd_b[...])

    @pl.when(j >= n_used)
    def _():
        y_ref[...] = jnp.zeros_like(y_ref)


def _experts_call(tile_expert, n_used, src_tok, h2, wg, wu, wd, *, tm):
    n_tiles = tile_expert.shape[0]
    wspec = lambda shape: pl.BlockSpec((None,) + shape, lambda j, te, nu: (te[j], 0, 0))
    grid_spec = pltpu.PrefetchScalarGridSpec(
        num_scalar_prefetch=2,
        grid=(n_tiles,),
        in_specs=[
            pl.BlockSpec((None, 1, tm), lambda j, te, nu: (j, 0, 0), memory_space=pltpu.SMEM),
            pl.BlockSpec((None, 1, tm), lambda j, te, nu: (jnp.minimum(j + 1, n_tiles - 1), 0, 0),
                         memory_space=pltpu.SMEM),
            pl.BlockSpec(memory_space=pl.ANY),
            wspec((D_MODEL, EXPERT_FF)),
            wspec((D_MODEL, EXPERT_FF)),
            wspec((EXPERT_FF, D_MODEL)),
        ],
        out_specs=pl.BlockSpec((tm, D_MODEL), lambda j, te, nu: (j, 0)),
        scratch_shapes=[
            pltpu.VMEM((2, tm, D_MODEL), F32),
            pltpu.VMEM((D_MODEL, EXPERT_FF), BF16),
            pltpu.VMEM((D_MODEL, EXPERT_FF), BF16),
            pltpu.VMEM((EXPERT_FF, D_MODEL), BF16),
            pltpu.SemaphoreType.DMA((2,)),
        ],
    )
    return pl.pallas_call(
        _experts_kernel,
        grid_spec=grid_spec,
        out_shape=jax.ShapeDtypeStruct((n_tiles * tm, D_MODEL), F32),
        compiler_params=_params("arbitrary"),
        name="experts",
    )(tile_expert, n_used, src_tok, src_tok, h2, wg, wu, wd)


def _combine_kernel(pos_cur, pos_nxt, x1_ref, g2_ref, nfg_ref, rinfo_ref, y_hbm, o_ref, ybuf, sem):
    i = pl.program_id(0)
    tm = x1_ref.shape[0]
    slot = i % 2

    def start_gather(pos_ref, s):
        def body(r, carry):
            for k in range(2):
                p = pos_ref[0, 2 * r + k]
                pltpu.make_async_copy(y_hbm.at[pl.ds(p, 1), :], ybuf.at[s, k, pl.ds(r, 1), :], sem.at[s]).start()
            return carry

        lax.fori_loop(0, tm, body, 0)

    @pl.when(i == 0)
    def _():
        start_gather(pos_cur, 0)

    @pl.when(i + 1 < pl.num_programs(0))
    def _():
        start_gather(pos_nxt, 1 - slot)

    for k in range(2):
        pltpu.make_async_copy(y_hbm.at[pl.ds(0, tm), :], ybuf.at[slot, k], sem.at[slot]).wait()
    info = rinfo_ref[...]
    w1 = info[:, RINFO_W1:RINFO_W1 + 1]
    w2 = info[:, RINFO_W2:RINFO_W2 + 1]
    y = x1_ref[...] + g2_ref[...] * (w1 * ybuf[slot, 0] + w2 * ybuf[slot, 1])
    ms = jnp.mean(y * y, axis=-1, keepdims=True)
    o_ref[...] = y * lax.rsqrt(ms + NORM_EPS) * nfg_ref[...]


def _combine_call(pos, x1, mod3, normf_g, rinfo, y_sorted, *, seq, tm):
    m = x1.shape[0]
    n_tiles = m // tm
    tiles_per_batch = seq // tm
    return pl.pallas_call(
        _combine_kernel,
        grid=(n_tiles,),
        in_specs=[
            pl.BlockSpec((None, 1, 2 * tm), lambda i: (i, 0, 0), memory_space=pltpu.SMEM),
            pl.BlockSpec((None, 1, 2 * tm), lambda i: (jnp.minimum(i + 1, n_tiles - 1), 0, 0),
                         memory_space=pltpu.SMEM),
            pl.BlockSpec((tm, D_MODEL), lambda i: (i, 0)),
            pl.BlockSpec((None, 1, D_MODEL), lambda i: (i // tiles_per_batch, 0, 5)),
            pl.BlockSpec((1, D_MODEL), lambda i: (0, 0)),
            pl.BlockSpec((tm, LANES), lambda i: (i, 0)),
            pl.BlockSpec(memory_space=pl.ANY),
        ],
        out_specs=pl.BlockSpec((tm, D_MODEL), lambda i: (i, 0)),
        out_shape=jax.ShapeDtypeStruct((m, D_MODEL), F32),
        scratch_shapes=[
            pltpu.VMEM((2, 2, tm, D_MODEL), F32),
            pltpu.SemaphoreType.DMA((2,)),
        ],
        compiler_params=_params("arbitrary"),
        name="combine",
    )(pos, pos, x1, mod3, normf_g, rinfo, y_sorted)


def _dispatch_plan(rinfo, counts, *, tm):
    n_tok = rinfo.shape[0]
    n_tiles = (2 * n_tok + N_EXPERTS * (tm - 1)) // tm + 1
    expert = rinfo[:, RINFO_E1:RINFO_E2 + 1].astype(jnp.int32)
    rank = rinfo[:, RINFO_R1:RINFO_R2 + 1].astype(jnp.int32)
    cnt = counts[0, :N_EXPERTS].astype(jnp.int32)
    tiles_e = (cnt + tm - 1) // tm
    end_tile = jnp.cumsum(tiles_e)
    start_row = (end_tile - tiles_e) * tm
    pos = start_row[expert] + rank
    n_used = end_tile[-1:]
    tile_ids = jnp.minimum(jnp.arange(n_tiles, dtype=jnp.int32), n_used - 1)
    tile_expert = jnp.sum(tile_ids[:, None] >= end_tile[None, :], axis=1).astype(jnp.int32)
    tok = jnp.repeat(jnp.arange(n_tok, dtype=jnp.int32), 2)
    src_tok = jnp.zeros((n_tiles * tm,), jnp.int32).at[pos.reshape(-1)].set(tok)
    return pos, tile_expert, n_used.astype(jnp.int32), src_tok.reshape(n_tiles, 1, tm)


def _group_lanes(fwd, bwd):
    both = jnp.concatenate([fwd.reshape(SSD_GROUPS, HEADS_PER_GROUP), bwd.reshape(SSD_GROUPS, HEADS_PER_GROUP)], axis=1)
    return jnp.pad(both, ((0, 0), (0, LANES - 2 * HEADS_PER_GROUP))).reshape(SSD_GROUPS, 1, LANES)


def _group_dt(dt, rows):
    both = dt[:, :2 * SSD_HEADS].reshape(rows, 2, SSD_GROUPS, HEADS_PER_GROUP)
    both = jnp.transpose(both, (2, 0, 1, 3)).reshape(SSD_GROUPS, rows, 2 * HEADS_PER_GROUP)
    return jnp.pad(both, ((0, 0), (0, 0), (0, LANES - 2 * HEADS_PER_GROUP)))


def kernel(x, c, ctx, c_ctx, w_mod, b_mod, norm1_g, w_in, conv_w, conv_b, dt_bias_f, dt_bias_b, a_log_f, a_log_b,
           d_skip, ssd_norm_g, cm_ln_g, cm_ln_b, w_spatial, b_spatial, w_out, norm2_g, w_router_group,
           b_router_group, w_router_expert, b_router_expert, w_exp_gate, w_exp_up, w_exp_down, normf_g):
    bsz, seq, _ = x.shape
    ctx_len = ctx.shape[1]
    i = 0

    cc = jnp.concatenate([c, c_ctx[None, :], jnp.zeros((MOD_ROWS - bsz - 1, D_MODEL), F32)], axis=0)
    mod = _modulation(cc, w_mod[i], b_mod[i][None, :])
    mod3 = mod.reshape(MOD_ROWS, 1, N_MOD * D_MODEL)

    bc0 = 2 * SSD_WIDTH
    dt0 = bc0 + 2 * BC_WIDTH
    dt1 = dt0 + 2 * SSD_HEADS
    wi = w_in[i]
    w_main = jnp.concatenate([wi[:, :bc0], wi[:, dt1:], wi[:, bc0:dt0]], axis=1).astype(BF16)
    w_dt = jnp.pad(wi[:, dt0:dt1], ((0, 0), (0, LANES - 2 * SSD_HEADS))).astype(BF16)
    g1row = norm1_g[i][None, :]

    x2 = x.reshape(bsz * seq, D_MODEL)
    ctx2 = ctx.reshape(bsz * ctx_len, D_MODEL)
    tm_in = 1024
    pm_x, dt_x = _in_proj(x2, mod3, lambda t: t // (seq // tm_in), g1row, w_main, w_dt, tm_in, 1024)
    pm_c, dt_c = _in_proj(ctx2, mod3, lambda t: bsz, g1row, w_main, w_dt, bsz * ctx_len, 1024)

    dtbias_g = _group_lanes(dt_bias_f[i], dt_bias_b[i])
    arow_g = _group_lanes(-jnp.exp(a_log_f[i].astype(F32)), -jnp.exp(a_log_b[i].astype(F32)))
    dskip_g = jnp.repeat(d_skip[i], SSD_HEAD_DIM).reshape(SSD_GROUPS, 1, GROUP_WIDTH)
    cw = conv_w[i]
    cb = conv_b[i][None, :]
    h_zero = jnp.zeros((bsz, SSD_GROUPS, SSD_STATE, GROUP_WIDTH), F32)

    hc_f, hc_b = _ssd(pm_c, _group_dt(dt_c, bsz * ctx_len), cw, cb, dtbias_g, arow_g, dskip_g, h_zero, h_zero,
                      batch=bsz, seq=ctx_len, conv_tile=ctx_len, conv_period=ctx_len, with_output=False)
    yd = _ssd(pm_x, _group_dt(dt_x, bsz * seq), cw, cb, dtbias_g, arow_g, dskip_g, hc_f, hc_b,
              batch=bsz, seq=seq, conv_tile=CHUNK, conv_period=GRID_W, with_output=True)

    x1 = _out_proj(yd, pm_x, x2, mod3, ssd_norm_g[i][None, :], cm_ln_g[i][None, :], cm_ln_b[i][None, :],
                   w_spatial[i].astype(BF16), b_spatial[i].T, w_out[i].astype(BF16), seq=seq, tm=512, tn=1024)

    w_re = jnp.transpose(w_router_expert[i], (1, 0, 2)).reshape(D_MODEL, N_EXPERTS)
    pad = LANES - N_EXPERTS - N_GROUPS
    w_router = jnp.pad(jnp.concatenate([w_re, w_router_group[i]], axis=1), ((0, 0), (0, pad)))
    b_router = jnp.pad(jnp.concatenate([b_router_expert[i].reshape(-1), b_router_group[i]]), (0, pad))[None, :]
    h2, rinfo, counts = _route_call(x1, mod3, norm2_g[i][None, :], w_router, b_router, seq=seq, tm=512)

    tm_e = 256
    pos, tile_expert, n_used, src_tok = _dispatch_plan(rinfo, counts, tm=tm_e)
    wg = w_exp_gate[i].reshape(N_EXPERTS, D_MODEL, EXPERT_FF)
    wu = w_exp_up[i].reshape(N_EXPERTS, D_MODEL, EXPERT_FF)
    wd = w_exp_down[i].reshape(N_EXPERTS, EXPERT_FF, D_MODEL)
    y_sorted = _experts_call(tile_expert, n_used, src_tok, h2, wg, wu, wd, tm=tm_e)
    tm_c = 256
    out = _combine_call(pos.reshape(bsz * seq // tm_c, 1, 2 * tm_c), x1, mod3, normf_g[None, :], rinfo, y_sorted,
                        seq=seq, tm=tm_c)
    return out.reshape(bsz, seq, D_MODEL)
```

```python
import functools

import jax
import jax.numpy as jnp
from jax import lax
from jax.experimental import pallas as pl
from jax.experimental.pallas import tpu as pltpu

F32 = jnp.float32
BF16 = jnp.bfloat16
HIGHEST = lax.Precision.HIGHEST

D_MODEL = 2048
GRID_W = 64
SSD_WIDTH = 2048
CM_WIDTH = 2048
SSD_HEADS = 32
SSD_HEAD_DIM = 64
SSD_GROUPS = 4
HEADS_PER_GROUP = SSD_HEADS // SSD_GROUPS
GROUP_WIDTH = HEADS_PER_GROUP * SSD_HEAD_DIM
SSD_STATE = 128
CHUNK = 128
BC_WIDTH = SSD_GROUPS * SSD_STATE
CM_HEADS = 8
CM_HEAD_DIM = CM_WIDTH // CM_HEADS
N_GROUPS = 4
N_EXPERTS = 32
EXPERTS_PER_GROUP = 8
EXPERT_FF = 512
N_MOD = 6
NORM_EPS = 1e-6
LANES = 128
NEG_BIG = -1e30
MOD_ROWS = 8
VMEM_LIMIT = 56 * 1024 * 1024


def _params(*sem):
    return pltpu.CompilerParams(dimension_semantics=sem, vmem_limit_bytes=VMEM_LIMIT)


def _silu(v):
    return v * jax.nn.sigmoid(v)


def _dot(a, b):
    return jnp.dot(a, b, preferred_element_type=F32)


def _mod_kernel(cc_ref, w_ref, b_ref, o_ref):
    a = _silu(cc_ref[...])
    hi = a.astype(BF16).astype(F32)
    lhs = jnp.concatenate([hi, a - hi], axis=0).astype(BF16)
    r = _dot(lhs, w_ref[...].astype(BF16))
    o_ref[...] = r[:MOD_ROWS] + r[MOD_ROWS:] + b_ref[...]


def _modulation(cc, w_mod, b_mod):
    n = w_mod.shape[1]
    tn = 1024
    return pl.pallas_call(
        _mod_kernel,
        grid=(n // tn,),
        in_specs=[
            pl.BlockSpec((MOD_ROWS, D_MODEL), lambda j: (0, 0)),
            pl.BlockSpec((D_MODEL, tn), lambda j: (0, j)),
            pl.BlockSpec((1, tn), lambda j: (0, j)),
        ],
        out_specs=pl.BlockSpec((MOD_ROWS, tn), lambda j: (0, j)),
        out_shape=jax.ShapeDtypeStruct((MOD_ROWS, n), F32),
        compiler_params=_params("arbitrary"),
        name="modulation",
    )(cc, w_mod, b_mod)


def _in_proj_kernel(x_ref, g_ref, sh_ref, sc_ref, w_ref, wdt_ref, o_ref, dt_ref, h_scr):
    tm = x_ref.shape[0]
    rc = 256

    @pl.when(pl.program_id(1) == 0)
    def _():
        gain = g_ref[...]
        scale = 1.0 + sc_ref[...]
        shift = sh_ref[...]

        def body(k, carry):
            r = pl.ds(pl.multiple_of(k * rc, rc), rc)
            xv = x_ref[r, :]
            ms = jnp.mean(xv * xv, axis=-1, keepdims=True)
            h = (xv * lax.rsqrt(ms + NORM_EPS) * gain) * scale + shift
            h_scr[r, :] = h.astype(BF16)
            return carry

        lax.fori_loop(0, tm // rc, body, 0)
        dt_ref[...] = _dot(h_scr[...], wdt_ref[...])

    o_ref[...] = _dot(h_scr[...], w_ref[...]).astype(o_ref.dtype)


def _in_proj(x2, mod3, mod_row_of_tile, norm_g, w_main, w_dt, tm, tn):
    m = x2.shape[0]
    n = w_main.shape[1]
    return pl.pallas_call(
        _in_proj_kernel,
        grid=(m // tm, n // tn),
        in_specs=[
            pl.BlockSpec((tm, D_MODEL), lambda i, j: (i, 0)),
            pl.BlockSpec((1, D_MODEL), lambda i, j: (0, 0)),
            pl.BlockSpec((None, 1, D_MODEL), lambda i, j: (mod_row_of_tile(i), 0, 0)),
            pl.BlockSpec((None, 1, D_MODEL), lambda i, j: (mod_row_of_tile(i), 0, 1)),
            pl.BlockSpec((D_MODEL, tn), lambda i, j: (0, j)),
            pl.BlockSpec((D_MODEL, LANES), lambda i, j: (0, 0)),
        ],
        out_specs=[
            pl.BlockSpec((tm, tn), lambda i, j: (i, j)),
            pl.BlockSpec((tm, LANES), lambda i, j: (i, 0)),
        ],
        out_shape=[
            jax.ShapeDtypeStruct((m, n), BF16),
            jax.ShapeDtypeStruct((m, LANES), F32),
        ],
        scratch_shapes=[pltpu.VMEM((tm, D_MODEL), BF16)],
        compiler_params=_params("parallel", "arbitrary"),
        name="in_proj",
    )(x2, norm_g, mod3, mod3, w_main, w_dt)


def _split3(v):
    hi = v.astype(BF16)
    rem = v - hi.astype(F32)
    mid = rem.astype(BF16)
    lo = (rem - mid.astype(F32)).astype(BF16)
    return hi, mid, lo


def _expand_heads(v, sel):
    hi, mid, lo = _split3(v)
    return _dot(hi, sel) + _dot(mid, sel) + _dot(lo, sel)


def _ssd_kernel(xs_ref, b_ref, c_ref, dt_ref, cwx_ref, cbx_ref, cwb_ref, cbb_ref, cwc_ref, cbc_ref,
                dtbias_ref, arow_ref, dskip_ref, h0f_ref, h0b_ref, *rest,
                n_chunks, conv_tile, conv_period, with_output):
    if with_output:
        y_ref, xs_s, b_s, c_s, cum_s, dts_s, hf_s, hb_s, y_s = rest
    else:
        hf_out, hb_out, xs_s, b_s, c_s, cum_s, dts_s, hf_s, hb_s = rest
    seq = n_chunks * CHUNK
    hpg = HEADS_PER_GROUP

    def conv_silu(src_ref, dst_ref, w_ref, bias_ref):
        cols = src_ref.shape[1]
        w0, w1, w2 = w_ref[0:1, :], w_ref[1:2, :], w_ref[2:3, :]
        bias = bias_ref[...]
        row = lax.broadcasted_iota(jnp.int32, (conv_tile, cols), 0) % conv_period
        first = row == 0
        last = row == conv_period - 1

        def body(k, carry):
            r = pl.ds(pl.multiple_of(k * conv_tile, conv_tile), conv_tile)
            v = src_ref[r, :].astype(F32)
            prev = jnp.where(first, 0.0, pltpu.roll(v, 1, 0))
            nxt = jnp.where(last, 0.0, pltpu.roll(v, conv_tile - 1, 0))
            y = bias + prev * w0 + v * w1 + nxt * w2
            dst_ref[r, :] = _silu(y).astype(BF16)
            return carry

        lax.fori_loop(0, seq // conv_tile, body, 0)

    conv_silu(xs_ref, xs_s, cwx_ref, cbx_ref)
    conv_silu(b_ref, b_s, cwb_ref, cbb_ref)
    conv_silu(c_ref, c_s, cwc_ref, cbc_ref)

    ii = lax.broadcasted_iota(jnp.int32, (CHUNK, CHUNK), 0)
    jj = lax.broadcasted_iota(jnp.int32, (CHUNK, CHUNK), 1)
    lower = ii >= jj
    upper = jj >= ii
    tri_lo = lower.astype(F32)
    tri_up = upper.astype(F32)
    arow = arow_ref[...]
    dtbias = dtbias_ref[...]
    group_shift = (LANES - 2 * hpg * pl.program_id(1)) % LANES

    def dt_body(k, carry):
        r = pl.ds(pl.multiple_of(k * CHUNK, CHUNK), CHUNK)
        raw = pltpu.roll(dt_ref[r, :], group_shift, 1) + dtbias
        dts = jnp.maximum(raw, 0.0) + jnp.log1p(jnp.exp(-jnp.abs(raw)))
        adt = dts * arow
        cf = jnp.dot(tri_lo, adt, precision=HIGHEST, preferred_element_type=F32)
        cr = jnp.dot(tri_up, adt, precision=HIGHEST, preferred_element_type=F32)
        dts_s[r, :] = dts
        cum_s[r, :] = jnp.where(jj < hpg, cf, cr)
        return carry

    lax.fori_loop(0, n_chunks, dt_body, 0)

    hf_s[...] = h0f_ref[...]
    hb_s[...] = h0b_ref[...]

    head_of_col = lax.broadcasted_iota(jnp.int32, (LANES, GROUP_WIDTH), 1) // SSD_HEAD_DIM
    sel_row = lax.broadcasted_iota(jnp.int32, (LANES, GROUP_WIDTH), 0)
    sel_f = (sel_row == head_of_col).astype(BF16)
    sel_b = (sel_row == head_of_col + hpg).astype(BF16)
    low_half = jj < SSD_HEAD_DIM

    def direction_terms(r, sel):
        cum = cum_s[r, :]
        dts = dts_s[r, :]
        tot = jnp.sum(dts * arow, axis=0, keepdims=True)
        wst = dts * jnp.exp(tot - cum)
        eoff = jnp.exp(cum)
        edec = jnp.broadcast_to(jnp.exp(tot), (8, LANES))
        ex = _expand_heads(jnp.concatenate([wst, eoff, edec], axis=0), sel)
        return cum, dts, ex[:CHUNK], ex[CHUNK:2 * CHUNK], ex[2 * CHUNK:2 * CHUNK + 1]

    def state_step(h_s, r, sel):
        cum, dts, w_state, e_off, e_dec = direction_terms(r, sel)
        xc = xs_s[r, :]
        xw = (xc.astype(F32) * w_state).astype(BF16)
        s_new = lax.dot_general(b_s[r, :], xw, (((0,), (0,)), ((), ())), preferred_element_type=F32)
        h = h_s[...]
        y_off = _dot(c_s[r, :], h.astype(BF16)) * e_off
        h_s[...] = h * e_dec + s_new
        return cum, dts, xc, y_off

    def fwd_body(k, carry):
        r = pl.ds(pl.multiple_of(k * CHUNK, CHUNK), CHUNK)
        cum, dts, xc, y_off = state_step(hf_s, r, sel_f)
        if with_output:
            g = lax.dot_general(c_s[r, :], b_s[r, :], (((1,), (1,)), ((), ())), preferred_element_type=F32)
            cum_t = cum.T
            dts_t = dts.T
            pieces = []
            for pair in range(hpg // 2):
                ms = []
                for h in (2 * pair, 2 * pair + 1):
                    hb = h + hpg
                    mf = jnp.exp(jnp.where(lower, cum[:, h:h + 1] - cum_t[h:h + 1, :], NEG_BIG)) * dts_t[h:h + 1, :]
                    mb = jnp.exp(jnp.where(upper, cum[:, hb:hb + 1] - cum_t[hb:hb + 1, :], NEG_BIG)) * dts_t[hb:hb + 1, :]
                    ms.append((g * (mf + mb)).astype(BF16))
                lhs = jnp.concatenate(ms, axis=1)
                xp = xc[:, pair * LANES:(pair + 1) * LANES]
                zero = jnp.zeros_like(xp)
                rhs = jnp.concatenate([jnp.where(low_half, xp, zero), jnp.where(low_half, zero, xp)], axis=0)
                pieces.append(_dot(lhs, rhs))
            y_diag = jnp.concatenate(pieces, axis=1)
            y_s[r, :] = y_diag + y_off + xc.astype(F32) * dskip_ref[...]
        return carry

    lax.fori_loop(0, n_chunks, fwd_body, 0)

    def bwd_body(k, carry):
        r = pl.ds(pl.multiple_of((n_chunks - 1 - k) * CHUNK, CHUNK), CHUNK)
        _, _, _, y_off = state_step(hb_s, r, sel_b)
        if with_output:
            y_ref[r, :] = (y_s[r, :] + y_off).astype(y_ref.dtype)
        return carry

    lax.fori_loop(0, n_chunks, bwd_body, 0)

    if not with_output:
        hf_out[...] = hf_s[...]
        hb_out[...] = hb_s[...]


def _ssd(pm, dtg, conv_w, conv_b, dtbias_g, arow_g, dskip_g, h0f, h0b, *, batch, seq, conv_tile, conv_period,
         with_output):
    n_chunks = seq // CHUNK
    xs_blk0 = SSD_WIDTH // GROUP_WIDTH
    b_blk0 = (2 * SSD_WIDTH + 2 * CM_WIDTH) // SSD_STATE
    c_blk0 = b_blk0 + SSD_GROUPS
    cw_b0 = SSD_WIDTH // SSD_STATE
    cw_c0 = cw_b0 + SSD_GROUPS
    state_spec = pl.BlockSpec((None, None, SSD_STATE, GROUP_WIDTH), lambda b, g: (b, g, 0, 0))
    in_specs = [
        pl.BlockSpec((seq, GROUP_WIDTH), lambda b, g: (b, xs_blk0 + g)),
        pl.BlockSpec((seq, SSD_STATE), lambda b, g: (b, b_blk0 + g)),
        pl.BlockSpec((seq, SSD_STATE), lambda b, g: (b, c_blk0 + g)),
        pl.BlockSpec((seq, LANES), lambda b, g: (b, 0)),
        pl.BlockSpec((3, GROUP_WIDTH), lambda b, g: (0, g)),
        pl.BlockSpec((1, GROUP_WIDTH), lambda b, g: (0, g)),
        pl.BlockSpec((3, SSD_STATE), lambda b, g: (0, cw_b0 + g)),
        pl.BlockSpec((1, SSD_STATE), lambda b, g: (0, cw_b0 + g)),
        pl.BlockSpec((3, SSD_STATE), lambda b, g: (0, cw_c0 + g)),
        pl.BlockSpec((1, SSD_STATE), lambda b, g: (0, cw_c0 + g)),
        pl.BlockSpec((None, 1, LANES), lambda b, g: (g, 0, 0)),
        pl.BlockSpec((None, 1, LANES), lambda b, g: (g, 0, 0)),
        pl.BlockSpec((None, 1, GROUP_WIDTH), lambda b, g: (g, 0, 0)),
        state_spec,
        state_spec,
    ]
    scratch = [
        pltpu.VMEM((seq, GROUP_WIDTH), BF16),
        pltpu.VMEM((seq, SSD_STATE), BF16),
        pltpu.VMEM((seq, SSD_STATE), BF16),
        pltpu.VMEM((seq, LANES), F32),
        pltpu.VMEM((seq, LANES), F32),
        pltpu.VMEM((SSD_STATE, GROUP_WIDTH), F32),
        pltpu.VMEM((SSD_STATE, GROUP_WIDTH), F32),
    ]
    if with_output:
        out_specs = pl.BlockSpec((seq, GROUP_WIDTH), lambda b, g: (b, g))
        out_shape = jax.ShapeDtypeStruct((batch * seq, SSD_WIDTH), BF16)
        scratch.append(pltpu.VMEM((seq, GROUP_WIDTH), F32))
    else:
        out_specs = [state_spec, state_spec]
        out_shape = [jax.ShapeDtypeStruct((batch, SSD_GROUPS, SSD_STATE, GROUP_WIDTH), F32)] * 2
    return pl.pallas_call(
        functools.partial(_ssd_kernel, n_chunks=n_chunks, conv_tile=conv_tile, conv_period=conv_period,
                          with_output=with_output),
        grid=(batch, SSD_GROUPS),
        in_specs=in_specs,
        out_specs=out_specs,
        out_shape=out_shape,
        scratch_shapes=scratch,
        compiler_params=_params("parallel", "parallel"),
        name="ssd_out" if with_output else "ssd_ctx",
    )(pm, pm, pm, dtg, conv_w, conv_b, conv_w, conv_b, conv_w, conv_b, dtbias_g, arow_g, dskip_g, h0f, h0b)


def _out_proj_kernel(yd_ref, z_ref, u_ref, v_ref, x_ref, g1_ref, ng_ref, lng_ref, lnb_ref, ws_ref, bst_ref, w_ref,
                     o_ref, mix_scr):
    tm = yd_ref.shape[0]

    @pl.when(pl.program_id(1) == 0)
    def _():
        def body(k, carry):
            r = pl.ds(pl.multiple_of(k * CHUNK, CHUNK), CHUNK)
            a = yd_ref[r, :].astype(F32) * _silu(z_ref[r, :].astype(F32))
            ms = jnp.mean(a * a, axis=-1, keepdims=True)
            mix_scr[r, 0:SSD_WIDTH] = (a * lax.rsqrt(ms + NORM_EPS) * ng_ref[...]).astype(BF16)
            gv = jax.nn.gelu(v_ref[r, :].astype(F32))
            mu = jnp.mean(gv, axis=-1, keepdims=True)
            xc = gv - mu
            var = jnp.mean(xc * xc, axis=-1, keepdims=True)
            ln = ((xc * lax.rsqrt(var + NORM_EPS)) * lng_ref[...] + lnb_ref[...]).astype(BF16)
            gu = jax.nn.gelu(u_ref[r, :].astype(F32))
            for h in range(CM_HEADS):
                c0, c1 = h * CM_HEAD_DIM, (h + 1) * CM_HEAD_DIM
                s = _dot(ws_ref[h], ln[:, c0:c1]) + bst_ref[:, h:h + 1]
                mix_scr[r, SSD_WIDTH + c0:SSD_WIDTH + c1] = (gu[:, c0:c1] * s).astype(BF16)
            return carry

        lax.fori_loop(0, tm // CHUNK, body, 0)

    o_ref[...] = x_ref[...] + g1_ref[...] * _dot(mix_scr[...], w_ref[...])


def _out_proj(yd, pm, x2, mod3, ssd_norm_g, ln_g, ln_b, ws, bst, w_out, *, seq, tm, tn):
    m = x2.shape[0]
    tiles_per_batch = seq // tm
    z_blk, u_blk, v_blk = 0, 2 * SSD_WIDTH // CM_WIDTH, 2 * SSD_WIDTH // CM_WIDTH + 1
    g1_blk0 = 2 * D_MODEL // tn
    row = lambda shape: pl.BlockSpec(shape, lambda i, j: (0, 0))
    return pl.pallas_call(
        _out_proj_kernel,
        grid=(m // tm, D_MODEL // tn),
        in_specs=[
            pl.BlockSpec((tm, SSD_WIDTH), lambda i, j: (i, 0)),
            pl.BlockSpec((tm, SSD_WIDTH), lambda i, j: (i, z_blk)),
            pl.BlockSpec((tm, CM_WIDTH), lambda i, j: (i, u_blk)),
            pl.BlockSpec((tm, CM_WIDTH), lambda i, j: (i, v_blk)),
            pl.BlockSpec((tm, tn), lambda i, j: (i, j)),
            pl.BlockSpec((None, 1, tn), lambda i, j: (i // tiles_per_batch, 0, g1_blk0 + j)),
            row((1, SSD_WIDTH)),
            row((1, CM_WIDTH)),
            row((1, CM_WIDTH)),
            pl.BlockSpec((CM_HEADS, CHUNK, CHUNK), lambda i, j: (0, 0, 0)),
            row((CHUNK, CM_HEADS)),
            pl.BlockSpec((SSD_WIDTH + CM_WIDTH, tn), lambda i, j: (0, j)),
        ],
        out_specs=pl.BlockSpec((tm, tn), lambda i, j: (i, j)),
        out_shape=jax.ShapeDtypeStruct((m, D_MODEL), F32),
        scratch_shapes=[pltpu.VMEM((tm, SSD_WIDTH + CM_WIDTH), BF16)],
        compiler_params=_params("parallel", "arbitrary"),
        name="out_proj",
    )(yd, pm, pm, pm, x2, mod3, ssd_norm_g, ln_g, ln_b, ws, bst, w_out)


def _route(logits):
    lane = lax.broadcasted_iota(jnp.int32, logits.shape, 1)
    lane_f = lane.astype(F32)
    is_group = (lane >= N_EXPERTS) & (lane < N_EXPERTS + N_GROUPS)
    lg = jnp.where(is_group, logits, NEG_BIG)
    mg = jnp.max(lg, axis=1, keepdims=True)
    top_pg = 1.0 / jnp.sum(jnp.exp(lg - mg), axis=1, keepdims=True)
    gi = jnp.min(jnp.where(lg == mg, lane_f, 1e9), axis=1, keepdims=True) - N_EXPERTS
    in_group = (lane < N_EXPERTS) & ((lane // EXPERTS_PER_GROUP).astype(F32) == gi)
    le = jnp.where(in_group, logits, NEG_BIG)
    m1 = jnp.max(le, axis=1, keepdims=True)
    i1 = jnp.min(jnp.where(le == m1, lane_f, 1e9), axis=1, keepdims=True)
    le2 = jnp.where(lane_f == i1, NEG_BIG, le)
    m2 = jnp.max(le2, axis=1, keepdims=True)
    i2 = jnp.min(jnp.where(le2 == m2, lane_f, 1e9), axis=1, keepdims=True)
    e2 = jnp.exp(m2 - m1)
    p1 = 1.0 / (1.0 + e2)
    p2 = e2 * p1
    return i1, i2, p1 * top_pg, p2 * top_pg


RINFO_E1, RINFO_E2, RINFO_W1, RINFO_W2, RINFO_R1, RINFO_R2 = range(6)
ROW_SLABS = D_MODEL // LANES


def _store_slabs(ref3, v):
    for k in range(ROW_SLABS):
        ref3[:, k, :] = v[:, k * LANES:(k + 1) * LANES]


def _load_slabs(ref3):
    return jnp.concatenate([ref3[:, k, :] for k in range(ROW_SLABS)], axis=1)


def _route_kernel(x1_ref, n2g_ref, sh2_ref, sc2_ref, wr_ref, br_ref, h2_ref, rinfo_ref, counts_ref, cnt_scr):
    tm = x1_ref.shape[0]

    @pl.when(pl.program_id(0) == 0)
    def _():
        cnt_scr[...] = jnp.zeros_like(cnt_scr)

    x1 = x1_ref[...]
    ms = jnp.mean(x1 * x1, axis=-1, keepdims=True)
    h2 = (x1 * lax.rsqrt(ms + NORM_EPS) * n2g_ref[...]) * (1.0 + sc2_ref[...]) + sh2_ref[...]
    _store_slabs(h2_ref, h2)
    logits =jnp.dot(h2, wr_ref[...], precision=HIGHEST, preferred_element_type=F32) + br_ref[...]
    i1, i2, w1, w2 = _route(logits)

    lane = lax.broadcasted_iota(jnp.int32, (tm, LANES), 1)
    lane_f = lane.astype(F32)
    oh1 = jnp.where(lane_f == i1, 1.0, 0.0)
    oh2 = jnp.where(lane_f == i2, 1.0, 0.0)
    before = (lax.broadcasted_iota(jnp.int32, (tm, tm), 0) > lax.broadcasted_iota(jnp.int32, (tm, tm), 1)).astype(BF16)
    carried = cnt_scr[...]
    tot1 = jnp.sum(oh1, axis=0, keepdims=True)
    r1 = jnp.sum(oh1 * (_dot(before, oh1.astype(BF16)) + carried), axis=1, keepdims=True)
    r2 = jnp.sum(oh2 * (_dot(before, oh2.astype(BF16)) + (carried + tot1)), axis=1, keepdims=True)
    counts = carried + tot1 + jnp.sum(oh2, axis=0, keepdims=True)
    cnt_scr[...] = counts
    counts_ref[...] = jnp.broadcast_to(counts, counts_ref.shape)

    info = jnp.zeros((tm, LANES), F32)
    for k, v in ((RINFO_E1, i1), (RINFO_E2, i2), (RINFO_W1, w1), (RINFO_W2, w2), (RINFO_R1, r1), (RINFO_R2, r2)):
        info = jnp.where(lane == k, v, info)
    rinfo_ref[...] = info


def _route_call(x1, mod3, norm2_g, w_router, b_router, *, seq, tm):
    m = x1.shape[0]
    tiles_per_batch = seq // tm
    row = lambda shape: pl.BlockSpec(shape, lambda i: (0, 0))
    modrow = lambda k: pl.BlockSpec((None, 1, D_MODEL), lambda i: (i // tiles_per_batch, 0, k))
    return pl.pallas_call(
        _route_kernel,
        grid=(m // tm,),
        in_specs=[
            pl.BlockSpec((tm, D_MODEL), lambda i: (i, 0)),
            row((1, D_MODEL)),
            modrow(3),
            modrow(4),
            row((D_MODEL, LANES)),
            row((1, LANES)),
        ],
        out_specs=[
            pl.BlockSpec((tm, ROW_SLABS, LANES), lambda i: (i, 0, 0)),
            pl.BlockSpec((tm, LANES), lambda i: (i, 0)),
            row((8, LANES)),
        ],
        out_shape=[
            jax.ShapeDtypeStruct((m, ROW_SLABS, LANES), F32),
            jax.ShapeDtypeStruct((m, LANES), F32),
            jax.ShapeDtypeStruct((8, LANES), F32),
        ],
        scratch_shapes=[pltpu.VMEM((1, LANES), F32)],
        compiler_params=_params("arbitrary"),
        name="route",
    )(x1, norm2_g, mod3, mod3, w_router, b_router)


def _experts_kernel(te_ref, nu_ref, idx_cur, idx_nxt, h2_hbm, wg_ref, wu_ref, wd_ref, y_ref,
                    xbuf, wg_b, wu_b, wd_b, sem):
    j = pl.program_id(0)
    n_used = nu_ref[0]
    tm = xbuf.shape[1]
    slot = j % 2

    def row_copy(idx_ref, r, s):
        return pltpu.make_async_copy(h2_hbm.at[idx_ref[0, r]], xbuf.at[s, r], sem.at[s])

    def wait_tile(s):
        pltpu.make_async_copy(h2_hbm.at[pl.ds(0, tm)], xbuf.at[s], sem.at[s]).wait()

    @pl.when(j == 0)
    def _():
        def body(r, carry):
            row_copy(idx_cur, r, 0).start()
            return carry

        lax.fori_loop(0, tm, body, 0)

    @pl.when((j < n_used) & ((j == 0) | (te_ref[j] != te_ref[jnp.maximum(j - 1, 0)])))
    def _():
        wg_b[...] = wg_ref[...].astype(BF16)
        wu_b[...] = wu_ref[...].astype(BF16)
        wd_b[...] = wd_ref[...].astype(BF16)

    @pl.when(j < n_used)
    def _():
        wait_tile(slot)
        for r in range(tm):
            row_copy(idx_nxt, r, 1 - slot).start()
        xt = _load_slabs(xbuf.at[slot]).astype(BF16)
        hid = (_silu(_dot(xt, wg_b[...])) * _dot(xt, wu_b[...])).astype(BF16)
        _store_slabs(y_ref, _dot(hid, wd_b[...]))

    @pl.when(j == n_used - 1)
    def _():
        wait_tile(1 - slot)

    @pl.when(j >= n_used)
    def _():
        y_ref[...] = jnp.zeros_like(y_ref)


def _experts_call(tile_expert, n_used, src_tok, h2, wg, wu, wd, *, tm):
    n_tiles = tile_expert.shape[0]
    wspec = lambda shape: pl.BlockSpec((None,) + shape, lambda j, te, nu: (te[j], 0, 0))
    grid_spec = pltpu.PrefetchScalarGridSpec(
        num_scalar_prefetch=2,
        grid=(n_tiles,),
        in_specs=[
            pl.BlockSpec((None, 1, tm), lambda j, te, nu: (j, 0, 0), memory_space=pltpu.SMEM),
            pl.BlockSpec((None, 1, tm), lambda j, te, nu: (jnp.minimum(j + 1, n_tiles - 1), 0, 0),
                         memory_space=pltpu.SMEM),
            pl.BlockSpec(memory_space=pl.ANY),
            wspec((D_MODEL, EXPERT_FF)),
            wspec((D_MODEL, EXPERT_FF)),
            wspec((EXPERT_FF, D_MODEL)),
        ],
        out_specs=pl.BlockSpec((tm, ROW_SLABS, LANES), lambda j, te, nu: (j, 0, 0)),
        scratch_shapes=[
            pltpu.VMEM((2, tm, ROW_SLABS, LANES), F32),
            pltpu.VMEM((D_MODEL, EXPERT_FF), BF16),
            pltpu.VMEM((D_MODEL, EXPERT_FF), BF16),
            pltpu.VMEM((EXPERT_FF, D_MODEL), BF16),
            pltpu.SemaphoreType.DMA((2,)),
        ],
    )
    return pl.pallas_call(
        _experts_kernel,
        grid_spec=grid_spec,
        out_shape=jax.ShapeDtypeStruct((n_tiles * tm, ROW_SLABS, LANES), F32),
        compiler_params=_params("arbitrary"),
        name="experts",
    )(tile_expert, n_used, src_tok, src_tok, h2, wg, wu, wd)


def _combine_kernel(pos_cur, pos_nxt, x1_ref, g2_ref, nfg_ref, rinfo_ref, y_hbm, o_ref, ybuf, sem):
    i = pl.program_id(0)
    tm = x1_ref.shape[0]
    slot = i % 2

    def row_copy(pos_ref, r, k, s):
        return pltpu.make_async_copy(y_hbm.at[pos_ref[0, 2 * r + k]], ybuf.at[s, k, r], sem.at[s])

    def wait_tile(s):
        for k in range(2):
            pltpu.make_async_copy(y_hbm.at[pl.ds(0, tm)], ybuf.at[s, k], sem.at[s]).wait()

    @pl.when(i == 0)
    def _():
        def body(r, carry):
            for k in range(2):
                row_copy(pos_cur, r, k, 0).start()
            return carry

        lax.fori_loop(0, tm, body, 0)

    wait_tile(slot)
    for r in range(tm):
        for k in range(2):
            row_copy(pos_nxt, r, k, 1 - slot).start()
    info = rinfo_ref[...]
    w1 = info[:, RINFO_W1:RINFO_W1 + 1]
    w2 = info[:, RINFO_W2:RINFO_W2 + 1]
    moe = w1 * _load_slabs(ybuf.at[slot, 0]) + w2 * _load_slabs(ybuf.at[slot, 1])
    y = x1_ref[...] + g2_ref[...] * moe
    ms = jnp.mean(y * y, axis=-1, keepdims=True)
    o_ref[...] = y * lax.rsqrt(ms + NORM_EPS) * nfg_ref[...]

    @pl.when(i == pl.num_programs(0) - 1)
    def _():
        wait_tile(1 - slot)


def _combine_call(pos, x1, mod3, normf_g, rinfo, y_sorted, *, seq, tm):
    m = x1.shape[0]
    n_tiles = m // tm
    tiles_per_batch = seq // tm
    return pl.pallas_call(
        _combine_kernel,
        grid=(n_tiles,),
        in_specs=[
            pl.BlockSpec((None, 1, 2 * tm), lambda i: (i, 0, 0), memory_space=pltpu.SMEM),
            pl.BlockSpec((None, 1, 2 * tm), lambda i: (jnp.minimum(i + 1, n_tiles - 1), 0, 0),
                         memory_space=pltpu.SMEM),
            pl.BlockSpec((tm, D_MODEL), lambda i: (i, 0)),
            pl.BlockSpec((None, 1, D_MODEL), lambda i: (i // tiles_per_batch, 0, 5)),
            pl.BlockSpec((1, D_MODEL), lambda i: (0, 0)),
            pl.BlockSpec((tm, LANES), lambda i: (i, 0)),
            pl.BlockSpec(memory_space=pl.ANY),
        ],
        out_specs=pl.BlockSpec((tm, D_MODEL), lambda i: (i, 0)),
        out_shape=jax.ShapeDtypeStruct((m, D_MODEL), F32),
        scratch_shapes=[
            pltpu.VMEM((2, 2, tm, ROW_SLABS, LANES), F32),
            pltpu.SemaphoreType.DMA((2,)),
        ],
        compiler_params=_params("arbitrary"),
        name="combine",
    )(pos, pos, x1, mod3, normf_g, rinfo, y_sorted)


def _dispatch_plan(rinfo, counts, *, tm):
    n_tok = rinfo.shape[0]
    n_tiles = (2 * n_tok + N_EXPERTS * (tm - 1)) // tm + 1
    expert = rinfo[:, RINFO_E1:RINFO_E2 + 1].astype(jnp.int32)
    rank = rinfo[:, RINFO_R1:RINFO_R2 + 1].astype(jnp.int32)
    cnt = counts[0, :N_EXPERTS].astype(jnp.int32)
    tiles_e = (cnt + tm - 1) // tm
    end_tile = jnp.cumsum(tiles_e)
    start_row = (end_tile - tiles_e) * tm
    pos = start_row[expert] + rank
    n_used = end_tile[-1:]
    tile_ids = jnp.minimum(jnp.arange(n_tiles, dtype=jnp.int32), n_used - 1)
    tile_expert = jnp.sum(tile_ids[:, None] >= end_tile[None, :], axis=1).astype(jnp.int32)
    tok = jnp.repeat(jnp.arange(n_tok, dtype=jnp.int32), 2)
    src_tok = jnp.zeros((n_tiles * tm,), jnp.int32).at[pos.reshape(-1)].set(tok)
    return pos, tile_expert, n_used.astype(jnp.int32), src_tok.reshape(n_tiles, 1, tm)


def _group_lanes(fwd, bwd):
    both = jnp.concatenate([fwd.reshape(SSD_GROUPS, HEADS_PER_GROUP), bwd.reshape(SSD_GROUPS, HEADS_PER_GROUP)], axis=1)
    return jnp.pad(both, ((0, 0), (0, LANES - 2 * HEADS_PER_GROUP))).reshape(SSD_GROUPS, 1, LANES)


def kernel(x, c, ctx, c_ctx, w_mod, b_mod, norm1_g, w_in, conv_w, conv_b, dt_bias_f, dt_bias_b, a_log_f, a_log_b,
           d_skip, ssd_norm_g, cm_ln_g, cm_ln_b, w_spatial, b_spatial, w_out, norm2_g, w_router_group,
           b_router_group, w_router_expert, b_router_expert, w_exp_gate, w_exp_up, w_exp_down, normf_g):
    bsz, seq, _ = x.shape
    ctx_len = ctx.shape[1]
    i = 0

    cc = jnp.concatenate([c, c_ctx[None, :], jnp.zeros((MOD_ROWS - bsz - 1, D_MODEL), F32)], axis=0)
    mod = _modulation(cc, w_mod[i], b_mod[i][None, :])
    mod3 = mod.reshape(MOD_ROWS, 1, N_MOD * D_MODEL)

    bc0 = 2 * SSD_WIDTH
    dt0 = bc0 + 2 * BC_WIDTH
    dt1 = dt0 + 2 * SSD_HEADS
    wi = w_in[i]
    w_main = jnp.concatenate([wi[:, :bc0], wi[:, dt1:], wi[:, bc0:dt0]], axis=1).astype(BF16)
    w_dt = wi[:, dt0:dt1].reshape(D_MODEL, 2, SSD_GROUPS, HEADS_PER_GROUP)
    w_dt = jnp.transpose(w_dt, (0, 2, 1, 3)).reshape(D_MODEL, 2 * SSD_HEADS)
    w_dt = jnp.pad(w_dt, ((0, 0), (0, LANES - 2 * SSD_HEADS))).astype(BF16)
    g1row = norm1_g[i][None, :]

    x2 = x.reshape(bsz * seq, D_MODEL)
    ctx2 = ctx.reshape(bsz * ctx_len, D_MODEL)
    tm_in = 1024
    pm_x, dt_x = _in_proj(x2, mod3, lambda t: t // (seq // tm_in), g1row, w_main, w_dt, tm_in, 1024)
    pm_c, dt_c = _in_proj(ctx2, mod3, lambda t: bsz, g1row, w_main, w_dt, bsz * ctx_len, 1024)

    dtbias_g = _group_lanes(dt_bias_f[i], dt_bias_b[i])
    arow_g = _group_lanes(-jnp.exp(a_log_f[i].astype(F32)), -jnp.exp(a_log_b[i].astype(F32)))
    dskip_g = jnp.repeat(d_skip[i], SSD_HEAD_DIM).reshape(SSD_GROUPS, 1, GROUP_WIDTH)
    cw = conv_w[i]
    cb = conv_b[i][None, :]
    h_zero = jnp.zeros((bsz, SSD_GROUPS, SSD_STATE, GROUP_WIDTH), F32)

    hc_f, hc_b = _ssd(pm_c, dt_c, cw, cb, dtbias_g, arow_g, dskip_g, h_zero, h_zero,
                      batch=bsz, seq=ctx_len, conv_tile=ctx_len, conv_period=ctx_len, with_output=False)
    yd = _ssd(pm_x, dt_x, cw, cb, dtbias_g, arow_g, dskip_g, hc_f, hc_b,
              batch=bsz, seq=seq, conv_tile=CHUNK, conv_period=GRID_W, with_output=True)

    x1 = _out_proj(yd, pm_x, x2, mod3, ssd_norm_g[i][None, :], cm_ln_g[i][None, :], cm_ln_b[i][None, :],
                   w_spatial[i].astype(BF16), b_spatial[i].T, w_out[i].astype(BF16), seq=seq, tm=512, tn=1024)

    w_re = jnp.transpose(w_router_expert[i], (1, 0, 2)).reshape(D_MODEL, N_EXPERTS)
    pad = LANES - N_EXPERTS - N_GROUPS
    w_router = jnp.pad(jnp.concatenate([w_re, w_router_group[i]], axis=1), ((0, 0), (0, pad)))
    b_router = jnp.pad(jnp.concatenate([b_router_expert[i].reshape(-1), b_router_group[i]]), (0, pad))[None, :]
    h2, rinfo, counts = _route_call(x1, mod3, norm2_g[i][None, :], w_router, b_router, seq=seq, tm=512)

    tm_e = 256
    pos, tile_expert, n_used, src_tok = _dispatch_plan(rinfo, counts, tm=tm_e)
    wg = w_exp_gate[i].reshape(N_EXPERTS, D_MODEL, EXPERT_FF)
    wu = w_exp_up[i].reshape(N_EXPERTS, D_MODEL, EXPERT_FF)
    wd = w_exp_down[i].reshape(N_EXPERTS, EXPERT_FF, D_MODEL)
    y_sorted = _experts_call(tile_expert, n_used, src_tok, h2, wg, wu, wd, tm=tm_e)
    tm_c = 256
    out = _combine_call(pos.reshape(bsz * seq // tm_c, 1, 2 * tm_c), x1, mod3, normf_g[None, :], rinfo, y_sorted,
                        seq=seq, tm=tm_c)
    return out.reshape(bsz, seq, D_MODEL)
```

```python
import functools

import jax
import jax.numpy as jnp
from jax import lax
from jax.experimental import pallas as pl
from jax.experimental.pallas import tpu as pltpu

F32 = jnp.float32
BF16 = jnp.bfloat16
HIGHEST = lax.Precision.HIGHEST

D_MODEL = 2048
GRID_W = 64
SSD_WIDTH = 2048
CM_WIDTH = 2048
SSD_HEADS = 32
SSD_HEAD_DIM = 64
SSD_GROUPS = 4
HEADS_PER_GROUP = SSD_HEADS // SSD_GROUPS
GROUP_WIDTH = HEADS_PER_GROUP * SSD_HEAD_DIM
SSD_STATE = 128
CHUNK = 128
BC_WIDTH = SSD_GROUPS * SSD_STATE
CM_HEADS = 8
CM_HEAD_DIM = CM_WIDTH // CM_HEADS
N_GROUPS = 4
N_EXPERTS = 32
EXPERTS_PER_GROUP = 8
EXPERT_FF = 512
N_MOD = 6
NORM_EPS = 1e-6
LANES = 128
NEG_BIG = -1e30
MOD_ROWS = 8
VMEM_LIMIT = 56 * 1024 * 1024


def _params(*sem):
    return pltpu.CompilerParams(dimension_semantics=sem, vmem_limit_bytes=VMEM_LIMIT)


def _silu(v):
    return v * jax.nn.sigmoid(v)


def _dot(a, b):
    return jnp.dot(a, b, preferred_element_type=F32)


def _mod_kernel(cc_ref, w_ref, b_ref, o_ref):
    a = _silu(cc_ref[...])
    hi = a.astype(BF16).astype(F32)
    lhs = jnp.concatenate([hi, a - hi], axis=0).astype(BF16)
    r = _dot(lhs, w_ref[...].astype(BF16))
    o_ref[...] = r[:MOD_ROWS] + r[MOD_ROWS:] + b_ref[...]


def _modulation(cc, w_mod, b_mod):
    n = w_mod.shape[1]
    tn = 1024
    return pl.pallas_call(
        _mod_kernel,
        grid=(n // tn,),
        in_specs=[
            pl.BlockSpec((MOD_ROWS, D_MODEL), lambda j: (0, 0)),
            pl.BlockSpec((D_MODEL, tn), lambda j: (0, j)),
            pl.BlockSpec((1, tn), lambda j: (0, j)),
        ],
        out_specs=pl.BlockSpec((MOD_ROWS, tn), lambda j: (0, j)),
        out_shape=jax.ShapeDtypeStruct((MOD_ROWS, n), F32),
        compiler_params=_params("arbitrary"),
        name="modulation",
    )(cc, w_mod, b_mod)


PREP_COLS = 512
PREP_UV_BLK0 = 2 * SSD_WIDTH // PREP_COLS
PREP_BC_BLK0 = (2 * SSD_WIDTH + 2 * CM_WIDTH) // PREP_COLS
PREP_N_BLKS = PREP_BC_BLK0 + 2 * BC_WIDTH // PREP_COLS
DT_COL0 = 2 * SSD_WIDTH + 2 * BC_WIDTH
UV_LANE_SHIFT = 2 * SSD_HEADS


def _w_in_prep_kernel(a_ref, nxt_ref, o_ref):
    j = pl.program_id(0)
    shifted = (j >= PREP_UV_BLK0) & (j < PREP_BC_BLK0)
    rc = 256

    @pl.when(jnp.logical_not(shifted))
    def _():
        o_ref[...] = a_ref[...].astype(BF16)

    @pl.when(shifted)
    def _():
        low = lax.broadcasted_iota(jnp.int32, (rc, LANES), 1) < LANES - UV_LANE_SHIFT

        def body(k, carry):
            r = pl.ds(pl.multiple_of(k * rc, rc), rc)
            tiles = [a_ref[r, t * LANES:(t + 1) * LANES] for t in range(PREP_COLS // LANES)] + [nxt_ref[r, :]]
            rolled = [pltpu.roll(t, LANES - UV_LANE_SHIFT, 1) for t in tiles]
            for t in range(PREP_COLS // LANES):
                o_ref[r, t * LANES:(t + 1) * LANES] = jnp.where(low, rolled[t], rolled[t + 1]).astype(BF16)
            return carry

        lax.fori_loop(0, D_MODEL // rc, body, 0)


def _w_in_prep(w_in):
    uv_src0 = DT_COL0 // PREP_COLS
    bc_src0 = 2 * SSD_WIDTH // PREP_COLS

    def src_blk(j):
        return jnp.where(j < PREP_UV_BLK0, j, jnp.where(j < PREP_BC_BLK0, j - PREP_UV_BLK0 + uv_src0,
                                                        j - PREP_BC_BLK0 + bc_src0))

    def nxt_blk(j):
        uv = jnp.clip(j - PREP_UV_BLK0, 0, PREP_BC_BLK0 - PREP_UV_BLK0 - 1)
        return (DT_COL0 + (uv + 1) * PREP_COLS) // LANES

    return pl.pallas_call(
        _w_in_prep_kernel,
        grid=(PREP_N_BLKS,),
        in_specs=[
            pl.BlockSpec((D_MODEL, PREP_COLS), lambda j: (0, src_blk(j))),
            pl.BlockSpec((D_MODEL, LANES), lambda j: (0, nxt_blk(j))),
        ],
        out_specs=pl.BlockSpec((D_MODEL, PREP_COLS), lambda j: (0, j)),
        out_shape=jax.ShapeDtypeStruct((D_MODEL, PREP_N_BLKS * PREP_COLS), BF16),
        compiler_params=_params("arbitrary"),
        name="w_in_prep",
    )(w_in, w_in)


def _in_proj_kernel(x_ref, g_ref, sh_ref, sc_ref, w_ref, wdt_ref, o_ref, dt_ref, h_scr):
    tm = x_ref.shape[0]
    rc = 256

    @pl.when(pl.program_id(1) == 0)
    def _():
        gain = g_ref[...]
        scale = 1.0 + sc_ref[...]
        shift = sh_ref[...]

        def body(k, carry):
            r = pl.ds(pl.multiple_of(k * rc, rc), rc)
            xv = x_ref[r, :]
            ms = jnp.mean(xv * xv, axis=-1, keepdims=True)
            h = (xv * lax.rsqrt(ms + NORM_EPS) * gain) * scale + shift
            h_scr[r, :] = h.astype(BF16)
            return carry

        lax.fori_loop(0, tm // rc, body, 0)
        dt_ref[...] = _dot(h_scr[...], wdt_ref[...])

    o_ref[...] = _dot(h_scr[...], w_ref[...]).astype(o_ref.dtype)


def _in_proj(x2, mod3, mod_row_of_tile, norm_g, w_main, w_dt, tm, tn):
    m = x2.shape[0]
    n = w_main.shape[1]
    return pl.pallas_call(
        _in_proj_kernel,
        grid=(m // tm, n // tn),
        in_specs=[
            pl.BlockSpec((tm, D_MODEL), lambda i, j: (i, 0)),
            pl.BlockSpec((1, D_MODEL), lambda i, j: (0, 0)),
            pl.BlockSpec((None, 1, D_MODEL), lambda i, j: (mod_row_of_tile(i), 0, 0)),
            pl.BlockSpec((None, 1, D_MODEL), lambda i, j: (mod_row_of_tile(i), 0, 1)),
            pl.BlockSpec((D_MODEL, tn), lambda i, j: (0, j)),
            pl.BlockSpec((D_MODEL, LANES), lambda i, j: (0, 0)),
        ],
        out_specs=[
            pl.BlockSpec((tm, tn), lambda i, j: (i, j)),
            pl.BlockSpec((tm, LANES), lambda i, j: (i, 0)),
        ],
        out_shape=[
            jax.ShapeDtypeStruct((m, n), BF16),
            jax.ShapeDtypeStruct((m, LANES), F32),
        ],
        scratch_shapes=[pltpu.VMEM((tm, D_MODEL), BF16)],
        compiler_params=_params("parallel", "arbitrary"),
        name="in_proj",
    )(x2, norm_g, mod3, mod3, w_main, w_dt)


def _split3(v):
    hi = v.astype(BF16)
    rem = v - hi.astype(F32)
    mid = rem.astype(BF16)
    lo = (rem - mid.astype(F32)).astype(BF16)
    return hi, mid, lo


def _expand_heads(v, sel):
    hi, mid, lo = _split3(v)
    return _dot(hi, sel) + _dot(mid, sel) + _dot(lo, sel)


def _ssd_kernel(xs_ref, b_ref, c_ref, dt_ref, cwx_ref, cbx_ref, cwb_ref, cbb_ref, cwc_ref, cbc_ref,
                dtbias_ref, arow_ref, dskip_ref, h0f_ref, h0b_ref, *rest,
                n_chunks, conv_tile, conv_period, with_output):
    if with_output:
        y_ref, xs_s, b_s, c_s, cum_s, dts_s, hf_s, hb_s, y_s = rest
    else:
        hf_out, hb_out, xs_s, b_s, c_s, cum_s, dts_s, hf_s, hb_s = rest
    seq = n_chunks * CHUNK
    hpg = HEADS_PER_GROUP

    def conv_silu(src_ref, dst_ref, w_ref, bias_ref):
        cols = src_ref.shape[1]
        w0, w1, w2 = w_ref[0:1, :], w_ref[1:2, :], w_ref[2:3, :]
        bias = bias_ref[...]
        row = lax.broadcasted_iota(jnp.int32, (conv_tile, cols), 0) % conv_period
        first = row == 0
        last = row == conv_period - 1

        def body(k, carry):
            r = pl.ds(pl.multiple_of(k * conv_tile, conv_tile), conv_tile)
            v = src_ref[r, :].astype(F32)
            prev = jnp.where(first, 0.0, pltpu.roll(v, 1, 0))
            nxt = jnp.where(last, 0.0, pltpu.roll(v, conv_tile - 1, 0))
            y = bias + prev * w0 + v * w1 + nxt * w2
            dst_ref[r, :] = _silu(y).astype(BF16)
            return carry

        lax.fori_loop(0, seq // conv_tile, body, 0)

    conv_silu(xs_ref, xs_s, cwx_ref, cbx_ref)
    conv_silu(b_ref, b_s, cwb_ref, cbb_ref)
    conv_silu(c_ref, c_s, cwc_ref, cbc_ref)

    ii = lax.broadcasted_iota(jnp.int32, (CHUNK, CHUNK), 0)
    jj = lax.broadcasted_iota(jnp.int32, (CHUNK, CHUNK), 1)
    lower = ii >= jj
    upper = jj >= ii
    tri_lo = lower.astype(F32)
    tri_up = upper.astype(F32)
    arow = arow_ref[...]
    dtbias = dtbias_ref[...]
    group_shift = (LANES - 2 * hpg * pl.program_id(1)) % LANES

    def dt_body(k, carry):
        r = pl.ds(pl.multiple_of(k * CHUNK, CHUNK), CHUNK)
        raw = pltpu.roll(dt_ref[r, :], group_shift, 1) + dtbias
        dts = jnp.maximum(raw, 0.0) + jnp.log1p(jnp.exp(-jnp.abs(raw)))
        adt = dts * arow
        cf = jnp.dot(tri_lo, adt, precision=HIGHEST, preferred_element_type=F32)
        cr = jnp.dot(tri_up, adt, precision=HIGHEST, preferred_element_type=F32)
        dts_s[r, :] = dts
        cum_s[r, :] = jnp.where(jj < hpg, cf, cr)
        return carry

    lax.fori_loop(0, n_chunks, dt_body, 0)

    hf_s[...] = h0f_ref[...]
    hb_s[...] = h0b_ref[...]

    head_of_col = lax.broadcasted_iota(jnp.int32, (LANES, GROUP_WIDTH), 1) // SSD_HEAD_DIM
    sel_row = lax.broadcasted_iota(jnp.int32, (LANES, GROUP_WIDTH), 0)
    sel_f = (sel_row == head_of_col).astype(BF16)
    sel_b = (sel_row == head_of_col + hpg).astype(BF16)
    low_half = jj < SSD_HEAD_DIM

    def direction_terms(r, sel):
        cum = cum_s[r, :]
        dts = dts_s[r, :]
        tot = jnp.sum(dts * arow, axis=0, keepdims=True)
        wst = dts * jnp.exp(tot - cum)
        eoff = jnp.exp(cum)
        edec = jnp.broadcast_to(jnp.exp(tot), (8, LANES))
        ex = _expand_heads(jnp.concatenate([wst, eoff, edec], axis=0), sel)
        return cum, dts, ex[:CHUNK], ex[CHUNK:2 * CHUNK], ex[2 * CHUNK:2 * CHUNK + 1]

    def state_step(h_s, r, sel):
        cum, dts, w_state, e_off, e_dec = direction_terms(r, sel)
        xc = xs_s[r, :]
        xw = (xc.astype(F32) * w_state).astype(BF16)
        s_new = lax.dot_general(b_s[r, :], xw, (((0,), (0,)), ((), ())), preferred_element_type=F32)
        h = h_s[...]
        y_off = _dot(c_s[r, :], h.astype(BF16)) * e_off
        h_s[...] = h * e_dec + s_new
        return cum, dts, xc, y_off

    def fwd_body(k, carry):
        r = pl.ds(pl.multiple_of(k * CHUNK, CHUNK), CHUNK)
        cum, dts, xc, y_off = state_step(hf_s, r, sel_f)
        if with_output:
            g = lax.dot_general(c_s[r, :], b_s[r, :], (((1,), (1,)), ((), ())), preferred_element_type=F32)
            cum_t = cum.T
            dts_t = dts.T
            pieces = []
            for pair in range(hpg // 2):
                ms = []
                for h in (2 * pair, 2 * pair + 1):
                    hb = h + hpg
                    mf = jnp.exp(jnp.where(lower, cum[:, h:h + 1] - cum_t[h:h + 1, :], NEG_BIG)) * dts_t[h:h + 1, :]
                    mb = jnp.exp(jnp.where(upper, cum[:, hb:hb + 1] - cum_t[hb:hb + 1, :], NEG_BIG)) * dts_t[hb:hb + 1, :]
                    ms.append((g * (mf + mb)).astype(BF16))
                lhs = jnp.concatenate(ms, axis=1)
                xp = xc[:, pair * LANES:(pair + 1) * LANES]
                zero = jnp.zeros_like(xp)
                rhs = jnp.concatenate([jnp.where(low_half, xp, zero), jnp.where(low_half, zero, xp)], axis=0)
                pieces.append(_dot(lhs, rhs))
            y_diag = jnp.concatenate(pieces, axis=1)
            y_s[r, :] = y_diag + y_off + xc.astype(F32) * dskip_ref[...]
        return carry

    lax.fori_loop(0, n_chunks, fwd_body, 0)

    def bwd_body(k, carry):
        r = pl.ds(pl.multiple_of((n_chunks - 1 - k) * CHUNK, CHUNK), CHUNK)
        _, _, _, y_off = state_step(hb_s, r, sel_b)
        if with_output:
            y_ref[r, :] = (y_s[r, :] + y_off).astype(y_ref.dtype)
        return carry

    lax.fori_loop(0, n_chunks, bwd_body, 0)

    if not with_output:
        hf_out[...] = hf_s[...]
        hb_out[...] = hb_s[...]


def _ssd(pm, dtg, conv_w, conv_b, dtbias_g, arow_g, dskip_g, h0f, h0b, *, batch, seq, conv_tile, conv_period,
         with_output):
    n_chunks = seq // CHUNK
    xs_blk0 = SSD_WIDTH // GROUP_WIDTH
    b_blk0 = (2 * SSD_WIDTH + 2 * CM_WIDTH) // SSD_STATE
    c_blk0 = b_blk0 + SSD_GROUPS
    cw_b0 = SSD_WIDTH // SSD_STATE
    cw_c0 = cw_b0 + SSD_GROUPS
    state_spec = pl.BlockSpec((None, None, SSD_STATE, GROUP_WIDTH), lambda b, g: (b, g, 0, 0))
    in_specs = [
        pl.BlockSpec((seq, GROUP_WIDTH), lambda b, g: (b, xs_blk0 + g)),
        pl.BlockSpec((seq, SSD_STATE), lambda b, g: (b, b_blk0 + g)),
        pl.BlockSpec((seq, SSD_STATE), lambda b, g: (b, c_blk0 + g)),
        pl.BlockSpec((seq, LANES), lambda b, g: (b, 0)),
        pl.BlockSpec((3, GROUP_WIDTH), lambda b, g: (0, g)),
        pl.BlockSpec((1, GROUP_WIDTH), lambda b, g: (0, g)),
        pl.BlockSpec((3, SSD_STATE), lambda b, g: (0, cw_b0 + g)),
        pl.BlockSpec((1, SSD_STATE), lambda b, g: (0, cw_b0 + g)),
        pl.BlockSpec((3, SSD_STATE), lambda b, g: (0, cw_c0 + g)),
        pl.BlockSpec((1, SSD_STATE), lambda b, g: (0, cw_c0 + g)),
        pl.BlockSpec((None, 1, LANES), lambda b, g: (g, 0, 0)),
        pl.BlockSpec((None, 1, LANES), lambda b, g: (g, 0, 0)),
        pl.BlockSpec((None, 1, GROUP_WIDTH), lambda b, g: (g, 0, 0)),
        state_spec,
        state_spec,
    ]
    scratch = [
        pltpu.VMEM((seq, GROUP_WIDTH), BF16),
        pltpu.VMEM((seq, SSD_STATE), BF16),
        pltpu.VMEM((seq, SSD_STATE), BF16),
        pltpu.VMEM((seq, LANES), F32),
        pltpu.VMEM((seq, LANES), F32),
        pltpu.VMEM((SSD_STATE, GROUP_WIDTH), F32),
        pltpu.VMEM((SSD_STATE, GROUP_WIDTH), F32),
    ]
    if with_output:
        out_specs = pl.BlockSpec((seq, GROUP_WIDTH), lambda b, g: (b, g))
        out_shape = jax.ShapeDtypeStruct((batch * seq, SSD_WIDTH), BF16)
        scratch.append(pltpu.VMEM((seq, GROUP_WIDTH), F32))
    else:
        out_specs = [state_spec, state_spec]
        out_shape = [jax.ShapeDtypeStruct((batch, SSD_GROUPS, SSD_STATE, GROUP_WIDTH), F32)] * 2
    return pl.pallas_call(
        functools.partial(_ssd_kernel, n_chunks=n_chunks, conv_tile=conv_tile, conv_period=conv_period,
                          with_output=with_output),
        grid=(batch, SSD_GROUPS),
        in_specs=in_specs,
        out_specs=out_specs,
        out_shape=out_shape,
        scratch_shapes=scratch,
        compiler_params=_params("parallel", "parallel"),
        name="ssd_out" if with_output else "ssd_ctx",
    )(pm, pm, pm, dtg, conv_w, conv_b, conv_w, conv_b, conv_w, conv_b, dtbias_g, arow_g, dskip_g, h0f, h0b)


def _out_proj_kernel(yd_ref, z_ref, u_ref, v_ref, x_ref, g1_ref, ng_ref, lng_ref, lnb_ref, ws_ref, bst_ref, w_ref,
                     o_ref, mix_scr):
    tm = yd_ref.shape[0]

    @pl.when(pl.program_id(1) == 0)
    def _():
        def body(k, carry):
            r = pl.ds(pl.multiple_of(k * CHUNK, CHUNK), CHUNK)
            a = yd_ref[r, :].astype(F32) * _silu(z_ref[r, :].astype(F32))
            ms = jnp.mean(a * a, axis=-1, keepdims=True)
            mix_scr[r, 0:SSD_WIDTH] = (a * lax.rsqrt(ms + NORM_EPS) * ng_ref[...]).astype(BF16)
            gv = jax.nn.gelu(v_ref[r, :].astype(F32))
            mu = jnp.mean(gv, axis=-1, keepdims=True)
            xc = gv - mu
            var = jnp.mean(xc * xc, axis=-1, keepdims=True)
            ln = ((xc * lax.rsqrt(var + NORM_EPS)) * lng_ref[...] + lnb_ref[...]).astype(BF16)
            gu = jax.nn.gelu(u_ref[r, :].astype(F32))
            for h in range(CM_HEADS):
                c0, c1 = h * CM_HEAD_DIM, (h + 1) * CM_HEAD_DIM
                s = _dot(ws_ref[h], ln[:, c0:c1]) + bst_ref[:, h:h + 1]
                mix_scr[r, SSD_WIDTH + c0:SSD_WIDTH + c1] = (gu[:, c0:c1] * s).astype(BF16)
            return carry

        lax.fori_loop(0, tm // CHUNK, body, 0)

    o_ref[...] = x_ref[...] + g1_ref[...] * _dot(mix_scr[...], w_ref[...])


def _out_proj(yd, pm, x2, mod3, ssd_norm_g, ln_g, ln_b, ws, bst, w_out, *, seq, tm, tn):
    m = x2.shape[0]
    tiles_per_batch = seq // tm
    z_blk, u_blk, v_blk = 0, 2 * SSD_WIDTH // CM_WIDTH, 2 * SSD_WIDTH // CM_WIDTH + 1
    g1_blk0 = 2 * D_MODEL // tn
    row = lambda shape: pl.BlockSpec(shape, lambda i, j: (0, 0))
    return pl.pallas_call(
        _out_proj_kernel,
        grid=(m // tm, D_MODEL // tn),
        in_specs=[
            pl.BlockSpec((tm, SSD_WIDTH), lambda i, j: (i, 0)),
            pl.BlockSpec((tm, SSD_WIDTH), lambda i, j: (i, z_blk)),
            pl.BlockSpec((tm, CM_WIDTH), lambda i, j: (i, u_blk)),
            pl.BlockSpec((tm, CM_WIDTH), lambda i, j: (i, v_blk)),
            pl.BlockSpec((tm, tn), lambda i, j: (i, j)),
            pl.BlockSpec((None, 1, tn), lambda i, j: (i // tiles_per_batch, 0, g1_blk0 + j)),
            row((1, SSD_WIDTH)),
            row((1, CM_WIDTH)),
            row((1, CM_WIDTH)),
            pl.BlockSpec((CM_HEADS, CHUNK, CHUNK), lambda i, j: (0, 0, 0)),
            row((CHUNK, CM_HEADS)),
            pl.BlockSpec((SSD_WIDTH + CM_WIDTH, tn), lambda i, j: (0, j)),
        ],
        out_specs=pl.BlockSpec((tm, tn), lambda i, j: (i, j)),
        out_shape=jax.ShapeDtypeStruct((m, D_MODEL), F32),
        scratch_shapes=[pltpu.VMEM((tm, SSD_WIDTH + CM_WIDTH), BF16)],
        compiler_params=_params("parallel", "arbitrary"),
        name="out_proj",
    )(yd, pm, pm, pm, x2, mod3, ssd_norm_g, ln_g, ln_b, ws, bst, w_out)


def _route(logits):
    lane = lax.broadcasted_iota(jnp.int32, logits.shape, 1)
    lane_f = lane.astype(F32)
    is_group = (lane >= N_EXPERTS) & (lane < N_EXPERTS + N_GROUPS)
    lg = jnp.where(is_group, logits, NEG_BIG)
    mg = jnp.max(lg, axis=1, keepdims=True)
    top_pg = 1.0 / jnp.sum(jnp.exp(lg - mg), axis=1, keepdims=True)
    gi = jnp.min(jnp.where(lg == mg, lane_f, 1e9), axis=1, keepdims=True) - N_EXPERTS
    in_group = (lane < N_EXPERTS) & ((lane // EXPERTS_PER_GROUP).astype(F32) == gi)
    le = jnp.where(in_group, logits, NEG_BIG)
    m1 = jnp.max(le, axis=1, keepdims=True)
    i1 = jnp.min(jnp.where(le == m1, lane_f, 1e9), axis=1, keepdims=True)
    le2 = jnp.where(lane_f == i1, NEG_BIG, le)
    m2 = jnp.max(le2, axis=1, keepdims=True)
    i2 = jnp.min(jnp.where(le2 == m2, lane_f, 1e9), axis=1, keepdims=True)
    e2 = jnp.exp(m2 - m1)
    p1 = 1.0 / (1.0 + e2)
    p2 = e2 * p1
    return i1, i2, p1 * top_pg, p2 * top_pg


RINFO_E1, RINFO_E2, RINFO_W1, RINFO_W2, RINFO_R1, RINFO_R2 = range(6)


def _route_kernel(x1_ref, n2g_ref, sh2_ref, sc2_ref, wr_ref, br_ref, h2_ref, rinfo_ref, counts_ref, cnt_scr):
    tm = x1_ref.shape[0]

    @pl.when(pl.program_id(0) == 0)
    def _():
        cnt_scr[...] = jnp.zeros_like(cnt_scr)

    x1 = x1_ref[...]
    ms = jnp.mean(x1 * x1, axis=-1, keepdims=True)
    h2 = (x1 * lax.rsqrt(ms + NORM_EPS) * n2g_ref[...]) * (1.0 + sc2_ref[...]) + sh2_ref[...]
    h2_ref[...] = h2
    logits = jnp.dot(h2, wr_ref[...], precision=HIGHEST, preferred_element_type=F32) + br_ref[...]
    i1, i2, w1, w2 = _route(logits)

    lane = lax.broadcasted_iota(jnp.int32, (tm, LANES), 1)
    lane_f = lane.astype(F32)
    oh1 = jnp.where(lane_f == i1, 1.0, 0.0)
    oh2 = jnp.where(lane_f == i2, 1.0, 0.0)
    before = (lax.broadcasted_iota(jnp.int32, (tm, tm), 0) > lax.broadcasted_iota(jnp.int32, (tm, tm), 1)).astype(BF16)
    carried = cnt_scr[...]
    tot1 = jnp.sum(oh1, axis=0, keepdims=True)
    r1 = jnp.sum(oh1 * (_dot(before, oh1.astype(BF16)) + carried), axis=1, keepdims=True)
    r2 = jnp.sum(oh2 * (_dot(before, oh2.astype(BF16)) + (carried + tot1)), axis=1, keepdims=True)
    counts = carried + tot1 + jnp.sum(oh2, axis=0, keepdims=True)
    cnt_scr[...] = counts
    counts_ref[...] = jnp.broadcast_to(counts, counts_ref.shape)

    info = jnp.zeros((tm, LANES), F32)
    for k, v in ((RINFO_E1, i1), (RINFO_E2, i2), (RINFO_W1, w1), (RINFO_W2, w2), (RINFO_R1, r1), (RINFO_R2, r2)):
        info = jnp.where(lane == k, v, info)
    rinfo_ref[...] = info


def _route_call(x1, mod3, norm2_g, w_router, b_router, *, seq, tm):
    m = x1.shape[0]
    tiles_per_batch = seq // tm
    row = lambda shape: pl.BlockSpec(shape, lambda i: (0, 0))
    modrow = lambda k: pl.BlockSpec((None, 1, D_MODEL), lambda i: (i // tiles_per_batch, 0, k))
    return pl.pallas_call(
        _route_kernel,
        grid=(m // tm,),
        in_specs=[
            pl.BlockSpec((tm, D_MODEL), lambda i: (i, 0)),
            row((1, D_MODEL)),
            modrow(3),
            modrow(4),
            row((D_MODEL, LANES)),
            row((1, LANES)),
        ],
        out_specs=[
            pl.BlockSpec((tm, D_MODEL), lambda i: (i, 0)),
            pl.BlockSpec((tm, LANES), lambda i: (i, 0)),
            row((8, LANES)),
        ],
        out_shape=[
            jax.ShapeDtypeStruct((m, D_MODEL), F32),
            jax.ShapeDtypeStruct((m, LANES), F32),
            jax.ShapeDtypeStruct((8, LANES), F32),
        ],
        scratch_shapes=[pltpu.VMEM((1, LANES), F32)],
        compiler_params=_params("arbitrary"),
        name="route",
    )(x1, norm2_g, mod3, mod3, w_router, b_router)


def _experts_kernel(te_ref, nu_ref, idx_cur, idx_nxt, h2_hbm, wg_ref, wu_ref, wd_ref, y_ref,
                    xbuf, wg_b, wu_b, wd_b, sem):
    j = pl.program_id(0)
    n_used = nu_ref[0]
    tm = xbuf.shape[1]
    slot = j % 2

    def row_copy(idx_ref, r, s):
        return pltpu.make_async_copy(h2_hbm.at[pl.ds(idx_ref[0, r], 1), :], xbuf.at[s, pl.ds(r, 1), :], sem.at[s])

    def wait_tile(s):
        pltpu.make_async_copy(h2_hbm.at[pl.ds(0, tm), :], xbuf.at[s], sem.at[s]).wait()

    @pl.when(j == 0)
    def _():
        def body(r, carry):
            row_copy(idx_cur, r, 0).start()
            return carry

        lax.fori_loop(0, tm, body, 0)

    @pl.when((j < n_used) & ((j == 0) | (te_ref[j] != te_ref[jnp.maximum(j - 1, 0)])))
    def _():
        wg_b[...] = wg_ref[...].astype(BF16)
        wu_b[...] = wu_ref[...].astype(BF16)
        wd_b[...] = wd_ref[...].astype(BF16)

    @pl.when(j < n_used)
    def _():
        wait_tile(slot)
        for r in range(tm):
            row_copy(idx_nxt, r, 1 - slot).start()
        xt = xbuf[slot].astype(BF16)
        hid = (_silu(_dot(xt, wg_b[...])) * _dot(xt, wu_b[...])).astype(BF16)
        y_ref[...] = _dot(hid, wd_b[...])

    @pl.when(j == n_used - 1)
    def _():
        wait_tile(1 - slot)

    @pl.when(j >= n_used)
    def _():
        y_ref[...] = jnp.zeros_like(y_ref)


def _experts_call(tile_expert, n_used, src_tok, h2, wg, wu, wd, *, tm):
    n_tiles = tile_expert.shape[0]
    wspec = lambda shape: pl.BlockSpec((None,) + shape, lambda j, te, nu: (te[j], 0, 0))
    grid_spec = pltpu.PrefetchScalarGridSpec(
        num_scalar_prefetch=2,
        grid=(n_tiles,),
        in_specs=[
            pl.BlockSpec((None, 1, tm), lambda j, te, nu: (j, 0, 0), memory_space=pltpu.SMEM),
            pl.BlockSpec((None, 1, tm), lambda j, te, nu: (jnp.minimum(j + 1, n_tiles - 1), 0, 0),
                         memory_space=pltpu.SMEM),
            pl.BlockSpec(memory_space=pl.ANY),
            wspec((D_MODEL, EXPERT_FF)),
            wspec((D_MODEL, EXPERT_FF)),
            wspec((EXPERT_FF, D_MODEL)),
        ],
        out_specs=pl.BlockSpec((tm, D_MODEL), lambda j, te, nu: (j, 0)),
        scratch_shapes=[
            pltpu.VMEM((2, tm, D_MODEL), F32),
            pltpu.VMEM((D_MODEL, EXPERT_FF), BF16),
            pltpu.VMEM((D_MODEL, EXPERT_FF), BF16),
            pltpu.VMEM((EXPERT_FF, D_MODEL), BF16),
            pltpu.SemaphoreType.DMA((2,)),
        ],
    )
    return pl.pallas_call(
        _experts_kernel,
        grid_spec=grid_spec,
        out_shape=jax.ShapeDtypeStruct((n_tiles * tm, D_MODEL), F32),
        compiler_params=_params("arbitrary"),
        name="experts",
    )(tile_expert, n_used, src_tok, src_tok, h2, wg, wu, wd)


def _combine_kernel(pos_cur, pos_nxt, x1_ref, g2_ref, nfg_ref, rinfo_ref, y_hbm, o_ref, ybuf, sem):
    i = pl.program_id(0)
    tm = x1_ref.shape[0]
    slot = i % 2

    def row_copy(pos_ref, r, k, s):
        return pltpu.make_async_copy(y_hbm.at[pl.ds(pos_ref[0, 2 * r + k], 1), :], ybuf.at[s, k, pl.ds(r, 1), :],
                                     sem.at[s])

    def wait_tile(s):
        for k in range(2):
            pltpu.make_async_copy(y_hbm.at[pl.ds(0, tm), :], ybuf.at[s, k], sem.at[s]).wait()

    @pl.when(i == 0)
    def _():
        def body(r, carry):
            for k in range(2):
                row_copy(pos_cur, r, k, 0).start()
            return carry

        lax.fori_loop(0, tm, body, 0)

    wait_tile(slot)
    for r in range(tm):
        for k in range(2):
            row_copy(pos_nxt, r, k, 1 - slot).start()
    info = rinfo_ref[...]
    w1 = info[:, RINFO_W1:RINFO_W1 + 1]
    w2 = info[:, RINFO_W2:RINFO_W2 + 1]
    moe = w1 * ybuf[slot, 0] + w2 * ybuf[slot, 1]
    y = x1_ref[...] + g2_ref[...] * moe
    ms = jnp.mean(y * y, axis=-1, keepdims=True)
    o_ref[...] = y * lax.rsqrt(ms + NORM_EPS) * nfg_ref[...]

    @pl.when(i == pl.num_programs(0) - 1)
    def _():
        wait_tile(1 - slot)


def _combine_call(pos, x1, mod3, normf_g, rinfo, y_sorted, *, seq, tm):
    m = x1.shape[0]
    n_tiles = m // tm
    tiles_per_batch = seq // tm
    return pl.pallas_call(
        _combine_kernel,
        grid=(n_tiles,),
        in_specs=[
            pl.BlockSpec((None, 1, 2 * tm), lambda i: (i, 0, 0), memory_space=pltpu.SMEM),
            pl.BlockSpec((None, 1, 2 * tm), lambda i: (jnp.minimum(i + 1, n_tiles - 1), 0, 0),
                         memory_space=pltpu.SMEM),
            pl.BlockSpec((tm, D_MODEL), lambda i: (i, 0)),
            pl.BlockSpec((None, 1, D_MODEL), lambda i: (i // tiles_per_batch, 0, 5)),
            pl.BlockSpec((1, D_MODEL), lambda i: (0, 0)),
            pl.BlockSpec((tm, LANES), lambda i: (i, 0)),
            pl.BlockSpec(memory_space=pl.ANY),
        ],
        out_specs=pl.BlockSpec((tm, D_MODEL), lambda i: (i, 0)),
        out_shape=jax.ShapeDtypeStruct((m, D_MODEL), F32),
        scratch_shapes=[
            pltpu.VMEM((2, 2, tm, D_MODEL), F32),
            pltpu.SemaphoreType.DMA((2,)),
        ],
        compiler_params=_params("arbitrary"),
        name="combine",
    )(pos, pos, x1, mod3, normf_g, rinfo, y_sorted)


def _dispatch_plan(rinfo, counts, *, tm):
    n_tok = rinfo.shape[0]
    n_tiles = (2 * n_tok + N_EXPERTS * (tm - 1)) // tm + 1
    expert = rinfo[:, RINFO_E1:RINFO_E2 + 1].astype(jnp.int32)
    rank = rinfo[:, RINFO_R1:RINFO_R2 + 1].astype(jnp.int32)
    cnt = counts[0, :N_EXPERTS].astype(jnp.int32)
    tiles_e = (cnt + tm - 1) // tm
    end_tile = jnp.cumsum(tiles_e)
    start_row = (end_tile - tiles_e) * tm
    pos = start_row[expert] + rank
    n_used = end_tile[-1:]
    tile_ids = jnp.minimum(jnp.arange(n_tiles, dtype=jnp.int32), n_used - 1)
    tile_expert = jnp.sum(tile_ids[:, None] >= end_tile[None, :], axis=1).astype(jnp.int32)
    tok = jnp.repeat(jnp.arange(n_tok, dtype=jnp.int32), 2)
    src_tok = jnp.zeros((n_tiles * tm,), jnp.int32).at[pos.reshape(-1)].set(tok)
    return pos, tile_expert, n_used.astype(jnp.int32), src_tok.reshape(n_tiles, 1, tm)


def _group_lanes(fwd, bwd):
    both = jnp.concatenate([fwd.reshape(SSD_GROUPS, HEADS_PER_GROUP), bwd.reshape(SSD_GROUPS, HEADS_PER_GROUP)], axis=1)
    return jnp.pad(both, ((0, 0), (0, LANES - 2 * HEADS_PER_GROUP))).reshape(SSD_GROUPS, 1, LANES)


def kernel(x, c, ctx, c_ctx, w_mod, b_mod, norm1_g, w_in, conv_w, conv_b, dt_bias_f, dt_bias_b, a_log_f, a_log_b,
           d_skip, ssd_norm_g, cm_ln_g, cm_ln_b, w_spatial, b_spatial, w_out, norm2_g, w_router_group,
           b_router_group, w_router_expert, b_router_expert, w_exp_gate, w_exp_up, w_exp_down, normf_g):
    bsz, seq, _ = x.shape
    ctx_len = ctx.shape[1]
    i = 0

    cc = jnp.concatenate([c, c_ctx[None, :], jnp.zeros((MOD_ROWS - bsz - 1, D_MODEL), F32)], axis=0)
    mod = _modulation(cc, w_mod[i], b_mod[i][None, :])
    mod3 = mod.reshape(MOD_ROWS, 1, N_MOD * D_MODEL)

    dt0 = DT_COL0
    dt1 = dt0 + 2 * SSD_HEADS
    wi = w_in[i]
    w_main = _w_in_prep(wi)
    w_dt = wi[:, dt0:dt1].reshape(D_MODEL, 2, SSD_GROUPS, HEADS_PER_GROUP)
    w_dt = jnp.transpose(w_dt, (0, 2, 1, 3)).reshape(D_MODEL, 2 * SSD_HEADS)
    w_dt = jnp.pad(w_dt, ((0, 0), (0, LANES - 2 * SSD_HEADS))).astype(BF16)
    g1row = norm1_g[i][None, :]

    x2 = x.reshape(bsz * seq, D_MODEL)
    ctx2 = ctx.reshape(bsz * ctx_len, D_MODEL)
    tm_in = 1024
    pm_x, dt_x = _in_proj(x2, mod3, lambda t: t // (seq // tm_in), g1row, w_main, w_dt, tm_in, 1024)
    pm_c, dt_c = _in_proj(ctx2, mod3, lambda t: bsz, g1row, w_main, w_dt, bsz * ctx_len, 1024)

    dtbias_g = _group_lanes(dt_bias_f[i], dt_bias_b[i])
    arow_g = _group_lanes(-jnp.exp(a_log_f[i].astype(F32)), -jnp.exp(a_log_b[i].astype(F32)))
    dskip_g = jnp.repeat(d_skip[i], SSD_HEAD_DIM).reshape(SSD_GROUPS, 1, GROUP_WIDTH)
    cw = conv_w[i]
    cb = conv_b[i][None, :]
    h_zero = jnp.zeros((bsz, SSD_GROUPS, SSD_STATE, GROUP_WIDTH), F32)

    hc_f, hc_b = _ssd(pm_c, dt_c, cw, cb, dtbias_g, arow_g, dskip_g, h_zero, h_zero,
                      batch=bsz, seq=ctx_len, conv_tile=ctx_len, conv_period=ctx_len, with_output=False)
    yd = _ssd(pm_x, dt_x, cw, cb, dtbias_g, arow_g, dskip_g, hc_f, hc_b,
              batch=bsz, seq=seq, conv_tile=CHUNK, conv_period=GRID_W, with_output=True)

    x1 = _out_proj(yd, pm_x, x2, mod3, ssd_norm_g[i][None, :], cm_ln_g[i][None, :], cm_ln_b[i][None, :],
                   w_spatial[i].astype(BF16), b_spatial[i].T, w_out[i].astype(BF16), seq=seq, tm=512, tn=1024)

    w_re = jnp.transpose(w_router_expert[i], (1, 0, 2)).reshape(D_MODEL, N_EXPERTS)
    pad = LANES - N_EXPERTS - N_GROUPS
    w_router = jnp.pad(jnp.concatenate([w_re, w_router_group[i]], axis=1), ((0, 0), (0, pad)))
    b_router = jnp.pad(jnp.concatenate([b_router_expert[i].reshape(-1), b_router_group[i]]), (0, pad))[None, :]
    h2, rinfo, counts = _route_call(x1, mod3, norm2_g[i][None, :], w_router, b_router, seq=seq, tm=512)

    tm_e = 256
    pos, tile_expert, n_used, src_tok = _dispatch_plan(rinfo, counts, tm=tm_e)
    wg = w_exp_gate[i].reshape(N_EXPERTS, D_MODEL, EXPERT_FF)
    wu = w_exp_up[i].reshape(N_EXPERTS, D_MODEL, EXPERT_FF)
    wd = w_exp_down[i].reshape(N_EXPERTS, EXPERT_FF, D_MODEL)
    y_sorted = _experts_call(tile_expert, n_used, src_tok, h2, wg, wu, wd, tm=tm_e)
    tm_c = 256
    out = _combine_call(pos.reshape(bsz * seq // tm_c, 1, 2 * tm_c), x1, mod3, normf_g[None, :], rinfo, y_sorted,
                        seq=seq, tm=tm_c)
    return out.reshape(bsz, seq, D_MODEL)
```

```python
import functools

import jax
import jax.numpy as jnp
from jax import lax
from jax.experimental import pallas as pl
from jax.experimental.pallas import tpu as pltpu

F32 = jnp.float32
BF16 = jnp.bfloat16
HIGHEST = lax.Precision.HIGHEST

D_MODEL = 2048
GRID_W = 64
SSD_WIDTH = 2048
CM_WIDTH = 2048
SSD_HEADS = 32
SSD_HEAD_DIM = 64
SSD_GROUPS = 4
HEADS_PER_GROUP = SSD_HEADS // SSD_GROUPS
GROUP_WIDTH = HEADS_PER_GROUP * SSD_HEAD_DIM
SSD_STATE = 128
CHUNK = 128
BC_WIDTH = SSD_GROUPS * SSD_STATE
CM_HEADS = 8
CM_HEAD_DIM = CM_WIDTH // CM_HEADS
N_GROUPS = 4
N_EXPERTS = 32
EXPERTS_PER_GROUP = 8
EXPERT_FF = 512
N_MOD = 6
NORM_EPS = 1e-6
LANES = 128
NEG_BIG = -1e30
MOD_ROWS = 8
VMEM_LIMIT = 56 * 1024 * 1024


def _params(*sem):
    return pltpu.CompilerParams(dimension_semantics=sem, vmem_limit_bytes=VMEM_LIMIT)


def _silu(v):
    return v * jax.nn.sigmoid(v)


def _dot(a, b):
    return jnp.dot(a, b, preferred_element_type=F32)


def _mod_kernel(cc_ref, w_ref, b_ref, o_ref):
    a = _silu(cc_ref[...])
    hi = a.astype(BF16).astype(F32)
    lhs = jnp.concatenate([hi, a - hi], axis=0).astype(BF16)
    r = _dot(lhs, w_ref[...].astype(BF16))
    o_ref[...] = r[:MOD_ROWS] + r[MOD_ROWS:] + b_ref[...]


def _modulation(cc, w_mod, b_mod):
    n = w_mod.shape[1]
    tn = 1024
    return pl.pallas_call(
        _mod_kernel,
        grid=(n // tn,),
        in_specs=[
            pl.BlockSpec((MOD_ROWS, D_MODEL), lambda j: (0, 0)),
            pl.BlockSpec((D_MODEL, tn), lambda j: (0, j)),
            pl.BlockSpec((1, tn), lambda j: (0, j)),
        ],
        out_specs=pl.BlockSpec((MOD_ROWS, tn), lambda j: (0, j)),
        out_shape=jax.ShapeDtypeStruct((MOD_ROWS, n), F32),
        compiler_params=_params("arbitrary"),
        name="modulation",
    )(cc, w_mod, b_mod)


PREP_COLS = 512
PREP_UV_BLK0 = 2 * SSD_WIDTH // PREP_COLS
PREP_BC_BLK0 = (2 * SSD_WIDTH + 2 * CM_WIDTH) // PREP_COLS
PREP_N_BLKS = PREP_BC_BLK0 + 2 * BC_WIDTH // PREP_COLS
DT_COL0 = 2 * SSD_WIDTH + 2 * BC_WIDTH
UV_LANE_SHIFT = 2 * SSD_HEADS


def _w_in_prep_kernel(a_ref, nxt_ref, o_ref, wdt_ref):
    j = pl.program_id(0)
    shifted = (j >= PREP_UV_BLK0) & (j < PREP_BC_BLK0)
    rc = 256

    @pl.when(j == PREP_UV_BLK0)
    def _():
        wdt_ref[...] = a_ref[:, 0:LANES].astype(BF16)

    @pl.when(jnp.logical_not(shifted))
    def _():
        o_ref[...] = a_ref[...].astype(BF16)

    @pl.when(shifted)
    def _():
        low = lax.broadcasted_iota(jnp.int32, (rc, LANES), 1) < LANES - UV_LANE_SHIFT

        def body(k, carry):
            r = pl.ds(pl.multiple_of(k * rc, rc), rc)
            tiles = [a_ref[r, t * LANES:(t + 1) * LANES] for t in range(PREP_COLS // LANES)] + [nxt_ref[r, :]]
            rolled = [pltpu.roll(t, LANES - UV_LANE_SHIFT, 1) for t in tiles]
            for t in range(PREP_COLS // LANES):
                o_ref[r, t * LANES:(t + 1) * LANES] = jnp.where(low, rolled[t], rolled[t + 1]).astype(BF16)
            return carry

        lax.fori_loop(0, D_MODEL // rc, body, 0)


def _w_in_prep(w_in):
    uv_src0 = DT_COL0 // PREP_COLS
    bc_src0 = 2 * SSD_WIDTH // PREP_COLS

    def src_blk(j):
        return jnp.where(j < PREP_UV_BLK0, j, jnp.where(j < PREP_BC_BLK0, j - PREP_UV_BLK0 + uv_src0,
                                                        j - PREP_BC_BLK0 + bc_src0))

    def nxt_blk(j):
        uv = jnp.clip(j - PREP_UV_BLK0, 0, PREP_BC_BLK0 - PREP_UV_BLK0 - 1)
        return (DT_COL0 + (uv + 1) * PREP_COLS) // LANES

    return pl.pallas_call(
        _w_in_prep_kernel,
        grid=(PREP_N_BLKS,),
        in_specs=[
            pl.BlockSpec((D_MODEL, PREP_COLS), lambda j: (0, src_blk(j))),
            pl.BlockSpec((D_MODEL, LANES), lambda j: (0, nxt_blk(j))),
        ],
        out_specs=[
            pl.BlockSpec((D_MODEL, PREP_COLS), lambda j: (0, j)),
            pl.BlockSpec((D_MODEL, LANES), lambda j: (0, 0)),
        ],
        out_shape=[
            jax.ShapeDtypeStruct((D_MODEL, PREP_N_BLKS * PREP_COLS), BF16),
            jax.ShapeDtypeStruct((D_MODEL, LANES), BF16),
        ],
        compiler_params=_params("arbitrary"),
        name="w_in_prep",
    )(w_in, w_in)


def _in_proj_kernel(x_ref, g_ref, sh_ref, sc_ref, w_ref, wdt_ref, o_ref, dt_ref, h_scr):
    tm = x_ref.shape[0]
    rc = 256

    @pl.when(pl.program_id(1) == 0)
    def _():
        gain = g_ref[...]
        scale = 1.0 + sc_ref[...]
        shift = sh_ref[...]

        def body(k, carry):
            r = pl.ds(pl.multiple_of(k * rc, rc), rc)
            xv = x_ref[r, :]
            ms = jnp.mean(xv * xv, axis=-1, keepdims=True)
            h = (xv * lax.rsqrt(ms + NORM_EPS) * gain) * scale + shift
            h_scr[r, :] = h.astype(BF16)
            return carry

        lax.fori_loop(0, tm // rc, body, 0)
        dt_ref[...] = _dot(h_scr[...], wdt_ref[...])

    o_ref[...] = _dot(h_scr[...], w_ref[...]).astype(o_ref.dtype)


def _in_proj(x2, mod3, mod_row_of_tile, norm_g, w_main, w_dt, tm, tn):
    m = x2.shape[0]
    n = w_main.shape[1]
    return pl.pallas_call(
        _in_proj_kernel,
        grid=(m // tm, n // tn),
        in_specs=[
            pl.BlockSpec((tm, D_MODEL), lambda i, j: (i, 0)),
            pl.BlockSpec((1, D_MODEL), lambda i, j: (0, 0)),
            pl.BlockSpec((None, 1, D_MODEL), lambda i, j: (mod_row_of_tile(i), 0, 0)),
            pl.BlockSpec((None, 1, D_MODEL), lambda i, j: (mod_row_of_tile(i), 0, 1)),
            pl.BlockSpec((D_MODEL, tn), lambda i, j: (0, j)),
            pl.BlockSpec((D_MODEL, LANES), lambda i, j: (0, 0)),
        ],
        out_specs=[
            pl.BlockSpec((tm, tn), lambda i, j: (i, j)),
            pl.BlockSpec((tm, LANES), lambda i, j: (i, 0)),
        ],
        out_shape=[
            jax.ShapeDtypeStruct((m, n), BF16),
            jax.ShapeDtypeStruct((m, LANES), F32),
        ],
        scratch_shapes=[pltpu.VMEM((tm, D_MODEL), BF16)],
        compiler_params=_params("parallel", "arbitrary"),
        name="in_proj",
    )(x2, norm_g, mod3, mod3, w_main, w_dt)


def _split3(v):
    hi = v.astype(BF16)
    rem = v - hi.astype(F32)
    mid = rem.astype(BF16)
    lo = (rem - mid.astype(F32)).astype(BF16)
    return hi, mid, lo


def _expand_heads(v, sel):
    hi, mid, lo = _split3(v)
    return _dot(hi, sel) + _dot(mid, sel) + _dot(lo, sel)


def _ssd_kernel(xs_ref, b_ref, c_ref, dt_ref, cwx_ref, cbx_ref, cwb_ref, cbb_ref, cwc_ref, cbc_ref,
                dtbias_ref, arow_ref, dskip_ref, h0f_ref, h0b_ref, *rest,
                n_chunks, conv_tile, conv_period, with_output):
    if with_output:
        y_ref, xs_s, b_s, c_s, cum_s, dts_s, hf_s, hb_s, y_s = rest
    else:
        hf_out, hb_out, xs_s, b_s, c_s, cum_s, dts_s, hf_s, hb_s = rest
    seq = n_chunks * CHUNK
    hpg = HEADS_PER_GROUP

    def conv_silu(src_ref, dst_ref, w_ref, bias_ref):
        cols = src_ref.shape[1]
        w0, w1, w2 = w_ref[0:1, :], w_ref[1:2, :], w_ref[2:3, :]
        bias = bias_ref[...]
        row = lax.broadcasted_iota(jnp.int32, (conv_tile, cols), 0) % conv_period
        first = row == 0
        last = row == conv_period - 1

        def body(k, carry):
            r = pl.ds(pl.multiple_of(k * conv_tile, conv_tile), conv_tile)
            v = src_ref[r, :].astype(F32)
            prev = jnp.where(first, 0.0, pltpu.roll(v, 1, 0))
            nxt = jnp.where(last, 0.0, pltpu.roll(v, conv_tile - 1, 0))
            y = bias + prev * w0 + v * w1 + nxt * w2
            dst_ref[r, :] = _silu(y).astype(BF16)
            return carry

        lax.fori_loop(0, seq // conv_tile, body, 0)

    conv_silu(xs_ref, xs_s, cwx_ref, cbx_ref)
    conv_silu(b_ref, b_s, cwb_ref, cbb_ref)
    conv_silu(c_ref, c_s, cwc_ref, cbc_ref)

    ii = lax.broadcasted_iota(jnp.int32, (CHUNK, CHUNK), 0)
    jj = lax.broadcasted_iota(jnp.int32, (CHUNK, CHUNK), 1)
    lower = ii >= jj
    upper = jj >= ii
    tri_lo = lower.astype(F32)
    tri_up = upper.astype(F32)
    arow = arow_ref[...]
    dtbias = dtbias_ref[...]
    first_head = hpg * pl.program_id(1)
    shift_f = (LANES - first_head) % LANES
    shift_b = (2 * LANES - (SSD_HEADS + first_head - hpg)) % LANES

    def dt_body(k, carry):
        r = pl.ds(pl.multiple_of(k * CHUNK, CHUNK), CHUNK)
        dt_all = dt_ref[r, :]
        raw = jnp.where(jj < hpg, pltpu.roll(dt_all, shift_f, 1), pltpu.roll(dt_all, shift_b, 1)) + dtbias
        dts = jnp.maximum(raw, 0.0) + jnp.log1p(jnp.exp(-jnp.abs(raw)))
        adt = dts * arow
        cf = jnp.dot(tri_lo, adt, precision=HIGHEST, preferred_element_type=F32)
        cr = jnp.dot(tri_up, adt, precision=HIGHEST, preferred_element_type=F32)
        dts_s[r, :] = dts
        cum_s[r, :] = jnp.where(jj < hpg, cf, cr)
        return carry

    lax.fori_loop(0, n_chunks, dt_body, 0)

    hf_s[...] = h0f_ref[...]
    hb_s[...] = h0b_ref[...]

    head_of_col = lax.broadcasted_iota(jnp.int32, (LANES, GROUP_WIDTH), 1) // SSD_HEAD_DIM
    sel_row = lax.broadcasted_iota(jnp.int32, (LANES, GROUP_WIDTH), 0)
    sel_f = (sel_row == head_of_col).astype(BF16)
    sel_b = (sel_row == head_of_col + hpg).astype(BF16)
    low_half = jj < SSD_HEAD_DIM

    def direction_terms(r, sel):
        cum = cum_s[r, :]
        dts = dts_s[r, :]
        tot = jnp.sum(dts * arow, axis=0, keepdims=True)
        wst = dts * jnp.exp(tot - cum)
        eoff = jnp.exp(cum)
        edec = jnp.broadcast_to(jnp.exp(tot), (8, LANES))
        ex = _expand_heads(jnp.concatenate([wst, eoff, edec], axis=0), sel)
        return cum, dts, ex[:CHUNK], ex[CHUNK:2 * CHUNK], ex[2 * CHUNK:2 * CHUNK + 1]

    def state_step(h_s, r, sel):
        cum, dts, w_state, e_off, e_dec = direction_terms(r, sel)
        xc = xs_s[r, :]
        xw = (xc.astype(F32) * w_state).astype(BF16)
        s_new = lax.dot_general(b_s[r, :], xw, (((0,), (0,)), ((), ())), preferred_element_type=F32)
        h = h_s[...]
        y_off = _dot(c_s[r, :], h.astype(BF16)) * e_off
        h_s[...] = h * e_dec + s_new
        return cum, dts, xc, y_off

    def fwd_body(k, carry):
        r = pl.ds(pl.multiple_of(k * CHUNK, CHUNK), CHUNK)
        cum, dts, xc, y_off = state_step(hf_s, r, sel_f)
        if with_output:
            g = lax.dot_general(c_s[r, :], b_s[r, :], (((1,), (1,)), ((), ())), preferred_element_type=F32)
            cum_t = cum.T
            dts_t = dts.T
            pieces = []
            for pair in range(hpg // 2):
                ms = []
                for h in (2 * pair, 2 * pair + 1):
                    hb = h + hpg
                    mf = jnp.exp(jnp.where(lower, cum[:, h:h + 1] - cum_t[h:h + 1, :], NEG_BIG)) * dts_t[h:h + 1, :]
                    mb = jnp.exp(jnp.where(upper, cum[:, hb:hb + 1] - cum_t[hb:hb + 1, :], NEG_BIG)) * dts_t[hb:hb + 1, :]
                    ms.append((g * (mf + mb)).astype(BF16))
                lhs = jnp.concatenate(ms, axis=1)
                xp = xc[:, pair * LANES:(pair + 1) * LANES]
                zero = jnp.zeros_like(xp)
                rhs = jnp.concatenate([jnp.where(low_half, xp, zero), jnp.where(low_half, zero, xp)], axis=0)
                pieces.append(_dot(lhs, rhs))
            y_diag = jnp.concatenate(pieces, axis=1)
            y_s[r, :] = y_diag + y_off + xc.astype(F32) * dskip_ref[...]
        return carry

    lax.fori_loop(0, n_chunks, fwd_body, 0)

    def bwd_body(k, carry):
        r = pl.ds(pl.multiple_of((n_chunks - 1 - k) * CHUNK, CHUNK), CHUNK)
        _, _, _, y_off = state_step(hb_s, r, sel_b)
        if with_output:
            y_ref[r, :] = (y_s[r, :] + y_off).astype(y_ref.dtype)
        return carry

    lax.fori_loop(0, n_chunks, bwd_body, 0)

    if not with_output:
        hf_out[...] = hf_s[...]
        hb_out[...] = hb_s[...]


def _ssd(pm, dtg, conv_w, conv_b, dtbias_g, arow_g, dskip_g, h0f, h0b, *, batch, seq, conv_tile, conv_period,
         with_output):
    n_chunks = seq // CHUNK
    xs_blk0 = SSD_WIDTH // GROUP_WIDTH
    b_blk0 = (2 * SSD_WIDTH + 2 * CM_WIDTH) // SSD_STATE
    c_blk0 = b_blk0 + SSD_GROUPS
    cw_b0 = SSD_WIDTH // SSD_STATE
    cw_c0 = cw_b0 + SSD_GROUPS
    state_spec = pl.BlockSpec((None, None, SSD_STATE, GROUP_WIDTH), lambda b, g: (b, g, 0, 0))
    in_specs = [
        pl.BlockSpec((seq, GROUP_WIDTH), lambda b, g: (b, xs_blk0 + g)),
        pl.BlockSpec((seq, SSD_STATE), lambda b, g: (b, b_blk0 + g)),
        pl.BlockSpec((seq, SSD_STATE), lambda b, g: (b, c_blk0 + g)),
        pl.BlockSpec((seq, LANES), lambda b, g: (b, 0)),
        pl.BlockSpec((3, GROUP_WIDTH), lambda b, g: (0, g)),
        pl.BlockSpec((1, GROUP_WIDTH), lambda b, g: (0, g)),
        pl.BlockSpec((3, SSD_STATE), lambda b, g: (0, cw_b0 + g)),
        pl.BlockSpec((1, SSD_STATE), lambda b, g: (0, cw_b0 + g)),
        pl.BlockSpec((3, SSD_STATE), lambda b, g: (0, cw_c0 + g)),
        pl.BlockSpec((1, SSD_STATE), lambda b, g: (0, cw_c0 + g)),
        pl.BlockSpec((None, 1, LANES), lambda b, g: (g, 0, 0)),
        pl.BlockSpec((None, 1, LANES), lambda b, g: (g, 0, 0)),
        pl.BlockSpec((None, 1, GROUP_WIDTH), lambda b, g: (g, 0, 0)),
        state_spec,
        state_spec,
    ]
    scratch = [
        pltpu.VMEM((seq, GROUP_WIDTH), BF16),
        pltpu.VMEM((seq, SSD_STATE), BF16),
        pltpu.VMEM((seq, SSD_STATE), BF16),
        pltpu.VMEM((seq, LANES), F32),
        pltpu.VMEM((seq, LANES), F32),
        pltpu.VMEM((SSD_STATE, GROUP_WIDTH), F32),
        pltpu.VMEM((SSD_STATE, GROUP_WIDTH), F32),
    ]
    if with_output:
        out_specs = pl.BlockSpec((seq, GROUP_WIDTH), lambda b, g: (b, g))
        out_shape = jax.ShapeDtypeStruct((batch * seq, SSD_WIDTH), BF16)
        scratch.append(pltpu.VMEM((seq, GROUP_WIDTH), F32))
    else:
        out_specs = [state_spec, state_spec]
        out_shape = [jax.ShapeDtypeStruct((batch, SSD_GROUPS, SSD_STATE, GROUP_WIDTH), F32)] * 2
    return pl.pallas_call(
        functools.partial(_ssd_kernel, n_chunks=n_chunks, conv_tile=conv_tile, conv_period=conv_period,
                          with_output=with_output),
        grid=(batch, SSD_GROUPS),
        in_specs=in_specs,
        out_specs=out_specs,
        out_shape=out_shape,
        scratch_shapes=scratch,
        compiler_params=_params("parallel", "parallel"),
        name="ssd_out" if with_output else "ssd_ctx",
    )(pm, pm, pm, dtg, conv_w, conv_b, conv_w, conv_b, conv_w, conv_b, dtbias_g, arow_g, dskip_g, h0f, h0b)


def _out_proj_kernel(yd_ref, z_ref, u_ref, v_ref, x_ref, g1_ref, ng_ref, lng_ref, lnb_ref, ws_ref, bst_ref, w_ref,
                     o_ref, mix_scr):
    tm = yd_ref.shape[0]

    @pl.when(pl.program_id(1) == 0)
    def _():
        def body(k, carry):
            r = pl.ds(pl.multiple_of(k * CHUNK, CHUNK), CHUNK)
            a = yd_ref[r, :].astype(F32) * _silu(z_ref[r, :].astype(F32))
            ms = jnp.mean(a * a, axis=-1, keepdims=True)
            mix_scr[r, 0:SSD_WIDTH] = (a * lax.rsqrt(ms + NORM_EPS) * ng_ref[...]).astype(BF16)
            gv = jax.nn.gelu(v_ref[r, :].astype(F32))
            mu = jnp.mean(gv, axis=-1, keepdims=True)
            xc = gv - mu
            var = jnp.mean(xc * xc, axis=-1, keepdims=True)
            ln = ((xc * lax.rsqrt(var + NORM_EPS)) * lng_ref[...] + lnb_ref[...]).astype(BF16)
            gu = jax.nn.gelu(u_ref[r, :].astype(F32))
            for h in range(CM_HEADS):
                c0, c1 = h * CM_HEAD_DIM, (h + 1) * CM_HEAD_DIM
                s = _dot(ws_ref[h], ln[:, c0:c1]) + bst_ref[:, h:h + 1]
                mix_scr[r, SSD_WIDTH + c0:SSD_WIDTH + c1] = (gu[:, c0:c1] * s).astype(BF16)
            return carry

        lax.fori_loop(0, tm // CHUNK, body, 0)

    o_ref[...] = x_ref[...] + g1_ref[...] * _dot(mix_scr[...], w_ref[...])


def _out_proj(yd, pm, x2, mod3, ssd_norm_g, ln_g, ln_b, ws, bst, w_out, *, seq, tm, tn):
    m = x2.shape[0]
    tiles_per_batch = seq // tm
    z_blk, u_blk, v_blk = 0, 2 * SSD_WIDTH // CM_WIDTH, 2 * SSD_WIDTH // CM_WIDTH + 1
    g1_blk0 = 2 * D_MODEL // tn
    row = lambda shape: pl.BlockSpec(shape, lambda i, j: (0, 0))
    return pl.pallas_call(
        _out_proj_kernel,
        grid=(m // tm, D_MODEL // tn),
        in_specs=[
            pl.BlockSpec((tm, SSD_WIDTH), lambda i, j: (i, 0)),
            pl.BlockSpec((tm, SSD_WIDTH), lambda i, j: (i, z_blk)),
            pl.BlockSpec((tm, CM_WIDTH), lambda i, j: (i, u_blk)),
            pl.BlockSpec((tm, CM_WIDTH), lambda i, j: (i, v_blk)),
            pl.BlockSpec((tm, tn), lambda i, j: (i, j)),
            pl.BlockSpec((None, 1, tn), lambda i, j: (i // tiles_per_batch, 0, g1_blk0 + j)),
            row((1, SSD_WIDTH)),
            row((1, CM_WIDTH)),
            row((1, CM_WIDTH)),
            pl.BlockSpec((CM_HEADS, CHUNK, CHUNK), lambda i, j: (0, 0, 0)),
            row((CHUNK, CM_HEADS)),
            pl.BlockSpec((SSD_WIDTH + CM_WIDTH, tn), lambda i, j: (0, j)),
        ],
        out_specs=pl.BlockSpec((tm, tn), lambda i, j: (i, j)),
        out_shape=jax.ShapeDtypeStruct((m, D_MODEL), F32),
        scratch_shapes=[pltpu.VMEM((tm, SSD_WIDTH + CM_WIDTH), BF16)],
        compiler_params=_params("parallel", "arbitrary"),
        name="out_proj",
    )(yd, pm, pm, pm, x2, mod3, ssd_norm_g, ln_g, ln_b, ws, bst, w_out)


def _route(logits):
    lane = lax.broadcasted_iota(jnp.int32, logits.shape, 1)
    lane_f = lane.astype(F32)
    is_group = (lane >= N_EXPERTS) & (lane < N_EXPERTS + N_GROUPS)
    lg = jnp.where(is_group, logits, NEG_BIG)
    mg = jnp.max(lg, axis=1, keepdims=True)
    top_pg = 1.0 / jnp.sum(jnp.exp(lg - mg), axis=1, keepdims=True)
    gi = jnp.min(jnp.where(lg == mg, lane_f, 1e9), axis=1, keepdims=True) - N_EXPERTS
    in_group = (lane < N_EXPERTS) & ((lane // EXPERTS_PER_GROUP).astype(F32) == gi)
    le = jnp.where(in_group, logits, NEG_BIG)
    m1 = jnp.max(le, axis=1, keepdims=True)
    i1 = jnp.min(jnp.where(le == m1, lane_f, 1e9), axis=1, keepdims=True)
    le2 = jnp.where(lane_f == i1, NEG_BIG, le)
    m2 = jnp.max(le2, axis=1, keepdims=True)
    i2 = jnp.min(jnp.where(le2 == m2, lane_f, 1e9), axis=1, keepdims=True)
    e2 = jnp.exp(m2 - m1)
    p1 = 1.0 / (1.0 + e2)
    p2 = e2 * p1
    return i1, i2, p1 * top_pg, p2 * top_pg


RINFO_E1, RINFO_E2, RINFO_W1, RINFO_W2, RINFO_R1, RINFO_R2 = range(6)


def _route_kernel(x1_ref, n2g_ref, sh2_ref, sc2_ref, wr_ref, br_ref, h2_ref, rinfo_ref, counts_ref, cnt_scr):
    tm = x1_ref.shape[0]

    @pl.when(pl.program_id(0) == 0)
    def _():
        cnt_scr[...] = jnp.zeros_like(cnt_scr)

    x1 = x1_ref[...]
    ms = jnp.mean(x1 * x1, axis=-1, keepdims=True)
    h2 = (x1 * lax.rsqrt(ms + NORM_EPS) * n2g_ref[...]) * (1.0 + sc2_ref[...]) + sh2_ref[...]
    h2_ref[...] = h2
    logits = jnp.dot(h2, wr_ref[...], precision=HIGHEST, preferred_element_type=F32) + br_ref[...]
    i1, i2, w1, w2 = _route(logits)

    lane = lax.broadcasted_iota(jnp.int32, (tm, LANES), 1)
    lane_f = lane.astype(F32)
    oh1 = jnp.where(lane_f == i1, 1.0, 0.0)
    oh2 = jnp.where(lane_f == i2, 1.0, 0.0)
    before = (lax.broadcasted_iota(jnp.int32, (tm, tm), 0) > lax.broadcasted_iota(jnp.int32, (tm, tm), 1)).astype(BF16)
    carried = cnt_scr[...]
    tot1 = jnp.sum(oh1, axis=0, keepdims=True)
    r1 = jnp.sum(oh1 * (_dot(before, oh1.astype(BF16)) + carried), axis=1, keepdims=True)
    r2 = jnp.sum(oh2 * (_dot(before, oh2.astype(BF16)) + (carried + tot1)), axis=1, keepdims=True)
    counts = carried + tot1 + jnp.sum(oh2, axis=0, keepdims=True)
    cnt_scr[...] = counts
    counts_ref[...] = jnp.broadcast_to(counts, counts_ref.shape)

    info = jnp.zeros((tm, LANES), F32)
    for k, v in ((RINFO_E1, i1), (RINFO_E2, i2), (RINFO_W1, w1), (RINFO_W2, w2), (RINFO_R1, r1), (RINFO_R2, r2)):
        info = jnp.where(lane == k, v, info)
    rinfo_ref[...] = info


def _route_call(x1, mod3, norm2_g, w_router, b_router, *, seq, tm):
    m = x1.shape[0]
    tiles_per_batch = seq // tm
    row = lambda shape: pl.BlockSpec(shape, lambda i: (0, 0))
    modrow = lambda k: pl.BlockSpec((None, 1, D_MODEL), lambda i: (i // tiles_per_batch, 0, k))
    return pl.pallas_call(
        _route_kernel,
        grid=(m // tm,),
        in_specs=[
            pl.BlockSpec((tm, D_MODEL), lambda i: (i, 0)),
            row((1, D_MODEL)),
            modrow(3),
            modrow(4),
            row((D_MODEL, LANES)),
            row((1, LANES)),
        ],
        out_specs=[
            pl.BlockSpec((tm, D_MODEL), lambda i: (i, 0)),
            pl.BlockSpec((tm, LANES), lambda i: (i, 0)),
            row((8, LANES)),
        ],
        out_shape=[
            jax.ShapeDtypeStruct((m, D_MODEL), F32),
            jax.ShapeDtypeStruct((m, LANES), F32),
            jax.ShapeDtypeStruct((8, LANES), F32),
        ],
        scratch_shapes=[pltpu.VMEM((1, LANES), F32)],
        compiler_params=_params("arbitrary"),
        name="route",
    )(x1, norm2_g, mod3, mod3, w_router, b_router)


def _experts_kernel(te_ref, nu_ref, idx_cur, idx_nxt, h2_hbm, wg_ref, wu_ref, wd_ref, y_ref,
                    xbuf, wg_b, wu_b, wd_b, sem):
    j = pl.program_id(0)
    n_used = nu_ref[0]
    tm = xbuf.shape[1]
    slot = j % 2

    def row_copy(idx_ref, r, s):
        return pltpu.make_async_copy(h2_hbm.at[pl.ds(idx_ref[0, r], 1), :], xbuf.at[s, pl.ds(r, 1), :], sem.at[s])

    def wait_tile(s):
        pltpu.make_async_copy(h2_hbm.at[pl.ds(0, tm), :], xbuf.at[s], sem.at[s]).wait()

    @pl.when(j == 0)
    def _():
        def body(r, carry):
            row_copy(idx_cur, r, 0).start()
            return carry

        lax.fori_loop(0, tm, body, 0)

    @pl.when((j < n_used) & ((j == 0) | (te_ref[j] != te_ref[jnp.maximum(j - 1, 0)])))
    def _():
        wg_b[...] = wg_ref[...].astype(BF16)
        wu_b[...] = wu_ref[...].astype(BF16)
        wd_b[...] = wd_ref[...].astype(BF16)

    @pl.when(j < n_used)
    def _():
        wait_tile(slot)
        for r in range(tm):
            row_copy(idx_nxt, r, 1 - slot).start(priority=r % 2)
        xt = xbuf[slot].astype(BF16)
        hid = (_silu(_dot(xt, wg_b[...])) * _dot(xt, wu_b[...])).astype(BF16)
        y_ref[...] = _dot(hid, wd_b[...])

    @pl.when(j == n_used - 1)
    def _():
        wait_tile(1 - slot)

    @pl.when(j >= n_used)
    def _():
        y_ref[...] = jnp.zeros_like(y_ref)


def _experts_call(tile_expert, n_used, src_tok, h2, wg, wu, wd, *, tm):
    n_tiles = tile_expert.shape[0]
    wspec = lambda shape: pl.BlockSpec((None,) + shape, lambda j, te, nu: (te[j], 0, 0))
    grid_spec = pltpu.PrefetchScalarGridSpec(
        num_scalar_prefetch=2,
        grid=(n_tiles,),
        in_specs=[
            pl.BlockSpec((None, 1, tm), lambda j, te, nu: (j, 0, 0), memory_space=pltpu.SMEM),
            pl.BlockSpec((None, 1, tm), lambda j, te, nu: (jnp.minimum(j + 1, n_tiles - 1), 0, 0),
                         memory_space=pltpu.SMEM),
            pl.BlockSpec(memory_space=pl.ANY),
            wspec((D_MODEL, EXPERT_FF)),
            wspec((D_MODEL, EXPERT_FF)),
            wspec((EXPERT_FF, D_MODEL)),
        ],
        out_specs=pl.BlockSpec((tm, D_MODEL), lambda j, te, nu: (j, 0)),
        scratch_shapes=[
            pltpu.VMEM((2, tm, D_MODEL), F32),
            pltpu.VMEM((D_MODEL, EXPERT_FF), BF16),
            pltpu.VMEM((D_MODEL, EXPERT_FF), BF16),
            pltpu.VMEM((EXPERT_FF, D_MODEL), BF16),
            pltpu.SemaphoreType.DMA((2,)),
        ],
    )
    return pl.pallas_call(
        _experts_kernel,
        grid_spec=grid_spec,
        out_shape=jax.ShapeDtypeStruct((n_tiles * tm, D_MODEL), F32),
        compiler_params=_params("arbitrary"),
        name="experts",
    )(tile_expert, n_used, src_tok, src_tok, h2, wg, wu, wd)


def _combine_kernel(pos_cur, pos_nxt, x1_ref, g2_ref, nfg_ref, rinfo_ref, y_hbm, o_ref, ybuf, sem):
    i = pl.program_id(0)
    tm = x1_ref.shape[0]
    slot = i % 2

    def row_copy(pos_ref, r, k, s):
        return pltpu.make_async_copy(y_hbm.at[pl.ds(pos_ref[0, 2 * r + k], 1), :], ybuf.at[s, k, pl.ds(r, 1), :],
                                     sem.at[s])

    def wait_tile(s):
        for k in range(2):
            pltpu.make_async_copy(y_hbm.at[pl.ds(0, tm), :], ybuf.at[s, k], sem.at[s]).wait()

    @pl.when(i == 0)
    def _():
        def body(r, carry):
            for k in range(2):
                row_copy(pos_cur, r, k, 0).start()
            return carry

        lax.fori_loop(0, tm, body, 0)

    wait_tile(slot)
    for r in range(tm):
        for k in range(2):
            row_copy(pos_nxt, r, k, 1 - slot).start(priority=k)
    info = rinfo_ref[...]
    w1 = info[:, RINFO_W1:RINFO_W1 + 1]
    w2 = info[:, RINFO_W2:RINFO_W2 + 1]
    moe = w1 * ybuf[slot, 0] + w2 * ybuf[slot, 1]
    y = x1_ref[...] + g2_ref[...] * moe
    ms = jnp.mean(y * y, axis=-1, keepdims=True)
    o_ref[...] = y * lax.rsqrt(ms + NORM_EPS) * nfg_ref[...]

    @pl.when(i == pl.num_programs(0) - 1)
    def _():
        wait_tile(1 - slot)


def _combine_call(pos, x1, mod3, normf_g, rinfo, y_sorted, *, seq, tm):
    m = x1.shape[0]
    n_tiles = m // tm
    tiles_per_batch = seq // tm
    return pl.pallas_call(
        _combine_kernel,
        grid=(n_tiles,),
        in_specs=[
            pl.BlockSpec((None, 1, 2 * tm), lambda i: (i, 0, 0), memory_space=pltpu.SMEM),
            pl.BlockSpec((None, 1, 2 * tm), lambda i: (jnp.minimum(i + 1, n_tiles - 1), 0, 0),
                         memory_space=pltpu.SMEM),
            pl.BlockSpec((tm, D_MODEL), lambda i: (i, 0)),
            pl.BlockSpec((None, 1, D_MODEL), lambda i: (i // tiles_per_batch, 0, 5)),
            pl.BlockSpec((1, D_MODEL), lambda i: (0, 0)),
            pl.BlockSpec((tm, LANES), lambda i: (i, 0)),
            pl.BlockSpec(memory_space=pl.ANY),
        ],
        out_specs=pl.BlockSpec((tm, D_MODEL), lambda i: (i, 0)),
        out_shape=jax.ShapeDtypeStruct((m, D_MODEL), F32),
        scratch_shapes=[
            pltpu.VMEM((2, 2, tm, D_MODEL), F32),
            pltpu.SemaphoreType.DMA((2,)),
        ],
        compiler_params=_params("arbitrary"),
        name="combine",
    )(pos, pos, x1, mod3, normf_g, rinfo, y_sorted)


def _dispatch_plan(rinfo, counts, *, tm):
    n_tok = rinfo.shape[0]
    n_tiles = (2 * n_tok + N_EXPERTS * (tm - 1)) // tm + 1
    expert = rinfo[:, RINFO_E1:RINFO_E2 + 1].astype(jnp.int32)
    rank = rinfo[:, RINFO_R1:RINFO_R2 + 1].astype(jnp.int32)
    cnt = counts[0, :N_EXPERTS].astype(jnp.int32)
    tiles_e = (cnt + tm - 1) // tm
    end_tile = jnp.cumsum(tiles_e)
    start_row = (end_tile - tiles_e) * tm
    pos = start_row[expert] + rank
    n_used = end_tile[-1:]
    tile_ids = jnp.minimum(jnp.arange(n_tiles, dtype=jnp.int32), n_used - 1)
    tile_expert = jnp.sum(tile_ids[:, None] >= end_tile[None, :], axis=1).astype(jnp.int32)
    tok = jnp.repeat(jnp.arange(n_tok, dtype=jnp.int32), 2)
    src_tok = jnp.zeros((n_tiles * tm,), jnp.int32).at[pos.reshape(-1)].set(tok)
    return pos, tile_expert, n_used.astype(jnp.int32), src_tok.reshape(n_tiles, 1, tm)


def _group_lanes(fwd, bwd):
    both = jnp.concatenate([fwd.reshape(SSD_GROUPS, HEADS_PER_GROUP), bwd.reshape(SSD_GROUPS, HEADS_PER_GROUP)], axis=1)
    return jnp.pad(both, ((0, 0), (0, LANES - 2 * HEADS_PER_GROUP))).reshape(SSD_GROUPS, 1, LANES)


def kernel(x, c, ctx, c_ctx, w_mod, b_mod, norm1_g, w_in, conv_w, conv_b, dt_bias_f, dt_bias_b, a_log_f, a_log_b,
           d_skip, ssd_norm_g, cm_ln_g, cm_ln_b, w_spatial, b_spatial, w_out, norm2_g, w_router_group,
           b_router_group, w_router_expert, b_router_expert, w_exp_gate, w_exp_up, w_exp_down, normf_g):
    bsz, seq, _ = x.shape
    ctx_len = ctx.shape[1]
    i = 0

    cc = jnp.concatenate([c, c_ctx[None, :], jnp.zeros((MOD_ROWS - bsz - 1, D_MODEL), F32)], axis=0)
    mod = _modulation(cc, w_mod[i], b_mod[i][None, :])
    mod3 = mod.reshape(MOD_ROWS, 1, N_MOD * D_MODEL)

    w_main, w_dt = _w_in_prep(w_in[i])
    g1row = norm1_g[i][None, :]

    x2 = x.reshape(bsz * seq, D_MODEL)
    ctx2 = ctx.reshape(bsz * ctx_len, D_MODEL)
    tm_in = 1024
    pm_x, dt_x = _in_proj(x2, mod3, lambda t: t // (seq // tm_in), g1row, w_main, w_dt, tm_in, 1024)
    pm_c, dt_c = _in_proj(ctx2, mod3, lambda t: bsz, g1row, w_main, w_dt, bsz * ctx_len, 1024)

    dtbias_g = _group_lanes(dt_bias_f[i], dt_bias_b[i])
    arow_g = _group_lanes(-jnp.exp(a_log_f[i].astype(F32)), -jnp.exp(a_log_b[i].astype(F32)))
    dskip_g = jnp.repeat(d_skip[i], SSD_HEAD_DIM).reshape(SSD_GROUPS, 1, GROUP_WIDTH)
    cw = conv_w[i]
    cb = conv_b[i][None, :]
    h_zero = jnp.zeros((bsz, SSD_GROUPS, SSD_STATE, GROUP_WIDTH), F32)

    hc_f, hc_b = _ssd(pm_c, dt_c, cw, cb, dtbias_g, arow_g, dskip_g, h_zero, h_zero,
                      batch=bsz, seq=ctx_len, conv_tile=ctx_len, conv_period=ctx_len, with_output=False)
    yd = _ssd(pm_x, dt_x, cw, cb, dtbias_g, arow_g, dskip_g, hc_f, hc_b,
              batch=bsz, seq=seq, conv_tile=CHUNK, conv_period=GRID_W, with_output=True)

    x1 = _out_proj(yd, pm_x, x2, mod3, ssd_norm_g[i][None, :], cm_ln_g[i][None, :], cm_ln_b[i][None, :],
                   w_spatial[i].astype(BF16), b_spatial[i].T, w_out[i].astype(BF16), seq=seq, tm=512, tn=1024)

    w_re = jnp.transpose(w_router_expert[i], (1, 0, 2)).reshape(D_MODEL, N_EXPERTS)
    pad = LANES - N_EXPERTS - N_GROUPS
    w_router = jnp.pad(jnp.concatenate([w_re, w_router_group[i]], axis=1), ((0, 0), (0, pad)))
    b_router = jnp.pad(jnp.concatenate([b_router_expert[i].reshape(-1), b_router_group[i]]), (0, pad))[None, :]
    h2, rinfo, counts = _route_call(x1, mod3, norm2_g[i][None, :], w_router, b_router, seq=seq, tm=512)

    tm_e = 256
    pos, tile_expert, n_used, src_tok = _dispatch_plan(rinfo, counts, tm=tm_e)
    wg = w_exp_gate[i].reshape(N_EXPERTS, D_MODEL, EXPERT_FF)
    wu = w_exp_up[i].reshape(N_EXPERTS, D_MODEL, EXPERT_FF)
    wd = w_exp_down[i].reshape(N_EXPERTS, EXPERT_FF, D_MODEL)
    y_sorted = _experts_call(tile_expert, n_used, src_tok, h2, wg, wu, wd, tm=tm_e)
    tm_c = 256
    out = _combine_call(pos.reshape(bsz * seq // tm_c, 1, 2 * tm_c), x1, mod3, normf_g[None, :], rinfo, y_sorted,
                        seq=seq, tm=tm_c)
    return out.reshape(bsz, seq, D_MODEL)
```

```python
import functools

import jax
import jax.numpy as jnp
from jax import lax
from jax.experimental import pallas as pl
from jax.experimental.pallas import tpu as pltpu

F32 = jnp.float32
BF16 = jnp.bfloat16
HIGHEST = lax.Precision.HIGHEST

D_MODEL = 2048
GRID_W = 64
SSD_WIDTH = 2048
CM_WIDTH = 2048
SSD_HEADS = 32
SSD_HEAD_DIM = 64
SSD_GROUPS = 4
HEADS_PER_GROUP = SSD_HEADS // SSD_GROUPS
GROUP_WIDTH = HEADS_PER_GROUP * SSD_HEAD_DIM
SSD_STATE = 128
CHUNK = 128
BC_WIDTH = SSD_GROUPS * SSD_STATE
CM_HEADS = 8
CM_HEAD_DIM = CM_WIDTH // CM_HEADS
N_GROUPS = 4
N_EXPERTS = 32
EXPERTS_PER_GROUP = 8
EXPERT_FF = 512
N_MOD = 6
NORM_EPS = 1e-6
LANES = 128
NEG_BIG = -1e30
MOD_ROWS = 8
VMEM_LIMIT = 56 * 1024 * 1024


def _params(*sem):
    return pltpu.CompilerParams(dimension_semantics=sem, vmem_limit_bytes=VMEM_LIMIT)


def _silu(v):
    return v * jax.nn.sigmoid(v)


def _dot(a, b):
    return jnp.dot(a, b, preferred_element_type=F32)


def _dot_nt(a, b):
    return lax.dot_general(a, b, (((1,), (1,)), ((), ())), preferred_element_type=F32)


def _mod_kernel(cc_ref, w_ref, b_ref, o_ref):
    a = _silu(cc_ref[...])
    hi = a.astype(BF16).astype(F32)
    lhs = jnp.concatenate([hi, a - hi], axis=0).astype(BF16)
    r = _dot(lhs, w_ref[...].astype(BF16))
    o_ref[...] = r[:MOD_ROWS] + r[MOD_ROWS:] + b_ref[...]


def _modulation(cc, w_mod, b_mod):
    n = w_mod.shape[1]
    tn = 1024
    return pl.pallas_call(
        _mod_kernel,
        grid=(n // tn,),
        in_specs=[
            pl.BlockSpec((MOD_ROWS, D_MODEL), lambda j: (0, 0)),
            pl.BlockSpec((D_MODEL, tn), lambda j: (0, j)),
            pl.BlockSpec((1, tn), lambda j: (0, j)),
        ],
        out_specs=pl.BlockSpec((MOD_ROWS, tn), lambda j: (0, j)),
        out_shape=jax.ShapeDtypeStruct((MOD_ROWS, n), F32),
        compiler_params=_params("arbitrary"),
        name="modulation",
    )(cc, w_mod, b_mod)


PREP_ROWS = 512
PREP_UV_BLK0 = 2 * SSD_WIDTH // PREP_ROWS
PREP_BC_BLK0 = (2 * SSD_WIDTH + 2 * CM_WIDTH) // PREP_ROWS
PREP_N_BLKS = PREP_BC_BLK0 + 2 * BC_WIDTH // PREP_ROWS
DT_COL0 = 2 * SSD_WIDTH + 2 * BC_WIDTH
UV_COL0 = DT_COL0 + 2 * SSD_HEADS


def _w_in_prep_kernel(a_ref, dt_ref, o_ref, wdt_ref):
    @pl.when(pl.program_id(0) == 0)
    def _():
        wdt_ref[...] = dt_ref[...].astype(BF16)

    o_ref[...] = a_ref[...].astype(BF16)


def _w_in_prep(w_in_t):
    def src_row(j):
        uv = UV_COL0 + (j - PREP_UV_BLK0) * PREP_ROWS
        bc = 2 * SSD_WIDTH + (j - PREP_BC_BLK0) * PREP_ROWS
        return pl.multiple_of(jnp.where(j < PREP_UV_BLK0, j * PREP_ROWS, jnp.where(j < PREP_BC_BLK0, uv, bc)), 8)

    return pl.pallas_call(
        _w_in_prep_kernel,
        grid=(PREP_N_BLKS,),
        in_specs=[
            pl.BlockSpec((pl.Element(PREP_ROWS), pl.Element(D_MODEL)), lambda j: (src_row(j), 0)),
            pl.BlockSpec((LANES, D_MODEL), lambda j: (DT_COL0 // LANES, 0)),
        ],
        out_specs=[
            pl.BlockSpec((PREP_ROWS, D_MODEL), lambda j: (j, 0)),
            pl.BlockSpec((LANES, D_MODEL), lambda j: (0, 0)),
        ],
        out_shape=[
            jax.ShapeDtypeStruct((PREP_N_BLKS * PREP_ROWS, D_MODEL), BF16),
            jax.ShapeDtypeStruct((LANES, D_MODEL), BF16),
        ],
        compiler_params=_params("arbitrary"),
        name="w_in_prep",
    )(w_in_t, w_in_t)


def _in_proj_kernel(x_ref, g_ref, sh_ref, sc_ref, w_ref, wdt_ref, o_ref, dt_ref, h_scr):
    tm = x_ref.shape[0]
    rc = 256

    @pl.when(pl.program_id(1) == 0)
    def _():
        gain = g_ref[...]
        scale = 1.0 + sc_ref[...]
        shift = sh_ref[...]

        def body(k, carry):
            r = pl.ds(pl.multiple_of(k * rc, rc), rc)
            xv = x_ref[r, :]
            ms = jnp.mean(xv * xv, axis=-1, keepdims=True)
            h = (xv * lax.rsqrt(ms + NORM_EPS) * gain) * scale + shift
            h_scr[r, :] = h.astype(BF16)
            return carry

        lax.fori_loop(0, tm // rc, body, 0)
        dt_ref[...] = _dot_nt(h_scr[...], wdt_ref[...])

    o_ref[...] = _dot_nt(h_scr[...], w_ref[...]).astype(o_ref.dtype)


def _in_proj(x2, mod3, mod_row_of_tile, norm_g, w_main, w_dt, tm, tn):
    m = x2.shape[0]
    n = w_main.shape[0]
    return pl.pallas_call(
        _in_proj_kernel,
        grid=(m // tm, n // tn),
        in_specs=[
            pl.BlockSpec((tm, D_MODEL), lambda i, j: (i, 0)),
            pl.BlockSpec((1, D_MODEL), lambda i, j: (0, 0)),
            pl.BlockSpec((None, 1, D_MODEL), lambda i, j: (mod_row_of_tile(i), 0, 0)),
            pl.BlockSpec((None, 1, D_MODEL), lambda i, j: (mod_row_of_tile(i), 0, 1)),
            pl.BlockSpec((tn, D_MODEL), lambda i, j: (j, 0)),
            pl.BlockSpec((LANES, D_MODEL), lambda i, j: (0, 0)),
        ],
        out_specs=[
            pl.BlockSpec((tm, tn), lambda i, j: (i, j)),
            pl.BlockSpec((tm, LANES), lambda i, j: (i, 0)),
        ],
        out_shape=[
            jax.ShapeDtypeStruct((m, n), BF16),
            jax.ShapeDtypeStruct((m, LANES), F32),
        ],
        scratch_shapes=[pltpu.VMEM((tm, D_MODEL), BF16)],
        compiler_params=_params("parallel", "arbitrary"),
        name="in_proj",
    )(x2, norm_g, mod3, mod3, w_main, w_dt)


def _split3(v):
    hi = v.astype(BF16)
    rem = v - hi.astype(F32)
    mid = rem.astype(BF16)
    lo = (rem - mid.astype(F32)).astype(BF16)
    return hi, mid, lo


GROUP_LANES = 2 * HEADS_PER_GROUP


def _stack_split3(v):
    hi, mid, lo = _split3(v)
    stacked = (hi.astype(F32) + pltpu.roll(mid.astype(F32), GROUP_LANES, 1)
               + pltpu.roll(lo.astype(F32), 2 * GROUP_LANES, 1))
    return stacked.astype(BF16)


LOG2_E = 1.4426950408889634


def _ssd_dt_kernel(dt_ref, bias_ref, a_ref, dts_ref, cum_ref, src_t_ref):
    hpg = HEADS_PER_GROUP
    ii = lax.broadcasted_iota(jnp.int32, (CHUNK, CHUNK), 0)
    jj = lax.broadcasted_iota(jnp.int32, (CHUNK, CHUNK), 1)
    tri = (ii >= jj).astype(BF16)
    tri3 = jnp.concatenate([tri, tri, tri], axis=1)
    bias = bias_ref[...]
    a_row = a_ref[...]

    def body(k, carry):
        r = pl.ds(pl.multiple_of(k * CHUNK, CHUNK), CHUNK)
        raw = dt_ref[r, :] + bias
        dts = jnp.where(jj < 2 * SSD_HEADS, jnp.maximum(raw, 0.0) + jnp.log1p(jnp.exp(-jnp.abs(raw))), 0.0)
        adt = dts * a_row
        hi, mid, lo = _split3(adt)
        cf = _dot(tri3, jnp.concatenate([hi, mid, lo], axis=0))
        cr = cf[CHUNK - 1:CHUNK, :] - cf + adt
        valid = jj < 2 * SSD_HEADS
        cum = jnp.where(jj < SSD_HEADS, cf, cr)
        src_t = ((cum - jnp.log(jnp.where(valid, dts, 1.0))) * LOG2_E).T
        rt = pl.ds(pl.multiple_of(k * GROUP_LANES, GROUP_LANES), GROUP_LANES)
        for g in range(SSD_GROUPS):
            f0, b0 = g * hpg, SSD_HEADS + g * hpg
            for src, dst in ((cum * LOG2_E, cum_ref), (dts, dts_ref)):
                lanes = jnp.where(jj < hpg, pltpu.roll(src, (LANES - f0) % LANES, 1),
                                  pltpu.roll(src, (LANES - (b0 - hpg)) % LANES, 1))
                dst[g, r, :] = jnp.where(jj < GROUP_LANES, lanes, 0.0)
            src_t_ref[g, rt, :] = jnp.concatenate([src_t[f0:f0 + hpg, :], src_t[b0:b0 + hpg, :]], axis=0)
        return carry

    lax.fori_loop(0, dt_ref.shape[0] // CHUNK, body, 0)


def _ssd_dt(dt, bias_row, a_row, *, rows_per_step):
    m = dt.shape[0]
    row = pl.BlockSpec((1, LANES), lambda i: (0, 0))
    full = pl.BlockSpec((SSD_GROUPS, rows_per_step, LANES), lambda i: (0, i, 0))
    tr = pl.BlockSpec((SSD_GROUPS, rows_per_step // CHUNK * GROUP_LANES, LANES), lambda i: (0, i, 0))
    full_shape = jax.ShapeDtypeStruct((SSD_GROUPS, m, LANES), F32)
    tr_shape = jax.ShapeDtypeStruct((SSD_GROUPS, m // CHUNK * GROUP_LANES, LANES), F32)
    return pl.pallas_call(
        _ssd_dt_kernel,
        grid=(m // rows_per_step,),
        in_specs=[pl.BlockSpec((rows_per_step, LANES), lambda i: (i, 0)), row, row],
        out_specs=[full, full, tr],
        out_shape=[full_shape, full_shape, tr_shape],
        compiler_params=_params("parallel"),
        name="ssd_dt",
    )(dt, bias_row, a_row)


def _ssd_kernel(xs_ref, b_ref, c_ref, dts_ref, cum_ref, src_t_ref, cwx_ref, cbx_ref, cwb_ref, cbb_ref,
                cwc_ref, cbc_ref, dskip_ref, h0f_ref, h0b_ref, *rest,
                n_chunks, conv_tile, conv_period, with_output):
    if with_output:
        y_ref, xs_s, b_s, c_s, hf_s, hb_s, y_s = rest
    else:
        hf_out, hb_out, xs_s, b_s, c_s, hf_s, hb_s = rest
    seq = n_chunks * CHUNK
    hpg = HEADS_PER_GROUP

    def conv_silu(src_ref, dst_ref, w_ref, bias_ref):
        cols = src_ref.shape[1]
        w0, w1, w2 = w_ref[0:1, :], w_ref[1:2, :], w_ref[2:3, :]
        bias = bias_ref[...]
        row = lax.broadcasted_iota(jnp.int32, (conv_tile, cols), 0) % conv_period
        first = row == 0
        last = row == conv_period - 1

        def body(k, carry):
            r = pl.ds(pl.multiple_of(k * conv_tile, conv_tile), conv_tile)
            v = src_ref[r, :].astype(F32)
            prev = jnp.where(first, 0.0, pltpu.roll(v, 1, 0))
            nxt = jnp.where(last, 0.0, pltpu.roll(v, conv_tile - 1, 0))
            y = bias + prev * w0 + v * w1 + nxt * w2
            dst_ref[r, :] = _silu(y).astype(BF16)
            return carry

        lax.fori_loop(0, seq // conv_tile, body, 0)

    conv_silu(xs_ref, xs_s, cwx_ref, cbx_ref)
    conv_silu(b_ref, b_s, cwb_ref, cbb_ref)
    conv_silu(c_ref, c_s, cwc_ref, cbc_ref)

    ii = lax.broadcasted_iota(jnp.int32, (CHUNK, CHUNK), 0)
    jj = lax.broadcasted_iota(jnp.int32, (CHUNK, CHUNK), 1)
    lower = ii >= jj
    upper = jj >= ii

    hf_s[...] = h0f_ref[...]
    hb_s[...] = h0b_ref[...]

    head_of_col = lax.broadcasted_iota(jnp.int32, (LANES, GROUP_WIDTH), 1) // SSD_HEAD_DIM
    sel_row = lax.broadcasted_iota(jnp.int32, (LANES, GROUP_WIDTH), 0)
    sel_valid = sel_row < 3 * GROUP_LANES
    sel_f = (sel_valid & (sel_row % GROUP_LANES == head_of_col)).astype(BF16)
    sel_b = (sel_valid & (sel_row % GROUP_LANES == head_of_col + hpg)).astype(BF16)
    low_half = jj < SSD_HEAD_DIM
    group_lane = jj < GROUP_LANES

    def direction_terms(r, sel):
        cum = cum_ref[r, :]
        dts = dts_ref[r, :]
        tot = jnp.where(jj[0:1, :] < hpg, cum[CHUNK - 1:CHUNK, :], cum[0:1, :])
        wst = dts * jnp.exp2(tot - cum)
        eoff = jnp.where(group_lane, jnp.exp2(cum), 0.0)
        edec = jnp.broadcast_to(jnp.where(group_lane[0:1, :], jnp.exp2(tot), 0.0), (8, LANES))
        ex = _dot(_stack_split3(jnp.concatenate([wst, eoff, edec], axis=0)), sel)
        return cum, ex[:CHUNK], ex[CHUNK:2 * CHUNK], ex[2 * CHUNK:2 * CHUNK + 1]

    def state_step(h_s, r, sel):
        cum, w_state, e_off, e_dec = direction_terms(r, sel)
        xc = xs_s[r, :]
        xw = (xc.astype(F32) * w_state).astype(BF16)
        s_new = lax.dot_general(b_s[r, :], xw, (((0,), (0,)), ((), ())), preferred_element_type=F32)
        h = h_s[...]
        y_off = _dot(c_s[r, :], h.astype(BF16)) * e_off
        h_s[...] = h * e_dec + s_new
        return cum, xc, y_off

    def fwd_body(k, carry):
        r = pl.ds(pl.multiple_of(k * CHUNK, CHUNK), CHUNK)
        cum, xc, y_off = state_step(hf_s, r, sel_f)
        if with_output:
            g = _dot_nt(c_s[r, :], b_s[r, :])
            rt = pl.ds(pl.multiple_of(k * GROUP_LANES, GROUP_LANES), GROUP_LANES)
            src_t = src_t_ref[rt, :]
            pieces = []
            for pair in range(hpg // 2):
                ms = []
                for h in (2 * pair, 2 * pair + 1):
                    hb = h + hpg
                    mf = jnp.exp2(jnp.where(lower, cum[:, h:h + 1] - src_t[h:h + 1, :], NEG_BIG))
                    mb = jnp.exp2(jnp.where(upper, cum[:, hb:hb + 1] - src_t[hb:hb + 1, :], NEG_BIG))
                    ms.append((g * (mf + mb)).astype(BF16))
                lhs = jnp.concatenate(ms, axis=1)
                xp = xc[:, pair * LANES:(pair + 1) * LANES]
                zero = jnp.zeros_like(xp)
                rhs = jnp.concatenate([jnp.where(low_half, xp, zero), jnp.where(low_half, zero, xp)], axis=0)
                pieces.append(_dot(lhs, rhs))
            y_diag = jnp.concatenate(pieces, axis=1)
            y_s[r, :] = y_diag + y_off + xc.astype(F32) * dskip_ref[...]
        return carry

    lax.fori_loop(0, n_chunks, fwd_body, 0)

    def bwd_body(k, carry):
        r = pl.ds(pl.multiple_of((n_chunks - 1 - k) * CHUNK, CHUNK), CHUNK)
        _, _, y_off = state_step(hb_s, r, sel_b)
        if with_output:
            y_ref[r, :] = (y_s[r, :] + y_off).astype(y_ref.dtype)
        return carry

    lax.fori_loop(0, n_chunks, bwd_body, 0)

    if not with_output:
        hf_out[...] = hf_s[...]
        hb_out[...] = hb_s[...]


def _ssd(pm, dt_terms, conv_w, conv_b, dskip_g, h0f, h0b, *, batch, seq, conv_tile, conv_period, with_output):
    n_chunks = seq // CHUNK
    dts_g, cum_g, src_t_g = dt_terms
    dt_spec = pl.BlockSpec((None, seq, LANES), lambda b, g: (g, b, 0))
    dt_t_spec = pl.BlockSpec((None, n_chunks * GROUP_LANES, LANES), lambda b, g: (g, b, 0))
    xs_blk0 = SSD_WIDTH // GROUP_WIDTH
    b_blk0 = (2 * SSD_WIDTH + 2 * CM_WIDTH) // SSD_STATE
    c_blk0 = b_blk0 + SSD_GROUPS
    cw_b0 = SSD_WIDTH // SSD_STATE
    cw_c0 = cw_b0 + SSD_GROUPS
    state_spec = pl.BlockSpec((None, None, SSD_STATE, GROUP_WIDTH), lambda b, g: (b, g, 0, 0))
    in_specs = [
        pl.BlockSpec((seq, GROUP_WIDTH), lambda b, g: (b, xs_blk0 + g)),
        pl.BlockSpec((seq, SSD_STATE), lambda b, g: (b, b_blk0 + g)),
        pl.BlockSpec((seq, SSD_STATE), lambda b, g: (b, c_blk0 + g)),
        dt_spec,
        dt_spec,
        dt_t_spec,
        pl.BlockSpec((3, GROUP_WIDTH), lambda b, g: (0, g)),
        pl.BlockSpec((1, GROUP_WIDTH), lambda b, g: (0, g)),
        pl.BlockSpec((3, SSD_STATE), lambda b, g: (0, cw_b0 + g)),
        pl.BlockSpec((1, SSD_STATE), lambda b, g: (0, cw_b0 + g)),
        pl.BlockSpec((3, SSD_STATE), lambda b, g: (0, cw_c0 + g)),
        pl.BlockSpec((1, SSD_STATE), lambda b, g: (0, cw_c0 + g)),
        pl.BlockSpec((None, 1, GROUP_WIDTH), lambda b, g: (g, 0, 0)),
        state_spec,
        state_spec,
    ]
    scratch = [
        pltpu.VMEM((seq, GROUP_WIDTH), BF16),
        pltpu.VMEM((seq, SSD_STATE), BF16),
        pltpu.VMEM((seq, SSD_STATE), BF16),
        pltpu.VMEM((SSD_STATE, GROUP_WIDTH), F32),
        pltpu.VMEM((SSD_STATE, GROUP_WIDTH), F32),
    ]
    if with_output:
        out_specs = pl.BlockSpec((seq, GROUP_WIDTH), lambda b, g: (b, g))
        out_shape = jax.ShapeDtypeStruct((batch * seq, SSD_WIDTH), BF16)
        scratch.append(pltpu.VMEM((seq, GROUP_WIDTH), F32))
    else:
        out_specs = [state_spec, state_spec]
        out_shape = [jax.ShapeDtypeStruct((batch, SSD_GROUPS, SSD_STATE, GROUP_WIDTH), F32)] * 2
    return pl.pallas_call(
        functools.partial(_ssd_kernel, n_chunks=n_chunks, conv_tile=conv_tile, conv_period=conv_period,
                          with_output=with_output),
        grid=(batch, SSD_GROUPS),
        in_specs=in_specs,
        out_specs=out_specs,
        out_shape=out_shape,
        scratch_shapes=scratch,
        compiler_params=_params("parallel", "parallel"),
        name="ssd_out" if with_output else "ssd_ctx",
    )(pm, pm, pm, dts_g, cum_g, src_t_g, conv_w, conv_b, conv_w, conv_b, conv_w, conv_b, dskip_g, h0f, h0b)


def _out_proj_kernel(yd_ref, z_ref, u_ref, v_ref, x_ref, g1_ref, ng_ref, lng_ref, lnb_ref, ws_ref, bst_ref, w_ref,
                     o_ref, mix_scr):
    tm = yd_ref.shape[0]

    @pl.when(pl.program_id(1) == 0)
    def _():
        def body(k, carry):
            r = pl.ds(pl.multiple_of(k * CHUNK, CHUNK), CHUNK)
            a = yd_ref[r, :].astype(F32) * _silu(z_ref[r, :].astype(F32))
            ms = jnp.mean(a * a, axis=-1, keepdims=True)
            mix_scr[r, 0:SSD_WIDTH] = (a * lax.rsqrt(ms + NORM_EPS) * ng_ref[...]).astype(BF16)
            gv = jax.nn.gelu(v_ref[r, :].astype(F32))
            mu = jnp.mean(gv, axis=-1, keepdims=True)
            xc = gv - mu
            var = jnp.mean(xc * xc, axis=-1, keepdims=True)
            ln = ((xc * lax.rsqrt(var + NORM_EPS)) * lng_ref[...] + lnb_ref[...]).astype(BF16)
            gu = jax.nn.gelu(u_ref[r, :].astype(F32))
            for h in range(CM_HEADS):
                c0, c1 = h * CM_HEAD_DIM, (h + 1) * CM_HEAD_DIM
                s = _dot(ws_ref[h], ln[:, c0:c1]) + bst_ref[:, h:h + 1]
                mix_scr[r, SSD_WIDTH + c0:SSD_WIDTH + c1] = (gu[:, c0:c1] * s).astype(BF16)
            return carry

        lax.fori_loop(0, tm // CHUNK, body, 0)

    o_ref[...] = x_ref[...] + g1_ref[...] * _dot(mix_scr[...], w_ref[...])


def _out_proj(yd, pm, x2, mod3, ssd_norm_g, ln_g, ln_b, ws, bst, w_out, *, seq, tm, tn):
    m = x2.shape[0]
    tiles_per_batch = seq // tm
    z_blk, u_blk, v_blk = 0, 2 * SSD_WIDTH // CM_WIDTH, 2 * SSD_WIDTH // CM_WIDTH + 1
    g1_blk0 = 2 * D_MODEL // tn
    row = lambda shape: pl.BlockSpec(shape, lambda i, j: (0, 0))
    return pl.pallas_call(
        _out_proj_kernel,
        grid=(m // tm, D_MODEL // tn),
        in_specs=[
            pl.BlockSpec((tm, SSD_WIDTH), lambda i, j: (i, 0)),
            pl.BlockSpec((tm, SSD_WIDTH), lambda i, j: (i, z_blk)),
            pl.BlockSpec((tm, CM_WIDTH), lambda i, j: (i, u_blk)),
            pl.BlockSpec((tm, CM_WIDTH), lambda i, j: (i, v_blk)),
            pl.BlockSpec((tm, tn), lambda i, j: (i, j)),
            pl.BlockSpec((None, 1, tn), lambda i, j: (i // tiles_per_batch, 0, g1_blk0 + j)),
            row((1, SSD_WIDTH)),
            row((1, CM_WIDTH)),
            row((1, CM_WIDTH)),
            pl.BlockSpec((CM_HEADS, CHUNK, CHUNK), lambda i, j: (0, 0, 0)),
            row((CHUNK, CM_HEADS)),
            pl.BlockSpec((SSD_WIDTH + CM_WIDTH, tn), lambda i, j: (0, j)),
        ],
        out_specs=pl.BlockSpec((tm, tn), lambda i, j: (i, j)),
        out_shape=jax.ShapeDtypeStruct((m, D_MODEL), F32),
        scratch_shapes=[pltpu.VMEM((tm, SSD_WIDTH + CM_WIDTH), BF16)],
        compiler_params=_params("parallel", "arbitrary"),
        name="out_proj",
    )(yd, pm, pm, pm, x2, mod3, ssd_norm_g, ln_g, ln_b, ws, bst, w_out)


def _route(logits):
    lane = lax.broadcasted_iota(jnp.int32, logits.shape, 1)
    lane_f = lane.astype(F32)
    is_group = (lane >= N_EXPERTS) & (lane < N_EXPERTS + N_GROUPS)
    lg = jnp.where(is_group, logits, NEG_BIG)
    mg = jnp.max(lg, axis=1, keepdims=True)
    top_pg = 1.0 / jnp.sum(jnp.exp(lg - mg), axis=1, keepdims=True)
    gi = jnp.min(jnp.where(lg == mg, lane_f, 1e9), axis=1, keepdims=True) - N_EXPERTS
    in_group = (lane < N_EXPERTS) & ((lane // EXPERTS_PER_GROUP).astype(F32) == gi)
    le = jnp.where(in_group, logits, NEG_BIG)
    m1 = jnp.max(le, axis=1, keepdims=True)
    i1 = jnp.min(jnp.where(le == m1, lane_f, 1e9), axis=1, keepdims=True)
    le2 = jnp.where(lane_f == i1, NEG_BIG, le)
    m2 = jnp.max(le2, axis=1, keepdims=True)
    i2 = jnp.min(jnp.where(le2 == m2, lane_f, 1e9), axis=1, keepdims=True)
    e2 = jnp.exp(m2 - m1)
    p1 = 1.0 / (1.0 + e2)
    p2 = e2 * p1
    return i1, i2, p1 * top_pg, p2 * top_pg


RINFO_E1, RINFO_E2, RINFO_W1, RINFO_W2, RINFO_R1, RINFO_R2 = range(6)


def _route_kernel(x1_ref, n2g_ref, sh2_ref, sc2_ref, wr_ref, br_ref, h2_ref, rinfo_ref, counts_ref, cnt_scr):
    tm = x1_ref.shape[0]

    @pl.when(pl.program_id(0) == 0)
    def _():
        cnt_scr[...] = jnp.zeros_like(cnt_scr)

    x1 = x1_ref[...]
    ms = jnp.mean(x1 * x1, axis=-1, keepdims=True)
    h2 = (x1 * lax.rsqrt(ms + NORM_EPS) * n2g_ref[...]) * (1.0 + sc2_ref[...]) + sh2_ref[...]
    h2_ref[...] = h2
    logits = jnp.dot(h2, wr_ref[...], precision=HIGHEST, preferred_element_type=F32) + br_ref[...]
    i1, i2, w1, w2 = _route(logits)

    lane = lax.broadcasted_iota(jnp.int32, (tm, LANES), 1)
    lane_f = lane.astype(F32)
    oh1 = jnp.where(lane_f == i1, 1.0, 0.0)
    oh2 = jnp.where(lane_f == i2, 1.0, 0.0)
    before = (lax.broadcasted_iota(jnp.int32, (tm, tm), 0) > lax.broadcasted_iota(jnp.int32, (tm, tm), 1)).astype(BF16)
    carried = cnt_scr[...]
    tot1 = jnp.sum(oh1, axis=0, keepdims=True)
    r1 = jnp.sum(oh1 * (_dot(before, oh1.astype(BF16)) + carried), axis=1, keepdims=True)
    r2 = jnp.sum(oh2 * (_dot(before, oh2.astype(BF16)) + (carried + tot1)), axis=1, keepdims=True)
    counts = carried + tot1 + jnp.sum(oh2, axis=0, keepdims=True)
    cnt_scr[...] = counts
    counts_ref[...] = jnp.broadcast_to(counts, counts_ref.shape)

    info = jnp.zeros((tm, LANES), F32)
    for k, v in ((RINFO_E1, i1), (RINFO_E2, i2), (RINFO_W1, w1), (RINFO_W2, w2), (RINFO_R1, r1), (RINFO_R2, r2)):
        info = jnp.where(lane == k, v, info)
    rinfo_ref[...] = info


def _route_call(x1, mod3, norm2_g, w_router, b_router, *, seq, tm):
    m = x1.shape[0]
    tiles_per_batch = seq // tm
    row = lambda shape: pl.BlockSpec(shape, lambda i: (0, 0))
    modrow = lambda k: pl.BlockSpec((None, 1, D_MODEL), lambda i: (i // tiles_per_batch, 0, k))
    return pl.pallas_call(
        _route_kernel,
        grid=(m // tm,),
        in_specs=[
            pl.BlockSpec((tm, D_MODEL), lambda i: (i, 0)),
            row((1, D_MODEL)),
            modrow(3),
            modrow(4),
            row((D_MODEL, LANES)),
            row((1, LANES)),
        ],
        out_specs=[
            pl.BlockSpec((tm, D_MODEL), lambda i: (i, 0)),
            pl.BlockSpec((tm, LANES), lambda i: (i, 0)),
            row((8, LANES)),
        ],
        out_shape=[
            jax.ShapeDtypeStruct((m, D_MODEL), F32),
            jax.ShapeDtypeStruct((m, LANES), F32),
            jax.ShapeDtypeStruct((8, LANES), F32),
        ],
        scratch_shapes=[pltpu.VMEM((1, LANES), F32)],
        compiler_params=_params("arbitrary"),
        name="route",
    )(x1, norm2_g, mod3, mod3, w_router, b_router)


def _experts_kernel(te_ref, nu_ref, idx_cur, idx_nxt, h2_hbm, wg_ref, wu_ref, wd_ref, y_ref,
                    xbuf, wg_b, wu_b, wd_b, sem):
    j = pl.program_id(0)
    n_used = nu_ref[0]
    tm = xbuf.shape[1]
    slot = j % 2

    def row_copy(idx_ref, r, s):
        return pltpu.make_async_copy(h2_hbm.at[pl.ds(idx_ref[0, r], 1), :], xbuf.at[s, pl.ds(r, 1), :], sem.at[s])

    def wait_tile(s):
        pltpu.make_async_copy(h2_hbm.at[pl.ds(0, tm), :], xbuf.at[s], sem.at[s]).wait()

    @pl.when(j == 0)
    def _():
        def body(r, carry):
            row_copy(idx_cur, r, 0).start()
            return carry

        lax.fori_loop(0, tm, body, 0)

    @pl.when((j < n_used) & ((j == 0) | (te_ref[j] != te_ref[jnp.maximum(j - 1, 0)])))
    def _():
        wg_b[...] = wg_ref[...].astype(BF16)
        wu_b[...] = wu_ref[...].astype(BF16)
        wd_b[...] = wd_ref[...].astype(BF16)

    @pl.when(j < n_used)
    def _():
        wait_tile(slot)
        for r in range(tm):
            row_copy(idx_nxt, r, 1 - slot).start(priority=r % 2)
        xt = xbuf[slot].astype(BF16)
        hid = (_silu(_dot(xt, wg_b[...])) * _dot(xt, wu_b[...])).astype(BF16)
        y_ref[...] = _dot(hid, wd_b[...])

    @pl.when(j == n_used - 1)
    def _():
        wait_tile(1 - slot)

    @pl.when(j >= n_used)
    def _():
        y_ref[...] = jnp.zeros_like(y_ref)


def _experts_call(tile_expert, n_used, src_tok, h2, wg, wu, wd, *, tm):
    n_tiles = tile_expert.shape[0]
    wspec = lambda shape: pl.BlockSpec((None,) + shape, lambda j, te, nu: (te[j], 0, 0))
    grid_spec = pltpu.PrefetchScalarGridSpec(
        num_scalar_prefetch=2,
        grid=(n_tiles,),
        in_specs=[
            pl.BlockSpec((None, 1, tm), lambda j, te, nu: (j, 0, 0), memory_space=pltpu.SMEM),
            pl.BlockSpec((None, 1, tm), lambda j, te, nu: (jnp.minimum(j + 1, n_tiles - 1), 0, 0),
                         memory_space=pltpu.SMEM),
            pl.BlockSpec(memory_space=pl.ANY),
            wspec((D_MODEL, EXPERT_FF)),
            wspec((D_MODEL, EXPERT_FF)),
            wspec((EXPERT_FF, D_MODEL)),
        ],
        out_specs=pl.BlockSpec((tm, D_MODEL), lambda j, te, nu: (j, 0)),
        scratch_shapes=[
            pltpu.VMEM((2, tm, D_MODEL), F32),
            pltpu.VMEM((D_MODEL, EXPERT_FF), BF16),
            pltpu.VMEM((D_MODEL, EXPERT_FF), BF16),
            pltpu.VMEM((EXPERT_FF, D_MODEL), BF16),
            pltpu.SemaphoreType.DMA((2,)),
        ],
    )
    return pl.pallas_call(
        _experts_kernel,
        grid_spec=grid_spec,
        out_shape=jax.ShapeDtypeStruct((n_tiles * tm, D_MODEL), F32),
        compiler_params=_params("arbitrary"),
        name="experts",
    )(tile_expert, n_used, src_tok, src_tok, h2, wg, wu, wd)


def _combine_kernel(pos_cur, pos_nxt, x1_ref, g2_ref, nfg_ref, rinfo_ref, y_hbm, o_ref, ybuf, sem):
    i = pl.program_id(0)
    tm = x1_ref.shape[0]
    slot = i % 2

    def row_copy(pos_ref, r, k, s):
        return pltpu.make_async_copy(y_hbm.at[pl.ds(pos_ref[0, 2 * r + k], 1), :], ybuf.at[s, k, pl.ds(r, 1), :],
                                     sem.at[s])

    def wait_tile(s):
        for k in range(2):
            pltpu.make_async_copy(y_hbm.at[pl.ds(0, tm), :], ybuf.at[s, k], sem.at[s]).wait()

    @pl.when(i == 0)
    def _():
        def body(r, carry):
            for k in range(2):
                row_copy(pos_cur, r, k, 0).start()
            return carry

        lax.fori_loop(0, tm, body, 0)

    wait_tile(slot)
    for r in range(tm):
        for k in range(2):
            row_copy(pos_nxt, r, k, 1 - slot).start(priority=k)
    info = rinfo_ref[...]
    w1 = info[:, RINFO_W1:RINFO_W1 + 1]
    w2 = info[:, RINFO_W2:RINFO_W2 + 1]
    moe = w1 * ybuf[slot, 0] + w2 * ybuf[slot, 1]
    y = x1_ref[...] + g2_ref[...] * moe
    ms = jnp.mean(y * y, axis=-1, keepdims=True)
    o_ref[...] = y * lax.rsqrt(ms + NORM_EPS) * nfg_ref[...]

    @pl.when(i == pl.num_programs(0) - 1)
    def _():
        wait_tile(1 - slot)


def _combine_call(pos, x1, mod3, normf_g, rinfo, y_sorted, *, seq, tm):
    m = x1.shape[0]
    n_tiles = m // tm
    tiles_per_batch = seq // tm
    return pl.pallas_call(
        _combine_kernel,
        grid=(n_tiles,),
        in_specs=[
            pl.BlockSpec((None, 1, 2 * tm), lambda i: (i, 0, 0), memory_space=pltpu.SMEM),
            pl.BlockSpec((None, 1, 2 * tm), lambda i: (jnp.minimum(i + 1, n_tiles - 1), 0, 0),
                         memory_space=pltpu.SMEM),
            pl.BlockSpec((tm, D_MODEL), lambda i: (i, 0)),
            pl.BlockSpec((None, 1, D_MODEL), lambda i: (i // tiles_per_batch, 0, 5)),
            pl.BlockSpec((1, D_MODEL), lambda i: (0, 0)),
            pl.BlockSpec((tm, LANES), lambda i: (i, 0)),
            pl.BlockSpec(memory_space=pl.ANY),
        ],
        out_specs=pl.BlockSpec((tm, D_MODEL), lambda i: (i, 0)),
        out_shape=jax.ShapeDtypeStruct((m, D_MODEL), F32),
        scratch_shapes=[
            pltpu.VMEM((2, 2, tm, D_MODEL), F32),
            pltpu.SemaphoreType.DMA((2,)),
        ],
        compiler_params=_params("arbitrary"),
        name="combine",
    )(pos, pos, x1, mod3, normf_g, rinfo, y_sorted)


def _dispatch_plan(rinfo, counts, *, tm):
    n_tok = rinfo.shape[0]
    n_tiles = (2 * n_tok + N_EXPERTS * (tm - 1)) // tm + 1
    expert = rinfo[:, RINFO_E1:RINFO_E2 + 1].astype(jnp.int32)
    rank = rinfo[:, RINFO_R1:RINFO_R2 + 1].astype(jnp.int32)
    cnt = counts[0, :N_EXPERTS].astype(jnp.int32)
    tiles_e = (cnt + tm - 1) // tm
    end_tile = jnp.cumsum(tiles_e)
    start_row = (end_tile - tiles_e) * tm
    is_e = expert[:, :, None] == jnp.arange(N_EXPERTS, dtype=jnp.int32)
    pos = jnp.sum(jnp.where(is_e, start_row, 0), axis=-1) + rank
    n_used = end_tile[-1:]
    tile_ids = jnp.minimum(jnp.arange(n_tiles, dtype=jnp.int32), n_used - 1)
    tile_expert = jnp.sum(tile_ids[:, None] >= end_tile[None, :], axis=1).astype(jnp.int32)
    tok = jnp.repeat(jnp.arange(n_tok, dtype=jnp.int32), 2)
    src_tok = jnp.zeros((n_tiles * tm,), jnp.int32).at[pos.reshape(-1)].set(tok)
    return pos, tile_expert, n_used.astype(jnp.int32), src_tok.reshape(n_tiles, 1, tm)


def kernel(x, c, ctx, c_ctx, w_mod, b_mod, norm1_g, w_in, conv_w, conv_b, dt_bias_f, dt_bias_b, a_log_f, a_log_b,
           d_skip, ssd_norm_g, cm_ln_g, cm_ln_b, w_spatial, b_spatial, w_out, norm2_g, w_router_group,
           b_router_group, w_router_expert, b_router_expert, w_exp_gate, w_exp_up, w_exp_down, normf_g):
    bsz, seq, _ = x.shape
    ctx_len = ctx.shape[1]
    i = 0

    cc = jnp.concatenate([c, c_ctx[None, :], jnp.zeros((MOD_ROWS - bsz - 1, D_MODEL), F32)], axis=0)
    mod = _modulation(cc, w_mod[i], b_mod[i][None, :])
    mod3 = mod.reshape(MOD_ROWS, 1, N_MOD * D_MODEL)

    w_main, w_dt = _w_in_prep(jnp.swapaxes(w_in[i], 0, 1))
    g1row = norm1_g[i][None, :]

    x2 = x.reshape(bsz * seq, D_MODEL)
    ctx2 = ctx.reshape(bsz * ctx_len, D_MODEL)
    tm_in = 1024
    pm_x, dt_x = _in_proj(x2, mod3, lambda t: t // (seq // tm_in), g1row, w_main, w_dt, tm_in, 1024)
    pm_c, dt_c = _in_proj(ctx2, mod3, lambda t: bsz, g1row, w_main, w_dt, bsz * ctx_len, 1024)

    lane_pad = (0, LANES - 2 * SSD_HEADS)
    dt_bias_row = jnp.pad(jnp.concatenate([dt_bias_f[i], dt_bias_b[i]]).astype(F32), lane_pad)[None, :]
    a_row = jnp.pad(-jnp.exp(jnp.concatenate([a_log_f[i], a_log_b[i]]).astype(F32)), lane_pad)[None, :]
    dskip_g = jnp.repeat(d_skip[i], SSD_HEAD_DIM).reshape(SSD_GROUPS, 1, GROUP_WIDTH)
    cw = conv_w[i]
    cb = conv_b[i][None, :]
    h_zero = jnp.zeros((bsz, SSD_GROUPS, SSD_STATE, GROUP_WIDTH), F32)

    dt_terms_c = _ssd_dt(dt_c, dt_bias_row, a_row, rows_per_step=bsz * ctx_len)
    dt_terms_x = _ssd_dt(dt_x, dt_bias_row, a_row, rows_per_step=1024)
    hc_f, hc_b = _ssd(pm_c, dt_terms_c, cw, cb, dskip_g, h_zero, h_zero,
                      batch=bsz, seq=ctx_len, conv_tile=ctx_len, conv_period=ctx_len, with_output=False)
    yd = _ssd(pm_x, dt_terms_x, cw, cb, dskip_g, hc_f, hc_b,
              batch=bsz, seq=seq, conv_tile=CHUNK, conv_period=GRID_W, with_output=True)

    x1 = _out_proj(yd, pm_x, x2, mod3, ssd_norm_g[i][None, :], cm_ln_g[i][None, :], cm_ln_b[i][None, :],
                   w_spatial[i].astype(BF16), b_spatial[i].T, w_out[i].astype(BF16), seq=seq, tm=512, tn=1024)

    w_re = jnp.transpose(w_router_expert[i], (1, 0, 2)).reshape(D_MODEL, N_EXPERTS)
    pad = LANES - N_EXPERTS - N_GROUPS
    w_router = jnp.pad(jnp.concatenate([w_re, w_router_group[i]], axis=1), ((0, 0), (0, pad)))
    b_router = jnp.pad(jnp.concatenate([b_router_expert[i].reshape(-1), b_router_group[i]]), (0, pad))[None, :]
    h2, rinfo, counts = _route_call(x1, mod3, norm2_g[i][None, :], w_router, b_router, seq=seq, tm=512)

    tm_e = 256
    pos, tile_expert, n_used, src_tok = _dispatch_plan(rinfo, counts, tm=tm_e)
    wg = w_exp_gate[i].reshape(N_EXPERTS, D_MODEL, EXPERT_FF)
    wu = w_exp_up[i].reshape(N_EXPERTS, D_MODEL, EXPERT_FF)
    wd = w_exp_down[i].reshape(N_EXPERTS, EXPERT_FF, D_MODEL)
    y_sorted = _experts_call(tile_expert, n_used, src_tok, h2, wg, wu, wd, tm=tm_e)
    tm_c = 256
    out = _combine_call(pos.reshape(bsz * seq // tm_c, 1, 2 * tm_c), x1, mod3, normf_g[None, :], rinfo, y_sorted,
                        seq=seq, tm=tm_c)
    return out.reshape(bsz, seq, D_MODEL)
```

```python
import functools

import jax
import jax.numpy as jnp
from jax import lax
from jax.experimental import pallas as pl
from jax.experimental.pallas import tpu as pltpu

F32 = jnp.float32
BF16 = jnp.bfloat16
HIGHEST = lax.Precision.HIGHEST

D_MODEL = 2048
GRID_W = 64
SSD_WIDTH = 2048
CM_WIDTH = 2048
SSD_HEADS = 32
SSD_HEAD_DIM = 64
SSD_GROUPS = 4
HEADS_PER_GROUP = SSD_HEADS // SSD_GROUPS
GROUP_WIDTH = HEADS_PER_GROUP * SSD_HEAD_DIM
SSD_STATE = 128
CHUNK = 128
BC_WIDTH = SSD_GROUPS * SSD_STATE
CM_HEADS = 8
CM_HEAD_DIM = CM_WIDTH // CM_HEADS
N_GROUPS = 4
N_EXPERTS = 32
EXPERTS_PER_GROUP = 8
EXPERT_FF = 512
N_MOD = 6
NORM_EPS = 1e-6
LANES = 128
NEG_BIG = -1e30
MOD_ROWS = 8
VMEM_LIMIT = 56 * 1024 * 1024


def _params(*sem):
    return pltpu.CompilerParams(dimension_semantics=sem, vmem_limit_bytes=VMEM_LIMIT)


def _silu(v):
    return v * jax.nn.sigmoid(v)


def _dot(a, b):
    return jnp.dot(a, b, preferred_element_type=F32)


def _dot_nt(a, b):
    return lax.dot_general(a, b, (((1,), (1,)), ((), ())), preferred_element_type=F32)


def _mod_kernel(cc_ref, w_ref, b_ref, o_ref):
    a = _silu(cc_ref[...])
    hi = a.astype(BF16).astype(F32)
    lhs = jnp.concatenate([hi, a - hi], axis=0).astype(BF16)
    r = _dot(lhs, w_ref[...].astype(BF16))
    o_ref[...] = r[:MOD_ROWS] + r[MOD_ROWS:] + b_ref[...]


def _modulation(cc, w_mod, b_mod):
    n = w_mod.shape[1]
    tn = 1024
    return pl.pallas_call(
        _mod_kernel,
        grid=(n // tn,),
        in_specs=[
            pl.BlockSpec((MOD_ROWS, D_MODEL), lambda j: (0, 0)),
            pl.BlockSpec((D_MODEL, tn), lambda j: (0, j)),
            pl.BlockSpec((1, tn), lambda j: (0, j)),
        ],
        out_specs=pl.BlockSpec((MOD_ROWS, tn), lambda j: (0, j)),
        out_shape=jax.ShapeDtypeStruct((MOD_ROWS, n), F32),
        compiler_params=_params("arbitrary"),
        name="modulation",
    )(cc, w_mod, b_mod)


PREP_ROWS = 512
PREP_UV_BLK0 = 2 * SSD_WIDTH // PREP_ROWS
PREP_BC_BLK0 = (2 * SSD_WIDTH + 2 * CM_WIDTH) // PREP_ROWS
PREP_N_BLKS = PREP_BC_BLK0 + 2 * BC_WIDTH // PREP_ROWS
DT_COL0 = 2 * SSD_WIDTH + 2 * BC_WIDTH
UV_COL0 = DT_COL0 + 2 * SSD_HEADS


def _w_in_prep_kernel(a_ref, dt_ref, o_ref, wdt_ref):
    @pl.when(pl.program_id(0) == 0)
    def _():
        wdt_ref[...] = dt_ref[...].astype(BF16)

    o_ref[...] = a_ref[...].astype(BF16)


def _w_in_prep(w_in_t):
    def src_row(j):
        uv = UV_COL0 + (j - PREP_UV_BLK0) * PREP_ROWS
        bc = 2 * SSD_WIDTH + (j - PREP_BC_BLK0) * PREP_ROWS
        return pl.multiple_of(jnp.where(j < PREP_UV_BLK0, j * PREP_ROWS, jnp.where(j < PREP_BC_BLK0, uv, bc)), 8)

    return pl.pallas_call(
        _w_in_prep_kernel,
        grid=(PREP_N_BLKS,),
        in_specs=[
            pl.BlockSpec((pl.Element(PREP_ROWS), pl.Element(D_MODEL)), lambda j: (src_row(j), 0)),
            pl.BlockSpec((LANES, D_MODEL), lambda j: (DT_COL0 // LANES, 0)),
        ],
        out_specs=[
            pl.BlockSpec((PREP_ROWS, D_MODEL), lambda j: (j, 0)),
            pl.BlockSpec((LANES, D_MODEL), lambda j: (0, 0)),
        ],
        out_shape=[
            jax.ShapeDtypeStruct((PREP_N_BLKS * PREP_ROWS, D_MODEL), BF16),
            jax.ShapeDtypeStruct((LANES, D_MODEL), BF16),
        ],
        compiler_params=_params("arbitrary"),
        name="w_in_prep",
    )(w_in_t, w_in_t)


def _in_proj_kernel(x_ref, g_ref, sh_ref, sc_ref, w_ref, wdt_ref, o_ref, dt_ref, h_scr):
    tm = x_ref.shape[0]
    rc = 256

    @pl.when(pl.program_id(1) == 0)
    def _():
        gain = g_ref[...]
        scale = 1.0 + sc_ref[...]
        shift = sh_ref[...]

        def body(k, carry):
            r = pl.ds(pl.multiple_of(k * rc, rc), rc)
            xv = x_ref[r, :]
            ms = jnp.mean(xv * xv, axis=-1, keepdims=True)
            h = (xv * lax.rsqrt(ms + NORM_EPS) * gain) * scale + shift
            h_scr[r, :] = h.astype(BF16)
            return carry

        lax.fori_loop(0, tm // rc, body, 0)
        dt_ref[...] = _dot_nt(h_scr[...], wdt_ref[...])

    o_ref[...] = _dot_nt(h_scr[...], w_ref[...]).astype(o_ref.dtype)


def _in_proj(x2, mod3, mod_row_of_tile, norm_g, w_main, w_dt, tm, tn):
    m = x2.shape[0]
    n = w_main.shape[0]
    return pl.pallas_call(
        _in_proj_kernel,
        grid=(m // tm, n // tn),
        in_specs=[
            pl.BlockSpec((tm, D_MODEL), lambda i, j: (i, 0)),
            pl.BlockSpec((1, D_MODEL), lambda i, j: (0, 0)),
            pl.BlockSpec((None, 1, D_MODEL), lambda i, j: (mod_row_of_tile(i), 0, 0)),
            pl.BlockSpec((None, 1, D_MODEL), lambda i, j: (mod_row_of_tile(i), 0, 1)),
            pl.BlockSpec((tn, D_MODEL), lambda i, j: (j, 0)),
            pl.BlockSpec((LANES, D_MODEL), lambda i, j: (0, 0)),
        ],
        out_specs=[
            pl.BlockSpec((tm, tn), lambda i, j: (i, j)),
            pl.BlockSpec((tm, LANES), lambda i, j: (i, 0)),
        ],
        out_shape=[
            jax.ShapeDtypeStruct((m, n), BF16),
            jax.ShapeDtypeStruct((m, LANES), F32),
        ],
        scratch_shapes=[pltpu.VMEM((tm, D_MODEL), BF16)],
        compiler_params=_params("parallel", "arbitrary"),
        name="in_proj",
    )(x2, norm_g, mod3, mod3, w_main, w_dt)


def _split3(v):
    hi = v.astype(BF16)
    rem = v - hi.astype(F32)
    mid = rem.astype(BF16)
    lo = (rem - mid.astype(F32)).astype(BF16)
    return hi, mid, lo


GROUP_LANES = 2 * HEADS_PER_GROUP


def _stack_split3(v):
    hi, mid, lo = _split3(v)
    stacked = (hi.astype(F32) + pltpu.roll(mid.astype(F32), GROUP_LANES, 1)
               + pltpu.roll(lo.astype(F32), 2 * GROUP_LANES, 1))
    return stacked.astype(BF16)


LOG2_E = 1.4426950408889634


def _ssd_dt_kernel(dt_ref, bias_ref, a_ref, dts_ref, cum_ref, src_t_ref):
    hpg = HEADS_PER_GROUP
    ii = lax.broadcasted_iota(jnp.int32, (CHUNK, CHUNK), 0)
    jj = lax.broadcasted_iota(jnp.int32, (CHUNK, CHUNK), 1)
    tri = (ii >= jj).astype(BF16)
    tri3 = jnp.concatenate([tri, tri, tri], axis=1)
    bias = bias_ref[...]
    a_row = a_ref[...]

    def body(k, carry):
        r = pl.ds(pl.multiple_of(k * CHUNK, CHUNK), CHUNK)
        raw = dt_ref[r, :] + bias
        dts = jnp.where(jj < 2 * SSD_HEADS, jnp.maximum(raw, 0.0) + jnp.log1p(jnp.exp(-jnp.abs(raw))), 0.0)
        adt = dts * a_row
        hi, mid, lo = _split3(adt)
        cf = _dot(tri3, jnp.concatenate([hi, mid, lo], axis=0))
        cr = cf[CHUNK - 1:CHUNK, :] - cf + adt
        valid = jj < 2 * SSD_HEADS
        cum = jnp.where(jj < SSD_HEADS, cf, cr)
        src_t = ((cum - jnp.log(jnp.where(valid, dts, 1.0))) * LOG2_E).T
        rt = pl.ds(pl.multiple_of(k * GROUP_LANES, GROUP_LANES), GROUP_LANES)
        for g in range(SSD_GROUPS):
            f0, b0 = g * hpg, SSD_HEADS + g * hpg
            for src, dst in ((cum * LOG2_E, cum_ref), (dts, dts_ref)):
                lanes = jnp.where(jj < hpg, pltpu.roll(src, (LANES - f0) % LANES, 1),
                                  pltpu.roll(src, (LANES - (b0 - hpg)) % LANES, 1))
                dst[g, r, :] = jnp.where(jj < GROUP_LANES, lanes, 0.0)
            src_t_ref[g, rt, :] = jnp.concatenate([src_t[f0:f0 + hpg, :], src_t[b0:b0 + hpg, :]], axis=0)
        return carry

    lax.fori_loop(0, dt_ref.shape[0] // CHUNK, body, 0)


def _ssd_dt(dt, bias_row, a_row, *, rows_per_step):
    m = dt.shape[0]
    row = pl.BlockSpec((1, LANES), lambda i: (0, 0))
    full = pl.BlockSpec((SSD_GROUPS, rows_per_step, LANES), lambda i: (0, i, 0))
    tr = pl.BlockSpec((SSD_GROUPS, rows_per_step // CHUNK * GROUP_LANES, LANES), lambda i: (0, i, 0))
    full_shape = jax.ShapeDtypeStruct((SSD_GROUPS, m, LANES), F32)
    tr_shape = jax.ShapeDtypeStruct((SSD_GROUPS, m // CHUNK * GROUP_LANES, LANES), F32)
    return pl.pallas_call(
        _ssd_dt_kernel,
        grid=(m // rows_per_step,),
        in_specs=[pl.BlockSpec((rows_per_step, LANES), lambda i: (i, 0)), row, row],
        out_specs=[full, full, tr],
        out_shape=[full_shape, full_shape, tr_shape],
        compiler_params=_params("parallel"),
        name="ssd_dt",
    )(dt, bias_row, a_row)


def _ssd_kernel(xs_ref, b_ref, c_ref, dts_ref, cum_ref, src_t_ref, cwx_ref, cbx_ref, cwb_ref, cbb_ref,
                cwc_ref, cbc_ref, dskip_ref, h0f_ref, h0b_ref, *rest,
                n_chunks, conv_tile, conv_period, with_output):
    if with_output:
        y_ref, xs_s, b_s, c_s, hf_s, hb_s, y_s = rest
    else:
        hf_out, hb_out, xs_s, b_s, c_s, hf_s, hb_s = rest
    seq = n_chunks * CHUNK
    hpg = HEADS_PER_GROUP

    def conv_silu(src_ref, dst_ref, w_ref, bias_ref):
        cols = src_ref.shape[1]
        w0, w1, w2 = w_ref[0:1, :], w_ref[1:2, :], w_ref[2:3, :]
        bias = bias_ref[...]
        row = lax.broadcasted_iota(jnp.int32, (conv_tile, cols), 0) % conv_period
        first = row == 0
        last = row == conv_period - 1

        def body(k, carry):
            r = pl.ds(pl.multiple_of(k * conv_tile, conv_tile), conv_tile)
            v = src_ref[r, :].astype(F32)
            prev = jnp.where(first, 0.0, pltpu.roll(v, 1, 0))
            nxt = jnp.where(last, 0.0, pltpu.roll(v, conv_tile - 1, 0))
            y = bias + prev * w0 + v * w1 + nxt * w2
            dst_ref[r, :] = _silu(y).astype(BF16)
            return carry

        lax.fori_loop(0, seq // conv_tile, body, 0)

    conv_silu(xs_ref, xs_s, cwx_ref, cbx_ref)
    conv_silu(b_ref, b_s, cwb_ref, cbb_ref)
    conv_silu(c_ref, c_s, cwc_ref, cbc_ref)

    ii = lax.broadcasted_iota(jnp.int32, (CHUNK, CHUNK), 0)
    jj = lax.broadcasted_iota(jnp.int32, (CHUNK, CHUNK), 1)
    lower = ii >= jj
    upper = jj >= ii

    hf_s[...] = h0f_ref[...]
    hb_s[...] = h0b_ref[...]

    head_of_col = lax.broadcasted_iota(jnp.int32, (LANES, GROUP_WIDTH), 1) // SSD_HEAD_DIM
    sel_row = lax.broadcasted_iota(jnp.int32, (LANES, GROUP_WIDTH), 0)
    sel_valid = sel_row < 3 * GROUP_LANES
    sel_f = (sel_valid & (sel_row % GROUP_LANES == head_of_col)).astype(BF16)
    sel_b = (sel_valid & (sel_row % GROUP_LANES == head_of_col + hpg)).astype(BF16)
    low_half = jj < SSD_HEAD_DIM
    group_lane = jj < GROUP_LANES

    def direction_terms(r, sel):
        cum = cum_ref[r, :]
        dts = dts_ref[r, :]
        tot = jnp.where(jj[0:1, :] < hpg, cum[CHUNK - 1:CHUNK, :], cum[0:1, :])
        wst = dts * jnp.exp2(tot - cum)
        eoff = jnp.where(group_lane, jnp.exp2(cum), 0.0)
        edec = jnp.broadcast_to(jnp.where(group_lane[0:1, :], jnp.exp2(tot), 0.0), (8, LANES))
        ex = _dot(_stack_split3(jnp.concatenate([wst, eoff, edec], axis=0)), sel)
        return cum, ex[:CHUNK], ex[CHUNK:2 * CHUNK], ex[2 * CHUNK:2 * CHUNK + 1]

    def state_step(h_s, r, sel):
        cum, w_state, e_off, e_dec = direction_terms(r, sel)
        xc = xs_s[r, :]
        xw = (xc.astype(F32) * w_state).astype(BF16)
        s_new = lax.dot_general(b_s[r, :], xw, (((0,), (0,)), ((), ())), preferred_element_type=F32)
        h = h_s[...]
        y_off = _dot(c_s[r, :], h.astype(BF16)) * e_off
        h_s[...] = h * e_dec + s_new
        return cum, xc, y_off

    def fwd_body(k, carry):
        r = pl.ds(pl.multiple_of(k * CHUNK, CHUNK), CHUNK)
        cum, xc, y_off = state_step(hf_s, r, sel_f)
        if with_output:
            g = _dot_nt(c_s[r, :], b_s[r, :])
            rt = pl.ds(pl.multiple_of(k * GROUP_LANES, GROUP_LANES), GROUP_LANES)
            src_t = src_t_ref[rt, :]
            pieces = []
            for pair in range(hpg // 2):
                ms = []
                for h in (2 * pair, 2 * pair + 1):
                    hb = h + hpg
                    mf = jnp.exp2(jnp.where(lower, cum[:, h:h + 1] - src_t[h:h + 1, :], NEG_BIG))
                    mb = jnp.exp2(jnp.where(upper, cum[:, hb:hb + 1] - src_t[hb:hb + 1, :], NEG_BIG))
                    ms.append((g * (mf + mb)).astype(BF16))
                lhs = jnp.concatenate(ms, axis=1)
                xp = xc[:, pair * LANES:(pair + 1) * LANES]
                zero = jnp.zeros_like(xp)
                rhs = jnp.concatenate([jnp.where(low_half, xp, zero), jnp.where(low_half, zero, xp)], axis=0)
                pieces.append(_dot(lhs, rhs))
            y_diag = jnp.concatenate(pieces, axis=1)
            y_s[r, :] = y_diag + y_off + xc.astype(F32) * dskip_ref[...]
        return carry

    lax.fori_loop(0, n_chunks, fwd_body, 0)

    def bwd_body(k, carry):
        r = pl.ds(pl.multiple_of((n_chunks - 1 - k) * CHUNK, CHUNK), CHUNK)
        _, _, y_off = state_step(hb_s, r, sel_b)
        if with_output:
            y_ref[r, :] = (y_s[r, :] + y_off).astype(y_ref.dtype)
        return carry

    lax.fori_loop(0, n_chunks, bwd_body, 0)

    if not with_output:
        hf_out[...] = hf_s[...]
        hb_out[...] = hb_s[...]


def _ssd(pm, dt_terms, conv_w, conv_b, dskip_g, h0f, h0b, *, batch, seq, conv_tile, conv_period, with_output):
    n_chunks = seq // CHUNK
    dts_g, cum_g, src_t_g = dt_terms
    dt_spec = pl.BlockSpec((None, seq, LANES), lambda b, g: (g, b, 0))
    dt_t_spec = pl.BlockSpec((None, n_chunks * GROUP_LANES, LANES), lambda b, g: (g, b, 0))
    xs_blk0 = SSD_WIDTH // GROUP_WIDTH
    b_blk0 = (2 * SSD_WIDTH + 2 * CM_WIDTH) // SSD_STATE
    c_blk0 = b_blk0 + SSD_GROUPS
    cw_b0 = SSD_WIDTH // SSD_STATE
    cw_c0 = cw_b0 + SSD_GROUPS
    state_spec = pl.BlockSpec((None, None, SSD_STATE, GROUP_WIDTH), lambda b, g: (b, g, 0, 0))
    in_specs = [
        pl.BlockSpec((seq, GROUP_WIDTH), lambda b, g: (b, xs_blk0 + g)),
        pl.BlockSpec((seq, SSD_STATE), lambda b, g: (b, b_blk0 + g)),
        pl.BlockSpec((seq, SSD_STATE), lambda b, g: (b, c_blk0 + g)),
        dt_spec,
        dt_spec,
        dt_t_spec,
        pl.BlockSpec((3, GROUP_WIDTH), lambda b, g: (0, g)),
        pl.BlockSpec((1, GROUP_WIDTH), lambda b, g: (0, g)),
        pl.BlockSpec((3, SSD_STATE), lambda b, g: (0, cw_b0 + g)),
        pl.BlockSpec((1, SSD_STATE), lambda b, g: (0, cw_b0 + g)),
        pl.BlockSpec((3, SSD_STATE), lambda b, g: (0, cw_c0 + g)),
        pl.BlockSpec((1, SSD_STATE), lambda b, g: (0, cw_c0 + g)),
        pl.BlockSpec((None, 1, GROUP_WIDTH), lambda b, g: (g, 0, 0)),
        state_spec,
        state_spec,
    ]
    scratch = [
        pltpu.VMEM((seq, GROUP_WIDTH), BF16),
        pltpu.VMEM((seq, SSD_STATE), BF16),
        pltpu.VMEM((seq, SSD_STATE), BF16),
        pltpu.VMEM((SSD_STATE, GROUP_WIDTH), F32),
        pltpu.VMEM((SSD_STATE, GROUP_WIDTH), F32),
    ]
    if with_output:
        out_specs = pl.BlockSpec((seq, GROUP_WIDTH), lambda b, g: (b, g))
        out_shape = jax.ShapeDtypeStruct((batch * seq, SSD_WIDTH), BF16)
        scratch.append(pltpu.VMEM((seq, GROUP_WIDTH), F32))
    else:
        out_specs = [state_spec, state_spec]
        out_shape = [jax.ShapeDtypeStruct((batch, SSD_GROUPS, SSD_STATE, GROUP_WIDTH), F32)] * 2
    return pl.pallas_call(
        functools.partial(_ssd_kernel, n_chunks=n_chunks, conv_tile=conv_tile, conv_period=conv_period,
                          with_output=with_output),
        grid=(batch, SSD_GROUPS),
        in_specs=in_specs,
        out_specs=out_specs,
        out_shape=out_shape,
        scratch_shapes=scratch,
        compiler_params=_params("parallel", "parallel"),
        name="ssd_out" if with_output else "ssd_ctx",
    )(pm, pm, pm, dts_g, cum_g, src_t_g, conv_w, conv_b, conv_w, conv_b, conv_w, conv_b, dskip_g, h0f, h0b)


def _out_proj_kernel(yd_ref, z_ref, u_ref, v_ref, x_ref, g1_ref, ng_ref, lng_ref, lnb_ref, ws_ref, bst_ref, w_ref,
                     o_ref, mix_scr):
    tm = yd_ref.shape[0]

    @pl.when(pl.program_id(1) == 0)
    def _():
        def body(k, carry):
            r = pl.ds(pl.multiple_of(k * CHUNK, CHUNK), CHUNK)
            a = yd_ref[r, :].astype(F32) * _silu(z_ref[r, :].astype(F32))
            ms = jnp.mean(a * a, axis=-1, keepdims=True)
            mix_scr[r, 0:SSD_WIDTH] = (a * lax.rsqrt(ms + NORM_EPS) * ng_ref[...]).astype(BF16)
            gv = jax.nn.gelu(v_ref[r, :].astype(F32))
            mu = jnp.mean(gv, axis=-1, keepdims=True)
            xc = gv - mu
            var = jnp.mean(xc * xc, axis=-1, keepdims=True)
            ln = ((xc * lax.rsqrt(var + NORM_EPS)) * lng_ref[...] + lnb_ref[...]).astype(BF16)
            gu = jax.nn.gelu(u_ref[r, :].astype(F32))
            for h in range(CM_HEADS):
                c0, c1 = h * CM_HEAD_DIM, (h + 1) * CM_HEAD_DIM
                s = _dot(ws_ref[h], ln[:, c0:c1]) + bst_ref[:, h:h + 1]
                mix_scr[r, SSD_WIDTH + c0:SSD_WIDTH + c1] = (gu[:, c0:c1] * s).astype(BF16)
            return carry

        lax.fori_loop(0, tm // CHUNK, body, 0)

    o_ref[...] = x_ref[...] + g1_ref[...] * _dot(mix_scr[...], w_ref[...])


def _out_proj(yd, pm, x2, mod3, ssd_norm_g, ln_g, ln_b, ws, bst, w_out, *, seq, tm, tn):
    m = x2.shape[0]
    tiles_per_batch = seq // tm
    z_blk, u_blk, v_blk = 0, 2 * SSD_WIDTH // CM_WIDTH, 2 * SSD_WIDTH // CM_WIDTH + 1
    g1_blk0 = 2 * D_MODEL // tn
    row = lambda shape: pl.BlockSpec(shape, lambda i, j: (0, 0))
    return pl.pallas_call(
        _out_proj_kernel,
        grid=(m // tm, D_MODEL // tn),
        in_specs=[
            pl.BlockSpec((tm, SSD_WIDTH), lambda i, j: (i, 0)),
            pl.BlockSpec((tm, SSD_WIDTH), lambda i, j: (i, z_blk)),
            pl.BlockSpec((tm, CM_WIDTH), lambda i, j: (i, u_blk)),
            pl.BlockSpec((tm, CM_WIDTH), lambda i, j: (i, v_blk)),
            pl.BlockSpec((tm, tn), lambda i, j: (i, j)),
            pl.BlockSpec((None, 1, tn), lambda i, j: (i // tiles_per_batch, 0, g1_blk0 + j)),
            row((1, SSD_WIDTH)),
            row((1, CM_WIDTH)),
            row((1, CM_WIDTH)),
            pl.BlockSpec((CM_HEADS, CHUNK, CHUNK), lambda i, j: (0, 0, 0)),
            row((CHUNK, CM_HEADS)),
            pl.BlockSpec((SSD_WIDTH + CM_WIDTH, tn), lambda i, j: (0, j)),
        ],
        out_specs=pl.BlockSpec((tm, tn), lambda i, j: (i, j)),
        out_shape=jax.ShapeDtypeStruct((m, D_MODEL), F32),
        scratch_shapes=[pltpu.VMEM((tm, SSD_WIDTH + CM_WIDTH), BF16)],
        compiler_params=_params("parallel", "arbitrary"),
        name="out_proj",
    )(yd, pm, pm, pm, x2, mod3, ssd_norm_g, ln_g, ln_b, ws, bst, w_out)


def _route(logits):
    lane = lax.broadcasted_iota(jnp.int32, logits.shape, 1)
    lane_f = lane.astype(F32)
    is_group = (lane >= N_EXPERTS) & (lane < N_EXPERTS + N_GROUPS)
    lg = jnp.where(is_group, logits, NEG_BIG)
    mg = jnp.max(lg, axis=1, keepdims=True)
    top_pg = 1.0 / jnp.sum(jnp.exp(lg - mg), axis=1, keepdims=True)
    gi = jnp.min(jnp.where(lg == mg, lane_f, 1e9), axis=1, keepdims=True) - N_EXPERTS
    in_group = (lane < N_EXPERTS) & ((lane // EXPERTS_PER_GROUP).astype(F32) == gi)
    le = jnp.where(in_group, logits, NEG_BIG)
    m1 = jnp.max(le, axis=1, keepdims=True)
    i1 = jnp.min(jnp.where(le == m1, lane_f, 1e9), axis=1, keepdims=True)
    le2 = jnp.where(lane_f == i1, NEG_BIG, le)
    m2 = jnp.max(le2, axis=1, keepdims=True)
    i2 = jnp.min(jnp.where(le2 == m2, lane_f, 1e9), axis=1, keepdims=True)
    e2 = jnp.exp(m2 - m1)
    p1 = 1.0 / (1.0 + e2)
    p2 = e2 * p1
    return i1, i2, p1 * top_pg, p2 * top_pg


RINFO_E1, RINFO_E2, RINFO_W1, RINFO_W2, RINFO_R1, RINFO_R2 = range(6)


def _route_kernel(x1_ref, n2g_ref, sh2_ref, sc2_ref, wr_ref, br_ref, h2_ref, rinfo_ref, counts_ref, cnt_scr):
    tm = x1_ref.shape[0]

    @pl.when(pl.program_id(0) == 0)
    def _():
        cnt_scr[...] = jnp.zeros_like(cnt_scr)

    x1 = x1_ref[...]
    ms = jnp.mean(x1 * x1, axis=-1, keepdims=True)
    h2 = (x1 * lax.rsqrt(ms + NORM_EPS) * n2g_ref[...]) * (1.0 + sc2_ref[...]) + sh2_ref[...]
    h2_ref[...] = h2
    logits = jnp.dot(h2, wr_ref[...], precision=HIGHEST, preferred_element_type=F32) + br_ref[...]
    i1, i2, w1, w2 = _route(logits)

    lane = lax.broadcasted_iota(jnp.int32, (tm, LANES), 1)
    lane_f = lane.astype(F32)
    oh1 = jnp.where(lane_f == i1, 1.0, 0.0)
    oh2 = jnp.where(lane_f == i2, 1.0, 0.0)
    before = (lax.broadcasted_iota(jnp.int32, (tm, tm), 0) > lax.broadcasted_iota(jnp.int32, (tm, tm), 1)).astype(BF16)
    carried = cnt_scr[...]
    tot1 = jnp.sum(oh1, axis=0, keepdims=True)
    r1 = jnp.sum(oh1 * (_dot(before, oh1.astype(BF16)) + carried), axis=1, keepdims=True)
    r2 = jnp.sum(oh2 * (_dot(before, oh2.astype(BF16)) + (carried + tot1)), axis=1, keepdims=True)
    counts = carried + tot1 + jnp.sum(oh2, axis=0, keepdims=True)
    cnt_scr[...] = counts
    counts_ref[...] = jnp.broadcast_to(counts, counts_ref.shape)

    info = jnp.zeros((tm, LANES), F32)
    for k, v in ((RINFO_E1, i1), (RINFO_E2, i2), (RINFO_W1, w1), (RINFO_W2, w2), (RINFO_R1, r1), (RINFO_R2, r2)):
        info = jnp.where(lane == k, v, info)
    rinfo_ref[...] = info


def _route_call(x1, mod3, norm2_g, w_router, b_router, *, seq, tm):
    m = x1.shape[0]
    tiles_per_batch = seq // tm
    row = lambda shape: pl.BlockSpec(shape, lambda i: (0, 0))
    modrow = lambda k: pl.BlockSpec((None, 1, D_MODEL), lambda i: (i // tiles_per_batch, 0, k))
    return pl.pallas_call(
        _route_kernel,
        grid=(m // tm,),
        in_specs=[
            pl.BlockSpec((tm, D_MODEL), lambda i: (i, 0)),
            row((1, D_MODEL)),
            modrow(3),
            modrow(4),
            row((D_MODEL, LANES)),
            row((1, LANES)),
        ],
        out_specs=[
            pl.BlockSpec((tm, D_MODEL), lambda i: (i, 0)),
            pl.BlockSpec((tm, LANES), lambda i: (i, 0)),
            row((8, LANES)),
        ],
        out_shape=[
            jax.ShapeDtypeStruct((m, D_MODEL), F32),
            jax.ShapeDtypeStruct((m, LANES), F32),
            jax.ShapeDtypeStruct((8, LANES), F32),
        ],
        scratch_shapes=[pltpu.VMEM((1, LANES), F32)],
        compiler_params=_params("arbitrary"),
        name="route",
    )(x1, norm2_g, mod3, mod3, w_router, b_router)


def _dispatch_kernel(lt_ref, nu_ref, pos_ref, h2_ref, x_hbm, zbuf, zsem, sem, *, tile_rows, n_tiles):
    tm = h2_ref.shape[0]
    n_used = nu_ref[0]

    def zero_copy(t):
        return pltpu.make_async_copy(zbuf, x_hbm.at[pl.ds(pl.multiple_of(t * tile_rows, tile_rows), tile_rows), :], zsem)

    @pl.when(pl.program_id(0) == 0)
    def _():
        zbuf[...] = jnp.zeros_like(zbuf)
        for phase in ("start", "wait"):
            for e in range(N_EXPERTS):
                @pl.when(lt_ref[e] >= 0)
                def _():
                    getattr(zero_copy(lt_ref[e]), phase)()

            def unused(t, carry):
                getattr(zero_copy(t), phase)()
                return carry

            lax.fori_loop(n_used, n_tiles, unused, 0)

    def body(r, carry):
        for k in range(2):
            pltpu.make_async_copy(h2_ref.at[pl.ds(r, 1), :], x_hbm.at[pl.ds(pos_ref[0, 2 * r + k], 1), :], sem).start()
        return carry

    lax.fori_loop(0, tm, body, 0, unroll=8)
    for k in range(2):
        pltpu.make_async_copy(h2_ref, x_hbm.at[pl.ds(0, tm), :], sem).wait()


def _dispatch_call(last_tile, n_used, pos, h2, *, tm, tile_rows, n_tiles):
    m = h2.shape[0]
    grid_spec = pltpu.PrefetchScalarGridSpec(
        num_scalar_prefetch=2,
        grid=(m // tm,),
        in_specs=[
            pl.BlockSpec((None, 1, 2 * tm), lambda i, lt, nu: (i, 0, 0), memory_space=pltpu.SMEM),
            pl.BlockSpec((tm, D_MODEL), lambda i, lt, nu: (i, 0)),
        ],
        out_specs=pl.BlockSpec(memory_space=pl.ANY),
        scratch_shapes=[
            pltpu.VMEM((tile_rows, D_MODEL), F32),
            pltpu.SemaphoreType.DMA(()),
            pltpu.SemaphoreType.DMA(()),
        ],
    )
    return pl.pallas_call(
        functools.partial(_dispatch_kernel, tile_rows=tile_rows, n_tiles=n_tiles),
        grid_spec=grid_spec,
        out_shape=jax.ShapeDtypeStruct((n_tiles * tile_rows, D_MODEL), F32),
        compiler_params=_params("arbitrary"),
        name="dispatch",
    )(last_tile, n_used, pos, h2)


def _experts_kernel(te_ref, nu_ref, x_ref, wg_ref, wu_ref, wd_ref, y_ref, wg_b, wu_b, wd_b):
    j = pl.program_id(0)
    n_used = nu_ref[0]

    @pl.when((j < n_used) & ((j == 0) | (te_ref[j] != te_ref[jnp.maximum(j - 1, 0)])))
    def _():
        wg_b[...] = wg_ref[...].astype(BF16)
        wu_b[...] = wu_ref[...].astype(BF16)
        wd_b[...] = wd_ref[...].astype(BF16)

    @pl.when(j < n_used)
    def _():
        xt = x_ref[...].astype(BF16)
        hid = (_silu(_dot(xt, wg_b[...])) * _dot(xt, wu_b[...])).astype(BF16)
        y_ref[...] = _dot(hid, wd_b[...])

    @pl.when(j >= n_used)
    def _():
        y_ref[...] = jnp.zeros_like(y_ref)


def _experts_call(tile_expert, n_used, x_sorted, wg, wu, wd, *, tm):
    n_tiles = tile_expert.shape[0]
    wspec = lambda shape: pl.BlockSpec((None,) + shape, lambda j, te, nu: (te[j], 0, 0))
    grid_spec = pltpu.PrefetchScalarGridSpec(
        num_scalar_prefetch=2,
        grid=(n_tiles,),
        in_specs=[
            pl.BlockSpec((tm, D_MODEL), lambda j, te, nu: (jnp.minimum(j, nu[0] - 1), 0)),
            wspec((D_MODEL, EXPERT_FF)),
            wspec((D_MODEL, EXPERT_FF)),
            wspec((EXPERT_FF, D_MODEL)),
        ],
        out_specs=pl.BlockSpec((tm, D_MODEL), lambda j, te, nu: (j, 0)),
        scratch_shapes=[
            pltpu.VMEM((D_MODEL, EXPERT_FF), BF16),
            pltpu.VMEM((D_MODEL, EXPERT_FF), BF16),
            pltpu.VMEM((EXPERT_FF, D_MODEL), BF16),
        ],
    )
    return pl.pallas_call(
        _experts_kernel,
        grid_spec=grid_spec,
        out_shape=jax.ShapeDtypeStruct((n_tiles * tm, D_MODEL), F32),
        compiler_params=_params("arbitrary"),
        name="experts",
    )(tile_expert, n_used, x_sorted, wg, wu, wd)


def _combine_kernel(pos_cur, pos_nxt, x1_ref, g2_ref, nfg_ref, rinfo_ref, y_hbm, o_ref, ybuf, sem):
    i = pl.program_id(0)
    tm = x1_ref.shape[0]
    slot = i % 2

    def row_copy(pos_ref, r, k, s):
        return pltpu.make_async_copy(y_hbm.at[pl.ds(pos_ref[0, 2 * r + k], 1), :], ybuf.at[s, k, pl.ds(r, 1), :],
                                     sem.at[s])

    def wait_tile(s):
        for k in range(2):
            pltpu.make_async_copy(y_hbm.at[pl.ds(0, tm), :], ybuf.at[s, k], sem.at[s]).wait()

    @pl.when(i == 0)
    def _():
        def body(r, carry):
            for k in range(2):
                row_copy(pos_cur, r, k, 0).start()
            return carry

        lax.fori_loop(0, tm, body, 0)

    wait_tile(slot)
    for r in range(tm):
        for k in range(2):
            row_copy(pos_nxt, r, k, 1 - slot).start(priority=k)
    info = rinfo_ref[...]
    w1 = info[:, RINFO_W1:RINFO_W1 + 1]
    w2 = info[:, RINFO_W2:RINFO_W2 + 1]
    moe = w1 * ybuf[slot, 0] + w2 * ybuf[slot, 1]
    y = x1_ref[...] + g2_ref[...] * moe
    ms = jnp.mean(y * y, axis=-1, keepdims=True)
    o_ref[...] = y * lax.rsqrt(ms + NORM_EPS) * nfg_ref[...]

    @pl.when(i == pl.num_programs(0) - 1)
    def _():
        wait_tile(1 - slot)


def _combine_call(pos, x1, mod3, normf_g, rinfo, y_sorted, *, seq, tm):
    m = x1.shape[0]
    n_tiles = m // tm
    tiles_per_batch = seq // tm
    return pl.pallas_call(
        _combine_kernel,
        grid=(n_tiles,),
        in_specs=[
            pl.BlockSpec((None, 1, 2 * tm), lambda i: (i, 0, 0), memory_space=pltpu.SMEM),
            pl.BlockSpec((None, 1, 2 * tm), lambda i: (jnp.minimum(i + 1, n_tiles - 1), 0, 0),
                         memory_space=pltpu.SMEM),
            pl.BlockSpec((tm, D_MODEL), lambda i: (i, 0)),
            pl.BlockSpec((None, 1, D_MODEL), lambda i: (i // tiles_per_batch, 0, 5)),
            pl.BlockSpec((1, D_MODEL), lambda i: (0, 0)),
            pl.BlockSpec((tm, LANES), lambda i: (i, 0)),
            pl.BlockSpec(memory_space=pl.ANY),
        ],
        out_specs=pl.BlockSpec((tm, D_MODEL), lambda i: (i, 0)),
        out_shape=jax.ShapeDtypeStruct((m, D_MODEL), F32),
        scratch_shapes=[
            pltpu.VMEM((2, 2, tm, D_MODEL), F32),
            pltpu.SemaphoreType.DMA((2,)),
        ],
        compiler_params=_params("arbitrary"),
        name="combine",
    )(pos, pos, x1, mod3, normf_g, rinfo, y_sorted)


def _dispatch_plan(rinfo, counts, *, tm):
    n_tok = rinfo.shape[0]
    n_tiles = (2 * n_tok + N_EXPERTS * (tm - 1)) // tm + 1
    expert = rinfo[:, RINFO_E1:RINFO_E2 + 1].astype(jnp.int32)
    rank = rinfo[:, RINFO_R1:RINFO_R2 + 1].astype(jnp.int32)
    cnt = counts[0, :N_EXPERTS].astype(jnp.int32)
    tiles_e = (cnt + tm - 1) // tm
    end_tile = jnp.cumsum(tiles_e)
    start_row = (end_tile - tiles_e) * tm
    is_e = expert[:, :, None] == jnp.arange(N_EXPERTS, dtype=jnp.int32)
    pos = jnp.sum(jnp.where(is_e, start_row, 0), axis=-1) + rank
    n_used = end_tile[-1:]
    tile_ids = jnp.minimum(jnp.arange(n_tiles, dtype=jnp.int32), n_used - 1)
    tile_expert = jnp.sum(tile_ids[:, None] >= end_tile[None, :], axis=1).astype(jnp.int32)
    last_tile = jnp.where(tiles_e > 0, end_tile - 1, -1).astype(jnp.int32)
    return pos, tile_expert, n_used.astype(jnp.int32), last_tile, n_tiles


def kernel(x, c, ctx, c_ctx, w_mod, b_mod, norm1_g, w_in, conv_w, conv_b, dt_bias_f, dt_bias_b, a_log_f, a_log_b,
           d_skip, ssd_norm_g, cm_ln_g, cm_ln_b, w_spatial, b_spatial, w_out, norm2_g, w_router_group,
           b_router_group, w_router_expert, b_router_expert, w_exp_gate, w_exp_up, w_exp_down, normf_g):
    bsz, seq, _ = x.shape
    ctx_len = ctx.shape[1]
    i = 0

    cc = jnp.concatenate([c, c_ctx[None, :], jnp.zeros((MOD_ROWS - bsz - 1, D_MODEL), F32)], axis=0)
    mod = _modulation(cc, w_mod[i], b_mod[i][None, :])
    mod3 = mod.reshape(MOD_ROWS, 1, N_MOD * D_MODEL)

    w_main, w_dt = _w_in_prep(jnp.swapaxes(w_in[i], 0, 1))
    g1row = norm1_g[i][None, :]

    x2 = x.reshape(bsz * seq, D_MODEL)
    ctx2 = ctx.reshape(bsz * ctx_len, D_MODEL)
    tm_in = 1024
    pm_x, dt_x = _in_proj(x2, mod3, lambda t: t // (seq // tm_in), g1row, w_main, w_dt, tm_in, 1024)
    pm_c, dt_c = _in_proj(ctx2, mod3, lambda t: bsz, g1row, w_main, w_dt, bsz * ctx_len, 1024)

    lane_pad = (0, LANES - 2 * SSD_HEADS)
    dt_bias_row = jnp.pad(jnp.concatenate([dt_bias_f[i], dt_bias_b[i]]).astype(F32), lane_pad)[None, :]
    a_row = jnp.pad(-jnp.exp(jnp.concatenate([a_log_f[i], a_log_b[i]]).astype(F32)), lane_pad)[None, :]
    dskip_g = jnp.repeat(d_skip[i], SSD_HEAD_DIM).reshape(SSD_GROUPS, 1, GROUP_WIDTH)
    cw = conv_w[i]
    cb = conv_b[i][None, :]
    h_zero = jnp.zeros((bsz, SSD_GROUPS, SSD_STATE, GROUP_WIDTH), F32)

    dt_terms_c = _ssd_dt(dt_c, dt_bias_row, a_row, rows_per_step=bsz * ctx_len)
    dt_terms_x = _ssd_dt(dt_x, dt_bias_row, a_row, rows_per_step=1024)
    hc_f, hc_b = _ssd(pm_c, dt_terms_c, cw, cb, dskip_g, h_zero, h_zero,
                      batch=bsz, seq=ctx_len, conv_tile=ctx_len, conv_period=ctx_len, with_output=False)
    yd = _ssd(pm_x, dt_terms_x, cw, cb, dskip_g, hc_f, hc_b,
              batch=bsz, seq=seq, conv_tile=CHUNK, conv_period=GRID_W, with_output=True)

    x1 = _out_proj(yd, pm_x, x2, mod3, ssd_norm_g[i][None, :], cm_ln_g[i][None, :], cm_ln_b[i][None, :],
                   w_spatial[i].astype(BF16), b_spatial[i].T, w_out[i].astype(BF16), seq=seq, tm=512, tn=1024)

    w_re = jnp.transpose(w_router_expert[i], (1, 0, 2)).reshape(D_MODEL, N_EXPERTS)
    pad = LANES - N_EXPERTS - N_GROUPS
    w_router = jnp.pad(jnp.concatenate([w_re, w_router_group[i]], axis=1), ((0, 0), (0, pad)))
    b_router = jnp.pad(jnp.concatenate([b_router_expert[i].reshape(-1), b_router_group[i]]), (0, pad))[None, :]
    h2, rinfo, counts = _route_call(x1, mod3, norm2_g[i][None, :], w_router, b_router, seq=seq, tm=512)

    tm_e = 256
    pos, tile_expert, n_used, last_tile, n_tiles = _dispatch_plan(rinfo, counts, tm=tm_e)
    tm_d = 512
    x_sorted = _dispatch_call(last_tile, n_used, pos.reshape(bsz * seq // tm_d, 1, 2 * tm_d), h2,
                              tm=tm_d, tile_rows=tm_e, n_tiles=n_tiles)
    wg = w_exp_gate[i].reshape(N_EXPERTS, D_MODEL, EXPERT_FF)
    wu = w_exp_up[i].reshape(N_EXPERTS, D_MODEL, EXPERT_FF)
    wd = w_exp_down[i].reshape(N_EXPERTS, EXPERT_FF, D_MODEL)
    y_sorted = _experts_call(tile_expert, n_used, x_sorted, wg, wu, wd, tm=tm_e)
    tm_c = 256
    out = _combine_call(pos.reshape(bsz * seq // tm_c, 1, 2 * tm_c), x1, mod3, normf_g[None, :], rinfo, y_sorted,
                        seq=seq, tm=tm_c)
    return out.reshape(bsz, seq, D_MODEL)
```

```python
import functools

import jax
import jax.numpy as jnp
from jax import lax
from jax.experimental import pallas as pl
from jax.experimental.pallas import tpu as pltpu

F32 = jnp.float32
BF16 = jnp.bfloat16
HIGHEST = lax.Precision.HIGHEST

D_MODEL = 2048
GRID_W = 64
SSD_WIDTH = 2048
CM_WIDTH = 2048
SSD_HEADS = 32
SSD_HEAD_DIM = 64
SSD_GROUPS = 4
HEADS_PER_GROUP = SSD_HEADS // SSD_GROUPS
GROUP_WIDTH = HEADS_PER_GROUP * SSD_HEAD_DIM
SSD_STATE = 128
CHUNK = 128
BC_WIDTH = SSD_GROUPS * SSD_STATE
CM_HEADS = 8
CM_HEAD_DIM = CM_WIDTH // CM_HEADS
N_GROUPS = 4
N_EXPERTS = 32
EXPERTS_PER_GROUP = 8
EXPERT_FF = 512
N_MOD = 6
NORM_EPS = 1e-6
LANES = 128
NEG_BIG = -1e30
MOD_ROWS = 8
VMEM_LIMIT = 56 * 1024 * 1024


def _params(*sem):
    return pltpu.CompilerParams(dimension_semantics=sem, vmem_limit_bytes=VMEM_LIMIT)


def _silu(v):
    return v * jax.nn.sigmoid(v)


def _dot(a, b):
    return jnp.dot(a, b, preferred_element_type=F32)


def _dot_nt(a, b):
    return lax.dot_general(a, b, (((1,), (1,)), ((), ())), preferred_element_type=F32)


def _mod_kernel(cc_ref, w_ref, b_ref, o_ref):
    a = _silu(cc_ref[...])
    hi = a.astype(BF16).astype(F32)
    lhs = jnp.concatenate([hi, a - hi], axis=0).astype(BF16)
    r = _dot(lhs, w_ref[...].astype(BF16))
    o_ref[...] = r[:MOD_ROWS] + r[MOD_ROWS:] + b_ref[...]


def _modulation(cc, w_mod, b_mod):
    n = w_mod.shape[1]
    tn = 1024
    return pl.pallas_call(
        _mod_kernel,
        grid=(n // tn,),
        in_specs=[
            pl.BlockSpec((MOD_ROWS, D_MODEL), lambda j: (0, 0)),
            pl.BlockSpec((D_MODEL, tn), lambda j: (0, j)),
            pl.BlockSpec((1, tn), lambda j: (0, j)),
        ],
        out_specs=pl.BlockSpec((MOD_ROWS, tn), lambda j: (0, j)),
        out_shape=jax.ShapeDtypeStruct((MOD_ROWS, n), F32),
        compiler_params=_params("arbitrary"),
        name="modulation",
    )(cc, w_mod, b_mod)


PREP_ROWS = 512
PREP_UV_BLK0 = 2 * SSD_WIDTH // PREP_ROWS
PREP_BC_BLK0 = (2 * SSD_WIDTH + 2 * CM_WIDTH) // PREP_ROWS
PREP_N_BLKS = PREP_BC_BLK0 + 2 * BC_WIDTH // PREP_ROWS
DT_COL0 = 2 * SSD_WIDTH + 2 * BC_WIDTH
UV_COL0 = DT_COL0 + 2 * SSD_HEADS


def _w_in_prep_kernel(a_ref, dt_ref, o_ref, wdt_ref):
    @pl.when(pl.program_id(0) == 0)
    def _():
        wdt_ref[...] = dt_ref[...].astype(BF16)

    o_ref[...] = a_ref[...].astype(BF16)


def _w_in_prep(w_in_t):
    def src_row(j):
        uv = UV_COL0 + (j - PREP_UV_BLK0) * PREP_ROWS
        bc = 2 * SSD_WIDTH + (j - PREP_BC_BLK0) * PREP_ROWS
        return pl.multiple_of(jnp.where(j < PREP_UV_BLK0, j * PREP_ROWS, jnp.where(j < PREP_BC_BLK0, uv, bc)), 8)

    return pl.pallas_call(
        _w_in_prep_kernel,
        grid=(PREP_N_BLKS,),
        in_specs=[
            pl.BlockSpec((pl.Element(PREP_ROWS), pl.Element(D_MODEL)), lambda j: (src_row(j), 0)),
            pl.BlockSpec((LANES, D_MODEL), lambda j: (DT_COL0 // LANES, 0)),
        ],
        out_specs=[
            pl.BlockSpec((PREP_ROWS, D_MODEL), lambda j: (j, 0)),
            pl.BlockSpec((LANES, D_MODEL), lambda j: (0, 0)),
        ],
        out_shape=[
            jax.ShapeDtypeStruct((PREP_N_BLKS * PREP_ROWS, D_MODEL), BF16),
            jax.ShapeDtypeStruct((LANES, D_MODEL), BF16),
        ],
        compiler_params=_params("arbitrary"),
        name="w_in_prep",
    )(w_in_t, w_in_t)


def _in_proj_kernel(x_ref, g_ref, sh_ref, sc_ref, w_ref, wdt_ref, o_ref, dt_ref, h_scr):
    tm = x_ref.shape[0]
    rc = 256

    @pl.when(pl.program_id(1) == 0)
    def _():
        gain = g_ref[...]
        scale = 1.0 + sc_ref[...]
        shift = sh_ref[...]

        def body(k, carry):
            r = pl.ds(pl.multiple_of(k * rc, rc), rc)
            xv = x_ref[r, :]
            ms = jnp.mean(xv * xv, axis=-1, keepdims=True)
            h = (xv * lax.rsqrt(ms + NORM_EPS) * gain) * scale + shift
            h_scr[r, :] = h.astype(BF16)
            return carry

        lax.fori_loop(0, tm // rc, body, 0)
        dt_ref[...] = _dot_nt(h_scr[...], wdt_ref[...])

    o_ref[...] = _dot_nt(h_scr[...], w_ref[...]).astype(o_ref.dtype)


def _in_proj(x2, mod3, mod_row_of_tile, norm_g, w_main, w_dt, tm, tn):
    m = x2.shape[0]
    n = w_main.shape[0]
    return pl.pallas_call(
        _in_proj_kernel,
        grid=(m // tm, n // tn),
        in_specs=[
            pl.BlockSpec((tm, D_MODEL), lambda i, j: (i, 0)),
            pl.BlockSpec((1, D_MODEL), lambda i, j: (0, 0)),
            pl.BlockSpec((None, 1, D_MODEL), lambda i, j: (mod_row_of_tile(i), 0, 0)),
            pl.BlockSpec((None, 1, D_MODEL), lambda i, j: (mod_row_of_tile(i), 0, 1)),
            pl.BlockSpec((tn, D_MODEL), lambda i, j: (j, 0)),
            pl.BlockSpec((LANES, D_MODEL), lambda i, j: (0, 0)),
        ],
        out_specs=[
            pl.BlockSpec((tm, tn), lambda i, j: (i, j)),
            pl.BlockSpec((tm, LANES), lambda i, j: (i, 0)),
        ],
        out_shape=[
            jax.ShapeDtypeStruct((m, n), BF16),
            jax.ShapeDtypeStruct((m, LANES), F32),
        ],
        scratch_shapes=[pltpu.VMEM((tm, D_MODEL), BF16)],
        compiler_params=_params("parallel", "arbitrary"),
        name="in_proj",
    )(x2, norm_g, mod3, mod3, w_main, w_dt)


def _split3(v):
    hi = v.astype(BF16)
    rem = v - hi.astype(F32)
    mid = rem.astype(BF16)
    lo = (rem - mid.astype(F32)).astype(BF16)
    return hi, mid, lo


GROUP_LANES = 2 * HEADS_PER_GROUP


def _stack_split3(v):
    hi, mid, lo = _split3(v)
    stacked = (hi.astype(F32) + pltpu.roll(mid.astype(F32), GROUP_LANES, 1)
               + pltpu.roll(lo.astype(F32), 2 * GROUP_LANES, 1))
    return stacked.astype(BF16)


LOG2_E = 1.4426950408889634


def _ssd_dt_kernel(dt_ref, bias_ref, a_ref, dts_ref, cum_ref, src_t_ref):
    hpg = HEADS_PER_GROUP
    ii = lax.broadcasted_iota(jnp.int32, (CHUNK, CHUNK), 0)
    jj = lax.broadcasted_iota(jnp.int32, (CHUNK, CHUNK), 1)
    tri = (ii >= jj).astype(BF16)
    tri3 = jnp.concatenate([tri, tri, tri], axis=1)
    bias = bias_ref[...]
    a_row = a_ref[...]

    def body(k, carry):
        r = pl.ds(pl.multiple_of(k * CHUNK, CHUNK), CHUNK)
        raw = dt_ref[r, :] + bias
        dts = jnp.where(jj < 2 * SSD_HEADS, jnp.maximum(raw, 0.0) + jnp.log1p(jnp.exp(-jnp.abs(raw))), 0.0)
        adt = dts * a_row
        hi, mid, lo = _split3(adt)
        cf = _dot(tri3, jnp.concatenate([hi, mid, lo], axis=0))
        cr = cf[CHUNK - 1:CHUNK, :] - cf + adt
        valid = jj < 2 * SSD_HEADS
        cum = jnp.where(jj < SSD_HEADS, cf, cr)
        src_t = ((cum - jnp.log(jnp.where(valid, dts, 1.0))) * LOG2_E).T
        rt = pl.ds(pl.multiple_of(k * GROUP_LANES, GROUP_LANES), GROUP_LANES)
        for g in range(SSD_GROUPS):
            f0, b0 = g * hpg, SSD_HEADS + g * hpg
            for src, dst in ((cum * LOG2_E, cum_ref), (dts, dts_ref)):
                lanes = jnp.where(jj < hpg, pltpu.roll(src, (LANES - f0) % LANES, 1),
                                  pltpu.roll(src, (LANES - (b0 - hpg)) % LANES, 1))
                dst[g, r, :] = jnp.where(jj < GROUP_LANES, lanes, 0.0)
            src_t_ref[g, rt, :] = jnp.concatenate([src_t[f0:f0 + hpg, :], src_t[b0:b0 + hpg, :]], axis=0)
        return carry

    lax.fori_loop(0, dt_ref.shape[0] // CHUNK, body, 0)


def _ssd_dt(dt, bias_row, a_row, *, rows_per_step):
    m = dt.shape[0]
    row = pl.BlockSpec((1, LANES), lambda i: (0, 0))
    full = pl.BlockSpec((SSD_GROUPS, rows_per_step, LANES), lambda i: (0, i, 0))
    tr = pl.BlockSpec((SSD_GROUPS, rows_per_step // CHUNK * GROUP_LANES, LANES), lambda i: (0, i, 0))
    full_shape = jax.ShapeDtypeStruct((SSD_GROUPS, m, LANES), F32)
    tr_shape = jax.ShapeDtypeStruct((SSD_GROUPS, m // CHUNK * GROUP_LANES, LANES), F32)
    return pl.pallas_call(
        _ssd_dt_kernel,
        grid=(m // rows_per_step,),
        in_specs=[pl.BlockSpec((rows_per_step, LANES), lambda i: (i, 0)), row, row],
        out_specs=[full, full, tr],
        out_shape=[full_shape, full_shape, tr_shape],
        compiler_params=_params("parallel"),
        name="ssd_dt",
    )(dt, bias_row, a_row)


def _ssd_kernel(xs_ref, b_ref, c_ref, dts_ref, cum_ref, src_t_ref, cwx_ref, cbx_ref, cwb_ref, cbb_ref,
                cwc_ref, cbc_ref, dskip_ref, h0f_ref, h0b_ref, *rest,
                n_chunks, conv_tile, conv_period, with_output):
    if with_output:
        y_ref, xs_s, b_s, c_s, hf_s, hb_s, y_s = rest
    else:
        hf_out, hb_out, xs_s, b_s, c_s, hf_s, hb_s = rest
    seq = n_chunks * CHUNK
    hpg = HEADS_PER_GROUP

    def conv_silu(src_ref, dst_ref, w_ref, bias_ref):
        cols = src_ref.shape[1]
        w0, w1, w2 = w_ref[0:1, :], w_ref[1:2, :], w_ref[2:3, :]
        bias = bias_ref[...]
        row = lax.broadcasted_iota(jnp.int32, (conv_tile, cols), 0) % conv_period
        first = row == 0
        last = row == conv_period - 1

        def body(k, carry):
            r = pl.ds(pl.multiple_of(k * conv_tile, conv_tile), conv_tile)
            v = src_ref[r, :].astype(F32)
            prev = jnp.where(first, 0.0, pltpu.roll(v, 1, 0))
            nxt = jnp.where(last, 0.0, pltpu.roll(v, conv_tile - 1, 0))
            y = bias + prev * w0 + v * w1 + nxt * w2
            dst_ref[r, :] = _silu(y).astype(BF16)
            return carry

        lax.fori_loop(0, seq // conv_tile, body, 0)

    conv_silu(xs_ref, xs_s, cwx_ref, cbx_ref)
    conv_silu(b_ref, b_s, cwb_ref, cbb_ref)
    conv_silu(c_ref, c_s, cwc_ref, cbc_ref)

    ii = lax.broadcasted_iota(jnp.int32, (CHUNK, CHUNK), 0)
    jj = lax.broadcasted_iota(jnp.int32, (CHUNK, CHUNK), 1)
    lower = ii >= jj
    upper = jj >= ii

    hf_s[...] = h0f_ref[...]
    hb_s[...] = h0b_ref[...]

    head_of_col = lax.broadcasted_iota(jnp.int32, (LANES, GROUP_WIDTH), 1) // SSD_HEAD_DIM
    sel_row = lax.broadcasted_iota(jnp.int32, (LANES, GROUP_WIDTH), 0)
    sel_valid = sel_row < 3 * GROUP_LANES
    sel_f = (sel_valid & (sel_row % GROUP_LANES == head_of_col)).astype(BF16)
    sel_b = (sel_valid & (sel_row % GROUP_LANES == head_of_col + hpg)).astype(BF16)
    low_half = jj < SSD_HEAD_DIM
    group_lane = jj < GROUP_LANES

    def direction_terms(r, sel):
        cum = cum_ref[r, :]
        dts = dts_ref[r, :]
        tot = jnp.where(jj[0:1, :] < hpg, cum[CHUNK - 1:CHUNK, :], cum[0:1, :])
        wst = dts * jnp.exp2(tot - cum)
        eoff = jnp.where(group_lane, jnp.exp2(cum), 0.0)
        edec = jnp.broadcast_to(jnp.where(group_lane[0:1, :], jnp.exp2(tot), 0.0), (8, LANES))
        ex = _dot(_stack_split3(jnp.concatenate([wst, eoff, edec], axis=0)), sel)
        return cum, ex[:CHUNK], ex[CHUNK:2 * CHUNK], ex[2 * CHUNK:2 * CHUNK + 1]

    def state_step(h_s, r, sel):
        cum, w_state, e_off, e_dec = direction_terms(r, sel)
        xc = xs_s[r, :]
        xw = (xc.astype(F32) * w_state).astype(BF16)
        s_new = lax.dot_general(b_s[r, :], xw, (((0,), (0,)), ((), ())), preferred_element_type=F32)
        h = h_s[...]
        y_off = _dot(c_s[r, :], h.astype(BF16)) * e_off
        h_s[...] = h * e_dec + s_new
        return cum, xc, y_off

    def fwd_body(k, carry):
        r = pl.ds(pl.multiple_of(k * CHUNK, CHUNK), CHUNK)
        cum, xc, y_off = state_step(hf_s, r, sel_f)
        if with_output:
            g = _dot_nt(c_s[r, :], b_s[r, :])
            rt = pl.ds(pl.multiple_of(k * GROUP_LANES, GROUP_LANES), GROUP_LANES)
            src_t = src_t_ref[rt, :]
            pieces = []
            for pair in range(hpg // 2):
                ms = []
                for h in (2 * pair, 2 * pair + 1):
                    hb = h + hpg
                    mf = jnp.exp2(jnp.where(lower, cum[:, h:h + 1] - src_t[h:h + 1, :], NEG_BIG))
                    mb = jnp.exp2(jnp.where(upper, cum[:, hb:hb + 1] - src_t[hb:hb + 1, :], NEG_BIG))
                    ms.append((g * (mf + mb)).astype(BF16))
                lhs = jnp.concatenate(ms, axis=1)
                xp = xc[:, pair * LANES:(pair + 1) * LANES]
                zero = jnp.zeros_like(xp)
                rhs = jnp.concatenate([jnp.where(low_half, xp, zero), jnp.where(low_half, zero, xp)], axis=0)
                pieces.append(_dot(lhs, rhs))
            y_diag = jnp.concatenate(pieces, axis=1)
            y_s[r, :] = y_diag + y_off + xc.astype(F32) * dskip_ref[...]
        return carry

    lax.fori_loop(0, n_chunks, fwd_body, 0)

    def bwd_body(k, carry):
        r = pl.ds(pl.multiple_of((n_chunks - 1 - k) * CHUNK, CHUNK), CHUNK)
        _, _, y_off = state_step(hb_s, r, sel_b)
        if with_output:
            y_ref[r, :] = (y_s[r, :] + y_off).astype(y_ref.dtype)
        return carry

    lax.fori_loop(0, n_chunks, bwd_body, 0)

    if not with_output:
        hf_out[...] = hf_s[...]
        hb_out[...] = hb_s[...]


def _ssd(pm, dt_terms, conv_w, conv_b, dskip_g, h0f, h0b, *, batch, seq, conv_tile, conv_period, with_output):
    n_chunks = seq // CHUNK
    dts_g, cum_g, src_t_g = dt_terms
    dt_spec = pl.BlockSpec((None, seq, LANES), lambda b, g: (g, b, 0))
    dt_t_spec = pl.BlockSpec((None, n_chunks * GROUP_LANES, LANES), lambda b, g: (g, b, 0))
    xs_blk0 = SSD_WIDTH // GROUP_WIDTH
    b_blk0 = (2 * SSD_WIDTH + 2 * CM_WIDTH) // SSD_STATE
    c_blk0 = b_blk0 + SSD_GROUPS
    cw_b0 = SSD_WIDTH // SSD_STATE
    cw_c0 = cw_b0 + SSD_GROUPS
    state_spec = pl.BlockSpec((None, None, SSD_STATE, GROUP_WIDTH), lambda b, g: (b, g, 0, 0))
    in_specs = [
        pl.BlockSpec((seq, GROUP_WIDTH), lambda b, g: (b, xs_blk0 + g)),
        pl.BlockSpec((seq, SSD_STATE), lambda b, g: (b, b_blk0 + g)),
        pl.BlockSpec((seq, SSD_STATE), lambda b, g: (b, c_blk0 + g)),
        dt_spec,
        dt_spec,
        dt_t_spec,
        pl.BlockSpec((3, GROUP_WIDTH), lambda b, g: (0, g)),
        pl.BlockSpec((1, GROUP_WIDTH), lambda b, g: (0, g)),
        pl.BlockSpec((3, SSD_STATE), lambda b, g: (0, cw_b0 + g)),
        pl.BlockSpec((1, SSD_STATE), lambda b, g: (0, cw_b0 + g)),
        pl.BlockSpec((3, SSD_STATE), lambda b, g: (0, cw_c0 + g)),
        pl.BlockSpec((1, SSD_STATE), lambda b, g: (0, cw_c0 + g)),
        pl.BlockSpec((None, 1, GROUP_WIDTH), lambda b, g: (g, 0, 0)),
        state_spec,
        state_spec,
    ]
    scratch = [
        pltpu.VMEM((seq, GROUP_WIDTH), BF16),
        pltpu.VMEM((seq, SSD_STATE), BF16),
        pltpu.VMEM((seq, SSD_STATE), BF16),
        pltpu.VMEM((SSD_STATE, GROUP_WIDTH), F32),
        pltpu.VMEM((SSD_STATE, GROUP_WIDTH), F32),
    ]
    if with_output:
        out_specs = pl.BlockSpec((seq, GROUP_WIDTH), lambda b, g: (b, g))
        out_shape = jax.ShapeDtypeStruct((batch * seq, SSD_WIDTH), BF16)
        scratch.append(pltpu.VMEM((seq, GROUP_WIDTH), F32))
    else:
        out_specs = [state_spec, state_spec]
        out_shape = [jax.ShapeDtypeStruct((batch, SSD_GROUPS, SSD_STATE, GROUP_WIDTH), F32)] * 2
    return pl.pallas_call(
        functools.partial(_ssd_kernel, n_chunks=n_chunks, conv_tile=conv_tile, conv_period=conv_period,
                          with_output=with_output),
        grid=(batch, SSD_GROUPS),
        in_specs=in_specs,
        out_specs=out_specs,
        out_shape=out_shape,
        scratch_shapes=scratch,
        compiler_params=_params("parallel", "parallel"),
        name="ssd_out" if with_output else "ssd_ctx",
    )(pm, pm, pm, dts_g, cum_g, src_t_g, conv_w, conv_b, conv_w, conv_b, conv_w, conv_b, dskip_g, h0f, h0b)


def _out_proj_kernel(yd_ref, z_ref, u_ref, v_ref, x_ref, g1_ref, ng_ref, lng_ref, lnb_ref, ws_ref, bst_ref, w_ref,
                     o_ref, mix_cur, mix_nxt):
    tm = yd_ref.shape[0]
    s = pl.program_id(0)
    n = pl.num_programs(0) - 1

    def build_mix():
        for k in range(tm // CHUNK):
            r = pl.ds(k * CHUNK, CHUNK)
            a = yd_ref[r, :].astype(F32) * _silu(z_ref[r, :].astype(F32))
            ms = jnp.mean(a * a, axis=-1, keepdims=True)
            mix_nxt[r, 0:SSD_WIDTH] = (a * lax.rsqrt(ms + NORM_EPS) * ng_ref[...]).astype(BF16)
            gv = jax.nn.gelu(v_ref[r, :].astype(F32))
            mu = jnp.mean(gv, axis=-1, keepdims=True)
            xc = gv - mu
            var = jnp.mean(xc * xc, axis=-1, keepdims=True)
            ln = ((xc * lax.rsqrt(var + NORM_EPS)) * lng_ref[...] + lnb_ref[...]).astype(BF16)
            gu = jax.nn.gelu(u_ref[r, :].astype(F32))
            for h in range(CM_HEADS):
                c0, c1 = h * CM_HEAD_DIM, (h + 1) * CM_HEAD_DIM
                sp = _dot(ws_ref[h], ln[:, c0:c1]) + bst_ref[:, h:h + 1]
                mix_nxt[r, SSD_WIDTH + c0:SSD_WIDTH + c1] = (gu[:, c0:c1] * sp).astype(BF16)

    def project():
        o_ref[...] = x_ref[...] + g1_ref[...] * _dot(mix_cur[...], w_ref[...])

    @pl.when(s == 0)
    def _():
        build_mix()

    @pl.when((s > 0) & (s < n))
    def _():
        project()
        build_mix()

    @pl.when(s == n)
    def _():
        project()

    @pl.when(s < n)
    def _():
        mix_cur[...] = mix_nxt[...]


def _out_proj(yd, pm, x2, mod3, ssd_norm_g, ln_g, ln_b, ws, bst, w_out, *, seq, tm):
    m = x2.shape[0]
    n = m // tm
    tiles_per_batch = seq // tm
    z_blk, u_blk, v_blk = 0, 2 * SSD_WIDTH // CM_WIDTH, 2 * SSD_WIDTH // CM_WIDTH + 1
    g1_blk = 2
    row = lambda shape: pl.BlockSpec(shape, lambda s: (0, 0))
    build_tile = lambda s: jnp.minimum(s, n - 1)
    proj_tile = lambda s: jnp.maximum(s - 1, 0)
    return pl.pallas_call(
        _out_proj_kernel,
        grid=(n + 1,),
        in_specs=[
            pl.BlockSpec((tm, SSD_WIDTH), lambda s: (build_tile(s), 0)),
            pl.BlockSpec((tm, SSD_WIDTH), lambda s: (build_tile(s), z_blk)),
            pl.BlockSpec((tm, CM_WIDTH), lambda s: (build_tile(s), u_blk)),
            pl.BlockSpec((tm, CM_WIDTH), lambda s: (build_tile(s), v_blk)),
            pl.BlockSpec((tm, D_MODEL), lambda s: (proj_tile(s), 0)),
            pl.BlockSpec((None, 1, D_MODEL), lambda s: (proj_tile(s) // tiles_per_batch, 0, g1_blk)),
            row((1, SSD_WIDTH)),
            row((1, CM_WIDTH)),
            row((1, CM_WIDTH)),
            pl.BlockSpec((CM_HEADS, CHUNK, CHUNK), lambda s: (0, 0, 0)),
            row((CHUNK, CM_HEADS)),
            pl.BlockSpec((SSD_WIDTH + CM_WIDTH, D_MODEL), lambda s: (0, 0), pipeline_mode=pl.Buffered(1)),
        ],
        out_specs=pl.BlockSpec((tm, D_MODEL), lambda s: (proj_tile(s), 0)),
        out_shape=jax.ShapeDtypeStruct((m, D_MODEL), F32),
        scratch_shapes=[pltpu.VMEM((tm, SSD_WIDTH + CM_WIDTH), BF16)] * 2,
        compiler_params=_params("arbitrary"),
        name="out_proj",
    )(yd, pm, pm, pm, x2, mod3, ssd_norm_g, ln_g, ln_b, ws, bst, w_out)


def _route(logits):
    lane = lax.broadcasted_iota(jnp.int32, logits.shape, 1)
    lane_f = lane.astype(F32)
    is_group = (lane >= N_EXPERTS) & (lane < N_EXPERTS + N_GROUPS)
    lg = jnp.where(is_group, logits, NEG_BIG)
    mg = jnp.max(lg, axis=1, keepdims=True)
    top_pg = 1.0 / jnp.sum(jnp.exp(lg - mg), axis=1, keepdims=True)
    gi = jnp.min(jnp.where(lg == mg, lane_f, 1e9), axis=1, keepdims=True) - N_EXPERTS
    in_group = (lane < N_EXPERTS) & ((lane // EXPERTS_PER_GROUP).astype(F32) == gi)
    le = jnp.where(in_group, logits, NEG_BIG)
    m1 = jnp.max(le, axis=1, keepdims=True)
    i1 = jnp.min(jnp.where(le == m1, lane_f, 1e9), axis=1, keepdims=True)
    le2 = jnp.where(lane_f == i1, NEG_BIG, le)
    m2 = jnp.max(le2, axis=1, keepdims=True)
    i2 = jnp.min(jnp.where(le2 == m2, lane_f, 1e9), axis=1, keepdims=True)
    e2 = jnp.exp(m2 - m1)
    p1 = 1.0 / (1.0 + e2)
    p2 = e2 * p1
    return i1, i2, p1 * top_pg, p2 * top_pg


RINFO_E1, RINFO_E2, RINFO_W1, RINFO_W2, RINFO_R1, RINFO_R2 = range(6)


def _route_kernel(x1_ref, n2g_ref, sh2_ref, sc2_ref, wr_ref, br_ref, h2_ref, rinfo_ref, counts_ref, cnt_scr):
    tm = x1_ref.shape[0]

    @pl.when(pl.program_id(0) == 0)
    def _():
        cnt_scr[...] = jnp.zeros_like(cnt_scr)

    x1 = x1_ref[...]
    ms = jnp.mean(x1 * x1, axis=-1, keepdims=True)
    h2 = (x1 * lax.rsqrt(ms + NORM_EPS) * n2g_ref[...]) * (1.0 + sc2_ref[...]) + sh2_ref[...]
    h2_ref[...] = h2
    logits = jnp.dot(h2, wr_ref[...], precision=HIGHEST, preferred_element_type=F32) + br_ref[...]
    i1, i2, w1, w2 = _route(logits)

    lane = lax.broadcasted_iota(jnp.int32, (tm, LANES), 1)
    lane_f = lane.astype(F32)
    oh1 = jnp.where(lane_f == i1, 1.0, 0.0)
    oh2 = jnp.where(lane_f == i2, 1.0, 0.0)
    before = (lax.broadcasted_iota(jnp.int32, (tm, tm), 0) > lax.broadcasted_iota(jnp.int32, (tm, tm), 1)).astype(BF16)
    carried = cnt_scr[...]
    tot1 = jnp.sum(oh1, axis=0, keepdims=True)
    r1 = jnp.sum(oh1 * (_dot(before, oh1.astype(BF16)) + carried), axis=1, keepdims=True)
    r2 = jnp.sum(oh2 * (_dot(before, oh2.astype(BF16)) + (carried + tot1)), axis=1, keepdims=True)
    counts = carried + tot1 + jnp.sum(oh2, axis=0, keepdims=True)
    cnt_scr[...] = counts
    counts_ref[...] = jnp.broadcast_to(counts, counts_ref.shape)

    info = jnp.zeros((tm, LANES), F32)
    for k, v in ((RINFO_E1, i1), (RINFO_E2, i2), (RINFO_W1, w1), (RINFO_W2, w2), (RINFO_R1, r1), (RINFO_R2, r2)):
        info = jnp.where(lane == k, v, info)
    rinfo_ref[...] = info


def _route_call(x1, mod3, norm2_g, w_router, b_router, *, seq, tm):
    m = x1.shape[0]
    tiles_per_batch = seq // tm
    row = lambda shape: pl.BlockSpec(shape, lambda i: (0, 0))
    modrow = lambda k: pl.BlockSpec((None, 1, D_MODEL), lambda i: (i // tiles_per_batch, 0, k))
    return pl.pallas_call(
        _route_kernel,
        grid=(m // tm,),
        in_specs=[
            pl.BlockSpec((tm, D_MODEL), lambda i: (i, 0)),
            row((1, D_MODEL)),
            modrow(3),
            modrow(4),
            row((D_MODEL, LANES)),
            row((1, LANES)),
        ],
        out_specs=[
            pl.BlockSpec((tm, D_MODEL), lambda i: (i, 0)),
            pl.BlockSpec((tm, LANES), lambda i: (i, 0)),
            row((8, LANES)),
        ],
        out_shape=[
            jax.ShapeDtypeStruct((m, D_MODEL), F32),
            jax.ShapeDtypeStruct((m, LANES), F32),
            jax.ShapeDtypeStruct((8, LANES), F32),
        ],
        scratch_shapes=[pltpu.VMEM((1, LANES), F32)],
        compiler_params=_params("arbitrary"),
        name="route",
    )(x1, norm2_g, mod3, mod3, w_router, b_router)


def _dispatch_kernel(lt_ref, nu_ref, pos_ref, h2_ref, x_hbm, zbuf, zsem, sem, *, tile_rows, n_tiles):
    tm = h2_ref.shape[0]
    n_used = nu_ref[0]

    def zero_copy(t):
        return pltpu.make_async_copy(zbuf, x_hbm.at[pl.ds(pl.multiple_of(t * tile_rows, tile_rows), tile_rows), :], zsem)

    @pl.when(pl.program_id(0) == 0)
    def _():
        zbuf[...] = jnp.zeros_like(zbuf)
        for phase in ("start", "wait"):
            for e in range(N_EXPERTS):
                @pl.when(lt_ref[e] >= 0)
                def _():
                    getattr(zero_copy(lt_ref[e]), phase)()

            def unused(t, carry):
                getattr(zero_copy(t), phase)()
                return carry

            lax.fori_loop(n_used, n_tiles, unused, 0)

    def body(r, carry):
        for k in range(2):
            pltpu.make_async_copy(h2_ref.at[pl.ds(r, 1), :], x_hbm.at[pl.ds(pos_ref[0, 2 * r + k], 1), :], sem).start()
        return carry

    lax.fori_loop(0, tm, body, 0, unroll=8)
    for k in range(2):
        pltpu.make_async_copy(h2_ref, x_hbm.at[pl.ds(0, tm), :], sem).wait()


def _dispatch_call(last_tile, n_used, pos, h2, *, tm, tile_rows, n_tiles):
    m = h2.shape[0]
    grid_spec = pltpu.PrefetchScalarGridSpec(
        num_scalar_prefetch=2,
        grid=(m // tm,),
        in_specs=[
            pl.BlockSpec((None, 1, 2 * tm), lambda i, lt, nu: (i, 0, 0), memory_space=pltpu.SMEM),
            pl.BlockSpec((tm, D_MODEL), lambda i, lt, nu: (i, 0)),
        ],
        out_specs=pl.BlockSpec(memory_space=pl.ANY),
        scratch_shapes=[
            pltpu.VMEM((tile_rows, D_MODEL), F32),
            pltpu.SemaphoreType.DMA(()),
            pltpu.SemaphoreType.DMA(()),
        ],
    )
    return pl.pallas_call(
        functools.partial(_dispatch_kernel, tile_rows=tile_rows, n_tiles=n_tiles),
        grid_spec=grid_spec,
        out_shape=jax.ShapeDtypeStruct((n_tiles * tile_rows, D_MODEL), F32),
        compiler_params=_params("arbitrary"),
        name="dispatch",
    )(last_tile, n_used, pos, h2)


def _experts_kernel(te_ref, nu_ref, x_ref, wg_ref, wu_ref, wd_ref, y_ref, wg_b, wu_b, wd_b):
    j = pl.program_id(0)
    n_used = nu_ref[0]

    @pl.when((j < n_used) & ((j == 0) | (te_ref[j] != te_ref[jnp.maximum(j - 1, 0)])))
    def _():
        wg_b[...] = wg_ref[...].astype(BF16)
        wu_b[...] = wu_ref[...].astype(BF16)
        wd_b[...] = wd_ref[...].astype(BF16)

    @pl.when(j < n_used)
    def _():
        xt = x_ref[...].astype(BF16)
        hid = (_silu(_dot(xt, wg_b[...])) * _dot(xt, wu_b[...])).astype(BF16)
        y_ref[...] = _dot(hid, wd_b[...])

    @pl.when(j >= n_used)
    def _():
        y_ref[...] = jnp.zeros_like(y_ref)


def _experts_call(tile_expert, n_used, x_sorted, wg, wu, wd, *, tm):
    n_tiles = tile_expert.shape[0]
    wspec = lambda shape: pl.BlockSpec((None,) + shape, lambda j, te, nu: (te[j], 0, 0))
    grid_spec = pltpu.PrefetchScalarGridSpec(
        num_scalar_prefetch=2,
        grid=(n_tiles,),
        in_specs=[
            pl.BlockSpec((tm, D_MODEL), lambda j, te, nu: (jnp.minimum(j, nu[0] - 1), 0)),
            wspec((D_MODEL, EXPERT_FF)),
            wspec((D_MODEL, EXPERT_FF)),
            wspec((EXPERT_FF, D_MODEL)),
        ],
        out_specs=pl.BlockSpec((tm, D_MODEL), lambda j, te, nu: (j, 0)),
        scratch_shapes=[
            pltpu.VMEM((D_MODEL, EXPERT_FF), BF16),
            pltpu.VMEM((D_MODEL, EXPERT_FF), BF16),
            pltpu.VMEM((EXPERT_FF, D_MODEL), BF16),
        ],
    )
    return pl.pallas_call(
        _experts_kernel,
        grid_spec=grid_spec,
        out_shape=jax.ShapeDtypeStruct((n_tiles * tm, D_MODEL), F32),
        compiler_params=_params("arbitrary"),
        name="experts",
    )(tile_expert, n_used, x_sorted, wg, wu, wd)


def _combine_kernel(pos_cur, pos_nxt, x1_ref, g2_ref, nfg_ref, rinfo_ref, y_hbm, o_ref, ybuf, sem):
    i = pl.program_id(0)
    tm = x1_ref.shape[0]
    slot = i % 2

    def row_copy(pos_ref, r, k, s):
        return pltpu.make_async_copy(y_hbm.at[pl.ds(pos_ref[0, 2 * r + k], 1), :], ybuf.at[s, k, pl.ds(r, 1), :],
                                     sem.at[s])

    def wait_tile(s):
        for k in range(2):
            pltpu.make_async_copy(y_hbm.at[pl.ds(0, tm), :], ybuf.at[s, k], sem.at[s]).wait()

    @pl.when(i == 0)
    def _():
        def body(r, carry):
            for k in range(2):
                row_copy(pos_cur, r, k, 0).start()
            return carry

        lax.fori_loop(0, tm, body, 0)

    wait_tile(slot)
    for r in range(tm):
        for k in range(2):
            row_copy(pos_nxt, r, k, 1 - slot).start(priority=k)
    info = rinfo_ref[...]
    w1 = info[:, RINFO_W1:RINFO_W1 + 1]
    w2 = info[:, RINFO_W2:RINFO_W2 + 1]
    moe = w1 * ybuf[slot, 0] + w2 * ybuf[slot, 1]
    y = x1_ref[...] + g2_ref[...] * moe
    ms = jnp.mean(y * y, axis=-1, keepdims=True)
    o_ref[...] = y * lax.rsqrt(ms + NORM_EPS) * nfg_ref[...]

    @pl.when(i == pl.num_programs(0) - 1)
    def _():
        wait_tile(1 - slot)


def _combine_call(pos, x1, mod3, normf_g, rinfo, y_sorted, *, seq, tm):
    m = x1.shape[0]
    n_tiles = m // tm
    tiles_per_batch = seq // tm
    return pl.pallas_call(
        _combine_kernel,
        grid=(n_tiles,),
        in_specs=[
            pl.BlockSpec((None, 1, 2 * tm), lambda i: (i, 0, 0), memory_space=pltpu.SMEM),
            pl.BlockSpec((None, 1, 2 * tm), lambda i: (jnp.minimum(i + 1, n_tiles - 1), 0, 0),
                         memory_space=pltpu.SMEM),
            pl.BlockSpec((tm, D_MODEL), lambda i: (i, 0)),
            pl.BlockSpec((None, 1, D_MODEL), lambda i: (i // tiles_per_batch, 0, 5)),
            pl.BlockSpec((1, D_MODEL), lambda i: (0, 0)),
            pl.BlockSpec((tm, LANES), lambda i: (i, 0)),
            pl.BlockSpec(memory_space=pl.ANY),
        ],
        out_specs=pl.BlockSpec((tm, D_MODEL), lambda i: (i, 0)),
        out_shape=jax.ShapeDtypeStruct((m, D_MODEL), F32),
        scratch_shapes=[
            pltpu.VMEM((2, 2, tm, D_MODEL), F32),
            pltpu.SemaphoreType.DMA((2,)),
        ],
        compiler_params=_params("arbitrary"),
        name="combine",
    )(pos, pos, x1, mod3, normf_g, rinfo, y_sorted)


def _dispatch_plan(rinfo, counts, *, tm):
    n_tok = rinfo.shape[0]
    n_tiles = (2 * n_tok + N_EXPERTS * (tm - 1)) // tm + 1
    expert = rinfo[:, RINFO_E1:RINFO_E2 + 1].astype(jnp.int32)
    rank = rinfo[:, RINFO_R1:RINFO_R2 + 1].astype(jnp.int32)
    cnt = counts[0, :N_EXPERTS].astype(jnp.int32)
    tiles_e = (cnt + tm - 1) // tm
    end_tile = jnp.cumsum(tiles_e)
    start_row = (end_tile - tiles_e) * tm
    is_e = expert[:, :, None] == jnp.arange(N_EXPERTS, dtype=jnp.int32)
    pos = jnp.sum(jnp.where(is_e, start_row, 0), axis=-1) + rank
    n_used = end_tile[-1:]
    tile_ids = jnp.minimum(jnp.arange(n_tiles, dtype=jnp.int32), n_used - 1)
    tile_expert = jnp.sum(tile_ids[:, None] >= end_tile[None, :], axis=1).astype(jnp.int32)
    last_tile = jnp.where(tiles_e > 0, end_tile - 1, -1).astype(jnp.int32)
    return pos, tile_expert, n_used.astype(jnp.int32), last_tile, n_tiles


def kernel(x, c, ctx, c_ctx, w_mod, b_mod, norm1_g, w_in, conv_w, conv_b, dt_bias_f, dt_bias_b, a_log_f, a_log_b,
           d_skip, ssd_norm_g, cm_ln_g, cm_ln_b, w_spatial, b_spatial, w_out, norm2_g, w_router_group,
           b_router_group, w_router_expert, b_router_expert, w_exp_gate, w_exp_up, w_exp_down, normf_g):
    bsz, seq, _ = x.shape
    ctx_len = ctx.shape[1]
    i = 0

    cc = jnp.concatenate([c, c_ctx[None, :], jnp.zeros((MOD_ROWS - bsz - 1, D_MODEL), F32)], axis=0)
    mod = _modulation(cc, w_mod[i], b_mod[i][None, :])
    mod3 = mod.reshape(MOD_ROWS, 1, N_MOD * D_MODEL)

    w_main, w_dt = _w_in_prep(jnp.swapaxes(w_in[i], 0, 1))
    g1row = norm1_g[i][None, :]

    x2 = x.reshape(bsz * seq, D_MODEL)
    ctx2 = ctx.reshape(bsz * ctx_len, D_MODEL)
    tm_in = 1024
    pm_x, dt_x = _in_proj(x2, mod3, lambda t: t // (seq // tm_in), g1row, w_main, w_dt, tm_in, 1024)
    pm_c, dt_c = _in_proj(ctx2, mod3, lambda t: bsz, g1row, w_main, w_dt, bsz * ctx_len, 1024)

    lane_pad = (0, LANES - 2 * SSD_HEADS)
    dt_bias_row = jnp.pad(jnp.concatenate([dt_bias_f[i], dt_bias_b[i]]).astype(F32), lane_pad)[None, :]
    a_row = jnp.pad(-jnp.exp(jnp.concatenate([a_log_f[i], a_log_b[i]]).astype(F32)), lane_pad)[None, :]
    dskip_g = jnp.repeat(d_skip[i], SSD_HEAD_DIM).reshape(SSD_GROUPS, 1, GROUP_WIDTH)
    cw = conv_w[i]
    cb = conv_b[i][None, :]
    h_zero = jnp.zeros((bsz, SSD_GROUPS, SSD_STATE, GROUP_WIDTH), F32)

    dt_terms_c = _ssd_dt(dt_c, dt_bias_row, a_row, rows_per_step=bsz * ctx_len)
    dt_terms_x = _ssd_dt(dt_x, dt_bias_row, a_row, rows_per_step=1024)
    hc_f, hc_b = _ssd(pm_c, dt_terms_c, cw, cb, dskip_g, h_zero, h_zero,
                      batch=bsz, seq=ctx_len, conv_tile=ctx_len, conv_period=ctx_len, with_output=False)
    yd = _ssd(pm_x, dt_terms_x, cw, cb, dskip_g, hc_f, hc_b,
              batch=bsz, seq=seq, conv_tile=CHUNK, conv_period=GRID_W, with_output=True)

    x1 = _out_proj(yd, pm_x, x2, mod3, ssd_norm_g[i][None, :], cm_ln_g[i][None, :], cm_ln_b[i][None, :],
                   w_spatial[i].astype(BF16), b_spatial[i].T, w_out[i].astype(BF16), seq=seq, tm=256)

    w_re = jnp.transpose(w_router_expert[i], (1, 0, 2)).reshape(D_MODEL, N_EXPERTS)
    pad = LANES - N_EXPERTS - N_GROUPS
    w_router = jnp.pad(jnp.concatenate([w_re, w_router_group[i]], axis=1), ((0, 0), (0, pad)))
    b_router = jnp.pad(jnp.concatenate([b_router_expert[i].reshape(-1), b_router_group[i]]), (0, pad))[None, :]
    h2, rinfo, counts = _route_call(x1, mod3, norm2_g[i][None, :], w_router, b_router, seq=seq, tm=512)

    tm_e = 256
    pos, tile_expert, n_used, last_tile, n_tiles = _dispatch_plan(rinfo, counts, tm=tm_e)
    tm_d = 512
    x_sorted = _dispatch_call(last_tile, n_used, pos.reshape(bsz * seq // tm_d, 1, 2 * tm_d), h2,
                              tm=tm_d, tile_rows=tm_e, n_tiles=n_tiles)
    wg = w_exp_gate[i].reshape(N_EXPERTS, D_MODEL, EXPERT_FF)
    wu = w_exp_up[i].reshape(N_EXPERTS, D_MODEL, EXPERT_FF)
    wd = w_exp_down[i].reshape(N_EXPERTS, EXPERT_FF, D_MODEL)
    y_sorted = _experts_call(tile_expert, n_used, x_sorted, wg, wu, wd, tm=tm_e)
    tm_c = 256
    out = _combine_call(pos.reshape(bsz * seq // tm_c, 1, 2 * tm_c), x1, mod3, normf_g[None, :], rinfo, y_sorted,
                        seq=seq, tm=tm_c)
    return out.reshape(bsz, seq, D_MODEL)
```

```python
import functools

import jax
import jax.numpy as jnp
from jax import lax
from jax.experimental import pallas as pl
from jax.experimental.pallas import tpu as pltpu

F32 = jnp.float32
BF16 = jnp.bfloat16
HIGHEST = lax.Precision.HIGHEST

D_MODEL = 2048
GRID_W = 64
SSD_WIDTH = 2048
CM_WIDTH = 2048
SSD_HEADS = 32
SSD_HEAD_DIM = 64
SSD_GROUPS = 4
HEADS_PER_GROUP = SSD_HEADS // SSD_GROUPS
GROUP_WIDTH = HEADS_PER_GROUP * SSD_HEAD_DIM
SSD_STATE = 128
CHUNK = 128
BC_WIDTH = SSD_GROUPS * SSD_STATE
CM_HEADS = 8
CM_HEAD_DIM = CM_WIDTH // CM_HEADS
N_GROUPS = 4
N_EXPERTS = 32
EXPERTS_PER_GROUP = 8
EXPERT_FF = 512
N_MOD = 6
NORM_EPS = 1e-6
LANES = 128
NEG_BIG = -1e30
MOD_ROWS = 8
VMEM_LIMIT = 56 * 1024 * 1024


def _params(*sem):
    return pltpu.CompilerParams(dimension_semantics=sem, vmem_limit_bytes=VMEM_LIMIT)


def _silu(v):
    return v * jax.nn.sigmoid(v)


def _dot(a, b):
    return jnp.dot(a, b, preferred_element_type=F32)


def _dot_nt(a, b):
    return lax.dot_general(a, b, (((1,), (1,)), ((), ())), preferred_element_type=F32)


def _mod_kernel(cc_ref, w_ref, b_ref, o_ref):
    a = _silu(cc_ref[...])
    hi = a.astype(BF16).astype(F32)
    lhs = jnp.concatenate([hi, a - hi], axis=0).astype(BF16)
    r = _dot(lhs, w_ref[...].astype(BF16))
    o_ref[...] = r[:MOD_ROWS] + r[MOD_ROWS:] + b_ref[...]


def _modulation(cc, w_mod, b_mod):
    n = w_mod.shape[1]
    tn = 1024
    return pl.pallas_call(
        _mod_kernel,
        grid=(n // tn,),
        in_specs=[
            pl.BlockSpec((MOD_ROWS, D_MODEL), lambda j: (0, 0)),
            pl.BlockSpec((D_MODEL, tn), lambda j: (0, j)),
            pl.BlockSpec((1, tn), lambda j: (0, j)),
        ],
        out_specs=pl.BlockSpec((MOD_ROWS, tn), lambda j: (0, j)),
        out_shape=jax.ShapeDtypeStruct((MOD_ROWS, n), F32),
        compiler_params=_params("arbitrary"),
        name="modulation",
    )(cc, w_mod, b_mod)


PREP_ROWS = 512
PREP_UV_BLK0 = 2 * SSD_WIDTH // PREP_ROWS
PREP_BC_BLK0 = (2 * SSD_WIDTH + 2 * CM_WIDTH) // PREP_ROWS
PREP_N_BLKS = PREP_BC_BLK0 + 2 * BC_WIDTH // PREP_ROWS
DT_COL0 = 2 * SSD_WIDTH + 2 * BC_WIDTH
UV_COL0 = DT_COL0 + 2 * SSD_HEADS


def _w_in_prep_kernel(a_ref, dt_ref, o_ref, wdt_ref):
    @pl.when(pl.program_id(0) == 0)
    def _():
        wdt_ref[...] = dt_ref[...].astype(BF16)

    o_ref[...] = a_ref[...].astype(BF16)


def _w_in_prep(w_in_t):
    def src_row(j):
        uv = UV_COL0 + (j - PREP_UV_BLK0) * PREP_ROWS
        bc = 2 * SSD_WIDTH + (j - PREP_BC_BLK0) * PREP_ROWS
        return pl.multiple_of(jnp.where(j < PREP_UV_BLK0, j * PREP_ROWS, jnp.where(j < PREP_BC_BLK0, uv, bc)), 8)

    return pl.pallas_call(
        _w_in_prep_kernel,
        grid=(PREP_N_BLKS,),
        in_specs=[
            pl.BlockSpec((pl.Element(PREP_ROWS), pl.Element(D_MODEL)), lambda j: (src_row(j), 0)),
            pl.BlockSpec((LANES, D_MODEL), lambda j: (DT_COL0 // LANES, 0)),
        ],
        out_specs=[
            pl.BlockSpec((PREP_ROWS, D_MODEL), lambda j: (j, 0)),
            pl.BlockSpec((LANES, D_MODEL), lambda j: (0, 0)),
        ],
        out_shape=[
            jax.ShapeDtypeStruct((PREP_N_BLKS * PREP_ROWS, D_MODEL), BF16),
            jax.ShapeDtypeStruct((LANES, D_MODEL), BF16),
        ],
        compiler_params=_params("arbitrary"),
        name="w_in_prep",
    )(w_in_t, w_in_t)


def _in_proj_kernel(x_ref, g_ref, sh_ref, sc_ref, w_ref, wdt_ref, o_ref, dt_ref, h_scr):
    tm = x_ref.shape[0]
    rc = 256

    @pl.when(pl.program_id(1) == 0)
    def _():
        gain = g_ref[...]
        scale = 1.0 + sc_ref[...]
        shift = sh_ref[...]

        def body(k, carry):
            r = pl.ds(pl.multiple_of(k * rc, rc), rc)
            xv = x_ref[r, :]
            ms = jnp.mean(xv * xv, axis=-1, keepdims=True)
            h = (xv * lax.rsqrt(ms + NORM_EPS) * gain) * scale + shift
            h_scr[r, :] = h.astype(BF16)
            return carry

        lax.fori_loop(0, tm // rc, body, 0)
        dt_ref[...] = _dot_nt(h_scr[...], wdt_ref[...])

    o_ref[...] = _dot_nt(h_scr[...], w_ref[...]).astype(o_ref.dtype)


def _in_proj(x2, mod3, mod_row_of_tile, norm_g, w_main, w_dt, tm, tn):
    m = x2.shape[0]
    n = w_main.shape[0]
    return pl.pallas_call(
        _in_proj_kernel,
        grid=(m // tm, n // tn),
        in_specs=[
            pl.BlockSpec((tm, D_MODEL), lambda i, j: (i, 0)),
            pl.BlockSpec((1, D_MODEL), lambda i, j: (0, 0)),
            pl.BlockSpec((None, 1, D_MODEL), lambda i, j: (mod_row_of_tile(i), 0, 0)),
            pl.BlockSpec((None, 1, D_MODEL), lambda i, j: (mod_row_of_tile(i), 0, 1)),
            pl.BlockSpec((tn, D_MODEL), lambda i, j: (j, 0)),
            pl.BlockSpec((LANES, D_MODEL), lambda i, j: (0, 0)),
        ],
        out_specs=[
            pl.BlockSpec((tm, tn), lambda i, j: (i, j)),
            pl.BlockSpec((tm, LANES), lambda i, j: (i, 0)),
        ],
        out_shape=[
            jax.ShapeDtypeStruct((m, n), BF16),
            jax.ShapeDtypeStruct((m, LANES), F32),
        ],
        scratch_shapes=[pltpu.VMEM((tm, D_MODEL), BF16)],
        compiler_params=_params("parallel", "arbitrary"),
        name="in_proj",
    )(x2, norm_g, mod3, mod3, w_main, w_dt)


def _split3(v):
    hi = v.astype(BF16)
    rem = v - hi.astype(F32)
    mid = rem.astype(BF16)
    lo = (rem - mid.astype(F32)).astype(BF16)
    return hi, mid, lo


GROUP_LANES = 2 * HEADS_PER_GROUP


def _stack_split3(v):
    hi, mid, lo = _split3(v)
    stacked = (hi.astype(F32) + pltpu.roll(mid.astype(F32), GROUP_LANES, 1)
               + pltpu.roll(lo.astype(F32), 2 * GROUP_LANES, 1))
    return stacked.astype(BF16)


LOG2_E = 1.4426950408889634


def _ssd_dt_kernel(dt_ref, bias_ref, a_ref, dts_ref, cum_ref, src_t_ref):
    hpg = HEADS_PER_GROUP
    ii = lax.broadcasted_iota(jnp.int32, (CHUNK, CHUNK), 0)
    jj = lax.broadcasted_iota(jnp.int32, (CHUNK, CHUNK), 1)
    tri = (ii >= jj).astype(BF16)
    tri3 = jnp.concatenate([tri, tri, tri], axis=1)
    bias = bias_ref[...]
    a_row = a_ref[...]

    def body(k, carry):
        r = pl.ds(pl.multiple_of(k * CHUNK, CHUNK), CHUNK)
        raw = dt_ref[r, :] + bias
        dts = jnp.where(jj < 2 * SSD_HEADS, jnp.maximum(raw, 0.0) + jnp.log1p(jnp.exp(-jnp.abs(raw))), 0.0)
        adt = dts * a_row
        hi, mid, lo = _split3(adt)
        cf = _dot(tri3, jnp.concatenate([hi, mid, lo], axis=0))
        cr = cf[CHUNK - 1:CHUNK, :] - cf + adt
        valid = jj < 2 * SSD_HEADS
        cum = jnp.where(jj < SSD_HEADS, cf, cr)
        src_t = ((cum - jnp.log(jnp.where(valid, dts, 1.0))) * LOG2_E).T
        rt = pl.ds(pl.multiple_of(k * GROUP_LANES, GROUP_LANES), GROUP_LANES)
        for g in range(SSD_GROUPS):
            f0, b0 = g * hpg, SSD_HEADS + g * hpg
            for src, dst in ((cum * LOG2_E, cum_ref), (dts, dts_ref)):
                lanes = jnp.where(jj < hpg, pltpu.roll(src, (LANES - f0) % LANES, 1),
                                  pltpu.roll(src, (LANES - (b0 - hpg)) % LANES, 1))
                dst[g, r, :] = jnp.where(jj < GROUP_LANES, lanes, 0.0)
            src_t_ref[g, rt, :] = jnp.concatenate([src_t[f0:f0 + hpg, :], src_t[b0:b0 + hpg, :]], axis=0)
        return carry

    lax.fori_loop(0, dt_ref.shape[0] // CHUNK, body, 0)


def _ssd_dt(dt, bias_row, a_row, *, rows_per_step):
    m = dt.shape[0]
    row = pl.BlockSpec((1, LANES), lambda i: (0, 0))
    full = pl.BlockSpec((SSD_GROUPS, rows_per_step, LANES), lambda i: (0, i, 0))
    tr = pl.BlockSpec((SSD_GROUPS, rows_per_step // CHUNK * GROUP_LANES, LANES), lambda i: (0, i, 0))
    full_shape = jax.ShapeDtypeStruct((SSD_GROUPS, m, LANES), F32)
    tr_shape = jax.ShapeDtypeStruct((SSD_GROUPS, m // CHUNK * GROUP_LANES, LANES), F32)
    return pl.pallas_call(
        _ssd_dt_kernel,
        grid=(m // rows_per_step,),
        in_specs=[pl.BlockSpec((rows_per_step, LANES), lambda i: (i, 0)), row, row],
        out_specs=[full, full, tr],
        out_shape=[full_shape, full_shape, tr_shape],
        compiler_params=_params("parallel"),
        name="ssd_dt",
    )(dt, bias_row, a_row)


def _ssd_kernel(xs_ref, b_ref, c_ref, dts_ref, cum_ref, src_t_ref, cwx_ref, cbx_ref, cwb_ref, cbb_ref,
                cwc_ref, cbc_ref, dskip_ref, h0f_ref, h0b_ref, *rest,
                n_chunks, conv_tile, conv_period, with_output):
    if with_output:
        y_ref, xs_s, b_s, c_s, hf_s, hb_s, y_s = rest
    else:
        hf_out, hb_out, xs_s, b_s, c_s, hf_s, hb_s = rest
    seq = n_chunks * CHUNK
    hpg = HEADS_PER_GROUP

    def conv_silu(src_ref, dst_ref, w_ref, bias_ref):
        cols = src_ref.shape[1]
        w0, w1, w2 = w_ref[0:1, :], w_ref[1:2, :], w_ref[2:3, :]
        bias = bias_ref[...]
        row = lax.broadcasted_iota(jnp.int32, (conv_tile, cols), 0) % conv_period
        first = row == 0
        last = row == conv_period - 1

        def body(k, carry):
            r = pl.ds(pl.multiple_of(k * conv_tile, conv_tile), conv_tile)
            v = src_ref[r, :].astype(F32)
            prev = jnp.where(first, 0.0, pltpu.roll(v, 1, 0))
            nxt = jnp.where(last, 0.0, pltpu.roll(v, conv_tile - 1, 0))
            y = bias + prev * w0 + v * w1 + nxt * w2
            dst_ref[r, :] = _silu(y).astype(BF16)
            return carry

        lax.fori_loop(0, seq // conv_tile, body, 0)

    conv_silu(xs_ref, xs_s, cwx_ref, cbx_ref)
    conv_silu(b_ref, b_s, cwb_ref, cbb_ref)
    conv_silu(c_ref, c_s, cwc_ref, cbc_ref)

    ii = lax.broadcasted_iota(jnp.int32, (CHUNK, CHUNK), 0)
    jj = lax.broadcasted_iota(jnp.int32, (CHUNK, CHUNK), 1)
    lower = ii >= jj
    upper = jj >= ii

    hf_s[...] = h0f_ref[...]
    hb_s[...] = h0b_ref[...]

    head_of_col = lax.broadcasted_iota(jnp.int32, (LANES, GROUP_WIDTH), 1) // SSD_HEAD_DIM
    sel_row = lax.broadcasted_iota(jnp.int32, (LANES, GROUP_WIDTH), 0)
    sel_valid = sel_row < 3 * GROUP_LANES
    sel_f = (sel_valid & (sel_row % GROUP_LANES == head_of_col)).astype(BF16)
    sel_b = (sel_valid & (sel_row % GROUP_LANES == head_of_col + hpg)).astype(BF16)
    low_half = jj < SSD_HEAD_DIM
    group_lane = jj < GROUP_LANES

    def direction_terms(r, sel):
        cum = cum_ref[r, :]
        dts = dts_ref[r, :]
        tot = jnp.where(jj[0:1, :] < hpg, cum[CHUNK - 1:CHUNK, :], cum[0:1, :])
        wst = dts * jnp.exp2(tot - cum)
        eoff = jnp.where(group_lane, jnp.exp2(cum), 0.0)
        edec = jnp.broadcast_to(jnp.where(group_lane[0:1, :], jnp.exp2(tot), 0.0), (8, LANES))
        ex = _dot(_stack_split3(jnp.concatenate([wst, eoff, edec], axis=0)), sel)
        return cum, ex[:CHUNK], ex[CHUNK:2 * CHUNK], ex[2 * CHUNK:2 * CHUNK + 1]

    def state_step(h_s, r, sel):
        cum, w_state, e_off, e_dec = direction_terms(r, sel)
        xc = xs_s[r, :]
        xw = (xc.astype(F32) * w_state).astype(BF16)
        s_new = lax.dot_general(b_s[r, :], xw, (((0,), (0,)), ((), ())), preferred_element_type=F32)
        h = h_s[...]
        y_off = _dot(c_s[r, :], h.astype(BF16)) * e_off
        h_s[...] = h * e_dec + s_new
        return cum, xc, y_off

    def fwd_body(k, carry):
        r = pl.ds(pl.multiple_of(k * CHUNK, CHUNK), CHUNK)
        cum, xc, y_off = state_step(hf_s, r, sel_f)
        if with_output:
            g = _dot_nt(c_s[r, :], b_s[r, :])
            rt = pl.ds(pl.multiple_of(k * GROUP_LANES, GROUP_LANES), GROUP_LANES)
            src_t = src_t_ref[rt, :]
            pieces = []
            for pair in range(hpg // 2):
                ms = []
                for h in (2 * pair, 2 * pair + 1):
                    hb = h + hpg
                    mf = jnp.exp2(jnp.where(lower, cum[:, h:h + 1] - src_t[h:h + 1, :], NEG_BIG))
                    mb = jnp.exp2(jnp.where(upper, cum[:, hb:hb + 1] - src_t[hb:hb + 1, :], NEG_BIG))
                    ms.append((g * (mf + mb)).astype(BF16))
                lhs = jnp.concatenate(ms, axis=1)
                xp = xc[:, pair * LANES:(pair + 1) * LANES]
                zero = jnp.zeros_like(xp)
                rhs = jnp.concatenate([jnp.where(low_half, xp, zero), jnp.where(low_half, zero, xp)], axis=0)
                pieces.append(_dot(lhs, rhs))
            y_diag = jnp.concatenate(pieces, axis=1)
            y_s[r, :] = y_diag + y_off + xc.astype(F32) * dskip_ref[...]
        return carry

    lax.fori_loop(0, n_chunks, fwd_body, 0)

    def bwd_body(k, carry):
        r = pl.ds(pl.multiple_of((n_chunks - 1 - k) * CHUNK, CHUNK), CHUNK)
        _, _, y_off = state_step(hb_s, r, sel_b)
        if with_output:
            y_ref[r, :] = (y_s[r, :] + y_off).astype(y_ref.dtype)
        return carry

    lax.fori_loop(0, n_chunks, bwd_body, 0)

    if not with_output:
        hf_out[...] = hf_s[...]
        hb_out[...] = hb_s[...]


def _ssd(pm, dt_terms, conv_w, conv_b, dskip_g, h0f, h0b, *, batch, seq, conv_tile, conv_period, with_output):
    n_chunks = seq // CHUNK
    dts_g, cum_g, src_t_g = dt_terms
    dt_spec = pl.BlockSpec((None, seq, LANES), lambda b, g: (g, b, 0))
    dt_t_spec = pl.BlockSpec((None, n_chunks * GROUP_LANES, LANES), lambda b, g: (g, b, 0))
    xs_blk0 = SSD_WIDTH // GROUP_WIDTH
    b_blk0 = (2 * SSD_WIDTH + 2 * CM_WIDTH) // SSD_STATE
    c_blk0 = b_blk0 + SSD_GROUPS
    cw_b0 = SSD_WIDTH // SSD_STATE
    cw_c0 = cw_b0 + SSD_GROUPS
    state_spec = pl.BlockSpec((None, None, SSD_STATE, GROUP_WIDTH), lambda b, g: (b, g, 0, 0))
    in_specs = [
        pl.BlockSpec((seq, GROUP_WIDTH), lambda b, g: (b, xs_blk0 + g)),
        pl.BlockSpec((seq, SSD_STATE), lambda b, g: (b, b_blk0 + g)),
        pl.BlockSpec((seq, SSD_STATE), lambda b, g: (b, c_blk0 + g)),
        dt_spec,
        dt_spec,
        dt_t_spec,
        pl.BlockSpec((3, GROUP_WIDTH), lambda b, g: (0, g)),
        pl.BlockSpec((1, GROUP_WIDTH), lambda b, g: (0, g)),
        pl.BlockSpec((3, SSD_STATE), lambda b, g: (0, cw_b0 + g)),
        pl.BlockSpec((1, SSD_STATE), lambda b, g: (0, cw_b0 + g)),
        pl.BlockSpec((3, SSD_STATE), lambda b, g: (0, cw_c0 + g)),
        pl.BlockSpec((1, SSD_STATE), lambda b, g: (0, cw_c0 + g)),
        pl.BlockSpec((None, 1, GROUP_WIDTH), lambda b, g: (g, 0, 0)),
        state_spec,
        state_spec,
    ]
    scratch = [
        pltpu.VMEM((seq, GROUP_WIDTH), BF16),
        pltpu.VMEM((seq, SSD_STATE), BF16),
        pltpu.VMEM((seq, SSD_STATE), BF16),
        pltpu.VMEM((SSD_STATE, GROUP_WIDTH), F32),
        pltpu.VMEM((SSD_STATE, GROUP_WIDTH), F32),
    ]
    if with_output:
        out_specs = pl.BlockSpec((seq, GROUP_WIDTH), lambda b, g: (b, g))
        out_shape = jax.ShapeDtypeStruct((batch * seq, SSD_WIDTH), BF16)
        scratch.append(pltpu.VMEM((seq, GROUP_WIDTH), F32))
    else:
        out_specs = [state_spec, state_spec]
        out_shape = [jax.ShapeDtypeStruct((batch, SSD_GROUPS, SSD_STATE, GROUP_WIDTH), F32)] * 2
    return pl.pallas_call(
        functools.partial(_ssd_kernel, n_chunks=n_chunks, conv_tile=conv_tile, conv_period=conv_period,
                          with_output=with_output),
        grid=(batch, SSD_GROUPS),
        in_specs=in_specs,
        out_specs=out_specs,
        out_shape=out_shape,
        scratch_shapes=scratch,
        compiler_params=_params("parallel", "parallel"),
        name="ssd_out" if with_output else "ssd_ctx",
    )(pm, pm, pm, dts_g, cum_g, src_t_g, conv_w, conv_b, conv_w, conv_b, conv_w, conv_b, dskip_g, h0f, h0b)


def _out_proj_kernel(yd_ref, z_ref, u_ref, v_ref, x_ref, g1_ref, ng_ref, lng_ref, lnb_ref, ws_ref, bst_ref, w_ref,
                     o_ref, mix_cur, mix_nxt):
    tm = yd_ref.shape[0]
    s = pl.program_id(0)
    n = pl.num_programs(0) - 1

    def build_mix():
        for k in range(tm // CHUNK):
            r = pl.ds(k * CHUNK, CHUNK)
            a = yd_ref[r, :].astype(F32) * _silu(z_ref[r, :].astype(F32))
            ms = jnp.mean(a * a, axis=-1, keepdims=True)
            mix_nxt[r, 0:SSD_WIDTH] = (a * lax.rsqrt(ms + NORM_EPS) * ng_ref[...]).astype(BF16)
            gv = jax.nn.gelu(v_ref[r, :].astype(F32))
            mu = jnp.mean(gv, axis=-1, keepdims=True)
            xc = gv - mu
            var = jnp.mean(xc * xc, axis=-1, keepdims=True)
            ln = ((xc * lax.rsqrt(var + NORM_EPS)) * lng_ref[...] + lnb_ref[...]).astype(BF16)
            gu = jax.nn.gelu(u_ref[r, :].astype(F32))
            for h in range(CM_HEADS):
                c0, c1 = h * CM_HEAD_DIM, (h + 1) * CM_HEAD_DIM
                sp = _dot(ws_ref[h], ln[:, c0:c1]) + bst_ref[:, h:h + 1]
                mix_nxt[r, SSD_WIDTH + c0:SSD_WIDTH + c1] = (gu[:, c0:c1] * sp).astype(BF16)

    def project():
        o_ref[...] = x_ref[...] + g1_ref[...] * _dot(mix_cur[...], w_ref[...])

    @pl.when(s == 0)
    def _():
        build_mix()

    @pl.when((s > 0) & (s < n))
    def _():
        project()
        build_mix()

    @pl.when(s == n)
    def _():
        project()

    @pl.when(s < n)
    def _():
        mix_cur[...] = mix_nxt[...]


def _out_proj(yd, pm, x2, mod3, ssd_norm_g, ln_g, ln_b, ws, bst, w_out, *, seq, tm):
    m = x2.shape[0]
    n = m // tm
    tiles_per_batch = seq // tm
    z_blk, u_blk, v_blk = 0, 2 * SSD_WIDTH // CM_WIDTH, 2 * SSD_WIDTH // CM_WIDTH + 1
    g1_blk = 2
    row = lambda shape: pl.BlockSpec(shape, lambda s: (0, 0))
    build_tile = lambda s: jnp.minimum(s, n - 1)
    proj_tile = lambda s: jnp.maximum(s - 1, 0)
    return pl.pallas_call(
        _out_proj_kernel,
        grid=(n + 1,),
        in_specs=[
            pl.BlockSpec((tm, SSD_WIDTH), lambda s: (build_tile(s), 0)),
            pl.BlockSpec((tm, SSD_WIDTH), lambda s: (build_tile(s), z_blk)),
            pl.BlockSpec((tm, CM_WIDTH), lambda s: (build_tile(s), u_blk)),
            pl.BlockSpec((tm, CM_WIDTH), lambda s: (build_tile(s), v_blk)),
            pl.BlockSpec((tm, D_MODEL), lambda s: (proj_tile(s), 0)),
            pl.BlockSpec((None, 1, D_MODEL), lambda s: (proj_tile(s) // tiles_per_batch, 0, g1_blk)),
            row((1, SSD_WIDTH)),
            row((1, CM_WIDTH)),
            row((1, CM_WIDTH)),
            pl.BlockSpec((CM_HEADS, CHUNK, CHUNK), lambda s: (0, 0, 0)),
            row((CHUNK, CM_HEADS)),
            pl.BlockSpec((SSD_WIDTH + CM_WIDTH, D_MODEL), lambda s: (0, 0), pipeline_mode=pl.Buffered(1)),
        ],
        out_specs=pl.BlockSpec((tm, D_MODEL), lambda s: (proj_tile(s), 0)),
        out_shape=jax.ShapeDtypeStruct((m, D_MODEL), F32),
        scratch_shapes=[pltpu.VMEM((tm, SSD_WIDTH + CM_WIDTH), BF16)] * 2,
        compiler_params=_params("arbitrary"),
        name="out_proj",
    )(yd, pm, pm, pm, x2, mod3, ssd_norm_g, ln_g, ln_b, ws, bst, w_out)


def _route(logits):
    lane = lax.broadcasted_iota(jnp.int32, logits.shape, 1)
    lane_f = lane.astype(F32)
    is_group = (lane >= N_EXPERTS) & (lane < N_EXPERTS + N_GROUPS)
    lg = jnp.where(is_group, logits, NEG_BIG)
    mg = jnp.max(lg, axis=1, keepdims=True)
    top_pg = 1.0 / jnp.sum(jnp.exp(lg - mg), axis=1, keepdims=True)
    gi = jnp.min(jnp.where(lg == mg, lane_f, 1e9), axis=1, keepdims=True) - N_EXPERTS
    in_group = (lane < N_EXPERTS) & ((lane // EXPERTS_PER_GROUP).astype(F32) == gi)
    le = jnp.where(in_group, logits, NEG_BIG)
    m1 = jnp.max(le, axis=1, keepdims=True)
    i1 = jnp.min(jnp.where(le == m1, lane_f, 1e9), axis=1, keepdims=True)
    le2 = jnp.where(lane_f == i1, NEG_BIG, le)
    m2 = jnp.max(le2, axis=1, keepdims=True)
    i2 = jnp.min(jnp.where(le2 == m2, lane_f, 1e9), axis=1, keepdims=True)
    e2 = jnp.exp(m2 - m1)
    p1 = 1.0 / (1.0 + e2)
    p2 = e2 * p1
    return i1, i2, p1 * top_pg, p2 * top_pg


RINFO_E1, RINFO_E2, RINFO_W1, RINFO_W2, RINFO_R1, RINFO_R2 = range(6)


def _route_kernel(x1_ref, n2g_ref, sh2_ref, sc2_ref, wr_ref, br_ref, h2_ref, rinfo_ref, counts_ref, cnt_scr):
    tm = x1_ref.shape[0]

    @pl.when(pl.program_id(0) == 0)
    def _():
        cnt_scr[...] = jnp.zeros_like(cnt_scr)

    x1 = x1_ref[...]
    ms = jnp.mean(x1 * x1, axis=-1, keepdims=True)
    h2 = (x1 * lax.rsqrt(ms + NORM_EPS) * n2g_ref[...]) * (1.0 + sc2_ref[...]) + sh2_ref[...]
    h2_ref[...] = h2
    logits = jnp.dot(h2, wr_ref[...], precision=HIGHEST, preferred_element_type=F32) + br_ref[...]
    i1, i2, w1, w2 = _route(logits)

    lane = lax.broadcasted_iota(jnp.int32, (tm, LANES), 1)
    lane_f = lane.astype(F32)
    oh1 = jnp.where(lane_f == i1, 1.0, 0.0)
    oh2 = jnp.where(lane_f == i2, 1.0, 0.0)
    before = (lax.broadcasted_iota(jnp.int32, (tm, tm), 0) > lax.broadcasted_iota(jnp.int32, (tm, tm), 1)).astype(BF16)
    carried = cnt_scr[...]
    tot1 = jnp.sum(oh1, axis=0, keepdims=True)
    r1 = jnp.sum(oh1 * (_dot(before, oh1.astype(BF16)) + carried), axis=1, keepdims=True)
    r2 = jnp.sum(oh2 * (_dot(before, oh2.astype(BF16)) + (carried + tot1)), axis=1, keepdims=True)
    counts = carried + tot1 + jnp.sum(oh2, axis=0, keepdims=True)
    cnt_scr[...] = counts
    counts_ref[...] = jnp.broadcast_to(counts, counts_ref.shape)

    info = jnp.zeros((tm, LANES), F32)
    for k, v in ((RINFO_E1, i1), (RINFO_E2, i2), (RINFO_W1, w1), (RINFO_W2, w2), (RINFO_R1, r1), (RINFO_R2, r2)):
        info = jnp.where(lane == k, v, info)
    rinfo_ref[...] = info


def _route_call(x1, mod3, norm2_g, w_router, b_router, *, seq, tm):
    m = x1.shape[0]
    tiles_per_batch = seq // tm
    row = lambda shape: pl.BlockSpec(shape, lambda i: (0, 0))
    modrow = lambda k: pl.BlockSpec((None, 1, D_MODEL), lambda i: (i // tiles_per_batch, 0, k))
    return pl.pallas_call(
        _route_kernel,
        grid=(m // tm,),
        in_specs=[
            pl.BlockSpec((tm, D_MODEL), lambda i: (i, 0)),
            row((1, D_MODEL)),
            modrow(3),
            modrow(4),
            row((D_MODEL, LANES)),
            row((1, LANES)),
        ],
        out_specs=[
            pl.BlockSpec((tm, D_MODEL), lambda i: (i, 0)),
            pl.BlockSpec((tm, LANES), lambda i: (i, 0)),
            row((8, LANES)),
        ],
        out_shape=[
            jax.ShapeDtypeStruct((m, D_MODEL), F32),
            jax.ShapeDtypeStruct((m, LANES), F32),
            jax.ShapeDtypeStruct((8, LANES), F32),
        ],
        scratch_shapes=[pltpu.VMEM((1, LANES), F32)],
        compiler_params=_params("arbitrary"),
        name="route",
    )(x1, norm2_g, mod3, mod3, w_router, b_router)


def _dispatch_kernel(lt_ref, nu_ref, pos_ref, h2_ref, x_hbm, zbuf, zsem, sem, *, tile_rows, n_tiles):
    tm = h2_ref.shape[0]
    n_used = nu_ref[0]

    def zero_copy(t):
        return pltpu.make_async_copy(zbuf, x_hbm.at[pl.ds(pl.multiple_of(t * tile_rows, tile_rows), tile_rows), :], zsem)

    @pl.when(pl.program_id(0) == 0)
    def _():
        zbuf[...] = jnp.zeros_like(zbuf)
        for phase in ("start", "wait"):
            for e in range(N_EXPERTS):
                @pl.when(lt_ref[e] >= 0)
                def _():
                    getattr(zero_copy(lt_ref[e]), phase)()

            def unused(t, carry):
                getattr(zero_copy(t), phase)()
                return carry

            lax.fori_loop(n_used, n_tiles, unused, 0)

    def body(r, carry):
        for k in range(2):
            pltpu.make_async_copy(h2_ref.at[pl.ds(r, 1), :], x_hbm.at[pl.ds(pos_ref[0, 2 * r + k], 1), :], sem).start()
        return carry

    lax.fori_loop(0, tm, body, 0, unroll=8)
    for k in range(2):
        pltpu.make_async_copy(h2_ref, x_hbm.at[pl.ds(0, tm), :], sem).wait()


def _dispatch_call(last_tile, n_used, pos, h2, *, tm, tile_rows, n_tiles):
    m = h2.shape[0]
    grid_spec = pltpu.PrefetchScalarGridSpec(
        num_scalar_prefetch=2,
        grid=(m // tm,),
        in_specs=[
            pl.BlockSpec((None, 1, 2 * tm), lambda i, lt, nu: (i, 0, 0), memory_space=pltpu.SMEM),
            pl.BlockSpec((tm, D_MODEL), lambda i, lt, nu: (i, 0)),
        ],
        out_specs=pl.BlockSpec(memory_space=pl.ANY),
        scratch_shapes=[
            pltpu.VMEM((tile_rows, D_MODEL), F32),
            pltpu.SemaphoreType.DMA(()),
            pltpu.SemaphoreType.DMA(()),
        ],
    )
    return pl.pallas_call(
        functools.partial(_dispatch_kernel, tile_rows=tile_rows, n_tiles=n_tiles),
        grid_spec=grid_spec,
        out_shape=jax.ShapeDtypeStruct((n_tiles * tile_rows, D_MODEL), F32),
        compiler_params=_params("arbitrary"),
        name="dispatch",
    )(last_tile, n_used, pos, h2)


def _experts_kernel(te_ref, nu_ref, ord_ref, nxt_ref, x_ref, wg_hbm, wu_hbm, wd_hbm, y_ref,
                    wg_f, wu_f, wd_f, wg_b, wu_b, wd_b, sem):
    j = pl.program_id(0)
    n_used = nu_ref[0]

    def weight_copies(e, slot):
        return [pltpu.make_async_copy(src.at[e], dst.at[slot], sem.at[slot])
                for src, dst in ((wg_hbm, wg_f), (wu_hbm, wu_f), (wd_hbm, wd_f))]

    @pl.when(j == 0)
    def _():
        for cp in weight_copies(te_ref[0], 0):
            cp.start()

    @pl.when((j < n_used) & ((j == 0) | (te_ref[j] != te_ref[jnp.maximum(j - 1, 0)])))
    def _():
        slot = ord_ref[j] % 2
        for cp in weight_copies(te_ref[j], slot):
            cp.wait()

        @pl.when(nxt_ref[j] >= 0)
        def _():
            for cp in weight_copies(nxt_ref[j], 1 - slot):
                cp.start()

        wg_b[...] = wg_f[slot].astype(BF16)
        wu_b[...] = wu_f[slot].astype(BF16)
        wd_b[...] = wd_f[slot].astype(BF16)

    @pl.when(j < n_used)
    def _():
        xt = x_ref[...].astype(BF16)
        hid = (_silu(_dot(xt, wg_b[...])) * _dot(xt, wu_b[...])).astype(BF16)
        y_ref[...] = _dot(hid, wd_b[...])

    @pl.when(j >= n_used)
    def _():
        y_ref[...] = jnp.zeros_like(y_ref)


def _experts_call(tile_expert, n_used, tile_ord, next_expert, x_sorted, wg, wu, wd, *, tm):
    n_tiles = tile_expert.shape[0]
    hbm = pl.BlockSpec(memory_space=pl.ANY)
    grid_spec = pltpu.PrefetchScalarGridSpec(
        num_scalar_prefetch=4,
        grid=(n_tiles,),
        in_specs=[
            pl.BlockSpec((tm, D_MODEL), lambda j, te, nu, od, nx: (jnp.minimum(j, nu[0] - 1), 0)),
            hbm,
            hbm,
            hbm,
        ],
        out_specs=pl.BlockSpec((tm, D_MODEL), lambda j, te, nu, od, nx: (j, 0)),
        scratch_shapes=[
            pltpu.VMEM((2, D_MODEL, EXPERT_FF), F32),
            pltpu.VMEM((2, D_MODEL, EXPERT_FF), F32),
            pltpu.VMEM((2, EXPERT_FF, D_MODEL), F32),
            pltpu.VMEM((D_MODEL, EXPERT_FF), BF16),
            pltpu.VMEM((D_MODEL, EXPERT_FF), BF16),
            pltpu.VMEM((EXPERT_FF, D_MODEL), BF16),
            pltpu.SemaphoreType.DMA((2,)),
        ],
    )
    return pl.pallas_call(
        _experts_kernel,
        grid_spec=grid_spec,
        out_shape=jax.ShapeDtypeStruct((n_tiles * tm, D_MODEL), F32),
        compiler_params=_params("arbitrary"),
        name="experts",
    )(tile_expert, n_used, tile_ord, next_expert, x_sorted, wg, wu, wd)


def _combine_kernel(pos_cur, pos_nxt, x1_ref, g2_ref, nfg_ref, rinfo_ref, y_hbm, o_ref, ybuf, sem):
    i = pl.program_id(0)
    tm = x1_ref.shape[0]
    slot = i % 2

    def row_copy(pos_ref, r, k, s):
        return pltpu.make_async_copy(y_hbm.at[pl.ds(pos_ref[0, 2 * r + k], 1), :], ybuf.at[s, k, pl.ds(r, 1), :],
                                     sem.at[s])

    def wait_tile(s):
        for k in range(2):
            pltpu.make_async_copy(y_hbm.at[pl.ds(0, tm), :], ybuf.at[s, k], sem.at[s]).wait()

    @pl.when(i == 0)
    def _():
        def body(r, carry):
            for k in range(2):
                row_copy(pos_cur, r, k, 0).start()
            return carry

        lax.fori_loop(0, tm, body, 0)

    wait_tile(slot)
    for r in range(tm):
        for k in range(2):
            row_copy(pos_nxt, r, k, 1 - slot).start(priority=k)
    info = rinfo_ref[...]
    w1 = info[:, RINFO_W1:RINFO_W1 + 1]
    w2 = info[:, RINFO_W2:RINFO_W2 + 1]
    moe = w1 * ybuf[slot, 0] + w2 * ybuf[slot, 1]
    y = x1_ref[...] + g2_ref[...] * moe
    ms = jnp.mean(y * y, axis=-1, keepdims=True)
    o_ref[...] = y * lax.rsqrt(ms + NORM_EPS) * nfg_ref[...]

    @pl.when(i == pl.num_programs(0) - 1)
    def _():
        wait_tile(1 - slot)


def _combine_call(pos, x1, mod3, normf_g, rinfo, y_sorted, *, seq, tm):
    m = x1.shape[0]
    n_tiles = m // tm
    tiles_per_batch = seq // tm
    return pl.pallas_call(
        _combine_kernel,
        grid=(n_tiles,),
        in_specs=[
            pl.BlockSpec((None, 1, 2 * tm), lambda i: (i, 0, 0), memory_space=pltpu.SMEM),
            pl.BlockSpec((None, 1, 2 * tm), lambda i: (jnp.minimum(i + 1, n_tiles - 1), 0, 0),
                         memory_space=pltpu.SMEM),
            pl.BlockSpec((tm, D_MODEL), lambda i: (i, 0)),
            pl.BlockSpec((None, 1, D_MODEL), lambda i: (i // tiles_per_batch, 0, 5)),
            pl.BlockSpec((1, D_MODEL), lambda i: (0, 0)),
            pl.BlockSpec((tm, LANES), lambda i: (i, 0)),
            pl.BlockSpec(memory_space=pl.ANY),
        ],
        out_specs=pl.BlockSpec((tm, D_MODEL), lambda i: (i, 0)),
        out_shape=jax.ShapeDtypeStruct((m, D_MODEL), F32),
        scratch_shapes=[
            pltpu.VMEM((2, 2, tm, D_MODEL), F32),
            pltpu.SemaphoreType.DMA((2,)),
        ],
        compiler_params=_params("arbitrary"),
        name="combine",
    )(pos, pos, x1, mod3, normf_g, rinfo, y_sorted)


def _dispatch_plan(rinfo, counts, *, tm):
    n_tok = rinfo.shape[0]
    n_tiles = (2 * n_tok + N_EXPERTS * (tm - 1)) // tm + 1
    expert = rinfo[:, RINFO_E1:RINFO_E2 + 1].astype(jnp.int32)
    rank = rinfo[:, RINFO_R1:RINFO_R2 + 1].astype(jnp.int32)
    cnt = counts[0, :N_EXPERTS].astype(jnp.int32)
    tiles_e = (cnt + tm - 1) // tm
    end_tile = jnp.cumsum(tiles_e)
    start_row = (end_tile - tiles_e) * tm
    is_e = expert[:, :, None] == jnp.arange(N_EXPERTS, dtype=jnp.int32)
    pos = jnp.sum(jnp.where(is_e, start_row, 0), axis=-1) + rank
    n_used = end_tile[-1:]
    tile_ids = jnp.minimum(jnp.arange(n_tiles, dtype=jnp.int32), n_used - 1)
    tile_expert = jnp.sum(tile_ids[:, None] >= end_tile[None, :], axis=1).astype(jnp.int32)
    last_tile = jnp.where(tiles_e > 0, end_tile - 1, -1).astype(jnp.int32)
    used = (tiles_e > 0).astype(jnp.int32)
    ord_e = jnp.cumsum(used) - used
    ids = jnp.arange(N_EXPERTS, dtype=jnp.int32)
    later = jnp.where((ids[None, :] > ids[:, None]) & (used[None, :] > 0), ids[None, :], N_EXPERTS)
    nxt_e = jnp.min(later, axis=1)
    nxt_e = jnp.where(nxt_e < N_EXPERTS, nxt_e, -1).astype(jnp.int32)
    is_te = tile_expert[:, None] == ids[None, :]
    tile_ord = jnp.sum(jnp.where(is_te, ord_e, 0), axis=1).astype(jnp.int32)
    next_expert = jnp.sum(jnp.where(is_te, nxt_e, 0), axis=1).astype(jnp.int32)
    return pos, (tile_expert, n_used.astype(jnp.int32), tile_ord, next_expert), last_tile, n_tiles


def kernel(x, c, ctx, c_ctx, w_mod, b_mod, norm1_g, w_in, conv_w, conv_b, dt_bias_f, dt_bias_b, a_log_f, a_log_b,
           d_skip, ssd_norm_g, cm_ln_g, cm_ln_b, w_spatial, b_spatial, w_out, norm2_g, w_router_group,
           b_router_group, w_router_expert, b_router_expert, w_exp_gate, w_exp_up, w_exp_down, normf_g):
    bsz, seq, _ = x.shape
    ctx_len = ctx.shape[1]
    i = 0

    cc = jnp.concatenate([c, c_ctx[None, :], jnp.zeros((MOD_ROWS - bsz - 1, D_MODEL), F32)], axis=0)
    mod = _modulation(cc, w_mod[i], b_mod[i][None, :])
    mod3 = mod.reshape(MOD_ROWS, 1, N_MOD * D_MODEL)

    w_main, w_dt = _w_in_prep(jnp.swapaxes(w_in[i], 0, 1))
    g1row = norm1_g[i][None, :]

    x2 = x.reshape(bsz * seq, D_MODEL)
    ctx2 = ctx.reshape(bsz * ctx_len, D_MODEL)
    tm_in = 1024
    pm_x, dt_x = _in_proj(x2, mod3, lambda t: t // (seq // tm_in), g1row, w_main, w_dt, tm_in, 1024)
    pm_c, dt_c = _in_proj(ctx2, mod3, lambda t: bsz, g1row, w_main, w_dt, bsz * ctx_len, 1024)

    lane_pad = (0, LANES - 2 * SSD_HEADS)
    dt_bias_row = jnp.pad(jnp.concatenate([dt_bias_f[i], dt_bias_b[i]]).astype(F32), lane_pad)[None, :]
    a_row = jnp.pad(-jnp.exp(jnp.concatenate([a_log_f[i], a_log_b[i]]).astype(F32)), lane_pad)[None, :]
    dskip_g = jnp.repeat(d_skip[i], SSD_HEAD_DIM).reshape(SSD_GROUPS, 1, GROUP_WIDTH)
    cw = conv_w[i]
    cb = conv_b[i][None, :]
    h_zero = jnp.zeros((bsz, SSD_GROUPS, SSD_STATE, GROUP_WIDTH), F32)

    dt_terms_c = _ssd_dt(dt_c, dt_bias_row, a_row, rows_per_step=bsz * ctx_len)
    dt_terms_x = _ssd_dt(dt_x, dt_bias_row, a_row, rows_per_step=1024)
    hc_f, hc_b = _ssd(pm_c, dt_terms_c, cw, cb, dskip_g, h_zero, h_zero,
                      batch=bsz, seq=ctx_len, conv_tile=ctx_len, conv_period=ctx_len, with_output=False)
    yd = _ssd(pm_x, dt_terms_x, cw, cb, dskip_g, hc_f, hc_b,
              batch=bsz, seq=seq, conv_tile=CHUNK, conv_period=GRID_W, with_output=True)

    x1 = _out_proj(yd, pm_x, x2, mod3, ssd_norm_g[i][None, :], cm_ln_g[i][None, :], cm_ln_b[i][None, :],
                   w_spatial[i].astype(BF16), b_spatial[i].T, w_out[i].astype(BF16), seq=seq, tm=256)

    w_re = jnp.transpose(w_router_expert[i], (1, 0, 2)).reshape(D_MODEL, N_EXPERTS)
    pad = LANES - N_EXPERTS - N_GROUPS
    w_router = jnp.pad(jnp.concatenate([w_re, w_router_group[i]], axis=1), ((0, 0), (0, pad)))
    b_router = jnp.pad(jnp.concatenate([b_router_expert[i].reshape(-1), b_router_group[i]]), (0, pad))[None, :]
    h2, rinfo, counts = _route_call(x1, mod3, norm2_g[i][None, :], w_router, b_router, seq=seq, tm=512)

    tm_e = 256
    pos, tile_tables, last_tile, n_tiles = _dispatch_plan(rinfo, counts, tm=tm_e)
    tm_d = 512
    x_sorted = _dispatch_call(last_tile, tile_tables[1], pos.reshape(bsz * seq // tm_d, 1, 2 * tm_d), h2,
                              tm=tm_d, tile_rows=tm_e, n_tiles=n_tiles)
    wg = w_exp_gate[i].reshape(N_EXPERTS, D_MODEL, EXPERT_FF)
    wu = w_exp_up[i].reshape(N_EXPERTS, D_MODEL, EXPERT_FF)
    wd = w_exp_down[i].reshape(N_EXPERTS, EXPERT_FF, D_MODEL)
    y_sorted = _experts_call(*tile_tables, x_sorted, wg, wu, wd, tm=tm_e)
    tm_c = 256
    out = _combine_call(pos.reshape(bsz * seq // tm_c, 1, 2 * tm_c), x1, mod3, normf_g[None, :], rinfo, y_sorted,
                        seq=seq, tm=tm_c)
    return out.reshape(bsz, seq, D_MODEL)
```

```python
import functools

import jax
import jax.numpy as jnp
from jax import lax
from jax.experimental import pallas as pl
from jax.experimental.pallas import tpu as pltpu

F32 = jnp.float32
BF16 = jnp.bfloat16
HIGHEST = lax.Precision.HIGHEST

D_MODEL = 2048
GRID_W = 64
SSD_WIDTH = 2048
CM_WIDTH = 2048
SSD_HEADS = 32
SSD_HEAD_DIM = 64
SSD_GROUPS = 4
HEADS_PER_GROUP = SSD_HEADS // SSD_GROUPS
GROUP_WIDTH = HEADS_PER_GROUP * SSD_HEAD_DIM
SSD_STATE = 128
CHUNK = 128
BC_WIDTH = SSD_GROUPS * SSD_STATE
CM_HEADS = 8
CM_HEAD_DIM = CM_WIDTH // CM_HEADS
N_GROUPS = 4
N_EXPERTS = 32
EXPERTS_PER_GROUP = 8
EXPERT_FF = 512
N_MOD = 6
NORM_EPS = 1e-6
LANES = 128
NEG_BIG = -1e30
MOD_ROWS = 8
VMEM_LIMIT = 56 * 1024 * 1024


def _params(*sem):
    return pltpu.CompilerParams(dimension_semantics=sem, vmem_limit_bytes=VMEM_LIMIT)


def _silu(v):
    return v * jax.nn.sigmoid(v)


def _dot(a, b):
    return jnp.dot(a, b, preferred_element_type=F32)


def _dot_nt(a, b):
    return lax.dot_general(a, b, (((1,), (1,)), ((), ())), preferred_element_type=F32)


def _mod_kernel(cc_ref, w_ref, b_ref, o_ref):
    a = _silu(cc_ref[...])
    hi = a.astype(BF16).astype(F32)
    lhs = jnp.concatenate([hi, a - hi], axis=0).astype(BF16)
    r = _dot(lhs, w_ref[...].astype(BF16))
    o_ref[...] = r[:MOD_ROWS] + r[MOD_ROWS:] + b_ref[...]


def _modulation(cc, w_mod, b_mod):
    n = w_mod.shape[1]
    tn = 1024
    return pl.pallas_call(
        _mod_kernel,
        grid=(n // tn,),
        in_specs=[
            pl.BlockSpec((MOD_ROWS, D_MODEL), lambda j: (0, 0)),
            pl.BlockSpec((D_MODEL, tn), lambda j: (0, j)),
            pl.BlockSpec((1, tn), lambda j: (0, j)),
        ],
        out_specs=pl.BlockSpec((MOD_ROWS, tn), lambda j: (0, j)),
        out_shape=jax.ShapeDtypeStruct((MOD_ROWS, n), F32),
        compiler_params=_params("arbitrary"),
        name="modulation",
    )(cc, w_mod, b_mod)


PREP_ROWS = 512
PREP_UV_BLK0 = 2 * SSD_WIDTH // PREP_ROWS
PREP_BC_BLK0 = (2 * SSD_WIDTH + 2 * CM_WIDTH) // PREP_ROWS
PREP_N_BLKS = PREP_BC_BLK0 + 2 * BC_WIDTH // PREP_ROWS
DT_COL0 = 2 * SSD_WIDTH + 2 * BC_WIDTH
UV_COL0 = DT_COL0 + 2 * SSD_HEADS


def _w_in_prep_kernel(a_ref, dt_ref, o_ref, wdt_ref):
    @pl.when(pl.program_id(0) == 0)
    def _():
        wdt_ref[...] = dt_ref[...].astype(BF16)

    o_ref[...] = a_ref[...].astype(BF16)


def _w_in_prep(w_in_t):
    def src_row(j):
        uv = UV_COL0 + (j - PREP_UV_BLK0) * PREP_ROWS
        bc = 2 * SSD_WIDTH + (j - PREP_BC_BLK0) * PREP_ROWS
        return pl.multiple_of(jnp.where(j < PREP_UV_BLK0, j * PREP_ROWS, jnp.where(j < PREP_BC_BLK0, uv, bc)), 8)

    return pl.pallas_call(
        _w_in_prep_kernel,
        grid=(PREP_N_BLKS,),
        in_specs=[
            pl.BlockSpec((pl.Element(PREP_ROWS), pl.Element(D_MODEL)), lambda j: (src_row(j), 0)),
            pl.BlockSpec((LANES, D_MODEL), lambda j: (DT_COL0 // LANES, 0)),
        ],
        out_specs=[
            pl.BlockSpec((PREP_ROWS, D_MODEL), lambda j: (j, 0)),
            pl.BlockSpec((LANES, D_MODEL), lambda j: (0, 0)),
        ],
        out_shape=[
            jax.ShapeDtypeStruct((PREP_N_BLKS * PREP_ROWS, D_MODEL), BF16),
            jax.ShapeDtypeStruct((LANES, D_MODEL), BF16),
        ],
        compiler_params=_params("arbitrary"),
        name="w_in_prep",
    )(w_in_t, w_in_t)


def _in_proj_kernel(x_ref, g_ref, sh_ref, sc_ref, w_ref, wdt_ref, o_ref, dt_ref, h_scr):
    tm = x_ref.shape[0]
    rc = 256

    @pl.when(pl.program_id(1) == 0)
    def _():
        gain = g_ref[...]
        scale = 1.0 + sc_ref[...]
        shift = sh_ref[...]

        def body(k, carry):
            r = pl.ds(pl.multiple_of(k * rc, rc), rc)
            xv = x_ref[r, :]
            ms = jnp.mean(xv * xv, axis=-1, keepdims=True)
            h = (xv * lax.rsqrt(ms + NORM_EPS) * gain) * scale + shift
            h_scr[r, :] = h.astype(BF16)
            return carry

        lax.fori_loop(0, tm // rc, body, 0)
        dt_ref[...] = _dot_nt(h_scr[...], wdt_ref[...])

    o_ref[...] = _dot_nt(h_scr[...], w_ref[...]).astype(o_ref.dtype)


def _in_proj(x2, mod3, mod_row_of_tile, norm_g, w_main, w_dt, tm, tn):
    m = x2.shape[0]
    n = w_main.shape[0]
    return pl.pallas_call(
        _in_proj_kernel,
        grid=(m // tm, n // tn),
        in_specs=[
            pl.BlockSpec((tm, D_MODEL), lambda i, j: (i, 0)),
            pl.BlockSpec((1, D_MODEL), lambda i, j: (0, 0)),
            pl.BlockSpec((None, 1, D_MODEL), lambda i, j: (mod_row_of_tile(i), 0, 0)),
            pl.BlockSpec((None, 1, D_MODEL), lambda i, j: (mod_row_of_tile(i), 0, 1)),
            pl.BlockSpec((tn, D_MODEL), lambda i, j: (j, 0)),
            pl.BlockSpec((LANES, D_MODEL), lambda i, j: (0, 0)),
        ],
        out_specs=[
            pl.BlockSpec((tm, tn), lambda i, j: (i, j)),
            pl.BlockSpec((tm, LANES), lambda i, j: (i, 0)),
        ],
        out_shape=[
            jax.ShapeDtypeStruct((m, n), BF16),
            jax.ShapeDtypeStruct((m, LANES), F32),
        ],
        scratch_shapes=[pltpu.VMEM((tm, D_MODEL), BF16)],
        compiler_params=_params("parallel", "arbitrary"),
        name="in_proj",
    )(x2, norm_g, mod3, mod3, w_main, w_dt)


def _split3(v):
    hi = v.astype(BF16)
    rem = v - hi.astype(F32)
    mid = rem.astype(BF16)
    lo = (rem - mid.astype(F32)).astype(BF16)
    return hi, mid, lo


GROUP_LANES = 2 * HEADS_PER_GROUP


def _stack_split3(v):
    hi, mid, lo = _split3(v)
    stacked = (hi.astype(F32) + pltpu.roll(mid.astype(F32), GROUP_LANES, 1)
               + pltpu.roll(lo.astype(F32), 2 * GROUP_LANES, 1))
    return stacked.astype(BF16)


LOG2_E = 1.4426950408889634


def _ssd_dt_kernel(dt_ref, bias_ref, a_ref, dts_ref, cum_ref, src_t_ref):
    hpg = HEADS_PER_GROUP
    ii = lax.broadcasted_iota(jnp.int32, (CHUNK, CHUNK), 0)
    jj = lax.broadcasted_iota(jnp.int32, (CHUNK, CHUNK), 1)
    tri = (ii >= jj).astype(BF16)
    tri3 = jnp.concatenate([tri, tri, tri], axis=1)
    bias = bias_ref[...]
    a_row = a_ref[...]

    def body(k, carry):
        r = pl.ds(pl.multiple_of(k * CHUNK, CHUNK), CHUNK)
        raw = dt_ref[r, :] + bias
        dts = jnp.where(jj < 2 * SSD_HEADS, jnp.maximum(raw, 0.0) + jnp.log1p(jnp.exp(-jnp.abs(raw))), 0.0)
        adt = dts * a_row
        hi, mid, lo = _split3(adt)
        cf = _dot(tri3, jnp.concatenate([hi, mid, lo], axis=0))
        cr = cf[CHUNK - 1:CHUNK, :] - cf + adt
        valid = jj < 2 * SSD_HEADS
        cum = jnp.where(jj < SSD_HEADS, cf, cr)
        src_t = ((cum - jnp.log(jnp.where(valid, dts, 1.0))) * LOG2_E).T
        rt = pl.ds(pl.multiple_of(k * GROUP_LANES, GROUP_LANES), GROUP_LANES)
        for g in range(SSD_GROUPS):
            f0, b0 = g * hpg, SSD_HEADS + g * hpg
            for src, dst in ((cum * LOG2_E, cum_ref), (dts, dts_ref)):
                lanes = jnp.where(jj < hpg, pltpu.roll(src, (LANES - f0) % LANES, 1),
                                  pltpu.roll(src, (LANES - (b0 - hpg)) % LANES, 1))
                dst[g, r, :] = jnp.where(jj < GROUP_LANES, lanes, 0.0)
            src_t_ref[g, rt, :] = jnp.concatenate([src_t[f0:f0 + hpg, :], src_t[b0:b0 + hpg, :]], axis=0)
        return carry

    lax.fori_loop(0, dt_ref.shape[0] // CHUNK, body, 0)


def _ssd_dt(dt, bias_row, a_row, *, rows_per_step):
    m = dt.shape[0]
    row = pl.BlockSpec((1, LANES), lambda i: (0, 0))
    full = pl.BlockSpec((SSD_GROUPS, rows_per_step, LANES), lambda i: (0, i, 0))
    tr = pl.BlockSpec((SSD_GROUPS, rows_per_step // CHUNK * GROUP_LANES, LANES), lambda i: (0, i, 0))
    full_shape = jax.ShapeDtypeStruct((SSD_GROUPS, m, LANES), F32)
    tr_shape = jax.ShapeDtypeStruct((SSD_GROUPS, m // CHUNK * GROUP_LANES, LANES), F32)
    return pl.pallas_call(
        _ssd_dt_kernel,
        grid=(m // rows_per_step,),
        in_specs=[pl.BlockSpec((rows_per_step, LANES), lambda i: (i, 0)), row, row],
        out_specs=[full, full, tr],
        out_shape=[full_shape, full_shape, tr_shape],
        compiler_params=_params("parallel"),
        name="ssd_dt",
    )(dt, bias_row, a_row)


def _ssd_kernel(xs_ref, b_ref, c_ref, dts_ref, cum_ref, src_t_ref, cwx_ref, cbx_ref, cwb_ref, cbb_ref,
                cwc_ref, cbc_ref, dskip_ref, h0f_ref, h0b_ref, *rest,
                n_chunks, conv_tile, conv_period, with_output):
    if with_output:
        y_ref, xs_s, b_s, c_s, hf_s, hb_s, y_s, yb_s = rest
    else:
        hf_out, hb_out, xs_s, b_s, c_s, hf_s, hb_s = rest
    seq = n_chunks * CHUNK
    hpg = HEADS_PER_GROUP

    def conv_silu(src_ref, dst_ref, w_ref, bias_ref):
        cols = src_ref.shape[1]
        w0, w1, w2 = w_ref[0:1, :], w_ref[1:2, :], w_ref[2:3, :]
        bias = bias_ref[...]
        row = lax.broadcasted_iota(jnp.int32, (conv_tile, cols), 0) % conv_period
        first = row == 0
        last = row == conv_period - 1

        def body(k, carry):
            r = pl.ds(pl.multiple_of(k * conv_tile, conv_tile), conv_tile)
            v = src_ref[r, :].astype(F32)
            prev = jnp.where(first, 0.0, pltpu.roll(v, 1, 0))
            nxt = jnp.where(last, 0.0, pltpu.roll(v, conv_tile - 1, 0))
            y = bias + prev * w0 + v * w1 + nxt * w2
            dst_ref[r, :] = _silu(y).astype(BF16)
            return carry

        lax.fori_loop(0, seq // conv_tile, body, 0)

    conv_silu(xs_ref, xs_s, cwx_ref, cbx_ref)
    conv_silu(b_ref, b_s, cwb_ref, cbb_ref)
    conv_silu(c_ref, c_s, cwc_ref, cbc_ref)

    ii = lax.broadcasted_iota(jnp.int32, (CHUNK, CHUNK), 0)
    jj = lax.broadcasted_iota(jnp.int32, (CHUNK, CHUNK), 1)
    lower = ii >= jj
    upper = jj >= ii

    hf_s[...] = h0f_ref[...]
    hb_s[...] = h0b_ref[...]

    head_of_col = lax.broadcasted_iota(jnp.int32, (LANES, GROUP_WIDTH), 1) // SSD_HEAD_DIM
    sel_row = lax.broadcasted_iota(jnp.int32, (LANES, GROUP_WIDTH), 0)
    sel_valid = sel_row < 3 * GROUP_LANES
    sel_f = (sel_valid & (sel_row % GROUP_LANES == head_of_col)).astype(BF16)
    sel_b = (sel_valid & (sel_row % GROUP_LANES == head_of_col + hpg)).astype(BF16)
    low_half = jj < SSD_HEAD_DIM
    group_lane = jj < GROUP_LANES

    def direction_terms(r, sel):
        cum = cum_ref[r, :]
        dts = dts_ref[r, :]
        tot = jnp.where(jj[0:1, :] < hpg, cum[CHUNK - 1:CHUNK, :], cum[0:1, :])
        wst = dts * jnp.exp2(tot - cum)
        eoff = jnp.where(group_lane, jnp.exp2(cum), 0.0)
        edec = jnp.broadcast_to(jnp.where(group_lane[0:1, :], jnp.exp2(tot), 0.0), (8, LANES))
        ex = _dot(_stack_split3(jnp.concatenate([wst, eoff, edec], axis=0)), sel)
        return cum, ex[:CHUNK], ex[CHUNK:2 * CHUNK], ex[2 * CHUNK:2 * CHUNK + 1]

    def state_step(h_s, r, sel):
        cum, w_state, e_off, e_dec = direction_terms(r, sel)
        xc = xs_s[r, :]
        xw = (xc.astype(F32) * w_state).astype(BF16)
        s_new = lax.dot_general(b_s[r, :], xw, (((0,), (0,)), ((), ())), preferred_element_type=F32)
        h = h_s[...]
        y_off = _dot(c_s[r, :], h.astype(BF16)) * e_off
        h_s[...] = h * e_dec + s_new
        return cum, xc, y_off

    def fwd_body(k, carry):
        r = pl.ds(pl.multiple_of(k * CHUNK, CHUNK), CHUNK)
        cum, xc, y_off = state_step(hf_s, r, sel_f)
        if with_output:
            g = _dot_nt(c_s[r, :], b_s[r, :])
            rt = pl.ds(pl.multiple_of(k * GROUP_LANES, GROUP_LANES), GROUP_LANES)
            src_t = src_t_ref[rt, :]
            pieces = []
            for pair in range(hpg // 2):
                ms = []
                for h in (2 * pair, 2 * pair + 1):
                    hb = h + hpg
                    mf = jnp.exp2(jnp.where(lower, cum[:, h:h + 1] - src_t[h:h + 1, :], NEG_BIG))
                    mb = jnp.exp2(jnp.where(upper, cum[:, hb:hb + 1] - src_t[hb:hb + 1, :], NEG_BIG))
                    ms.append((g * (mf + mb)).astype(BF16))
                lhs = jnp.concatenate(ms, axis=1)
                xp = xc[:, pair * LANES:(pair + 1) * LANES]
                zero = jnp.zeros_like(xp)
                rhs = jnp.concatenate([jnp.where(low_half, xp, zero), jnp.where(low_half, zero, xp)], axis=0)
                pieces.append(_dot(lhs, rhs))
            y_diag = jnp.concatenate(pieces, axis=1)
            y_s[r, :] = y_diag + y_off + xc.astype(F32) * dskip_ref[...]
        rb = pl.ds(pl.multiple_of((n_chunks - 1 - k) * CHUNK, CHUNK), CHUNK)
        _, _, y_off_b = state_step(hb_s, rb, sel_b)
        if with_output:
            yb_s[rb, :] = y_off_b
        return carry

    lax.fori_loop(0, n_chunks, fwd_body, 0, unroll=4 if n_chunks % 4 == 0 else 2)

    if with_output:
        def sum_body(k, carry):
            r = pl.ds(pl.multiple_of(k * CHUNK, CHUNK), CHUNK)
            y_ref[r, :] = (y_s[r, :] + yb_s[r, :]).astype(y_ref.dtype)
            return carry

        lax.fori_loop(0, n_chunks, sum_body, 0)

    if not with_output:
        hf_out[...] = hf_s[...]
        hb_out[...] = hb_s[...]


def _ssd(pm, dt_terms, conv_w, conv_b, dskip_g, h0f, h0b, *, batch, seq, conv_tile, conv_period, with_output):
    n_chunks = seq // CHUNK
    dts_g, cum_g, src_t_g = dt_terms
    dt_spec = pl.BlockSpec((None, seq, LANES), lambda b, g: (g, b, 0))
    dt_t_spec = pl.BlockSpec((None, n_chunks * GROUP_LANES, LANES), lambda b, g: (g, b, 0))
    xs_blk0 = SSD_WIDTH // GROUP_WIDTH
    b_blk0 = (2 * SSD_WIDTH + 2 * CM_WIDTH) // SSD_STATE
    c_blk0 = b_blk0 + SSD_GROUPS
    cw_b0 = SSD_WIDTH // SSD_STATE
    cw_c0 = cw_b0 + SSD_GROUPS
    state_spec = pl.BlockSpec((None, None, SSD_STATE, GROUP_WIDTH), lambda b, g: (b, g, 0, 0))
    in_specs = [
        pl.BlockSpec((seq, GROUP_WIDTH), lambda b, g: (b, xs_blk0 + g)),
        pl.BlockSpec((seq, SSD_STATE), lambda b, g: (b, b_blk0 + g)),
        pl.BlockSpec((seq, SSD_STATE), lambda b, g: (b, c_blk0 + g)),
        dt_spec,
        dt_spec,
        dt_t_spec,
        pl.BlockSpec((3, GROUP_WIDTH), lambda b, g: (0, g)),
        pl.BlockSpec((1, GROUP_WIDTH), lambda b, g: (0, g)),
        pl.BlockSpec((3, SSD_STATE), lambda b, g: (0, cw_b0 + g)),
        pl.BlockSpec((1, SSD_STATE), lambda b, g: (0, cw_b0 + g)),
        pl.BlockSpec((3, SSD_STATE), lambda b, g: (0, cw_c0 + g)),
        pl.BlockSpec((1, SSD_STATE), lambda b, g: (0, cw_c0 + g)),
        pl.BlockSpec((None, 1, GROUP_WIDTH), lambda b, g: (g, 0, 0)),
        state_spec,
        state_spec,
    ]
    scratch = [
        pltpu.VMEM((seq, GROUP_WIDTH), BF16),
        pltpu.VMEM((seq, SSD_STATE), BF16),
        pltpu.VMEM((seq, SSD_STATE), BF16),
        pltpu.VMEM((SSD_STATE, GROUP_WIDTH), F32),
        pltpu.VMEM((SSD_STATE, GROUP_WIDTH), F32),
    ]
    if with_output:
        out_specs = pl.BlockSpec((seq, GROUP_WIDTH), lambda b, g: (b, g))
        out_shape = jax.ShapeDtypeStruct((batch * seq, SSD_WIDTH), BF16)
        scratch += [pltpu.VMEM((seq, GROUP_WIDTH), F32)] * 2
    else:
        out_specs = [state_spec, state_spec]
        out_shape = [jax.ShapeDtypeStruct((batch, SSD_GROUPS, SSD_STATE, GROUP_WIDTH), F32)] * 2
    return pl.pallas_call(
        functools.partial(_ssd_kernel, n_chunks=n_chunks, conv_tile=conv_tile, conv_period=conv_period,
                          with_output=with_output),
        grid=(batch, SSD_GROUPS),
        in_specs=in_specs,
        out_specs=out_specs,
        out_shape=out_shape,
        scratch_shapes=scratch,
        compiler_params=_params("parallel", "parallel"),
        name="ssd_out" if with_output else "ssd_ctx",
    )(pm, pm, pm, dts_g, cum_g, src_t_g, conv_w, conv_b, conv_w, conv_b, conv_w, conv_b, dskip_g, h0f, h0b)


def _out_proj_kernel(yd_ref, z_ref, u_ref, v_ref, x_ref, g1_ref, ng_ref, lng_ref, lnb_ref, ws_ref, bst_ref, w_ref,
                     o_ref, mix_cur, mix_nxt):
    tm = yd_ref.shape[0]
    s = pl.program_id(0)
    n = pl.num_programs(0) - 1

    def build_steps():
        steps = []
        for k in range(tm // CHUNK):
            r = pl.ds(k * CHUNK, CHUNK)
            shared = {}

            def ssd_part(r=r):
                a = yd_ref[r, :].astype(F32) * _silu(z_ref[r, :].astype(F32))
                ms = jnp.mean(a * a, axis=-1, keepdims=True)
                mix_nxt[r, 0:SSD_WIDTH] = (a * lax.rsqrt(ms + NORM_EPS) * ng_ref[...]).astype(BF16)

            def ln_part(r=r, shared=shared):
                gv = jax.nn.gelu(v_ref[r, :].astype(F32))
                mu = jnp.mean(gv, axis=-1, keepdims=True)
                xc = gv - mu
                var = jnp.mean(xc * xc, axis=-1, keepdims=True)
                shared["ln"] = ((xc * lax.rsqrt(var + NORM_EPS)) * lng_ref[...] + lnb_ref[...]).astype(BF16)

            def gate_part(h0, h1, r=r, shared=shared):
                for h in range(h0, h1):
                    c0, c1 = h * CM_HEAD_DIM, (h + 1) * CM_HEAD_DIM
                    sp = _dot(ws_ref[h], shared["ln"][:, c0:c1]) + bst_ref[:, h:h + 1]
                    gu = jax.nn.gelu(u_ref[r, c0:c1].astype(F32))
                    mix_nxt[r, SSD_WIDTH + c0:SSD_WIDTH + c1] = (gu * sp).astype(BF16)

            steps += [ssd_part, ln_part, functools.partial(gate_part, 0, CM_HEADS // 2),
                      functools.partial(gate_part, CM_HEADS // 2, CM_HEADS)]
        return steps

    n_proj = 8
    pw = D_MODEL // n_proj

    def project_steps():
        def piece(q):
            c = slice(q * pw, (q + 1) * pw)
            o_ref[:, c] = x_ref[:, c] + g1_ref[:, c] * _dot(mix_cur[...], w_ref[:, c])

        return [functools.partial(piece, q) for q in range(n_proj)]

    @pl.when(s == 0)
    def _():
        for step in build_steps():
            step()

    @pl.when((s > 0) & (s < n))
    def _():
        for proj, build in zip(project_steps(), build_steps()):
            proj()
            build()

    @pl.when(s == n)
    def _():
        for step in project_steps():
            step()

    @pl.when(s < n)
    def _():
        mix_cur[...] = mix_nxt[...]


def _out_proj(yd, pm, x2, mod3, ssd_norm_g, ln_g, ln_b, ws, bst, w_out, *, seq, tm):
    m = x2.shape[0]
    n = m // tm
    tiles_per_batch = seq // tm
    z_blk, u_blk, v_blk = 0, 2 * SSD_WIDTH // CM_WIDTH, 2 * SSD_WIDTH // CM_WIDTH + 1
    g1_blk = 2
    row = lambda shape: pl.BlockSpec(shape, lambda s: (0, 0))
    build_tile = lambda s: jnp.minimum(s, n - 1)
    proj_tile = lambda s: jnp.maximum(s - 1, 0)
    return pl.pallas_call(
        _out_proj_kernel,
        grid=(n + 1,),
        in_specs=[
            pl.BlockSpec((tm, SSD_WIDTH), lambda s: (build_tile(s), 0)),
            pl.BlockSpec((tm, SSD_WIDTH), lambda s: (build_tile(s), z_blk)),
            pl.BlockSpec((tm, CM_WIDTH), lambda s: (build_tile(s), u_blk)),
            pl.BlockSpec((tm, CM_WIDTH), lambda s: (build_tile(s), v_blk)),
            pl.BlockSpec((tm, D_MODEL), lambda s: (proj_tile(s), 0)),
            pl.BlockSpec((None, 1, D_MODEL), lambda s: (proj_tile(s) // tiles_per_batch, 0, g1_blk)),
            row((1, SSD_WIDTH)),
            row((1, CM_WIDTH)),
            row((1, CM_WIDTH)),
            pl.BlockSpec((CM_HEADS, CHUNK, CHUNK), lambda s: (0, 0, 0)),
            row((CHUNK, CM_HEADS)),
            pl.BlockSpec((SSD_WIDTH + CM_WIDTH, D_MODEL), lambda s: (0, 0), pipeline_mode=pl.Buffered(1)),
        ],
        out_specs=pl.BlockSpec((tm, D_MODEL), lambda s: (proj_tile(s), 0)),
        out_shape=jax.ShapeDtypeStruct((m, D_MODEL), F32),
        scratch_shapes=[pltpu.VMEM((tm, SSD_WIDTH + CM_WIDTH), BF16)] * 2,
        compiler_params=_params("arbitrary"),
        name="out_proj",
    )(yd, pm, pm, pm, x2, mod3, ssd_norm_g, ln_g, ln_b, ws, bst, w_out)


def _route(logits):
    lane = lax.broadcasted_iota(jnp.int32, logits.shape, 1)
    lane_f = lane.astype(F32)
    is_group = (lane >= N_EXPERTS) & (lane < N_EXPERTS + N_GROUPS)
    lg = jnp.where(is_group, logits, NEG_BIG)
    mg = jnp.max(lg, axis=1, keepdims=True)
    top_pg = 1.0 / jnp.sum(jnp.exp(lg - mg), axis=1, keepdims=True)
    gi = jnp.min(jnp.where(lg == mg, lane_f, 1e9), axis=1, keepdims=True) - N_EXPERTS
    in_group = (lane < N_EXPERTS) & ((lane // EXPERTS_PER_GROUP).astype(F32) == gi)
    le = jnp.where(in_group, logits, NEG_BIG)
    m1 = jnp.max(le, axis=1, keepdims=True)
    i1 = jnp.min(jnp.where(le == m1, lane_f, 1e9), axis=1, keepdims=True)
    le2 = jnp.where(lane_f == i1, NEG_BIG, le)
    m2 = jnp.max(le2, axis=1, keepdims=True)
    i2 = jnp.min(jnp.where(le2 == m2, lane_f, 1e9), axis=1, keepdims=True)
    e2 = jnp.exp(m2 - m1)
    p1 = 1.0 / (1.0 + e2)
    p2 = e2 * p1
    return i1, i2, p1 * top_pg, p2 * top_pg


RINFO_E1, RINFO_E2, RINFO_W1, RINFO_W2, RINFO_R1, RINFO_R2 = range(6)


def _route_kernel(x1_ref, n2g_ref, sh2_ref, sc2_ref, wr_ref, br_ref, h2_ref, rinfo_ref, counts_ref, cnt_scr):
    tm = x1_ref.shape[0]

    @pl.when(pl.program_id(0) == 0)
    def _():
        cnt_scr[...] = jnp.zeros_like(cnt_scr)

    x1 = x1_ref[...]
    ms = jnp.mean(x1 * x1, axis=-1, keepdims=True)
    h2 = (x1 * lax.rsqrt(ms + NORM_EPS) * n2g_ref[...]) * (1.0 + sc2_ref[...]) + sh2_ref[...]
    h2_ref[...] = h2
    logits = jnp.dot(h2, wr_ref[...], precision=HIGHEST, preferred_element_type=F32) + br_ref[...]
    i1, i2, w1, w2 = _route(logits)

    lane = lax.broadcasted_iota(jnp.int32, (tm, LANES), 1)
    lane_f = lane.astype(F32)
    oh1 = jnp.where(lane_f == i1, 1.0, 0.0)
    oh2 = jnp.where(lane_f == i2, 1.0, 0.0)
    before = (lax.broadcasted_iota(jnp.int32, (tm, tm), 0) > lax.broadcasted_iota(jnp.int32, (tm, tm), 1)).astype(BF16)
    carried = cnt_scr[...]
    tot1 = jnp.sum(oh1, axis=0, keepdims=True)
    r1 = jnp.sum(oh1 * (_dot(before, oh1.astype(BF16)) + carried), axis=1, keepdims=True)
    r2 = jnp.sum(oh2 * (_dot(before, oh2.astype(BF16)) + (carried + tot1)), axis=1, keepdims=True)
    counts = carried + tot1 + jnp.sum(oh2, axis=0, keepdims=True)
    cnt_scr[...] = counts
    counts_ref[...] = jnp.broadcast_to(counts, counts_ref.shape)

    info = jnp.zeros((tm, LANES), F32)
    for k, v in ((RINFO_E1, i1), (RINFO_E2, i2), (RINFO_W1, w1), (RINFO_W2, w2), (RINFO_R1, r1), (RINFO_R2, r2)):
        info = jnp.where(lane == k, v, info)
    rinfo_ref[...] = info


def _route_call(x1, mod3, norm2_g, w_router, b_router, *, seq, tm):
    m = x1.shape[0]
    tiles_per_batch = seq // tm
    row = lambda shape: pl.BlockSpec(shape, lambda i: (0, 0))
    modrow = lambda k: pl.BlockSpec((None, 1, D_MODEL), lambda i: (i // tiles_per_batch, 0, k))
    return pl.pallas_call(
        _route_kernel,
        grid=(m // tm,),
        in_specs=[
            pl.BlockSpec((tm, D_MODEL), lambda i: (i, 0)),
            row((1, D_MODEL)),
            modrow(3),
            modrow(4),
            row((D_MODEL, LANES)),
            row((1, LANES)),
        ],
        out_specs=[
            pl.BlockSpec((tm, D_MODEL), lambda i: (i, 0)),
            pl.BlockSpec((tm, LANES), lambda i: (i, 0)),
            row((8, LANES)),
        ],
        out_shape=[
            jax.ShapeDtypeStruct((m, D_MODEL), F32),
            jax.ShapeDtypeStruct((m, LANES), F32),
            jax.ShapeDtypeStruct((8, LANES), F32),
        ],
        scratch_shapes=[pltpu.VMEM((1, LANES), F32)],
        compiler_params=_params("arbitrary"),
        name="route",
    )(x1, norm2_g, mod3, mod3, w_router, b_router)


def _dispatch_kernel(lt_ref, nu_ref, pos_ref, h2_ref, x_hbm, zbuf, zsem, sem, *, tile_rows, n_tiles):
    tm = h2_ref.shape[0]
    n_used = nu_ref[0]

    def zero_copy(t):
        return pltpu.make_async_copy(zbuf, x_hbm.at[pl.ds(pl.multiple_of(t * tile_rows, tile_rows), tile_rows), :], zsem)

    @pl.when(pl.program_id(0) == 0)
    def _():
        zbuf[...] = jnp.zeros_like(zbuf)
        for phase in ("start", "wait"):
            for e in range(N_EXPERTS):
                @pl.when(lt_ref[e] >= 0)
                def _():
                    getattr(zero_copy(lt_ref[e]), phase)()

            def unused(t, carry):
                getattr(zero_copy(t), phase)()
                return carry

            lax.fori_loop(n_used, n_tiles, unused, 0)

    def body(r, carry):
        for k in range(2):
            pltpu.make_async_copy(h2_ref.at[pl.ds(r, 1), :], x_hbm.at[pl.ds(pos_ref[0, 2 * r + k], 1), :], sem).start()
        return carry

    lax.fori_loop(0, tm, body, 0, unroll=8)
    for k in range(2):
        pltpu.make_async_copy(h2_ref, x_hbm.at[pl.ds(0, tm), :], sem).wait()


def _dispatch_call(last_tile, n_used, pos, h2, *, tm, tile_rows, n_tiles):
    m = h2.shape[0]
    grid_spec = pltpu.PrefetchScalarGridSpec(
        num_scalar_prefetch=2,
        grid=(m // tm,),
        in_specs=[
            pl.BlockSpec((None, 1, 2 * tm), lambda i, lt, nu: (i, 0, 0), memory_space=pltpu.SMEM),
            pl.BlockSpec((tm, D_MODEL), lambda i, lt, nu: (i, 0)),
        ],
        out_specs=pl.BlockSpec(memory_space=pl.ANY),
        scratch_shapes=[
            pltpu.VMEM((tile_rows, D_MODEL), F32),
            pltpu.SemaphoreType.DMA(()),
            pltpu.SemaphoreType.DMA(()),
        ],
    )
    return pl.pallas_call(
        functools.partial(_dispatch_kernel, tile_rows=tile_rows, n_tiles=n_tiles),
        grid_spec=grid_spec,
        out_shape=jax.ShapeDtypeStruct((n_tiles * tile_rows, D_MODEL), F32),
        compiler_params=_params("arbitrary"),
        name="dispatch",
    )(last_tile, n_used, pos, h2)


def _experts_kernel(te_ref, nu_ref, ord_ref, nxt_ref, x_ref, wg_hbm, wu_hbm, wd_hbm, y_ref,
                    wg_f, wu_f, wd_f, wg_b, wu_b, wd_b, sem):
    j = pl.program_id(0)
    n_used = nu_ref[0]

    def weight_copies(e, slot):
        return [pltpu.make_async_copy(src.at[e], dst.at[slot], sem.at[slot])
                for src, dst in ((wg_hbm, wg_f), (wu_hbm, wu_f), (wd_hbm, wd_f))]

    @pl.when(j == 0)
    def _():
        for cp in weight_copies(te_ref[0], 0):
            cp.start()

    @pl.when((j < n_used) & ((j == 0) | (te_ref[j] != te_ref[jnp.maximum(j - 1, 0)])))
    def _():
        slot = ord_ref[j] % 2
        for cp in weight_copies(te_ref[j], slot):
            cp.wait()

        @pl.when(nxt_ref[j] >= 0)
        def _():
            for cp in weight_copies(nxt_ref[j], 1 - slot):
                cp.start()

        wg_b[...] = wg_f[slot].astype(BF16)
        wu_b[...] = wu_f[slot].astype(BF16)
        wd_b[...] = wd_f[slot].astype(BF16)

    @pl.when(j < n_used)
    def _():
        xt = x_ref[...].astype(BF16)
        hid = (_silu(_dot(xt, wg_b[...])) * _dot(xt, wu_b[...])).astype(BF16)
        y_ref[...] = _dot(hid, wd_b[...])

    @pl.when(j >= n_used)
    def _():
        y_ref[...] = jnp.zeros_like(y_ref)


def _experts_call(tile_expert, n_used, tile_ord, next_expert, x_sorted, wg, wu, wd, *, tm):
    n_tiles = tile_expert.shape[0]
    hbm = pl.BlockSpec(memory_space=pl.ANY)
    grid_spec = pltpu.PrefetchScalarGridSpec(
        num_scalar_prefetch=4,
        grid=(n_tiles,),
        in_specs=[
            pl.BlockSpec((tm, D_MODEL), lambda j, te, nu, od, nx: (jnp.minimum(j, nu[0] - 1), 0)),
            hbm,
            hbm,
            hbm,
        ],
        out_specs=pl.BlockSpec((tm, D_MODEL), lambda j, te, nu, od, nx: (j, 0)),
        scratch_shapes=[
            pltpu.VMEM((2, D_MODEL, EXPERT_FF), F32),
            pltpu.VMEM((2, D_MODEL, EXPERT_FF), F32),
            pltpu.VMEM((2, EXPERT_FF, D_MODEL), F32),
            pltpu.VMEM((D_MODEL, EXPERT_FF), BF16),
            pltpu.VMEM((D_MODEL, EXPERT_FF), BF16),
            pltpu.VMEM((EXPERT_FF, D_MODEL), BF16),
            pltpu.SemaphoreType.DMA((2,)),
        ],
    )
    return pl.pallas_call(
        _experts_kernel,
        grid_spec=grid_spec,
        out_shape=jax.ShapeDtypeStruct((n_tiles * tm, D_MODEL), F32),
        compiler_params=_params("arbitrary"),
        name="experts",
    )(tile_expert, n_used, tile_ord, next_expert, x_sorted, wg, wu, wd)


def _combine_kernel(pos_cur, pos_nxt, x1_ref, g2_ref, nfg_ref, rinfo_ref, y_hbm, o_ref, ybuf, sem):
    i = pl.program_id(0)
    tm = x1_ref.shape[0]
    slot = i % 2

    def row_copy(pos_ref, r, k, s):
        return pltpu.make_async_copy(y_hbm.at[pl.ds(pos_ref[0, 2 * r + k], 1), :], ybuf.at[s, k, pl.ds(r, 1), :],
                                     sem.at[s])

    def wait_tile(s):
        for k in range(2):
            pltpu.make_async_copy(y_hbm.at[pl.ds(0, tm), :], ybuf.at[s, k], sem.at[s]).wait()

    @pl.when(i == 0)
    def _():
        def body(r, carry):
            for k in range(2):
                row_copy(pos_cur, r, k, 0).start()
            return carry

        lax.fori_loop(0, tm, body, 0)

    wait_tile(slot)
    for r in range(tm):
        for k in range(2):
            row_copy(pos_nxt, r, k, 1 - slot).start(priority=k)
    info = rinfo_ref[...]
    w1 = info[:, RINFO_W1:RINFO_W1 + 1]
    w2 = info[:, RINFO_W2:RINFO_W2 + 1]
    moe = w1 * ybuf[slot, 0] + w2 * ybuf[slot, 1]
    y = x1_ref[...] + g2_ref[...] * moe
    ms = jnp.mean(y * y, axis=-1, keepdims=True)
    o_ref[...] = y * lax.rsqrt(ms + NORM_EPS) * nfg_ref[...]

    @pl.when(i == pl.num_programs(0) - 1)
    def _():
        wait_tile(1 - slot)


def _combine_call(pos, x1, mod3, normf_g, rinfo, y_sorted, *, seq, tm):
    m = x1.shape[0]
    n_tiles = m // tm
    tiles_per_batch = seq // tm
    return pl.pallas_call(
        _combine_kernel,
        grid=(n_tiles,),
        in_specs=[
            pl.BlockSpec((None, 1, 2 * tm), lambda i: (i, 0, 0), memory_space=pltpu.SMEM),
            pl.BlockSpec((None, 1, 2 * tm), lambda i: (jnp.minimum(i + 1, n_tiles - 1), 0, 0),
                         memory_space=pltpu.SMEM),
            pl.BlockSpec((tm, D_MODEL), lambda i: (i, 0)),
            pl.BlockSpec((None, 1, D_MODEL), lambda i: (i // tiles_per_batch, 0, 5)),
            pl.BlockSpec((1, D_MODEL), lambda i: (0, 0)),
            pl.BlockSpec((tm, LANES), lambda i: (i, 0)),
            pl.BlockSpec(memory_space=pl.ANY),
        ],
        out_specs=pl.BlockSpec((tm, D_MODEL), lambda i: (i, 0)),
        out_shape=jax.ShapeDtypeStruct((m, D_MODEL), F32),
        scratch_shapes=[
            pltpu.VMEM((2, 2, tm, D_MODEL), F32),
            pltpu.SemaphoreType.DMA((2,)),
        ],
        compiler_params=_params("arbitrary"),
        name="combine",
    )(pos, pos, x1, mod3, normf_g, rinfo, y_sorted)


def _dispatch_plan(rinfo, counts, *, tm):
    n_tok = rinfo.shape[0]
    n_tiles = (2 * n_tok + N_EXPERTS * (tm - 1)) // tm + 1
    expert = rinfo[:, RINFO_E1:RINFO_E2 + 1].astype(jnp.int32)
    rank = rinfo[:, RINFO_R1:RINFO_R2 + 1].astype(jnp.int32)
    cnt = counts[0, :N_EXPERTS].astype(jnp.int32)
    tiles_e = (cnt + tm - 1) // tm
    end_tile = jnp.cumsum(tiles_e)
    start_row = (end_tile - tiles_e) * tm
    is_e = expert[:, :, None] == jnp.arange(N_EXPERTS, dtype=jnp.int32)
    pos = jnp.sum(jnp.where(is_e, start_row, 0), axis=-1) + rank
    n_used = end_tile[-1:]
    tile_ids = jnp.minimum(jnp.arange(n_tiles, dtype=jnp.int32), n_used - 1)
    tile_expert = jnp.sum(tile_ids[:, None] >= end_tile[None, :], axis=1).astype(jnp.int32)
    last_tile = jnp.where(tiles_e > 0, end_tile - 1, -1).astype(jnp.int32)
    used = (tiles_e > 0).astype(jnp.int32)
    ord_e = jnp.cumsum(used) - used
    ids = jnp.arange(N_EXPERTS, dtype=jnp.int32)
    later = jnp.where((ids[None, :] > ids[:, None]) & (used[None, :] > 0), ids[None, :], N_EXPERTS)
    nxt_e = jnp.min(later, axis=1)
    nxt_e = jnp.where(nxt_e < N_EXPERTS, nxt_e, -1).astype(jnp.int32)
    is_te = tile_expert[:, None] == ids[None, :]
    tile_ord = jnp.sum(jnp.where(is_te, ord_e, 0), axis=1).astype(jnp.int32)
    next_expert = jnp.sum(jnp.where(is_te, nxt_e, 0), axis=1).astype(jnp.int32)
    return pos, (tile_expert, n_used.astype(jnp.int32), tile_ord, next_expert), last_tile, n_tiles


def kernel(x, c, ctx, c_ctx, w_mod, b_mod, norm1_g, w_in, conv_w, conv_b, dt_bias_f, dt_bias_b, a_log_f, a_log_b,
           d_skip, ssd_norm_g, cm_ln_g, cm_ln_b, w_spatial, b_spatial, w_out, norm2_g, w_router_group,
           b_router_group, w_router_expert, b_router_expert, w_exp_gate, w_exp_up, w_exp_down, normf_g):
    bsz, seq, _ = x.shape
    ctx_len = ctx.shape[1]
    i = 0

    cc = jnp.concatenate([c, c_ctx[None, :], jnp.zeros((MOD_ROWS - bsz - 1, D_MODEL), F32)], axis=0)
    mod = _modulation(cc, w_mod[i], b_mod[i][None, :])
    mod3 = mod.reshape(MOD_ROWS, 1, N_MOD * D_MODEL)

    w_main, w_dt = _w_in_prep(jnp.swapaxes(w_in[i], 0, 1))
    g1row = norm1_g[i][None, :]

    x2 = x.reshape(bsz * seq, D_MODEL)
    ctx2 = ctx.reshape(bsz * ctx_len, D_MODEL)
    tm_in = 1024
    pm_x, dt_x = _in_proj(x2, mod3, lambda t: t // (seq // tm_in), g1row, w_main, w_dt, tm_in, 1024)
    pm_c, dt_c = _in_proj(ctx2, mod3, lambda t: bsz, g1row, w_main, w_dt, bsz * ctx_len, 1024)

    lane_pad = (0, LANES - 2 * SSD_HEADS)
    dt_bias_row = jnp.pad(jnp.concatenate([dt_bias_f[i], dt_bias_b[i]]).astype(F32), lane_pad)[None, :]
    a_row = jnp.pad(-jnp.exp(jnp.concatenate([a_log_f[i], a_log_b[i]]).astype(F32)), lane_pad)[None, :]
    dskip_g = jnp.repeat(d_skip[i], SSD_HEAD_DIM).reshape(SSD_GROUPS, 1, GROUP_WIDTH)
    cw = conv_w[i]
    cb = conv_b[i][None, :]
    h_zero = jnp.zeros((bsz, SSD_GROUPS, SSD_STATE, GROUP_WIDTH), F32)

    dt_terms_c = _ssd_dt(dt_c, dt_bias_row, a_row, rows_per_step=bsz * ctx_len)
    dt_terms_x = _ssd_dt(dt_x, dt_bias_row, a_row, rows_per_step=1024)
    hc_f, hc_b = _ssd(pm_c, dt_terms_c, cw, cb, dskip_g, h_zero, h_zero,
                      batch=bsz, seq=ctx_len, conv_tile=ctx_len, conv_period=ctx_len, with_output=False)
    yd = _ssd(pm_x, dt_terms_x, cw, cb, dskip_g, hc_f, hc_b,
              batch=bsz, seq=seq, conv_tile=CHUNK, conv_period=GRID_W, with_output=True)

    x1 = _out_proj(yd, pm_x, x2, mod3, ssd_norm_g[i][None, :], cm_ln_g[i][None, :], cm_ln_b[i][None, :],
                   w_spatial[i].astype(BF16), b_spatial[i].T, w_out[i].astype(BF16), seq=seq, tm=256)

    w_re = jnp.transpose(w_router_expert[i], (1, 0, 2)).reshape(D_MODEL, N_EXPERTS)
    pad = LANES - N_EXPERTS - N_GROUPS
    w_router = jnp.pad(jnp.concatenate([w_re, w_router_group[i]], axis=1), ((0, 0), (0, pad)))
    b_router = jnp.pad(jnp.concatenate([b_router_expert[i].reshape(-1), b_router_group[i]]), (0, pad))[None, :]
    h2, rinfo, counts = _route_call(x1, mod3, norm2_g[i][None, :], w_router, b_router, seq=seq, tm=512)

    tm_e = 256
    pos, tile_tables, last_tile, n_tiles = _dispatch_plan(rinfo, counts, tm=tm_e)
    tm_d = 512
    x_sorted = _dispatch_call(last_tile, tile_tables[1], pos.reshape(bsz * seq // tm_d, 1, 2 * tm_d), h2,
                              tm=tm_d, tile_rows=tm_e, n_tiles=n_tiles)
    wg = w_exp_gate[i].reshape(N_EXPERTS, D_MODEL, EXPERT_FF)
    wu = w_exp_up[i].reshape(N_EXPERTS, D_MODEL, EXPERT_FF)
    wd = w_exp_down[i].reshape(N_EXPERTS, EXPERT_FF, D_MODEL)
    y_sorted = _experts_call(*tile_tables, x_sorted, wg, wu, wd, tm=tm_e)
    tm_c = 256
    out = _combine_call(pos.reshape(bsz * seq // tm_c, 1, 2 * tm_c), x1, mod3, normf_g[None, :], rinfo, y_sorted,
                        seq=seq, tm=tm_c)
    return out.reshape(bsz, seq, D_MODEL)
```

```python
import functools

import jax
import jax.numpy as jnp
from jax import lax
from jax.experimental import pallas as pl
from jax.experimental.pallas import tpu as pltpu

F32 = jnp.float32
BF16 = jnp.bfloat16
HIGHEST = lax.Precision.HIGHEST

D_MODEL = 2048
GRID_W = 64
SSD_WIDTH = 2048
CM_WIDTH = 2048
SSD_HEADS = 32
SSD_HEAD_DIM = 64
SSD_GROUPS = 4
HEADS_PER_GROUP = SSD_HEADS // SSD_GROUPS
GROUP_WIDTH = HEADS_PER_GROUP * SSD_HEAD_DIM
SSD_STATE = 128
CHUNK = 128
BC_WIDTH = SSD_GROUPS * SSD_STATE
CM_HEADS = 8
CM_HEAD_DIM = CM_WIDTH // CM_HEADS
N_GROUPS = 4
N_EXPERTS = 32
EXPERTS_PER_GROUP = 8
EXPERT_FF = 512
N_MOD = 6
NORM_EPS = 1e-6
LANES = 128
NEG_BIG = -1e30
MOD_ROWS = 8
VMEM_LIMIT = 56 * 1024 * 1024


def _params(*sem):
    return pltpu.CompilerParams(dimension_semantics=sem, vmem_limit_bytes=VMEM_LIMIT)


def _silu(v):
    return v * jax.nn.sigmoid(v)


def _dot(a, b):
    return jnp.dot(a, b, preferred_element_type=F32)


def _dot_nt(a, b):
    return lax.dot_general(a, b, (((1,), (1,)), ((), ())), preferred_element_type=F32)


def _mod_kernel(cc_ref, w_ref, b_ref, o_ref):
    a = _silu(cc_ref[...])
    hi = a.astype(BF16).astype(F32)
    lhs = jnp.concatenate([hi, a - hi], axis=0).astype(BF16)
    r = _dot(lhs, w_ref[...].astype(BF16))
    o_ref[...] = r[:MOD_ROWS] + r[MOD_ROWS:] + b_ref[...]


def _modulation(cc, w_mod, b_mod):
    n = w_mod.shape[1]
    tn = 1024
    return pl.pallas_call(
        _mod_kernel,
        grid=(n // tn,),
        in_specs=[
            pl.BlockSpec((MOD_ROWS, D_MODEL), lambda j: (0, 0)),
            pl.BlockSpec((D_MODEL, tn), lambda j: (0, j)),
            pl.BlockSpec((1, tn), lambda j: (0, j)),
        ],
        out_specs=pl.BlockSpec((MOD_ROWS, tn), lambda j: (0, j)),
        out_shape=jax.ShapeDtypeStruct((MOD_ROWS, n), F32),
        compiler_params=_params("arbitrary"),
        name="modulation",
    )(cc, w_mod, b_mod)


PREP_ROWS = 512
PREP_UV_BLK0 = 2 * SSD_WIDTH // PREP_ROWS
PREP_BC_BLK0 = (2 * SSD_WIDTH + 2 * CM_WIDTH) // PREP_ROWS
PREP_N_BLKS = PREP_BC_BLK0 + 2 * BC_WIDTH // PREP_ROWS
DT_COL0 = 2 * SSD_WIDTH + 2 * BC_WIDTH
UV_COL0 = DT_COL0 + 2 * SSD_HEADS


def _w_in_prep_kernel(a_ref, dt_ref, o_ref, wdt_ref):
    @pl.when(pl.program_id(0) == 0)
    def _():
        wdt_ref[...] = dt_ref[...].astype(BF16)

    o_ref[...] = a_ref[...].astype(BF16)


def _w_in_prep(w_in_t):
    def src_row(j):
        uv = UV_COL0 + (j - PREP_UV_BLK0) * PREP_ROWS
        bc = 2 * SSD_WIDTH + (j - PREP_BC_BLK0) * PREP_ROWS
        return pl.multiple_of(jnp.where(j < PREP_UV_BLK0, j * PREP_ROWS, jnp.where(j < PREP_BC_BLK0, uv, bc)), 8)

    return pl.pallas_call(
        _w_in_prep_kernel,
        grid=(PREP_N_BLKS,),
        in_specs=[
            pl.BlockSpec((pl.Element(PREP_ROWS), pl.Element(D_MODEL)), lambda j: (src_row(j), 0)),
            pl.BlockSpec((LANES, D_MODEL), lambda j: (DT_COL0 // LANES, 0)),
        ],
        out_specs=[
            pl.BlockSpec((PREP_ROWS, D_MODEL), lambda j: (j, 0)),
            pl.BlockSpec((LANES, D_MODEL), lambda j: (0, 0)),
        ],
        out_shape=[
            jax.ShapeDtypeStruct((PREP_N_BLKS * PREP_ROWS, D_MODEL), BF16),
            jax.ShapeDtypeStruct((LANES, D_MODEL), BF16),
        ],
        compiler_params=_params("arbitrary"),
        name="w_in_prep",
    )(w_in_t, w_in_t)


def _in_proj_kernel(x_ref, g_ref, sh_ref, sc_ref, w_ref, wdt_ref, o_ref, dt_ref, h_scr):
    tm = x_ref.shape[0]
    rc = 256

    @pl.when(pl.program_id(1) == 0)
    def _():
        gain = g_ref[...]
        scale = 1.0 + sc_ref[...]
        shift = sh_ref[...]

        def body(k, carry):
            r = pl.ds(pl.multiple_of(k * rc, rc), rc)
            xv = x_ref[r, :]
            ms = jnp.mean(xv * xv, axis=-1, keepdims=True)
            h = (xv * lax.rsqrt(ms + NORM_EPS) * gain) * scale + shift
            h_scr[r, :] = h.astype(BF16)
            return carry

        lax.fori_loop(0, tm // rc, body, 0)
        dt_ref[...] = _dot_nt(h_scr[...], wdt_ref[...])

    o_ref[...] = _dot_nt(h_scr[...], w_ref[...]).astype(o_ref.dtype)


def _in_proj(x2, mod3, mod_row_of_tile, norm_g, w_main, w_dt, tm, tn, w_blocks=None):
    m = x2.shape[0]
    if w_blocks is None:
        w_blocks = tuple(range(w_main.shape[0] // tn))
    n = len(w_blocks) * tn

    def w_blk(j):
        if w_blocks == tuple(range(len(w_blocks))):
            return j
        blk = jnp.int32(w_blocks[-1])
        for q in range(len(w_blocks) - 2, -1, -1):
            blk = jnp.where(j == q, w_blocks[q], blk)
        return blk

    return pl.pallas_call(
        _in_proj_kernel,
        grid=(m // tm, n // tn),
        in_specs=[
            pl.BlockSpec((tm, D_MODEL), lambda i, j: (i, 0)),
            pl.BlockSpec((1, D_MODEL), lambda i, j: (0, 0)),
            pl.BlockSpec((None, 1, D_MODEL), lambda i, j: (mod_row_of_tile(i), 0, 0)),
            pl.BlockSpec((None, 1, D_MODEL), lambda i, j: (mod_row_of_tile(i), 0, 1)),
            pl.BlockSpec((tn, D_MODEL), lambda i, j: (w_blk(j), 0)),
            pl.BlockSpec((LANES, D_MODEL), lambda i, j: (0, 0)),
        ],
        out_specs=[
            pl.BlockSpec((tm, tn), lambda i, j: (i, j)),
            pl.BlockSpec((tm, LANES), lambda i, j: (i, 0)),
        ],
        out_shape=[
            jax.ShapeDtypeStruct((m, n), BF16),
            jax.ShapeDtypeStruct((m, LANES), F32),
        ],
        scratch_shapes=[pltpu.VMEM((tm, D_MODEL), BF16)],
        compiler_params=_params("parallel", "arbitrary"),
        name="in_proj",
    )(x2, norm_g, mod3, mod3, w_main, w_dt)


def _split3(v):
    hi = v.astype(BF16)
    rem = v - hi.astype(F32)
    mid = rem.astype(BF16)
    lo = (rem - mid.astype(F32)).astype(BF16)
    return hi, mid, lo


GROUP_LANES = 2 * HEADS_PER_GROUP


def _stack_split3(v):
    hi, mid, lo = _split3(v)
    stacked = (hi.astype(F32) + pltpu.roll(mid.astype(F32), GROUP_LANES, 1)
               + pltpu.roll(lo.astype(F32), 2 * GROUP_LANES, 1))
    return stacked.astype(BF16)


LOG2_E = 1.4426950408889634


def _ssd_dt_kernel(dt_ref, bias_ref, a_ref, dts_ref, cum_ref, src_t_ref):
    hpg = HEADS_PER_GROUP
    ii = lax.broadcasted_iota(jnp.int32, (CHUNK, CHUNK), 0)
    jj = lax.broadcasted_iota(jnp.int32, (CHUNK, CHUNK), 1)
    tri = (ii >= jj).astype(BF16)
    tri3 = jnp.concatenate([tri, tri, tri], axis=1)
    bias = bias_ref[...]
    a_row = a_ref[...]

    def body(k, carry):
        r = pl.ds(pl.multiple_of(k * CHUNK, CHUNK), CHUNK)
        raw = dt_ref[r, :] + bias
        dts = jnp.where(jj < 2 * SSD_HEADS, jnp.maximum(raw, 0.0) + jnp.log1p(jnp.exp(-jnp.abs(raw))), 0.0)
        adt = dts * a_row
        hi, mid, lo = _split3(adt)
        cf = _dot(tri3, jnp.concatenate([hi, mid, lo], axis=0))
        cr = cf[CHUNK - 1:CHUNK, :] - cf + adt
        valid = jj < 2 * SSD_HEADS
        cum = jnp.where(jj < SSD_HEADS, cf, cr)
        src_t = ((cum - jnp.log(jnp.where(valid, dts, 1.0))) * LOG2_E).T
        rt = pl.ds(pl.multiple_of(k * GROUP_LANES, GROUP_LANES), GROUP_LANES)
        for g in range(SSD_GROUPS):
            f0, b0 = g * hpg, SSD_HEADS + g * hpg
            for src, dst in ((cum * LOG2_E, cum_ref), (dts, dts_ref)):
                lanes = jnp.where(jj < hpg, pltpu.roll(src, (LANES - f0) % LANES, 1),
                                  pltpu.roll(src, (LANES - (b0 - hpg)) % LANES, 1))
                dst[g, r, :] = jnp.where(jj < GROUP_LANES, lanes, 0.0)
            src_t_ref[g, rt, :] = jnp.concatenate([src_t[f0:f0 + hpg, :], src_t[b0:b0 + hpg, :]], axis=0)
        return carry

    lax.fori_loop(0, dt_ref.shape[0] // CHUNK, body, 0)


def _ssd_dt(dt, bias_row, a_row, *, rows_per_step):
    m = dt.shape[0]
    row = pl.BlockSpec((1, LANES), lambda i: (0, 0))
    full = pl.BlockSpec((SSD_GROUPS, rows_per_step, LANES), lambda i: (0, i, 0))
    tr = pl.BlockSpec((SSD_GROUPS, rows_per_step // CHUNK * GROUP_LANES, LANES), lambda i: (0, i, 0))
    full_shape = jax.ShapeDtypeStruct((SSD_GROUPS, m, LANES), F32)
    tr_shape = jax.ShapeDtypeStruct((SSD_GROUPS, m // CHUNK * GROUP_LANES, LANES), F32)
    return pl.pallas_call(
        _ssd_dt_kernel,
        grid=(m // rows_per_step,),
        in_specs=[pl.BlockSpec((rows_per_step, LANES), lambda i: (i, 0)), row, row],
        out_specs=[full, full, tr],
        out_shape=[full_shape, full_shape, tr_shape],
        compiler_params=_params("parallel"),
        name="ssd_dt",
    )(dt, bias_row, a_row)


def _ssd_kernel(xs_ref, b_ref, c_ref, dts_ref, cum_ref, src_t_ref, cwx_ref, cbx_ref, cwb_ref, cbb_ref,
                cwc_ref, cbc_ref, dskip_ref, h0f_ref, h0b_ref, *rest,
                n_chunks, conv_tile, conv_period, with_output):
    if with_output:
        y_ref, xs_s, b_s, c_s, hf_s, hb_s, y_s, yb_s = rest
    else:
        hf_out, hb_out, xs_s, b_s, c_s, hf_s, hb_s = rest
    seq = n_chunks * CHUNK
    hpg = HEADS_PER_GROUP

    def conv_silu(src_ref, dst_ref, w_ref, bias_ref):
        cols = src_ref.shape[1]
        w0, w1, w2 = w_ref[0:1, :], w_ref[1:2, :], w_ref[2:3, :]
        bias = bias_ref[...]
        row = lax.broadcasted_iota(jnp.int32, (conv_tile, cols), 0) % conv_period
        first = row == 0
        last = row == conv_period - 1

        def body(k, carry):
            r = pl.ds(pl.multiple_of(k * conv_tile, conv_tile), conv_tile)
            v = src_ref[r, :].astype(F32)
            prev = jnp.where(first, 0.0, pltpu.roll(v, 1, 0))
            nxt = jnp.where(last, 0.0, pltpu.roll(v, conv_tile - 1, 0))
            y = bias + prev * w0 + v * w1 + nxt * w2
            dst_ref[r, :] = _silu(y).astype(BF16)
            return carry

        lax.fori_loop(0, seq // conv_tile, body, 0)

    conv_silu(xs_ref, xs_s, cwx_ref, cbx_ref)
    conv_silu(b_ref, b_s, cwb_ref, cbb_ref)
    conv_silu(c_ref, c_s, cwc_ref, cbc_ref)

    ii = lax.broadcasted_iota(jnp.int32, (CHUNK, CHUNK), 0)
    jj = lax.broadcasted_iota(jnp.int32, (CHUNK, CHUNK), 1)
    lower = ii >= jj
    upper = jj >= ii

    hf_s[...] = h0f_ref[...]
    hb_s[...] = h0b_ref[...]

    head_of_col = lax.broadcasted_iota(jnp.int32, (LANES, GROUP_WIDTH), 1) // SSD_HEAD_DIM
    sel_row = lax.broadcasted_iota(jnp.int32, (LANES, GROUP_WIDTH), 0)
    sel_valid = sel_row < 3 * GROUP_LANES
    sel_f = (sel_valid & (sel_row % GROUP_LANES == head_of_col)).astype(BF16)
    sel_b = (sel_valid & (sel_row % GROUP_LANES == head_of_col + hpg)).astype(BF16)
    low_half = jj < SSD_HEAD_DIM
    group_lane = jj < GROUP_LANES

    def direction_terms(r, sel):
        cum = cum_ref[r, :]
        dts = dts_ref[r, :]
        tot = jnp.where(jj[0:1, :] < hpg, cum[CHUNK - 1:CHUNK, :], cum[0:1, :])
        wst = dts * jnp.exp2(tot - cum)
        eoff = jnp.where(group_lane, jnp.exp2(cum), 0.0)
        edec = jnp.broadcast_to(jnp.where(group_lane[0:1, :], jnp.exp2(tot), 0.0), (8, LANES))
        ex = _dot(_stack_split3(jnp.concatenate([wst, eoff, edec], axis=0)), sel)
        return cum, ex[:CHUNK], ex[CHUNK:2 * CHUNK], ex[2 * CHUNK:2 * CHUNK + 1]

    def state_step(h_s, r, sel):
        cum, w_state, e_off, e_dec = direction_terms(r, sel)
        xc = xs_s[r, :]
        xw = (xc.astype(F32) * w_state).astype(BF16)
        s_new = lax.dot_general(b_s[r, :], xw, (((0,), (0,)), ((), ())), preferred_element_type=F32)
        h = h_s[...]
        y_off = _dot(c_s[r, :], h.astype(BF16)) * e_off
        h_s[...] = h * e_dec + s_new
        return cum, xc, y_off

    def fwd_body(k, carry):
        r = pl.ds(pl.multiple_of(k * CHUNK, CHUNK), CHUNK)
        cum, xc, y_off = state_step(hf_s, r, sel_f)
        if with_output:
            g = _dot_nt(c_s[r, :], b_s[r, :])
            rt = pl.ds(pl.multiple_of(k * GROUP_LANES, GROUP_LANES), GROUP_LANES)
            src_t = src_t_ref[rt, :]
            pieces = []
            for pair in range(hpg // 2):
                ms = []
                for h in (2 * pair, 2 * pair + 1):
                    hb = h + hpg
                    mf = jnp.exp2(jnp.where(lower, cum[:, h:h + 1] - src_t[h:h + 1, :], NEG_BIG))
                    mb = jnp.exp2(jnp.where(upper, cum[:, hb:hb + 1] - src_t[hb:hb + 1, :], NEG_BIG))
                    ms.append((g * (mf + mb)).astype(BF16))
                lhs = jnp.concatenate(ms, axis=1)
                xp = xc[:, pair * LANES:(pair + 1) * LANES]
                zero = jnp.zeros_like(xp)
                rhs = jnp.concatenate([jnp.where(low_half, xp, zero), jnp.where(low_half, zero, xp)], axis=0)
                pieces.append(_dot(lhs, rhs))
            y_diag = jnp.concatenate(pieces, axis=1)
            y_s[r, :] = y_diag + y_off + xc.astype(F32) * dskip_ref[...]
        rb = pl.ds(pl.multiple_of((n_chunks - 1 - k) * CHUNK, CHUNK), CHUNK)
        _, _, y_off_b = state_step(hb_s, rb, sel_b)
        if with_output:
            yb_s[rb, :] = y_off_b
        return carry

    lax.fori_loop(0, n_chunks, fwd_body, 0, unroll=4 if n_chunks % 4 == 0 else 2)

    if with_output:
        def sum_body(k, carry):
            r = pl.ds(pl.multiple_of(k * CHUNK, CHUNK), CHUNK)
            y_ref[r, :] = (y_s[r, :] + yb_s[r, :]).astype(y_ref.dtype)
            return carry

        lax.fori_loop(0, n_chunks, sum_body, 0)

    if not with_output:
        hf_out[...] = hf_s[...]
        hb_out[...] = hb_s[...]


def _ssd(pm, xs_col0, bc_col0, dt_terms, conv_w, conv_b, dskip_g, h0f, h0b, *, batch, seq, conv_tile, conv_period,
         with_output):
    n_chunks = seq // CHUNK
    dts_g, cum_g, src_t_g = dt_terms
    dt_spec = pl.BlockSpec((None, seq, LANES), lambda b, g: (g, b, 0))
    dt_t_spec = pl.BlockSpec((None, n_chunks * GROUP_LANES, LANES), lambda b, g: (g, b, 0))
    xs_blk0 = xs_col0 // GROUP_WIDTH
    b_blk0 = bc_col0 // SSD_STATE
    c_blk0 = b_blk0 + SSD_GROUPS
    cw_b0 = SSD_WIDTH // SSD_STATE
    cw_c0 = cw_b0 + SSD_GROUPS
    state_spec = pl.BlockSpec((None, None, SSD_STATE, GROUP_WIDTH), lambda b, g: (b, g, 0, 0))
    in_specs = [
        pl.BlockSpec((seq, GROUP_WIDTH), lambda b, g: (b, xs_blk0 + g)),
        pl.BlockSpec((seq, SSD_STATE), lambda b, g: (b, b_blk0 + g)),
        pl.BlockSpec((seq, SSD_STATE), lambda b, g: (b, c_blk0 + g)),
        dt_spec,
        dt_spec,
        dt_t_spec,
        pl.BlockSpec((3, GROUP_WIDTH), lambda b, g: (0, g)),
        pl.BlockSpec((1, GROUP_WIDTH), lambda b, g: (0, g)),
        pl.BlockSpec((3, SSD_STATE), lambda b, g: (0, cw_b0 + g)),
        pl.BlockSpec((1, SSD_STATE), lambda b, g: (0, cw_b0 + g)),
        pl.BlockSpec((3, SSD_STATE), lambda b, g: (0, cw_c0 + g)),
        pl.BlockSpec((1, SSD_STATE), lambda b, g: (0, cw_c0 + g)),
        pl.BlockSpec((None, 1, GROUP_WIDTH), lambda b, g: (g, 0, 0)),
        state_spec,
        state_spec,
    ]
    scratch = [
        pltpu.VMEM((seq, GROUP_WIDTH), BF16),
        pltpu.VMEM((seq, SSD_STATE), BF16),
        pltpu.VMEM((seq, SSD_STATE), BF16),
        pltpu.VMEM((SSD_STATE, GROUP_WIDTH), F32),
        pltpu.VMEM((SSD_STATE, GROUP_WIDTH), F32),
    ]
    if with_output:
        out_specs = pl.BlockSpec((seq, GROUP_WIDTH), lambda b, g: (b, g))
        out_shape = jax.ShapeDtypeStruct((batch * seq, SSD_WIDTH), BF16)
        scratch += [pltpu.VMEM((seq, GROUP_WIDTH), F32)] * 2
    else:
        out_specs = [state_spec, state_spec]
        out_shape = [jax.ShapeDtypeStruct((batch, SSD_GROUPS, SSD_STATE, GROUP_WIDTH), F32)] * 2
    return pl.pallas_call(
        functools.partial(_ssd_kernel, n_chunks=n_chunks, conv_tile=conv_tile, conv_period=conv_period,
                          with_output=with_output),
        grid=(batch, SSD_GROUPS),
        in_specs=in_specs,
        out_specs=out_specs,
        out_shape=out_shape,
        scratch_shapes=scratch,
        compiler_params=_params("parallel", "parallel"),
        name="ssd_out" if with_output else "ssd_ctx",
    )(pm, pm, pm, dts_g, cum_g, src_t_g, conv_w, conv_b, conv_w, conv_b, conv_w, conv_b, dskip_g, h0f, h0b)


def _out_proj_kernel(yd_ref, z_ref, u_ref, v_ref, x_ref, g1_ref, ng_ref, lng_ref, lnb_ref, ws_ref, bst_ref, w_ref,
                     o_ref, mix_cur, mix_nxt):
    tm = yd_ref.shape[0]
    s = pl.program_id(0)
    n = pl.num_programs(0) - 1

    def build_steps():
        steps = []
        for k in range(tm // CHUNK):
            r = pl.ds(k * CHUNK, CHUNK)
            shared = {}

            def ssd_part(r=r):
                a = yd_ref[r, :].astype(F32) * _silu(z_ref[r, :].astype(F32))
                ms = jnp.mean(a * a, axis=-1, keepdims=True)
                mix_nxt[r, 0:SSD_WIDTH] = (a * lax.rsqrt(ms + NORM_EPS) * ng_ref[...]).astype(BF16)

            def ln_part(r=r, shared=shared):
                gv = jax.nn.gelu(v_ref[r, :].astype(F32))
                mu = jnp.mean(gv, axis=-1, keepdims=True)
                xc = gv - mu
                var = jnp.mean(xc * xc, axis=-1, keepdims=True)
                shared["ln"] = ((xc * lax.rsqrt(var + NORM_EPS)) * lng_ref[...] + lnb_ref[...]).astype(BF16)

            def gate_part(h0, h1, r=r, shared=shared):
                for h in range(h0, h1):
                    c0, c1 = h * CM_HEAD_DIM, (h + 1) * CM_HEAD_DIM
                    sp = _dot(ws_ref[h], shared["ln"][:, c0:c1]) + bst_ref[:, h:h + 1]
                    gu = jax.nn.gelu(u_ref[r, c0:c1].astype(F32))
                    mix_nxt[r, SSD_WIDTH + c0:SSD_WIDTH + c1] = (gu * sp).astype(BF16)

            steps += [ssd_part, ln_part, functools.partial(gate_part, 0, CM_HEADS // 2),
                      functools.partial(gate_part, CM_HEADS // 2, CM_HEADS)]
        return steps

    n_proj = 8
    pw = D_MODEL // n_proj

    def project_steps():
        def piece(q):
            c = slice(q * pw, (q + 1) * pw)
            o_ref[:, c] = x_ref[:, c] + g1_ref[:, c] * _dot(mix_cur[...], w_ref[:, c])

        return [functools.partial(piece, q) for q in range(n_proj)]

    @pl.when(s == 0)
    def _():
        for step in build_steps():
            step()

    @pl.when((s > 0) & (s < n))
    def _():
        for proj, build in zip(project_steps(), build_steps()):
            proj()
            build()

    @pl.when(s == n)
    def _():
        for step in project_steps():
            step()

    @pl.when(s < n)
    def _():
        mix_cur[...] = mix_nxt[...]


def _out_proj(yd, pm, x2, mod3, ssd_norm_g, ln_g, ln_b, ws, bst, w_out, *, seq, tm):
    m = x2.shape[0]
    n = m // tm
    tiles_per_batch = seq // tm
    z_blk, u_blk, v_blk = 0, 2 * SSD_WIDTH // CM_WIDTH, 2 * SSD_WIDTH // CM_WIDTH + 1
    g1_blk = 2
    row = lambda shape: pl.BlockSpec(shape, lambda s: (0, 0))
    build_tile = lambda s: jnp.minimum(s, n - 1)
    proj_tile = lambda s: jnp.maximum(s - 1, 0)
    return pl.pallas_call(
        _out_proj_kernel,
        grid=(n + 1,),
        in_specs=[
            pl.BlockSpec((tm, SSD_WIDTH), lambda s: (build_tile(s), 0)),
            pl.BlockSpec((tm, SSD_WIDTH), lambda s: (build_tile(s), z_blk)),
            pl.BlockSpec((tm, CM_WIDTH), lambda s: (build_tile(s), u_blk)),
            pl.BlockSpec((tm, CM_WIDTH), lambda s: (build_tile(s), v_blk)),
            pl.BlockSpec((tm, D_MODEL), lambda s: (proj_tile(s), 0)),
            pl.BlockSpec((None, 1, D_MODEL), lambda s: (proj_tile(s) // tiles_per_batch, 0, g1_blk)),
            row((1, SSD_WIDTH)),
            row((1, CM_WIDTH)),
            row((1, CM_WIDTH)),
            pl.BlockSpec((CM_HEADS, CHUNK, CHUNK), lambda s: (0, 0, 0)),
            row((CHUNK, CM_HEADS)),
            pl.BlockSpec((SSD_WIDTH + CM_WIDTH, D_MODEL), lambda s: (0, 0), pipeline_mode=pl.Buffered(1)),
        ],
        out_specs=pl.BlockSpec((tm, D_MODEL), lambda s: (proj_tile(s), 0)),
        out_shape=jax.ShapeDtypeStruct((m, D_MODEL), F32),
        scratch_shapes=[pltpu.VMEM((tm, SSD_WIDTH + CM_WIDTH), BF16)] * 2,
        compiler_params=_params("arbitrary"),
        name="out_proj",
    )(yd, pm, pm, pm, x2, mod3, ssd_norm_g, ln_g, ln_b, ws, bst, w_out)


def _route(logits):
    lane = lax.broadcasted_iota(jnp.int32, logits.shape, 1)
    lane_f = lane.astype(F32)
    is_group = (lane >= N_EXPERTS) & (lane < N_EXPERTS + N_GROUPS)
    lg = jnp.where(is_group, logits, NEG_BIG)
    mg = jnp.max(lg, axis=1, keepdims=True)
    top_pg = 1.0 / jnp.sum(jnp.exp(lg - mg), axis=1, keepdims=True)
    gi = jnp.min(jnp.where(lg == mg, lane_f, 1e9), axis=1, keepdims=True) - N_EXPERTS
    in_group = (lane < N_EXPERTS) & ((lane // EXPERTS_PER_GROUP).astype(F32) == gi)
    le = jnp.where(in_group, logits, NEG_BIG)
    m1 = jnp.max(le, axis=1, keepdims=True)
    i1 = jnp.min(jnp.where(le == m1, lane_f, 1e9), axis=1, keepdims=True)
    le2 = jnp.where(lane_f == i1, NEG_BIG, le)
    m2 = jnp.max(le2, axis=1, keepdims=True)
    i2 = jnp.min(jnp.where(le2 == m2, lane_f, 1e9), axis=1, keepdims=True)
    e2 = jnp.exp(m2 - m1)
    p1 = 1.0 / (1.0 + e2)
    p2 = e2 * p1
    return i1, i2, p1 * top_pg, p2 * top_pg


RINFO_E1, RINFO_E2, RINFO_W1, RINFO_W2, RINFO_R1, RINFO_R2 = range(6)


def _route_kernel(x1_ref, n2g_ref, sh2_ref, sc2_ref, wr_ref, br_ref, h2_ref, rinfo_ref, counts_ref, cnt_scr):
    tm = x1_ref.shape[0]

    @pl.when(pl.program_id(0) == 0)
    def _():
        cnt_scr[...] = jnp.zeros_like(cnt_scr)

    x1 = x1_ref[...]
    ms = jnp.mean(x1 * x1, axis=-1, keepdims=True)
    h2 = (x1 * lax.rsqrt(ms + NORM_EPS) * n2g_ref[...]) * (1.0 + sc2_ref[...]) + sh2_ref[...]
    h2_ref[...] = h2
    hi, mid, _ = _split3(h2)
    p = _dot(hi, wr_ref[...])
    logits = p[:, :LANES] + p[:, LANES:] + _dot(mid, wr_ref[:, :LANES]) + br_ref[...]
    i1, i2, w1, w2 = _route(logits)

    lane = lax.broadcasted_iota(jnp.int32, (tm, LANES), 1)
    lane_f = lane.astype(F32)
    oh1 = jnp.where(lane_f == i1, 1.0, 0.0)
    oh2 = jnp.where(lane_f == i2, 1.0, 0.0)
    before = (lax.broadcasted_iota(jnp.int32, (tm, tm), 0) > lax.broadcasted_iota(jnp.int32, (tm, tm), 1)).astype(BF16)
    carried = cnt_scr[...]
    tot1 = jnp.sum(oh1, axis=0, keepdims=True)
    r1 = jnp.sum(oh1 * (_dot(before, oh1.astype(BF16)) + carried), axis=1, keepdims=True)
    r2 = jnp.sum(oh2 * (_dot(before, oh2.astype(BF16)) + (carried + tot1)), axis=1, keepdims=True)
    counts = carried + tot1 + jnp.sum(oh2, axis=0, keepdims=True)
    cnt_scr[...] = counts
    counts_ref[...] = jnp.broadcast_to(counts, counts_ref.shape)

    info = jnp.zeros((tm, LANES), F32)
    for k, v in ((RINFO_E1, i1), (RINFO_E2, i2), (RINFO_W1, w1), (RINFO_W2, w2), (RINFO_R1, r1), (RINFO_R2, r2)):
        info = jnp.where(lane == k, v, info)
    rinfo_ref[...] = info


def _route_call(x1, mod3, norm2_g, w_router, b_router, *, seq, tm):
    m = x1.shape[0]
    tiles_per_batch = seq // tm
    row = lambda shape: pl.BlockSpec(shape, lambda i: (0, 0))
    modrow = lambda k: pl.BlockSpec((None, 1, D_MODEL), lambda i: (i // tiles_per_batch, 0, k))
    return pl.pallas_call(
        _route_kernel,
        grid=(m // tm,),
        in_specs=[
            pl.BlockSpec((tm, D_MODEL), lambda i: (i, 0)),
            row((1, D_MODEL)),
            modrow(3),
            modrow(4),
            row((D_MODEL, 2 * LANES)),
            row((1, LANES)),
        ],
        out_specs=[
            pl.BlockSpec((tm, D_MODEL), lambda i: (i, 0)),
            pl.BlockSpec((tm, LANES), lambda i: (i, 0)),
            row((8, LANES)),
        ],
        out_shape=[
            jax.ShapeDtypeStruct((m, D_MODEL), F32),
            jax.ShapeDtypeStruct((m, LANES), F32),
            jax.ShapeDtypeStruct((8, LANES), F32),
        ],
        scratch_shapes=[pltpu.VMEM((1, LANES), F32)],
        compiler_params=_params("arbitrary"),
        name="route",
    )(x1, norm2_g, mod3, mod3, w_router, b_router)


def _dispatch_kernel(lt_ref, nu_ref, pos_ref, h2_ref, x_hbm, zbuf, zsem, sem, *, tile_rows, n_tiles):
    tm = h2_ref.shape[0]
    n_used = nu_ref[0]

    def zero_copy(t):
        return pltpu.make_async_copy(zbuf, x_hbm.at[pl.ds(pl.multiple_of(t * tile_rows, tile_rows), tile_rows), :], zsem)

    @pl.when(pl.program_id(0) == 0)
    def _():
        zbuf[...] = jnp.zeros_like(zbuf)
        for phase in ("start", "wait"):
            for e in range(N_EXPERTS):
                @pl.when(lt_ref[e] >= 0)
                def _():
                    getattr(zero_copy(lt_ref[e]), phase)()

            def unused(t, carry):
                getattr(zero_copy(t), phase)()
                return carry

            lax.fori_loop(n_used, n_tiles, unused, 0)

    def body(r, carry):
        for k in range(2):
            pltpu.make_async_copy(h2_ref.at[pl.ds(r, 1), :], x_hbm.at[pl.ds(pos_ref[0, 2 * r + k], 1), :], sem).start()
        return carry

    lax.fori_loop(0, tm, body, 0, unroll=8)
    for k in range(2):
        pltpu.make_async_copy(h2_ref, x_hbm.at[pl.ds(0, tm), :], sem).wait()


def _dispatch_call(last_tile, n_used, pos, h2, *, tm, tile_rows, n_tiles):
    m = h2.shape[0]
    grid_spec = pltpu.PrefetchScalarGridSpec(
        num_scalar_prefetch=2,
        grid=(m // tm,),
        in_specs=[
            pl.BlockSpec((None, 1, 2 * tm), lambda i, lt, nu: (i, 0, 0), memory_space=pltpu.SMEM),
            pl.BlockSpec((tm, D_MODEL), lambda i, lt, nu: (i, 0)),
        ],
        out_specs=pl.BlockSpec(memory_space=pl.ANY),
        scratch_shapes=[
            pltpu.VMEM((tile_rows, D_MODEL), F32),
            pltpu.SemaphoreType.DMA(()),
            pltpu.SemaphoreType.DMA(()),
        ],
    )
    return pl.pallas_call(
        functools.partial(_dispatch_kernel, tile_rows=tile_rows, n_tiles=n_tiles),
        grid_spec=grid_spec,
        out_shape=jax.ShapeDtypeStruct((n_tiles * tile_rows, D_MODEL), F32),
        compiler_params=_params("arbitrary"),
        name="dispatch",
    )(last_tile, n_used, pos, h2)


def _experts_kernel(te_ref, nu_ref, ord_ref, nxt_ref, x_ref, wg_hbm, wu_hbm, wd_hbm, y_ref,
                    wg_f, wu_f, wd_f, wg_b, wu_b, wd_b, sem):
    j = pl.program_id(0)
    n_used = nu_ref[0]

    def weight_copies(e, slot):
        return [pltpu.make_async_copy(src.at[e], dst.at[slot], sem.at[slot])
                for src, dst in ((wg_hbm, wg_f), (wu_hbm, wu_f), (wd_hbm, wd_f))]

    @pl.when(j == 0)
    def _():
        for cp in weight_copies(te_ref[0], 0):
            cp.start()

    @pl.when((j < n_used) & ((j == 0) | (te_ref[j] != te_ref[jnp.maximum(j - 1, 0)])))
    def _():
        slot = ord_ref[j] % 2
        for cp in weight_copies(te_ref[j], slot):
            cp.wait()

        @pl.when(nxt_ref[j] >= 0)
        def _():
            for cp in weight_copies(nxt_ref[j], 1 - slot):
                cp.start()

        wg_b[...] = wg_f[slot].astype(BF16)
        wu_b[...] = wu_f[slot].astype(BF16)
        wd_b[...] = wd_f[slot].astype(BF16)

    @pl.when(j < n_used)
    def _():
        xt = x_ref[...].astype(BF16)
        hid = (_silu(_dot(xt, wg_b[...])) * _dot(xt, wu_b[...])).astype(BF16)
        y_ref[...] = _dot(hid, wd_b[...])

    @pl.when(j >= n_used)
    def _():
        y_ref[...] = jnp.zeros_like(y_ref)


def _experts_call(tile_expert, n_used, tile_ord, next_expert, x_sorted, wg, wu, wd, *, tm):
    n_tiles = tile_expert.shape[0]
    hbm = pl.BlockSpec(memory_space=pl.ANY)
    grid_spec = pltpu.PrefetchScalarGridSpec(
        num_scalar_prefetch=4,
        grid=(n_tiles,),
        in_specs=[
            pl.BlockSpec((tm, D_MODEL), lambda j, te, nu, od, nx: (jnp.minimum(j, nu[0] - 1), 0)),
            hbm,
            hbm,
            hbm,
        ],
        out_specs=pl.BlockSpec((tm, D_MODEL), lambda j, te, nu, od, nx: (j, 0)),
        scratch_shapes=[
            pltpu.VMEM((2, D_MODEL, EXPERT_FF), F32),
            pltpu.VMEM((2, D_MODEL, EXPERT_FF), F32),
            pltpu.VMEM((2, EXPERT_FF, D_MODEL), F32),
            pltpu.VMEM((D_MODEL, EXPERT_FF), BF16),
            pltpu.VMEM((D_MODEL, EXPERT_FF), BF16),
            pltpu.VMEM((EXPERT_FF, D_MODEL), BF16),
            pltpu.SemaphoreType.DMA((2,)),
        ],
    )
    return pl.pallas_call(
        _experts_kernel,
        grid_spec=grid_spec,
        out_shape=jax.ShapeDtypeStruct((n_tiles * tm, D_MODEL), F32),
        compiler_params=_params("arbitrary"),
        name="experts",
    )(tile_expert, n_used, tile_ord, next_expert, x_sorted, wg, wu, wd)


def _combine_kernel(pos_cur, pos_nxt, x1_ref, g2_ref, nfg_ref, rinfo_ref, y_hbm, o_ref, ybuf, sem):
    i = pl.program_id(0)
    tm = x1_ref.shape[0]
    slot = i % 2

    def row_copy(pos_ref, r, k, s):
        return pltpu.make_async_copy(y_hbm.at[pl.ds(pos_ref[0, 2 * r + k], 1), :], ybuf.at[s, k, pl.ds(r, 1), :],
                                     sem.at[s])

    def wait_tile(s):
        for k in range(2):
            pltpu.make_async_copy(y_hbm.at[pl.ds(0, tm), :], ybuf.at[s, k], sem.at[s]).wait()

    @pl.when(i == 0)
    def _():
        def body(r, carry):
            for k in range(2):
                row_copy(pos_cur, r, k, 0).start()
            return carry

        lax.fori_loop(0, tm, body, 0)

    wait_tile(slot)
    for r in range(tm):
        for k in range(2):
            row_copy(pos_nxt, r, k, 1 - slot).start(priority=k)
    info = rinfo_ref[...]
    w1 = info[:, RINFO_W1:RINFO_W1 + 1]
    w2 = info[:, RINFO_W2:RINFO_W2 + 1]
    moe = w1 * ybuf[slot, 0] + w2 * ybuf[slot, 1]
    y = x1_ref[...] + g2_ref[...] * moe
    ms = jnp.mean(y * y, axis=-1, keepdims=True)
    o_ref[...] = y * lax.rsqrt(ms + NORM_EPS) * nfg_ref[...]

    @pl.when(i == pl.num_programs(0) - 1)
    def _():
        wait_tile(1 - slot)


def _combine_call(pos, x1, mod3, normf_g, rinfo, y_sorted, *, seq, tm):
    m = x1.shape[0]
    n_tiles = m // tm
    tiles_per_batch = seq // tm
    return pl.pallas_call(
        _combine_kernel,
        grid=(n_tiles,),
        in_specs=[
            pl.BlockSpec((None, 1, 2 * tm), lambda i: (i, 0, 0), memory_space=pltpu.SMEM),
            pl.BlockSpec((None, 1, 2 * tm), lambda i: (jnp.minimum(i + 1, n_tiles - 1), 0, 0),
                         memory_space=pltpu.SMEM),
            pl.BlockSpec((tm, D_MODEL), lambda i: (i, 0)),
            pl.BlockSpec((None, 1, D_MODEL), lambda i: (i // tiles_per_batch, 0, 5)),
            pl.BlockSpec((1, D_MODEL), lambda i: (0, 0)),
            pl.BlockSpec((tm, LANES), lambda i: (i, 0)),
            pl.BlockSpec(memory_space=pl.ANY),
        ],
        out_specs=pl.BlockSpec((tm, D_MODEL), lambda i: (i, 0)),
        out_shape=jax.ShapeDtypeStruct((m, D_MODEL), F32),
        scratch_shapes=[
            pltpu.VMEM((2, 2, tm, D_MODEL), F32),
            pltpu.SemaphoreType.DMA((2,)),
        ],
        compiler_params=_params("arbitrary"),
        name="combine",
    )(pos, pos, x1, mod3, normf_g, rinfo, y_sorted)


def _dispatch_plan(rinfo, counts, *, tm):
    n_tok = rinfo.shape[0]
    n_tiles = (2 * n_tok + N_EXPERTS * (tm - 1)) // tm + 1
    expert = rinfo[:, RINFO_E1:RINFO_E2 + 1].astype(jnp.int32)
    rank = rinfo[:, RINFO_R1:RINFO_R2 + 1].astype(jnp.int32)
    cnt = counts[0, :N_EXPERTS].astype(jnp.int32)
    tiles_e = (cnt + tm - 1) // tm
    end_tile = jnp.cumsum(tiles_e)
    start_row = (end_tile - tiles_e) * tm
    is_e = expert[:, :, None] == jnp.arange(N_EXPERTS, dtype=jnp.int32)
    pos = jnp.sum(jnp.where(is_e, start_row, 0), axis=-1) + rank
    n_used = end_tile[-1:]
    tile_ids = jnp.minimum(jnp.arange(n_tiles, dtype=jnp.int32), n_used - 1)
    tile_expert = jnp.sum(tile_ids[:, None] >= end_tile[None, :], axis=1).astype(jnp.int32)
    last_tile = jnp.where(tiles_e > 0, end_tile - 1, -1).astype(jnp.int32)
    used = (tiles_e > 0).astype(jnp.int32)
    ord_e = jnp.cumsum(used) - used
    ids = jnp.arange(N_EXPERTS, dtype=jnp.int32)
    later = jnp.where((ids[None, :] > ids[:, None]) & (used[None, :] > 0), ids[None, :], N_EXPERTS)
    nxt_e = jnp.min(later, axis=1)
    nxt_e = jnp.where(nxt_e < N_EXPERTS, nxt_e, -1).astype(jnp.int32)
    is_te = tile_expert[:, None] == ids[None, :]
    tile_ord = jnp.sum(jnp.where(is_te, ord_e, 0), axis=1).astype(jnp.int32)
    next_expert = jnp.sum(jnp.where(is_te, nxt_e, 0), axis=1).astype(jnp.int32)
    return pos, (tile_expert, n_used.astype(jnp.int32), tile_ord, next_expert), last_tile, n_tiles


def kernel(x, c, ctx, c_ctx, w_mod, b_mod, norm1_g, w_in, conv_w, conv_b, dt_bias_f, dt_bias_b, a_log_f, a_log_b,
           d_skip, ssd_norm_g, cm_ln_g, cm_ln_b, w_spatial, b_spatial, w_out, norm2_g, w_router_group,
           b_router_group, w_router_expert, b_router_expert, w_exp_gate, w_exp_up, w_exp_down, normf_g):
    bsz, seq, _ = x.shape
    ctx_len = ctx.shape[1]
    i = 0

    cc = jnp.concatenate([c, c_ctx[None, :], jnp.zeros((MOD_ROWS - bsz - 1, D_MODEL), F32)], axis=0)
    mod = _modulation(cc, w_mod[i], b_mod[i][None, :])
    mod3 = mod.reshape(MOD_ROWS, 1, N_MOD * D_MODEL)

    w_main, w_dt = _w_in_prep(jnp.swapaxes(w_in[i], 0, 1))
    g1row = norm1_g[i][None, :]

    x2 = x.reshape(bsz * seq, D_MODEL)
    ctx2 = ctx.reshape(bsz * ctx_len, D_MODEL)
    tm_in = 1024
    tn_in = 1024
    xs_col0, bc_col0 = SSD_WIDTH, 2 * SSD_WIDTH + 2 * CM_WIDTH
    pm_x, dt_x = _in_proj(x2, mod3, lambda t: t // (seq // tm_in), g1row, w_main, w_dt, tm_in, tn_in)
    ctx_blocks = tuple(range(xs_col0 // tn_in, 2 * SSD_WIDTH // tn_in)) + (bc_col0 // tn_in,)
    pm_c, dt_c = _in_proj(ctx2, mod3, lambda t: bsz, g1row, w_main, w_dt, bsz * ctx_len, tn_in, ctx_blocks)

    lane_pad = (0, LANES - 2 * SSD_HEADS)
    dt_bias_row = jnp.pad(jnp.concatenate([dt_bias_f[i], dt_bias_b[i]]).astype(F32), lane_pad)[None, :]
    a_row = jnp.pad(-jnp.exp(jnp.concatenate([a_log_f[i], a_log_b[i]]).astype(F32)), lane_pad)[None, :]
    dskip_g = jnp.repeat(d_skip[i], SSD_HEAD_DIM).reshape(SSD_GROUPS, 1, GROUP_WIDTH)
    cw = conv_w[i]
    cb = conv_b[i][None, :]
    h_zero = jnp.zeros((bsz, SSD_GROUPS, SSD_STATE, GROUP_WIDTH), F32)

    dt_terms_c = _ssd_dt(dt_c, dt_bias_row, a_row, rows_per_step=bsz * ctx_len)
    dt_terms_x = _ssd_dt(dt_x, dt_bias_row, a_row, rows_per_step=1024)
    hc_f, hc_b = _ssd(pm_c, 0, SSD_WIDTH, dt_terms_c, cw, cb, dskip_g, h_zero, h_zero,
                      batch=bsz, seq=ctx_len, conv_tile=ctx_len, conv_period=ctx_len, with_output=False)
    yd = _ssd(pm_x, xs_col0, bc_col0, dt_terms_x, cw, cb, dskip_g, hc_f, hc_b,
              batch=bsz, seq=seq, conv_tile=CHUNK, conv_period=GRID_W, with_output=True)

    x1 = _out_proj(yd, pm_x, x2, mod3, ssd_norm_g[i][None, :], cm_ln_g[i][None, :], cm_ln_b[i][None, :],
                   w_spatial[i].astype(BF16), b_spatial[i].T, w_out[i].astype(BF16), seq=seq, tm=256)

    w_re = jnp.transpose(w_router_expert[i], (1, 0, 2)).reshape(D_MODEL, N_EXPERTS)
    pad = LANES - N_EXPERTS - N_GROUPS
    w_router = jnp.pad(jnp.concatenate([w_re, w_router_group[i]], axis=1), ((0, 0), (0, pad)))
    w_router_hi = w_router.astype(BF16)
    w_router = jnp.concatenate([w_router_hi, (w_router - w_router_hi.astype(F32)).astype(BF16)], axis=1)
    b_router = jnp.pad(jnp.concatenate([b_router_expert[i].reshape(-1), b_router_group[i]]), (0, pad))[None, :]
    h2, rinfo, counts = _route_call(x1, mod3, norm2_g[i][None, :], w_router, b_router, seq=seq, tm=512)

    tm_e = 256
    pos, tile_tables, last_tile, n_tiles = _dispatch_plan(rinfo, counts, tm=tm_e)
    tm_d = 512
    x_sorted = _dispatch_call(last_tile, tile_tables[1], pos.reshape(bsz * seq // tm_d, 1, 2 * tm_d), h2,
                              tm=tm_d, tile_rows=tm_e, n_tiles=n_tiles)
    wg = w_exp_gate[i].reshape(N_EXPERTS, D_MODEL, EXPERT_FF)
    wu = w_exp_up[i].reshape(N_EXPERTS, D_MODEL, EXPERT_FF)
    wd = w_exp_down[i].reshape(N_EXPERTS, EXPERT_FF, D_MODEL)
    y_sorted = _experts_call(*tile_tables, x_sorted, wg, wu, wd, tm=tm_e)
    tm_c = 256
    out = _combine_call(pos.reshape(bsz * seq // tm_c, 1, 2 * tm_c), x1, mod3, normf_g[None, :], rinfo, y_sorted,
                        seq=seq, tm=tm_c)
    return out.reshape(bsz, seq, D_MODEL)
```

```python
import functools

import jax
import jax.numpy as jnp
from jax import lax
from jax.experimental import pallas as pl
from jax.experimental.pallas import tpu as pltpu

F32 = jnp.float32
BF16 = jnp.bfloat16
HIGHEST = lax.Precision.HIGHEST

D_MODEL = 2048
GRID_W = 64
SSD_WIDTH = 2048
CM_WIDTH = 2048
SSD_HEADS = 32
SSD_HEAD_DIM = 64
SSD_GROUPS = 4
HEADS_PER_GROUP = SSD_HEADS // SSD_GROUPS
GROUP_WIDTH = HEADS_PER_GROUP * SSD_HEAD_DIM
SSD_STATE = 128
CHUNK = 128
BC_WIDTH = SSD_GROUPS * SSD_STATE
CM_HEADS = 8
CM_HEAD_DIM = CM_WIDTH // CM_HEADS
N_GROUPS = 4
N_EXPERTS = 32
EXPERTS_PER_GROUP = 8
EXPERT_FF = 512
N_MOD = 6
NORM_EPS = 1e-6
LANES = 128
NEG_BIG = -1e30
MOD_ROWS = 8
VMEM_LIMIT = 56 * 1024 * 1024


def _params(*sem):
    return pltpu.CompilerParams(dimension_semantics=sem, vmem_limit_bytes=VMEM_LIMIT)


LOG2_E = 1.4426950408889634
GELU_C = 0.7978845608028654
GELU_K1 = -2.0 * GELU_C * LOG2_E
GELU_K3 = GELU_K1 * 0.044715


def _silu(v):
    return v / (1.0 + jnp.exp2(v * (-LOG2_E)))


def _gelu_tanh(v):
    return v / (1.0 + jnp.exp2(v * (GELU_K1 + GELU_K3 * (v * v))))


def _dot(a, b):
    return jnp.dot(a, b, preferred_element_type=F32)


def _dot_nt(a, b):
    return lax.dot_general(a, b, (((1,), (1,)), ((), ())), preferred_element_type=F32)


def _mod_kernel(cc_ref, w_ref, b_ref, o_ref):
    a = _silu(cc_ref[...])
    hi = a.astype(BF16).astype(F32)
    lhs = jnp.concatenate([hi, a - hi], axis=0).astype(BF16)
    r = _dot(lhs, w_ref[...].astype(BF16))
    o_ref[...] = r[:MOD_ROWS] + r[MOD_ROWS:] + b_ref[...]


def _modulation(cc, w_mod, b_mod):
    n = w_mod.shape[1]
    tn = 1024
    return pl.pallas_call(
        _mod_kernel,
        grid=(n // tn,),
        in_specs=[
            pl.BlockSpec((MOD_ROWS, D_MODEL), lambda j: (0, 0)),
            pl.BlockSpec((D_MODEL, tn), lambda j: (0, j)),
            pl.BlockSpec((1, tn), lambda j: (0, j)),
        ],
        out_specs=pl.BlockSpec((MOD_ROWS, tn), lambda j: (0, j)),
        out_shape=jax.ShapeDtypeStruct((MOD_ROWS, n), F32),
        compiler_params=_params("arbitrary"),
        name="modulation",
    )(cc, w_mod, b_mod)


PREP_ROWS = 512
PREP_UV_BLK0 = 2 * SSD_WIDTH // PREP_ROWS
PREP_BC_BLK0 = (2 * SSD_WIDTH + 2 * CM_WIDTH) // PREP_ROWS
PREP_N_BLKS = PREP_BC_BLK0 + 2 * BC_WIDTH // PREP_ROWS
DT_COL0 = 2 * SSD_WIDTH + 2 * BC_WIDTH
UV_COL0 = DT_COL0 + 2 * SSD_HEADS


def _w_in_prep_kernel(a_ref, dt_ref, o_ref, wdt_ref):
    @pl.when(pl.program_id(0) == 0)
    def _():
        wdt_ref[...] = dt_ref[...].astype(BF16)

    o_ref[...] = a_ref[...].astype(BF16)


def _w_in_prep(w_in_t):
    def src_row(j):
        uv = UV_COL0 + (j - PREP_UV_BLK0) * PREP_ROWS
        bc = 2 * SSD_WIDTH + (j - PREP_BC_BLK0) * PREP_ROWS
        return pl.multiple_of(jnp.where(j < PREP_UV_BLK0, j * PREP_ROWS, jnp.where(j < PREP_BC_BLK0, uv, bc)), 8)

    return pl.pallas_call(
        _w_in_prep_kernel,
        grid=(PREP_N_BLKS,),
        in_specs=[
            pl.BlockSpec((pl.Element(PREP_ROWS), pl.Element(D_MODEL)), lambda j: (src_row(j), 0)),
            pl.BlockSpec((LANES, D_MODEL), lambda j: (DT_COL0 // LANES, 0)),
        ],
        out_specs=[
            pl.BlockSpec((PREP_ROWS, D_MODEL), lambda j: (j, 0)),
            pl.BlockSpec((LANES, D_MODEL), lambda j: (0, 0)),
        ],
        out_shape=[
            jax.ShapeDtypeStruct((PREP_N_BLKS * PREP_ROWS, D_MODEL), BF16),
            jax.ShapeDtypeStruct((LANES, D_MODEL), BF16),
        ],
        compiler_params=_params("arbitrary"),
        name="w_in_prep",
    )(w_in_t, w_in_t)


def _in_proj_kernel(x_ref, g_ref, sh_ref, sc_ref, w_ref, wdt_ref, o_ref, dt_ref, h_scr):
    tm = x_ref.shape[0]
    rc = 256

    @pl.when(pl.program_id(1) == 0)
    def _():
        gain = g_ref[...]
        scale = 1.0 + sc_ref[...]
        shift = sh_ref[...]

        def body(k, carry):
            r = pl.ds(pl.multiple_of(k * rc, rc), rc)
            xv = x_ref[r, :]
            ms = jnp.mean(xv * xv, axis=-1, keepdims=True)
            h = (xv * lax.rsqrt(ms + NORM_EPS) * gain) * scale + shift
            h_scr[r, :] = h.astype(BF16)
            return carry

        lax.fori_loop(0, tm // rc, body, 0)
        dt_ref[...] = _dot_nt(h_scr[...], wdt_ref[...])

    o_ref[...] = _dot_nt(h_scr[...], w_ref[...]).astype(o_ref.dtype)


def _in_proj(x2, mod3, mod_row_of_tile, norm_g, w_main, w_dt, tm, tn, w_blocks=None):
    m = x2.shape[0]
    if w_blocks is None:
        w_blocks = tuple(range(w_main.shape[0] // tn))
    n = len(w_blocks) * tn

    def w_blk(j):
        if w_blocks == tuple(range(len(w_blocks))):
            return j
        blk = jnp.int32(w_blocks[-1])
        for q in range(len(w_blocks) - 2, -1, -1):
            blk = jnp.where(j == q, w_blocks[q], blk)
        return blk

    return pl.pallas_call(
        _in_proj_kernel,
        grid=(m // tm, n // tn),
        in_specs=[
            pl.BlockSpec((tm, D_MODEL), lambda i, j: (i, 0)),
            pl.BlockSpec((1, D_MODEL), lambda i, j: (0, 0)),
            pl.BlockSpec((None, 1, D_MODEL), lambda i, j: (mod_row_of_tile(i), 0, 0)),
            pl.BlockSpec((None, 1, D_MODEL), lambda i, j: (mod_row_of_tile(i), 0, 1)),
            pl.BlockSpec((tn, D_MODEL), lambda i, j: (w_blk(j), 0)),
            pl.BlockSpec((LANES, D_MODEL), lambda i, j: (0, 0)),
        ],
        out_specs=[
            pl.BlockSpec((tm, tn), lambda i, j: (i, j)),
            pl.BlockSpec((tm, LANES), lambda i, j: (i, 0)),
        ],
        out_shape=[
            jax.ShapeDtypeStruct((m, n), BF16),
            jax.ShapeDtypeStruct((m, LANES), F32),
        ],
        scratch_shapes=[pltpu.VMEM((tm, D_MODEL), BF16)],
        compiler_params=_params("parallel", "arbitrary"),
        name="in_proj",
    )(x2, norm_g, mod3, mod3, w_main, w_dt)


def _split3(v):
    hi = v.astype(BF16)
    rem = v - hi.astype(F32)
    mid = rem.astype(BF16)
    lo = (rem - mid.astype(F32)).astype(BF16)
    return hi, mid, lo


GROUP_LANES = 2 * HEADS_PER_GROUP


def _stack_split3(v):
    hi, mid, lo = _split3(v)
    stacked = (hi.astype(F32) + pltpu.roll(mid.astype(F32), GROUP_LANES, 1)
               + pltpu.roll(lo.astype(F32), 2 * GROUP_LANES, 1))
    return stacked.astype(BF16)


def _ssd_dt_kernel(dt_ref, bias_ref, a_ref, dts_ref, cum_ref, src_t_ref):
    hpg = HEADS_PER_GROUP
    ii = lax.broadcasted_iota(jnp.int32, (CHUNK, CHUNK), 0)
    jj = lax.broadcasted_iota(jnp.int32, (CHUNK, CHUNK), 1)
    tri = (ii >= jj).astype(BF16)
    tri3 = jnp.concatenate([tri, tri, tri], axis=1)
    bias = bias_ref[...]
    a_row = a_ref[...]

    def body(k, carry):
        r = pl.ds(pl.multiple_of(k * CHUNK, CHUNK), CHUNK)
        raw = dt_ref[r, :] + bias
        dts = jnp.where(jj < 2 * SSD_HEADS, jnp.maximum(raw, 0.0) + jnp.log1p(jnp.exp(-jnp.abs(raw))), 0.0)
        adt = dts * a_row
        hi, mid, lo = _split3(adt)
        cf = _dot(tri3, jnp.concatenate([hi, mid, lo], axis=0))
        cr = cf[CHUNK - 1:CHUNK, :] - cf + adt
        valid = jj < 2 * SSD_HEADS
        cum = jnp.where(jj < SSD_HEADS, cf, cr)
        src_t = ((cum - jnp.log(jnp.where(valid, dts, 1.0))) * LOG2_E).T
        rt = pl.ds(pl.multiple_of(k * GROUP_LANES, GROUP_LANES), GROUP_LANES)
        for g in range(SSD_GROUPS):
            f0, b0 = g * hpg, SSD_HEADS + g * hpg
            for src, dst in ((cum * LOG2_E, cum_ref), (dts, dts_ref)):
                lanes = jnp.where(jj < hpg, pltpu.roll(src, (LANES - f0) % LANES, 1),
                                  pltpu.roll(src, (LANES - (b0 - hpg)) % LANES, 1))
                dst[g, r, :] = jnp.where(jj < GROUP_LANES, lanes, 0.0)
            src_t_ref[g, rt, :] = jnp.concatenate([src_t[f0:f0 + hpg, :], src_t[b0:b0 + hpg, :]], axis=0)
        return carry

    lax.fori_loop(0, dt_ref.shape[0] // CHUNK, body, 0)


def _ssd_dt(dt, bias_row, a_row, *, rows_per_step):
    m = dt.shape[0]
    row = pl.BlockSpec((1, LANES), lambda i: (0, 0))
    full = pl.BlockSpec((SSD_GROUPS, rows_per_step, LANES), lambda i: (0, i, 0))
    tr = pl.BlockSpec((SSD_GROUPS, rows_per_step // CHUNK * GROUP_LANES, LANES), lambda i: (0, i, 0))
    full_shape = jax.ShapeDtypeStruct((SSD_GROUPS, m, LANES), F32)
    tr_shape = jax.ShapeDtypeStruct((SSD_GROUPS, m // CHUNK * GROUP_LANES, LANES), F32)
    return pl.pallas_call(
        _ssd_dt_kernel,
        grid=(m // rows_per_step,),
        in_specs=[pl.BlockSpec((rows_per_step, LANES), lambda i: (i, 0)), row, row],
        out_specs=[full, full, tr],
        out_shape=[full_shape, full_shape, tr_shape],
        compiler_params=_params("parallel"),
        name="ssd_dt",
    )(dt, bias_row, a_row)


def _ssd_kernel(xs_ref, b_ref, c_ref, dts_ref, cum_ref, src_t_ref, cwx_ref, cbx_ref, cwb_ref, cbb_ref,
                cwc_ref, cbc_ref, dskip_ref, h0f_ref, h0b_ref, *rest,
                n_chunks, conv_tile, conv_period, with_output):
    if with_output:
        y_ref, xs_s, b_s, c_s, hf_s, hb_s, y_s, yb_s = rest
    else:
        hf_out, hb_out, xs_s, b_s, c_s, hf_s, hb_s = rest
    seq = n_chunks * CHUNK
    hpg = HEADS_PER_GROUP

    def conv_silu(src_ref, dst_ref, w_ref, bias_ref):
        cols = src_ref.shape[1]
        w1 = w_ref[1:2, :]
        bias = bias_ref[...]
        row = lax.broadcasted_iota(jnp.int32, (conv_tile, cols), 0) % conv_period
        w0 = jnp.where(row == 0, 0.0, w_ref[0:1, :])
        w2 = jnp.where(row == conv_period - 1, 0.0, w_ref[2:3, :])

        def body(k, carry):
            r = pl.ds(pl.multiple_of(k * conv_tile, conv_tile), conv_tile)
            v = src_ref[r, :].astype(F32)
            y = bias + pltpu.roll(v, 1, 0) * w0 + v * w1 + pltpu.roll(v, conv_tile - 1, 0) * w2
            dst_ref[r, :] = _silu(y).astype(BF16)
            return carry

        lax.fori_loop(0, seq // conv_tile, body, 0)

    conv_silu(xs_ref, xs_s, cwx_ref, cbx_ref)
    conv_silu(b_ref, b_s, cwb_ref, cbb_ref)
    conv_silu(c_ref, c_s, cwc_ref, cbc_ref)

    ii = lax.broadcasted_iota(jnp.int32, (CHUNK, CHUNK), 0)
    jj = lax.broadcasted_iota(jnp.int32, (CHUNK, CHUNK), 1)
    lower = ii >= jj
    upper = jj >= ii

    hf_s[...] = h0f_ref[...]
    hb_s[...] = h0b_ref[...]

    head_of_col = lax.broadcasted_iota(jnp.int32, (LANES, GROUP_WIDTH), 1) // SSD_HEAD_DIM
    sel_row = lax.broadcasted_iota(jnp.int32, (LANES, GROUP_WIDTH), 0)
    sel_valid = sel_row < 3 * GROUP_LANES
    sel_f = (sel_valid & (sel_row % GROUP_LANES == head_of_col)).astype(BF16)
    sel_b = (sel_valid & (sel_row % GROUP_LANES == head_of_col + hpg)).astype(BF16)
    low_half = jj < SSD_HEAD_DIM
    group_lane = jj < GROUP_LANES

    def direction_terms(r, sel):
        cum = cum_ref[r, :]
        dts = dts_ref[r, :]
        tot = jnp.where(jj[0:1, :] < hpg, cum[CHUNK - 1:CHUNK, :], cum[0:1, :])
        wst = dts * jnp.exp2(tot - cum)
        eoff = jnp.where(group_lane, jnp.exp2(cum), 0.0)
        edec = jnp.broadcast_to(jnp.where(group_lane[0:1, :], jnp.exp2(tot), 0.0), (8, LANES))
        ex = _dot(_stack_split3(jnp.concatenate([wst, eoff, edec], axis=0)), sel)
        return cum, ex[:CHUNK], ex[CHUNK:2 * CHUNK], ex[2 * CHUNK:2 * CHUNK + 1]

    def state_step(h_s, r, sel):
        cum, w_state, e_off, e_dec = direction_terms(r, sel)
        xc = xs_s[r, :]
        xw = (xc.astype(F32) * w_state).astype(BF16)
        s_new = lax.dot_general(b_s[r, :], xw, (((0,), (0,)), ((), ())), preferred_element_type=F32)
        h = h_s[...]
        y_off = _dot(c_s[r, :], h.astype(BF16)) * e_off
        h_s[...] = h * e_dec + s_new
        return cum, xc, y_off

    def fwd_body(k, carry):
        r = pl.ds(pl.multiple_of(k * CHUNK, CHUNK), CHUNK)
        cum, xc, y_off = state_step(hf_s, r, sel_f)
        if with_output:
            g = _dot_nt(c_s[r, :], b_s[r, :])
            rt = pl.ds(pl.multiple_of(k * GROUP_LANES, GROUP_LANES), GROUP_LANES)
            src_t = src_t_ref[rt, :]
            pieces = []
            for pair in range(hpg // 2):
                ms = []
                for h in (2 * pair, 2 * pair + 1):
                    hb = h + hpg
                    mf = jnp.exp2(jnp.where(lower, cum[:, h:h + 1] - src_t[h:h + 1, :], NEG_BIG))
                    mb = jnp.exp2(jnp.where(upper, cum[:, hb:hb + 1] - src_t[hb:hb + 1, :], NEG_BIG))
                    ms.append((g * (mf + mb)).astype(BF16))
                lhs = jnp.concatenate(ms, axis=1)
                xp = xc[:, pair * LANES:(pair + 1) * LANES]
                zero = jnp.zeros_like(xp)
                rhs = jnp.concatenate([jnp.where(low_half, xp, zero), jnp.where(low_half, zero, xp)], axis=0)
                pieces.append(_dot(lhs, rhs))
            y_diag = jnp.concatenate(pieces, axis=1)
            y_s[r, :] = y_diag + y_off + xc.astype(F32) * dskip_ref[...]
        rb = pl.ds(pl.multiple_of((n_chunks - 1 - k) * CHUNK, CHUNK), CHUNK)
        _, _, y_off_b = state_step(hb_s, rb, sel_b)
        if with_output:
            yb_s[rb, :] = y_off_b
        return carry

    lax.fori_loop(0, n_chunks, fwd_body, 0, unroll=4 if n_chunks % 4 == 0 else 2)

    if with_output:
        def sum_body(k, carry):
            r = pl.ds(pl.multiple_of(k * CHUNK, CHUNK), CHUNK)
            y_ref[r, :] = (y_s[r, :] + yb_s[r, :]).astype(y_ref.dtype)
            return carry

        lax.fori_loop(0, n_chunks, sum_body, 0)

    if not with_output:
        hf_out[...] = hf_s[...]
        hb_out[...] = hb_s[...]


def _ssd(pm, xs_col0, bc_col0, dt_terms, conv_w, conv_b, dskip_g, h0f, h0b, *, batch, seq, conv_tile, conv_period,
         with_output):
    n_chunks = seq // CHUNK
    dts_g, cum_g, src_t_g = dt_terms
    dt_spec = pl.BlockSpec((None, seq, LANES), lambda b, g: (g, b, 0))
    dt_t_spec = pl.BlockSpec((None, n_chunks * GROUP_LANES, LANES), lambda b, g: (g, b, 0))
    xs_blk0 = xs_col0 // GROUP_WIDTH
    b_blk0 = bc_col0 // SSD_STATE
    c_blk0 = b_blk0 + SSD_GROUPS
    cw_b0 = SSD_WIDTH // SSD_STATE
    cw_c0 = cw_b0 + SSD_GROUPS
    state_spec = pl.BlockSpec((None, None, SSD_STATE, GROUP_WIDTH), lambda b, g: (b, g, 0, 0))
    in_specs = [
        pl.BlockSpec((seq, GROUP_WIDTH), lambda b, g: (b, xs_blk0 + g)),
        pl.BlockSpec((seq, SSD_STATE), lambda b, g: (b, b_blk0 + g)),
        pl.BlockSpec((seq, SSD_STATE), lambda b, g: (b, c_blk0 + g)),
        dt_spec,
        dt_spec,
        dt_t_spec,
        pl.BlockSpec((3, GROUP_WIDTH), lambda b, g: (0, g)),
        pl.BlockSpec((1, GROUP_WIDTH), lambda b, g: (0, g)),
        pl.BlockSpec((3, SSD_STATE), lambda b, g: (0, cw_b0 + g)),
        pl.BlockSpec((1, SSD_STATE), lambda b, g: (0, cw_b0 + g)),
        pl.BlockSpec((3, SSD_STATE), lambda b, g: (0, cw_c0 + g)),
        pl.BlockSpec((1, SSD_STATE), lambda b, g: (0, cw_c0 + g)),
        pl.BlockSpec((None, 1, GROUP_WIDTH), lambda b, g: (g, 0, 0)),
        state_spec,
        state_spec,
    ]
    scratch = [
        pltpu.VMEM((seq, GROUP_WIDTH), BF16),
        pltpu.VMEM((seq, SSD_STATE), BF16),
        pltpu.VMEM((seq, SSD_STATE), BF16),
        pltpu.VMEM((SSD_STATE, GROUP_WIDTH), F32),
        pltpu.VMEM((SSD_STATE, GROUP_WIDTH), F32),
    ]
    if with_output:
        out_specs = pl.BlockSpec((seq, GROUP_WIDTH), lambda b, g: (b, g))
        out_shape = jax.ShapeDtypeStruct((batch * seq, SSD_WIDTH), BF16)
        scratch += [pltpu.VMEM((seq, GROUP_WIDTH), F32)] * 2
    else:
        out_specs = [state_spec, state_spec]
        out_shape = [jax.ShapeDtypeStruct((batch, SSD_GROUPS, SSD_STATE, GROUP_WIDTH), F32)] * 2
    return pl.pallas_call(
        functools.partial(_ssd_kernel, n_chunks=n_chunks, conv_tile=conv_tile, conv_period=conv_period,
                          with_output=with_output),
        grid=(batch, SSD_GROUPS),
        in_specs=in_specs,
        out_specs=out_specs,
        out_shape=out_shape,
        scratch_shapes=scratch,
        compiler_params=_params("parallel", "parallel"),
        name="ssd_out" if with_output else "ssd_ctx",
    )(pm, pm, pm, dts_g, cum_g, src_t_g, conv_w, conv_b, conv_w, conv_b, conv_w, conv_b, dskip_g, h0f, h0b)


def _out_proj_kernel(yd_ref, z_ref, u_ref, v_ref, x_ref, g1_ref, ng_ref, lng_ref, lnb_ref, ws_ref, bst_ref, w_ref,
                     o_ref, mix_cur, mix_nxt):
    tm = yd_ref.shape[0]
    s = pl.program_id(0)
    n = pl.num_programs(0) - 1

    def build_steps():
        steps = []
        for k in range(tm // CHUNK):
            r = pl.ds(k * CHUNK, CHUNK)
            shared = {}

            def ssd_part(r=r):
                a = yd_ref[r, :].astype(F32) * _silu(z_ref[r, :].astype(F32))
                ms = jnp.mean(a * a, axis=-1, keepdims=True)
                mix_nxt[r, 0:SSD_WIDTH] = (a * lax.rsqrt(ms + NORM_EPS) * ng_ref[...]).astype(BF16)

            def ln_part(r=r, shared=shared):
                gv = _gelu_tanh(v_ref[r, :].astype(F32))
                mu = jnp.mean(gv, axis=-1, keepdims=True)
                xc = gv - mu
                var = jnp.mean(xc * xc, axis=-1, keepdims=True)
                shared["ln"] = ((xc * lax.rsqrt(var + NORM_EPS)) * lng_ref[...] + lnb_ref[...]).astype(BF16)

            def gate_part(h0, h1, r=r, shared=shared):
                for h in range(h0, h1):
                    c0, c1 = h * CM_HEAD_DIM, (h + 1) * CM_HEAD_DIM
                    sp = _dot(ws_ref[h], shared["ln"][:, c0:c1]) + bst_ref[:, h:h + 1]
                    gu = _gelu_tanh(u_ref[r, c0:c1].astype(F32))
                    mix_nxt[r, SSD_WIDTH + c0:SSD_WIDTH + c1] = (gu * sp).astype(BF16)

            steps += [ssd_part, ln_part, functools.partial(gate_part, 0, CM_HEADS // 2),
                      functools.partial(gate_part, CM_HEADS // 2, CM_HEADS)]
        return steps

    n_proj = 8
    pw = D_MODEL // n_proj

    def project_steps():
        def piece(q):
            c = slice(q * pw, (q + 1) * pw)
            o_ref[:, c] = x_ref[:, c] + g1_ref[:, c] * _dot(mix_cur[...], w_ref[:, c])

        return [functools.partial(piece, q) for q in range(n_proj)]

    @pl.when(s == 0)
    def _():
        for step in build_steps():
            step()

    @pl.when((s > 0) & (s < n))
    def _():
        for proj, build in zip(project_steps(), build_steps()):
            proj()
            build()

    @pl.when(s == n)
    def _():
        for step in project_steps():
            step()

    @pl.when(s < n)
    def _():
        mix_cur[...] = mix_nxt[...]


def _out_proj(yd, pm, x2, mod3, ssd_norm_g, ln_g, ln_b, ws, bst, w_out, *, seq, tm):
    m = x2.shape[0]
    n = m // tm
    tiles_per_batch = seq // tm
    z_blk, u_blk, v_blk = 0, 2 * SSD_WIDTH // CM_WIDTH, 2 * SSD_WIDTH // CM_WIDTH + 1
    g1_blk = 2
    row = lambda shape: pl.BlockSpec(shape, lambda s: (0, 0))
    build_tile = lambda s: jnp.minimum(s, n - 1)
    proj_tile = lambda s: jnp.maximum(s - 1, 0)
    return pl.pallas_call(
        _out_proj_kernel,
        grid=(n + 1,),
        in_specs=[
            pl.BlockSpec((tm, SSD_WIDTH), lambda s: (build_tile(s), 0)),
            pl.BlockSpec((tm, SSD_WIDTH), lambda s: (build_tile(s), z_blk)),
            pl.BlockSpec((tm, CM_WIDTH), lambda s: (build_tile(s), u_blk)),
            pl.BlockSpec((tm, CM_WIDTH), lambda s: (build_tile(s), v_blk)),
            pl.BlockSpec((tm, D_MODEL), lambda s: (proj_tile(s), 0)),
            pl.BlockSpec((None, 1, D_MODEL), lambda s: (proj_tile(s) // tiles_per_batch, 0, g1_blk)),
            row((1, SSD_WIDTH)),
            row((1, CM_WIDTH)),
            row((1, CM_WIDTH)),
            pl.BlockSpec((CM_HEADS, CHUNK, CHUNK), lambda s: (0, 0, 0)),
            row((CHUNK, CM_HEADS)),
            pl.BlockSpec((SSD_WIDTH + CM_WIDTH, D_MODEL), lambda s: (0, 0), pipeline_mode=pl.Buffered(1)),
        ],
        out_specs=pl.BlockSpec((tm, D_MODEL), lambda s: (proj_tile(s), 0)),
        out_shape=jax.ShapeDtypeStruct((m, D_MODEL), F32),
        scratch_shapes=[pltpu.VMEM((tm, SSD_WIDTH + CM_WIDTH), BF16)] * 2,
        compiler_params=_params("arbitrary"),
        name="out_proj",
    )(yd, pm, pm, pm, x2, mod3, ssd_norm_g, ln_g, ln_b, ws, bst, w_out)


def _route(logits):
    lane = lax.broadcasted_iota(jnp.int32, logits.shape, 1)
    lane_f = lane.astype(F32)
    is_group = (lane >= N_EXPERTS) & (lane < N_EXPERTS + N_GROUPS)
    lg = jnp.where(is_group, logits, NEG_BIG)
    mg = jnp.max(lg, axis=1, keepdims=True)
    top_pg = 1.0 / jnp.sum(jnp.exp(lg - mg), axis=1, keepdims=True)
    gi = jnp.min(jnp.where(lg == mg, lane_f, 1e9), axis=1, keepdims=True) - N_EXPERTS
    in_group = (lane < N_EXPERTS) & ((lane // EXPERTS_PER_GROUP).astype(F32) == gi)
    le = jnp.where(in_group, logits, NEG_BIG)
    m1 = jnp.max(le, axis=1, keepdims=True)
    i1 = jnp.min(jnp.where(le == m1, lane_f, 1e9), axis=1, keepdims=True)
    le2 = jnp.where(lane_f == i1, NEG_BIG, le)
    m2 = jnp.max(le2, axis=1, keepdims=True)
    i2 = jnp.min(jnp.where(le2 == m2, lane_f, 1e9), axis=1, keepdims=True)
    e2 = jnp.exp(m2 - m1)
    p1 = 1.0 / (1.0 + e2)
    p2 = e2 * p1
    return i1, i2, p1 * top_pg, p2 * top_pg


RINFO_E1, RINFO_E2, RINFO_W1, RINFO_W2, RINFO_R1, RINFO_R2 = range(6)


def _route_kernel(x1_ref, n2g_ref, sh2_ref, sc2_ref, wr_ref, br_ref, h2_ref, rinfo_ref, counts_ref, cnt_scr):
    tm = x1_ref.shape[0]

    @pl.when(pl.program_id(0) == 0)
    def _():
        cnt_scr[...] = jnp.zeros_like(cnt_scr)

    x1 = x1_ref[...]
    ms = jnp.mean(x1 * x1, axis=-1, keepdims=True)
    h2 = (x1 * lax.rsqrt(ms + NORM_EPS) * n2g_ref[...]) * (1.0 + sc2_ref[...]) + sh2_ref[...]
    h2_ref[...] = h2
    hi, mid, _ = _split3(h2)
    p = _dot(hi, wr_ref[...])
    logits = p[:, :LANES] + p[:, LANES:] + _dot(mid, wr_ref[:, :LANES]) + br_ref[...]
    i1, i2, w1, w2 = _route(logits)

    lane = lax.broadcasted_iota(jnp.int32, (tm, LANES), 1)
    lane_f = lane.astype(F32)
    oh1 = jnp.where(lane_f == i1, 1.0, 0.0)
    oh2 = jnp.where(lane_f == i2, 1.0, 0.0)
    before = (lax.broadcasted_iota(jnp.int32, (tm, tm), 0) > lax.broadcasted_iota(jnp.int32, (tm, tm), 1)).astype(BF16)
    carried = cnt_scr[...]
    tot1 = jnp.sum(oh1, axis=0, keepdims=True)
    r1 = jnp.sum(oh1 * (_dot(before, oh1.astype(BF16)) + carried), axis=1, keepdims=True)
    r2 = jnp.sum(oh2 * (_dot(before, oh2.astype(BF16)) + (carried + tot1)), axis=1, keepdims=True)
    counts = carried + tot1 + jnp.sum(oh2, axis=0, keepdims=True)
    cnt_scr[...] = counts
    counts_ref[...] = jnp.broadcast_to(counts, counts_ref.shape)

    info = jnp.zeros((tm, LANES), F32)
    for k, v in ((RINFO_E1, i1), (RINFO_E2, i2), (RINFO_W1, w1), (RINFO_W2, w2), (RINFO_R1, r1), (RINFO_R2, r2)):
        info = jnp.where(lane == k, v, info)
    rinfo_ref[...] = info


def _route_call(x1, mod3, norm2_g, w_router, b_router, *, seq, tm):
    m = x1.shape[0]
    tiles_per_batch = seq // tm
    row = lambda shape: pl.BlockSpec(shape, lambda i: (0, 0))
    modrow = lambda k: pl.BlockSpec((None, 1, D_MODEL), lambda i: (i // tiles_per_batch, 0, k))
    return pl.pallas_call(
        _route_kernel,
        grid=(m // tm,),
        in_specs=[
            pl.BlockSpec((tm, D_MODEL), lambda i: (i, 0)),
            row((1, D_MODEL)),
            modrow(3),
            modrow(4),
            row((D_MODEL, 2 * LANES)),
            row((1, LANES)),
        ],
        out_specs=[
            pl.BlockSpec((tm, D_MODEL), lambda i: (i, 0)),
            pl.BlockSpec((tm, LANES), lambda i: (i, 0)),
            row((8, LANES)),
        ],
        out_shape=[
            jax.ShapeDtypeStruct((m, D_MODEL), F32),
            jax.ShapeDtypeStruct((m, LANES), F32),
            jax.ShapeDtypeStruct((8, LANES), F32),
        ],
        scratch_shapes=[pltpu.VMEM((1, LANES), F32)],
        compiler_params=_params("arbitrary"),
        name="route",
    )(x1, norm2_g, mod3, mod3, w_router, b_router)


def _dispatch_kernel(lt_ref, nu_ref, pos_ref, h2_ref, x_hbm, zbuf, zsem, sem, *, tile_rows, n_tiles):
    tm = h2_ref.shape[0]
    n_used = nu_ref[0]

    def zero_copy(t):
        return pltpu.make_async_copy(zbuf, x_hbm.at[pl.ds(pl.multiple_of(t * tile_rows, tile_rows), tile_rows), :], zsem)

    @pl.when(pl.program_id(0) == 0)
    def _():
        zbuf[...] = jnp.zeros_like(zbuf)
        for phase in ("start", "wait"):
            for e in range(N_EXPERTS):
                @pl.when(lt_ref[e] >= 0)
                def _():
                    getattr(zero_copy(lt_ref[e]), phase)()

            def unused(t, carry):
                getattr(zero_copy(t), phase)()
                return carry

            lax.fori_loop(n_used, n_tiles, unused, 0)

    def body(r, carry):
        for k in range(2):
            pltpu.make_async_copy(h2_ref.at[pl.ds(r, 1), :], x_hbm.at[pl.ds(pos_ref[0, 2 * r + k], 1), :],
                                  sem).start(priority=k)
        return carry

    lax.fori_loop(0, tm, body, 0, unroll=8)
    for k in range(2):
        pltpu.make_async_copy(h2_ref, x_hbm.at[pl.ds(0, tm), :], sem).wait()


def _dispatch_call(last_tile, n_used, pos, h2, *, tm, tile_rows, n_tiles):
    m = h2.shape[0]
    grid_spec = pltpu.PrefetchScalarGridSpec(
        num_scalar_prefetch=2,
        grid=(m // tm,),
        in_specs=[
            pl.BlockSpec((None, 1, 2 * tm), lambda i, lt, nu: (i, 0, 0), memory_space=pltpu.SMEM),
            pl.BlockSpec((tm, D_MODEL), lambda i, lt, nu: (i, 0)),
        ],
        out_specs=pl.BlockSpec(memory_space=pl.ANY),
        scratch_shapes=[
            pltpu.VMEM((tile_rows, D_MODEL), F32),
            pltpu.SemaphoreType.DMA(()),
            pltpu.SemaphoreType.DMA(()),
        ],
    )
    return pl.pallas_call(
        functools.partial(_dispatch_kernel, tile_rows=tile_rows, n_tiles=n_tiles),
        grid_spec=grid_spec,
        out_shape=jax.ShapeDtypeStruct((n_tiles * tile_rows, D_MODEL), F32),
        compiler_params=_params("arbitrary"),
        name="dispatch",
    )(last_tile, n_used, pos, h2)


def _experts_kernel(te_ref, nu_ref, ord_ref, nxt_ref, x_ref, wg_hbm, wu_hbm, wd_hbm, y_ref,
                    wg_f, wu_f, wd_f, wg_b, wu_b, wd_b, sem):
    j = pl.program_id(0)
    n_used = nu_ref[0]

    def weight_copies(e, slot):
        return [pltpu.make_async_copy(src.at[e], dst.at[slot], sem.at[slot])
                for src, dst in ((wg_hbm, wg_f), (wu_hbm, wu_f), (wd_hbm, wd_f))]

    @pl.when(j == 0)
    def _():
        for cp in weight_copies(te_ref[0], 0):
            cp.start()

    @pl.when((j < n_used) & ((j == 0) | (te_ref[j] != te_ref[jnp.maximum(j - 1, 0)])))
    def _():
        slot = ord_ref[j] % 2
        for cp in weight_copies(te_ref[j], slot):
            cp.wait()

        @pl.when(nxt_ref[j] >= 0)
        def _():
            for cp in weight_copies(nxt_ref[j], 1 - slot):
                cp.start()

        wg_b[...] = wg_f[slot].astype(BF16)
        wu_b[...] = wu_f[slot].astype(BF16)
        wd_b[...] = wd_f[slot].astype(BF16)

    @pl.when(j < n_used)
    def _():
        xt = x_ref[...].astype(BF16)
        hid = (_silu(_dot(xt, wg_b[...])) * _dot(xt, wu_b[...])).astype(BF16)
        y_ref[...] = _dot(hid, wd_b[...])

    @pl.when(j >= n_used)
    def _():
        y_ref[...] = jnp.zeros_like(y_ref)


def _experts_call(tile_expert, n_used, tile_ord, next_expert, x_sorted, wg, wu, wd, *, tm):
    n_tiles = tile_expert.shape[0]
    hbm = pl.BlockSpec(memory_space=pl.ANY)
    grid_spec = pltpu.PrefetchScalarGridSpec(
        num_scalar_prefetch=4,
        grid=(n_tiles,),
        in_specs=[
            pl.BlockSpec((tm, D_MODEL), lambda j, te, nu, od, nx: (jnp.minimum(j, nu[0] - 1), 0)),
            hbm,
            hbm,
            hbm,
        ],
        out_specs=pl.BlockSpec((tm, D_MODEL), lambda j, te, nu, od, nx: (j, 0)),
        scratch_shapes=[
            pltpu.VMEM((2, D_MODEL, EXPERT_FF), F32),
            pltpu.VMEM((2, D_MODEL, EXPERT_FF), F32),
            pltpu.VMEM((2, EXPERT_FF, D_MODEL), F32),
            pltpu.VMEM((D_MODEL, EXPERT_FF), BF16),
            pltpu.VMEM((D_MODEL, EXPERT_FF), BF16),
            pltpu.VMEM((EXPERT_FF, D_MODEL), BF16),
            pltpu.SemaphoreType.DMA((2,)),
        ],
    )
    return pl.pallas_call(
        _experts_kernel,
        grid_spec=grid_spec,
        out_shape=jax.ShapeDtypeStruct((n_tiles * tm, D_MODEL), F32),
        compiler_params=_params("arbitrary"),
        name="experts",
    )(tile_expert, n_used, tile_ord, next_expert, x_sorted, wg, wu, wd)


def _combine_kernel(pos_cur, pos_nxt, x1_ref, g2_ref, nfg_ref, rinfo_ref, y_hbm, o_ref, ybuf, sem):
    i = pl.program_id(0)
    tm = x1_ref.shape[0]
    slot = i % 2

    def row_copy(pos_ref, r, k, s):
        return pltpu.make_async_copy(y_hbm.at[pl.ds(pos_ref[0, 2 * r + k], 1), :], ybuf.at[s, k, pl.ds(r, 1), :],
                                     sem.at[s])

    def wait_tile(s):
        for k in range(2):
            pltpu.make_async_copy(y_hbm.at[pl.ds(0, tm), :], ybuf.at[s, k], sem.at[s]).wait()

    @pl.when(i == 0)
    def _():
        def body(r, carry):
            for k in range(2):
                row_copy(pos_cur, r, k, 0).start()
            return carry

        lax.fori_loop(0, tm, body, 0)

    wait_tile(slot)
    for r in range(tm):
        for k in range(2):
            row_copy(pos_nxt, r, k, 1 - slot).start(priority=k)
    info = rinfo_ref[...]
    w1 = info[:, RINFO_W1:RINFO_W1 + 1]
    w2 = info[:, RINFO_W2:RINFO_W2 + 1]
    moe = w1 * ybuf[slot, 0] + w2 * ybuf[slot, 1]
    y = x1_ref[...] + g2_ref[...] * moe
    ms = jnp.mean(y * y, axis=-1, keepdims=True)
    o_ref[...] = y * lax.rsqrt(ms + NORM_EPS) * nfg_ref[...]

    @pl.when(i == pl.num_programs(0) - 1)
    def _():
        wait_tile(1 - slot)


def _combine_call(pos, x1, mod3, normf_g, rinfo, y_sorted, *, seq, tm):
    m = x1.shape[0]
    n_tiles = m // tm
    tiles_per_batch = seq // tm
    return pl.pallas_call(
        _combine_kernel,
        grid=(n_tiles,),
        in_specs=[
            pl.BlockSpec((None, 1, 2 * tm), lambda i: (i, 0, 0), memory_space=pltpu.SMEM),
            pl.BlockSpec((None, 1, 2 * tm), lambda i: (jnp.minimum(i + 1, n_tiles - 1), 0, 0),
                         memory_space=pltpu.SMEM),
            pl.BlockSpec((tm, D_MODEL), lambda i: (i, 0)),
            pl.BlockSpec((None, 1, D_MODEL), lambda i: (i // tiles_per_batch, 0, 5)),
            pl.BlockSpec((1, D_MODEL), lambda i: (0, 0)),
            pl.BlockSpec((tm, LANES), lambda i: (i, 0)),
            pl.BlockSpec(memory_space=pl.ANY),
        ],
        out_specs=pl.BlockSpec((tm, D_MODEL), lambda i: (i, 0)),
        out_shape=jax.ShapeDtypeStruct((m, D_MODEL), F32),
        scratch_shapes=[
            pltpu.VMEM((2, 2, tm, D_MODEL), F32),
            pltpu.SemaphoreType.DMA((2,)),
        ],
        compiler_params=_params("arbitrary"),
        name="combine",
    )(pos, pos, x1, mod3, normf_g, rinfo, y_sorted)


def _dispatch_plan(rinfo, counts, *, tm):
    n_tok = rinfo.shape[0]
    n_tiles = (2 * n_tok + N_EXPERTS * (tm - 1)) // tm + 1
    expert = rinfo[:, RINFO_E1:RINFO_E2 + 1].astype(jnp.int32)
    rank = rinfo[:, RINFO_R1:RINFO_R2 + 1].astype(jnp.int32)
    cnt = counts[0, :N_EXPERTS].astype(jnp.int32)
    tiles_e = (cnt + tm - 1) // tm
    end_tile = jnp.cumsum(tiles_e)
    start_row = (end_tile - tiles_e) * tm
    is_e = expert[:, :, None] == jnp.arange(N_EXPERTS, dtype=jnp.int32)
    pos = jnp.sum(jnp.where(is_e, start_row, 0), axis=-1) + rank
    n_used = end_tile[-1:]
    tile_ids = jnp.minimum(jnp.arange(n_tiles, dtype=jnp.int32), n_used - 1)
    tile_expert = jnp.sum(tile_ids[:, None] >= end_tile[None, :], axis=1).astype(jnp.int32)
    last_tile = jnp.where(tiles_e > 0, end_tile - 1, -1).astype(jnp.int32)
    used = (tiles_e > 0).astype(jnp.int32)
    ord_e = jnp.cumsum(used) - used
    ids = jnp.arange(N_EXPERTS, dtype=jnp.int32)
    later = jnp.where((ids[None, :] > ids[:, None]) & (used[None, :] > 0), ids[None, :], N_EXPERTS)
    nxt_e = jnp.min(later, axis=1)
    nxt_e = jnp.where(nxt_e < N_EXPERTS, nxt_e, -1).astype(jnp.int32)
    is_te = tile_expert[:, None] == ids[None, :]
    tile_ord = jnp.sum(jnp.where(is_te, ord_e, 0), axis=1).astype(jnp.int32)
    next_expert = jnp.sum(jnp.where(is_te, nxt_e, 0), axis=1).astype(jnp.int32)
    return pos, (tile_expert, n_used.astype(jnp.int32), tile_ord, next_expert), last_tile, n_tiles


def kernel(x, c, ctx, c_ctx, w_mod, b_mod, norm1_g, w_in, conv_w, conv_b, dt_bias_f, dt_bias_b, a_log_f, a_log_b,
           d_skip, ssd_norm_g, cm_ln_g, cm_ln_b, w_spatial, b_spatial, w_out, norm2_g, w_router_group,
           b_router_group, w_router_expert, b_router_expert, w_exp_gate, w_exp_up, w_exp_down, normf_g):
    bsz, seq, _ = x.shape
    ctx_len = ctx.shape[1]
    i = 0

    cc = jnp.concatenate([c, c_ctx[None, :], jnp.zeros((MOD_ROWS - bsz - 1, D_MODEL), F32)], axis=0)
    mod = _modulation(cc, w_mod[i], b_mod[i][None, :])
    mod3 = mod.reshape(MOD_ROWS, 1, N_MOD * D_MODEL)

    w_main, w_dt = _w_in_prep(jnp.swapaxes(w_in[i], 0, 1))
    g1row = norm1_g[i][None, :]

    x2 = x.reshape(bsz * seq, D_MODEL)
    ctx2 = ctx.reshape(bsz * ctx_len, D_MODEL)
    tm_in = 1024
    tn_in = 1024
    xs_col0, bc_col0 = SSD_WIDTH, 2 * SSD_WIDTH + 2 * CM_WIDTH
    pm_x, dt_x = _in_proj(x2, mod3, lambda t: t // (seq // tm_in), g1row, w_main, w_dt, tm_in, tn_in)
    ctx_blocks = tuple(range(xs_col0 // tn_in, 2 * SSD_WIDTH // tn_in)) + (bc_col0 // tn_in,)
    pm_c, dt_c = _in_proj(ctx2, mod3, lambda t: bsz, g1row, w_main, w_dt, bsz * ctx_len, tn_in, ctx_blocks)

    lane_pad = (0, LANES - 2 * SSD_HEADS)
    dt_bias_row = jnp.pad(jnp.concatenate([dt_bias_f[i], dt_bias_b[i]]).astype(F32), lane_pad)[None, :]
    a_row = jnp.pad(-jnp.exp(jnp.concatenate([a_log_f[i], a_log_b[i]]).astype(F32)), lane_pad)[None, :]
    dskip_g = jnp.repeat(d_skip[i], SSD_HEAD_DIM).reshape(SSD_GROUPS, 1, GROUP_WIDTH)
    cw = conv_w[i]
    cb = conv_b[i][None, :]
    h_zero = jnp.zeros((bsz, SSD_GROUPS, SSD_STATE, GROUP_WIDTH), F32)

    dt_terms_c = _ssd_dt(dt_c, dt_bias_row, a_row, rows_per_step=bsz * ctx_len)
    dt_terms_x = _ssd_dt(dt_x, dt_bias_row, a_row, rows_per_step=1024)
    hc_f, hc_b = _ssd(pm_c, 0, SSD_WIDTH, dt_terms_c, cw, cb, dskip_g, h_zero, h_zero,
                      batch=bsz, seq=ctx_len, conv_tile=ctx_len, conv_period=ctx_len, with_output=False)
    yd = _ssd(pm_x, xs_col0, bc_col0, dt_terms_x, cw, cb, dskip_g, hc_f, hc_b,
              batch=bsz, seq=seq, conv_tile=CHUNK, conv_period=GRID_W, with_output=True)

    x1 = _out_proj(yd, pm_x, x2, mod3, ssd_norm_g[i][None, :], cm_ln_g[i][None, :], cm_ln_b[i][None, :],
                   w_spatial[i].astype(BF16), b_spatial[i].T, w_out[i].astype(BF16), seq=seq, tm=256)

    w_re = jnp.transpose(w_router_expert[i], (1, 0, 2)).reshape(D_MODEL, N_EXPERTS)
    pad = LANES - N_EXPERTS - N_GROUPS
    w_router = jnp.pad(jnp.concatenate([w_re, w_router_group[i]], axis=1), ((0, 0), (0, pad)))
    w_router_hi = w_router.astype(BF16)
    w_router = jnp.concatenate([w_router_hi, (w_router - w_router_hi.astype(F32)).astype(BF16)], axis=1)
    b_router = jnp.pad(jnp.concatenate([b_router_expert[i].reshape(-1), b_router_group[i]]), (0, pad))[None, :]
    h2, rinfo, counts = _route_call(x1, mod3, norm2_g[i][None, :], w_router, b_router, seq=seq, tm=512)

    tm_e = 256
    pos, tile_tables, last_tile, n_tiles = _dispatch_plan(rinfo, counts, tm=tm_e)
    tm_d = 512
    x_sorted = _dispatch_call(last_tile, tile_tables[1], pos.reshape(bsz * seq // tm_d, 1, 2 * tm_d), h2,
                              tm=tm_d, tile_rows=tm_e, n_tiles=n_tiles)
    wg = w_exp_gate[i].reshape(N_EXPERTS, D_MODEL, EXPERT_FF)
    wu = w_exp_up[i].reshape(N_EXPERTS, D_MODEL, EXPERT_FF)
    wd = w_exp_down[i].reshape(N_EXPERTS, EXPERT_FF, D_MODEL)
    y_sorted = _experts_call(*tile_tables, x_sorted, wg, wu, wd, tm=tm_e)
    tm_c = 256
    out = _combine_call(pos.reshape(bsz * seq // tm_c, 1, 2 * tm_c), x1, mod3, normf_g[None, :], rinfo, y_sorted,
                        seq=seq, tm=tm_c)
    return out.reshape(bsz, seq, D_MODEL)
```

```python
import functools

import jax
import jax.numpy as jnp
from jax import lax
from jax.experimental import pallas as pl
from jax.experimental.pallas import tpu as pltpu

F32 = jnp.float32
BF16 = jnp.bfloat16
HIGHEST = lax.Precision.HIGHEST

D_MODEL = 2048
GRID_W = 64
SSD_WIDTH = 2048
CM_WIDTH = 2048
SSD_HEADS = 32
SSD_HEAD_DIM = 64
SSD_GROUPS = 4
HEADS_PER_GROUP = SSD_HEADS // SSD_GROUPS
GROUP_WIDTH = HEADS_PER_GROUP * SSD_HEAD_DIM
SSD_STATE = 128
CHUNK = 128
BC_WIDTH = SSD_GROUPS * SSD_STATE
CM_HEADS = 8
CM_HEAD_DIM = CM_WIDTH // CM_HEADS
N_GROUPS = 4
N_EXPERTS = 32
EXPERTS_PER_GROUP = 8
EXPERT_FF = 512
N_MOD = 6
NORM_EPS = 1e-6
LANES = 128
NEG_BIG = -1e30
MOD_ROWS = 8
VMEM_LIMIT = 56 * 1024 * 1024


def _params(*sem):
    return pltpu.CompilerParams(dimension_semantics=sem, vmem_limit_bytes=VMEM_LIMIT)


LOG2_E = 1.4426950408889634
GELU_C = 0.7978845608028654
GELU_K1 = -2.0 * GELU_C * LOG2_E
GELU_K3 = GELU_K1 * 0.044715


def _silu(v):
    return v / (1.0 + jnp.exp2(v * (-LOG2_E)))


def _gelu_tanh(v):
    return v / (1.0 + jnp.exp2(v * (GELU_K1 + GELU_K3 * (v * v))))


def _dot(a, b):
    return jnp.dot(a, b, preferred_element_type=F32)


def _dot_nt(a, b):
    return lax.dot_general(a, b, (((1,), (1,)), ((), ())), preferred_element_type=F32)


def _mod_kernel(cc_ref, w_ref, b_ref, o_ref):
    a = _silu(cc_ref[...])
    hi = a.astype(BF16).astype(F32)
    lhs = jnp.concatenate([hi, a - hi], axis=0).astype(BF16)
    r = _dot(lhs, w_ref[...].astype(BF16))
    o_ref[...] = r[:MOD_ROWS] + r[MOD_ROWS:] + b_ref[...]


def _modulation(cc, w_mod, b_mod):
    n = w_mod.shape[1]
    tn = 1024
    return pl.pallas_call(
        _mod_kernel,
        grid=(n // tn,),
        in_specs=[
            pl.BlockSpec((MOD_ROWS, D_MODEL), lambda j: (0, 0)),
            pl.BlockSpec((D_MODEL, tn), lambda j: (0, j)),
            pl.BlockSpec((1, tn), lambda j: (0, j)),
        ],
        out_specs=pl.BlockSpec((MOD_ROWS, tn), lambda j: (0, j)),
        out_shape=jax.ShapeDtypeStruct((MOD_ROWS, n), F32),
        compiler_params=_params("arbitrary"),
        name="modulation",
    )(cc, w_mod, b_mod)


PREP_ROWS = 512
PREP_UV_BLK0 = 2 * SSD_WIDTH // PREP_ROWS
PREP_BC_BLK0 = (2 * SSD_WIDTH + 2 * CM_WIDTH) // PREP_ROWS
PREP_N_BLKS = PREP_BC_BLK0 + 2 * BC_WIDTH // PREP_ROWS
DT_COL0 = 2 * SSD_WIDTH + 2 * BC_WIDTH
UV_COL0 = DT_COL0 + 2 * SSD_HEADS


def _w_in_prep_kernel(a_ref, dt_ref, o_ref, wdt_ref):
    @pl.when(pl.program_id(0) == 0)
    def _():
        wdt_ref[...] = dt_ref[...].astype(BF16)

    o_ref[...] = a_ref[...].astype(BF16)


def _w_in_prep(w_in_t):
    def src_row(j):
        uv = UV_COL0 + (j - PREP_UV_BLK0) * PREP_ROWS
        bc = 2 * SSD_WIDTH + (j - PREP_BC_BLK0) * PREP_ROWS
        return pl.multiple_of(jnp.where(j < PREP_UV_BLK0, j * PREP_ROWS, jnp.where(j < PREP_BC_BLK0, uv, bc)), 8)

    return pl.pallas_call(
        _w_in_prep_kernel,
        grid=(PREP_N_BLKS,),
        in_specs=[
            pl.BlockSpec((pl.Element(PREP_ROWS), pl.Element(D_MODEL)), lambda j: (src_row(j), 0)),
            pl.BlockSpec((LANES, D_MODEL), lambda j: (DT_COL0 // LANES, 0)),
        ],
        out_specs=[
            pl.BlockSpec((PREP_ROWS, D_MODEL), lambda j: (j, 0)),
            pl.BlockSpec((LANES, D_MODEL), lambda j: (0, 0)),
        ],
        out_shape=[
            jax.ShapeDtypeStruct((PREP_N_BLKS * PREP_ROWS, D_MODEL), BF16),
            jax.ShapeDtypeStruct((LANES, D_MODEL), BF16),
        ],
        compiler_params=_params("arbitrary"),
        name="w_in_prep",
    )(w_in_t, w_in_t)


ACT_NONE, ACT_SILU, ACT_GELU = 0, 1, 2


def _in_proj_kernel(x_ref, g_ref, sh_ref, sc_ref, w_ref, wdt_ref, o_ref, dt_ref, h_scr, *, block_acts):
    tm = x_ref.shape[0]
    rc = 256

    @pl.when(pl.program_id(1) == 0)
    def _():
        gain = g_ref[...]
        scale = 1.0 + sc_ref[...]
        shift = sh_ref[...]

        def body(k, carry):
            r = pl.ds(pl.multiple_of(k * rc, rc), rc)
            xv = x_ref[r, :]
            ms = jnp.mean(xv * xv, axis=-1, keepdims=True)
            h = (xv * lax.rsqrt(ms + NORM_EPS) * gain) * scale + shift
            h_scr[r, :] = h.astype(BF16)
            return carry

        lax.fori_loop(0, tm // rc, body, 0)
        dt_ref[...] = _dot_nt(h_scr[...], wdt_ref[...])

    j = pl.program_id(1)
    for act, fn in ((ACT_NONE, lambda v: v), (ACT_SILU, _silu), (ACT_GELU, _gelu_tanh)):
        blocks = [q for q, a in enumerate(block_acts) if a == act]
        if blocks:
            is_act = functools.reduce(jnp.logical_or, [j == q for q in blocks])

            @pl.when(is_act)
            def _():
                o_ref[...] = fn(_dot_nt(h_scr[...], w_ref[...])).astype(o_ref.dtype)


def _in_proj(x2, mod3, mod_row_of_tile, norm_g, w_main, w_dt, tm, tn, w_blocks=None):
    m = x2.shape[0]
    if w_blocks is None:
        w_blocks = tuple(range(w_main.shape[0] // tn))
    n = len(w_blocks) * tn
    z_end, uv0, uv_end = SSD_WIDTH // tn, 2 * SSD_WIDTH // tn, (2 * SSD_WIDTH + 2 * CM_WIDTH) // tn
    block_acts = tuple(ACT_SILU if b < z_end else ACT_GELU if uv0 <= b < uv_end else ACT_NONE for b in w_blocks)

    def w_blk(j):
        if w_blocks == tuple(range(len(w_blocks))):
            return j
        blk = jnp.int32(w_blocks[-1])
        for q in range(len(w_blocks) - 2, -1, -1):
            blk = jnp.where(j == q, w_blocks[q], blk)
        return blk

    return pl.pallas_call(
        functools.partial(_in_proj_kernel, block_acts=block_acts),
        grid=(m // tm, n // tn),
        in_specs=[
            pl.BlockSpec((tm, D_MODEL), lambda i, j: (i, 0)),
            pl.BlockSpec((1, D_MODEL), lambda i, j: (0, 0)),
            pl.BlockSpec((None, 1, D_MODEL), lambda i, j: (mod_row_of_tile(i), 0, 0)),
            pl.BlockSpec((None, 1, D_MODEL), lambda i, j: (mod_row_of_tile(i), 0, 1)),
            pl.BlockSpec((tn, D_MODEL), lambda i, j: (w_blk(j), 0)),
            pl.BlockSpec((LANES, D_MODEL), lambda i, j: (0, 0)),
        ],
        out_specs=[
            pl.BlockSpec((tm, tn), lambda i, j: (i, j)),
            pl.BlockSpec((tm, LANES), lambda i, j: (i, 0)),
        ],
        out_shape=[
            jax.ShapeDtypeStruct((m, n), BF16),
            jax.ShapeDtypeStruct((m, LANES), F32),
        ],
        scratch_shapes=[pltpu.VMEM((tm, D_MODEL), BF16)],
        compiler_params=_params("parallel", "arbitrary"),
        name="in_proj",
    )(x2, norm_g, mod3, mod3, w_main, w_dt)


def _split3(v):
    hi = v.astype(BF16)
    rem = v - hi.astype(F32)
    mid = rem.astype(BF16)
    lo = (rem - mid.astype(F32)).astype(BF16)
    return hi, mid, lo


GROUP_LANES = 2 * HEADS_PER_GROUP


def _stack_split3(v):
    hi, mid, lo = _split3(v)
    stacked = (hi.astype(F32) + pltpu.roll(mid.astype(F32), GROUP_LANES, 1)
               + pltpu.roll(lo.astype(F32), 2 * GROUP_LANES, 1))
    return stacked.astype(BF16)


def _ssd_dt_kernel(dt_ref, bias_ref, a_ref, dts_ref, cum_ref, src_t_ref):
    hpg = HEADS_PER_GROUP
    ii = lax.broadcasted_iota(jnp.int32, (CHUNK, CHUNK), 0)
    jj = lax.broadcasted_iota(jnp.int32, (CHUNK, CHUNK), 1)
    tri = (ii >= jj).astype(BF16)
    tri3 = jnp.concatenate([tri, tri, tri], axis=1)
    bias = bias_ref[...]
    a_row = a_ref[...]

    def body(k, carry):
        r = pl.ds(pl.multiple_of(k * CHUNK, CHUNK), CHUNK)
        raw = dt_ref[r, :] + bias
        dts = jnp.where(jj < 2 * SSD_HEADS, jnp.maximum(raw, 0.0) + jnp.log1p(jnp.exp(-jnp.abs(raw))), 0.0)
        adt = dts * a_row
        hi, mid, lo = _split3(adt)
        cf = _dot(tri3, jnp.concatenate([hi, mid, lo], axis=0))
        cr = cf[CHUNK - 1:CHUNK, :] - cf + adt
        valid = jj < 2 * SSD_HEADS
        cum = jnp.where(jj < SSD_HEADS, cf, cr)
        src_t = ((cum - jnp.log(jnp.where(valid, dts, 1.0))) * LOG2_E).T
        rt = pl.ds(pl.multiple_of(k * GROUP_LANES, GROUP_LANES), GROUP_LANES)
        for g in range(SSD_GROUPS):
            f0, b0 = g * hpg, SSD_HEADS + g * hpg
            for src, dst in ((cum * LOG2_E, cum_ref), (dts, dts_ref)):
                lanes = jnp.where(jj < hpg, pltpu.roll(src, (LANES - f0) % LANES, 1),
                                  pltpu.roll(src, (LANES - (b0 - hpg)) % LANES, 1))
                dst[g, r, :] = jnp.where(jj < GROUP_LANES, lanes, 0.0)
            src_t_ref[g, rt, :] = jnp.concatenate([src_t[f0:f0 + hpg, :], src_t[b0:b0 + hpg, :]], axis=0)
        return carry

    lax.fori_loop(0, dt_ref.shape[0] // CHUNK, body, 0)


def _ssd_dt(dt, bias_row, a_row, *, rows_per_step):
    m = dt.shape[0]
    row = pl.BlockSpec((1, LANES), lambda i: (0, 0))
    full = pl.BlockSpec((SSD_GROUPS, rows_per_step, LANES), lambda i: (0, i, 0))
    tr = pl.BlockSpec((SSD_GROUPS, rows_per_step // CHUNK * GROUP_LANES, LANES), lambda i: (0, i, 0))
    full_shape = jax.ShapeDtypeStruct((SSD_GROUPS, m, LANES), F32)
    tr_shape = jax.ShapeDtypeStruct((SSD_GROUPS, m // CHUNK * GROUP_LANES, LANES), F32)
    return pl.pallas_call(
        _ssd_dt_kernel,
        grid=(m // rows_per_step,),
        in_specs=[pl.BlockSpec((rows_per_step, LANES), lambda i: (i, 0)), row, row],
        out_specs=[full, full, tr],
        out_shape=[full_shape, full_shape, tr_shape],
        compiler_params=_params("parallel"),
        name="ssd_dt",
    )(dt, bias_row, a_row)


def _ssd_kernel(xs_ref, b_ref, c_ref, dts_ref, cum_ref, src_t_ref, cwx_ref, cbx_ref, cwb_ref, cbb_ref,
                cwc_ref, cbc_ref, dskip_ref, h0f_ref, h0b_ref, *rest,
                n_chunks, conv_tile, conv_period, with_output):
    if with_output:
        y_ref, xs_s, b_s, c_s, hf_s, hb_s, y_s, yb_s = rest
    else:
        hf_out, hb_out, xs_s, b_s, c_s, hf_s, hb_s = rest
    seq = n_chunks * CHUNK
    hpg = HEADS_PER_GROUP

    def conv_silu(src_ref, dst_ref, w_ref, bias_ref):
        cols = src_ref.shape[1]
        w1 = w_ref[1:2, :]
        bias = bias_ref[...]
        row = lax.broadcasted_iota(jnp.int32, (conv_tile, cols), 0) % conv_period
        w0 = jnp.where(row == 0, 0.0, w_ref[0:1, :])
        w2 = jnp.where(row == conv_period - 1, 0.0, w_ref[2:3, :])

        def body(k, carry):
            r = pl.ds(pl.multiple_of(k * conv_tile, conv_tile), conv_tile)
            v = src_ref[r, :].astype(F32)
            y = bias + pltpu.roll(v, 1, 0) * w0 + v * w1 + pltpu.roll(v, conv_tile - 1, 0) * w2
            dst_ref[r, :] = _silu(y).astype(BF16)
            return carry

        lax.fori_loop(0, seq // conv_tile, body, 0)

    conv_silu(xs_ref, xs_s, cwx_ref, cbx_ref)
    conv_silu(b_ref, b_s, cwb_ref, cbb_ref)
    conv_silu(c_ref, c_s, cwc_ref, cbc_ref)

    ii = lax.broadcasted_iota(jnp.int32, (CHUNK, CHUNK), 0)
    jj = lax.broadcasted_iota(jnp.int32, (CHUNK, CHUNK), 1)
    lower = ii >= jj
    upper = jj >= ii

    hf_s[...] = h0f_ref[...]
    hb_s[...] = h0b_ref[...]

    head_of_col = lax.broadcasted_iota(jnp.int32, (LANES, GROUP_WIDTH), 1) // SSD_HEAD_DIM
    sel_row = lax.broadcasted_iota(jnp.int32, (LANES, GROUP_WIDTH), 0)
    sel_valid = sel_row < 3 * GROUP_LANES
    sel_f = (sel_valid & (sel_row % GROUP_LANES == head_of_col)).astype(BF16)
    sel_b = (sel_valid & (sel_row % GROUP_LANES == head_of_col + hpg)).astype(BF16)
    low_half = jj < SSD_HEAD_DIM
    group_lane = jj < GROUP_LANES

    def direction_terms(r, sel):
        cum = cum_ref[r, :]
        dts = dts_ref[r, :]
        tot = jnp.where(jj[0:1, :] < hpg, cum[CHUNK - 1:CHUNK, :], cum[0:1, :])
        wst = dts * jnp.exp2(tot - cum)
        eoff = jnp.where(group_lane, jnp.exp2(cum), 0.0)
        edec = jnp.broadcast_to(jnp.where(group_lane[0:1, :], jnp.exp2(tot), 0.0), (8, LANES))
        ex = _dot(_stack_split3(jnp.concatenate([wst, eoff, edec], axis=0)), sel)
        return cum, ex[:CHUNK], ex[CHUNK:2 * CHUNK], ex[2 * CHUNK:2 * CHUNK + 1]

    def state_step(h_s, r, sel):
        cum, w_state, e_off, e_dec = direction_terms(r, sel)
        xc = xs_s[r, :]
        xw = (xc.astype(F32) * w_state).astype(BF16)
        s_new = lax.dot_general(b_s[r, :], xw, (((0,), (0,)), ((), ())), preferred_element_type=F32)
        h = h_s[...]
        y_off = _dot(c_s[r, :], h.astype(BF16)) * e_off
        h_s[...] = h * e_dec + s_new
        return cum, xc, y_off

    def fwd_body(k, carry):
        r = pl.ds(pl.multiple_of(k * CHUNK, CHUNK), CHUNK)
        cum, xc, y_off = state_step(hf_s, r, sel_f)
        if with_output:
            g = _dot_nt(c_s[r, :], b_s[r, :])
            rt = pl.ds(pl.multiple_of(k * GROUP_LANES, GROUP_LANES), GROUP_LANES)
            src_t = src_t_ref[rt, :]
            pieces = []
            for pair in range(hpg // 2):
                ms = []
                for h in (2 * pair, 2 * pair + 1):
                    hb = h + hpg
                    mf = jnp.exp2(jnp.where(lower, cum[:, h:h + 1] - src_t[h:h + 1, :], NEG_BIG))
                    mb = jnp.exp2(jnp.where(upper, cum[:, hb:hb + 1] - src_t[hb:hb + 1, :], NEG_BIG))
                    ms.append((g * (mf + mb)).astype(BF16))
                lhs = jnp.concatenate(ms, axis=1)
                xp = xc[:, pair * LANES:(pair + 1) * LANES]
                zero = jnp.zeros_like(xp)
                rhs = jnp.concatenate([jnp.where(low_half, xp, zero), jnp.where(low_half, zero, xp)], axis=0)
                pieces.append(_dot(lhs, rhs))
            y_diag = jnp.concatenate(pieces, axis=1)
            y_s[r, :] = y_diag + y_off + xc.astype(F32) * dskip_ref[...]
        rb = pl.ds(pl.multiple_of((n_chunks - 1 - k) * CHUNK, CHUNK), CHUNK)
        _, _, y_off_b = state_step(hb_s, rb, sel_b)
        if with_output:
            yb_s[rb, :] = y_off_b
        return carry

    lax.fori_loop(0, n_chunks, fwd_body, 0, unroll=4 if n_chunks % 4 == 0 else 2)

    if with_output:
        def sum_body(k, carry):
            r = pl.ds(pl.multiple_of(k * CHUNK, CHUNK), CHUNK)
            y_ref[r, :] = (y_s[r, :] + yb_s[r, :]).astype(y_ref.dtype)
            return carry

        lax.fori_loop(0, n_chunks, sum_body, 0)

    if not with_output:
        hf_out[...] = hf_s[...]
        hb_out[...] = hb_s[...]


def _ssd(pm, xs_col0, bc_col0, dt_terms, conv_w, conv_b, dskip_g, h0f, h0b, *, batch, seq, conv_tile, conv_period,
         with_output):
    n_chunks = seq // CHUNK
    dts_g, cum_g, src_t_g = dt_terms
    dt_spec = pl.BlockSpec((None, seq, LANES), lambda b, g: (g, b, 0))
    dt_t_spec = pl.BlockSpec((None, n_chunks * GROUP_LANES, LANES), lambda b, g: (g, b, 0))
    xs_blk0 = xs_col0 // GROUP_WIDTH
    b_blk0 = bc_col0 // SSD_STATE
    c_blk0 = b_blk0 + SSD_GROUPS
    cw_b0 = SSD_WIDTH // SSD_STATE
    cw_c0 = cw_b0 + SSD_GROUPS
    state_spec = pl.BlockSpec((None, None, SSD_STATE, GROUP_WIDTH), lambda b, g: (b, g, 0, 0))
    in_specs = [
        pl.BlockSpec((seq, GROUP_WIDTH), lambda b, g: (b, xs_blk0 + g)),
        pl.BlockSpec((seq, SSD_STATE), lambda b, g: (b, b_blk0 + g)),
        pl.BlockSpec((seq, SSD_STATE), lambda b, g: (b, c_blk0 + g)),
        dt_spec,
        dt_spec,
        dt_t_spec,
        pl.BlockSpec((3, GROUP_WIDTH), lambda b, g: (0, g)),
        pl.BlockSpec((1, GROUP_WIDTH), lambda b, g: (0, g)),
        pl.BlockSpec((3, SSD_STATE), lambda b, g: (0, cw_b0 + g)),
        pl.BlockSpec((1, SSD_STATE), lambda b, g: (0, cw_b0 + g)),
        pl.BlockSpec((3, SSD_STATE), lambda b, g: (0, cw_c0 + g)),
        pl.BlockSpec((1, SSD_STATE), lambda b, g: (0, cw_c0 + g)),
        pl.BlockSpec((None, 1, GROUP_WIDTH), lambda b, g: (g, 0, 0)),
        state_spec,
        state_spec,
    ]
    scratch = [
        pltpu.VMEM((seq, GROUP_WIDTH), BF16),
        pltpu.VMEM((seq, SSD_STATE), BF16),
        pltpu.VMEM((seq, SSD_STATE), BF16),
        pltpu.VMEM((SSD_STATE, GROUP_WIDTH), F32),
        pltpu.VMEM((SSD_STATE, GROUP_WIDTH), F32),
    ]
    if with_output:
        out_specs = pl.BlockSpec((seq, GROUP_WIDTH), lambda b, g: (b, g))
        out_shape = jax.ShapeDtypeStruct((batch * seq, SSD_WIDTH), BF16)
        scratch += [pltpu.VMEM((seq, GROUP_WIDTH), F32)] * 2
    else:
        out_specs = [state_spec, state_spec]
        out_shape = [jax.ShapeDtypeStruct((batch, SSD_GROUPS, SSD_STATE, GROUP_WIDTH), F32)] * 2
    return pl.pallas_call(
        functools.partial(_ssd_kernel, n_chunks=n_chunks, conv_tile=conv_tile, conv_period=conv_period,
                          with_output=with_output),
        grid=(batch, SSD_GROUPS),
        in_specs=in_specs,
        out_specs=out_specs,
        out_shape=out_shape,
        scratch_shapes=scratch,
        compiler_params=_params("parallel", "parallel"),
        name="ssd_out" if with_output else "ssd_ctx",
    )(pm, pm, pm, dts_g, cum_g, src_t_g, conv_w, conv_b, conv_w, conv_b, conv_w, conv_b, dskip_g, h0f, h0b)


def _out_proj_kernel(yd_ref, z_ref, u_ref, v_ref, x_ref, g1_ref, ng_ref, lng_ref, lnb_ref, ws_ref, bst_ref, w_ref,
                     o_ref, mix_cur, mix_nxt):
    tm = yd_ref.shape[0]
    s = pl.program_id(0)
    n = pl.num_programs(0) - 1

    def build_steps():
        steps = []
        for k in range(tm // CHUNK):
            r = pl.ds(k * CHUNK, CHUNK)
            shared = {}

            def ssd_part(r=r):
                a = yd_ref[r, :].astype(F32) * z_ref[r, :].astype(F32)
                ms = jnp.mean(a * a, axis=-1, keepdims=True)
                mix_nxt[r, 0:SSD_WIDTH] = (a * lax.rsqrt(ms + NORM_EPS) * ng_ref[...]).astype(BF16)

            def ln_part(r=r, shared=shared):
                gv = v_ref[r, :].astype(F32)
                mu = jnp.mean(gv, axis=-1, keepdims=True)
                xc = gv - mu
                var = jnp.mean(xc * xc, axis=-1, keepdims=True)
                shared["ln"] = ((xc * lax.rsqrt(var + NORM_EPS)) * lng_ref[...] + lnb_ref[...]).astype(BF16)

            def gate_part(h0, h1, r=r, shared=shared):
                for h in range(h0, h1):
                    c0, c1 = h * CM_HEAD_DIM, (h + 1) * CM_HEAD_DIM
                    sp = _dot(ws_ref[h], shared["ln"][:, c0:c1]) + bst_ref[:, h:h + 1]
                    gu = u_ref[r, c0:c1].astype(F32)
                    mix_nxt[r, SSD_WIDTH + c0:SSD_WIDTH + c1] = (gu * sp).astype(BF16)

            steps += [ssd_part, ln_part, functools.partial(gate_part, 0, CM_HEADS // 2),
                      functools.partial(gate_part, CM_HEADS // 2, CM_HEADS)]
        return steps

    n_proj = 8
    pw = D_MODEL // n_proj

    def project_steps():
        def piece(q):
            c = slice(q * pw, (q + 1) * pw)
            o_ref[:, c] = x_ref[:, c] + g1_ref[:, c] * _dot(mix_cur[...], w_ref[:, c])

        return [functools.partial(piece, q) for q in range(n_proj)]

    @pl.when(s == 0)
    def _():
        for step in build_steps():
            step()

    @pl.when((s > 0) & (s < n))
    def _():
        for proj, build in zip(project_steps(), build_steps()):
            proj()
            build()

    @pl.when(s == n)
    def _():
        for step in project_steps():
            step()

    @pl.when(s < n)
    def _():
        mix_cur[...] = mix_nxt[...]


def _out_proj(yd, pm, x2, mod3, ssd_norm_g, ln_g, ln_b, ws, bst, w_out, *, seq, tm):
    m = x2.shape[0]
    n = m // tm
    tiles_per_batch = seq // tm
    z_blk, u_blk, v_blk = 0, 2 * SSD_WIDTH // CM_WIDTH, 2 * SSD_WIDTH // CM_WIDTH + 1
    g1_blk = 2
    row = lambda shape: pl.BlockSpec(shape, lambda s: (0, 0))
    build_tile = lambda s: jnp.minimum(s, n - 1)
    proj_tile = lambda s: jnp.maximum(s - 1, 0)
    return pl.pallas_call(
        _out_proj_kernel,
        grid=(n + 1,),
        in_specs=[
            pl.BlockSpec((tm, SSD_WIDTH), lambda s: (build_tile(s), 0)),
            pl.BlockSpec((tm, SSD_WIDTH), lambda s: (build_tile(s), z_blk)),
            pl.BlockSpec((tm, CM_WIDTH), lambda s: (build_tile(s), u_blk)),
            pl.BlockSpec((tm, CM_WIDTH), lambda s: (build_tile(s), v_blk)),
            pl.BlockSpec((tm, D_MODEL), lambda s: (proj_tile(s), 0)),
            pl.BlockSpec((None, 1, D_MODEL), lambda s: (proj_tile(s) // tiles_per_batch, 0, g1_blk)),
            row((1, SSD_WIDTH)),
            row((1, CM_WIDTH)),
            row((1, CM_WIDTH)),
            pl.BlockSpec((CM_HEADS, CHUNK, CHUNK), lambda s: (0, 0, 0)),
            row((CHUNK, CM_HEADS)),
            pl.BlockSpec((SSD_WIDTH + CM_WIDTH, D_MODEL), lambda s: (0, 0), pipeline_mode=pl.Buffered(1)),
        ],
        out_specs=pl.BlockSpec((tm, D_MODEL), lambda s: (proj_tile(s), 0)),
        out_shape=jax.ShapeDtypeStruct((m, D_MODEL), F32),
        scratch_shapes=[pltpu.VMEM((tm, SSD_WIDTH + CM_WIDTH), BF16)] * 2,
        compiler_params=_params("arbitrary"),
        name="out_proj",
    )(yd, pm, pm, pm, x2, mod3, ssd_norm_g, ln_g, ln_b, ws, bst, w_out)


def _route(logits):
    lane = lax.broadcasted_iota(jnp.int32, logits.shape, 1)
    lane_f = lane.astype(F32)
    is_group = (lane >= N_EXPERTS) & (lane < N_EXPERTS + N_GROUPS)
    lg = jnp.where(is_group, logits, NEG_BIG)
    mg = jnp.max(lg, axis=1, keepdims=True)
    top_pg = 1.0 / jnp.sum(jnp.exp(lg - mg), axis=1, keepdims=True)
    gi = jnp.min(jnp.where(lg == mg, lane_f, 1e9), axis=1, keepdims=True) - N_EXPERTS
    in_group = (lane < N_EXPERTS) & ((lane // EXPERTS_PER_GROUP).astype(F32) == gi)
    le = jnp.where(in_group, logits, NEG_BIG)
    m1 = jnp.max(le, axis=1, keepdims=True)
    i1 = jnp.min(jnp.where(le == m1, lane_f, 1e9), axis=1, keepdims=True)
    le2 = jnp.where(lane_f == i1, NEG_BIG, le)
    m2 = jnp.max(le2, axis=1, keepdims=True)
    i2 = jnp.min(jnp.where(le2 == m2, lane_f, 1e9), axis=1, keepdims=True)
    e2 = jnp.exp(m2 - m1)
    p1 = 1.0 / (1.0 + e2)
    p2 = e2 * p1
    return i1, i2, p1 * top_pg, p2 * top_pg


RINFO_E1, RINFO_E2, RINFO_W1, RINFO_W2, RINFO_R1, RINFO_R2 = range(6)


def _route_kernel(x1_ref, n2g_ref, sh2_ref, sc2_ref, wr_ref, br_ref, h2_ref, rinfo_ref, counts_ref, cnt_scr):
    tm = x1_ref.shape[0]

    @pl.when(pl.program_id(0) == 0)
    def _():
        cnt_scr[...] = jnp.zeros_like(cnt_scr)

    x1 = x1_ref[...]
    ms = jnp.mean(x1 * x1, axis=-1, keepdims=True)
    h2 = (x1 * lax.rsqrt(ms + NORM_EPS) * n2g_ref[...]) * (1.0 + sc2_ref[...]) + sh2_ref[...]
    h2_ref[...] = h2
    hi, mid, _ = _split3(h2)
    p = _dot(hi, wr_ref[...])
    logits = p[:, :LANES] + p[:, LANES:] + _dot(mid, wr_ref[:, :LANES]) + br_ref[...]
    i1, i2, w1, w2 = _route(logits)

    lane = lax.broadcasted_iota(jnp.int32, (tm, LANES), 1)
    lane_f = lane.astype(F32)
    oh1 = jnp.where(lane_f == i1, 1.0, 0.0)
    oh2 = jnp.where(lane_f == i2, 1.0, 0.0)
    before = (lax.broadcasted_iota(jnp.int32, (tm, tm), 0) > lax.broadcasted_iota(jnp.int32, (tm, tm), 1)).astype(BF16)
    carried = cnt_scr[...]
    tot1 = jnp.sum(oh1, axis=0, keepdims=True)
    r1 = jnp.sum(oh1 * (_dot(before, oh1.astype(BF16)) + carried), axis=1, keepdims=True)
    r2 = jnp.sum(oh2 * (_dot(before, oh2.astype(BF16)) + (carried + tot1)), axis=1, keepdims=True)
    counts = carried + tot1 + jnp.sum(oh2, axis=0, keepdims=True)
    cnt_scr[...] = counts
    counts_ref[...] = jnp.broadcast_to(counts, counts_ref.shape)

    info = jnp.zeros((tm, LANES), F32)
    for k, v in ((RINFO_E1, i1), (RINFO_E2, i2), (RINFO_W1, w1), (RINFO_W2, w2), (RINFO_R1, r1), (RINFO_R2, r2)):
        info = jnp.where(lane == k, v, info)
    rinfo_ref[...] = info


def _route_call(x1, mod3, norm2_g, w_router, b_router, *, seq, tm):
    m = x1.shape[0]
    tiles_per_batch = seq // tm
    row = lambda shape: pl.BlockSpec(shape, lambda i: (0, 0))
    modrow = lambda k: pl.BlockSpec((None, 1, D_MODEL), lambda i: (i // tiles_per_batch, 0, k))
    return pl.pallas_call(
        _route_kernel,
        grid=(m // tm,),
        in_specs=[
            pl.BlockSpec((tm, D_MODEL), lambda i: (i, 0)),
            row((1, D_MODEL)),
            modrow(3),
            modrow(4),
            row((D_MODEL, 2 * LANES)),
            row((1, LANES)),
        ],
        out_specs=[
            pl.BlockSpec((tm, D_MODEL), lambda i: (i, 0)),
            pl.BlockSpec((tm, LANES), lambda i: (i, 0)),
            row((8, LANES)),
        ],
        out_shape=[
            jax.ShapeDtypeStruct((m, D_MODEL), F32),
            jax.ShapeDtypeStruct((m, LANES), F32),
            jax.ShapeDtypeStruct((8, LANES), F32),
        ],
        scratch_shapes=[pltpu.VMEM((1, LANES), F32)],
        compiler_params=_params("arbitrary"),
        name="route",
    )(x1, norm2_g, mod3, mod3, w_router, b_router)


def _dispatch_kernel(lt_ref, nu_ref, pos_ref, h2_ref, x_hbm, zbuf, zsem, sem, *, tile_rows, n_tiles):
    tm = h2_ref.shape[0]
    n_used = nu_ref[0]

    def zero_copy(t):
        return pltpu.make_async_copy(zbuf, x_hbm.at[pl.ds(pl.multiple_of(t * tile_rows, tile_rows), tile_rows), :], zsem)

    @pl.when(pl.program_id(0) == 0)
    def _():
        zbuf[...] = jnp.zeros_like(zbuf)
        for phase in ("start", "wait"):
            for e in range(N_EXPERTS):
                @pl.when(lt_ref[e] >= 0)
                def _():
                    getattr(zero_copy(lt_ref[e]), phase)()

            def unused(t, carry):
                getattr(zero_copy(t), phase)()
                return carry

            lax.fori_loop(n_used, n_tiles, unused, 0)

    def body(r, carry):
        for k in range(2):
            pltpu.make_async_copy(h2_ref.at[pl.ds(r, 1), :], x_hbm.at[pl.ds(pos_ref[0, 2 * r + k], 1), :],
                                  sem).start(priority=k)
        return carry

    lax.fori_loop(0, tm, body, 0, unroll=8)
    for k in range(2):
        pltpu.make_async_copy(h2_ref, x_hbm.at[pl.ds(0, tm), :], sem).wait()


def _dispatch_call(last_tile, n_used, pos, h2, *, tm, tile_rows, n_tiles):
    m = h2.shape[0]
    grid_spec = pltpu.PrefetchScalarGridSpec(
        num_scalar_prefetch=2,
        grid=(m // tm,),
        in_specs=[
            pl.BlockSpec((None, 1, 2 * tm), lambda i, lt, nu: (i, 0, 0), memory_space=pltpu.SMEM),
            pl.BlockSpec((tm, D_MODEL), lambda i, lt, nu: (i, 0)),
        ],
        out_specs=pl.BlockSpec(memory_space=pl.ANY),
        scratch_shapes=[
            pltpu.VMEM((tile_rows, D_MODEL), F32),
            pltpu.SemaphoreType.DMA(()),
            pltpu.SemaphoreType.DMA(()),
        ],
    )
    return pl.pallas_call(
        functools.partial(_dispatch_kernel, tile_rows=tile_rows, n_tiles=n_tiles),
        grid_spec=grid_spec,
        out_shape=jax.ShapeDtypeStruct((n_tiles * tile_rows, D_MODEL), F32),
        compiler_params=_params("arbitrary"),
        name="dispatch",
    )(last_tile, n_used, pos, h2)


def _experts_kernel(te_ref, nu_ref, ord_ref, nxt_ref, x_ref, wg_hbm, wu_hbm, wd_hbm, y_ref,
                    wg_f, wu_f, wd_f, wg_b, wu_b, wd_b, sem):
    j = pl.program_id(0)
    n_used = nu_ref[0]

    def weight_copies(e, slot):
        return [pltpu.make_async_copy(src.at[e], dst.at[slot], sem.at[slot])
                for src, dst in ((wg_hbm, wg_f), (wu_hbm, wu_f), (wd_hbm, wd_f))]

    @pl.when(j == 0)
    def _():
        for cp in weight_copies(te_ref[0], 0):
            cp.start()

    @pl.when((j < n_used) & ((j == 0) | (te_ref[j] != te_ref[jnp.maximum(j - 1, 0)])))
    def _():
        slot = ord_ref[j] % 2
        for cp in weight_copies(te_ref[j], slot):
            cp.wait()

        @pl.when(nxt_ref[j] >= 0)
        def _():
            for cp in weight_copies(nxt_ref[j], 1 - slot):
                cp.start()

        wg_b[...] = wg_f[slot].astype(BF16)
        wu_b[...] = wu_f[slot].astype(BF16)
        wd_b[...] = wd_f[slot].astype(BF16)

    @pl.when(j < n_used)
    def _():
        xt = x_ref[...].astype(BF16)
        hid = (_silu(_dot(xt, wg_b[...])) * _dot(xt, wu_b[...])).astype(BF16)
        y_ref[...] = _dot(hid, wd_b[...])

    @pl.when(j >= n_used)
    def _():
        y_ref[...] = jnp.zeros_like(y_ref)


def _experts_call(tile_expert, n_used, tile_ord, next_expert, x_sorted, wg, wu, wd, *, tm):
    n_tiles = tile_expert.shape[0]
    hbm = pl.BlockSpec(memory_space=pl.ANY)
    grid_spec = pltpu.PrefetchScalarGridSpec(
        num_scalar_prefetch=4,
        grid=(n_tiles,),
        in_specs=[
            pl.BlockSpec((tm, D_MODEL), lambda j, te, nu, od, nx: (jnp.minimum(j, nu[0] - 1), 0)),
            hbm,
            hbm,
            hbm,
        ],
        out_specs=pl.BlockSpec((tm, D_MODEL), lambda j, te, nu, od, nx: (j, 0)),
        scratch_shapes=[
            pltpu.VMEM((2, D_MODEL, EXPERT_FF), F32),
            pltpu.VMEM((2, D_MODEL, EXPERT_FF), F32),
            pltpu.VMEM((2, EXPERT_FF, D_MODEL), F32),
            pltpu.VMEM((D_MODEL, EXPERT_FF), BF16),
            pltpu.VMEM((D_MODEL, EXPERT_FF), BF16),
            pltpu.VMEM((EXPERT_FF, D_MODEL), BF16),
            pltpu.SemaphoreType.DMA((2,)),
        ],
    )
    return pl.pallas_call(
        _experts_kernel,
        grid_spec=grid_spec,
        out_shape=jax.ShapeDtypeStruct((n_tiles * tm, D_MODEL), F32),
        compiler_params=_params("arbitrary"),
        name="experts",
    )(tile_expert, n_used, tile_ord, next_expert, x_sorted, wg, wu, wd)


def _combine_kernel(pos_cur, pos_nxt, x1_ref, g2_ref, nfg_ref, rinfo_ref, y_hbm, o_ref, ybuf, sem):
    i = pl.program_id(0)
    tm = x1_ref.shape[0]
    slot = i % 2

    def row_copy(pos_ref, r, k, s):
        return pltpu.make_async_copy(y_hbm.at[pl.ds(pos_ref[0, 2 * r + k], 1), :], ybuf.at[s, k, pl.ds(r, 1), :],
                                     sem.at[s])

    def wait_tile(s):
        for k in range(2):
            pltpu.make_async_copy(y_hbm.at[pl.ds(0, tm), :], ybuf.at[s, k], sem.at[s]).wait()

    @pl.when(i == 0)
    def _():
        def body(r, carry):
            for k in range(2):
                row_copy(pos_cur, r, k, 0).start()
            return carry

        lax.fori_loop(0, tm, body, 0)

    wait_tile(slot)
    for r in range(tm):
        for k in range(2):
            row_copy(pos_nxt, r, k, 1 - slot).start(priority=k)
    info = rinfo_ref[...]
    w1 = info[:, RINFO_W1:RINFO_W1 + 1]
    w2 = info[:, RINFO_W2:RINFO_W2 + 1]
    moe = w1 * ybuf[slot, 0] + w2 * ybuf[slot, 1]
    y = x1_ref[...] + g2_ref[...] * moe
    ms = jnp.mean(y * y, axis=-1, keepdims=True)
    o_ref[...] = y * lax.rsqrt(ms + NORM_EPS) * nfg_ref[...]

    @pl.when(i == pl.num_programs(0) - 1)
    def _():
        wait_tile(1 - slot)


def _combine_call(pos, x1, mod3, normf_g, rinfo, y_sorted, *, seq, tm):
    m = x1.shape[0]
    n_tiles = m // tm
    tiles_per_batch = seq // tm
    return pl.pallas_call(
        _combine_kernel,
        grid=(n_tiles,),
        in_specs=[
            pl.BlockSpec((None, 1, 2 * tm), lambda i: (i, 0, 0), memory_space=pltpu.SMEM),
            pl.BlockSpec((None, 1, 2 * tm), lambda i: (jnp.minimum(i + 1, n_tiles - 1), 0, 0),
                         memory_space=pltpu.SMEM),
            pl.BlockSpec((tm, D_MODEL), lambda i: (i, 0)),
            pl.BlockSpec((None, 1, D_MODEL), lambda i: (i // tiles_per_batch, 0, 5)),
            pl.BlockSpec((1, D_MODEL), lambda i: (0, 0)),
            pl.BlockSpec((tm, LANES), lambda i: (i, 0)),
            pl.BlockSpec(memory_space=pl.ANY),
        ],
        out_specs=pl.BlockSpec((tm, D_MODEL), lambda i: (i, 0)),
        out_shape=jax.ShapeDtypeStruct((m, D_MODEL), F32),
        scratch_shapes=[
            pltpu.VMEM((2, 2, tm, D_MODEL), F32),
            pltpu.SemaphoreType.DMA((2,)),
        ],
        compiler_params=_params("arbitrary"),
        name="combine",
    )(pos, pos, x1, mod3, normf_g, rinfo, y_sorted)


def _dispatch_plan(rinfo, counts, *, tm):
    n_tok = rinfo.shape[0]
    n_tiles = (2 * n_tok + N_EXPERTS * (tm - 1)) // tm + 1
    expert = rinfo[:, RINFO_E1:RINFO_E2 + 1].astype(jnp.int32)
    rank = rinfo[:, RINFO_R1:RINFO_R2 + 1].astype(jnp.int32)
    cnt = counts[0, :N_EXPERTS].astype(jnp.int32)
    tiles_e = (cnt + tm - 1) // tm
    end_tile = jnp.cumsum(tiles_e)
    start_row = (end_tile - tiles_e) * tm
    is_e = expert[:, :, None] == jnp.arange(N_EXPERTS, dtype=jnp.int32)
    pos = jnp.sum(jnp.where(is_e, start_row, 0), axis=-1) + rank
    n_used = end_tile[-1:]
    tile_ids = jnp.minimum(jnp.arange(n_tiles, dtype=jnp.int32), n_used - 1)
    tile_expert = jnp.sum(tile_ids[:, None] >= end_tile[None, :], axis=1).astype(jnp.int32)
    last_tile = jnp.where(tiles_e > 0, end_tile - 1, -1).astype(jnp.int32)
    used = (tiles_e > 0).astype(jnp.int32)
    ord_e = jnp.cumsum(used) - used
    ids = jnp.arange(N_EXPERTS, dtype=jnp.int32)
    later = jnp.where((ids[None, :] > ids[:, None]) & (used[None, :] > 0), ids[None, :], N_EXPERTS)
    nxt_e = jnp.min(later, axis=1)
    nxt_e = jnp.where(nxt_e < N_EXPERTS, nxt_e, -1).astype(jnp.int32)
    is_te = tile_expert[:, None] == ids[None, :]
    tile_ord = jnp.sum(jnp.where(is_te, ord_e, 0), axis=1).astype(jnp.int32)
    next_expert = jnp.sum(jnp.where(is_te, nxt_e, 0), axis=1).astype(jnp.int32)
    return pos, (tile_expert, n_used.astype(jnp.int32), tile_ord, next_expert), last_tile, n_tiles


def kernel(x, c, ctx, c_ctx, w_mod, b_mod, norm1_g, w_in, conv_w, conv_b, dt_bias_f, dt_bias_b, a_log_f, a_log_b,
           d_skip, ssd_norm_g, cm_ln_g, cm_ln_b, w_spatial, b_spatial, w_out, norm2_g, w_router_group,
           b_router_group, w_router_expert, b_router_expert, w_exp_gate, w_exp_up, w_exp_down, normf_g):
    bsz, seq, _ = x.shape
    ctx_len = ctx.shape[1]
    i = 0

    cc = jnp.concatenate([c, c_ctx[None, :], jnp.zeros((MOD_ROWS - bsz - 1, D_MODEL), F32)], axis=0)
    mod = _modulation(cc, w_mod[i], b_mod[i][None, :])
    mod3 = mod.reshape(MOD_ROWS, 1, N_MOD * D_MODEL)

    w_main, w_dt = _w_in_prep(jnp.swapaxes(w_in[i], 0, 1))
    g1row = norm1_g[i][None, :]

    x2 = x.reshape(bsz * seq, D_MODEL)
    ctx2 = ctx.reshape(bsz * ctx_len, D_MODEL)
    tm_in = 1024
    tn_in = 1024
    xs_col0, bc_col0 = SSD_WIDTH, 2 * SSD_WIDTH + 2 * CM_WIDTH
    pm_x, dt_x = _in_proj(x2, mod3, lambda t: t // (seq // tm_in), g1row, w_main, w_dt, tm_in, tn_in)
    ctx_blocks = tuple(range(xs_col0 // tn_in, 2 * SSD_WIDTH // tn_in)) + (bc_col0 // tn_in,)
    pm_c, dt_c = _in_proj(ctx2, mod3, lambda t: bsz, g1row, w_main, w_dt, bsz * ctx_len, tn_in, ctx_blocks)

    lane_pad = (0, LANES - 2 * SSD_HEADS)
    dt_bias_row = jnp.pad(jnp.concatenate([dt_bias_f[i], dt_bias_b[i]]).astype(F32), lane_pad)[None, :]
    a_row = jnp.pad(-jnp.exp(jnp.concatenate([a_log_f[i], a_log_b[i]]).astype(F32)), lane_pad)[None, :]
    dskip_g = jnp.repeat(d_skip[i], SSD_HEAD_DIM).reshape(SSD_GROUPS, 1, GROUP_WIDTH)
    cw = conv_w[i]
    cb = conv_b[i][None, :]
    h_zero = jnp.zeros((bsz, SSD_GROUPS, SSD_STATE, GROUP_WIDTH), F32)

    dt_terms_c = _ssd_dt(dt_c, dt_bias_row, a_row, rows_per_step=bsz * ctx_len)
    dt_terms_x = _ssd_dt(dt_x, dt_bias_row, a_row, rows_per_step=1024)
    hc_f, hc_b = _ssd(pm_c, 0, SSD_WIDTH, dt_terms_c, cw, cb, dskip_g, h_zero, h_zero,
                      batch=bsz, seq=ctx_len, conv_tile=ctx_len, conv_period=ctx_len, with_output=False)
    yd = _ssd(pm_x, xs_col0, bc_col0, dt_terms_x, cw, cb, dskip_g, hc_f, hc_b,
              batch=bsz, seq=seq, conv_tile=CHUNK, conv_period=GRID_W, with_output=True)

    x1 = _out_proj(yd, pm_x, x2, mod3, ssd_norm_g[i][None, :], cm_ln_g[i][None, :], cm_ln_b[i][None, :],
                   w_spatial[i].astype(BF16), b_spatial[i].T, w_out[i].astype(BF16), seq=seq, tm=256)

    w_re = jnp.transpose(w_router_expert[i], (1, 0, 2)).reshape(D_MODEL, N_EXPERTS)
    pad = LANES - N_EXPERTS - N_GROUPS
    w_router = jnp.pad(jnp.concatenate([w_re, w_router_group[i]], axis=1), ((0, 0), (0, pad)))
    w_router_hi = w_router.astype(BF16)
    w_router = jnp.concatenate([w_router_hi, (w_router - w_router_hi.astype(F32)).astype(BF16)], axis=1)
    b_router = jnp.pad(jnp.concatenate([b_router_expert[i].reshape(-1), b_router_group[i]]), (0, pad))[None, :]
    h2, rinfo, counts = _route_call(x1, mod3, norm2_g[i][None, :], w_router, b_router, seq=seq, tm=512)

    tm_e = 256
    pos, tile_tables, last_tile, n_tiles = _dispatch_plan(rinfo, counts, tm=tm_e)
    tm_d = 512
    x_sorted = _dispatch_call(last_tile, tile_tables[1], pos.reshape(bsz * seq // tm_d, 1, 2 * tm_d), h2,
                              tm=tm_d, tile_rows=tm_e, n_tiles=n_tiles)
    wg = w_exp_gate[i].reshape(N_EXPERTS, D_MODEL, EXPERT_FF)
    wu = w_exp_up[i].reshape(N_EXPERTS, D_MODEL, EXPERT_FF)
    wd = w_exp_down[i].reshape(N_EXPERTS, EXPERT_FF, D_MODEL)
    y_sorted = _experts_call(*tile_tables, x_sorted, wg, wu, wd, tm=tm_e)
    tm_c = 256
    out = _combine_call(pos.reshape(bsz * seq // tm_c, 1, 2 * tm_c), x1, mod3, normf_g[None, :], rinfo, y_sorted,
                        seq=seq, tm=tm_c)
    return out.reshape(bsz, seq, D_MODEL)
```

```python
import functools

import jax
import jax.numpy as jnp
from jax import lax
from jax.experimental import pallas as pl
from jax.experimental.pallas import tpu as pltpu

F32 = jnp.float32
BF16 = jnp.bfloat16
HIGHEST = lax.Precision.HIGHEST

D_MODEL = 2048
GRID_W = 64
SSD_WIDTH = 2048
CM_WIDTH = 2048
SSD_HEADS = 32
SSD_HEAD_DIM = 64
SSD_GROUPS = 4
HEADS_PER_GROUP = SSD_HEADS // SSD_GROUPS
GROUP_WIDTH = HEADS_PER_GROUP * SSD_HEAD_DIM
SSD_STATE = 128
CHUNK = 128
BC_WIDTH = SSD_GROUPS * SSD_STATE
CM_HEADS = 8
CM_HEAD_DIM = CM_WIDTH // CM_HEADS
N_GROUPS = 4
N_EXPERTS = 32
EXPERTS_PER_GROUP = 8
EXPERT_FF = 512
N_MOD = 6
NORM_EPS = 1e-6
LANES = 128
NEG_BIG = -1e30
MOD_ROWS = 8
VMEM_LIMIT = 56 * 1024 * 1024


def _params(*sem):
    return pltpu.CompilerParams(dimension_semantics=sem, vmem_limit_bytes=VMEM_LIMIT)


LOG2_E = 1.4426950408889634
GELU_C = 0.7978845608028654
GELU_K1 = -2.0 * GELU_C * LOG2_E
GELU_K3 = GELU_K1 * 0.044715


def _silu(v):
    return v / (1.0 + jnp.exp2(v * (-LOG2_E)))


def _gelu_tanh(v):
    return v / (1.0 + jnp.exp2(v * (GELU_K1 + GELU_K3 * (v * v))))


def _dot(a, b):
    return jnp.dot(a, b, preferred_element_type=F32)


def _dot_nt(a, b):
    return lax.dot_general(a, b, (((1,), (1,)), ((), ())), preferred_element_type=F32)


def _mod_kernel(cc_ref, w_ref, b_ref, o_ref):
    a = _silu(cc_ref[...])
    hi = a.astype(BF16).astype(F32)
    lhs = jnp.concatenate([hi, a - hi], axis=0).astype(BF16)
    r = _dot(lhs, w_ref[...].astype(BF16))
    o_ref[...] = r[:MOD_ROWS] + r[MOD_ROWS:] + b_ref[...]


def _modulation(cc, w_mod, b_mod):
    n = w_mod.shape[1]
    tn = 1024
    return pl.pallas_call(
        _mod_kernel,
        grid=(n // tn,),
        in_specs=[
            pl.BlockSpec((MOD_ROWS, D_MODEL), lambda j: (0, 0)),
            pl.BlockSpec((D_MODEL, tn), lambda j: (0, j)),
            pl.BlockSpec((1, tn), lambda j: (0, j)),
        ],
        out_specs=pl.BlockSpec((MOD_ROWS, tn), lambda j: (0, j)),
        out_shape=jax.ShapeDtypeStruct((MOD_ROWS, n), F32),
        compiler_params=_params("arbitrary"),
        name="modulation",
    )(cc, w_mod, b_mod)


PREP_ROWS = 512
PREP_UV_BLK0 = 2 * SSD_WIDTH // PREP_ROWS
PREP_BC_BLK0 = (2 * SSD_WIDTH + 2 * CM_WIDTH) // PREP_ROWS
PREP_N_BLKS = PREP_BC_BLK0 + 2 * BC_WIDTH // PREP_ROWS
DT_COL0 = 2 * SSD_WIDTH + 2 * BC_WIDTH
UV_COL0 = DT_COL0 + 2 * SSD_HEADS


def _w_in_prep_kernel(a_ref, dt_ref, o_ref, wdt_ref):
    @pl.when(pl.program_id(0) == 0)
    def _():
        hpg = HEADS_PER_GROUP
        parts = []
        for g in range(SSD_GROUPS):
            parts += [dt_ref[hpg * g:hpg * (g + 1), :], dt_ref[SSD_HEADS + hpg * g:SSD_HEADS + hpg * (g + 1), :]]
        parts.append(dt_ref[2 * SSD_HEADS:, :])
        wdt_ref[...] = jnp.concatenate(parts, axis=0).astype(BF16)

    o_ref[...] = a_ref[...].astype(BF16)


def _w_in_prep(w_in_t):
    def src_row(j):
        uv = UV_COL0 + (j - PREP_UV_BLK0) * PREP_ROWS
        bc = 2 * SSD_WIDTH + (j - PREP_BC_BLK0) * PREP_ROWS
        return pl.multiple_of(jnp.where(j < PREP_UV_BLK0, j * PREP_ROWS, jnp.where(j < PREP_BC_BLK0, uv, bc)), 8)

    return pl.pallas_call(
        _w_in_prep_kernel,
        grid=(PREP_N_BLKS,),
        in_specs=[
            pl.BlockSpec((pl.Element(PREP_ROWS), pl.Element(D_MODEL)), lambda j: (src_row(j), 0)),
            pl.BlockSpec((LANES, D_MODEL), lambda j: (DT_COL0 // LANES, 0)),
        ],
        out_specs=[
            pl.BlockSpec((PREP_ROWS, D_MODEL), lambda j: (j, 0)),
            pl.BlockSpec((LANES, D_MODEL), lambda j: (0, 0)),
        ],
        out_shape=[
            jax.ShapeDtypeStruct((PREP_N_BLKS * PREP_ROWS, D_MODEL), BF16),
            jax.ShapeDtypeStruct((LANES, D_MODEL), BF16),
        ],
        compiler_params=_params("arbitrary"),
        name="w_in_prep",
    )(w_in_t, w_in_t)


ACT_NONE, ACT_SILU, ACT_GELU = 0, 1, 2


def _in_proj_kernel(x_ref, g_ref, sh_ref, sc_ref, w_ref, wdt_ref, o_ref, dt_ref, h_scr, *, block_acts):
    tm = x_ref.shape[0]
    rc = 256

    @pl.when(pl.program_id(1) == 0)
    def _():
        gain = g_ref[...]
        scale = 1.0 + sc_ref[...]
        shift = sh_ref[...]

        def body(k, carry):
            r = pl.ds(pl.multiple_of(k * rc, rc), rc)
            xv = x_ref[r, :]
            ms = jnp.mean(xv * xv, axis=-1, keepdims=True)
            h = (xv * lax.rsqrt(ms + NORM_EPS) * gain) * scale + shift
            h_scr[r, :] = h.astype(BF16)
            return carry

        lax.fori_loop(0, tm // rc, body, 0)
        dt_ref[...] = _dot_nt(h_scr[...], wdt_ref[...])

    j = pl.program_id(1)
    for act, fn in ((ACT_NONE, lambda v: v), (ACT_SILU, _silu), (ACT_GELU, _gelu_tanh)):
        blocks = [q for q, a in enumerate(block_acts) if a == act]
        if blocks:
            is_act = functools.reduce(jnp.logical_or, [j == q for q in blocks])

            @pl.when(is_act)
            def _():
                o_ref[...] = fn(_dot_nt(h_scr[...], w_ref[...])).astype(o_ref.dtype)


def _in_proj(x2, mod3, mod_row_of_tile, norm_g, w_main, w_dt, tm, tn, w_blocks=None):
    m = x2.shape[0]
    if w_blocks is None:
        w_blocks = tuple(range(w_main.shape[0] // tn))
    n = len(w_blocks) * tn
    z_end, uv0, uv_end = SSD_WIDTH // tn, 2 * SSD_WIDTH // tn, (2 * SSD_WIDTH + 2 * CM_WIDTH) // tn
    block_acts = tuple(ACT_SILU if b < z_end else ACT_GELU if uv0 <= b < uv_end else ACT_NONE for b in w_blocks)

    def w_blk(j):
        if w_blocks == tuple(range(len(w_blocks))):
            return j
        blk = jnp.int32(w_blocks[-1])
        for q in range(len(w_blocks) - 2, -1, -1):
            blk = jnp.where(j == q, w_blocks[q], blk)
        return blk

    return pl.pallas_call(
        functools.partial(_in_proj_kernel, block_acts=block_acts),
        grid=(m // tm, n // tn),
        in_specs=[
            pl.BlockSpec((tm, D_MODEL), lambda i, j: (i, 0)),
            pl.BlockSpec((1, D_MODEL), lambda i, j: (0, 0)),
            pl.BlockSpec((None, 1, D_MODEL), lambda i, j: (mod_row_of_tile(i), 0, 0)),
            pl.BlockSpec((None, 1, D_MODEL), lambda i, j: (mod_row_of_tile(i), 0, 1)),
            pl.BlockSpec((tn, D_MODEL), lambda i, j: (w_blk(j), 0)),
            pl.BlockSpec((LANES, D_MODEL), lambda i, j: (0, 0)),
        ],
        out_specs=[
            pl.BlockSpec((tm, tn), lambda i, j: (i, j)),
            pl.BlockSpec((tm, LANES), lambda i, j: (i, 0)),
        ],
        out_shape=[
            jax.ShapeDtypeStruct((m, n), BF16),
            jax.ShapeDtypeStruct((m, LANES), F32),
        ],
        scratch_shapes=[pltpu.VMEM((tm, D_MODEL), BF16)],
        compiler_params=_params("parallel", "arbitrary"),
        name="in_proj",
    )(x2, norm_g, mod3, mod3, w_main, w_dt)


def _split3(v):
    hi = v.astype(BF16)
    rem = v - hi.astype(F32)
    mid = rem.astype(BF16)
    lo = (rem - mid.astype(F32)).astype(BF16)
    return hi, mid, lo


GROUP_LANES = 2 * HEADS_PER_GROUP


def _stack_split3(v):
    hi, mid, lo = _split3(v)
    stacked = (hi.astype(F32) + pltpu.roll(mid.astype(F32), GROUP_LANES, 1)
               + pltpu.roll(lo.astype(F32), 2 * GROUP_LANES, 1))
    return stacked.astype(BF16)


def _ssd_dt_kernel(dt_ref, bias_ref, a_ref, dts_ref, cum_ref, src_t_ref):
    hpg = HEADS_PER_GROUP
    ii = lax.broadcasted_iota(jnp.int32, (CHUNK, CHUNK), 0)
    jj = lax.broadcasted_iota(jnp.int32, (CHUNK, CHUNK), 1)
    tri = (ii >= jj).astype(BF16)
    tri3 = jnp.concatenate([tri, tri, tri], axis=1)
    bias = bias_ref[...]
    a_row = a_ref[...]

    def body(k, carry):
        r = pl.ds(pl.multiple_of(k * CHUNK, CHUNK), CHUNK)
        raw = dt_ref[r, :] + bias
        dts = jnp.where(jj < 2 * SSD_HEADS, jnp.maximum(raw, 0.0) + jnp.log1p(jnp.exp(-jnp.abs(raw))), 0.0)
        adt = dts * a_row
        hi, mid, lo = _split3(adt)
        cf = _dot(tri3, jnp.concatenate([hi, mid, lo], axis=0))
        cr = cf[CHUNK - 1:CHUNK, :] - cf + adt
        valid = jj < 2 * SSD_HEADS
        cum = jnp.where(jj % GROUP_LANES < hpg, cf, cr)
        src_t = ((cum - jnp.log(jnp.where(valid, dts, 1.0))) * LOG2_E).T
        rt = pl.ds(pl.multiple_of(k * GROUP_LANES, GROUP_LANES), GROUP_LANES)
        for g in range(SSD_GROUPS):
            g0 = g * GROUP_LANES
            for src, dst in ((cum * LOG2_E, cum_ref), (dts, dts_ref)):
                lanes = src if g == 0 else pltpu.roll(src, LANES - g0, 1)
                dst[g, r, :] = jnp.where(jj < GROUP_LANES, lanes, 0.0)
            src_t_ref[g, rt, :] = src_t[g0:g0 + GROUP_LANES, :]
        return carry

    lax.fori_loop(0, dt_ref.shape[0] // CHUNK, body, 0)


def _ssd_dt(dt, bias_row, a_row, *, rows_per_step):
    m = dt.shape[0]
    row = pl.BlockSpec((1, LANES), lambda i: (0, 0))
    full = pl.BlockSpec((SSD_GROUPS, rows_per_step, LANES), lambda i: (0, i, 0))
    tr = pl.BlockSpec((SSD_GROUPS, rows_per_step // CHUNK * GROUP_LANES, LANES), lambda i: (0, i, 0))
    full_shape = jax.ShapeDtypeStruct((SSD_GROUPS, m, LANES), F32)
    tr_shape = jax.ShapeDtypeStruct((SSD_GROUPS, m // CHUNK * GROUP_LANES, LANES), F32)
    return pl.pallas_call(
        _ssd_dt_kernel,
        grid=(m // rows_per_step,),
        in_specs=[pl.BlockSpec((rows_per_step, LANES), lambda i: (i, 0)), row, row],
        out_specs=[full, full, tr],
        out_shape=[full_shape, full_shape, tr_shape],
        compiler_params=_params("parallel"),
        name="ssd_dt",
    )(dt, bias_row, a_row)


def _ssd_kernel(xs_ref, b_ref, c_ref, dts_ref, cum_ref, src_t_ref, cwx_ref, cbx_ref, cwb_ref, cbb_ref,
                cwc_ref, cbc_ref, dskip_ref, h0f_ref, h0b_ref, *rest,
                n_chunks, conv_tile, conv_period, with_output):
    if with_output:
        y_ref, xs_s, b_s, c_s, hf_s, hb_s, y_s, yb_s = rest
    else:
        hf_out, hb_out, xs_s, b_s, c_s, hf_s, hb_s = rest
    seq = n_chunks * CHUNK
    hpg = HEADS_PER_GROUP

    def conv_silu(src_ref, dst_ref, w_ref, bias_ref):
        cols = src_ref.shape[1]
        w1 = w_ref[1:2, :]
        bias = bias_ref[...]
        row = lax.broadcasted_iota(jnp.int32, (conv_tile, cols), 0) % conv_period
        w0 = jnp.where(row == 0, 0.0, w_ref[0:1, :])
        w2 = jnp.where(row == conv_period - 1, 0.0, w_ref[2:3, :])

        def body(k, carry):
            r = pl.ds(pl.multiple_of(k * conv_tile, conv_tile), conv_tile)
            v = src_ref[r, :].astype(F32)
            y = bias + pltpu.roll(v, 1, 0) * w0 + v * w1 + pltpu.roll(v, conv_tile - 1, 0) * w2
            dst_ref[r, :] = _silu(y).astype(BF16)
            return carry

        lax.fori_loop(0, seq // conv_tile, body, 0)

    conv_silu(xs_ref, xs_s, cwx_ref, cbx_ref)
    conv_silu(b_ref, b_s, cwb_ref, cbb_ref)
    conv_silu(c_ref, c_s, cwc_ref, cbc_ref)

    ii = lax.broadcasted_iota(jnp.int32, (CHUNK, CHUNK), 0)
    jj = lax.broadcasted_iota(jnp.int32, (CHUNK, CHUNK), 1)
    lower = ii >= jj
    upper = jj >= ii

    hf_s[...] = h0f_ref[...]
    hb_s[...] = h0b_ref[...]

    head_of_col = lax.broadcasted_iota(jnp.int32, (LANES, GROUP_WIDTH), 1) // SSD_HEAD_DIM
    sel_row = lax.broadcasted_iota(jnp.int32, (LANES, GROUP_WIDTH), 0)
    sel_valid = sel_row < 3 * GROUP_LANES
    sel_f = (sel_valid & (sel_row % GROUP_LANES == head_of_col)).astype(BF16)
    sel_b = (sel_valid & (sel_row % GROUP_LANES == head_of_col + hpg)).astype(BF16)
    low_half = jj < SSD_HEAD_DIM
    group_lane = jj < GROUP_LANES

    def direction_terms(r, sel):
        cum = cum_ref[r, :]
        dts = dts_ref[r, :]
        tot = jnp.where(jj[0:1, :] < hpg, cum[CHUNK - 1:CHUNK, :], cum[0:1, :])
        wst = dts * jnp.exp2(tot - cum)
        eoff = jnp.where(group_lane, jnp.exp2(cum), 0.0)
        edec = jnp.broadcast_to(jnp.where(group_lane[0:1, :], jnp.exp2(tot), 0.0), (8, LANES))
        ex = _dot(_stack_split3(jnp.concatenate([wst, eoff, edec], axis=0)), sel)
        return cum, ex[:CHUNK], ex[CHUNK:2 * CHUNK], ex[2 * CHUNK:2 * CHUNK + 1]

    def state_step(h_s, r, sel):
        cum, w_state, e_off, e_dec = direction_terms(r, sel)
        xc = xs_s[r, :]
        xw = (xc.astype(F32) * w_state).astype(BF16)
        s_new = lax.dot_general(b_s[r, :], xw, (((0,), (0,)), ((), ())), preferred_element_type=F32)
        h = h_s[...]
        y_off = _dot(c_s[r, :], h.astype(BF16)) * e_off
        h_s[...] = h * e_dec + s_new
        return cum, xc, y_off

    def fwd_body(k, carry):
        r = pl.ds(pl.multiple_of(k * CHUNK, CHUNK), CHUNK)
        cum, xc, y_off = state_step(hf_s, r, sel_f)
        if with_output:
            g = _dot_nt(c_s[r, :], b_s[r, :])
            rt = pl.ds(pl.multiple_of(k * GROUP_LANES, GROUP_LANES), GROUP_LANES)
            src_t = src_t_ref[rt, :]
            pieces = []
            for pair in range(hpg // 2):
                ms = []
                for h in (2 * pair, 2 * pair + 1):
                    hb = h + hpg
                    mf = jnp.exp2(jnp.where(lower, cum[:, h:h + 1] - src_t[h:h + 1, :], NEG_BIG))
                    mb = jnp.exp2(jnp.where(upper, cum[:, hb:hb + 1] - src_t[hb:hb + 1, :], NEG_BIG))
                    ms.append((g * (mf + mb)).astype(BF16))
                lhs = jnp.concatenate(ms, axis=1)
                xp = xc[:, pair * LANES:(pair + 1) * LANES]
                zero = jnp.zeros_like(xp)
                rhs = jnp.concatenate([jnp.where(low_half, xp, zero), jnp.where(low_half, zero, xp)], axis=0)
                pieces.append(_dot(lhs, rhs))
            y_diag = jnp.concatenate(pieces, axis=1)
            y_s[r, :] = y_diag + y_off + xc.astype(F32) * dskip_ref[...]
        rb = pl.ds(pl.multiple_of((n_chunks - 1 - k) * CHUNK, CHUNK), CHUNK)
        _, _, y_off_b = state_step(hb_s, rb, sel_b)
        if with_output:
            yb_s[rb, :] = y_off_b
        return carry

    lax.fori_loop(0, n_chunks, fwd_body, 0, unroll=8 if n_chunks % 8 == 0 else 2)

    if with_output:
        def sum_body(k, carry):
            r = pl.ds(pl.multiple_of(k * CHUNK, CHUNK), CHUNK)
            y_ref[r, :] = (y_s[r, :] + yb_s[r, :]).astype(y_ref.dtype)
            return carry

        lax.fori_loop(0, n_chunks, sum_body, 0)

    if not with_output:
        hf_out[...] = hf_s[...]
        hb_out[...] = hb_s[...]


def _ssd(pm, xs_col0, bc_col0, dt_terms, conv_w, conv_b, dskip_g, h0f, h0b, *, batch, seq, conv_tile, conv_period,
         with_output):
    n_chunks = seq // CHUNK
    dts_g, cum_g, src_t_g = dt_terms
    dt_spec = pl.BlockSpec((None, seq, LANES), lambda b, g: (g, b, 0))
    dt_t_spec = pl.BlockSpec((None, n_chunks * GROUP_LANES, LANES), lambda b, g: (g, b, 0))
    xs_blk0 = xs_col0 // GROUP_WIDTH
    b_blk0 = bc_col0 // SSD_STATE
    c_blk0 = b_blk0 + SSD_GROUPS
    cw_b0 = SSD_WIDTH // SSD_STATE
    cw_c0 = cw_b0 + SSD_GROUPS
    state_spec = pl.BlockSpec((None, None, SSD_STATE, GROUP_WIDTH), lambda b, g: (b, g, 0, 0))
    in_specs = [
        pl.BlockSpec((seq, GROUP_WIDTH), lambda b, g: (b, xs_blk0 + g)),
        pl.BlockSpec((seq, SSD_STATE), lambda b, g: (b, b_blk0 + g)),
        pl.BlockSpec((seq, SSD_STATE), lambda b, g: (b, c_blk0 + g)),
        dt_spec,
        dt_spec,
        dt_t_spec,
        pl.BlockSpec((3, GROUP_WIDTH), lambda b, g: (0, g)),
        pl.BlockSpec((1, GROUP_WIDTH), lambda b, g: (0, g)),
        pl.BlockSpec((3, SSD_STATE), lambda b, g: (0, cw_b0 + g)),
        pl.BlockSpec((1, SSD_STATE), lambda b, g: (0, cw_b0 + g)),
        pl.BlockSpec((3, SSD_STATE), lambda b, g: (0, cw_c0 + g)),
        pl.BlockSpec((1, SSD_STATE), lambda b, g: (0, cw_c0 + g)),
        pl.BlockSpec((None, 1, GROUP_WIDTH), lambda b, g: (g, 0, 0)),
        state_spec,
        state_spec,
    ]
    scratch = [
        pltpu.VMEM((seq, GROUP_WIDTH), BF16),
        pltpu.VMEM((seq, SSD_STATE), BF16),
        pltpu.VMEM((seq, SSD_STATE), BF16),
        pltpu.VMEM((SSD_STATE, GROUP_WIDTH), F32),
        pltpu.VMEM((SSD_STATE, GROUP_WIDTH), F32),
    ]
    if with_output:
        out_specs = pl.BlockSpec((seq, GROUP_WIDTH), lambda b, g: (b, g))
        out_shape = jax.ShapeDtypeStruct((batch * seq, SSD_WIDTH), BF16)
        scratch += [pltpu.VMEM((seq, GROUP_WIDTH), F32)] * 2
    else:
        out_specs = [state_spec, state_spec]
        out_shape = [jax.ShapeDtypeStruct((batch, SSD_GROUPS, SSD_STATE, GROUP_WIDTH), F32)] * 2
    return pl.pallas_call(
        functools.partial(_ssd_kernel, n_chunks=n_chunks, conv_tile=conv_tile, conv_period=conv_period,
                          with_output=with_output),
        grid=(batch, SSD_GROUPS),
        in_specs=in_specs,
        out_specs=out_specs,
        out_shape=out_shape,
        scratch_shapes=scratch,
        compiler_params=_params("parallel", "parallel"),
        name="ssd_out" if with_output else "ssd_ctx",
    )(pm, pm, pm, dts_g, cum_g, src_t_g, conv_w, conv_b, conv_w, conv_b, conv_w, conv_b, dskip_g, h0f, h0b)


def _out_proj_kernel(yd_ref, z_ref, u_ref, v_ref, x_ref, g1_ref, ng_ref, lng_ref, lnb_ref, ws_ref, bst_ref, w_ref,
                     o_ref, mix_cur, mix_nxt):
    tm = yd_ref.shape[0]
    s = pl.program_id(0)
    n = pl.num_programs(0) - 1

    def build_steps():
        steps = []
        for k in range(tm // CHUNK):
            r = pl.ds(k * CHUNK, CHUNK)
            shared = {}

            def ssd_part(r=r):
                a = yd_ref[r, :].astype(F32) * z_ref[r, :].astype(F32)
                ms = jnp.mean(a * a, axis=-1, keepdims=True)
                mix_nxt[r, 0:SSD_WIDTH] = (a * lax.rsqrt(ms + NORM_EPS) * ng_ref[...]).astype(BF16)

            def ln_part(r=r, shared=shared):
                gv = v_ref[r, :].astype(F32)
                mu = jnp.mean(gv, axis=-1, keepdims=True)
                xc = gv - mu
                var = jnp.mean(xc * xc, axis=-1, keepdims=True)
                shared["ln"] = ((xc * lax.rsqrt(var + NORM_EPS)) * lng_ref[...] + lnb_ref[...]).astype(BF16)

            def gate_part(h0, h1, r=r, shared=shared):
                for h in range(h0, h1):
                    c0, c1 = h * CM_HEAD_DIM, (h + 1) * CM_HEAD_DIM
                    sp = _dot(ws_ref[h], shared["ln"][:, c0:c1]) + bst_ref[:, h:h + 1]
                    gu = u_ref[r, c0:c1].astype(F32)
                    mix_nxt[r, SSD_WIDTH + c0:SSD_WIDTH + c1] = (gu * sp).astype(BF16)

            steps += [ssd_part, ln_part, functools.partial(gate_part, 0, CM_HEADS // 2),
                      functools.partial(gate_part, CM_HEADS // 2, CM_HEADS)]
        return steps

    n_proj = 8
    pw = D_MODEL // n_proj

    def project_steps():
        def piece(q):
            c = slice(q * pw, (q + 1) * pw)
            o_ref[:, c] = x_ref[:, c] + g1_ref[:, c] * _dot(mix_cur[...], w_ref[:, c])

        return [functools.partial(piece, q) for q in range(n_proj)]

    @pl.when(s == 0)
    def _():
        for step in build_steps():
            step()

    @pl.when((s > 0) & (s < n))
    def _():
        for proj, build in zip(project_steps(), build_steps()):
            proj()
            build()

    @pl.when(s == n)
    def _():
        for step in project_steps():
            step()

    @pl.when(s < n)
    def _():
        mix_cur[...] = mix_nxt[...]


def _out_proj(yd, pm, x2, mod3, ssd_norm_g, ln_g, ln_b, ws, bst, w_out, *, seq, tm):
    m = x2.shape[0]
    n = m // tm
    tiles_per_batch = seq // tm
    z_blk, u_blk, v_blk = 0, 2 * SSD_WIDTH // CM_WIDTH, 2 * SSD_WIDTH // CM_WIDTH + 1
    g1_blk = 2
    row = lambda shape: pl.BlockSpec(shape, lambda s: (0, 0))
    build_tile = lambda s: jnp.minimum(s, n - 1)
    proj_tile = lambda s: jnp.maximum(s - 1, 0)
    return pl.pallas_call(
        _out_proj_kernel,
        grid=(n + 1,),
        in_specs=[
            pl.BlockSpec((tm, SSD_WIDTH), lambda s: (build_tile(s), 0)),
            pl.BlockSpec((tm, SSD_WIDTH), lambda s: (build_tile(s), z_blk)),
            pl.BlockSpec((tm, CM_WIDTH), lambda s: (build_tile(s), u_blk)),
            pl.BlockSpec((tm, CM_WIDTH), lambda s: (build_tile(s), v_blk)),
            pl.BlockSpec((tm, D_MODEL), lambda s: (proj_tile(s), 0)),
            pl.BlockSpec((None, 1, D_MODEL), lambda s: (proj_tile(s) // tiles_per_batch, 0, g1_blk)),
            row((1, SSD_WIDTH)),
            row((1, CM_WIDTH)),
            row((1, CM_WIDTH)),
            pl.BlockSpec((CM_HEADS, CHUNK, CHUNK), lambda s: (0, 0, 0)),
            row((CHUNK, CM_HEADS)),
            pl.BlockSpec((SSD_WIDTH + CM_WIDTH, D_MODEL), lambda s: (0, 0), pipeline_mode=pl.Buffered(1)),
        ],
        out_specs=pl.BlockSpec((tm, D_MODEL), lambda s: (proj_tile(s), 0)),
        out_shape=jax.ShapeDtypeStruct((m, D_MODEL), F32),
        scratch_shapes=[pltpu.VMEM((tm, SSD_WIDTH + CM_WIDTH), BF16)] * 2,
        compiler_params=_params("arbitrary"),
        name="out_proj",
    )(yd, pm, pm, pm, x2, mod3, ssd_norm_g, ln_g, ln_b, ws, bst, w_out)


def _route(logits):
    lane = lax.broadcasted_iota(jnp.int32, logits.shape, 1)
    lane_f = lane.astype(F32)
    is_group = (lane >= N_EXPERTS) & (lane < N_EXPERTS + N_GROUPS)
    lg = jnp.where(is_group, logits, NEG_BIG)
    mg = jnp.max(lg, axis=1, keepdims=True)
    top_pg = 1.0 / jnp.sum(jnp.exp(lg - mg), axis=1, keepdims=True)
    gi = jnp.min(jnp.where(lg == mg, lane_f, 1e9), axis=1, keepdims=True) - N_EXPERTS
    in_group = (lane < N_EXPERTS) & ((lane // EXPERTS_PER_GROUP).astype(F32) == gi)
    le = jnp.where(in_group, logits, NEG_BIG)
    m1 = jnp.max(le, axis=1, keepdims=True)
    i1 = jnp.min(jnp.where(le == m1, lane_f, 1e9), axis=1, keepdims=True)
    le2 = jnp.where(lane_f == i1, NEG_BIG, le)
    m2 = jnp.max(le2, axis=1, keepdims=True)
    i2 = jnp.min(jnp.where(le2 == m2, lane_f, 1e9), axis=1, keepdims=True)
    e2 = jnp.exp(m2 - m1)
    p1 = 1.0 / (1.0 + e2)
    p2 = e2 * p1
    return i1, i2, p1 * top_pg, p2 * top_pg


RINFO_E1, RINFO_E2, RINFO_W1, RINFO_W2, RINFO_R1, RINFO_R2 = range(6)


def _route_kernel(x1_ref, n2g_ref, sh2_ref, sc2_ref, wr_ref, br_ref, h2_ref, rinfo_ref, counts_ref, cnt_scr):
    tm = x1_ref.shape[0]

    @pl.when(pl.program_id(0) == 0)
    def _():
        cnt_scr[...] = jnp.zeros_like(cnt_scr)

    x1 = x1_ref[...]
    ms = jnp.mean(x1 * x1, axis=-1, keepdims=True)
    h2 = (x1 * lax.rsqrt(ms + NORM_EPS) * n2g_ref[...]) * (1.0 + sc2_ref[...]) + sh2_ref[...]
    h2_ref[...] = h2
    hi, mid, _ = _split3(h2)
    p = _dot(hi, wr_ref[...])
    logits = p[:, :LANES] + p[:, LANES:] + _dot(mid, wr_ref[:, :LANES]) + br_ref[...]
    i1, i2, w1, w2 = _route(logits)

    lane = lax.broadcasted_iota(jnp.int32, (tm, LANES), 1)
    lane_f = lane.astype(F32)
    oh1 = jnp.where(lane_f == i1, 1.0, 0.0)
    oh2 = jnp.where(lane_f == i2, 1.0, 0.0)
    before = (lax.broadcasted_iota(jnp.int32, (tm, tm), 0) > lax.broadcasted_iota(jnp.int32, (tm, tm), 1)).astype(BF16)
    carried = cnt_scr[...]
    tot1 = jnp.sum(oh1, axis=0, keepdims=True)
    r1 = jnp.sum(oh1 * (_dot(before, oh1.astype(BF16)) + carried), axis=1, keepdims=True)
    r2 = jnp.sum(oh2 * (_dot(before, oh2.astype(BF16)) + (carried + tot1)), axis=1, keepdims=True)
    counts = carried + tot1 + jnp.sum(oh2, axis=0, keepdims=True)
    cnt_scr[...] = counts
    counts_ref[...] = jnp.broadcast_to(counts, counts_ref.shape)

    info = jnp.zeros((tm, LANES), F32)
    for k, v in ((RINFO_E1, i1), (RINFO_E2, i2), (RINFO_W1, w1), (RINFO_W2, w2), (RINFO_R1, r1), (RINFO_R2, r2)):
        info = jnp.where(lane == k, v, info)
    rinfo_ref[...] = info


def _route_call(x1, mod3, norm2_g, w_router, b_router, *, seq, tm):
    m = x1.shape[0]
    tiles_per_batch = seq // tm
    row = lambda shape: pl.BlockSpec(shape, lambda i: (0, 0))
    modrow = lambda k: pl.BlockSpec((None, 1, D_MODEL), lambda i: (i // tiles_per_batch, 0, k))
    return pl.pallas_call(
        _route_kernel,
        grid=(m // tm,),
        in_specs=[
            pl.BlockSpec((tm, D_MODEL), lambda i: (i, 0)),
            row((1, D_MODEL)),
            modrow(3),
            modrow(4),
            row((D_MODEL, 2 * LANES)),
            row((1, LANES)),
        ],
        out_specs=[
            pl.BlockSpec((tm, D_MODEL), lambda i: (i, 0)),
            pl.BlockSpec((tm, LANES), lambda i: (i, 0)),
            row((8, LANES)),
        ],
        out_shape=[
            jax.ShapeDtypeStruct((m, D_MODEL), F32),
            jax.ShapeDtypeStruct((m, LANES), F32),
            jax.ShapeDtypeStruct((8, LANES), F32),
        ],
        scratch_shapes=[pltpu.VMEM((1, LANES), F32)],
        compiler_params=_params("arbitrary"),
        name="route",
    )(x1, norm2_g, mod3, mod3, w_router, b_router)


def _dispatch_kernel(lt_ref, nu_ref, pos_ref, h2_ref, x_hbm, zbuf, zsem, sem, *, tile_rows, n_tiles):
    tm = h2_ref.shape[0]
    n_used = nu_ref[0]

    def zero_copy(t):
        return pltpu.make_async_copy(zbuf, x_hbm.at[pl.ds(pl.multiple_of(t * tile_rows, tile_rows), tile_rows), :], zsem)

    @pl.when(pl.program_id(0) == 0)
    def _():
        zbuf[...] = jnp.zeros_like(zbuf)
        for phase in ("start", "wait"):
            for e in range(N_EXPERTS):
                @pl.when(lt_ref[e] >= 0)
                def _():
                    getattr(zero_copy(lt_ref[e]), phase)()

            def unused(t, carry):
                getattr(zero_copy(t), phase)()
                return carry

            lax.fori_loop(n_used, n_tiles, unused, 0)

    def body(r, carry):
        for k in range(2):
            pltpu.make_async_copy(h2_ref.at[pl.ds(r, 1), :], x_hbm.at[pl.ds(pos_ref[0, 2 * r + k], 1), :],
                                  sem).start(priority=k)
        return carry

    lax.fori_loop(0, tm, body, 0, unroll=8)
    for k in range(2):
        pltpu.make_async_copy(h2_ref, x_hbm.at[pl.ds(0, tm), :], sem).wait()


def _dispatch_call(last_tile, n_used, pos, h2, *, tm, tile_rows, n_tiles):
    m = h2.shape[0]
    grid_spec = pltpu.PrefetchScalarGridSpec(
        num_scalar_prefetch=2,
        grid=(m // tm,),
        in_specs=[
            pl.BlockSpec((None, 1, 2 * tm), lambda i, lt, nu: (i, 0, 0), memory_space=pltpu.SMEM),
            pl.BlockSpec((tm, D_MODEL), lambda i, lt, nu: (i, 0)),
        ],
        out_specs=pl.BlockSpec(memory_space=pl.ANY),
        scratch_shapes=[
            pltpu.VMEM((tile_rows, D_MODEL), F32),
            pltpu.SemaphoreType.DMA(()),
            pltpu.SemaphoreType.DMA(()),
        ],
    )
    return pl.pallas_call(
        functools.partial(_dispatch_kernel, tile_rows=tile_rows, n_tiles=n_tiles),
        grid_spec=grid_spec,
        out_shape=jax.ShapeDtypeStruct((n_tiles * tile_rows, D_MODEL), F32),
        compiler_params=_params("arbitrary"),
        name="dispatch",
    )(last_tile, n_used, pos, h2)


def _experts_kernel(te_ref, nu_ref, ord_ref, nxt_ref, x_ref, wg_hbm, wu_hbm, wd_hbm, y_ref,
                    wg_f, wu_f, wd_f, wg_b, wu_b, wd_b, sem):
    j = pl.program_id(0)
    n_used = nu_ref[0]

    def weight_copies(e, slot):
        return [pltpu.make_async_copy(src.at[e], dst.at[slot], sem.at[slot])
                for src, dst in ((wg_hbm, wg_f), (wu_hbm, wu_f), (wd_hbm, wd_f))]

    @pl.when(j == 0)
    def _():
        for cp in weight_copies(te_ref[0], 0):
            cp.start()

    @pl.when((j < n_used) & ((j == 0) | (te_ref[j] != te_ref[jnp.maximum(j - 1, 0)])))
    def _():
        slot = ord_ref[j] % 2
        for cp in weight_copies(te_ref[j], slot):
            cp.wait()

        @pl.when(nxt_ref[j] >= 0)
        def _():
            for cp in weight_copies(nxt_ref[j], 1 - slot):
                cp.start()

        wg_b[...] = wg_f[slot].astype(BF16)
        wu_b[...] = wu_f[slot].astype(BF16)
        wd_b[...] = wd_f[slot].astype(BF16)

    @pl.when(j < n_used)
    def _():
        xt = x_ref[...].astype(BF16)
        hid = (_silu(_dot(xt, wg_b[...])) * _dot(xt, wu_b[...])).astype(BF16)
        y_ref[...] = _dot(hid, wd_b[...])

    @pl.when(j >= n_used)
    def _():
        y_ref[...] = jnp.zeros_like(y_ref)


def _experts_call(tile_expert, n_used, tile_ord, next_expert, x_sorted, wg, wu, wd, *, tm):
    n_tiles = tile_expert.shape[0]
    hbm = pl.BlockSpec(memory_space=pl.ANY)
    grid_spec = pltpu.PrefetchScalarGridSpec(
        num_scalar_prefetch=4,
        grid=(n_tiles,),
        in_specs=[
            pl.BlockSpec((tm, D_MODEL), lambda j, te, nu, od, nx: (jnp.minimum(j, nu[0] - 1), 0)),
            hbm,
            hbm,
            hbm,
        ],
        out_specs=pl.BlockSpec((tm, D_MODEL), lambda j, te, nu, od, nx: (j, 0)),
        scratch_shapes=[
            pltpu.VMEM((2, D_MODEL, EXPERT_FF), F32),
            pltpu.VMEM((2, D_MODEL, EXPERT_FF), F32),
            pltpu.VMEM((2, EXPERT_FF, D_MODEL), F32),
            pltpu.VMEM((D_MODEL, EXPERT_FF), BF16),
            pltpu.VMEM((D_MODEL, EXPERT_FF), BF16),
            pltpu.VMEM((EXPERT_FF, D_MODEL), BF16),
            pltpu.SemaphoreType.DMA((2,)),
        ],
    )
    return pl.pallas_call(
        _experts_kernel,
        grid_spec=grid_spec,
        out_shape=jax.ShapeDtypeStruct((n_tiles * tm, D_MODEL), F32),
        compiler_params=_params("arbitrary"),
        name="experts",
    )(tile_expert, n_used, tile_ord, next_expert, x_sorted, wg, wu, wd)


def _combine_kernel(pos_cur, pos_nxt, x1_ref, g2_ref, nfg_ref, rinfo_ref, y_hbm, o_ref, ybuf, sem):
    i = pl.program_id(0)
    tm = x1_ref.shape[0]
    slot = i % 2

    def row_copy(pos_ref, r, k, s):
        return pltpu.make_async_copy(y_hbm.at[pl.ds(pos_ref[0, 2 * r + k], 1), :], ybuf.at[s, k, pl.ds(r, 1), :],
                                     sem.at[s])

    def wait_tile(s):
        for k in range(2):
            pltpu.make_async_copy(y_hbm.at[pl.ds(0, tm), :], ybuf.at[s, k], sem.at[s]).wait()

    @pl.when(i == 0)
    def _():
        def body(r, carry):
            for k in range(2):
                row_copy(pos_cur, r, k, 0).start()
            return carry

        lax.fori_loop(0, tm, body, 0)

    wait_tile(slot)
    for r in range(tm):
        for k in range(2):
            row_copy(pos_nxt, r, k, 1 - slot).start(priority=k)
    info = rinfo_ref[...]
    w1 = info[:, RINFO_W1:RINFO_W1 + 1]
    w2 = info[:, RINFO_W2:RINFO_W2 + 1]
    moe = w1 * ybuf[slot, 0] + w2 * ybuf[slot, 1]
    y = x1_ref[...] + g2_ref[...] * moe
    ms = jnp.mean(y * y, axis=-1, keepdims=True)
    o_ref[...] = y * lax.rsqrt(ms + NORM_EPS) * nfg_ref[...]

    @pl.when(i == pl.num_programs(0) - 1)
    def _():
        wait_tile(1 - slot)


def _combine_call(pos, x1, mod3, normf_g, rinfo, y_sorted, *, seq, tm):
    m = x1.shape[0]
    n_tiles = m // tm
    tiles_per_batch = seq // tm
    return pl.pallas_call(
        _combine_kernel,
        grid=(n_tiles,),
        in_specs=[
            pl.BlockSpec((None, 1, 2 * tm), lambda i: (i, 0, 0), memory_space=pltpu.SMEM),
            pl.BlockSpec((None, 1, 2 * tm), lambda i: (jnp.minimum(i + 1, n_tiles - 1), 0, 0),
                         memory_space=pltpu.SMEM),
            pl.BlockSpec((tm, D_MODEL), lambda i: (i, 0)),
            pl.BlockSpec((None, 1, D_MODEL), lambda i: (i // tiles_per_batch, 0, 5)),
            pl.BlockSpec((1, D_MODEL), lambda i: (0, 0)),
            pl.BlockSpec((tm, LANES), lambda i: (i, 0)),
            pl.BlockSpec(memory_space=pl.ANY),
        ],
        out_specs=pl.BlockSpec((tm, D_MODEL), lambda i: (i, 0)),
        out_shape=jax.ShapeDtypeStruct((m, D_MODEL), F32),
        scratch_shapes=[
            pltpu.VMEM((2, 2, tm, D_MODEL), F32),
            pltpu.SemaphoreType.DMA((2,)),
        ],
        compiler_params=_params("arbitrary"),
        name="combine",
    )(pos, pos, x1, mod3, normf_g, rinfo, y_sorted)


def _dispatch_plan(rinfo, counts, *, tm):
    n_tok = rinfo.shape[0]
    n_tiles = (2 * n_tok + N_EXPERTS * (tm - 1)) // tm + 1
    expert = rinfo[:, RINFO_E1:RINFO_E2 + 1].astype(jnp.int32)
    rank = rinfo[:, RINFO_R1:RINFO_R2 + 1].astype(jnp.int32)
    cnt = counts[0, :N_EXPERTS].astype(jnp.int32)
    tiles_e = (cnt + tm - 1) // tm
    end_tile = jnp.cumsum(tiles_e)
    start_row = (end_tile - tiles_e) * tm
    is_e = expert[:, :, None] == jnp.arange(N_EXPERTS, dtype=jnp.int32)
    pos = jnp.sum(jnp.where(is_e, start_row, 0), axis=-1) + rank
    n_used = end_tile[-1:]
    tile_ids = jnp.minimum(jnp.arange(n_tiles, dtype=jnp.int32), n_used - 1)
    tile_expert = jnp.sum(tile_ids[:, None] >= end_tile[None, :], axis=1).astype(jnp.int32)
    last_tile = jnp.where(tiles_e > 0, end_tile - 1, -1).astype(jnp.int32)
    used = (tiles_e > 0).astype(jnp.int32)
    ord_e = jnp.cumsum(used) - used
    ids = jnp.arange(N_EXPERTS, dtype=jnp.int32)
    later = jnp.where((ids[None, :] > ids[:, None]) & (used[None, :] > 0), ids[None, :], N_EXPERTS)
    nxt_e = jnp.min(later, axis=1)
    nxt_e = jnp.where(nxt_e < N_EXPERTS, nxt_e, -1).astype(jnp.int32)
    is_te = tile_expert[:, None] == ids[None, :]
    tile_ord = jnp.sum(jnp.where(is_te, ord_e, 0), axis=1).astype(jnp.int32)
    next_expert = jnp.sum(jnp.where(is_te, nxt_e, 0), axis=1).astype(jnp.int32)
    return pos, (tile_expert, n_used.astype(jnp.int32), tile_ord, next_expert), last_tile, n_tiles


def kernel(x, c, ctx, c_ctx, w_mod, b_mod, norm1_g, w_in, conv_w, conv_b, dt_bias_f, dt_bias_b, a_log_f, a_log_b,
           d_skip, ssd_norm_g, cm_ln_g, cm_ln_b, w_spatial, b_spatial, w_out, norm2_g, w_router_group,
           b_router_group, w_router_expert, b_router_expert, w_exp_gate, w_exp_up, w_exp_down, normf_g):
    bsz, seq, _ = x.shape
    ctx_len = ctx.shape[1]
    i = 0

    cc = jnp.concatenate([c, c_ctx[None, :], jnp.zeros((MOD_ROWS - bsz - 1, D_MODEL), F32)], axis=0)
    mod = _modulation(cc, w_mod[i], b_mod[i][None, :])
    mod3 = mod.reshape(MOD_ROWS, 1, N_MOD * D_MODEL)

    w_main, w_dt = _w_in_prep(jnp.swapaxes(w_in[i], 0, 1))
    g1row = norm1_g[i][None, :]

    x2 = x.reshape(bsz * seq, D_MODEL)
    ctx2 = ctx.reshape(bsz * ctx_len, D_MODEL)
    tm_in = 1024
    tn_in = 1024
    xs_col0, bc_col0 = SSD_WIDTH, 2 * SSD_WIDTH + 2 * CM_WIDTH
    pm_x, dt_x = _in_proj(x2, mod3, lambda t: t // (seq // tm_in), g1row, w_main, w_dt, tm_in, tn_in)
    ctx_blocks = tuple(range(xs_col0 // tn_in, 2 * SSD_WIDTH // tn_in)) + (bc_col0 // tn_in,)
    pm_c, dt_c = _in_proj(ctx2, mod3, lambda t: bsz, g1row, w_main, w_dt, bsz * ctx_len, tn_in, ctx_blocks)

    def dt_lanes(fwd, bwd):
        both = jnp.stack([fwd.reshape(SSD_GROUPS, HEADS_PER_GROUP), bwd.reshape(SSD_GROUPS, HEADS_PER_GROUP)], axis=1)
        return jnp.pad(both.reshape(-1).astype(F32), (0, LANES - 2 * SSD_HEADS))[None, :]

    dt_bias_row = dt_lanes(dt_bias_f[i], dt_bias_b[i])
    a_row = dt_lanes(-jnp.exp(a_log_f[i].astype(F32)), -jnp.exp(a_log_b[i].astype(F32)))
    dskip_g = jnp.repeat(d_skip[i], SSD_HEAD_DIM).reshape(SSD_GROUPS, 1, GROUP_WIDTH)
    cw = conv_w[i]
    cb = conv_b[i][None, :]
    h_zero = jnp.zeros((bsz, SSD_GROUPS, SSD_STATE, GROUP_WIDTH), F32)

    dt_terms_c = _ssd_dt(dt_c, dt_bias_row, a_row, rows_per_step=bsz * ctx_len)
    dt_terms_x = _ssd_dt(dt_x, dt_bias_row, a_row, rows_per_step=1024)
    hc_f, hc_b = _ssd(pm_c, 0, SSD_WIDTH, dt_terms_c, cw, cb, dskip_g, h_zero, h_zero,
                      batch=bsz, seq=ctx_len, conv_tile=ctx_len, conv_period=ctx_len, with_output=False)
    yd = _ssd(pm_x, xs_col0, bc_col0, dt_terms_x, cw, cb, dskip_g, hc_f, hc_b,
              batch=bsz, seq=seq, conv_tile=CHUNK, conv_period=GRID_W, with_output=True)

    x1 = _out_proj(yd, pm_x, x2, mod3, ssd_norm_g[i][None, :], cm_ln_g[i][None, :], cm_ln_b[i][None, :],
                   w_spatial[i].astype(BF16), b_spatial[i].T, w_out[i].astype(BF16), seq=seq, tm=256)

    w_re = jnp.transpose(w_router_expert[i], (1, 0, 2)).reshape(D_MODEL, N_EXPERTS)
    pad = LANES - N_EXPERTS - N_GROUPS
    w_router = jnp.pad(jnp.concatenate([w_re, w_router_group[i]], axis=1), ((0, 0), (0, pad)))
    w_router_hi = w_router.astype(BF16)
    w_router = jnp.concatenate([w_router_hi, (w_router - w_router_hi.astype(F32)).astype(BF16)], axis=1)
    b_router = jnp.pad(jnp.concatenate([b_router_expert[i].reshape(-1), b_router_group[i]]), (0, pad))[None, :]
    h2, rinfo, counts = _route_call(x1, mod3, norm2_g[i][None, :], w_router, b_router, seq=seq, tm=512)

    tm_e = 256
    pos, tile_tables, last_tile, n_tiles = _dispatch_plan(rinfo, counts, tm=tm_e)
    tm_d = 512
    x_sorted = _dispatch_call(last_tile, tile_tables[1], pos.reshape(bsz * seq // tm_d, 1, 2 * tm_d), h2,
                              tm=tm_d, tile_rows=tm_e, n_tiles=n_tiles)
    wg = w_exp_gate[i].reshape(N_EXPERTS, D_MODEL, EXPERT_FF)
    wu = w_exp_up[i].reshape(N_EXPERTS, D_MODEL, EXPERT_FF)
    wd = w_exp_down[i].reshape(N_EXPERTS, EXPERT_FF, D_MODEL)
    y_sorted = _experts_call(*tile_tables, x_sorted, wg, wu, wd, tm=tm_e)
    tm_c = 256
    out = _combine_call(pos.reshape(bsz * seq // tm_c, 1, 2 * tm_c), x1, mod3, normf_g[None, :], rinfo, y_sorted,
                        seq=seq, tm=tm_c)
    return out.reshape(bsz, seq, D_MODEL)
```

```python
import functools

import jax
import jax.numpy as jnp
from jax import lax
from jax.experimental import pallas as pl
from jax.experimental.pallas import tpu as pltpu

F32 = jnp.float32
BF16 = jnp.bfloat16
HIGHEST = lax.Precision.HIGHEST

D_MODEL = 2048
GRID_W = 64
SSD_WIDTH = 2048
CM_WIDTH = 2048
SSD_HEADS = 32
SSD_HEAD_DIM = 64
SSD_GROUPS = 4
HEADS_PER_GROUP = SSD_HEADS // SSD_GROUPS
GROUP_WIDTH = HEADS_PER_GROUP * SSD_HEAD_DIM
SSD_STATE = 128
CHUNK = 128
BC_WIDTH = SSD_GROUPS * SSD_STATE
CM_HEADS = 8
CM_HEAD_DIM = CM_WIDTH // CM_HEADS
N_GROUPS = 4
N_EXPERTS = 32
EXPERTS_PER_GROUP = 8
EXPERT_FF = 512
N_MOD = 6
NORM_EPS = 1e-6
LANES = 128
NEG_BIG = -1e30
MOD_ROWS = 8
VMEM_LIMIT = 56 * 1024 * 1024


def _params(*sem):
    return pltpu.CompilerParams(dimension_semantics=sem, vmem_limit_bytes=VMEM_LIMIT)


LOG2_E = 1.4426950408889634
GELU_C = 0.7978845608028654
GELU_K1 = -2.0 * GELU_C * LOG2_E
GELU_K3 = GELU_K1 * 0.044715


def _silu(v):
    return v / (1.0 + jnp.exp2(v * (-LOG2_E)))


def _gelu_tanh(v):
    return v / (1.0 + jnp.exp2(v * (GELU_K1 + GELU_K3 * (v * v))))


def _dot(a, b):
    return jnp.dot(a, b, preferred_element_type=F32)


def _dot_nt(a, b):
    return lax.dot_general(a, b, (((1,), (1,)), ((), ())), preferred_element_type=F32)


def _mod_kernel(cc_ref, w_ref, b_ref, o_ref):
    a = _silu(cc_ref[...])
    hi = a.astype(BF16).astype(F32)
    lhs = jnp.concatenate([hi, a - hi], axis=0).astype(BF16)
    r = _dot(lhs, w_ref[...].astype(BF16))
    o_ref[...] = r[:MOD_ROWS] + r[MOD_ROWS:] + b_ref[...]


def _modulation(cc, w_mod, b_mod):
    n = w_mod.shape[1]
    tn = 1024
    return pl.pallas_call(
        _mod_kernel,
        grid=(n // tn,),
        in_specs=[
            pl.BlockSpec((MOD_ROWS, D_MODEL), lambda j: (0, 0)),
            pl.BlockSpec((D_MODEL, tn), lambda j: (0, j)),
            pl.BlockSpec((1, tn), lambda j: (0, j)),
        ],
        out_specs=pl.BlockSpec((MOD_ROWS, tn), lambda j: (0, j)),
        out_shape=jax.ShapeDtypeStruct((MOD_ROWS, n), F32),
        compiler_params=_params("arbitrary"),
        name="modulation",
    )(cc, w_mod, b_mod)


PREP_ROWS = 512
PREP_UV_BLK0 = 2 * SSD_WIDTH // PREP_ROWS
PREP_BC_BLK0 = (2 * SSD_WIDTH + 2 * CM_WIDTH) // PREP_ROWS
PREP_N_BLKS = PREP_BC_BLK0 + 2 * BC_WIDTH // PREP_ROWS
DT_COL0 = 2 * SSD_WIDTH + 2 * BC_WIDTH
UV_COL0 = DT_COL0 + 2 * SSD_HEADS


def _w_in_prep_kernel(a_ref, dt_ref, o_ref, wdt_ref):
    @pl.when(pl.program_id(0) == 0)
    def _():
        hpg = HEADS_PER_GROUP
        parts = []
        for g in range(SSD_GROUPS):
            parts += [dt_ref[hpg * g:hpg * (g + 1), :], dt_ref[SSD_HEADS + hpg * g:SSD_HEADS + hpg * (g + 1), :]]
        parts.append(dt_ref[2 * SSD_HEADS:, :])
        wdt_ref[...] = jnp.concatenate(parts, axis=0).astype(BF16)

    o_ref[...] = a_ref[...].astype(BF16)


def _w_in_prep(w_in_t):
    def src_row(j):
        uv = UV_COL0 + (j - PREP_UV_BLK0) * PREP_ROWS
        bc = 2 * SSD_WIDTH + (j - PREP_BC_BLK0) * PREP_ROWS
        return pl.multiple_of(jnp.where(j < PREP_UV_BLK0, j * PREP_ROWS, jnp.where(j < PREP_BC_BLK0, uv, bc)), 8)

    return pl.pallas_call(
        _w_in_prep_kernel,
        grid=(PREP_N_BLKS,),
        in_specs=[
            pl.BlockSpec((pl.Element(PREP_ROWS), pl.Element(D_MODEL)), lambda j: (src_row(j), 0)),
            pl.BlockSpec((LANES, D_MODEL), lambda j: (DT_COL0 // LANES, 0)),
        ],
        out_specs=[
            pl.BlockSpec((PREP_ROWS, D_MODEL), lambda j: (j, 0)),
            pl.BlockSpec((LANES, D_MODEL), lambda j: (0, 0)),
        ],
        out_shape=[
            jax.ShapeDtypeStruct((PREP_N_BLKS * PREP_ROWS, D_MODEL), BF16),
            jax.ShapeDtypeStruct((LANES, D_MODEL), BF16),
        ],
        compiler_params=_params("arbitrary"),
        name="w_in_prep",
    )(w_in_t, w_in_t)


ACT_NONE, ACT_SILU, ACT_GELU = 0, 1, 2


def _in_proj_kernel(x_ref, g_ref, sh_ref, sc_ref, w_ref, wdt_ref, o_ref, dt_ref, h_scr, *, block_acts):
    tm = x_ref.shape[0]
    rc = 256

    @pl.when(pl.program_id(1) == 0)
    def _():
        gain = g_ref[...]
        scale = 1.0 + sc_ref[...]
        shift = sh_ref[...]

        def body(k, carry):
            r = pl.ds(pl.multiple_of(k * rc, rc), rc)
            xv = x_ref[r, :]
            ms = jnp.mean(xv * xv, axis=-1, keepdims=True)
            h = (xv * lax.rsqrt(ms + NORM_EPS) * gain) * scale + shift
            h_scr[r, :] = h.astype(BF16)
            return carry

        lax.fori_loop(0, tm // rc, body, 0)
        dt_ref[...] = _dot_nt(h_scr[...], wdt_ref[...])

    j = pl.program_id(1)
    for act, fn in ((ACT_NONE, lambda v: v), (ACT_SILU, _silu), (ACT_GELU, _gelu_tanh)):
        blocks = [q for q, a in enumerate(block_acts) if a == act]
        if blocks:
            is_act = functools.reduce(jnp.logical_or, [j == q for q in blocks])

            @pl.when(is_act)
            def _():
                o_ref[...] = fn(_dot_nt(h_scr[...], w_ref[...])).astype(o_ref.dtype)


def _in_proj(x2, mod3, mod_row_of_tile, norm_g, w_main, w_dt, tm, tn, w_blocks=None):
    m = x2.shape[0]
    if w_blocks is None:
        w_blocks = tuple(range(w_main.shape[0] // tn))
    n = len(w_blocks) * tn
    z_end, uv0, uv_end = SSD_WIDTH // tn, 2 * SSD_WIDTH // tn, (2 * SSD_WIDTH + 2 * CM_WIDTH) // tn
    block_acts = tuple(ACT_SILU if b < z_end else ACT_GELU if uv0 <= b < uv_end else ACT_NONE for b in w_blocks)

    def w_blk(j):
        if w_blocks == tuple(range(len(w_blocks))):
            return j
        blk = jnp.int32(w_blocks[-1])
        for q in range(len(w_blocks) - 2, -1, -1):
            blk = jnp.where(j == q, w_blocks[q], blk)
        return blk

    return pl.pallas_call(
        functools.partial(_in_proj_kernel, block_acts=block_acts),
        grid=(m // tm, n // tn),
        in_specs=[
            pl.BlockSpec((tm, D_MODEL), lambda i, j: (i, 0)),
            pl.BlockSpec((1, D_MODEL), lambda i, j: (0, 0)),
            pl.BlockSpec((None, 1, D_MODEL), lambda i, j: (mod_row_of_tile(i), 0, 0)),
            pl.BlockSpec((None, 1, D_MODEL), lambda i, j: (mod_row_of_tile(i), 0, 1)),
            pl.BlockSpec((tn, D_MODEL), lambda i, j: (w_blk(j), 0)),
            pl.BlockSpec((LANES, D_MODEL), lambda i, j: (0, 0)),
        ],
        out_specs=[
            pl.BlockSpec((tm, tn), lambda i, j: (i, j)),
            pl.BlockSpec((tm, LANES), lambda i, j: (i, 0)),
        ],
        out_shape=[
            jax.ShapeDtypeStruct((m, n), BF16),
            jax.ShapeDtypeStruct((m, LANES), F32),
        ],
        scratch_shapes=[pltpu.VMEM((tm, D_MODEL), BF16)],
        compiler_params=_params("parallel", "arbitrary"),
        name="in_proj",
    )(x2, norm_g, mod3, mod3, w_main, w_dt)


def _split3(v):
    hi = v.astype(BF16)
    rem = v - hi.astype(F32)
    mid = rem.astype(BF16)
    lo = (rem - mid.astype(F32)).astype(BF16)
    return hi, mid, lo


GROUP_LANES = 2 * HEADS_PER_GROUP


def _stack_split3(v):
    hi, mid, lo = _split3(v)
    stacked = (hi.astype(F32) + pltpu.roll(mid.astype(F32), GROUP_LANES, 1)
               + pltpu.roll(lo.astype(F32), 2 * GROUP_LANES, 1))
    return stacked.astype(BF16)


def _ssd_dt_kernel(dt_ref, bias_ref, a_ref, dts_ref, cum_ref, src_t_ref):
    hpg = HEADS_PER_GROUP
    ii = lax.broadcasted_iota(jnp.int32, (CHUNK, CHUNK), 0)
    jj = lax.broadcasted_iota(jnp.int32, (CHUNK, CHUNK), 1)
    tri = (ii >= jj).astype(BF16)
    tri3 = jnp.concatenate([tri, tri, tri], axis=1)
    bias = bias_ref[...]
    a_row = a_ref[...]

    def body(k, carry):
        r = pl.ds(pl.multiple_of(k * CHUNK, CHUNK), CHUNK)
        raw = dt_ref[r, :] + bias
        dts = jnp.where(jj < 2 * SSD_HEADS, jnp.maximum(raw, 0.0) + jnp.log1p(jnp.exp(-jnp.abs(raw))), 0.0)
        adt = dts * a_row
        hi, mid, lo = _split3(adt)
        cf = _dot(tri3, jnp.concatenate([hi, mid, lo], axis=0))
        cr = cf[CHUNK - 1:CHUNK, :] - cf + adt
        valid = jj < 2 * SSD_HEADS
        cum = jnp.where(jj % GROUP_LANES < hpg, cf, cr)
        src_t = ((cum - jnp.log(jnp.where(valid, dts, 1.0))) * LOG2_E).T
        rt = pl.ds(pl.multiple_of(k * GROUP_LANES, GROUP_LANES), GROUP_LANES)
        for g in range(SSD_GROUPS):
            g0 = g * GROUP_LANES
            for src, dst in ((cum * LOG2_E, cum_ref), (dts, dts_ref)):
                lanes = src if g == 0 else pltpu.roll(src, LANES - g0, 1)
                dst[g, r, :] = jnp.where(jj < GROUP_LANES, lanes, 0.0)
            src_t_ref[g, rt, :] = src_t[g0:g0 + GROUP_LANES, :]
        return carry

    lax.fori_loop(0, dt_ref.shape[0] // CHUNK, body, 0)


def _ssd_dt(dt, bias_row, a_row, *, rows_per_step):
    m = dt.shape[0]
    row = pl.BlockSpec((1, LANES), lambda i: (0, 0))
    full = pl.BlockSpec((SSD_GROUPS, rows_per_step, LANES), lambda i: (0, i, 0))
    tr = pl.BlockSpec((SSD_GROUPS, rows_per_step // CHUNK * GROUP_LANES, LANES), lambda i: (0, i, 0))
    full_shape = jax.ShapeDtypeStruct((SSD_GROUPS, m, LANES), F32)
    tr_shape = jax.ShapeDtypeStruct((SSD_GROUPS, m // CHUNK * GROUP_LANES, LANES), F32)
    return pl.pallas_call(
        _ssd_dt_kernel,
        grid=(m // rows_per_step,),
        in_specs=[pl.BlockSpec((rows_per_step, LANES), lambda i: (i, 0)), row, row],
        out_specs=[full, full, tr],
        out_shape=[full_shape, full_shape, tr_shape],
        compiler_params=_params("parallel"),
        name="ssd_dt",
    )(dt, bias_row, a_row)


def _ssd_kernel(xs_ref, b_ref, c_ref, dts_ref, cum_ref, src_t_ref, cwx_ref, cbx_ref, cwb_ref, cbb_ref,
                cwc_ref, cbc_ref, dskip_ref, h0f_ref, h0b_ref, *rest,
                n_chunks, conv_tile, conv_period, with_output):
    if with_output:
        y_ref, xs_s, b_s, c_s, hf_s, hb_s, y_s, yb_s = rest
    else:
        hf_out, hb_out, xs_s, b_s, c_s, hf_s, hb_s = rest
    seq = n_chunks * CHUNK
    hpg = HEADS_PER_GROUP

    def conv_silu(src_ref, dst_ref, w_ref, bias_ref):
        cols = src_ref.shape[1]
        w1 = w_ref[1:2, :]
        bias = bias_ref[...]
        row = lax.broadcasted_iota(jnp.int32, (conv_tile, cols), 0) % conv_period
        w0 = jnp.where(row == 0, 0.0, w_ref[0:1, :])
        w2 = jnp.where(row == conv_period - 1, 0.0, w_ref[2:3, :])

        def body(k, carry):
            r = pl.ds(pl.multiple_of(k * conv_tile, conv_tile), conv_tile)
            v = src_ref[r, :].astype(F32)
            y = bias + pltpu.roll(v, 1, 0) * w0 + v * w1 + pltpu.roll(v, conv_tile - 1, 0) * w2
            dst_ref[r, :] = _silu(y).astype(BF16)
            return carry

        lax.fori_loop(0, seq // conv_tile, body, 0)

    conv_silu(xs_ref, xs_s, cwx_ref, cbx_ref)
    conv_silu(b_ref, b_s, cwb_ref, cbb_ref)
    conv_silu(c_ref, c_s, cwc_ref, cbc_ref)

    ii = lax.broadcasted_iota(jnp.int32, (CHUNK, CHUNK), 0)
    jj = lax.broadcasted_iota(jnp.int32, (CHUNK, CHUNK), 1)
    lower = ii >= jj
    upper = jj >= ii

    hf_s[...] = h0f_ref[...]
    hb_s[...] = h0b_ref[...]

    head_of_col = lax.broadcasted_iota(jnp.int32, (LANES, GROUP_WIDTH), 1) // SSD_HEAD_DIM
    sel_row = lax.broadcasted_iota(jnp.int32, (LANES, GROUP_WIDTH), 0)
    sel_valid = sel_row < 3 * GROUP_LANES
    sel_f = (sel_valid & (sel_row % GROUP_LANES == head_of_col)).astype(BF16)
    sel_b = (sel_valid & (sel_row % GROUP_LANES == head_of_col + hpg)).astype(BF16)
    low_half = jj < SSD_HEAD_DIM
    group_lane = jj < GROUP_LANES

    def direction_terms(r, sel):
        cum = cum_ref[r, :]
        dts = dts_ref[r, :]
        tot = jnp.where(jj[0:1, :] < hpg, cum[CHUNK - 1:CHUNK, :], cum[0:1, :])
        wst = dts * jnp.exp2(tot - cum)
        eoff = jnp.where(group_lane, jnp.exp2(cum), 0.0)
        edec = jnp.broadcast_to(jnp.where(group_lane[0:1, :], jnp.exp2(tot), 0.0), (8, LANES))
        ex = _dot(_stack_split3(jnp.concatenate([wst, eoff, edec], axis=0)), sel)
        return cum, ex[:CHUNK], ex[CHUNK:2 * CHUNK], ex[2 * CHUNK:2 * CHUNK + 1]

    def state_step(h_s, r, sel):
        cum, w_state, e_off, e_dec = direction_terms(r, sel)
        xc = xs_s[r, :]
        xw = (xc.astype(F32) * w_state).astype(BF16)
        s_new = lax.dot_general(b_s[r, :], xw, (((0,), (0,)), ((), ())), preferred_element_type=F32)
        h = h_s[...]
        y_off = _dot(c_s[r, :], h.astype(BF16)) * e_off
        h_s[...] = h * e_dec + s_new
        return cum, xc, y_off

    def fwd_body(k, carry):
        r = pl.ds(pl.multiple_of(k * CHUNK, CHUNK), CHUNK)
        cum, xc, y_off = state_step(hf_s, r, sel_f)
        if with_output:
            g = _dot_nt(c_s[r, :], b_s[r, :])
            rt = pl.ds(pl.multiple_of(k * GROUP_LANES, GROUP_LANES), GROUP_LANES)
            src_t = src_t_ref[rt, :]
            pieces = []
            for pair in range(hpg // 2):
                ms = []
                for h in (2 * pair, 2 * pair + 1):
                    hb = h + hpg
                    mf = jnp.exp2(jnp.where(lower, cum[:, h:h + 1] - src_t[h:h + 1, :], NEG_BIG))
                    mb = jnp.exp2(jnp.where(upper, cum[:, hb:hb + 1] - src_t[hb:hb + 1, :], NEG_BIG))
                    ms.append((g * (mf + mb)).astype(BF16))
                lhs = jnp.concatenate(ms, axis=1)
                xp = xc[:, pair * LANES:(pair + 1) * LANES]
                zero = jnp.zeros_like(xp)
                rhs = jnp.concatenate([jnp.where(low_half, xp, zero), jnp.where(low_half, zero, xp)], axis=0)
                pieces.append(_dot(lhs, rhs))
            y_diag = jnp.concatenate(pieces, axis=1)
            y_s[r, :] = y_diag + y_off + xc.astype(F32) * dskip_ref[...]
        rb = pl.ds(pl.multiple_of((n_chunks - 1 - k) * CHUNK, CHUNK), CHUNK)
        _, _, y_off_b = state_step(hb_s, rb, sel_b)
        if with_output:
            yb_s[rb, :] = y_off_b
        return carry

    lax.fori_loop(0, n_chunks, fwd_body, 0, unroll=8 if n_chunks % 8 == 0 else 2)

    if with_output:
        def sum_body(k, carry):
            r = pl.ds(pl.multiple_of(k * CHUNK, CHUNK), CHUNK)
            y_ref[r, :] = (y_s[r, :] + yb_s[r, :]).astype(y_ref.dtype)
            return carry

        lax.fori_loop(0, n_chunks, sum_body, 0)

    if not with_output:
        hf_out[...] = hf_s[...]
        hb_out[...] = hb_s[...]


def _ssd(pm, xs_col0, bc_col0, dt_terms, conv_w, conv_b, dskip_g, h0f, h0b, *, batch, seq, conv_tile, conv_period,
         with_output):
    n_chunks = seq // CHUNK
    dts_g, cum_g, src_t_g = dt_terms
    dt_spec = pl.BlockSpec((None, seq, LANES), lambda b, g: (g, b, 0))
    dt_t_spec = pl.BlockSpec((None, n_chunks * GROUP_LANES, LANES), lambda b, g: (g, b, 0))
    xs_blk0 = xs_col0 // GROUP_WIDTH
    b_blk0 = bc_col0 // SSD_STATE
    c_blk0 = b_blk0 + SSD_GROUPS
    cw_b0 = SSD_WIDTH // SSD_STATE
    cw_c0 = cw_b0 + SSD_GROUPS
    state_spec = pl.BlockSpec((None, None, SSD_STATE, GROUP_WIDTH), lambda b, g: (b, g, 0, 0))
    in_specs = [
        pl.BlockSpec((seq, GROUP_WIDTH), lambda b, g: (b, xs_blk0 + g)),
        pl.BlockSpec((seq, SSD_STATE), lambda b, g: (b, b_blk0 + g)),
        pl.BlockSpec((seq, SSD_STATE), lambda b, g: (b, c_blk0 + g)),
        dt_spec,
        dt_spec,
        dt_t_spec,
        pl.BlockSpec((3, GROUP_WIDTH), lambda b, g: (0, g)),
        pl.BlockSpec((1, GROUP_WIDTH), lambda b, g: (0, g)),
        pl.BlockSpec((3, SSD_STATE), lambda b, g: (0, cw_b0 + g)),
        pl.BlockSpec((1, SSD_STATE), lambda b, g: (0, cw_b0 + g)),
        pl.BlockSpec((3, SSD_STATE), lambda b, g: (0, cw_c0 + g)),
        pl.BlockSpec((1, SSD_STATE), lambda b, g: (0, cw_c0 + g)),
        pl.BlockSpec((None, 1, GROUP_WIDTH), lambda b, g: (g, 0, 0)),
        state_spec,
        state_spec,
    ]
    scratch = [
        pltpu.VMEM((seq, GROUP_WIDTH), BF16),
        pltpu.VMEM((seq, SSD_STATE), BF16),
        pltpu.VMEM((seq, SSD_STATE), BF16),
        pltpu.VMEM((SSD_STATE, GROUP_WIDTH), F32),
        pltpu.VMEM((SSD_STATE, GROUP_WIDTH), F32),
    ]
    if with_output:
        out_specs = pl.BlockSpec((seq, GROUP_WIDTH), lambda b, g: (b, g))
        out_shape = jax.ShapeDtypeStruct((batch * seq, SSD_WIDTH), BF16)
        scratch += [pltpu.VMEM((seq, GROUP_WIDTH), F32)] * 2
    else:
        out_specs = [state_spec, state_spec]
        out_shape = [jax.ShapeDtypeStruct((batch, SSD_GROUPS, SSD_STATE, GROUP_WIDTH), F32)] * 2
    return pl.pallas_call(
        functools.partial(_ssd_kernel, n_chunks=n_chunks, conv_tile=conv_tile, conv_period=conv_period,
                          with_output=with_output),
        grid=(batch, SSD_GROUPS),
        in_specs=in_specs,
        out_specs=out_specs,
        out_shape=out_shape,
        scratch_shapes=scratch,
        compiler_params=_params("parallel", "parallel"),
        name="ssd_out" if with_output else "ssd_ctx",
    )(pm, pm, pm, dts_g, cum_g, src_t_g, conv_w, conv_b, conv_w, conv_b, conv_w, conv_b, dskip_g, h0f, h0b)


def _out_proj_kernel(yd_ref, z_ref, u_ref, v_ref, x_ref, g1_ref, ng_ref, lng_ref, lnb_ref, ws_ref, bst_ref, w_ref,
                     o_ref, mix_cur, mix_nxt):
    tm = yd_ref.shape[0]
    s = pl.program_id(0)
    n = pl.num_programs(0) - 1

    def build_steps():
        steps = []
        for k in range(tm // CHUNK):
            r = pl.ds(k * CHUNK, CHUNK)
            shared = {}

            def ssd_part(r=r):
                a = yd_ref[r, :].astype(F32) * z_ref[r, :].astype(F32)
                ms = jnp.mean(a * a, axis=-1, keepdims=True)
                mix_nxt[r, 0:SSD_WIDTH] = (a * lax.rsqrt(ms + NORM_EPS) * ng_ref[...]).astype(BF16)

            def ln_part(r=r, shared=shared):
                gv = v_ref[r, :].astype(F32)
                mu = jnp.mean(gv, axis=-1, keepdims=True)
                xc = gv - mu
                var = jnp.mean(xc * xc, axis=-1, keepdims=True)
                shared["ln"] = ((xc * lax.rsqrt(var + NORM_EPS)) * lng_ref[...] + lnb_ref[...]).astype(BF16)

            def gate_part(h0, h1, r=r, shared=shared):
                for h in range(h0, h1):
                    c0, c1 = h * CM_HEAD_DIM, (h + 1) * CM_HEAD_DIM
                    sp = _dot(ws_ref[h], shared["ln"][:, c0:c1]) + bst_ref[:, h:h + 1]
                    gu = u_ref[r, c0:c1].astype(F32)
                    mix_nxt[r, SSD_WIDTH + c0:SSD_WIDTH + c1] = (gu * sp).astype(BF16)

            steps += [ssd_part, ln_part, functools.partial(gate_part, 0, CM_HEADS // 2),
                      functools.partial(gate_part, CM_HEADS // 2, CM_HEADS)]
        return steps

    n_proj = 8
    pw = D_MODEL // n_proj

    def project_steps():
        def piece(q):
            c = slice(q * pw, (q + 1) * pw)
            o_ref[:, c] = x_ref[:, c] + g1_ref[:, c] * _dot(mix_cur[...], w_ref[:, c])

        return [functools.partial(piece, q) for q in range(n_proj)]

    @pl.when(s == 0)
    def _():
        for step in build_steps():
            step()

    @pl.when((s > 0) & (s < n))
    def _():
        for proj, build in zip(project_steps(), build_steps()):
            proj()
            build()

    @pl.when(s == n)
    def _():
        for step in project_steps():
            step()

    @pl.when(s < n)
    def _():
        mix_cur[...] = mix_nxt[...]


def _out_proj(yd, pm, x2, mod3, ssd_norm_g, ln_g, ln_b, ws, bst, w_out, *, seq, tm):
    m = x2.shape[0]
    n = m // tm
    tiles_per_batch = seq // tm
    z_blk, u_blk, v_blk = 0, 2 * SSD_WIDTH // CM_WIDTH, 2 * SSD_WIDTH // CM_WIDTH + 1
    g1_blk = 2
    row = lambda shape: pl.BlockSpec(shape, lambda s: (0, 0))
    build_tile = lambda s: jnp.minimum(s, n - 1)
    proj_tile = lambda s: jnp.maximum(s - 1, 0)
    return pl.pallas_call(
        _out_proj_kernel,
        grid=(n + 1,),
        in_specs=[
            pl.BlockSpec((tm, SSD_WIDTH), lambda s: (build_tile(s), 0)),
            pl.BlockSpec((tm, SSD_WIDTH), lambda s: (build_tile(s), z_blk)),
            pl.BlockSpec((tm, CM_WIDTH), lambda s: (build_tile(s), u_blk)),
            pl.BlockSpec((tm, CM_WIDTH), lambda s: (build_tile(s), v_blk)),
            pl.BlockSpec((tm, D_MODEL), lambda s: (proj_tile(s), 0)),
            pl.BlockSpec((None, 1, D_MODEL), lambda s: (proj_tile(s) // tiles_per_batch, 0, g1_blk)),
            row((1, SSD_WIDTH)),
            row((1, CM_WIDTH)),
            row((1, CM_WIDTH)),
            pl.BlockSpec((CM_HEADS, CHUNK, CHUNK), lambda s: (0, 0, 0)),
            row((CHUNK, CM_HEADS)),
            pl.BlockSpec((SSD_WIDTH + CM_WIDTH, D_MODEL), lambda s: (0, 0), pipeline_mode=pl.Buffered(1)),
        ],
        out_specs=pl.BlockSpec((tm, D_MODEL), lambda s: (proj_tile(s), 0)),
        out_shape=jax.ShapeDtypeStruct((m, D_MODEL), F32),
        scratch_shapes=[pltpu.VMEM((tm, SSD_WIDTH + CM_WIDTH), BF16)] * 2,
        compiler_params=_params("arbitrary"),
        name="out_proj",
    )(yd, pm, pm, pm, x2, mod3, ssd_norm_g, ln_g, ln_b, ws, bst, w_out)


def _route(logits):
    lane = lax.broadcasted_iota(jnp.int32, logits.shape, 1)
    lane_f = lane.astype(F32)
    is_group = (lane >= N_EXPERTS) & (lane < N_EXPERTS + N_GROUPS)
    lg = jnp.where(is_group, logits, NEG_BIG)
    mg = jnp.max(lg, axis=1, keepdims=True)
    top_pg = 1.0 / jnp.sum(jnp.exp(lg - mg), axis=1, keepdims=True)
    gi = jnp.min(jnp.where(lg == mg, lane_f, 1e9), axis=1, keepdims=True) - N_EXPERTS
    in_group = (lane < N_EXPERTS) & ((lane // EXPERTS_PER_GROUP).astype(F32) == gi)
    le = jnp.where(in_group, logits, NEG_BIG)
    m1 = jnp.max(le, axis=1, keepdims=True)
    i1 = jnp.min(jnp.where(le == m1, lane_f, 1e9), axis=1, keepdims=True)
    le2 = jnp.where(lane_f == i1, NEG_BIG, le)
    m2 = jnp.max(le2, axis=1, keepdims=True)
    i2 = jnp.min(jnp.where(le2 == m2, lane_f, 1e9), axis=1, keepdims=True)
    e2 = jnp.exp(m2 - m1)
    p1 = 1.0 / (1.0 + e2)
    p2 = e2 * p1
    return i1, i2, p1 * top_pg, p2 * top_pg


RINFO_E1, RINFO_E2, RINFO_W1, RINFO_W2, RINFO_R1, RINFO_R2 = range(6)


def _route_kernel(x1_ref, n2g_ref, sh2_ref, sc2_ref, wr_ref, br_ref, h2_ref, rinfo_ref, counts_ref, cnt_scr):
    tm = x1_ref.shape[0]

    @pl.when(pl.program_id(0) == 0)
    def _():
        cnt_scr[...] = jnp.zeros_like(cnt_scr)

    x1 = x1_ref[...]
    ms = jnp.mean(x1 * x1, axis=-1, keepdims=True)
    h2 = (x1 * lax.rsqrt(ms + NORM_EPS) * n2g_ref[...]) * (1.0 + sc2_ref[...]) + sh2_ref[...]
    h2_ref[...] = h2
    hi, mid, _ = _split3(h2)
    p = _dot(hi, wr_ref[...])
    logits = p[:, :LANES] + p[:, LANES:] + _dot(mid, wr_ref[:, :LANES]) + br_ref[...]
    i1, i2, w1, w2 = _route(logits)

    lane = lax.broadcasted_iota(jnp.int32, (tm, LANES), 1)
    lane_f = lane.astype(F32)
    oh1 = jnp.where(lane_f == i1, 1.0, 0.0)
    oh2 = jnp.where(lane_f == i2, 1.0, 0.0)
    before = (lax.broadcasted_iota(jnp.int32, (tm, tm), 0) > lax.broadcasted_iota(jnp.int32, (tm, tm), 1)).astype(BF16)
    carried = cnt_scr[...]
    tot1 = jnp.sum(oh1, axis=0, keepdims=True)
    r1 = jnp.sum(oh1 * (_dot(before, oh1.astype(BF16)) + carried), axis=1, keepdims=True)
    r2 = jnp.sum(oh2 * (_dot(before, oh2.astype(BF16)) + (carried + tot1)), axis=1, keepdims=True)
    counts = carried + tot1 + jnp.sum(oh2, axis=0, keepdims=True)
    cnt_scr[...] = counts
    counts_ref[...] = jnp.broadcast_to(counts, counts_ref.shape)

    info = jnp.zeros((tm, LANES), F32)
    for k, v in ((RINFO_E1, i1), (RINFO_E2, i2), (RINFO_W1, w1), (RINFO_W2, w2), (RINFO_R1, r1), (RINFO_R2, r2)):
        info = jnp.where(lane == k, v, info)
    rinfo_ref[...] = info


def _route_call(x1, mod3, norm2_g, w_router, b_router, *, seq, tm):
    m = x1.shape[0]
    tiles_per_batch = seq // tm
    row = lambda shape: pl.BlockSpec(shape, lambda i: (0, 0))
    modrow = lambda k: pl.BlockSpec((None, 1, D_MODEL), lambda i: (i // tiles_per_batch, 0, k))
    return pl.pallas_call(
        _route_kernel,
        grid=(m // tm,),
        in_specs=[
            pl.BlockSpec((tm, D_MODEL), lambda i: (i, 0)),
            row((1, D_MODEL)),
            modrow(3),
            modrow(4),
            row((D_MODEL, 2 * LANES)),
            row((1, LANES)),
        ],
        out_specs=[
            pl.BlockSpec((tm, D_MODEL), lambda i: (i, 0)),
            pl.BlockSpec((tm, LANES), lambda i: (i, 0)),
            row((8, LANES)),
        ],
        out_shape=[
            jax.ShapeDtypeStruct((m, D_MODEL), F32),
            jax.ShapeDtypeStruct((m, LANES), F32),
            jax.ShapeDtypeStruct((8, LANES), F32),
        ],
        scratch_shapes=[pltpu.VMEM((1, LANES), F32)],
        compiler_params=_params("arbitrary"),
        name="route",
    )(x1, norm2_g, mod3, mod3, w_router, b_router)


def _dispatch_kernel(lt_ref, nu_ref, pos_ref, h2_ref, x_hbm, zbuf, zsem, sem, *, tile_rows, n_tiles):
    tm = h2_ref.shape[0]
    n_used = nu_ref[0]

    def zero_copy(t):
        return pltpu.make_async_copy(zbuf, x_hbm.at[pl.ds(pl.multiple_of(t * tile_rows, tile_rows), tile_rows), :], zsem)

    @pl.when(pl.program_id(0) == 0)
    def _():
        zbuf[...] = jnp.zeros_like(zbuf)
        for phase in ("start", "wait"):
            for e in range(N_EXPERTS):
                @pl.when(lt_ref[e] >= 0)
                def _():
                    getattr(zero_copy(lt_ref[e]), phase)()

            def unused(t, carry):
                getattr(zero_copy(t), phase)()
                return carry

            lax.fori_loop(n_used, n_tiles, unused, 0)

    def body(r, carry):
        for k in range(2):
            pltpu.make_async_copy(h2_ref.at[pl.ds(r, 1), :], x_hbm.at[pl.ds(pos_ref[0, 2 * r + k], 1), :],
                                  sem).start(priority=k)
        return carry

    lax.fori_loop(0, tm, body, 0, unroll=8)
    for k in range(2):
        pltpu.make_async_copy(h2_ref, x_hbm.at[pl.ds(0, tm), :], sem).wait()


def _dispatch_call(last_tile, n_used, pos, h2, *, tm, tile_rows, n_tiles):
    m = h2.shape[0]
    grid_spec = pltpu.PrefetchScalarGridSpec(
        num_scalar_prefetch=2,
        grid=(m // tm,),
        in_specs=[
            pl.BlockSpec((None, 1, 2 * tm), lambda i, lt, nu: (i, 0, 0), memory_space=pltpu.SMEM),
            pl.BlockSpec((tm, D_MODEL), lambda i, lt, nu: (i, 0)),
        ],
        out_specs=pl.BlockSpec(memory_space=pl.ANY),
        scratch_shapes=[
            pltpu.VMEM((tile_rows, D_MODEL), F32),
            pltpu.SemaphoreType.DMA(()),
            pltpu.SemaphoreType.DMA(()),
        ],
    )
    return pl.pallas_call(
        functools.partial(_dispatch_kernel, tile_rows=tile_rows, n_tiles=n_tiles),
        grid_spec=grid_spec,
        out_shape=jax.ShapeDtypeStruct((n_tiles * tile_rows, D_MODEL), F32),
        compiler_params=_params("arbitrary"),
        name="dispatch",
    )(last_tile, n_used, pos, h2)


def _experts_kernel(te_ref, nu_ref, ord_ref, nxt_ref, x_ref, wg_hbm, wu_hbm, wd_hbm, y_ref,
                    wg_f, wu_f, wd_f, wg_b, wu_b, wd_b, sem):
    j = pl.program_id(0)
    n_used = nu_ref[0]

    def weight_copies(e, slot):
        return [pltpu.make_async_copy(src.at[e], dst.at[slot], sem.at[slot])
                for src, dst in ((wg_hbm, wg_f), (wu_hbm, wu_f), (wd_hbm, wd_f))]

    @pl.when(j == 0)
    def _():
        for cp in weight_copies(te_ref[0], 0):
            cp.start()

    @pl.when((j < n_used) & ((j == 0) | (te_ref[j] != te_ref[jnp.maximum(j - 1, 0)])))
    def _():
        slot = ord_ref[j] % 2
        for cp in weight_copies(te_ref[j], slot):
            cp.wait()

        @pl.when(nxt_ref[j] >= 0)
        def _():
            for cp in weight_copies(nxt_ref[j], 1 - slot):
                cp.start()

        wg_b[...] = wg_f[slot].astype(BF16)
        wu_b[...] = wu_f[slot].astype(BF16)
        wd_b[...] = wd_f[slot].astype(BF16)

    @pl.when(j < n_used)
    def _():
        xt = x_ref[...].astype(BF16)
        hid = (_silu(_dot(xt, wg_b[...])) * _dot(xt, wu_b[...])).astype(BF16)
        y_ref[...] = _dot(hid, wd_b[...])

    @pl.when(j >= n_used)
    def _():
        y_ref[...] = jnp.zeros_like(y_ref)


def _experts_call(tile_expert, n_used, tile_ord, next_expert, x_sorted, wg, wu, wd, *, tm):
    n_tiles = tile_expert.shape[0]
    hbm = pl.BlockSpec(memory_space=pl.ANY)
    grid_spec = pltpu.PrefetchScalarGridSpec(
        num_scalar_prefetch=4,
        grid=(n_tiles,),
        in_specs=[
            pl.BlockSpec((tm, D_MODEL), lambda j, te, nu, od, nx: (jnp.minimum(j, nu[0] - 1), 0)),
            hbm,
            hbm,
            hbm,
        ],
        out_specs=pl.BlockSpec((tm, D_MODEL), lambda j, te, nu, od, nx: (j, 0)),
        scratch_shapes=[
            pltpu.VMEM((2, D_MODEL, EXPERT_FF), F32),
            pltpu.VMEM((2, D_MODEL, EXPERT_FF), F32),
            pltpu.VMEM((2, EXPERT_FF, D_MODEL), F32),
            pltpu.VMEM((D_MODEL, EXPERT_FF), BF16),
            pltpu.VMEM((D_MODEL, EXPERT_FF), BF16),
            pltpu.VMEM((EXPERT_FF, D_MODEL), BF16),
            pltpu.SemaphoreType.DMA((2,)),
        ],
    )
    return pl.pallas_call(
        _experts_kernel,
        grid_spec=grid_spec,
        out_shape=jax.ShapeDtypeStruct((n_tiles * tm, D_MODEL), F32),
        compiler_params=_params("arbitrary"),
        name="experts",
    )(tile_expert, n_used, tile_ord, next_expert, x_sorted, wg, wu, wd)


def _combine_kernel(pos_cur, pos_nxt, x1_ref, g2_ref, nfg_ref, rinfo_ref, y_hbm, o_ref, ybuf, sem):
    i = pl.program_id(0)
    tm = x1_ref.shape[0]
    slot = i % 2

    def row_copy(pos_ref, r, k, s):
        return pltpu.make_async_copy(y_hbm.at[pl.ds(pos_ref[0, 2 * r + k], 1), :], ybuf.at[s, k, pl.ds(r, 1), :],
                                     sem.at[s])

    def wait_tile(s):
        for k in range(2):
            pltpu.make_async_copy(y_hbm.at[pl.ds(0, tm), :], ybuf.at[s, k], sem.at[s]).wait()

    @pl.when(i == 0)
    def _():
        def body(r, carry):
            for k in range(2):
                row_copy(pos_cur, r, k, 0).start()
            return carry

        lax.fori_loop(0, tm, body, 0)

    def step(cur):
        wait_tile(cur)
        for r in range(tm):
            for k in range(2):
                row_copy(pos_nxt, r, k, 1 - cur).start(priority=k)
        info = rinfo_ref[...]
        w1 = info[:, RINFO_W1:RINFO_W1 + 1]
        w2 = info[:, RINFO_W2:RINFO_W2 + 1]
        moe = w1 * ybuf[cur, 0] + w2 * ybuf[cur, 1]
        y = x1_ref[...] + g2_ref[...] * moe
        ms = jnp.mean(y * y, axis=-1, keepdims=True)
        o_ref[...] = y * lax.rsqrt(ms + NORM_EPS) * nfg_ref[...]

        @pl.when(i == pl.num_programs(0) - 1)
        def _():
            wait_tile(1 - cur)

    for cur in range(2):
        pl.when(slot == cur)(functools.partial(step, cur))


def _combine_call(pos, x1, mod3, normf_g, rinfo, y_sorted, *, seq, tm):
    m = x1.shape[0]
    n_tiles = m // tm
    tiles_per_batch = seq // tm
    return pl.pallas_call(
        _combine_kernel,
        grid=(n_tiles,),
        in_specs=[
            pl.BlockSpec((None, 1, 2 * tm), lambda i: (i, 0, 0), memory_space=pltpu.SMEM),
            pl.BlockSpec((None, 1, 2 * tm), lambda i: (jnp.minimum(i + 1, n_tiles - 1), 0, 0),
                         memory_space=pltpu.SMEM),
            pl.BlockSpec((tm, D_MODEL), lambda i: (i, 0)),
            pl.BlockSpec((None, 1, D_MODEL), lambda i: (i // tiles_per_batch, 0, 5)),
            pl.BlockSpec((1, D_MODEL), lambda i: (0, 0)),
            pl.BlockSpec((tm, LANES), lambda i: (i, 0)),
            pl.BlockSpec(memory_space=pl.ANY),
        ],
        out_specs=pl.BlockSpec((tm, D_MODEL), lambda i: (i, 0)),
        out_shape=jax.ShapeDtypeStruct((m, D_MODEL), F32),
        scratch_shapes=[
            pltpu.VMEM((2, 2, tm, D_MODEL), F32),
            pltpu.SemaphoreType.DMA((2,)),
        ],
        compiler_params=_params("arbitrary"),
        name="combine",
    )(pos, pos, x1, mod3, normf_g, rinfo, y_sorted)


def _dispatch_plan(rinfo, counts, *, tm):
    n_tok = rinfo.shape[0]
    n_tiles = (2 * n_tok + N_EXPERTS * (tm - 1)) // tm + 1
    expert = rinfo[:, RINFO_E1:RINFO_E2 + 1].astype(jnp.int32)
    rank = rinfo[:, RINFO_R1:RINFO_R2 + 1].astype(jnp.int32)
    cnt = counts[0, :N_EXPERTS].astype(jnp.int32)
    tiles_e = (cnt + tm - 1) // tm
    end_tile = jnp.cumsum(tiles_e)
    start_row = (end_tile - tiles_e) * tm
    is_e = expert[:, :, None] == jnp.arange(N_EXPERTS, dtype=jnp.int32)
    pos = jnp.sum(jnp.where(is_e, start_row, 0), axis=-1) + rank
    n_used = end_tile[-1:]
    tile_ids = jnp.minimum(jnp.arange(n_tiles, dtype=jnp.int32), n_used - 1)
    tile_expert = jnp.sum(tile_ids[:, None] >= end_tile[None, :], axis=1).astype(jnp.int32)
    last_tile = jnp.where(tiles_e > 0, end_tile - 1, -1).astype(jnp.int32)
    used = (tiles_e > 0).astype(jnp.int32)
    ord_e = jnp.cumsum(used) - used
    ids = jnp.arange(N_EXPERTS, dtype=jnp.int32)
    later = jnp.where((ids[None, :] > ids[:, None]) & (used[None, :] > 0), ids[None, :], N_EXPERTS)
    nxt_e = jnp.min(later, axis=1)
    nxt_e = jnp.where(nxt_e < N_EXPERTS, nxt_e, -1).astype(jnp.int32)
    is_te = tile_expert[:, None] == ids[None, :]
    tile_ord = jnp.sum(jnp.where(is_te, ord_e, 0), axis=1).astype(jnp.int32)
    next_expert = jnp.sum(jnp.where(is_te, nxt_e, 0), axis=1).astype(jnp.int32)
    return pos, (tile_expert, n_used.astype(jnp.int32), tile_ord, next_expert), last_tile, n_tiles


def kernel(x, c, ctx, c_ctx, w_mod, b_mod, norm1_g, w_in, conv_w, conv_b, dt_bias_f, dt_bias_b, a_log_f, a_log_b,
           d_skip, ssd_norm_g, cm_ln_g, cm_ln_b, w_spatial, b_spatial, w_out, norm2_g, w_router_group,
           b_router_group, w_router_expert, b_router_expert, w_exp_gate, w_exp_up, w_exp_down, normf_g):
    bsz, seq, _ = x.shape
    ctx_len = ctx.shape[1]
    i = 0

    cc = jnp.concatenate([c, c_ctx[None, :], jnp.zeros((MOD_ROWS - bsz - 1, D_MODEL), F32)], axis=0)
    mod = _modulation(cc, w_mod[i], b_mod[i][None, :])
    mod3 = mod.reshape(MOD_ROWS, 1, N_MOD * D_MODEL)

    w_main, w_dt = _w_in_prep(jnp.swapaxes(w_in[i], 0, 1))
    g1row = norm1_g[i][None, :]

    x2 = x.reshape(bsz * seq, D_MODEL)
    ctx2 = ctx.reshape(bsz * ctx_len, D_MODEL)
    tm_in = 1024
    tn_in = 1024
    xs_col0, bc_col0 = SSD_WIDTH, 2 * SSD_WIDTH + 2 * CM_WIDTH
    pm_x, dt_x = _in_proj(x2, mod3, lambda t: t // (seq // tm_in), g1row, w_main, w_dt, tm_in, tn_in)
    ctx_blocks = tuple(range(xs_col0 // tn_in, 2 * SSD_WIDTH // tn_in)) + (bc_col0 // tn_in,)
    pm_c, dt_c = _in_proj(ctx2, mod3, lambda t: bsz, g1row, w_main, w_dt, bsz * ctx_len, tn_in, ctx_blocks)

    def dt_lanes(fwd, bwd):
        both = jnp.stack([fwd.reshape(SSD_GROUPS, HEADS_PER_GROUP), bwd.reshape(SSD_GROUPS, HEADS_PER_GROUP)], axis=1)
        return jnp.pad(both.reshape(-1).astype(F32), (0, LANES - 2 * SSD_HEADS))[None, :]

    dt_bias_row = dt_lanes(dt_bias_f[i], dt_bias_b[i])
    a_row = dt_lanes(-jnp.exp(a_log_f[i].astype(F32)), -jnp.exp(a_log_b[i].astype(F32)))
    dskip_g = jnp.repeat(d_skip[i], SSD_HEAD_DIM).reshape(SSD_GROUPS, 1, GROUP_WIDTH)
    cw = conv_w[i]
    cb = conv_b[i][None, :]
    h_zero = jnp.zeros((bsz, SSD_GROUPS, SSD_STATE, GROUP_WIDTH), F32)

    dt_terms_c = _ssd_dt(dt_c, dt_bias_row, a_row, rows_per_step=bsz * ctx_len)
    dt_terms_x = _ssd_dt(dt_x, dt_bias_row, a_row, rows_per_step=1024)
    hc_f, hc_b = _ssd(pm_c, 0, SSD_WIDTH, dt_terms_c, cw, cb, dskip_g, h_zero, h_zero,
                      batch=bsz, seq=ctx_len, conv_tile=ctx_len, conv_period=ctx_len, with_output=False)
    yd = _ssd(pm_x, xs_col0, bc_col0, dt_terms_x, cw, cb, dskip_g, hc_f, hc_b,
              batch=bsz, seq=seq, conv_tile=CHUNK, conv_period=GRID_W, with_output=True)

    x1 = _out_proj(yd, pm_x, x2, mod3, ssd_norm_g[i][None, :], cm_ln_g[i][None, :], cm_ln_b[i][None, :],
                   w_spatial[i].astype(BF16), b_spatial[i].T, w_out[i].astype(BF16), seq=seq, tm=256)

    w_re = jnp.transpose(w_router_expert[i], (1, 0, 2)).reshape(D_MODEL, N_EXPERTS)
    pad = LANES - N_EXPERTS - N_GROUPS
    w_router = jnp.pad(jnp.concatenate([w_re, w_router_group[i]], axis=1), ((0, 0), (0, pad)))
    w_router_hi = w_router.astype(BF16)
    w_router = jnp.concatenate([w_router_hi, (w_router - w_router_hi.astype(F32)).astype(BF16)], axis=1)
    b_router = jnp.pad(jnp.concatenate([b_router_expert[i].reshape(-1), b_router_group[i]]), (0, pad))[None, :]
    h2, rinfo, counts = _route_call(x1, mod3, norm2_g[i][None, :], w_router, b_router, seq=seq, tm=512)

    tm_e = 256
    pos, tile_tables, last_tile, n_tiles = _dispatch_plan(rinfo, counts, tm=tm_e)
    tm_d = 512
    x_sorted = _dispatch_call(last_tile, tile_tables[1], pos.reshape(bsz * seq // tm_d, 1, 2 * tm_d), h2,
                              tm=tm_d, tile_rows=tm_e, n_tiles=n_tiles)
    wg = w_exp_gate[i].reshape(N_EXPERTS, D_MODEL, EXPERT_FF)
    wu = w_exp_up[i].reshape(N_EXPERTS, D_MODEL, EXPERT_FF)
    wd = w_exp_down[i].reshape(N_EXPERTS, EXPERT_FF, D_MODEL)
    y_sorted = _experts_call(*tile_tables, x_sorted, wg, wu, wd, tm=tm_e)
    tm_c = 256
    out = _combine_call(pos.reshape(bsz * seq // tm_c, 1, 2 * tm_c), x1, mod3, normf_g[None, :], rinfo, y_sorted,
                        seq=seq, tm=tm_c)
    return out.reshape(bsz, seq, D_MODEL)
```

```python
import functools

import jax
import jax.numpy as jnp
from jax import lax
from jax.experimental import pallas as pl
from jax.experimental.pallas import tpu as pltpu

F32 = jnp.float32
BF16 = jnp.bfloat16
HIGHEST = lax.Precision.HIGHEST

D_MODEL = 2048
GRID_W = 64
SSD_WIDTH = 2048
CM_WIDTH = 2048
SSD_HEADS = 32
SSD_HEAD_DIM = 64
SSD_GROUPS = 4
HEADS_PER_GROUP = SSD_HEADS // SSD_GROUPS
GROUP_WIDTH = HEADS_PER_GROUP * SSD_HEAD_DIM
SSD_STATE = 128
CHUNK = 128
BC_WIDTH = SSD_GROUPS * SSD_STATE
CM_HEADS = 8
CM_HEAD_DIM = CM_WIDTH // CM_HEADS
N_GROUPS = 4
N_EXPERTS = 32
EXPERTS_PER_GROUP = 8
EXPERT_FF = 512
N_MOD = 6
NORM_EPS = 1e-6
LANES = 128
NEG_BIG = -1e30
MOD_ROWS = 8
VMEM_LIMIT = 56 * 1024 * 1024


def _params(*sem):
    return pltpu.CompilerParams(dimension_semantics=sem, vmem_limit_bytes=VMEM_LIMIT)


LOG2_E = 1.4426950408889634
GELU_C = 0.7978845608028654
GELU_K1 = -2.0 * GELU_C * LOG2_E
GELU_K3 = GELU_K1 * 0.044715


def _silu(v):
    return v / (1.0 + jnp.exp2(v * (-LOG2_E)))


def _gelu_tanh(v):
    return v / (1.0 + jnp.exp2(v * (GELU_K1 + GELU_K3 * (v * v))))


def _dot(a, b):
    return jnp.dot(a, b, preferred_element_type=F32)


def _dot_nt(a, b):
    return lax.dot_general(a, b, (((1,), (1,)), ((), ())), preferred_element_type=F32)


def _mod_kernel(cc_ref, w_ref, b_ref, o_ref):
    a = _silu(cc_ref[...])
    hi = a.astype(BF16).astype(F32)
    lhs = jnp.concatenate([hi, a - hi], axis=0).astype(BF16)
    r = _dot(lhs, w_ref[...].astype(BF16))
    o_ref[...] = r[:MOD_ROWS] + r[MOD_ROWS:] + b_ref[...]


def _modulation(cc, w_mod, b_mod):
    n = w_mod.shape[1]
    tn = 1024
    return pl.pallas_call(
        _mod_kernel,
        grid=(n // tn,),
        in_specs=[
            pl.BlockSpec((MOD_ROWS, D_MODEL), lambda j: (0, 0)),
            pl.BlockSpec((D_MODEL, tn), lambda j: (0, j)),
            pl.BlockSpec((1, tn), lambda j: (0, j)),
        ],
        out_specs=pl.BlockSpec((MOD_ROWS, tn), lambda j: (0, j)),
        out_shape=jax.ShapeDtypeStruct((MOD_ROWS, n), F32),
        compiler_params=_params("arbitrary"),
        name="modulation",
    )(cc, w_mod, b_mod)


PREP_ROWS = 512
PREP_UV_BLK0 = 2 * SSD_WIDTH // PREP_ROWS
PREP_BC_BLK0 = (2 * SSD_WIDTH + 2 * CM_WIDTH) // PREP_ROWS
PREP_N_BLKS = PREP_BC_BLK0 + 2 * BC_WIDTH // PREP_ROWS
DT_COL0 = 2 * SSD_WIDTH + 2 * BC_WIDTH
UV_COL0 = DT_COL0 + 2 * SSD_HEADS


def _w_in_prep_kernel(a_ref, dt_ref, o_ref, wdt_ref):
    @pl.when(pl.program_id(0) == 0)
    def _():
        hpg = HEADS_PER_GROUP
        parts = []
        for g in range(SSD_GROUPS):
            parts += [dt_ref[hpg * g:hpg * (g + 1), :], dt_ref[SSD_HEADS + hpg * g:SSD_HEADS + hpg * (g + 1), :]]
        parts.append(dt_ref[2 * SSD_HEADS:, :])
        wdt_ref[...] = jnp.concatenate(parts, axis=0).astype(BF16)

    o_ref[...] = a_ref[...].astype(BF16)


def _w_in_prep(w_in_t):
    def src_row(j):
        uv = UV_COL0 + (j - PREP_UV_BLK0) * PREP_ROWS
        bc = 2 * SSD_WIDTH + (j - PREP_BC_BLK0) * PREP_ROWS
        return pl.multiple_of(jnp.where(j < PREP_UV_BLK0, j * PREP_ROWS, jnp.where(j < PREP_BC_BLK0, uv, bc)), 8)

    return pl.pallas_call(
        _w_in_prep_kernel,
        grid=(PREP_N_BLKS,),
        in_specs=[
            pl.BlockSpec((pl.Element(PREP_ROWS), pl.Element(D_MODEL)), lambda j: (src_row(j), 0)),
            pl.BlockSpec((LANES, D_MODEL), lambda j: (DT_COL0 // LANES, 0)),
        ],
        out_specs=[
            pl.BlockSpec((PREP_ROWS, D_MODEL), lambda j: (j, 0)),
            pl.BlockSpec((LANES, D_MODEL), lambda j: (0, 0)),
        ],
        out_shape=[
            jax.ShapeDtypeStruct((PREP_N_BLKS * PREP_ROWS, D_MODEL), BF16),
            jax.ShapeDtypeStruct((LANES, D_MODEL), BF16),
        ],
        compiler_params=_params("arbitrary"),
        name="w_in_prep",
    )(w_in_t, w_in_t)


ACT_NONE, ACT_SILU, ACT_GELU = 0, 1, 2


def _in_proj_kernel(x_ref, g_ref, sh_ref, sc_ref, w_ref, wdt_ref, o_ref, dt_ref, h_scr, *, block_acts):
    tm = x_ref.shape[0]
    rc = 256

    @pl.when(pl.program_id(1) == 0)
    def _():
        gain = g_ref[...]
        scale = 1.0 + sc_ref[...]
        shift = sh_ref[...]

        def body(k, carry):
            r = pl.ds(pl.multiple_of(k * rc, rc), rc)
            xv = x_ref[r, :]
            ms = jnp.mean(xv * xv, axis=-1, keepdims=True)
            h = (xv * lax.rsqrt(ms + NORM_EPS) * gain) * scale + shift
            h_scr[r, :] = h.astype(BF16)
            return carry

        lax.fori_loop(0, tm // rc, body, 0)
        dt_ref[...] = _dot_nt(h_scr[...], wdt_ref[...])

    j = pl.program_id(1)
    for act, fn in ((ACT_NONE, lambda v: v), (ACT_SILU, _silu), (ACT_GELU, _gelu_tanh)):
        blocks = [q for q, a in enumerate(block_acts) if a == act]
        if blocks:
            is_act = functools.reduce(jnp.logical_or, [j == q for q in blocks])

            @pl.when(is_act)
            def _():
                o_ref[...] = fn(_dot_nt(h_scr[...], w_ref[...])).astype(o_ref.dtype)


def _in_proj(x2, mod3, mod_row_of_tile, norm_g, w_main, w_dt, tm, tn, w_blocks=None):
    m = x2.shape[0]
    if w_blocks is None:
        w_blocks = tuple(range(w_main.shape[0] // tn))
    n = len(w_blocks) * tn
    z_end, uv0, uv_end = SSD_WIDTH // tn, 2 * SSD_WIDTH // tn, (2 * SSD_WIDTH + 2 * CM_WIDTH) // tn
    block_acts = tuple(ACT_SILU if b < z_end else ACT_GELU if uv0 <= b < uv_end else ACT_NONE for b in w_blocks)

    def w_blk(j):
        if w_blocks == tuple(range(len(w_blocks))):
            return j
        blk = jnp.int32(w_blocks[-1])
        for q in range(len(w_blocks) - 2, -1, -1):
            blk = jnp.where(j == q, w_blocks[q], blk)
        return blk

    return pl.pallas_call(
        functools.partial(_in_proj_kernel, block_acts=block_acts),
        grid=(m // tm, n // tn),
        in_specs=[
            pl.BlockSpec((tm, D_MODEL), lambda i, j: (i, 0)),
            pl.BlockSpec((1, D_MODEL), lambda i, j: (0, 0)),
            pl.BlockSpec((None, 1, D_MODEL), lambda i, j: (mod_row_of_tile(i), 0, 0)),
            pl.BlockSpec((None, 1, D_MODEL), lambda i, j: (mod_row_of_tile(i), 0, 1)),
            pl.BlockSpec((tn, D_MODEL), lambda i, j: (w_blk(j), 0)),
            pl.BlockSpec((LANES, D_MODEL), lambda i, j: (0, 0)),
        ],
        out_specs=[
            pl.BlockSpec((tm, tn), lambda i, j: (i, j)),
            pl.BlockSpec((tm, LANES), lambda i, j: (i, 0)),
        ],
        out_shape=[
            jax.ShapeDtypeStruct((m, n), BF16),
            jax.ShapeDtypeStruct((m, LANES), F32),
        ],
        scratch_shapes=[pltpu.VMEM((tm, D_MODEL), BF16)],
        compiler_params=_params("parallel", "arbitrary"),
        name="in_proj",
    )(x2, norm_g, mod3, mod3, w_main, w_dt)


def _split3(v):
    hi = v.astype(BF16)
    rem = v - hi.astype(F32)
    mid = rem.astype(BF16)
    lo = (rem - mid.astype(F32)).astype(BF16)
    return hi, mid, lo


GROUP_LANES = 2 * HEADS_PER_GROUP


STACKED_PIECES = 3


def _stack_split(v):
    hi, mid, lo = _split3(v)
    stacked = (hi.astype(F32) + pltpu.roll(mid.astype(F32), GROUP_LANES, 1)
               + pltpu.roll(lo.astype(F32), 2 * GROUP_LANES, 1))
    return stacked.astype(BF16)


def _ssd_dt_kernel(dt_ref, bias_ref, a_ref, dts_ref, cum_ref, src_t_ref):
    hpg = HEADS_PER_GROUP
    ii = lax.broadcasted_iota(jnp.int32, (CHUNK, CHUNK), 0)
    jj = lax.broadcasted_iota(jnp.int32, (CHUNK, CHUNK), 1)
    tri = (ii >= jj).astype(BF16)
    tri3 = jnp.concatenate([tri, tri, tri], axis=1)
    bias = bias_ref[...]
    a_row = a_ref[...]

    def body(k, carry):
        r = pl.ds(pl.multiple_of(k * CHUNK, CHUNK), CHUNK)
        raw = dt_ref[r, :] + bias
        dts = jnp.where(jj < 2 * SSD_HEADS, jnp.maximum(raw, 0.0) + jnp.log1p(jnp.exp(-jnp.abs(raw))), 0.0)
        adt = dts * a_row
        hi, mid, lo = _split3(adt)
        cf = _dot(tri3, jnp.concatenate([hi, mid, lo], axis=0))
        cr = cf[CHUNK - 1:CHUNK, :] - cf + adt
        valid = jj < 2 * SSD_HEADS
        cum = jnp.where(jj % GROUP_LANES < hpg, cf, cr)
        src_t = ((cum - jnp.log(jnp.where(valid, dts, 1.0))) * LOG2_E).T
        rt = pl.ds(pl.multiple_of(k * GROUP_LANES, GROUP_LANES), GROUP_LANES)
        for g in range(SSD_GROUPS):
            g0 = g * GROUP_LANES
            for src, dst in ((cum * LOG2_E, cum_ref), (dts, dts_ref)):
                lanes = src if g == 0 else pltpu.roll(src, LANES - g0, 1)
                dst[g, r, :] = jnp.where(jj < GROUP_LANES, lanes, 0.0)
            src_t_ref[g, rt, :] = src_t[g0:g0 + GROUP_LANES, :]
        return carry

    lax.fori_loop(0, dt_ref.shape[0] // CHUNK, body, 0)


def _ssd_dt(dt, bias_row, a_row, *, rows_per_step):
    m = dt.shape[0]
    row = pl.BlockSpec((1, LANES), lambda i: (0, 0))
    full = pl.BlockSpec((SSD_GROUPS, rows_per_step, LANES), lambda i: (0, i, 0))
    tr = pl.BlockSpec((SSD_GROUPS, rows_per_step // CHUNK * GROUP_LANES, LANES), lambda i: (0, i, 0))
    full_shape = jax.ShapeDtypeStruct((SSD_GROUPS, m, LANES), F32)
    tr_shape = jax.ShapeDtypeStruct((SSD_GROUPS, m // CHUNK * GROUP_LANES, LANES), F32)
    return pl.pallas_call(
        _ssd_dt_kernel,
        grid=(m // rows_per_step,),
        in_specs=[pl.BlockSpec((rows_per_step, LANES), lambda i: (i, 0)), row, row],
        out_specs=[full, full, tr],
        out_shape=[full_shape, full_shape, tr_shape],
        compiler_params=_params("parallel"),
        name="ssd_dt",
    )(dt, bias_row, a_row)


def _ssd_kernel(xs_ref, b_ref, c_ref, dts_ref, cum_ref, src_t_ref, cwx_ref, cbx_ref, cwb_ref, cbb_ref,
                cwc_ref, cbc_ref, dskip_ref, h0f_ref, h0b_ref, *rest,
                n_chunks, conv_tile, conv_period, with_output):
    if with_output:
        y_ref, xs_s, b_s, c_s, hf_s, hb_s, y_s, yb_s = rest
    else:
        hf_out, hb_out, xs_s, b_s, c_s, hf_s, hb_s = rest
    seq = n_chunks * CHUNK
    hpg = HEADS_PER_GROUP

    def conv_silu(src_ref, dst_ref, w_ref, bias_ref):
        cols = src_ref.shape[1]
        w1 = w_ref[1:2, :]
        bias = bias_ref[...]
        row = lax.broadcasted_iota(jnp.int32, (conv_tile, cols), 0) % conv_period
        w0 = jnp.where(row == 0, 0.0, w_ref[0:1, :])
        w2 = jnp.where(row == conv_period - 1, 0.0, w_ref[2:3, :])

        def body(k, carry):
            r = pl.ds(pl.multiple_of(k * conv_tile, conv_tile), conv_tile)
            v = src_ref[r, :].astype(F32)
            y = bias + pltpu.roll(v, 1, 0) * w0 + v * w1 + pltpu.roll(v, conv_tile - 1, 0) * w2
            dst_ref[r, :] = _silu(y).astype(BF16)
            return carry

        lax.fori_loop(0, seq // conv_tile, body, 0)

    conv_silu(xs_ref, xs_s, cwx_ref, cbx_ref)
    conv_silu(b_ref, b_s, cwb_ref, cbb_ref)
    conv_silu(c_ref, c_s, cwc_ref, cbc_ref)

    ii = lax.broadcasted_iota(jnp.int32, (CHUNK, CHUNK), 0)
    jj = lax.broadcasted_iota(jnp.int32, (CHUNK, CHUNK), 1)
    lower = ii >= jj
    upper = jj >= ii

    hf_s[...] = h0f_ref[...]
    hb_s[...] = h0b_ref[...]

    head_of_col = lax.broadcasted_iota(jnp.int32, (LANES, GROUP_WIDTH), 1) // SSD_HEAD_DIM
    sel_row = lax.broadcasted_iota(jnp.int32, (LANES, GROUP_WIDTH), 0)
    sel_valid = sel_row < STACKED_PIECES * GROUP_LANES
    sel_f = (sel_valid & (sel_row % GROUP_LANES == head_of_col)).astype(BF16)
    sel_b = (sel_valid & (sel_row % GROUP_LANES == head_of_col + hpg)).astype(BF16)
    low_half = jj < SSD_HEAD_DIM
    group_lane = jj < GROUP_LANES

    def direction_terms(r, sel):
        cum = cum_ref[r, :]
        dts = dts_ref[r, :]
        tot = jnp.where(jj[0:1, :] < hpg, cum[CHUNK - 1:CHUNK, :], cum[0:1, :])
        wst = dts * jnp.exp2(tot - cum)
        eoff = jnp.where(group_lane, jnp.exp2(cum), 0.0)
        edec = jnp.broadcast_to(jnp.where(group_lane[0:1, :], jnp.exp2(tot), 0.0), (8, LANES))
        ex = _dot(_stack_split(jnp.concatenate([wst, eoff, edec], axis=0)), sel)
        return cum, ex[:CHUNK], ex[CHUNK:2 * CHUNK], ex[2 * CHUNK:2 * CHUNK + 1]

    def state_step(h_s, r, sel):
        cum, w_state, e_off, e_dec = direction_terms(r, sel)
        xc = xs_s[r, :]
        xw = (xc.astype(F32) * w_state).astype(BF16)
        s_new = lax.dot_general(b_s[r, :], xw, (((0,), (0,)), ((), ())), preferred_element_type=F32)
        h = h_s[...]
        y_off = _dot(c_s[r, :], h.astype(BF16)) * e_off
        h_s[...] = h * e_dec + s_new
        return cum, xc, y_off

    def fwd_body(k, carry):
        r = pl.ds(pl.multiple_of(k * CHUNK, CHUNK), CHUNK)
        cum, xc, y_off = state_step(hf_s, r, sel_f)
        if with_output:
            g = _dot_nt(c_s[r, :], b_s[r, :])
            rt = pl.ds(pl.multiple_of(k * GROUP_LANES, GROUP_LANES), GROUP_LANES)
            src_t = src_t_ref[rt, :]
            pieces = []
            for pair in range(hpg // 2):
                ms = []
                for h in (2 * pair, 2 * pair + 1):
                    hb = h + hpg
                    mf = jnp.exp2(jnp.where(lower, cum[:, h:h + 1] - src_t[h:h + 1, :], NEG_BIG))
                    mb = jnp.exp2(jnp.where(upper, cum[:, hb:hb + 1] - src_t[hb:hb + 1, :], NEG_BIG))
                    ms.append((g * (mf + mb)).astype(BF16))
                lhs = jnp.concatenate(ms, axis=1)
                xp = xc[:, pair * LANES:(pair + 1) * LANES]
                zero = jnp.zeros_like(xp)
                rhs = jnp.concatenate([jnp.where(low_half, xp, zero), jnp.where(low_half, zero, xp)], axis=0)
                pieces.append(_dot(lhs, rhs))
            y_diag = jnp.concatenate(pieces, axis=1)
            y_s[r, :] = y_diag + y_off + xc.astype(F32) * dskip_ref[...]
        rb = pl.ds(pl.multiple_of((n_chunks - 1 - k) * CHUNK, CHUNK), CHUNK)
        _, _, y_off_b = state_step(hb_s, rb, sel_b)
        if with_output:
            yb_s[rb, :] = y_off_b
        return carry

    lax.fori_loop(0, n_chunks, fwd_body, 0, unroll=8 if n_chunks % 8 == 0 else 2)

    if with_output:
        def sum_body(k, carry):
            r = pl.ds(pl.multiple_of(k * CHUNK, CHUNK), CHUNK)
            y_ref[r, :] = (y_s[r, :] + yb_s[r, :]).astype(y_ref.dtype)
            return carry

        lax.fori_loop(0, n_chunks, sum_body, 0)

    if not with_output:
        hf_out[...] = hf_s[...]
        hb_out[...] = hb_s[...]


def _ssd(pm, xs_col0, bc_col0, dt_terms, conv_w, conv_b, dskip_g, h0f, h0b, *, batch, seq, conv_tile, conv_period,
         with_output):
    n_chunks = seq // CHUNK
    dts_g, cum_g, src_t_g = dt_terms
    dt_spec = pl.BlockSpec((None, seq, LANES), lambda b, g: (g, b, 0))
    dt_t_spec = pl.BlockSpec((None, n_chunks * GROUP_LANES, LANES), lambda b, g: (g, b, 0))
    xs_blk0 = xs_col0 // GROUP_WIDTH
    b_blk0 = bc_col0 // SSD_STATE
    c_blk0 = b_blk0 + SSD_GROUPS
    cw_b0 = SSD_WIDTH // SSD_STATE
    cw_c0 = cw_b0 + SSD_GROUPS
    state_spec = pl.BlockSpec((None, None, SSD_STATE, GROUP_WIDTH), lambda b, g: (b, g, 0, 0))
    in_specs = [
        pl.BlockSpec((seq, GROUP_WIDTH), lambda b, g: (b, xs_blk0 + g)),
        pl.BlockSpec((seq, SSD_STATE), lambda b, g: (b, b_blk0 + g)),
        pl.BlockSpec((seq, SSD_STATE), lambda b, g: (b, c_blk0 + g)),
        dt_spec,
        dt_spec,
        dt_t_spec,
        pl.BlockSpec((3, GROUP_WIDTH), lambda b, g: (0, g)),
        pl.BlockSpec((1, GROUP_WIDTH), lambda b, g: (0, g)),
        pl.BlockSpec((3, SSD_STATE), lambda b, g: (0, cw_b0 + g)),
        pl.BlockSpec((1, SSD_STATE), lambda b, g: (0, cw_b0 + g)),
        pl.BlockSpec((3, SSD_STATE), lambda b, g: (0, cw_c0 + g)),
        pl.BlockSpec((1, SSD_STATE), lambda b, g: (0, cw_c0 + g)),
        pl.BlockSpec((None, 1, GROUP_WIDTH), lambda b, g: (g, 0, 0)),
        state_spec,
        state_spec,
    ]
    scratch = [
        pltpu.VMEM((seq, GROUP_WIDTH), BF16),
        pltpu.VMEM((seq, SSD_STATE), BF16),
        pltpu.VMEM((seq, SSD_STATE), BF16),
        pltpu.VMEM((SSD_STATE, GROUP_WIDTH), F32),
        pltpu.VMEM((SSD_STATE, GROUP_WIDTH), F32),
    ]
    if with_output:
        out_specs = pl.BlockSpec((seq, GROUP_WIDTH), lambda b, g: (b, g))
        out_shape = jax.ShapeDtypeStruct((batch * seq, SSD_WIDTH), BF16)
        scratch += [pltpu.VMEM((seq, GROUP_WIDTH), F32)] * 2
    else:
        out_specs = [state_spec, state_spec]
        out_shape = [jax.ShapeDtypeStruct((batch, SSD_GROUPS, SSD_STATE, GROUP_WIDTH), F32)] * 2
    return pl.pallas_call(
        functools.partial(_ssd_kernel, n_chunks=n_chunks, conv_tile=conv_tile, conv_period=conv_period,
                          with_output=with_output),
        grid=(batch, SSD_GROUPS),
        in_specs=in_specs,
        out_specs=out_specs,
        out_shape=out_shape,
        scratch_shapes=scratch,
        compiler_params=_params("parallel", "parallel"),
        name="ssd_out" if with_output else "ssd_ctx",
    )(pm, pm, pm, dts_g, cum_g, src_t_g, conv_w, conv_b, conv_w, conv_b, conv_w, conv_b, dskip_g, h0f, h0b)


def _out_proj_kernel(yd_ref, z_ref, u_ref, v_ref, x_ref, g1_ref, ng_ref, lng_ref, lnb_ref, ws_ref, bst_ref, w_ref,
                     o_ref, mix_cur, mix_nxt):
    tm = yd_ref.shape[0]
    s = pl.program_id(0)
    n = pl.num_programs(0) - 1

    def build_steps():
        steps = []
        for k in range(tm // CHUNK):
            r = pl.ds(k * CHUNK, CHUNK)
            shared = {}

            def ssd_part(r=r):
                a = yd_ref[r, :].astype(F32) * z_ref[r, :].astype(F32)
                ms = jnp.mean(a * a, axis=-1, keepdims=True)
                mix_nxt[r, 0:SSD_WIDTH] = (a * lax.rsqrt(ms + NORM_EPS) * ng_ref[...]).astype(BF16)

            def ln_part(r=r, shared=shared):
                gv = v_ref[r, :].astype(F32)
                mu = jnp.mean(gv, axis=-1, keepdims=True)
                xc = gv - mu
                var = jnp.mean(xc * xc, axis=-1, keepdims=True)
                shared["ln"] = ((xc * lax.rsqrt(var + NORM_EPS)) * lng_ref[...] + lnb_ref[...]).astype(BF16)

            def gate_part(h0, h1, r=r, shared=shared):
                for h in range(h0, h1):
                    c0, c1 = h * CM_HEAD_DIM, (h + 1) * CM_HEAD_DIM
                    sp = _dot(ws_ref[h], shared["ln"][:, c0:c1]) + bst_ref[:, h:h + 1]
                    gu = u_ref[r, c0:c1].astype(F32)
                    mix_nxt[r, SSD_WIDTH + c0:SSD_WIDTH + c1] = (gu * sp).astype(BF16)

            steps += [ssd_part, ln_part, functools.partial(gate_part, 0, CM_HEADS // 2),
                      functools.partial(gate_part, CM_HEADS // 2, CM_HEADS)]
        return steps

    n_proj = 8
    pw = D_MODEL // n_proj

    def project_steps():
        def piece(q):
            c = slice(q * pw, (q + 1) * pw)
            o_ref[:, c] = x_ref[:, c] + g1_ref[:, c] * _dot(mix_cur[...], w_ref[:, c])

        return [functools.partial(piece, q) for q in range(n_proj)]

    @pl.when(s == 0)
    def _():
        for step in build_steps():
            step()

    @pl.when((s > 0) & (s < n))
    def _():
        for proj, build in zip(project_steps(), build_steps()):
            proj()
            build()

    @pl.when(s == n)
    def _():
        for step in project_steps():
            step()

    @pl.when(s < n)
    def _():
        mix_cur[...] = mix_nxt[...]


def _out_proj(yd, pm, x2, mod3, ssd_norm_g, ln_g, ln_b, ws, bst, w_out, *, seq, tm):
    m = x2.shape[0]
    n = m // tm
    tiles_per_batch = seq // tm
    z_blk, u_blk, v_blk = 0, 2 * SSD_WIDTH // CM_WIDTH, 2 * SSD_WIDTH // CM_WIDTH + 1
    g1_blk = 2
    row = lambda shape: pl.BlockSpec(shape, lambda s: (0, 0))
    build_tile = lambda s: jnp.minimum(s, n - 1)
    proj_tile = lambda s: jnp.maximum(s - 1, 0)
    return pl.pallas_call(
        _out_proj_kernel,
        grid=(n + 1,),
        in_specs=[
            pl.BlockSpec((tm, SSD_WIDTH), lambda s: (build_tile(s), 0)),
            pl.BlockSpec((tm, SSD_WIDTH), lambda s: (build_tile(s), z_blk)),
            pl.BlockSpec((tm, CM_WIDTH), lambda s: (build_tile(s), u_blk)),
            pl.BlockSpec((tm, CM_WIDTH), lambda s: (build_tile(s), v_blk)),
            pl.BlockSpec((tm, D_MODEL), lambda s: (proj_tile(s), 0)),
            pl.BlockSpec((None, 1, D_MODEL), lambda s: (proj_tile(s) // tiles_per_batch, 0, g1_blk)),
            row((1, SSD_WIDTH)),
            row((1, CM_WIDTH)),
            row((1, CM_WIDTH)),
            pl.BlockSpec((CM_HEADS, CHUNK, CHUNK), lambda s: (0, 0, 0)),
            row((CHUNK, CM_HEADS)),
            pl.BlockSpec((SSD_WIDTH + CM_WIDTH, D_MODEL), lambda s: (0, 0), pipeline_mode=pl.Buffered(1)),
        ],
        out_specs=pl.BlockSpec((tm, D_MODEL), lambda s: (proj_tile(s), 0)),
        out_shape=jax.ShapeDtypeStruct((m, D_MODEL), F32),
        scratch_shapes=[pltpu.VMEM((tm, SSD_WIDTH + CM_WIDTH), BF16)] * 2,
        compiler_params=_params("arbitrary"),
        name="out_proj",
    )(yd, pm, pm, pm, x2, mod3, ssd_norm_g, ln_g, ln_b, ws, bst, w_out)


def _route(logits):
    lane = lax.broadcasted_iota(jnp.int32, logits.shape, 1)
    lane_f = lane.astype(F32)
    is_group = (lane >= N_EXPERTS) & (lane < N_EXPERTS + N_GROUPS)
    lg = jnp.where(is_group, logits, NEG_BIG)
    mg = jnp.max(lg, axis=1, keepdims=True)
    top_pg = 1.0 / jnp.sum(jnp.exp(lg - mg), axis=1, keepdims=True)
    gi = jnp.min(jnp.where(lg == mg, lane_f, 1e9), axis=1, keepdims=True) - N_EXPERTS
    in_group = (lane < N_EXPERTS) & ((lane // EXPERTS_PER_GROUP).astype(F32) == gi)
    le = jnp.where(in_group, logits, NEG_BIG)
    m1 = jnp.max(le, axis=1, keepdims=True)
    i1 = jnp.min(jnp.where(le == m1, lane_f, 1e9), axis=1, keepdims=True)
    le2 = jnp.where(lane_f == i1, NEG_BIG, le)
    m2 = jnp.max(le2, axis=1, keepdims=True)
    i2 = jnp.min(jnp.where(le2 == m2, lane_f, 1e9), axis=1, keepdims=True)
    e2 = jnp.exp(m2 - m1)
    p1 = 1.0 / (1.0 + e2)
    p2 = e2 * p1
    return i1, i2, p1 * top_pg, p2 * top_pg


RINFO_E1, RINFO_E2, RINFO_W1, RINFO_W2, RINFO_R1, RINFO_R2 = range(6)


def _norm2_modulate(x1, gain, scale, shift):
    ms = jnp.mean(x1 * x1, axis=-1, keepdims=True)
    return (x1 * lax.rsqrt(ms + NORM_EPS) * gain) * (1.0 + scale) + shift


def _route_kernel(x1_ref, n2g_ref, sh2_ref, sc2_ref, wr_ref, br_ref, rinfo_ref, counts_ref, cnt_scr):
    tm = x1_ref.shape[0]

    @pl.when(pl.program_id(0) == 0)
    def _():
        cnt_scr[...] = jnp.zeros_like(cnt_scr)

    h2 = _norm2_modulate(x1_ref[...], n2g_ref[...], sc2_ref[...], sh2_ref[...])
    hi, mid, _ = _split3(h2)
    p = _dot(hi, wr_ref[...])
    logits = p[:, :LANES] + p[:, LANES:] + _dot(mid, wr_ref[:, :LANES]) + br_ref[...]
    i1, i2, w1, w2 = _route(logits)

    lane = lax.broadcasted_iota(jnp.int32, (tm, LANES), 1)
    lane_f = lane.astype(F32)
    oh1 = jnp.where(lane_f == i1, 1.0, 0.0)
    oh2 = jnp.where(lane_f == i2, 1.0, 0.0)
    before = (lax.broadcasted_iota(jnp.int32, (tm, tm), 0) > lax.broadcasted_iota(jnp.int32, (tm, tm), 1)).astype(BF16)
    carried = cnt_scr[...]
    tot1 = jnp.sum(oh1, axis=0, keepdims=True)
    r1 = jnp.sum(oh1 * (_dot(before, oh1.astype(BF16)) + carried), axis=1, keepdims=True)
    r2 = jnp.sum(oh2 * (_dot(before, oh2.astype(BF16)) + (carried + tot1)), axis=1, keepdims=True)
    counts = carried + tot1 + jnp.sum(oh2, axis=0, keepdims=True)
    cnt_scr[...] = counts
    counts_ref[...] = jnp.broadcast_to(counts, counts_ref.shape)

    info = jnp.zeros((tm, LANES), F32)
    for k, v in ((RINFO_E1, i1), (RINFO_E2, i2), (RINFO_W1, w1), (RINFO_W2, w2), (RINFO_R1, r1), (RINFO_R2, r2)):
        info = jnp.where(lane == k, v, info)
    rinfo_ref[...] = info


def _route_call(x1, mod3, norm2_g, w_router, b_router, *, seq, tm):
    m = x1.shape[0]
    tiles_per_batch = seq // tm
    row = lambda shape: pl.BlockSpec(shape, lambda i: (0, 0))
    modrow = lambda k: pl.BlockSpec((None, 1, D_MODEL), lambda i: (i // tiles_per_batch, 0, k))
    return pl.pallas_call(
        _route_kernel,
        grid=(m // tm,),
        in_specs=[
            pl.BlockSpec((tm, D_MODEL), lambda i: (i, 0)),
            row((1, D_MODEL)),
            modrow(3),
            modrow(4),
            row((D_MODEL, 2 * LANES)),
            row((1, LANES)),
        ],
        out_specs=[
            pl.BlockSpec((tm, LANES), lambda i: (i, 0)),
            row((8, LANES)),
        ],
        out_shape=[
            jax.ShapeDtypeStruct((m, LANES), F32),
            jax.ShapeDtypeStruct((8, LANES), F32),
        ],
        scratch_shapes=[pltpu.VMEM((1, LANES), F32)],
        compiler_params=_params("arbitrary"),
        name="route",
    )(x1, norm2_g, mod3, mod3, w_router, b_router)


def _dispatch_kernel(lt_ref, nu_ref, pos_ref, x1_ref, n2g_ref, sh2_ref, sc2_ref, x_hbm, zbuf, h2_ref, zsem, sem, *,
                     tile_rows, n_tiles):
    tm = h2_ref.shape[0]
    n_used = nu_ref[0]
    rc = 256
    for k in range(tm // rc):
        r = pl.ds(k * rc, rc)
        h2_ref[r, :] = _norm2_modulate(x1_ref[r, :], n2g_ref[...], sc2_ref[...], sh2_ref[...])

    def zero_copy(t):
        return pltpu.make_async_copy(zbuf, x_hbm.at[pl.ds(pl.multiple_of(t * tile_rows, tile_rows), tile_rows), :], zsem)

    @pl.when(pl.program_id(0) == 0)
    def _():
        zbuf[...] = jnp.zeros_like(zbuf)
        for phase in ("start", "wait"):
            for e in range(N_EXPERTS):
                @pl.when(lt_ref[e] >= 0)
                def _():
                    getattr(zero_copy(lt_ref[e]), phase)()

            def unused(t, carry):
                getattr(zero_copy(t), phase)()
                return carry

            lax.fori_loop(n_used, n_tiles, unused, 0)

    def body(r, carry):
        for k in range(2):
            pltpu.make_async_copy(h2_ref.at[pl.ds(r, 1), :], x_hbm.at[pl.ds(pos_ref[0, 2 * r + k], 1), :],
                                  sem).start(priority=k)
        return carry

    lax.fori_loop(0, tm, body, 0, unroll=8)
    for k in range(2):
        pltpu.make_async_copy(h2_ref, x_hbm.at[pl.ds(0, tm), :], sem).wait()


def _dispatch_call(last_tile, n_used, pos, x1, mod3, norm2_g, *, seq, tm, tile_rows, n_tiles):
    m = x1.shape[0]
    tiles_per_batch = seq // tm
    modrow = lambda k: pl.BlockSpec((None, 1, D_MODEL), lambda i, lt, nu: (i // tiles_per_batch, 0, k))
    grid_spec = pltpu.PrefetchScalarGridSpec(
        num_scalar_prefetch=2,
        grid=(m // tm,),
        in_specs=[
            pl.BlockSpec((None, 1, 2 * tm), lambda i, lt, nu: (i, 0, 0), memory_space=pltpu.SMEM),
            pl.BlockSpec((tm, D_MODEL), lambda i, lt, nu: (i, 0)),
            pl.BlockSpec((1, D_MODEL), lambda i, lt, nu: (0, 0)),
            modrow(3),
            modrow(4),
        ],
        out_specs=pl.BlockSpec(memory_space=pl.ANY),
        scratch_shapes=[
            pltpu.VMEM((tile_rows, D_MODEL), F32),
            pltpu.VMEM((tm, D_MODEL), F32),
            pltpu.SemaphoreType.DMA(()),
            pltpu.SemaphoreType.DMA(()),
        ],
    )
    return pl.pallas_call(
        functools.partial(_dispatch_kernel, tile_rows=tile_rows, n_tiles=n_tiles),
        grid_spec=grid_spec,
        out_shape=jax.ShapeDtypeStruct((n_tiles * tile_rows, D_MODEL), F32),
        compiler_params=_params("arbitrary"),
        name="dispatch",
    )(last_tile, n_used, pos, x1, norm2_g, mod3, mod3)


def _experts_kernel(te_ref, nu_ref, ord_ref, nxt_ref, x_ref, wg_hbm, wu_hbm, wd_hbm, y_ref,
                    wg_f, wu_f, wd_f, wg_b, wu_b, wd_b, sem):
    j = pl.program_id(0)
    n_used = nu_ref[0]

    def weight_copies(e, slot):
        return [pltpu.make_async_copy(src.at[e], dst.at[slot], sem.at[slot])
                for src, dst in ((wg_hbm, wg_f), (wu_hbm, wu_f), (wd_hbm, wd_f))]

    @pl.when(j == 0)
    def _():
        for cp in weight_copies(te_ref[0], 0):
            cp.start()

    @pl.when((j < n_used) & ((j == 0) | (te_ref[j] != te_ref[jnp.maximum(j - 1, 0)])))
    def _():
        slot = ord_ref[j] % 2
        for cp in weight_copies(te_ref[j], slot):
            cp.wait()

        @pl.when(nxt_ref[j] >= 0)
        def _():
            for cp in weight_copies(nxt_ref[j], 1 - slot):
                cp.start()

        wg_b[...] = wg_f[slot].astype(BF16)
        wu_b[...] = wu_f[slot].astype(BF16)
        wd_b[...] = wd_f[slot].astype(BF16)

    @pl.when(j < n_used)
    def _():
        xt = x_ref[...].astype(BF16)
        hid = (_silu(_dot(xt, wg_b[...])) * _dot(xt, wu_b[...])).astype(BF16)
        y_ref[...] = _dot(hid, wd_b[...])

    @pl.when(j >= n_used)
    def _():
        y_ref[...] = jnp.zeros_like(y_ref)


def _experts_call(tile_expert, n_used, tile_ord, next_expert, x_sorted, wg, wu, wd, *, tm):
    n_tiles = tile_expert.shape[0]
    hbm = pl.BlockSpec(memory_space=pl.ANY)
    grid_spec = pltpu.PrefetchScalarGridSpec(
        num_scalar_prefetch=4,
        grid=(n_tiles,),
        in_specs=[
            pl.BlockSpec((tm, D_MODEL), lambda j, te, nu, od, nx: (jnp.minimum(j, nu[0] - 1), 0)),
            hbm,
            hbm,
            hbm,
        ],
        out_specs=pl.BlockSpec((tm, D_MODEL), lambda j, te, nu, od, nx: (j, 0)),
        scratch_shapes=[
            pltpu.VMEM((2, D_MODEL, EXPERT_FF), F32),
            pltpu.VMEM((2, D_MODEL, EXPERT_FF), F32),
            pltpu.VMEM((2, EXPERT_FF, D_MODEL), F32),
            pltpu.VMEM((D_MODEL, EXPERT_FF), BF16),
            pltpu.VMEM((D_MODEL, EXPERT_FF), BF16),
            pltpu.VMEM((EXPERT_FF, D_MODEL), BF16),
            pltpu.SemaphoreType.DMA((2,)),
        ],
    )
    return pl.pallas_call(
        _experts_kernel,
        grid_spec=grid_spec,
        out_shape=jax.ShapeDtypeStruct((n_tiles * tm, D_MODEL), F32),
        compiler_params=_params("arbitrary"),
        name="experts",
    )(tile_expert, n_used, tile_ord, next_expert, x_sorted, wg, wu, wd)


def _combine_kernel(pos_cur, pos_nxt, x1_ref, g2_ref, nfg_ref, rinfo_ref, y_hbm, o_ref, ybuf, sem):
    i = pl.program_id(0)
    tm = x1_ref.shape[0]
    slot = i % 2

    def row_copy(pos_ref, r, k, s):
        return pltpu.make_async_copy(y_hbm.at[pl.ds(pos_ref[0, 2 * r + k], 1), :], ybuf.at[s, k, pl.ds(r, 1), :],
                                     sem.at[s])

    def wait_tile(s):
        for k in range(2):
            pltpu.make_async_copy(y_hbm.at[pl.ds(0, tm), :], ybuf.at[s, k], sem.at[s]).wait()

    @pl.when(i == 0)
    def _():
        def body(r, carry):
            for k in range(2):
                row_copy(pos_cur, r, k, 0).start()
            return carry

        lax.fori_loop(0, tm, body, 0)

    def step(cur):
        wait_tile(cur)
        for r in range(tm):
            for k in range(2):
                row_copy(pos_nxt, r, k, 1 - cur).start(priority=k)
        info = rinfo_ref[...]
        w1 = info[:, RINFO_W1:RINFO_W1 + 1]
        w2 = info[:, RINFO_W2:RINFO_W2 + 1]
        moe = w1 * ybuf[cur, 0] + w2 * ybuf[cur, 1]
        y = x1_ref[...] + g2_ref[...] * moe
        ms = jnp.mean(y * y, axis=-1, keepdims=True)
        o_ref[...] = y * lax.rsqrt(ms + NORM_EPS) * nfg_ref[...]

        @pl.when(i == pl.num_programs(0) - 1)
        def _():
            wait_tile(1 - cur)

    for cur in range(2):
        pl.when(slot == cur)(functools.partial(step, cur))


def _combine_call(pos, x1, mod3, normf_g, rinfo, y_sorted, *, seq, tm):
    m = x1.shape[0]
    n_tiles = m // tm
    tiles_per_batch = seq // tm
    return pl.pallas_call(
        _combine_kernel,
        grid=(n_tiles,),
        in_specs=[
            pl.BlockSpec((None, 1, 2 * tm), lambda i: (i, 0, 0), memory_space=pltpu.SMEM),
            pl.BlockSpec((None, 1, 2 * tm), lambda i: (jnp.minimum(i + 1, n_tiles - 1), 0, 0),
                         memory_space=pltpu.SMEM),
            pl.BlockSpec((tm, D_MODEL), lambda i: (i, 0)),
            pl.BlockSpec((None, 1, D_MODEL), lambda i: (i // tiles_per_batch, 0, 5)),
            pl.BlockSpec((1, D_MODEL), lambda i: (0, 0)),
            pl.BlockSpec((tm, LANES), lambda i: (i, 0)),
            pl.BlockSpec(memory_space=pl.ANY),
        ],
        out_specs=pl.BlockSpec((tm, D_MODEL), lambda i: (i, 0)),
        out_shape=jax.ShapeDtypeStruct((m, D_MODEL), F32),
        scratch_shapes=[
            pltpu.VMEM((2, 2, tm, D_MODEL), F32),
            pltpu.SemaphoreType.DMA((2,)),
        ],
        compiler_params=_params("arbitrary"),
        name="combine",
    )(pos, pos, x1, mod3, normf_g, rinfo, y_sorted)


def _dispatch_plan(rinfo, counts, *, tm):
    n_tok = rinfo.shape[0]
    n_tiles = (2 * n_tok + N_EXPERTS * (tm - 1)) // tm + 1
    expert = rinfo[:, RINFO_E1:RINFO_E2 + 1].astype(jnp.int32)
    rank = rinfo[:, RINFO_R1:RINFO_R2 + 1].astype(jnp.int32)
    cnt = counts[0, :N_EXPERTS].astype(jnp.int32)
    tiles_e = (cnt + tm - 1) // tm
    end_tile = jnp.cumsum(tiles_e)
    start_row = (end_tile - tiles_e) * tm
    is_e = expert[:, :, None] == jnp.arange(N_EXPERTS, dtype=jnp.int32)
    pos = jnp.sum(jnp.where(is_e, start_row, 0), axis=-1) + rank
    n_used = end_tile[-1:]
    tile_ids = jnp.minimum(jnp.arange(n_tiles, dtype=jnp.int32), n_used - 1)
    tile_expert = jnp.sum(tile_ids[:, None] >= end_tile[None, :], axis=1).astype(jnp.int32)
    last_tile = jnp.where(tiles_e > 0, end_tile - 1, -1).astype(jnp.int32)
    used = (tiles_e > 0).astype(jnp.int32)
    ord_e = jnp.cumsum(used) - used
    ids = jnp.arange(N_EXPERTS, dtype=jnp.int32)
    later = jnp.where((ids[None, :] > ids[:, None]) & (used[None, :] > 0), ids[None, :], N_EXPERTS)
    nxt_e = jnp.min(later, axis=1)
    nxt_e = jnp.where(nxt_e < N_EXPERTS, nxt_e, -1).astype(jnp.int32)
    is_te = tile_expert[:, None] == ids[None, :]
    tile_ord = jnp.sum(jnp.where(is_te, ord_e, 0), axis=1).astype(jnp.int32)
    next_expert = jnp.sum(jnp.where(is_te, nxt_e, 0), axis=1).astype(jnp.int32)
    return pos, (tile_expert, n_used.astype(jnp.int32), tile_ord, next_expert), last_tile, n_tiles


def kernel(x, c, ctx, c_ctx, w_mod, b_mod, norm1_g, w_in, conv_w, conv_b, dt_bias_f, dt_bias_b, a_log_f, a_log_b,
           d_skip, ssd_norm_g, cm_ln_g, cm_ln_b, w_spatial, b_spatial, w_out, norm2_g, w_router_group,
           b_router_group, w_router_expert, b_router_expert, w_exp_gate, w_exp_up, w_exp_down, normf_g):
    bsz, seq, _ = x.shape
    ctx_len = ctx.shape[1]
    i = 0

    cc = jnp.concatenate([c, c_ctx[None, :], jnp.zeros((MOD_ROWS - bsz - 1, D_MODEL), F32)], axis=0)
    mod = _modulation(cc, w_mod[i], b_mod[i][None, :])
    mod3 = mod.reshape(MOD_ROWS, 1, N_MOD * D_MODEL)

    w_main, w_dt = _w_in_prep(jnp.swapaxes(w_in[i], 0, 1))
    g1row = norm1_g[i][None, :]

    x2 = x.reshape(bsz * seq, D_MODEL)
    ctx2 = ctx.reshape(bsz * ctx_len, D_MODEL)
    tm_in = 1024
    tn_in = 1024
    xs_col0, bc_col0 = SSD_WIDTH, 2 * SSD_WIDTH + 2 * CM_WIDTH
    pm_x, dt_x = _in_proj(x2, mod3, lambda t: t // (seq // tm_in), g1row, w_main, w_dt, tm_in, tn_in)
    ctx_blocks = tuple(range(xs_col0 // tn_in, 2 * SSD_WIDTH // tn_in)) + (bc_col0 // tn_in,)
    pm_c, dt_c = _in_proj(ctx2, mod3, lambda t: bsz, g1row, w_main, w_dt, bsz * ctx_len, tn_in, ctx_blocks)

    def dt_lanes(fwd, bwd):
        both = jnp.stack([fwd.reshape(SSD_GROUPS, HEADS_PER_GROUP), bwd.reshape(SSD_GROUPS, HEADS_PER_GROUP)], axis=1)
        return jnp.pad(both.reshape(-1).astype(F32), (0, LANES - 2 * SSD_HEADS))[None, :]

    dt_bias_row = dt_lanes(dt_bias_f[i], dt_bias_b[i])
    a_row = dt_lanes(-jnp.exp(a_log_f[i].astype(F32)), -jnp.exp(a_log_b[i].astype(F32)))
    dskip_g = jnp.repeat(d_skip[i], SSD_HEAD_DIM).reshape(SSD_GROUPS, 1, GROUP_WIDTH)
    cw = conv_w[i]
    cb = conv_b[i][None, :]
    h_zero = jnp.zeros((bsz, SSD_GROUPS, SSD_STATE, GROUP_WIDTH), F32)

    dt_terms_c = _ssd_dt(dt_c, dt_bias_row, a_row, rows_per_step=bsz * ctx_len)
    dt_terms_x = _ssd_dt(dt_x, dt_bias_row, a_row, rows_per_step=1024)
    hc_f, hc_b = _ssd(pm_c, 0, SSD_WIDTH, dt_terms_c, cw, cb, dskip_g, h_zero, h_zero,
                      batch=bsz, seq=ctx_len, conv_tile=ctx_len, conv_period=ctx_len, with_output=False)
    yd = _ssd(pm_x, xs_col0, bc_col0, dt_terms_x, cw, cb, dskip_g, hc_f, hc_b,
              batch=bsz, seq=seq, conv_tile=CHUNK, conv_period=GRID_W, with_output=True)

    x1 = _out_proj(yd, pm_x, x2, mod3, ssd_norm_g[i][None, :], cm_ln_g[i][None, :], cm_ln_b[i][None, :],
                   w_spatial[i].astype(BF16), b_spatial[i].T, w_out[i].astype(BF16), seq=seq, tm=256)

    w_re = jnp.transpose(w_router_expert[i], (1, 0, 2)).reshape(D_MODEL, N_EXPERTS)
    pad = LANES - N_EXPERTS - N_GROUPS
    w_router = jnp.pad(jnp.concatenate([w_re, w_router_group[i]], axis=1), ((0, 0), (0, pad)))
    w_router_hi = w_router.astype(BF16)
    w_router = jnp.concatenate([w_router_hi, (w_router - w_router_hi.astype(F32)).astype(BF16)], axis=1)
    b_router = jnp.pad(jnp.concatenate([b_router_expert[i].reshape(-1), b_router_group[i]]), (0, pad))[None, :]
    rinfo, counts = _route_call(x1, mod3, norm2_g[i][None, :], w_router, b_router, seq=seq, tm=512)

    tm_e = 256
    pos, tile_tables, last_tile, n_tiles = _dispatch_plan(rinfo, counts, tm=tm_e)
    tm_d = 512
    x_sorted = _dispatch_call(last_tile, tile_tables[1], pos.reshape(bsz * seq // tm_d, 1, 2 * tm_d), x1, mod3,
                              norm2_g[i][None, :], seq=seq, tm=tm_d, tile_rows=tm_e, n_tiles=n_tiles)
    wg = w_exp_gate[i].reshape(N_EXPERTS, D_MODEL, EXPERT_FF)
    wu = w_exp_up[i].reshape(N_EXPERTS, D_MODEL, EXPERT_FF)
    wd = w_exp_down[i].reshape(N_EXPERTS, EXPERT_FF, D_MODEL)
    y_sorted = _experts_call(*tile_tables, x_sorted, wg, wu, wd, tm=tm_e)
    tm_c = 256
    out = _combine_call(pos.reshape(bsz * seq // tm_c, 1, 2 * tm_c), x1, mod3, normf_g[None, :], rinfo, y_sorted,
                        seq=seq, tm=tm_c)
    return out.reshape(bsz, seq, D_MODEL)
```

```python
import functools

import jax
import jax.numpy as jnp
from jax import lax
from jax.experimental import pallas as pl
from jax.experimental.pallas import tpu as pltpu

F32 = jnp.float32
BF16 = jnp.bfloat16
HIGHEST = lax.Precision.HIGHEST

D_MODEL = 2048
GRID_W = 64
SSD_WIDTH = 2048
CM_WIDTH = 2048
SSD_HEADS = 32
SSD_HEAD_DIM = 64
SSD_GROUPS = 4
HEADS_PER_GROUP = SSD_HEADS // SSD_GROUPS
GROUP_WIDTH = HEADS_PER_GROUP * SSD_HEAD_DIM
SSD_STATE = 128
CHUNK = 128
BC_WIDTH = SSD_GROUPS * SSD_STATE
CM_HEADS = 8
CM_HEAD_DIM = CM_WIDTH // CM_HEADS
N_GROUPS = 4
N_EXPERTS = 32
EXPERTS_PER_GROUP = 8
EXPERT_FF = 512
N_MOD = 6
NORM_EPS = 1e-6
LANES = 128
NEG_BIG = -1e30
MOD_ROWS = 8
VMEM_LIMIT = 56 * 1024 * 1024


def _params(*sem):
    return pltpu.CompilerParams(dimension_semantics=sem, vmem_limit_bytes=VMEM_LIMIT)


LOG2_E = 1.4426950408889634
GELU_C = 0.7978845608028654
GELU_K1 = -2.0 * GELU_C * LOG2_E
GELU_K3 = GELU_K1 * 0.044715


def _silu(v):
    return v / (1.0 + jnp.exp2(v * (-LOG2_E)))


def _gelu_tanh(v):
    return v / (1.0 + jnp.exp2(v * (GELU_K1 + GELU_K3 * (v * v))))


def _dot(a, b):
    return jnp.dot(a, b, preferred_element_type=F32)


def _dot_nt(a, b):
    return lax.dot_general(a, b, (((1,), (1,)), ((), ())), preferred_element_type=F32)


def _mod_kernel(cc_ref, w_ref, b_ref, o_ref):
    a = _silu(cc_ref[...])
    hi = a.astype(BF16).astype(F32)
    lhs = jnp.concatenate([hi, a - hi], axis=0).astype(BF16)
    r = _dot(lhs, w_ref[...].astype(BF16))
    o_ref[...] = r[:MOD_ROWS] + r[MOD_ROWS:] + b_ref[...]


def _modulation(cc, w_mod, b_mod):
    n = w_mod.shape[1]
    tn = 1024
    return pl.pallas_call(
        _mod_kernel,
        grid=(n // tn,),
        in_specs=[
            pl.BlockSpec((MOD_ROWS, D_MODEL), lambda j: (0, 0)),
            pl.BlockSpec((D_MODEL, tn), lambda j: (0, j)),
            pl.BlockSpec((1, tn), lambda j: (0, j)),
        ],
        out_specs=pl.BlockSpec((MOD_ROWS, tn), lambda j: (0, j)),
        out_shape=jax.ShapeDtypeStruct((MOD_ROWS, n), F32),
        compiler_params=_params("arbitrary"),
        name="modulation",
    )(cc, w_mod, b_mod)


PREP_ROWS = 512
PREP_UV_BLK0 = 2 * SSD_WIDTH // PREP_ROWS
PREP_BC_BLK0 = (2 * SSD_WIDTH + 2 * CM_WIDTH) // PREP_ROWS
PREP_N_BLKS = PREP_BC_BLK0 + 2 * BC_WIDTH // PREP_ROWS
DT_COL0 = 2 * SSD_WIDTH + 2 * BC_WIDTH
UV_COL0 = DT_COL0 + 2 * SSD_HEADS


def _w_in_prep_kernel(a_ref, dt_ref, o_ref, wdt_ref):
    @pl.when(pl.program_id(0) == 0)
    def _():
        hpg = HEADS_PER_GROUP
        parts = []
        for g in range(SSD_GROUPS):
            parts += [dt_ref[hpg * g:hpg * (g + 1), :], dt_ref[SSD_HEADS + hpg * g:SSD_HEADS + hpg * (g + 1), :]]
        parts.append(dt_ref[2 * SSD_HEADS:, :])
        wdt_ref[...] = jnp.concatenate(parts, axis=0).astype(BF16)

    o_ref[...] = a_ref[...].astype(BF16)


def _w_in_prep(w_in_t):
    def src_row(j):
        uv = UV_COL0 + (j - PREP_UV_BLK0) * PREP_ROWS
        bc = 2 * SSD_WIDTH + (j - PREP_BC_BLK0) * PREP_ROWS
        return pl.multiple_of(jnp.where(j < PREP_UV_BLK0, j * PREP_ROWS, jnp.where(j < PREP_BC_BLK0, uv, bc)), 8)

    return pl.pallas_call(
        _w_in_prep_kernel,
        grid=(PREP_N_BLKS,),
        in_specs=[
            pl.BlockSpec((pl.Element(PREP_ROWS), pl.Element(D_MODEL)), lambda j: (src_row(j), 0)),
            pl.BlockSpec((LANES, D_MODEL), lambda j: (DT_COL0 // LANES, 0)),
        ],
        out_specs=[
            pl.BlockSpec((PREP_ROWS, D_MODEL), lambda j: (j, 0)),
            pl.BlockSpec((LANES, D_MODEL), lambda j: (0, 0)),
        ],
        out_shape=[
            jax.ShapeDtypeStruct((PREP_N_BLKS * PREP_ROWS, D_MODEL), BF16),
            jax.ShapeDtypeStruct((LANES, D_MODEL), BF16),
        ],
        compiler_params=_params("arbitrary"),
        name="w_in_prep",
    )(w_in_t, w_in_t)


ACT_NONE, ACT_SILU, ACT_GELU = 0, 1, 2


def _in_proj_kernel(x_ref, g_ref, sh_ref, sc_ref, w_ref, wdt_ref, o_ref, dt_ref, h_scr, *, block_acts):
    tm = x_ref.shape[0]
    rc = 256

    @pl.when(pl.program_id(1) == 0)
    def _():
        gain = g_ref[...]
        scale = 1.0 + sc_ref[...]
        shift = sh_ref[...]

        def body(k, carry):
            r = pl.ds(pl.multiple_of(k * rc, rc), rc)
            xv = x_ref[r, :]
            ms = jnp.mean(xv * xv, axis=-1, keepdims=True)
            h = (xv * lax.rsqrt(ms + NORM_EPS) * gain) * scale + shift
            h_scr[r, :] = h.astype(BF16)
            return carry

        lax.fori_loop(0, tm // rc, body, 0)
        dt_ref[...] = _dot_nt(h_scr[...], wdt_ref[...])

    j = pl.program_id(1)
    for act, fn in ((ACT_NONE, lambda v: v), (ACT_SILU, _silu), (ACT_GELU, _gelu_tanh)):
        blocks = [q for q, a in enumerate(block_acts) if a == act]
        if blocks:
            is_act = functools.reduce(jnp.logical_or, [j == q for q in blocks])

            @pl.when(is_act)
            def _():
                o_ref[...] = fn(_dot_nt(h_scr[...], w_ref[...])).astype(o_ref.dtype)


def _in_proj(x2, mod3, mod_row_of_tile, norm_g, w_main, w_dt, tm, tn, w_blocks=None):
    m = x2.shape[0]
    if w_blocks is None:
        w_blocks = tuple(range(w_main.shape[0] // tn))
    n = len(w_blocks) * tn
    z_end, uv0, uv_end = SSD_WIDTH // tn, 2 * SSD_WIDTH // tn, (2 * SSD_WIDTH + 2 * CM_WIDTH) // tn
    block_acts = tuple(ACT_SILU if b < z_end else ACT_GELU if uv0 <= b < uv_end else ACT_NONE for b in w_blocks)

    def w_blk(j):
        if w_blocks == tuple(range(len(w_blocks))):
            return j
        blk = jnp.int32(w_blocks[-1])
        for q in range(len(w_blocks) - 2, -1, -1):
            blk = jnp.where(j == q, w_blocks[q], blk)
        return blk

    return pl.pallas_call(
        functools.partial(_in_proj_kernel, block_acts=block_acts),
        grid=(m // tm, n // tn),
        in_specs=[
            pl.BlockSpec((tm, D_MODEL), lambda i, j: (i, 0)),
            pl.BlockSpec((1, D_MODEL), lambda i, j: (0, 0)),
            pl.BlockSpec((None, 1, D_MODEL), lambda i, j: (mod_row_of_tile(i), 0, 0)),
            pl.BlockSpec((None, 1, D_MODEL), lambda i, j: (mod_row_of_tile(i), 0, 1)),
            pl.BlockSpec((tn, D_MODEL), lambda i, j: (w_blk(j), 0)),
            pl.BlockSpec((LANES, D_MODEL), lambda i, j: (0, 0)),
        ],
        out_specs=[
            pl.BlockSpec((tm, tn), lambda i, j: (i, j)),
            pl.BlockSpec((tm, LANES), lambda i, j: (i, 0)),
        ],
        out_shape=[
            jax.ShapeDtypeStruct((m, n), BF16),
            jax.ShapeDtypeStruct((m, LANES), F32),
        ],
        scratch_shapes=[pltpu.VMEM((tm, D_MODEL), BF16)],
        compiler_params=_params("parallel", "arbitrary"),
        name="in_proj",
    )(x2, norm_g, mod3, mod3, w_main, w_dt)


def _split3(v):
    hi = v.astype(BF16)
    rem = v - hi.astype(F32)
    mid = rem.astype(BF16)
    lo = (rem - mid.astype(F32)).astype(BF16)
    return hi, mid, lo


GROUP_LANES = 2 * HEADS_PER_GROUP


def _stack_split3(v):
    hi, mid, lo = _split3(v)
    stacked = (hi.astype(F32) + pltpu.roll(mid.astype(F32), GROUP_LANES, 1)
               + pltpu.roll(lo.astype(F32), 2 * GROUP_LANES, 1))
    return stacked.astype(BF16)


def _ssd_dt_kernel(dt_ref, bias_ref, a_ref, dts_ref, cum_ref, src_t_ref):
    hpg = HEADS_PER_GROUP
    ii = lax.broadcasted_iota(jnp.int32, (CHUNK, CHUNK), 0)
    jj = lax.broadcasted_iota(jnp.int32, (CHUNK, CHUNK), 1)
    tri = (ii >= jj).astype(BF16)
    tri3 = jnp.concatenate([tri, tri, tri], axis=1)
    bias = bias_ref[...]
    a_row = a_ref[...]

    def body(k, carry):
        r = pl.ds(pl.multiple_of(k * CHUNK, CHUNK), CHUNK)
        raw = dt_ref[r, :] + bias
        dts = jnp.where(jj < 2 * SSD_HEADS, jnp.maximum(raw, 0.0) + jnp.log1p(jnp.exp(-jnp.abs(raw))), 0.0)
        adt = dts * a_row
        hi, mid, lo = _split3(adt)
        cf = _dot(tri3, jnp.concatenate([hi, mid, lo], axis=0))
        cr = cf[CHUNK - 1:CHUNK, :] - cf + adt
        valid = jj < 2 * SSD_HEADS
        cum = jnp.where(jj % GROUP_LANES < hpg, cf, cr)
        src_t = ((cum - jnp.log(jnp.where(valid, dts, 1.0))) * LOG2_E).T
        rt = pl.ds(pl.multiple_of(k * GROUP_LANES, GROUP_LANES), GROUP_LANES)
        for g in range(SSD_GROUPS):
            g0 = g * GROUP_LANES
            for src, dst in ((cum * LOG2_E, cum_ref), (dts, dts_ref)):
                lanes = src if g == 0 else pltpu.roll(src, LANES - g0, 1)
                dst[g, r, :] = jnp.where(jj < GROUP_LANES, lanes, 0.0)
            src_t_ref[g, rt, :] = src_t[g0:g0 + GROUP_LANES, :]
        return carry

    lax.fori_loop(0, dt_ref.shape[0] // CHUNK, body, 0)


def _ssd_dt(dt, bias_row, a_row, *, rows_per_step):
    m = dt.shape[0]
    row = pl.BlockSpec((1, LANES), lambda i: (0, 0))
    full = pl.BlockSpec((SSD_GROUPS, rows_per_step, LANES), lambda i: (0, i, 0))
    tr = pl.BlockSpec((SSD_GROUPS, rows_per_step // CHUNK * GROUP_LANES, LANES), lambda i: (0, i, 0))
    full_shape = jax.ShapeDtypeStruct((SSD_GROUPS, m, LANES), F32)
    tr_shape = jax.ShapeDtypeStruct((SSD_GROUPS, m // CHUNK * GROUP_LANES, LANES), F32)
    return pl.pallas_call(
        _ssd_dt_kernel,
        grid=(m // rows_per_step,),
        in_specs=[pl.BlockSpec((rows_per_step, LANES), lambda i: (i, 0)), row, row],
        out_specs=[full, full, tr],
        out_shape=[full_shape, full_shape, tr_shape],
        compiler_params=_params("parallel"),
        name="ssd_dt",
    )(dt, bias_row, a_row)


def _ssd_kernel(xs_ref, b_ref, c_ref, dts_ref, cum_ref, src_t_ref, cwx_ref, cbx_ref, cwb_ref, cbb_ref,
                cwc_ref, cbc_ref, dskip_ref, h0f_ref, h0b_ref, *rest,
                n_chunks, conv_tile, conv_period, with_output):
    if with_output:
        y_ref, xs_s, b_s, c_s, hf_s, hb_s, y_s, yb_s = rest
    else:
        hf_out, hb_out, xs_s, b_s, c_s, hf_s, hb_s = rest
    seq = n_chunks * CHUNK
    hpg = HEADS_PER_GROUP

    def conv_silu(src_ref, dst_ref, w_ref, bias_ref):
        cols = src_ref.shape[1]
        w1 = w_ref[1:2, :]
        bias = bias_ref[...]
        row = lax.broadcasted_iota(jnp.int32, (conv_tile, cols), 0) % conv_period
        w0 = jnp.where(row == 0, 0.0, w_ref[0:1, :])
        w2 = jnp.where(row == conv_period - 1, 0.0, w_ref[2:3, :])

        def body(k, carry):
            r = pl.ds(pl.multiple_of(k * conv_tile, conv_tile), conv_tile)
            v = src_ref[r, :].astype(F32)
            y = bias + pltpu.roll(v, 1, 0) * w0 + v * w1 + pltpu.roll(v, conv_tile - 1, 0) * w2
            dst_ref[r, :] = _silu(y).astype(BF16)
            return carry

        lax.fori_loop(0, seq // conv_tile, body, 0)

    conv_silu(xs_ref, xs_s, cwx_ref, cbx_ref)
    conv_silu(b_ref, b_s, cwb_ref, cbb_ref)
    conv_silu(c_ref, c_s, cwc_ref, cbc_ref)

    ii = lax.broadcasted_iota(jnp.int32, (CHUNK, CHUNK), 0)
    jj = lax.broadcasted_iota(jnp.int32, (CHUNK, CHUNK), 1)
    lower = ii >= jj
    upper = jj >= ii

    hf_s[...] = h0f_ref[...]
    hb_s[...] = h0b_ref[...]

    head_of_col = lax.broadcasted_iota(jnp.int32, (LANES, GROUP_WIDTH), 1) // SSD_HEAD_DIM
    sel_row = lax.broadcasted_iota(jnp.int32, (LANES, GROUP_WIDTH), 0)
    sel_valid = sel_row < 3 * GROUP_LANES
    sel_f = (sel_valid & (sel_row % GROUP_LANES == head_of_col)).astype(BF16)
    sel_b = (sel_valid & (sel_row % GROUP_LANES == head_of_col + hpg)).astype(BF16)
    low_half = jj < SSD_HEAD_DIM
    group_lane = jj < GROUP_LANES

    def direction_terms(r, sel):
        cum = cum_ref[r, :]
        dts = dts_ref[r, :]
        tot = jnp.where(jj[0:1, :] < hpg, cum[CHUNK - 1:CHUNK, :], cum[0:1, :])
        wst = dts * jnp.exp2(tot - cum)
        eoff = jnp.where(group_lane, jnp.exp2(cum), 0.0)
        edec = jnp.broadcast_to(jnp.where(group_lane[0:1, :], jnp.exp2(tot), 0.0), (8, LANES))
        ex = _dot(_stack_split3(jnp.concatenate([wst, eoff, edec], axis=0)), sel)
        return cum, ex[:CHUNK], ex[CHUNK:2 * CHUNK], ex[2 * CHUNK:2 * CHUNK + 1]

    def state_step(h_s, r, sel):
        cum, w_state, e_off, e_dec = direction_terms(r, sel)
        xc = xs_s[r, :]
        xw = (xc.astype(F32) * w_state).astype(BF16)
        s_new = lax.dot_general(b_s[r, :], xw, (((0,), (0,)), ((), ())), preferred_element_type=F32)
        h = h_s[...]
        y_off = _dot(c_s[r, :], h.astype(BF16)) * e_off
        h_s[...] = h * e_dec + s_new
        return cum, xc, y_off

    def fwd_body(k, carry):
        r = pl.ds(pl.multiple_of(k * CHUNK, CHUNK), CHUNK)
        cum, xc, y_off = state_step(hf_s, r, sel_f)
        if with_output:
            g = _dot_nt(c_s[r, :], b_s[r, :])
            rt = pl.ds(pl.multiple_of(k * GROUP_LANES, GROUP_LANES), GROUP_LANES)
            src_t = src_t_ref[rt, :]
            pieces = []
            for pair in range(hpg // 2):
                ms = []
                for h in (2 * pair, 2 * pair + 1):
                    hb = h + hpg
                    mf = jnp.exp2(jnp.where(lower, cum[:, h:h + 1] - src_t[h:h + 1, :], NEG_BIG))
                    mb = jnp.exp2(jnp.where(upper, cum[:, hb:hb + 1] - src_t[hb:hb + 1, :], NEG_BIG))
                    ms.append((g * (mf + mb)).astype(BF16))
                lhs = jnp.concatenate(ms, axis=1)
                xp = xc[:, pair * LANES:(pair + 1) * LANES]
                zero = jnp.zeros_like(xp)
                rhs = jnp.concatenate([jnp.where(low_half, xp, zero), jnp.where(low_half, zero, xp)], axis=0)
                pieces.append(_dot(lhs, rhs))
            y_diag = jnp.concatenate(pieces, axis=1)
            y_s[r, :] = y_diag + y_off + xc.astype(F32) * dskip_ref[...]
        rb = pl.ds(pl.multiple_of((n_chunks - 1 - k) * CHUNK, CHUNK), CHUNK)
        _, _, y_off_b = state_step(hb_s, rb, sel_b)
        if with_output:
            yb_s[rb, :] = y_off_b
        return carry

    lax.fori_loop(0, n_chunks, fwd_body, 0, unroll=8 if n_chunks % 8 == 0 else 2)

    if with_output:
        def sum_body(k, carry):
            r = pl.ds(pl.multiple_of(k * CHUNK, CHUNK), CHUNK)
            y_ref[r, :] = (y_s[r, :] + yb_s[r, :]).astype(y_ref.dtype)
            return carry

        lax.fori_loop(0, n_chunks, sum_body, 0)

    if not with_output:
        hf_out[...] = hf_s[...]
        hb_out[...] = hb_s[...]


def _ssd(pm, xs_col0, bc_col0, dt_terms, conv_w, conv_b, dskip_g, h0f, h0b, *, batch, seq, conv_tile, conv_period,
         with_output):
    n_chunks = seq // CHUNK
    dts_g, cum_g, src_t_g = dt_terms
    dt_spec = pl.BlockSpec((None, seq, LANES), lambda b, g: (g, b, 0))
    dt_t_spec = pl.BlockSpec((None, n_chunks * GROUP_LANES, LANES), lambda b, g: (g, b, 0))
    xs_blk0 = xs_col0 // GROUP_WIDTH
    b_blk0 = bc_col0 // SSD_STATE
    c_blk0 = b_blk0 + SSD_GROUPS
    cw_b0 = SSD_WIDTH // SSD_STATE
    cw_c0 = cw_b0 + SSD_GROUPS
    state_spec = pl.BlockSpec((None, None, SSD_STATE, GROUP_WIDTH), lambda b, g: (b, g, 0, 0))
    in_specs = [
        pl.BlockSpec((seq, GROUP_WIDTH), lambda b, g: (b, xs_blk0 + g)),
        pl.BlockSpec((seq, SSD_STATE), lambda b, g: (b, b_blk0 + g)),
        pl.BlockSpec((seq, SSD_STATE), lambda b, g: (b, c_blk0 + g)),
        dt_spec,
        dt_spec,
        dt_t_spec,
        pl.BlockSpec((3, GROUP_WIDTH), lambda b, g: (0, g)),
        pl.BlockSpec((1, GROUP_WIDTH), lambda b, g: (0, g)),
        pl.BlockSpec((3, SSD_STATE), lambda b, g: (0, cw_b0 + g)),
        pl.BlockSpec((1, SSD_STATE), lambda b, g: (0, cw_b0 + g)),
        pl.BlockSpec((3, SSD_STATE), lambda b, g: (0, cw_c0 + g)),
        pl.BlockSpec((1, SSD_STATE), lambda b, g: (0, cw_c0 + g)),
        pl.BlockSpec((None, 1, GROUP_WIDTH), lambda b, g: (g, 0, 0)),
        state_spec,
        state_spec,
    ]
    scratch = [
        pltpu.VMEM((seq, GROUP_WIDTH), BF16),
        pltpu.VMEM((seq, SSD_STATE), BF16),
        pltpu.VMEM((seq, SSD_STATE), BF16),
        pltpu.VMEM((SSD_STATE, GROUP_WIDTH), F32),
        pltpu.VMEM((SSD_STATE, GROUP_WIDTH), F32),
    ]
    if with_output:
        out_specs = pl.BlockSpec((seq, GROUP_WIDTH), lambda b, g: (b, g))
        out_shape = jax.ShapeDtypeStruct((batch * seq, SSD_WIDTH), BF16)
        scratch += [pltpu.VMEM((seq, GROUP_WIDTH), F32)] * 2
    else:
        out_specs = [state_spec, state_spec]
        out_shape = [jax.ShapeDtypeStruct((batch, SSD_GROUPS, SSD_STATE, GROUP_WIDTH), F32)] * 2
    return pl.pallas_call(
        functools.partial(_ssd_kernel, n_chunks=n_chunks, conv_tile=conv_tile, conv_period=conv_period,
                          with_output=with_output),
        grid=(batch, SSD_GROUPS),
        in_specs=in_specs,
        out_specs=out_specs,
        out_shape=out_shape,
        scratch_shapes=scratch,
        compiler_params=_params("parallel", "parallel"),
        name="ssd_out" if with_output else "ssd_ctx",
    )(pm, pm, pm, dts_g, cum_g, src_t_g, conv_w, conv_b, conv_w, conv_b, conv_w, conv_b, dskip_g, h0f, h0b)


def _out_proj_kernel(yd_ref, z_ref, u_ref, v_ref, x_ref, g1_ref, ng_ref, lng_ref, lnb_ref, ws_ref, bst_ref, w_ref,
                     o_ref, mix_cur, mix_nxt):
    tm = yd_ref.shape[0]
    s = pl.program_id(0)
    n = pl.num_programs(0) - 1

    def build_steps():
        steps = []
        for k in range(tm // CHUNK):
            r = pl.ds(k * CHUNK, CHUNK)
            shared = {}

            def ssd_part(r=r):
                a = yd_ref[r, :].astype(F32) * z_ref[r, :].astype(F32)
                ms = jnp.mean(a * a, axis=-1, keepdims=True)
                mix_nxt[r, 0:SSD_WIDTH] = (a * lax.rsqrt(ms + NORM_EPS) * ng_ref[...]).astype(BF16)

            def ln_part(r=r, shared=shared):
                gv = v_ref[r, :].astype(F32)
                mu = jnp.mean(gv, axis=-1, keepdims=True)
                xc = gv - mu
                var = jnp.mean(xc * xc, axis=-1, keepdims=True)
                shared["ln"] = ((xc * lax.rsqrt(var + NORM_EPS)) * lng_ref[...] + lnb_ref[...]).astype(BF16)

            def gate_part(h0, h1, r=r, shared=shared):
                for h in range(h0, h1):
                    c0, c1 = h * CM_HEAD_DIM, (h + 1) * CM_HEAD_DIM
                    sp = _dot(ws_ref[h], shared["ln"][:, c0:c1]) + bst_ref[:, h:h + 1]
                    gu = u_ref[r, c0:c1].astype(F32)
                    mix_nxt[r, SSD_WIDTH + c0:SSD_WIDTH + c1] = (gu * sp).astype(BF16)

            steps += [ssd_part, ln_part, functools.partial(gate_part, 0, CM_HEADS // 2),
                      functools.partial(gate_part, CM_HEADS // 2, CM_HEADS)]
        return steps

    n_proj = 8
    pw = D_MODEL // n_proj

    def project_steps():
        def piece(q):
            c = slice(q * pw, (q + 1) * pw)
            o_ref[:, c] = x_ref[:, c] + g1_ref[:, c] * _dot(mix_cur[...], w_ref[:, c])

        return [functools.partial(piece, q) for q in range(n_proj)]

    @pl.when(s == 0)
    def _():
        for step in build_steps():
            step()

    @pl.when((s > 0) & (s < n))
    def _():
        for proj, build in zip(project_steps(), build_steps()):
            proj()
            build()

    @pl.when(s == n)
    def _():
        for step in project_steps():
            step()

    @pl.when(s < n)
    def _():
        mix_cur[...] = mix_nxt[...]


def _out_proj(yd, pm, x2, mod3, ssd_norm_g, ln_g, ln_b, ws, bst, w_out, *, seq, tm):
    m = x2.shape[0]
    n = m // tm
    tiles_per_batch = seq // tm
    z_blk, u_blk, v_blk = 0, 2 * SSD_WIDTH // CM_WIDTH, 2 * SSD_WIDTH // CM_WIDTH + 1
    g1_blk = 2
    row = lambda shape: pl.BlockSpec(shape, lambda s: (0, 0))
    build_tile = lambda s: jnp.minimum(s, n - 1)
    proj_tile = lambda s: jnp.maximum(s - 1, 0)
    return pl.pallas_call(
        _out_proj_kernel,
        grid=(n + 1,),
        in_specs=[
            pl.BlockSpec((tm, SSD_WIDTH), lambda s: (build_tile(s), 0)),
            pl.BlockSpec((tm, SSD_WIDTH), lambda s: (build_tile(s), z_blk)),
            pl.BlockSpec((tm, CM_WIDTH), lambda s: (build_tile(s), u_blk)),
            pl.BlockSpec((tm, CM_WIDTH), lambda s: (build_tile(s), v_blk)),
            pl.BlockSpec((tm, D_MODEL), lambda s: (proj_tile(s), 0)),
            pl.BlockSpec((None, 1, D_MODEL), lambda s: (proj_tile(s) // tiles_per_batch, 0, g1_blk)),
            row((1, SSD_WIDTH)),
            row((1, CM_WIDTH)),
            row((1, CM_WIDTH)),
            pl.BlockSpec((CM_HEADS, CHUNK, CHUNK), lambda s: (0, 0, 0)),
            row((CHUNK, CM_HEADS)),
            pl.BlockSpec((SSD_WIDTH + CM_WIDTH, D_MODEL), lambda s: (0, 0), pipeline_mode=pl.Buffered(1)),
        ],
        out_specs=pl.BlockSpec((tm, D_MODEL), lambda s: (proj_tile(s), 0)),
        out_shape=jax.ShapeDtypeStruct((m, D_MODEL), F32),
        scratch_shapes=[pltpu.VMEM((tm, SSD_WIDTH + CM_WIDTH), BF16)] * 2,
        compiler_params=_params("arbitrary"),
        name="out_proj",
    )(yd, pm, pm, pm, x2, mod3, ssd_norm_g, ln_g, ln_b, ws, bst, w_out)


def _route(logits):
    lane = lax.broadcasted_iota(jnp.int32, logits.shape, 1)
    lane_f = lane.astype(F32)
    is_group = (lane >= N_EXPERTS) & (lane < N_EXPERTS + N_GROUPS)
    lg = jnp.where(is_group, logits, NEG_BIG)
    mg = jnp.max(lg, axis=1, keepdims=True)
    top_pg = 1.0 / jnp.sum(jnp.exp(lg - mg), axis=1, keepdims=True)
    gi = jnp.min(jnp.where(lg == mg, lane_f, 1e9), axis=1, keepdims=True) - N_EXPERTS
    in_group = (lane < N_EXPERTS) & ((lane // EXPERTS_PER_GROUP).astype(F32) == gi)
    le = jnp.where(in_group, logits, NEG_BIG)
    m1 = jnp.max(le, axis=1, keepdims=True)
    i1 = jnp.min(jnp.where(le == m1, lane_f, 1e9), axis=1, keepdims=True)
    le2 = jnp.where(lane_f == i1, NEG_BIG, le)
    m2 = jnp.max(le2, axis=1, keepdims=True)
    i2 = jnp.min(jnp.where(le2 == m2, lane_f, 1e9), axis=1, keepdims=True)
    e2 = jnp.exp(m2 - m1)
    p1 = 1.0 / (1.0 + e2)
    p2 = e2 * p1
    return i1, i2, p1 * top_pg, p2 * top_pg


RINFO_E1, RINFO_E2, RINFO_W1, RINFO_W2, RINFO_R1, RINFO_R2 = range(6)


def _route_kernel(x1_ref, n2g_ref, sh2_ref, sc2_ref, wr_ref, br_ref, h2_ref, rinfo_ref, counts_ref, cnt_scr):
    tm = x1_ref.shape[0]

    @pl.when(pl.program_id(0) == 0)
    def _():
        cnt_scr[...] = jnp.zeros_like(cnt_scr)

    x1 = x1_ref[...]
    ms = jnp.mean(x1 * x1, axis=-1, keepdims=True)
    h2 = (x1 * lax.rsqrt(ms + NORM_EPS) * n2g_ref[...]) * (1.0 + sc2_ref[...]) + sh2_ref[...]
    h2_ref[...] = h2
    hi, mid, _ = _split3(h2)
    p = _dot(hi, wr_ref[...])
    logits = p[:, :LANES] + p[:, LANES:] + _dot(mid, wr_ref[:, :LANES]) + br_ref[...]
    i1, i2, w1, w2 = _route(logits)

    lane = lax.broadcasted_iota(jnp.int32, (tm, LANES), 1)
    lane_f = lane.astype(F32)
    oh1 = jnp.where(lane_f == i1, 1.0, 0.0)
    oh2 = jnp.where(lane_f == i2, 1.0, 0.0)
    before = (lax.broadcasted_iota(jnp.int32, (tm, tm), 0) > lax.broadcasted_iota(jnp.int32, (tm, tm), 1)).astype(BF16)
    carried = cnt_scr[...]
    tot1 = jnp.sum(oh1, axis=0, keepdims=True)
    r1 = jnp.sum(oh1 * (_dot(before, oh1.astype(BF16)) + carried), axis=1, keepdims=True)
    r2 = jnp.sum(oh2 * (_dot(before, oh2.astype(BF16)) + (carried + tot1)), axis=1, keepdims=True)
    counts = carried + tot1 + jnp.sum(oh2, axis=0, keepdims=True)
    cnt_scr[...] = counts
    counts_ref[...] = jnp.broadcast_to(counts, counts_ref.shape)

    info = jnp.zeros((tm, LANES), F32)
    for k, v in ((RINFO_E1, i1), (RINFO_E2, i2), (RINFO_W1, w1), (RINFO_W2, w2), (RINFO_R1, r1), (RINFO_R2, r2)):
        info = jnp.where(lane == k, v, info)
    rinfo_ref[...] = info


def _route_call(x1, mod3, norm2_g, w_router, b_router, *, seq, tm):
    m = x1.shape[0]
    tiles_per_batch = seq // tm
    row = lambda shape: pl.BlockSpec(shape, lambda i: (0, 0))
    modrow = lambda k: pl.BlockSpec((None, 1, D_MODEL), lambda i: (i // tiles_per_batch, 0, k))
    return pl.pallas_call(
        _route_kernel,
        grid=(m // tm,),
        in_specs=[
            pl.BlockSpec((tm, D_MODEL), lambda i: (i, 0)),
            row((1, D_MODEL)),
            modrow(3),
            modrow(4),
            row((D_MODEL, 2 * LANES)),
            row((1, LANES)),
        ],
        out_specs=[
            pl.BlockSpec((tm, D_MODEL), lambda i: (i, 0)),
            pl.BlockSpec((tm, LANES), lambda i: (i, 0)),
            row((8, LANES)),
        ],
        out_shape=[
            jax.ShapeDtypeStruct((m, D_MODEL), F32),
            jax.ShapeDtypeStruct((m, LANES), F32),
            jax.ShapeDtypeStruct((8, LANES), F32),
        ],
        scratch_shapes=[pltpu.VMEM((1, LANES), F32)],
        compiler_params=_params("arbitrary"),
        name="route",
    )(x1, norm2_g, mod3, mod3, w_router, b_router)


SUBLANES = 8


def _dispatch_kernel(ps_ref, pl_ref, nu_ref, pos_ref, h2_ref, x_hbm, zbuf, zsem, sem, *, tile_rows, n_tiles):
    tm = h2_ref.shape[0]
    n_used = nu_ref[0]

    def zero_fill(phase):
        def go(rows, start):
            getattr(pltpu.make_async_copy(zbuf.at[pl.ds(0, rows), :], x_hbm.at[pl.ds(start, rows), :], zsem), phase)()

        for e in range(N_EXPERTS):
            start, length = ps_ref[e], pl_ref[e]
            head = (SUBLANES - (start & (SUBLANES - 1))) & (SUBLANES - 1)
            for q in range(SUBLANES - 1):
                pl.when(q < head)(functools.partial(go, 1, start + q))
            rem = length - head
            blk = tile_rows // 2
            while blk >= SUBLANES:
                off = pl.multiple_of(start + head + (rem & ~(2 * blk - 1)), SUBLANES)
                pl.when((rem & blk) != 0)(functools.partial(go, blk, off))
                blk //= 2

        def unused(t, carry):
            go(tile_rows, pl.multiple_of(t * tile_rows, tile_rows))
            return carry

        lax.fori_loop(n_used, n_tiles, unused, 0)

    @pl.when(pl.program_id(0) == 0)
    def _():
        zbuf[...] = jnp.zeros_like(zbuf)
        zero_fill("start")

    def body(r, carry):
        for k in range(2):
            pltpu.make_async_copy(h2_ref.at[pl.ds(r, 1), :], x_hbm.at[pl.ds(pos_ref[0, 2 * r + k], 1), :],
                                  sem).start(priority=k)
        return carry

    lax.fori_loop(0, tm, body, 0, unroll=8)
    for k in range(2):
        pltpu.make_async_copy(h2_ref, x_hbm.at[pl.ds(0, tm), :], sem).wait()

    @pl.when(pl.program_id(0) == pl.num_programs(0) - 1)
    def _():
        zero_fill("wait")


def _dispatch_call(pad_start, pad_len, n_used, pos, h2, *, tm, tile_rows, n_tiles):
    m = h2.shape[0]
    grid_spec = pltpu.PrefetchScalarGridSpec(
        num_scalar_prefetch=3,
        grid=(m // tm,),
        in_specs=[
            pl.BlockSpec((None, 1, 2 * tm), lambda i, ps, pn, nu: (i, 0, 0), memory_space=pltpu.SMEM),
            pl.BlockSpec((tm, D_MODEL), lambda i, ps, pn, nu: (i, 0)),
        ],
        out_specs=pl.BlockSpec(memory_space=pl.ANY),
        scratch_shapes=[
            pltpu.VMEM((tile_rows, D_MODEL), F32),
            pltpu.SemaphoreType.DMA(()),
            pltpu.SemaphoreType.DMA(()),
        ],
    )
    return pl.pallas_call(
        functools.partial(_dispatch_kernel, tile_rows=tile_rows, n_tiles=n_tiles),
        grid_spec=grid_spec,
        out_shape=jax.ShapeDtypeStruct((n_tiles * tile_rows, D_MODEL), F32),
        compiler_params=_params("arbitrary"),
        name="dispatch",
    )(pad_start, pad_len, n_used, pos, h2)


def _experts_kernel(te_ref, nu_ref, ord_ref, nxt_ref, x_ref, wg_hbm, wu_hbm, wd_hbm, y_ref,
                    wg_f, wu_f, wd_f, wg_b, wu_b, wd_b, sem):
    j = pl.program_id(0)
    n_used = nu_ref[0]

    def weight_copies(e, slot):
        return [pltpu.make_async_copy(src.at[e], dst.at[slot], sem.at[slot])
                for src, dst in ((wg_hbm, wg_f), (wu_hbm, wu_f), (wd_hbm, wd_f))]

    @pl.when(j == 0)
    def _():
        for cp in weight_copies(te_ref[0], 0):
            cp.start()

    @pl.when((j < n_used) & ((j == 0) | (te_ref[j] != te_ref[jnp.maximum(j - 1, 0)])))
    def _():
        slot = ord_ref[j] % 2
        for cp in weight_copies(te_ref[j], slot):
            cp.wait()

        @pl.when(nxt_ref[j] >= 0)
        def _():
            for cp in weight_copies(nxt_ref[j], 1 - slot):
                cp.start()

        wg_b[...] = wg_f[slot].astype(BF16)
        wu_b[...] = wu_f[slot].astype(BF16)
        wd_b[...] = wd_f[slot].astype(BF16)

    @pl.when(j < n_used)
    def _():
        xt = x_ref[...].astype(BF16)
        hid = (_silu(_dot(xt, wg_b[...])) * _dot(xt, wu_b[...])).astype(BF16)
        y_ref[...] = _dot(hid, wd_b[...])

    @pl.when(j >= n_used)
    def _():
        y_ref[...] = jnp.zeros_like(y_ref)


def _experts_call(tile_expert, n_used, tile_ord, next_expert, x_sorted, wg, wu, wd, *, tm):
    n_tiles = tile_expert.shape[0]
    hbm = pl.BlockSpec(memory_space=pl.ANY)
    grid_spec = pltpu.PrefetchScalarGridSpec(
        num_scalar_prefetch=4,
        grid=(n_tiles,),
        in_specs=[
            pl.BlockSpec((tm, D_MODEL), lambda j, te, nu, od, nx: (jnp.minimum(j, nu[0] - 1), 0)),
            hbm,
            hbm,
            hbm,
        ],
        out_specs=pl.BlockSpec((tm, D_MODEL), lambda j, te, nu, od, nx: (j, 0)),
        scratch_shapes=[
            pltpu.VMEM((2, D_MODEL, EXPERT_FF), F32),
            pltpu.VMEM((2, D_MODEL, EXPERT_FF), F32),
            pltpu.VMEM((2, EXPERT_FF, D_MODEL), F32),
            pltpu.VMEM((D_MODEL, EXPERT_FF), BF16),
            pltpu.VMEM((D_MODEL, EXPERT_FF), BF16),
            pltpu.VMEM((EXPERT_FF, D_MODEL), BF16),
            pltpu.SemaphoreType.DMA((2,)),
        ],
    )
    return pl.pallas_call(
        _experts_kernel,
        grid_spec=grid_spec,
        out_shape=jax.ShapeDtypeStruct((n_tiles * tm, D_MODEL), F32),
        compiler_params=_params("arbitrary"),
        name="experts",
    )(tile_expert, n_used, tile_ord, next_expert, x_sorted, wg, wu, wd)


def _combine_kernel(pos_cur, pos_nxt, x1_ref, g2_ref, nfg_ref, rinfo_ref, y_hbm, o_ref, ybuf, sem):
    i = pl.program_id(0)
    tm = x1_ref.shape[0]
    slot = i % 2

    def row_copy(pos_ref, r, k, s):
        return pltpu.make_async_copy(y_hbm.at[pl.ds(pos_ref[0, 2 * r + k], 1), :], ybuf.at[s, k, pl.ds(r, 1), :],
                                     sem.at[s])

    def wait_tile(s):
        for k in range(2):
            pltpu.make_async_copy(y_hbm.at[pl.ds(0, tm), :], ybuf.at[s, k], sem.at[s]).wait()

    @pl.when(i == 0)
    def _():
        def body(r, carry):
            for k in range(2):
                row_copy(pos_cur, r, k, 0).start()
            return carry

        lax.fori_loop(0, tm, body, 0)

    def step(cur):
        wait_tile(cur)
        for r in range(tm):
            for k in range(2):
                row_copy(pos_nxt, r, k, 1 - cur).start(priority=k)
        info = rinfo_ref[...]
        w1 = info[:, RINFO_W1:RINFO_W1 + 1]
        w2 = info[:, RINFO_W2:RINFO_W2 + 1]
        moe = w1 * ybuf[cur, 0] + w2 * ybuf[cur, 1]
        y = x1_ref[...] + g2_ref[...] * moe
        ms = jnp.mean(y * y, axis=-1, keepdims=True)
        o_ref[...] = y * lax.rsqrt(ms + NORM_EPS) * nfg_ref[...]

        @pl.when(i == pl.num_programs(0) - 1)
        def _():
            wait_tile(1 - cur)

    for cur in range(2):
        pl.when(slot == cur)(functools.partial(step, cur))


def _combine_call(pos, x1, mod3, normf_g, rinfo, y_sorted, *, seq, tm):
    m = x1.shape[0]
    n_tiles = m // tm
    tiles_per_batch = seq // tm
    return pl.pallas_call(
        _combine_kernel,
        grid=(n_tiles,),
        in_specs=[
            pl.BlockSpec((None, 1, 2 * tm), lambda i: (i, 0, 0), memory_space=pltpu.SMEM),
            pl.BlockSpec((None, 1, 2 * tm), lambda i: (jnp.minimum(i + 1, n_tiles - 1), 0, 0),
                         memory_space=pltpu.SMEM),
            pl.BlockSpec((tm, D_MODEL), lambda i: (i, 0)),
            pl.BlockSpec((None, 1, D_MODEL), lambda i: (i // tiles_per_batch, 0, 5)),
            pl.BlockSpec((1, D_MODEL), lambda i: (0, 0)),
            pl.BlockSpec((tm, LANES), lambda i: (i, 0)),
            pl.BlockSpec(memory_space=pl.ANY),
        ],
        out_specs=pl.BlockSpec((tm, D_MODEL), lambda i: (i, 0)),
        out_shape=jax.ShapeDtypeStruct((m, D_MODEL), F32),
        scratch_shapes=[
            pltpu.VMEM((2, 2, tm, D_MODEL), F32),
            pltpu.SemaphoreType.DMA((2,)),
        ],
        compiler_params=_params("arbitrary"),
        name="combine",
    )(pos, pos, x1, mod3, normf_g, rinfo, y_sorted)


def _dispatch_plan(rinfo, counts, *, tm):
    n_tok = rinfo.shape[0]
    n_tiles = (2 * n_tok + N_EXPERTS * (tm - 1)) // tm + 1
    expert = rinfo[:, RINFO_E1:RINFO_E2 + 1].astype(jnp.int32)
    rank = rinfo[:, RINFO_R1:RINFO_R2 + 1].astype(jnp.int32)
    cnt = counts[0, :N_EXPERTS].astype(jnp.int32)
    tiles_e = (cnt + tm - 1) // tm
    end_tile = jnp.cumsum(tiles_e)
    start_row = (end_tile - tiles_e) * tm
    is_e = expert[:, :, None] == jnp.arange(N_EXPERTS, dtype=jnp.int32)
    pos = jnp.sum(jnp.where(is_e, start_row, 0), axis=-1) + rank
    n_used = end_tile[-1:]
    tile_ids = jnp.minimum(jnp.arange(n_tiles, dtype=jnp.int32), n_used - 1)
    tile_expert = jnp.sum(tile_ids[:, None] >= end_tile[None, :], axis=1).astype(jnp.int32)
    pad_rows = ((start_row + cnt).astype(jnp.int32), (tiles_e * tm - cnt).astype(jnp.int32))
    used = (tiles_e > 0).astype(jnp.int32)
    ord_e = jnp.cumsum(used) - used
    ids = jnp.arange(N_EXPERTS, dtype=jnp.int32)
    later = jnp.where((ids[None, :] > ids[:, None]) & (used[None, :] > 0), ids[None, :], N_EXPERTS)
    nxt_e = jnp.min(later, axis=1)
    nxt_e = jnp.where(nxt_e < N_EXPERTS, nxt_e, -1).astype(jnp.int32)
    is_te = tile_expert[:, None] == ids[None, :]
    tile_ord = jnp.sum(jnp.where(is_te, ord_e, 0), axis=1).astype(jnp.int32)
    next_expert = jnp.sum(jnp.where(is_te, nxt_e, 0), axis=1).astype(jnp.int32)
    return pos, (tile_expert, n_used.astype(jnp.int32), tile_ord, next_expert), pad_rows, n_tiles


def kernel(x, c, ctx, c_ctx, w_mod, b_mod, norm1_g, w_in, conv_w, conv_b, dt_bias_f, dt_bias_b, a_log_f, a_log_b,
           d_skip, ssd_norm_g, cm_ln_g, cm_ln_b, w_spatial, b_spatial, w_out, norm2_g, w_router_group,
           b_router_group, w_router_expert, b_router_expert, w_exp_gate, w_exp_up, w_exp_down, normf_g):
    bsz, seq, _ = x.shape
    ctx_len = ctx.shape[1]
    i = 0

    cc = jnp.concatenate([c, c_ctx[None, :], jnp.zeros((MOD_ROWS - bsz - 1, D_MODEL), F32)], axis=0)
    mod = _modulation(cc, w_mod[i], b_mod[i][None, :])
    mod3 = mod.reshape(MOD_ROWS, 1, N_MOD * D_MODEL)

    w_main, w_dt = _w_in_prep(jnp.swapaxes(w_in[i], 0, 1))
    g1row = norm1_g[i][None, :]

    x2 = x.reshape(bsz * seq, D_MODEL)
    ctx2 = ctx.reshape(bsz * ctx_len, D_MODEL)
    tm_in = 1024
    tn_in = 1024
    xs_col0, bc_col0 = SSD_WIDTH, 2 * SSD_WIDTH + 2 * CM_WIDTH
    pm_x, dt_x = _in_proj(x2, mod3, lambda t: t // (seq // tm_in), g1row, w_main, w_dt, tm_in, tn_in)
    ctx_blocks = tuple(range(xs_col0 // tn_in, 2 * SSD_WIDTH // tn_in)) + (bc_col0 // tn_in,)
    pm_c, dt_c = _in_proj(ctx2, mod3, lambda t: bsz, g1row, w_main, w_dt, bsz * ctx_len, tn_in, ctx_blocks)

    def dt_lanes(fwd, bwd):
        both = jnp.stack([fwd.reshape(SSD_GROUPS, HEADS_PER_GROUP), bwd.reshape(SSD_GROUPS, HEADS_PER_GROUP)], axis=1)
        return jnp.pad(both.reshape(-1).astype(F32), (0, LANES - 2 * SSD_HEADS))[None, :]

    dt_bias_row = dt_lanes(dt_bias_f[i], dt_bias_b[i])
    a_row = dt_lanes(-jnp.exp(a_log_f[i].astype(F32)), -jnp.exp(a_log_b[i].astype(F32)))
    dskip_g = jnp.repeat(d_skip[i], SSD_HEAD_DIM).reshape(SSD_GROUPS, 1, GROUP_WIDTH)
    cw = conv_w[i]
    cb = conv_b[i][None, :]
    h_zero = jnp.zeros((bsz, SSD_GROUPS, SSD_STATE, GROUP_WIDTH), F32)

    dt_terms_c = _ssd_dt(dt_c, dt_bias_row, a_row, rows_per_step=bsz * ctx_len)
    dt_terms_x = _ssd_dt(dt_x, dt_bias_row, a_row, rows_per_step=1024)
    hc_f, hc_b = _ssd(pm_c, 0, SSD_WIDTH, dt_terms_c, cw, cb, dskip_g, h_zero, h_zero,
                      batch=bsz, seq=ctx_len, conv_tile=ctx_len, conv_period=ctx_len, with_output=False)
    yd = _ssd(pm_x, xs_col0, bc_col0, dt_terms_x, cw, cb, dskip_g, hc_f, hc_b,
              batch=bsz, seq=seq, conv_tile=CHUNK, conv_period=GRID_W, with_output=True)

    x1 = _out_proj(yd, pm_x, x2, mod3, ssd_norm_g[i][None, :], cm_ln_g[i][None, :], cm_ln_b[i][None, :],
                   w_spatial[i].astype(BF16), b_spatial[i].T, w_out[i].astype(BF16), seq=seq, tm=256)

    w_re = jnp.transpose(w_router_expert[i], (1, 0, 2)).reshape(D_MODEL, N_EXPERTS)
    pad = LANES - N_EXPERTS - N_GROUPS
    w_router = jnp.pad(jnp.concatenate([w_re, w_router_group[i]], axis=1), ((0, 0), (0, pad)))
    w_router_hi = w_router.astype(BF16)
    w_router = jnp.concatenate([w_router_hi, (w_router - w_router_hi.astype(F32)).astype(BF16)], axis=1)
    b_router = jnp.pad(jnp.concatenate([b_router_expert[i].reshape(-1), b_router_group[i]]), (0, pad))[None, :]
    h2, rinfo, counts = _route_call(x1, mod3, norm2_g[i][None, :], w_router, b_router, seq=seq, tm=512)

    tm_e = 256
    pos, tile_tables, pad_rows, n_tiles = _dispatch_plan(rinfo, counts, tm=tm_e)
    tm_d = 512
    x_sorted = _dispatch_call(*pad_rows, tile_tables[1], pos.reshape(bsz * seq // tm_d, 1, 2 * tm_d), h2,
                              tm=tm_d, tile_rows=tm_e, n_tiles=n_tiles)
    wg = w_exp_gate[i].reshape(N_EXPERTS, D_MODEL, EXPERT_FF)
    wu = w_exp_up[i].reshape(N_EXPERTS, D_MODEL, EXPERT_FF)
    wd = w_exp_down[i].reshape(N_EXPERTS, EXPERT_FF, D_MODEL)
    y_sorted = _experts_call(*tile_tables, x_sorted, wg, wu, wd, tm=tm_e)
    tm_c = 256
    out = _combine_call(pos.reshape(bsz * seq // tm_c, 1, 2 * tm_c), x1, mod3, normf_g[None, :], rinfo, y_sorted,
                        seq=seq, tm=tm_c)
    return out.reshape(bsz, seq, D_MODEL)
```

```python
import functools

import jax
import jax.numpy as jnp
from jax import lax
from jax.experimental import pallas as pl
from jax.experimental.pallas import tpu as pltpu

F32 = jnp.float32
BF16 = jnp.bfloat16
HIGHEST = lax.Precision.HIGHEST

D_MODEL = 2048
GRID_W = 64
SSD_WIDTH = 2048
CM_WIDTH = 2048
SSD_HEADS = 32
SSD_HEAD_DIM = 64
SSD_GROUPS = 4
HEADS_PER_GROUP = SSD_HEADS // SSD_GROUPS
GROUP_WIDTH = HEADS_PER_GROUP * SSD_HEAD_DIM
SSD_STATE = 128
CHUNK = 128
BC_WIDTH = SSD_GROUPS * SSD_STATE
CM_HEADS = 8
CM_HEAD_DIM = CM_WIDTH // CM_HEADS
N_GROUPS = 4
N_EXPERTS = 32
EXPERTS_PER_GROUP = 8
EXPERT_FF = 512
N_MOD = 6
NORM_EPS = 1e-6
LANES = 128
NEG_BIG = -1e30
MOD_ROWS = 8
VMEM_LIMIT = 56 * 1024 * 1024


def _params(*sem):
    return pltpu.CompilerParams(dimension_semantics=sem, vmem_limit_bytes=VMEM_LIMIT)


LOG2_E = 1.4426950408889634
GELU_C = 0.7978845608028654
GELU_K1 = -2.0 * GELU_C * LOG2_E
GELU_K3 = GELU_K1 * 0.044715


def _silu(v):
    return v / (1.0 + jnp.exp2(v * (-LOG2_E)))


def _gelu_tanh(v):
    return v / (1.0 + jnp.exp2(v * (GELU_K1 + GELU_K3 * (v * v))))


def _dot(a, b):
    return jnp.dot(a, b, preferred_element_type=F32)


def _dot_nt(a, b):
    return lax.dot_general(a, b, (((1,), (1,)), ((), ())), preferred_element_type=F32)


def _mod_kernel(cc_ref, w_ref, b_ref, o_ref):
    a = _silu(cc_ref[...])
    hi = a.astype(BF16).astype(F32)
    lhs = jnp.concatenate([hi, a - hi], axis=0).astype(BF16)
    r = _dot(lhs, w_ref[...].astype(BF16))
    o_ref[...] = r[:MOD_ROWS] + r[MOD_ROWS:] + b_ref[...]


def _modulation(cc, w_mod, b_mod):
    n = w_mod.shape[1]
    tn = 1024
    return pl.pallas_call(
        _mod_kernel,
        grid=(n // tn,),
        in_specs=[
            pl.BlockSpec((MOD_ROWS, D_MODEL), lambda j: (0, 0)),
            pl.BlockSpec((D_MODEL, tn), lambda j: (0, j)),
            pl.BlockSpec((1, tn), lambda j: (0, j)),
        ],
        out_specs=pl.BlockSpec((MOD_ROWS, tn), lambda j: (0, j)),
        out_shape=jax.ShapeDtypeStruct((MOD_ROWS, n), F32),
        compiler_params=_params("arbitrary"),
        name="modulation",
    )(cc, w_mod, b_mod)


PREP_ROWS = 512
PREP_UV_BLK0 = 2 * SSD_WIDTH // PREP_ROWS
PREP_BC_BLK0 = (2 * SSD_WIDTH + 2 * CM_WIDTH) // PREP_ROWS
PREP_N_BLKS = PREP_BC_BLK0 + 2 * BC_WIDTH // PREP_ROWS
DT_COL0 = 2 * SSD_WIDTH + 2 * BC_WIDTH
UV_COL0 = DT_COL0 + 2 * SSD_HEADS


def _w_in_prep_kernel(a_ref, dt_ref, o_ref, wdt_ref):
    @pl.when(pl.program_id(0) == 0)
    def _():
        hpg = HEADS_PER_GROUP
        parts = []
        for g in range(SSD_GROUPS):
            parts += [dt_ref[hpg * g:hpg * (g + 1), :], dt_ref[SSD_HEADS + hpg * g:SSD_HEADS + hpg * (g + 1), :]]
        parts.append(dt_ref[2 * SSD_HEADS:, :])
        wdt_ref[...] = jnp.concatenate(parts, axis=0).astype(BF16)

    o_ref[...] = a_ref[...].astype(BF16)


def _w_in_prep(w_in_t):
    def src_row(j):
        uv = UV_COL0 + (j - PREP_UV_BLK0) * PREP_ROWS
        bc = 2 * SSD_WIDTH + (j - PREP_BC_BLK0) * PREP_ROWS
        return pl.multiple_of(jnp.where(j < PREP_UV_BLK0, j * PREP_ROWS, jnp.where(j < PREP_BC_BLK0, uv, bc)), 8)

    return pl.pallas_call(
        _w_in_prep_kernel,
        grid=(PREP_N_BLKS,),
        in_specs=[
            pl.BlockSpec((pl.Element(PREP_ROWS), pl.Element(D_MODEL)), lambda j: (src_row(j), 0)),
            pl.BlockSpec((LANES, D_MODEL), lambda j: (DT_COL0 // LANES, 0)),
        ],
        out_specs=[
            pl.BlockSpec((PREP_ROWS, D_MODEL), lambda j: (j, 0)),
            pl.BlockSpec((LANES, D_MODEL), lambda j: (0, 0)),
        ],
        out_shape=[
            jax.ShapeDtypeStruct((PREP_N_BLKS * PREP_ROWS, D_MODEL), BF16),
            jax.ShapeDtypeStruct((LANES, D_MODEL), BF16),
        ],
        compiler_params=_params("arbitrary"),
        name="w_in_prep",
    )(w_in_t, w_in_t)


ACT_NONE, ACT_SILU, ACT_GELU = 0, 1, 2


def _in_proj_kernel(x_ref, g_ref, sh_ref, sc_ref, w_ref, wdt_ref, o_ref, dt_ref, h_scr, *, block_acts):
    tm = x_ref.shape[0]
    rc = 256

    @pl.when(pl.program_id(1) == 0)
    def _():
        gain = g_ref[...]
        scale = 1.0 + sc_ref[...]
        shift = sh_ref[...]

        def body(k, carry):
            r = pl.ds(pl.multiple_of(k * rc, rc), rc)
            xv = x_ref[r, :]
            ms = jnp.mean(xv * xv, axis=-1, keepdims=True)
            h = (xv * lax.rsqrt(ms + NORM_EPS) * gain) * scale + shift
            h_scr[r, :] = h.astype(BF16)
            return carry

        lax.fori_loop(0, tm // rc, body, 0)
        dt_ref[...] = _dot_nt(h_scr[...], wdt_ref[...])

    j = pl.program_id(1)
    for act, fn in ((ACT_NONE, lambda v: v), (ACT_SILU, _silu), (ACT_GELU, _gelu_tanh)):
        blocks = [q for q, a in enumerate(block_acts) if a == act]
        if blocks:
            is_act = functools.reduce(jnp.logical_or, [j == q for q in blocks])

            @pl.when(is_act)
            def _():
                o_ref[...] = fn(_dot_nt(h_scr[...], w_ref[...])).astype(o_ref.dtype)


def _in_proj(x2, mod3, mod_row_of_tile, norm_g, w_main, w_dt, tm, tn, w_blocks=None):
    m = x2.shape[0]
    if w_blocks is None:
        w_blocks = tuple(range(w_main.shape[0] // tn))
    n = len(w_blocks) * tn
    z_end, uv0, uv_end = SSD_WIDTH // tn, 2 * SSD_WIDTH // tn, (2 * SSD_WIDTH + 2 * CM_WIDTH) // tn
    block_acts = tuple(ACT_SILU if b < z_end else ACT_GELU if uv0 <= b < uv_end else ACT_NONE for b in w_blocks)

    def w_blk(j):
        if w_blocks == tuple(range(len(w_blocks))):
            return j
        blk = jnp.int32(w_blocks[-1])
        for q in range(len(w_blocks) - 2, -1, -1):
            blk = jnp.where(j == q, w_blocks[q], blk)
        return blk

    return pl.pallas_call(
        functools.partial(_in_proj_kernel, block_acts=block_acts),
        grid=(m // tm, n // tn),
        in_specs=[
            pl.BlockSpec((tm, D_MODEL), lambda i, j: (i, 0)),
            pl.BlockSpec((1, D_MODEL), lambda i, j: (0, 0)),
            pl.BlockSpec((None, 1, D_MODEL), lambda i, j: (mod_row_of_tile(i), 0, 0)),
            pl.BlockSpec((None, 1, D_MODEL), lambda i, j: (mod_row_of_tile(i), 0, 1)),
            pl.BlockSpec((tn, D_MODEL), lambda i, j: (w_blk(j), 0)),
            pl.BlockSpec((LANES, D_MODEL), lambda i, j: (0, 0)),
        ],
        out_specs=[
            pl.BlockSpec((tm, tn), lambda i, j: (i, j)),
            pl.BlockSpec((tm, LANES), lambda i, j: (i, 0)),
        ],
        out_shape=[
            jax.ShapeDtypeStruct((m, n), BF16),
            jax.ShapeDtypeStruct((m, LANES), F32),
        ],
        scratch_shapes=[pltpu.VMEM((tm, D_MODEL), BF16)],
        compiler_params=_params("parallel", "arbitrary"),
        name="in_proj",
    )(x2, norm_g, mod3, mod3, w_main, w_dt)


def _split3(v):
    hi = v.astype(BF16)
    rem = v - hi.astype(F32)
    mid = rem.astype(BF16)
    lo = (rem - mid.astype(F32)).astype(BF16)
    return hi, mid, lo


GROUP_LANES = 2 * HEADS_PER_GROUP


def _stack_split3(v):
    hi, mid, lo = _split3(v)
    stacked = (hi.astype(F32) + pltpu.roll(mid.astype(F32), GROUP_LANES, 1)
               + pltpu.roll(lo.astype(F32), 2 * GROUP_LANES, 1))
    return stacked.astype(BF16)


def _ssd_dt_kernel(dt_ref, bias_ref, a_ref, dts_ref, cum_ref, src_t_ref):
    hpg = HEADS_PER_GROUP
    ii = lax.broadcasted_iota(jnp.int32, (CHUNK, CHUNK), 0)
    jj = lax.broadcasted_iota(jnp.int32, (CHUNK, CHUNK), 1)
    tri = (ii >= jj).astype(BF16)
    tri3 = jnp.concatenate([tri, tri, tri], axis=1)
    bias = bias_ref[...]
    a_row = a_ref[...]

    def body(k, carry):
        r = pl.ds(pl.multiple_of(k * CHUNK, CHUNK), CHUNK)
        raw = dt_ref[r, :] + bias
        dts = jnp.where(jj < 2 * SSD_HEADS, jnp.maximum(raw, 0.0) + jnp.log1p(jnp.exp(-jnp.abs(raw))), 0.0)
        adt = dts * a_row
        hi, mid, lo = _split3(adt)
        cf = _dot(tri3, jnp.concatenate([hi, mid, lo], axis=0))
        cr = cf[CHUNK - 1:CHUNK, :] - cf + adt
        valid = jj < 2 * SSD_HEADS
        cum = jnp.where(jj % GROUP_LANES < hpg, cf, cr)
        src_t = ((cum - jnp.log(jnp.where(valid, dts, 1.0))) * LOG2_E).T
        rt = pl.ds(pl.multiple_of(k * GROUP_LANES, GROUP_LANES), GROUP_LANES)
        for g in range(SSD_GROUPS):
            g0 = g * GROUP_LANES
            for src, dst in ((cum * LOG2_E, cum_ref), (dts, dts_ref)):
                lanes = src if g == 0 else pltpu.roll(src, LANES - g0, 1)
                dst[g, r, :] = jnp.where(jj < GROUP_LANES, lanes, 0.0)
            src_t_ref[g, rt, :] = src_t[g0:g0 + GROUP_LANES, :]
        return carry

    lax.fori_loop(0, dt_ref.shape[0] // CHUNK, body, 0)


def _ssd_dt(dt, bias_row, a_row, *, rows_per_step):
    m = dt.shape[0]
    row = pl.BlockSpec((1, LANES), lambda i: (0, 0))
    full = pl.BlockSpec((SSD_GROUPS, rows_per_step, LANES), lambda i: (0, i, 0))
    tr = pl.BlockSpec((SSD_GROUPS, rows_per_step // CHUNK * GROUP_LANES, LANES), lambda i: (0, i, 0))
    full_shape = jax.ShapeDtypeStruct((SSD_GROUPS, m, LANES), F32)
    tr_shape = jax.ShapeDtypeStruct((SSD_GROUPS, m // CHUNK * GROUP_LANES, LANES), F32)
    return pl.pallas_call(
        _ssd_dt_kernel,
        grid=(m // rows_per_step,),
        in_specs=[pl.BlockSpec((rows_per_step, LANES), lambda i: (i, 0)), row, row],
        out_specs=[full, full, tr],
        out_shape=[full_shape, full_shape, tr_shape],
        compiler_params=_params("parallel"),
        name="ssd_dt",
    )(dt, bias_row, a_row)


def _ssd_kernel(xs_ref, b_ref, c_ref, dts_ref, cum_ref, src_t_ref, cwx_ref, cbx_ref, cwb_ref, cbb_ref,
                cwc_ref, cbc_ref, dskip_ref, h0f_ref, h0b_ref, *rest,
                n_chunks, conv_tile, conv_period, with_output):
    if with_output:
        y_ref, xs_s, b_s, c_s, hf_s, hb_s, y_s, yb_s = rest
    else:
        hf_out, hb_out, xs_s, b_s, c_s, hf_s, hb_s = rest
    seq = n_chunks * CHUNK
    hpg = HEADS_PER_GROUP

    def conv_silu(src_ref, dst_ref, w_ref, bias_ref):
        cols = src_ref.shape[1]
        w1 = w_ref[1:2, :]
        bias = bias_ref[...]
        row = lax.broadcasted_iota(jnp.int32, (conv_tile, cols), 0) % conv_period
        w0 = jnp.where(row == 0, 0.0, w_ref[0:1, :])
        w2 = jnp.where(row == conv_period - 1, 0.0, w_ref[2:3, :])

        def body(k, carry):
            r = pl.ds(pl.multiple_of(k * conv_tile, conv_tile), conv_tile)
            v = src_ref[r, :].astype(F32)
            y = bias + pltpu.roll(v, 1, 0) * w0 + v * w1 + pltpu.roll(v, conv_tile - 1, 0) * w2
            dst_ref[r, :] = _silu(y).astype(BF16)
            return carry

        lax.fori_loop(0, seq // conv_tile, body, 0)

    conv_silu(xs_ref, xs_s, cwx_ref, cbx_ref)
    conv_silu(b_ref, b_s, cwb_ref, cbb_ref)
    conv_silu(c_ref, c_s, cwc_ref, cbc_ref)

    ii = lax.broadcasted_iota(jnp.int32, (CHUNK, CHUNK), 0)
    jj = lax.broadcasted_iota(jnp.int32, (CHUNK, CHUNK), 1)
    lower = ii >= jj
    upper = jj >= ii

    hf_s[...] = h0f_ref[...]
    hb_s[...] = h0b_ref[...]

    head_of_col = lax.broadcasted_iota(jnp.int32, (LANES, GROUP_WIDTH), 1) // SSD_HEAD_DIM
    sel_row = lax.broadcasted_iota(jnp.int32, (LANES, GROUP_WIDTH), 0)
    sel_valid = sel_row < 3 * GROUP_LANES
    sel_f = (sel_valid & (sel_row % GROUP_LANES == head_of_col)).astype(BF16)
    sel_b = (sel_valid & (sel_row % GROUP_LANES == head_of_col + hpg)).astype(BF16)
    low_half = jj < SSD_HEAD_DIM
    group_lane = jj < GROUP_LANES

    def direction_terms(r, sel):
        cum = cum_ref[r, :]
        dts = dts_ref[r, :]
        tot = jnp.where(jj[0:1, :] < hpg, cum[CHUNK - 1:CHUNK, :], cum[0:1, :])
        wst = dts * jnp.exp2(tot - cum)
        eoff = jnp.where(group_lane, jnp.exp2(cum), 0.0)
        edec = jnp.broadcast_to(jnp.where(group_lane[0:1, :], jnp.exp2(tot), 0.0), (8, LANES))
        ex = _dot(_stack_split3(jnp.concatenate([wst, eoff, edec], axis=0)), sel)
        return cum, ex[:CHUNK], ex[CHUNK:2 * CHUNK], ex[2 * CHUNK:2 * CHUNK + 1]

    def state_step(h_s, r, sel):
        cum, w_state, e_off, e_dec = direction_terms(r, sel)
        xc = xs_s[r, :]
        xw = (xc.astype(F32) * w_state).astype(BF16)
        s_new = lax.dot_general(b_s[r, :], xw, (((0,), (0,)), ((), ())), preferred_element_type=F32)
        h = h_s[...]
        y_off = _dot(c_s[r, :], h.astype(BF16)) * e_off
        h_s[...] = h * e_dec + s_new
        return cum, xc, y_off

    def scan_body(k, carry, *, second_half):
        r = pl.ds(pl.multiple_of(k * CHUNK, CHUNK), CHUNK)
        cum, xc, y_off = state_step(hf_s, r, sel_f)
        if with_output:
            g = _dot_nt(c_s[r, :], b_s[r, :])
            rt = pl.ds(pl.multiple_of(k * GROUP_LANES, GROUP_LANES), GROUP_LANES)
            src_t = src_t_ref[rt, :]
            pieces = []
            for pair in range(hpg // 2):
                ms = []
                for h in (2 * pair, 2 * pair + 1):
                    hb = h + hpg
                    mf = jnp.exp2(jnp.where(lower, cum[:, h:h + 1] - src_t[h:h + 1, :], NEG_BIG))
                    mb = jnp.exp2(jnp.where(upper, cum[:, hb:hb + 1] - src_t[hb:hb + 1, :], NEG_BIG))
                    ms.append((g * (mf + mb)).astype(BF16))
                lhs = jnp.concatenate(ms, axis=1)
                xp = xc[:, pair * LANES:(pair + 1) * LANES]
                zero = jnp.zeros_like(xp)
                rhs = jnp.concatenate([jnp.where(low_half, xp, zero), jnp.where(low_half, zero, xp)], axis=0)
                pieces.append(_dot(lhs, rhs))
            y_fwd = jnp.concatenate(pieces, axis=1) + y_off + xc.astype(F32) * dskip_ref[...]
            if second_half:
                y_ref[r, :] = (y_fwd + yb_s[r, :]).astype(y_ref.dtype)
            else:
                y_s[r, :] = y_fwd
        rb = pl.ds(pl.multiple_of((n_chunks - 1 - k) * CHUNK, CHUNK), CHUNK)
        _, _, y_off_b = state_step(hb_s, rb, sel_b)
        if with_output:
            if second_half:
                y_ref[rb, :] = (y_s[rb, :] + y_off_b).astype(y_ref.dtype)
            else:
                yb_s[rb, :] = y_off_b
        return carry

    half = n_chunks // 2
    unroll = 8 if half % 8 == 0 else 1
    lax.fori_loop(0, half, functools.partial(scan_body, second_half=False), 0, unroll=unroll)
    lax.fori_loop(half, n_chunks, functools.partial(scan_body, second_half=True), 0, unroll=unroll)

    if not with_output:
        hf_out[...] = hf_s[...]
        hb_out[...] = hb_s[...]


def _ssd(pm, xs_col0, bc_col0, dt_terms, conv_w, conv_b, dskip_g, h0f, h0b, *, batch, seq, conv_tile, conv_period,
         with_output):
    n_chunks = seq // CHUNK
    dts_g, cum_g, src_t_g = dt_terms
    dt_spec = pl.BlockSpec((None, seq, LANES), lambda b, g: (g, b, 0))
    dt_t_spec = pl.BlockSpec((None, n_chunks * GROUP_LANES, LANES), lambda b, g: (g, b, 0))
    xs_blk0 = xs_col0 // GROUP_WIDTH
    b_blk0 = bc_col0 // SSD_STATE
    c_blk0 = b_blk0 + SSD_GROUPS
    cw_b0 = SSD_WIDTH // SSD_STATE
    cw_c0 = cw_b0 + SSD_GROUPS
    state_spec = pl.BlockSpec((None, None, SSD_STATE, GROUP_WIDTH), lambda b, g: (b, g, 0, 0))
    in_specs = [
        pl.BlockSpec((seq, GROUP_WIDTH), lambda b, g: (b, xs_blk0 + g)),
        pl.BlockSpec((seq, SSD_STATE), lambda b, g: (b, b_blk0 + g)),
        pl.BlockSpec((seq, SSD_STATE), lambda b, g: (b, c_blk0 + g)),
        dt_spec,
        dt_spec,
        dt_t_spec,
        pl.BlockSpec((3, GROUP_WIDTH), lambda b, g: (0, g)),
        pl.BlockSpec((1, GROUP_WIDTH), lambda b, g: (0, g)),
        pl.BlockSpec((3, SSD_STATE), lambda b, g: (0, cw_b0 + g)),
        pl.BlockSpec((1, SSD_STATE), lambda b, g: (0, cw_b0 + g)),
        pl.BlockSpec((3, SSD_STATE), lambda b, g: (0, cw_c0 + g)),
        pl.BlockSpec((1, SSD_STATE), lambda b, g: (0, cw_c0 + g)),
        pl.BlockSpec((None, 1, GROUP_WIDTH), lambda b, g: (g, 0, 0)),
        state_spec,
        state_spec,
    ]
    scratch = [
        pltpu.VMEM((seq, GROUP_WIDTH), BF16),
        pltpu.VMEM((seq, SSD_STATE), BF16),
        pltpu.VMEM((seq, SSD_STATE), BF16),
        pltpu.VMEM((SSD_STATE, GROUP_WIDTH), F32),
        pltpu.VMEM((SSD_STATE, GROUP_WIDTH), F32),
    ]
    if with_output:
        out_specs = pl.BlockSpec((seq, GROUP_WIDTH), lambda b, g: (b, g))
        out_shape = jax.ShapeDtypeStruct((batch * seq, SSD_WIDTH), BF16)
        scratch += [pltpu.VMEM((seq, GROUP_WIDTH), F32)] * 2
    else:
        out_specs = [state_spec, state_spec]
        out_shape = [jax.ShapeDtypeStruct((batch, SSD_GROUPS, SSD_STATE, GROUP_WIDTH), F32)] * 2
    return pl.pallas_call(
        functools.partial(_ssd_kernel, n_chunks=n_chunks, conv_tile=conv_tile, conv_period=conv_period,
                          with_output=with_output),
        grid=(batch, SSD_GROUPS),
        in_specs=in_specs,
        out_specs=out_specs,
        out_shape=out_shape,
        scratch_shapes=scratch,
        compiler_params=_params("parallel", "parallel"),
        name="ssd_out" if with_output else "ssd_ctx",
    )(pm, pm, pm, dts_g, cum_g, src_t_g, conv_w, conv_b, conv_w, conv_b, conv_w, conv_b, dskip_g, h0f, h0b)


def _out_proj_kernel(yd_ref, z_ref, u_ref, v_ref, x_ref, g1_ref, ng_ref, lng_ref, lnb_ref, ws_ref, bst_ref, w_ref,
                     o_ref, mix_cur, mix_nxt):
    tm = yd_ref.shape[0]
    s = pl.program_id(0)
    n = pl.num_programs(0) - 1

    def build_steps():
        steps = []
        for k in range(tm // CHUNK):
            r = pl.ds(k * CHUNK, CHUNK)
            shared = {}

            def ssd_part(r=r):
                a = yd_ref[r, :].astype(F32) * z_ref[r, :].astype(F32)
                ms = jnp.mean(a * a, axis=-1, keepdims=True)
                mix_nxt[r, 0:SSD_WIDTH] = (a * lax.rsqrt(ms + NORM_EPS) * ng_ref[...]).astype(BF16)

            def ln_part(r=r, shared=shared):
                gv = v_ref[r, :].astype(F32)
                mu = jnp.mean(gv, axis=-1, keepdims=True)
                xc = gv - mu
                var = jnp.mean(xc * xc, axis=-1, keepdims=True)
                shared["ln"] = ((xc * lax.rsqrt(var + NORM_EPS)) * lng_ref[...] + lnb_ref[...]).astype(BF16)

            def gate_part(h0, h1, r=r, shared=shared):
                for h in range(h0, h1):
                    c0, c1 = h * CM_HEAD_DIM, (h + 1) * CM_HEAD_DIM
                    sp = _dot(ws_ref[h], shared["ln"][:, c0:c1]) + bst_ref[:, h:h + 1]
                    gu = u_ref[r, c0:c1].astype(F32)
                    mix_nxt[r, SSD_WIDTH + c0:SSD_WIDTH + c1] = (gu * sp).astype(BF16)

            steps += [ssd_part, ln_part, functools.partial(gate_part, 0, CM_HEADS // 2),
                      functools.partial(gate_part, CM_HEADS // 2, CM_HEADS)]
        return steps

    n_proj = 8
    pw = D_MODEL // n_proj

    def project_steps():
        def piece(q):
            c = slice(q * pw, (q + 1) * pw)
            o_ref[:, c] = x_ref[:, c] + g1_ref[:, c] * _dot(mix_cur[...], w_ref[:, c])

        return [functools.partial(piece, q) for q in range(n_proj)]

    @pl.when(s == 0)
    def _():
        for step in build_steps():
            step()

    @pl.when((s > 0) & (s < n))
    def _():
        for proj, build in zip(project_steps(), build_steps()):
            proj()
            build()

    @pl.when(s == n)
    def _():
        for step in project_steps():
            step()

    @pl.when(s < n)
    def _():
        mix_cur[...] = mix_nxt[...]


def _out_proj(yd, pm, x2, mod3, ssd_norm_g, ln_g, ln_b, ws, bst, w_out, *, seq, tm):
    m = x2.shape[0]
    n = m // tm
    tiles_per_batch = seq // tm
    z_blk, u_blk, v_blk = 0, 2 * SSD_WIDTH // CM_WIDTH, 2 * SSD_WIDTH // CM_WIDTH + 1
    g1_blk = 2
    row = lambda shape: pl.BlockSpec(shape, lambda s: (0, 0))
    build_tile = lambda s: jnp.minimum(s, n - 1)
    proj_tile = lambda s: jnp.maximum(s - 1, 0)
    return pl.pallas_call(
        _out_proj_kernel,
        grid=(n + 1,),
        in_specs=[
            pl.BlockSpec((tm, SSD_WIDTH), lambda s: (build_tile(s), 0)),
            pl.BlockSpec((tm, SSD_WIDTH), lambda s: (build_tile(s), z_blk)),
            pl.BlockSpec((tm, CM_WIDTH), lambda s: (build_tile(s), u_blk)),
            pl.BlockSpec((tm, CM_WIDTH), lambda s: (build_tile(s), v_blk)),
            pl.BlockSpec((tm, D_MODEL), lambda s: (proj_tile(s), 0)),
            pl.BlockSpec((None, 1, D_MODEL), lambda s: (proj_tile(s) // tiles_per_batch, 0, g1_blk)),
            row((1, SSD_WIDTH)),
            row((1, CM_WIDTH)),
            row((1, CM_WIDTH)),
            pl.BlockSpec((CM_HEADS, CHUNK, CHUNK), lambda s: (0, 0, 0)),
            row((CHUNK, CM_HEADS)),
            pl.BlockSpec((SSD_WIDTH + CM_WIDTH, D_MODEL), lambda s: (0, 0), pipeline_mode=pl.Buffered(1)),
        ],
        out_specs=pl.BlockSpec((tm, D_MODEL), lambda s: (proj_tile(s), 0)),
        out_shape=jax.ShapeDtypeStruct((m, D_MODEL), F32),
        scratch_shapes=[pltpu.VMEM((tm, SSD_WIDTH + CM_WIDTH), BF16)] * 2,
        compiler_params=_params("arbitrary"),
        name="out_proj",
    )(yd, pm, pm, pm, x2, mod3, ssd_norm_g, ln_g, ln_b, ws, bst, w_out)


def _route(logits):
    lane = lax.broadcasted_iota(jnp.int32, logits.shape, 1)
    lane_f = lane.astype(F32)
    is_group = (lane >= N_EXPERTS) & (lane < N_EXPERTS + N_GROUPS)
    lg = jnp.where(is_group, logits, NEG_BIG)
    mg = jnp.max(lg, axis=1, keepdims=True)
    top_pg = 1.0 / jnp.sum(jnp.exp(lg - mg), axis=1, keepdims=True)
    gi = jnp.min(jnp.where(lg == mg, lane_f, 1e9), axis=1, keepdims=True) - N_EXPERTS
    in_group = (lane < N_EXPERTS) & ((lane // EXPERTS_PER_GROUP).astype(F32) == gi)
    le = jnp.where(in_group, logits, NEG_BIG)
    m1 = jnp.max(le, axis=1, keepdims=True)
    i1 = jnp.min(jnp.where(le == m1, lane_f, 1e9), axis=1, keepdims=True)
    le2 = jnp.where(lane_f == i1, NEG_BIG, le)
    m2 = jnp.max(le2, axis=1, keepdims=True)
    i2 = jnp.min(jnp.where(le2 == m2, lane_f, 1e9), axis=1, keepdims=True)
    e2 = jnp.exp(m2 - m1)
    p1 = 1.0 / (1.0 + e2)
    p2 = e2 * p1
    return i1, i2, p1 * top_pg, p2 * top_pg


RINFO_E1, RINFO_E2, RINFO_W1, RINFO_W2, RINFO_R1, RINFO_R2 = range(6)


def _route_kernel(x1_ref, n2g_ref, sh2_ref, sc2_ref, wr_ref, br_ref, h2_ref, rinfo_ref, counts_ref, cnt_scr):
    tm = x1_ref.shape[0]

    @pl.when(pl.program_id(0) == 0)
    def _():
        cnt_scr[...] = jnp.zeros_like(cnt_scr)

    x1 = x1_ref[...]
    ms = jnp.mean(x1 * x1, axis=-1, keepdims=True)
    h2 = (x1 * lax.rsqrt(ms + NORM_EPS) * n2g_ref[...]) * (1.0 + sc2_ref[...]) + sh2_ref[...]
    h2_ref[...] = h2
    hi, mid, _ = _split3(h2)
    p = _dot(hi, wr_ref[...])
    logits = p[:, :LANES] + p[:, LANES:] + _dot(mid, wr_ref[:, :LANES]) + br_ref[...]
    i1, i2, w1, w2 = _route(logits)

    lane = lax.broadcasted_iota(jnp.int32, (tm, LANES), 1)
    lane_f = lane.astype(F32)
    oh1 = jnp.where(lane_f == i1, 1.0, 0.0)
    oh2 = jnp.where(lane_f == i2, 1.0, 0.0)
    before = (lax.broadcasted_iota(jnp.int32, (tm, tm), 0) > lax.broadcasted_iota(jnp.int32, (tm, tm), 1)).astype(BF16)
    carried = cnt_scr[...]
    tot1 = jnp.sum(oh1, axis=0, keepdims=True)
    r1 = jnp.sum(oh1 * (_dot(before, oh1.astype(BF16)) + carried), axis=1, keepdims=True)
    r2 = jnp.sum(oh2 * (_dot(before, oh2.astype(BF16)) + (carried + tot1)), axis=1, keepdims=True)
    counts = carried + tot1 + jnp.sum(oh2, axis=0, keepdims=True)
    cnt_scr[...] = counts
    counts_ref[...] = jnp.broadcast_to(counts, counts_ref.shape)

    info = jnp.zeros((tm, LANES), F32)
    for k, v in ((RINFO_E1, i1), (RINFO_E2, i2), (RINFO_W1, w1), (RINFO_W2, w2), (RINFO_R1, r1), (RINFO_R2, r2)):
        info = jnp.where(lane == k, v, info)
    rinfo_ref[...] = info


def _route_call(x1, mod3, norm2_g, w_router, b_router, *, seq, tm):
    m = x1.shape[0]
    tiles_per_batch = seq // tm
    row = lambda shape: pl.BlockSpec(shape, lambda i: (0, 0))
    modrow = lambda k: pl.BlockSpec((None, 1, D_MODEL), lambda i: (i // tiles_per_batch, 0, k))
    return pl.pallas_call(
        _route_kernel,
        grid=(m // tm,),
        in_specs=[
            pl.BlockSpec((tm, D_MODEL), lambda i: (i, 0)),
            row((1, D_MODEL)),
            modrow(3),
            modrow(4),
            row((D_MODEL, 2 * LANES)),
            row((1, LANES)),
        ],
        out_specs=[
            pl.BlockSpec((tm, D_MODEL), lambda i: (i, 0)),
            pl.BlockSpec((tm, LANES), lambda i: (i, 0)),
            row((8, LANES)),
        ],
        out_shape=[
            jax.ShapeDtypeStruct((m, D_MODEL), F32),
            jax.ShapeDtypeStruct((m, LANES), F32),
            jax.ShapeDtypeStruct((8, LANES), F32),
        ],
        scratch_shapes=[pltpu.VMEM((1, LANES), F32)],
        compiler_params=_params("arbitrary"),
        name="route",
    )(x1, norm2_g, mod3, mod3, w_router, b_router)


SUBLANES = 8


def _dispatch_kernel(ps_ref, pl_ref, nu_ref, pos_ref, h2_ref, x_hbm, zbuf, zsem, sem, *, tile_rows, n_tiles):
    tm = h2_ref.shape[0]
    n_used = nu_ref[0]

    def zero_fill(phase):
        def go(rows, start):
            getattr(pltpu.make_async_copy(zbuf.at[pl.ds(0, rows), :], x_hbm.at[pl.ds(start, rows), :], zsem), phase)()

        for e in range(N_EXPERTS):
            start, length = ps_ref[e], pl_ref[e]
            head = (SUBLANES - (start & (SUBLANES - 1))) & (SUBLANES - 1)
            for q in range(SUBLANES - 1):
                pl.when(q < head)(functools.partial(go, 1, start + q))
            rem = length - head
            blk = tile_rows // 2
            while blk >= SUBLANES:
                off = pl.multiple_of(start + head + (rem & ~(2 * blk - 1)), SUBLANES)
                pl.when((rem & blk) != 0)(functools.partial(go, blk, off))
                blk //= 2

        def unused(t, carry):
            go(tile_rows, pl.multiple_of(t * tile_rows, tile_rows))
            return carry

        lax.fori_loop(n_used, n_tiles, unused, 0)

    @pl.when(pl.program_id(0) == 0)
    def _():
        zbuf[...] = jnp.zeros_like(zbuf)
        zero_fill("start")

    def body(r, carry):
        for k in range(2):
            pltpu.make_async_copy(h2_ref.at[pl.ds(r, 1), :], x_hbm.at[pl.ds(pos_ref[0, 2 * r + k], 1), :],
                                  sem).start(priority=k)
        return carry

    lax.fori_loop(0, tm, body, 0, unroll=8)
    for k in range(2):
        pltpu.make_async_copy(h2_ref, x_hbm.at[pl.ds(0, tm), :], sem).wait()

    @pl.when(pl.program_id(0) == pl.num_programs(0) - 1)
    def _():
        zero_fill("wait")


def _dispatch_call(pad_start, pad_len, n_used, pos, h2, *, tm, tile_rows, n_tiles):
    m = h2.shape[0]
    grid_spec = pltpu.PrefetchScalarGridSpec(
        num_scalar_prefetch=3,
        grid=(m // tm,),
        in_specs=[
            pl.BlockSpec((None, 1, 2 * tm), lambda i, ps, pn, nu: (i, 0, 0), memory_space=pltpu.SMEM),
            pl.BlockSpec((tm, D_MODEL), lambda i, ps, pn, nu: (i, 0)),
        ],
        out_specs=pl.BlockSpec(memory_space=pl.ANY),
        scratch_shapes=[
            pltpu.VMEM((tile_rows, D_MODEL), F32),
            pltpu.SemaphoreType.DMA(()),
            pltpu.SemaphoreType.DMA(()),
        ],
    )
    return pl.pallas_call(
        functools.partial(_dispatch_kernel, tile_rows=tile_rows, n_tiles=n_tiles),
        grid_spec=grid_spec,
        out_shape=jax.ShapeDtypeStruct((n_tiles * tile_rows, D_MODEL), F32),
        compiler_params=_params("arbitrary"),
        name="dispatch",
    )(pad_start, pad_len, n_used, pos, h2)


def _experts_kernel(te_ref, nu_ref, ord_ref, nxt_ref, x_ref, wg_hbm, wu_hbm, wd_hbm, y_ref,
                    wg_f, wu_f, wd_f, wg_b, wu_b, wd_b, sem):
    j = pl.program_id(0)
    n_used = nu_ref[0]

    def weight_copies(e, slot):
        return [pltpu.make_async_copy(src.at[e], dst.at[slot], sem.at[slot])
                for src, dst in ((wg_hbm, wg_f), (wu_hbm, wu_f), (wd_hbm, wd_f))]

    @pl.when(j == 0)
    def _():
        for cp in weight_copies(te_ref[0], 0):
            cp.start()

    @pl.when((j < n_used) & ((j == 0) | (te_ref[j] != te_ref[jnp.maximum(j - 1, 0)])))
    def _():
        slot = ord_ref[j] % 2
        for cp in weight_copies(te_ref[j], slot):
            cp.wait()

        @pl.when(nxt_ref[j] >= 0)
        def _():
            for cp in weight_copies(nxt_ref[j], 1 - slot):
                cp.start()

        wg_b[...] = wg_f[slot].astype(BF16)
        wu_b[...] = wu_f[slot].astype(BF16)
        wd_b[...] = wd_f[slot].astype(BF16)

    @pl.when(j < n_used)
    def _():
        xt = x_ref[...].astype(BF16)
        hid = (_silu(_dot(xt, wg_b[...])) * _dot(xt, wu_b[...])).astype(BF16)
        y_ref[...] = _dot(hid, wd_b[...])

    @pl.when(j >= n_used)
    def _():
        y_ref[...] = jnp.zeros_like(y_ref)


def _experts_call(tile_expert, n_used, tile_ord, next_expert, x_sorted, wg, wu, wd, *, tm):
    n_tiles = tile_expert.shape[0]
    hbm = pl.BlockSpec(memory_space=pl.ANY)
    grid_spec = pltpu.PrefetchScalarGridSpec(
        num_scalar_prefetch=4,
        grid=(n_tiles,),
        in_specs=[
            pl.BlockSpec((tm, D_MODEL), lambda j, te, nu, od, nx: (jnp.minimum(j, nu[0] - 1), 0)),
            hbm,
            hbm,
            hbm,
        ],
        out_specs=pl.BlockSpec((tm, D_MODEL), lambda j, te, nu, od, nx: (j, 0)),
        scratch_shapes=[
            pltpu.VMEM((2, D_MODEL, EXPERT_FF), F32),
            pltpu.VMEM((2, D_MODEL, EXPERT_FF), F32),
            pltpu.VMEM((2, EXPERT_FF, D_MODEL), F32),
            pltpu.VMEM((D_MODEL, EXPERT_FF), BF16),
            pltpu.VMEM((D_MODEL, EXPERT_FF), BF16),
            pltpu.VMEM((EXPERT_FF, D_MODEL), BF16),
            pltpu.SemaphoreType.DMA((2,)),
        ],
    )
    return pl.pallas_call(
        _experts_kernel,
        grid_spec=grid_spec,
        out_shape=jax.ShapeDtypeStruct((n_tiles * tm, D_MODEL), F32),
        compiler_params=_params("arbitrary"),
        name="experts",
    )(tile_expert, n_used, tile_ord, next_expert, x_sorted, wg, wu, wd)


def _combine_kernel(pos_cur, pos_nxt, x1_ref, g2_ref, nfg_ref, rinfo_ref, y_hbm, o_ref, ybuf, sem):
    i = pl.program_id(0)
    tm = x1_ref.shape[0]
    slot = i % 2

    def row_copy(pos_ref, r, k, s):
        return pltpu.make_async_copy(y_hbm.at[pl.ds(pos_ref[0, 2 * r + k], 1), :], ybuf.at[s, k, pl.ds(r, 1), :],
                                     sem.at[s])

    def wait_tile(s):
        for k in range(2):
            pltpu.make_async_copy(y_hbm.at[pl.ds(0, tm), :], ybuf.at[s, k], sem.at[s]).wait()

    @pl.when(i == 0)
    def _():
        def body(r, carry):
            for k in range(2):
                row_copy(pos_cur, r, k, 0).start()
            return carry

        lax.fori_loop(0, tm, body, 0)

    def step(cur):
        wait_tile(cur)
        for r in range(tm):
            for k in range(2):
                row_copy(pos_nxt, r, k, 1 - cur).start(priority=k)
        info = rinfo_ref[...]
        w1 = info[:, RINFO_W1:RINFO_W1 + 1]
        w2 = info[:, RINFO_W2:RINFO_W2 + 1]
        moe = w1 * ybuf[cur, 0] + w2 * ybuf[cur, 1]
        y = x1_ref[...] + g2_ref[...] * moe
        ms = jnp.mean(y * y, axis=-1, keepdims=True)
        o_ref[...] = y * lax.rsqrt(ms + NORM_EPS) * nfg_ref[...]

        @pl.when(i == pl.num_programs(0) - 1)
        def _():
            wait_tile(1 - cur)

    for cur in range(2):
        pl.when(slot == cur)(functools.partial(step, cur))


def _combine_call(pos, x1, mod3, normf_g, rinfo, y_sorted, *, seq, tm):
    m = x1.shape[0]
    n_tiles = m // tm
    tiles_per_batch = seq // tm
    return pl.pallas_call(
        _combine_kernel,
        grid=(n_tiles,),
        in_specs=[
            pl.BlockSpec((None, 1, 2 * tm), lambda i: (i, 0, 0), memory_space=pltpu.SMEM),
            pl.BlockSpec((None, 1, 2 * tm), lambda i: (jnp.minimum(i + 1, n_tiles - 1), 0, 0),
                         memory_space=pltpu.SMEM),
            pl.BlockSpec((tm, D_MODEL), lambda i: (i, 0)),
            pl.BlockSpec((None, 1, D_MODEL), lambda i: (i // tiles_per_batch, 0, 5)),
            pl.BlockSpec((1, D_MODEL), lambda i: (0, 0)),
            pl.BlockSpec((tm, LANES), lambda i: (i, 0)),
            pl.BlockSpec(memory_space=pl.ANY),
        ],
        out_specs=pl.BlockSpec((tm, D_MODEL), lambda i: (i, 0)),
        out_shape=jax.ShapeDtypeStruct((m, D_MODEL), F32),
        scratch_shapes=[
            pltpu.VMEM((2, 2, tm, D_MODEL), F32),
            pltpu.SemaphoreType.DMA((2,)),
        ],
        compiler_params=_params("arbitrary"),
        name="combine",
    )(pos, pos, x1, mod3, normf_g, rinfo, y_sorted)


def _dispatch_plan(rinfo, counts, *, tm):
    n_tok = rinfo.shape[0]
    n_tiles = (2 * n_tok + N_EXPERTS * (tm - 1)) // tm + 1
    expert = rinfo[:, RINFO_E1:RINFO_E2 + 1].astype(jnp.int32)
    rank = rinfo[:, RINFO_R1:RINFO_R2 + 1].astype(jnp.int32)
    cnt = counts[0, :N_EXPERTS].astype(jnp.int32)
    tiles_e = (cnt + tm - 1) // tm
    end_tile = jnp.cumsum(tiles_e)
    start_row = (end_tile - tiles_e) * tm
    is_e = expert[:, :, None] == jnp.arange(N_EXPERTS, dtype=jnp.int32)
    pos = jnp.sum(jnp.where(is_e, start_row, 0), axis=-1) + rank
    n_used = end_tile[-1:]
    tile_ids = jnp.minimum(jnp.arange(n_tiles, dtype=jnp.int32), n_used - 1)
    tile_expert = jnp.sum(tile_ids[:, None] >= end_tile[None, :], axis=1).astype(jnp.int32)
    pad_rows = ((start_row + cnt).astype(jnp.int32), (tiles_e * tm - cnt).astype(jnp.int32))
    used = (tiles_e > 0).astype(jnp.int32)
    ord_e = jnp.cumsum(used) - used
    ids = jnp.arange(N_EXPERTS, dtype=jnp.int32)
    later = jnp.where((ids[None, :] > ids[:, None]) & (used[None, :] > 0), ids[None, :], N_EXPERTS)
    nxt_e = jnp.min(later, axis=1)
    nxt_e = jnp.where(nxt_e < N_EXPERTS, nxt_e, -1).astype(jnp.int32)
    is_te = tile_expert[:, None] == ids[None, :]
    tile_ord = jnp.sum(jnp.where(is_te, ord_e, 0), axis=1).astype(jnp.int32)
    next_expert = jnp.sum(jnp.where(is_te, nxt_e, 0), axis=1).astype(jnp.int32)
    return pos, (tile_expert, n_used.astype(jnp.int32), tile_ord, next_expert), pad_rows, n_tiles


def kernel(x, c, ctx, c_ctx, w_mod, b_mod, norm1_g, w_in, conv_w, conv_b, dt_bias_f, dt_bias_b, a_log_f, a_log_b,
           d_skip, ssd_norm_g, cm_ln_g, cm_ln_b, w_spatial, b_spatial, w_out, norm2_g, w_router_group,
           b_router_group, w_router_expert, b_router_expert, w_exp_gate, w_exp_up, w_exp_down, normf_g):
    bsz, seq, _ = x.shape
    ctx_len = ctx.shape[1]
    i = 0

    cc = jnp.concatenate([c, c_ctx[None, :], jnp.zeros((MOD_ROWS - bsz - 1, D_MODEL), F32)], axis=0)
    mod = _modulation(cc, w_mod[i], b_mod[i][None, :])
    mod3 = mod.reshape(MOD_ROWS, 1, N_MOD * D_MODEL)

    w_main, w_dt = _w_in_prep(jnp.swapaxes(w_in[i], 0, 1))
    g1row = norm1_g[i][None, :]

    x2 = x.reshape(bsz * seq, D_MODEL)
    ctx2 = ctx.reshape(bsz * ctx_len, D_MODEL)
    tm_in = 1024
    tn_in = 1024
    xs_col0, bc_col0 = SSD_WIDTH, 2 * SSD_WIDTH + 2 * CM_WIDTH
    pm_x, dt_x = _in_proj(x2, mod3, lambda t: t // (seq // tm_in), g1row, w_main, w_dt, tm_in, tn_in)
    ctx_blocks = tuple(range(xs_col0 // tn_in, 2 * SSD_WIDTH // tn_in)) + (bc_col0 // tn_in,)
    pm_c, dt_c = _in_proj(ctx2, mod3, lambda t: bsz, g1row, w_main, w_dt, bsz * ctx_len, tn_in, ctx_blocks)

    def dt_lanes(fwd, bwd):
        both = jnp.stack([fwd.reshape(SSD_GROUPS, HEADS_PER_GROUP), bwd.reshape(SSD_GROUPS, HEADS_PER_GROUP)], axis=1)
        return jnp.pad(both.reshape(-1).astype(F32), (0, LANES - 2 * SSD_HEADS))[None, :]

    dt_bias_row = dt_lanes(dt_bias_f[i], dt_bias_b[i])
    a_row = dt_lanes(-jnp.exp(a_log_f[i].astype(F32)), -jnp.exp(a_log_b[i].astype(F32)))
    dskip_g = jnp.repeat(d_skip[i], SSD_HEAD_DIM).reshape(SSD_GROUPS, 1, GROUP_WIDTH)
    cw = conv_w[i]
    cb = conv_b[i][None, :]
    h_zero = jnp.zeros((bsz, SSD_GROUPS, SSD_STATE, GROUP_WIDTH), F32)

    dt_terms_c = _ssd_dt(dt_c, dt_bias_row, a_row, rows_per_step=bsz * ctx_len)
    dt_terms_x = _ssd_dt(dt_x, dt_bias_row, a_row, rows_per_step=1024)
    hc_f, hc_b = _ssd(pm_c, 0, SSD_WIDTH, dt_terms_c, cw, cb, dskip_g, h_zero, h_zero,
                      batch=bsz, seq=ctx_len, conv_tile=ctx_len, conv_period=ctx_len, with_output=False)
    yd = _ssd(pm_x, xs_col0, bc_col0, dt_terms_x, cw, cb, dskip_g, hc_f, hc_b,
              batch=bsz, seq=seq, conv_tile=CHUNK, conv_period=GRID_W, with_output=True)

    x1 = _out_proj(yd, pm_x, x2, mod3, ssd_norm_g[i][None, :], cm_ln_g[i][None, :], cm_ln_b[i][None, :],
                   w_spatial[i].astype(BF16), b_spatial[i].T, w_out[i].astype(BF16), seq=seq, tm=256)

    w_re = jnp.transpose(w_router_expert[i], (1, 0, 2)).reshape(D_MODEL, N_EXPERTS)
    pad = LANES - N_EXPERTS - N_GROUPS
    w_router = jnp.pad(jnp.concatenate([w_re, w_router_group[i]], axis=1), ((0, 0), (0, pad)))
    w_router_hi = w_router.astype(BF16)
    w_router = jnp.concatenate([w_router_hi, (w_router - w_router_hi.astype(F32)).astype(BF16)], axis=1)
    b_router = jnp.pad(jnp.concatenate([b_router_expert[i].reshape(-1), b_router_group[i]]), (0, pad))[None, :]
    h2, rinfo, counts = _route_call(x1, mod3, norm2_g[i][None, :], w_router, b_router, seq=seq, tm=512)

    tm_e = 256
    pos, tile_tables, pad_rows, n_tiles = _dispatch_plan(rinfo, counts, tm=tm_e)
    tm_d = 512
    x_sorted = _dispatch_call(*pad_rows, tile_tables[1], pos.reshape(bsz * seq // tm_d, 1, 2 * tm_d), h2,
                              tm=tm_d, tile_rows=tm_e, n_tiles=n_tiles)
    wg = w_exp_gate[i].reshape(N_EXPERTS, D_MODEL, EXPERT_FF)
    wu = w_exp_up[i].reshape(N_EXPERTS, D_MODEL, EXPERT_FF)
    wd = w_exp_down[i].reshape(N_EXPERTS, EXPERT_FF, D_MODEL)
    y_sorted = _experts_call(*tile_tables, x_sorted, wg, wu, wd, tm=tm_e)
    tm_c = 256
    out = _combine_call(pos.reshape(bsz * seq // tm_c, 1, 2 * tm_c), x1, mod3, normf_g[None, :], rinfo, y_sorted,
                        seq=seq, tm=tm_c)
    return out.reshape(bsz, seq, D_MODEL)
```

```python
import functools

import jax
import jax.numpy as jnp
from jax import lax
from jax.experimental import pallas as pl
from jax.experimental.pallas import tpu as pltpu

F32 = jnp.float32
BF16 = jnp.bfloat16
HIGHEST = lax.Precision.HIGHEST

D_MODEL = 2048
GRID_W = 64
SSD_WIDTH = 2048
CM_WIDTH = 2048
SSD_HEADS = 32
SSD_HEAD_DIM = 64
SSD_GROUPS = 4
HEADS_PER_GROUP = SSD_HEADS // SSD_GROUPS
GROUP_WIDTH = HEADS_PER_GROUP * SSD_HEAD_DIM
SSD_STATE = 128
CHUNK = 128
BC_WIDTH = SSD_GROUPS * SSD_STATE
CM_HEADS = 8
CM_HEAD_DIM = CM_WIDTH // CM_HEADS
N_GROUPS = 4
N_EXPERTS = 32
EXPERTS_PER_GROUP = 8
EXPERT_FF = 512
N_MOD = 6
NORM_EPS = 1e-6
LANES = 128
NEG_BIG = -1e30
MOD_ROWS = 8
VMEM_LIMIT = 56 * 1024 * 1024


def _params(*sem):
    return pltpu.CompilerParams(dimension_semantics=sem, vmem_limit_bytes=VMEM_LIMIT)


LOG2_E = 1.4426950408889634
GELU_C = 0.7978845608028654
GELU_K1 = -2.0 * GELU_C * LOG2_E
GELU_K3 = GELU_K1 * 0.044715


def _silu(v):
    return v / (1.0 + jnp.exp2(v * (-LOG2_E)))


def _gelu_tanh(v):
    return v / (1.0 + jnp.exp2(v * (GELU_K1 + GELU_K3 * (v * v))))


def _dot(a, b):
    return jnp.dot(a, b, preferred_element_type=F32)


def _dot_nt(a, b):
    return lax.dot_general(a, b, (((1,), (1,)), ((), ())), preferred_element_type=F32)


def _mod_kernel(cc_ref, w_ref, b_ref, o_ref):
    a = _silu(cc_ref[...])
    hi = a.astype(BF16).astype(F32)
    lhs = jnp.concatenate([hi, a - hi], axis=0).astype(BF16)
    r = _dot(lhs, w_ref[...].astype(BF16))
    o_ref[...] = r[:MOD_ROWS] + r[MOD_ROWS:] + b_ref[...]


def _modulation(cc, w_mod, b_mod):
    n = w_mod.shape[1]
    tn = 1024
    return pl.pallas_call(
        _mod_kernel,
        grid=(n // tn,),
        in_specs=[
            pl.BlockSpec((MOD_ROWS, D_MODEL), lambda j: (0, 0)),
            pl.BlockSpec((D_MODEL, tn), lambda j: (0, j)),
            pl.BlockSpec((1, tn), lambda j: (0, j)),
        ],
        out_specs=pl.BlockSpec((MOD_ROWS, tn), lambda j: (0, j)),
        out_shape=jax.ShapeDtypeStruct((MOD_ROWS, n), F32),
        compiler_params=_params("arbitrary"),
        name="modulation",
    )(cc, w_mod, b_mod)


PREP_ROWS = 512
PREP_UV_BLK0 = 2 * SSD_WIDTH // PREP_ROWS
PREP_BC_BLK0 = (2 * SSD_WIDTH + 2 * CM_WIDTH) // PREP_ROWS
PREP_N_BLKS = PREP_BC_BLK0 + 2 * BC_WIDTH // PREP_ROWS
DT_COL0 = 2 * SSD_WIDTH + 2 * BC_WIDTH
UV_COL0 = DT_COL0 + 2 * SSD_HEADS


def _w_in_prep_kernel(a_ref, dt_ref, o_ref, wdt_ref):
    @pl.when(pl.program_id(0) == 0)
    def _():
        hpg = HEADS_PER_GROUP
        parts = []
        for g in range(SSD_GROUPS):
            parts += [dt_ref[hpg * g:hpg * (g + 1), :], dt_ref[SSD_HEADS + hpg * g:SSD_HEADS + hpg * (g + 1), :]]
        parts.append(dt_ref[2 * SSD_HEADS:, :])
        wdt_ref[...] = jnp.concatenate(parts, axis=0).astype(BF16)

    o_ref[...] = a_ref[...].astype(BF16)


def _w_in_prep(w_in_t):
    def src_row(j):
        uv = UV_COL0 + (j - PREP_UV_BLK0) * PREP_ROWS
        bc = 2 * SSD_WIDTH + (j - PREP_BC_BLK0) * PREP_ROWS
        return pl.multiple_of(jnp.where(j < PREP_UV_BLK0, j * PREP_ROWS, jnp.where(j < PREP_BC_BLK0, uv, bc)), 8)

    return pl.pallas_call(
        _w_in_prep_kernel,
        grid=(PREP_N_BLKS,),
        in_specs=[
            pl.BlockSpec((pl.Element(PREP_ROWS), pl.Element(D_MODEL)), lambda j: (src_row(j), 0)),
            pl.BlockSpec((LANES, D_MODEL), lambda j: (DT_COL0 // LANES, 0)),
        ],
        out_specs=[
            pl.BlockSpec((PREP_ROWS, D_MODEL), lambda j: (j, 0)),
            pl.BlockSpec((LANES, D_MODEL), lambda j: (0, 0)),
        ],
        out_shape=[
            jax.ShapeDtypeStruct((PREP_N_BLKS * PREP_ROWS, D_MODEL), BF16),
            jax.ShapeDtypeStruct((LANES, D_MODEL), BF16),
        ],
        compiler_params=_params("arbitrary"),
        name="w_in_prep",
    )(w_in_t, w_in_t)


ACT_NONE, ACT_SILU, ACT_GELU = 0, 1, 2


def _in_proj_kernel(x_ref, g_ref, sh_ref, sc_ref, w_ref, wdt_ref, o_ref, dt_ref, h_scr, *, block_acts):
    tm = x_ref.shape[0]
    rc = 256

    @pl.when(pl.program_id(1) == 0)
    def _():
        gain = g_ref[...]
        scale = 1.0 + sc_ref[...]
        shift = sh_ref[...]

        def body(k, carry):
            r = pl.ds(pl.multiple_of(k * rc, rc), rc)
            xv = x_ref[r, :]
            ms = jnp.mean(xv * xv, axis=-1, keepdims=True)
            h = (xv * lax.rsqrt(ms + NORM_EPS) * gain) * scale + shift
            h_scr[r, :] = h.astype(BF16)
            return carry

        lax.fori_loop(0, tm // rc, body, 0)
        dt_ref[...] = _dot_nt(h_scr[...], wdt_ref[...])

    j = pl.program_id(1)
    for act, fn in ((ACT_NONE, lambda v: v), (ACT_SILU, _silu), (ACT_GELU, _gelu_tanh)):
        blocks = [q for q, a in enumerate(block_acts) if a == act]
        if blocks:
            is_act = functools.reduce(jnp.logical_or, [j == q for q in blocks])

            @pl.when(is_act)
            def _():
                o_ref[...] = fn(_dot_nt(h_scr[...], w_ref[...])).astype(o_ref.dtype)


def _in_proj(x2, mod3, mod_row_of_tile, norm_g, w_main, w_dt, tm, tn, w_blocks=None):
    m = x2.shape[0]
    if w_blocks is None:
        w_blocks = tuple(range(w_main.shape[0] // tn))
    n = len(w_blocks) * tn
    z_end, uv0, uv_end = SSD_WIDTH // tn, 2 * SSD_WIDTH // tn, (2 * SSD_WIDTH + 2 * CM_WIDTH) // tn
    block_acts = tuple(ACT_SILU if b < z_end else ACT_GELU if uv0 <= b < uv_end else ACT_NONE for b in w_blocks)

    def w_blk(j):
        if w_blocks == tuple(range(len(w_blocks))):
            return j
        blk = jnp.int32(w_blocks[-1])
        for q in range(len(w_blocks) - 2, -1, -1):
            blk = jnp.where(j == q, w_blocks[q], blk)
        return blk

    return pl.pallas_call(
        functools.partial(_in_proj_kernel, block_acts=block_acts),
        grid=(m // tm, n // tn),
        in_specs=[
            pl.BlockSpec((tm, D_MODEL), lambda i, j: (i, 0)),
            pl.BlockSpec((1, D_MODEL), lambda i, j: (0, 0)),
            pl.BlockSpec((None, 1, D_MODEL), lambda i, j: (mod_row_of_tile(i), 0, 0)),
            pl.BlockSpec((None, 1, D_MODEL), lambda i, j: (mod_row_of_tile(i), 0, 1)),
            pl.BlockSpec((tn, D_MODEL), lambda i, j: (w_blk(j), 0)),
            pl.BlockSpec((LANES, D_MODEL), lambda i, j: (0, 0)),
        ],
        out_specs=[
            pl.BlockSpec((tm, tn), lambda i, j: (i, j)),
            pl.BlockSpec((tm, LANES), lambda i, j: (i, 0)),
        ],
        out_shape=[
            jax.ShapeDtypeStruct((m, n), BF16),
            jax.ShapeDtypeStruct((m, LANES), F32),
        ],
        scratch_shapes=[pltpu.VMEM((tm, D_MODEL), BF16)],
        compiler_params=_params("parallel", "arbitrary"),
        name="in_proj",
    )(x2, norm_g, mod3, mod3, w_main, w_dt)


def _split3(v):
    hi = v.astype(BF16)
    rem = v - hi.astype(F32)
    mid = rem.astype(BF16)
    lo = (rem - mid.astype(F32)).astype(BF16)
    return hi, mid, lo


GROUP_LANES = 2 * HEADS_PER_GROUP


def _stack_split3(v):
    hi, mid, lo = _split3(v)
    stacked = (hi.astype(F32) + pltpu.roll(mid.astype(F32), GROUP_LANES, 1)
               + pltpu.roll(lo.astype(F32), 2 * GROUP_LANES, 1))
    return stacked.astype(BF16)


def _ssd_dt_kernel(dt_ref, bias_ref, a_ref, dts_ref, cum_ref, src_t_ref):
    hpg = HEADS_PER_GROUP
    ii = lax.broadcasted_iota(jnp.int32, (CHUNK, CHUNK), 0)
    jj = lax.broadcasted_iota(jnp.int32, (CHUNK, CHUNK), 1)
    tri = (ii >= jj).astype(BF16)
    tri3 = jnp.concatenate([tri, tri, tri], axis=1)
    bias = bias_ref[...]
    a_row = a_ref[...]

    def body(k, carry):
        r = pl.ds(pl.multiple_of(k * CHUNK, CHUNK), CHUNK)
        raw = dt_ref[r, :] + bias
        dts = jnp.where(jj < 2 * SSD_HEADS, jnp.maximum(raw, 0.0) + jnp.log1p(jnp.exp(-jnp.abs(raw))), 0.0)
        adt = dts * a_row
        hi, mid, lo = _split3(adt)
        cf = _dot(tri3, jnp.concatenate([hi, mid, lo], axis=0))
        cr = cf[CHUNK - 1:CHUNK, :] - cf + adt
        valid = jj < 2 * SSD_HEADS
        cum = jnp.where(jj % GROUP_LANES < hpg, cf, cr)
        src_t = ((cum - jnp.log(jnp.where(valid, dts, 1.0))) * LOG2_E).T
        rt = pl.ds(pl.multiple_of(k * GROUP_LANES, GROUP_LANES), GROUP_LANES)
        for g in range(SSD_GROUPS):
            g0 = g * GROUP_LANES
            for src, dst in ((cum * LOG2_E, cum_ref), (dts, dts_ref)):
                lanes = src if g == 0 else pltpu.roll(src, LANES - g0, 1)
                dst[g, r, :] = jnp.where(jj < GROUP_LANES, lanes, 0.0)
            src_t_ref[g, rt, :] = src_t[g0:g0 + GROUP_LANES, :]
        return carry

    lax.fori_loop(0, dt_ref.shape[0] // CHUNK, body, 0)


def _ssd_dt(dt, bias_row, a_row, *, rows_per_step):
    m = dt.shape[0]
    row = pl.BlockSpec((1, LANES), lambda i: (0, 0))
    full = pl.BlockSpec((SSD_GROUPS, rows_per_step, LANES), lambda i: (0, i, 0))
    tr = pl.BlockSpec((SSD_GROUPS, rows_per_step // CHUNK * GROUP_LANES, LANES), lambda i: (0, i, 0))
    full_shape = jax.ShapeDtypeStruct((SSD_GROUPS, m, LANES), F32)
    tr_shape = jax.ShapeDtypeStruct((SSD_GROUPS, m // CHUNK * GROUP_LANES, LANES), F32)
    return pl.pallas_call(
        _ssd_dt_kernel,
        grid=(m // rows_per_step,),
        in_specs=[pl.BlockSpec((rows_per_step, LANES), lambda i: (i, 0)), row, row],
        out_specs=[full, full, tr],
        out_shape=[full_shape, full_shape, tr_shape],
        compiler_params=_params("parallel"),
        name="ssd_dt",
    )(dt, bias_row, a_row)


def _ssd_kernel(xs_ref, b_ref, c_ref, dts_ref, cum_ref, src_t_ref, cwx_ref, cbx_ref, cwb_ref, cbb_ref,
                cwc_ref, cbc_ref, dskip_ref, h0f_ref, h0b_ref, *rest,
                n_chunks, conv_tile, conv_period, with_output):
    if with_output:
        y_ref, xs_s, b_s, c_s, hf_s, hb_s, y_s, yb_s = rest
    else:
        hf_out, hb_out, xs_s, b_s, c_s, hf_s, hb_s = rest
    seq = n_chunks * CHUNK
    hpg = HEADS_PER_GROUP

    def conv_tile_fn(src_ref, dst_ref, w_ref, bias_ref):
        cols = src_ref.shape[1]
        w1 = w_ref[1:2, :]
        bias = bias_ref[...]
        row = lax.broadcasted_iota(jnp.int32, (conv_tile, cols), 0) % conv_period
        w0 = jnp.where(row == 0, 0.0, w_ref[0:1, :])
        w2 = jnp.where(row == conv_period - 1, 0.0, w_ref[2:3, :])

        def one_tile(k):
            r = pl.ds(pl.multiple_of(k * conv_tile, conv_tile), conv_tile)
            v = src_ref[r, :].astype(F32)
            y = bias + pltpu.roll(v, 1, 0) * w0 + v * w1 + pltpu.roll(v, conv_tile - 1, 0) * w2
            dst_ref[r, :] = _silu(y).astype(BF16)

        return one_tile

    conv_fns = (conv_tile_fn(xs_ref, xs_s, cwx_ref, cbx_ref), conv_tile_fn(b_ref, b_s, cwb_ref, cbb_ref),
                conv_tile_fn(c_ref, c_s, cwc_ref, cbc_ref))
    lazy_conv = conv_tile == CHUNK
    if not lazy_conv:
        for fn in conv_fns:
            lax.fori_loop(0, seq // conv_tile, lambda k, carry, fn=fn: (fn(k), carry)[1], 0)

    ii = lax.broadcasted_iota(jnp.int32, (CHUNK, CHUNK), 0)
    jj = lax.broadcasted_iota(jnp.int32, (CHUNK, CHUNK), 1)
    lower = ii >= jj
    upper = jj >= ii

    hf_s[...] = h0f_ref[...]
    hb_s[...] = h0b_ref[...]

    head_of_col = lax.broadcasted_iota(jnp.int32, (LANES, GROUP_WIDTH), 1) // SSD_HEAD_DIM
    sel_row = lax.broadcasted_iota(jnp.int32, (LANES, GROUP_WIDTH), 0)
    sel_valid = sel_row < 3 * GROUP_LANES
    sel_f = (sel_valid & (sel_row % GROUP_LANES == head_of_col)).astype(BF16)
    sel_b = (sel_valid & (sel_row % GROUP_LANES == head_of_col + hpg)).astype(BF16)
    low_half = jj < SSD_HEAD_DIM
    group_lane = jj < GROUP_LANES

    def direction_terms(r, sel):
        cum = cum_ref[r, :]
        dts = dts_ref[r, :]
        tot = jnp.where(jj[0:1, :] < hpg, cum[CHUNK - 1:CHUNK, :], cum[0:1, :])
        wst = dts * jnp.exp2(tot - cum)
        eoff = jnp.where(group_lane, jnp.exp2(cum), 0.0)
        edec = jnp.broadcast_to(jnp.where(group_lane[0:1, :], jnp.exp2(tot), 0.0), (8, LANES))
        ex = _dot(_stack_split3(jnp.concatenate([wst, eoff, edec], axis=0)), sel)
        return cum, ex[:CHUNK], ex[CHUNK:2 * CHUNK], ex[2 * CHUNK:2 * CHUNK + 1]

    def state_step(h_s, r, sel):
        cum, w_state, e_off, e_dec = direction_terms(r, sel)
        xc = xs_s[r, :]
        xw = (xc.astype(F32) * w_state).astype(BF16)
        s_new = lax.dot_general(b_s[r, :], xw, (((0,), (0,)), ((), ())), preferred_element_type=F32)
        h = h_s[...]
        y_off = _dot(c_s[r, :], h.astype(BF16)) * e_off
        h_s[...] = h * e_dec + s_new
        return cum, xc, y_off

    def scan_body(k, carry, *, second_half):
        if lazy_conv and not second_half:
            for fn in conv_fns:
                fn(k)
                fn(n_chunks - 1 - k)
        r = pl.ds(pl.multiple_of(k * CHUNK, CHUNK), CHUNK)
        cum, xc, y_off = state_step(hf_s, r, sel_f)
        if with_output:
            g = _dot_nt(c_s[r, :], b_s[r, :])
            rt = pl.ds(pl.multiple_of(k * GROUP_LANES, GROUP_LANES), GROUP_LANES)
            src_t = src_t_ref[rt, :]
            pieces = []
            for pair in range(hpg // 2):
                ms = []
                for h in (2 * pair, 2 * pair + 1):
                    hb = h + hpg
                    mf = jnp.exp2(jnp.where(lower, cum[:, h:h + 1] - src_t[h:h + 1, :], NEG_BIG))
                    mb = jnp.exp2(jnp.where(upper, cum[:, hb:hb + 1] - src_t[hb:hb + 1, :], NEG_BIG))
                    ms.append((g * (mf + mb)).astype(BF16))
                lhs = jnp.concatenate(ms, axis=1)
                xp = xc[:, pair * LANES:(pair + 1) * LANES]
                zero = jnp.zeros_like(xp)
                rhs = jnp.concatenate([jnp.where(low_half, xp, zero), jnp.where(low_half, zero, xp)], axis=0)
                pieces.append(_dot(lhs, rhs))
            y_fwd = jnp.concatenate(pieces, axis=1) + y_off + xc.astype(F32) * dskip_ref[...]
            if second_half:
                y_ref[r, :] = (y_fwd + yb_s[r, :]).astype(y_ref.dtype)
            else:
                y_s[r, :] = y_fwd
        rb = pl.ds(pl.multiple_of((n_chunks - 1 - k) * CHUNK, CHUNK), CHUNK)
        _, _, y_off_b = state_step(hb_s, rb, sel_b)
        if with_output:
            if second_half:
                y_ref[rb, :] = (y_s[rb, :] + y_off_b).astype(y_ref.dtype)
            else:
                yb_s[rb, :] = y_off_b
        return carry

    half = n_chunks // 2
    unroll = 8 if half % 8 == 0 else 1
    lax.fori_loop(0, half, functools.partial(scan_body, second_half=False), 0, unroll=unroll)
    lax.fori_loop(half, n_chunks, functools.partial(scan_body, second_half=True), 0, unroll=unroll)

    if not with_output:
        hf_out[...] = hf_s[...]
        hb_out[...] = hb_s[...]


def _ssd(pm, xs_col0, bc_col0, dt_terms, conv_w, conv_b, dskip_g, h0f, h0b, *, batch, seq, conv_tile, conv_period,
         with_output):
    n_chunks = seq // CHUNK
    dts_g, cum_g, src_t_g = dt_terms
    dt_spec = pl.BlockSpec((None, seq, LANES), lambda b, g: (g, b, 0))
    dt_t_spec = pl.BlockSpec((None, n_chunks * GROUP_LANES, LANES), lambda b, g: (g, b, 0))
    xs_blk0 = xs_col0 // GROUP_WIDTH
    b_blk0 = bc_col0 // SSD_STATE
    c_blk0 = b_blk0 + SSD_GROUPS
    cw_b0 = SSD_WIDTH // SSD_STATE
    cw_c0 = cw_b0 + SSD_GROUPS
    state_spec = pl.BlockSpec((None, None, SSD_STATE, GROUP_WIDTH), lambda b, g: (b, g, 0, 0))
    in_specs = [
        pl.BlockSpec((seq, GROUP_WIDTH), lambda b, g: (b, xs_blk0 + g)),
        pl.BlockSpec((seq, SSD_STATE), lambda b, g: (b, b_blk0 + g)),
        pl.BlockSpec((seq, SSD_STATE), lambda b, g: (b, c_blk0 + g)),
        dt_spec,
        dt_spec,
        dt_t_spec,
        pl.BlockSpec((3, GROUP_WIDTH), lambda b, g: (0, g)),
        pl.BlockSpec((1, GROUP_WIDTH), lambda b, g: (0, g)),
        pl.BlockSpec((3, SSD_STATE), lambda b, g: (0, cw_b0 + g)),
        pl.BlockSpec((1, SSD_STATE), lambda b, g: (0, cw_b0 + g)),
        pl.BlockSpec((3, SSD_STATE), lambda b, g: (0, cw_c0 + g)),
        pl.BlockSpec((1, SSD_STATE), lambda b, g: (0, cw_c0 + g)),
        pl.BlockSpec((None, 1, GROUP_WIDTH), lambda b, g: (g, 0, 0)),
        state_spec,
        state_spec,
    ]
    scratch = [
        pltpu.VMEM((seq, GROUP_WIDTH), BF16),
        pltpu.VMEM((seq, SSD_STATE), BF16),
        pltpu.VMEM((seq, SSD_STATE), BF16),
        pltpu.VMEM((SSD_STATE, GROUP_WIDTH), F32),
        pltpu.VMEM((SSD_STATE, GROUP_WIDTH), F32),
    ]
    if with_output:
        out_specs = pl.BlockSpec((seq, GROUP_WIDTH), lambda b, g: (b, g))
        out_shape = jax.ShapeDtypeStruct((batch * seq, SSD_WIDTH), BF16)
        scratch += [pltpu.VMEM((seq, GROUP_WIDTH), F32)] * 2
    else:
        out_specs = [state_spec, state_spec]
        out_shape = [jax.ShapeDtypeStruct((batch, SSD_GROUPS, SSD_STATE, GROUP_WIDTH), F32)] * 2
    return pl.pallas_call(
        functools.partial(_ssd_kernel, n_chunks=n_chunks, conv_tile=conv_tile, conv_period=conv_period,
                          with_output=with_output),
        grid=(batch, SSD_GROUPS),
        in_specs=in_specs,
        out_specs=out_specs,
        out_shape=out_shape,
        scratch_shapes=scratch,
        compiler_params=_params("parallel", "parallel"),
        name="ssd_out" if with_output else "ssd_ctx",
    )(pm, pm, pm, dts_g, cum_g, src_t_g, conv_w, conv_b, conv_w, conv_b, conv_w, conv_b, dskip_g, h0f, h0b)


def _out_proj_kernel(yd_ref, z_ref, u_ref, v_ref, x_ref, g1_ref, ng_ref, lng_ref, lnb_ref, ws_ref, bst_ref, w_ref,
                     o_ref, mix_cur, mix_nxt):
    tm = yd_ref.shape[0]
    s = pl.program_id(0)
    n = pl.num_programs(0) - 1

    def build_steps():
        steps = []
        for k in range(tm // CHUNK):
            r = pl.ds(k * CHUNK, CHUNK)
            shared = {}

            def ssd_part(r=r):
                a = yd_ref[r, :].astype(F32) * z_ref[r, :].astype(F32)
                ms = jnp.mean(a * a, axis=-1, keepdims=True)
                mix_nxt[r, 0:SSD_WIDTH] = (a * lax.rsqrt(ms + NORM_EPS) * ng_ref[...]).astype(BF16)

            def ln_part(r=r, shared=shared):
                gv = v_ref[r, :].astype(F32)
                mu = jnp.mean(gv, axis=-1, keepdims=True)
                xc = gv - mu
                var = jnp.mean(xc * xc, axis=-1, keepdims=True)
                shared["ln"] = ((xc * lax.rsqrt(var + NORM_EPS)) * lng_ref[...] + lnb_ref[...]).astype(BF16)

            def gate_part(h0, h1, r=r, shared=shared):
                for h in range(h0, h1):
                    c0, c1 = h * CM_HEAD_DIM, (h + 1) * CM_HEAD_DIM
                    sp = _dot(ws_ref[h], shared["ln"][:, c0:c1]) + bst_ref[:, h:h + 1]
                    gu = u_ref[r, c0:c1].astype(F32)
                    mix_nxt[r, SSD_WIDTH + c0:SSD_WIDTH + c1] = (gu * sp).astype(BF16)

            steps += [ssd_part, ln_part, functools.partial(gate_part, 0, CM_HEADS // 2),
                      functools.partial(gate_part, CM_HEADS // 2, CM_HEADS)]
        return steps

    n_proj = 8
    pw = D_MODEL // n_proj

    def project_steps():
        def piece(q):
            c = slice(q * pw, (q + 1) * pw)
            o_ref[:, c] = x_ref[:, c] + g1_ref[:, c] * _dot(mix_cur[...], w_ref[:, c])

        return [functools.partial(piece, q) for q in range(n_proj)]

    @pl.when(s == 0)
    def _():
        for step in build_steps():
            step()

    @pl.when((s > 0) & (s < n))
    def _():
        for proj, build in zip(project_steps(), build_steps()):
            proj()
            build()

    @pl.when(s == n)
    def _():
        for step in project_steps():
            step()

    @pl.when(s < n)
    def _():
        mix_cur[...] = mix_nxt[...]


def _out_proj(yd, pm, x2, mod3, ssd_norm_g, ln_g, ln_b, ws, bst, w_out, *, seq, tm):
    m = x2.shape[0]
    n = m // tm
    tiles_per_batch = seq // tm
    z_blk, u_blk, v_blk = 0, 2 * SSD_WIDTH // CM_WIDTH, 2 * SSD_WIDTH // CM_WIDTH + 1
    g1_blk = 2
    row = lambda shape: pl.BlockSpec(shape, lambda s: (0, 0))
    build_tile = lambda s: jnp.minimum(s, n - 1)
    proj_tile = lambda s: jnp.maximum(s - 1, 0)
    return pl.pallas_call(
        _out_proj_kernel,
        grid=(n + 1,),
        in_specs=[
            pl.BlockSpec((tm, SSD_WIDTH), lambda s: (build_tile(s), 0)),
            pl.BlockSpec((tm, SSD_WIDTH), lambda s: (build_tile(s), z_blk)),
            pl.BlockSpec((tm, CM_WIDTH), lambda s: (build_tile(s), u_blk)),
            pl.BlockSpec((tm, CM_WIDTH), lambda s: (build_tile(s), v_blk)),
            pl.BlockSpec((tm, D_MODEL), lambda s: (proj_tile(s), 0)),
            pl.BlockSpec((None, 1, D_MODEL), lambda s: (proj_tile(s) // tiles_per_batch, 0, g1_blk)),
            row((1, SSD_WIDTH)),
            row((1, CM_WIDTH)),
            row((1, CM_WIDTH)),
            pl.BlockSpec((CM_HEADS, CHUNK, CHUNK), lambda s: (0, 0, 0)),
            row((CHUNK, CM_HEADS)),
            pl.BlockSpec((SSD_WIDTH + CM_WIDTH, D_MODEL), lambda s: (0, 0), pipeline_mode=pl.Buffered(1)),
        ],
        out_specs=pl.BlockSpec((tm, D_MODEL), lambda s: (proj_tile(s), 0)),
        out_shape=jax.ShapeDtypeStruct((m, D_MODEL), F32),
        scratch_shapes=[pltpu.VMEM((tm, SSD_WIDTH + CM_WIDTH), BF16)] * 2,
        compiler_params=_params("arbitrary"),
        name="out_proj",
    )(yd, pm, pm, pm, x2, mod3, ssd_norm_g, ln_g, ln_b, ws, bst, w_out)


def _route(logits):
    lane = lax.broadcasted_iota(jnp.int32, logits.shape, 1)
    lane_f = lane.astype(F32)
    is_group = (lane >= N_EXPERTS) & (lane < N_EXPERTS + N_GROUPS)
    lg = jnp.where(is_group, logits, NEG_BIG)
    mg = jnp.max(lg, axis=1, keepdims=True)
    top_pg = 1.0 / jnp.sum(jnp.exp(lg - mg), axis=1, keepdims=True)
    gi = jnp.min(jnp.where(lg == mg, lane_f, 1e9), axis=1, keepdims=True) - N_EXPERTS
    in_group = (lane < N_EXPERTS) & ((lane // EXPERTS_PER_GROUP).astype(F32) == gi)
    le = jnp.where(in_group, logits, NEG_BIG)
    m1 = jnp.max(le, axis=1, keepdims=True)
    i1 = jnp.min(jnp.where(le == m1, lane_f, 1e9), axis=1, keepdims=True)
    le2 = jnp.where(lane_f == i1, NEG_BIG, le)
    m2 = jnp.max(le2, axis=1, keepdims=True)
    i2 = jnp.min(jnp.where(le2 == m2, lane_f, 1e9), axis=1, keepdims=True)
    e2 = jnp.exp(m2 - m1)
    p1 = 1.0 / (1.0 + e2)
    p2 = e2 * p1
    return i1, i2, p1 * top_pg, p2 * top_pg


RINFO_E1, RINFO_E2, RINFO_W1, RINFO_W2, RINFO_R1, RINFO_R2 = range(6)


def _route_kernel(x1_ref, n2g_ref, sh2_ref, sc2_ref, wr_ref, br_ref, h2_ref, rinfo_ref, counts_ref, cnt_scr):
    tm = x1_ref.shape[0]

    @pl.when(pl.program_id(0) == 0)
    def _():
        cnt_scr[...] = jnp.zeros_like(cnt_scr)

    x1 = x1_ref[...]
    ms = jnp.mean(x1 * x1, axis=-1, keepdims=True)
    h2 = (x1 * lax.rsqrt(ms + NORM_EPS) * n2g_ref[...]) * (1.0 + sc2_ref[...]) + sh2_ref[...]
    h2_ref[...] = h2
    hi, mid, _ = _split3(h2)
    p = _dot(hi, wr_ref[...])
    logits = p[:, :LANES] + p[:, LANES:] + _dot(mid, wr_ref[:, :LANES]) + br_ref[...]
    i1, i2, w1, w2 = _route(logits)

    lane = lax.broadcasted_iota(jnp.int32, (tm, LANES), 1)
    lane_f = lane.astype(F32)
    oh1 = jnp.where(lane_f == i1, 1.0, 0.0)
    oh2 = jnp.where(lane_f == i2, 1.0, 0.0)
    before = (lax.broadcasted_iota(jnp.int32, (tm, tm), 0) > lax.broadcasted_iota(jnp.int32, (tm, tm), 1)).astype(BF16)
    carried = cnt_scr[...]
    tot1 = jnp.sum(oh1, axis=0, keepdims=True)
    r1 = jnp.sum(oh1 * (_dot(before, oh1.astype(BF16)) + carried), axis=1, keepdims=True)
    r2 = jnp.sum(oh2 * (_dot(before, oh2.astype(BF16)) + (carried + tot1)), axis=1, keepdims=True)
    counts = carried + tot1 + jnp.sum(oh2, axis=0, keepdims=True)
    cnt_scr[...] = counts
    counts_ref[...] = jnp.broadcast_to(counts, counts_ref.shape)

    info = jnp.zeros((tm, LANES), F32)
    for k, v in ((RINFO_E1, i1), (RINFO_E2, i2), (RINFO_W1, w1), (RINFO_W2, w2), (RINFO_R1, r1), (RINFO_R2, r2)):
        info = jnp.where(lane == k, v, info)
    rinfo_ref[...] = info


def _route_call(x1, mod3, norm2_g, w_router, b_router, *, seq, tm):
    m = x1.shape[0]
    tiles_per_batch = seq // tm
    row = lambda shape: pl.BlockSpec(shape, lambda i: (0, 0))
    modrow = lambda k: pl.BlockSpec((None, 1, D_MODEL), lambda i: (i // tiles_per_batch, 0, k))
    return pl.pallas_call(
        _route_kernel,
        grid=(m // tm,),
        in_specs=[
            pl.BlockSpec((tm, D_MODEL), lambda i: (i, 0)),
            row((1, D_MODEL)),
            modrow(3),
            modrow(4),
            row((D_MODEL, 2 * LANES)),
            row((1, LANES)),
        ],
        out_specs=[
            pl.BlockSpec((tm, D_MODEL), lambda i: (i, 0)),
            pl.BlockSpec((tm, LANES), lambda i: (i, 0)),
            row((8, LANES)),
        ],
        out_shape=[
            jax.ShapeDtypeStruct((m, D_MODEL), F32),
            jax.ShapeDtypeStruct((m, LANES), F32),
            jax.ShapeDtypeStruct((8, LANES), F32),
        ],
        scratch_shapes=[pltpu.VMEM((1, LANES), F32)],
        compiler_params=_params("arbitrary"),
        name="route",
    )(x1, norm2_g, mod3, mod3, w_router, b_router)


SUBLANES = 8


def _dispatch_kernel(ps_ref, pl_ref, nu_ref, pos_ref, h2_ref, x_hbm, zbuf, zsem, sem, *, tile_rows, n_tiles):
    tm = h2_ref.shape[0]
    n_used = nu_ref[0]

    def zero_fill(phase):
        def go(rows, start):
            getattr(pltpu.make_async_copy(zbuf.at[pl.ds(0, rows), :], x_hbm.at[pl.ds(start, rows), :], zsem), phase)()

        for e in range(N_EXPERTS):
            start, length = ps_ref[e], pl_ref[e]
            head = (SUBLANES - (start & (SUBLANES - 1))) & (SUBLANES - 1)
            for q in range(SUBLANES - 1):
                pl.when(q < head)(functools.partial(go, 1, start + q))
            rem = length - head
            blk = tile_rows // 2
            while blk >= SUBLANES:
                off = pl.multiple_of(start + head + (rem & ~(2 * blk - 1)), SUBLANES)
                pl.when((rem & blk) != 0)(functools.partial(go, blk, off))
                blk //= 2

        def unused(t, carry):
            go(tile_rows, pl.multiple_of(t * tile_rows, tile_rows))
            return carry

        lax.fori_loop(n_used, n_tiles, unused, 0)

    @pl.when(pl.program_id(0) == 0)
    def _():
        zbuf[...] = jnp.zeros_like(zbuf)
        zero_fill("start")

    def body(r, carry):
        for k in range(2):
            pltpu.make_async_copy(h2_ref.at[pl.ds(r, 1), :], x_hbm.at[pl.ds(pos_ref[0, 2 * r + k], 1), :],
                                  sem).start(priority=k)
        return carry

    lax.fori_loop(0, tm, body, 0, unroll=8)
    for k in range(2):
        pltpu.make_async_copy(h2_ref, x_hbm.at[pl.ds(0, tm), :], sem).wait()

    @pl.when(pl.program_id(0) == pl.num_programs(0) - 1)
    def _():
        zero_fill("wait")


def _dispatch_call(pad_start, pad_len, n_used, pos, h2, *, tm, tile_rows, n_tiles):
    m = h2.shape[0]
    grid_spec = pltpu.PrefetchScalarGridSpec(
        num_scalar_prefetch=3,
        grid=(m // tm,),
        in_specs=[
            pl.BlockSpec((None, 1, 2 * tm), lambda i, ps, pn, nu: (i, 0, 0), memory_space=pltpu.SMEM),
            pl.BlockSpec((tm, D_MODEL), lambda i, ps, pn, nu: (i, 0)),
        ],
        out_specs=pl.BlockSpec(memory_space=pl.ANY),
        scratch_shapes=[
            pltpu.VMEM((tile_rows, D_MODEL), F32),
            pltpu.SemaphoreType.DMA(()),
            pltpu.SemaphoreType.DMA(()),
        ],
    )
    return pl.pallas_call(
        functools.partial(_dispatch_kernel, tile_rows=tile_rows, n_tiles=n_tiles),
        grid_spec=grid_spec,
        out_shape=jax.ShapeDtypeStruct((n_tiles * tile_rows, D_MODEL), F32),
        compiler_params=_params("arbitrary"),
        name="dispatch",
    )(pad_start, pad_len, n_used, pos, h2)


def _experts_kernel(te_ref, nu_ref, ord_ref, nxt_ref, x_ref, wg_hbm, wu_hbm, wd_hbm, y_ref,
                    wg_f, wu_f, wd_f, wg_b, wu_b, wd_b, sem):
    j = pl.program_id(0)
    n_used = nu_ref[0]

    def weight_copies(e, slot):
        return [pltpu.make_async_copy(src.at[e], dst.at[slot], sem.at[slot])
                for src, dst in ((wg_hbm, wg_f), (wu_hbm, wu_f), (wd_hbm, wd_f))]

    @pl.when(j == 0)
    def _():
        for cp in weight_copies(te_ref[0], 0):
            cp.start()

    @pl.when((j < n_used) & ((j == 0) | (te_ref[j] != te_ref[jnp.maximum(j - 1, 0)])))
    def _():
        slot = ord_ref[j] % 2
        for cp in weight_copies(te_ref[j], slot):
            cp.wait()

        @pl.when(nxt_ref[j] >= 0)
        def _():
            for cp in weight_copies(nxt_ref[j], 1 - slot):
                cp.start()

        wg_b[...] = wg_f[slot].astype(BF16)
        wu_b[...] = wu_f[slot].astype(BF16)
        wd_b[...] = wd_f[slot].astype(BF16)

    @pl.when(j < n_used)
    def _():
        xt = x_ref[...].astype(BF16)
        hid = (_silu(_dot(xt, wg_b[...])) * _dot(xt, wu_b[...])).astype(BF16)
        y_ref[...] = _dot(hid, wd_b[...])

    @pl.when(j >= n_used)
    def _():
        y_ref[...] = jnp.zeros_like(y_ref)


def _experts_call(tile_expert, n_used, tile_ord, next_expert, x_sorted, wg, wu, wd, *, tm):
    n_tiles = tile_expert.shape[0]
    hbm = pl.BlockSpec(memory_space=pl.ANY)
    grid_spec = pltpu.PrefetchScalarGridSpec(
        num_scalar_prefetch=4,
        grid=(n_tiles,),
        in_specs=[
            pl.BlockSpec((tm, D_MODEL), lambda j, te, nu, od, nx: (jnp.minimum(j, nu[0] - 1), 0)),
            hbm,
            hbm,
            hbm,
        ],
        out_specs=pl.BlockSpec((tm, D_MODEL), lambda j, te, nu, od, nx: (j, 0)),
        scratch_shapes=[
            pltpu.VMEM((2, D_MODEL, EXPERT_FF), F32),
            pltpu.VMEM((2, D_MODEL, EXPERT_FF), F32),
            pltpu.VMEM((2, EXPERT_FF, D_MODEL), F32),
            pltpu.VMEM((D_MODEL, EXPERT_FF), BF16),
            pltpu.VMEM((D_MODEL, EXPERT_FF), BF16),
            pltpu.VMEM((EXPERT_FF, D_MODEL), BF16),
            pltpu.SemaphoreType.DMA((2,)),
        ],
    )
    return pl.pallas_call(
        _experts_kernel,
        grid_spec=grid_spec,
        out_shape=jax.ShapeDtypeStruct((n_tiles * tm, D_MODEL), F32),
        compiler_params=_params("arbitrary"),
        name="experts",
    )(tile_expert, n_used, tile_ord, next_expert, x_sorted, wg, wu, wd)


def _combine_kernel(pos_cur, pos_nxt, x1_ref, g2_ref, nfg_ref, rinfo_ref, y_hbm, o_ref, ybuf, sem):
    i = pl.program_id(0)
    tm = x1_ref.shape[0]
    slot = i % 2

    def row_copy(pos_ref, r, k, s):
        return pltpu.make_async_copy(y_hbm.at[pl.ds(pos_ref[0, 2 * r + k], 1), :], ybuf.at[s, k, pl.ds(r, 1), :],
                                     sem.at[s])

    def wait_tile(s):
        for k in range(2):
            pltpu.make_async_copy(y_hbm.at[pl.ds(0, tm), :], ybuf.at[s, k], sem.at[s]).wait()

    @pl.when(i == 0)
    def _():
        def body(r, carry):
            for k in range(2):
                row_copy(pos_cur, r, k, 0).start()
            return carry

        lax.fori_loop(0, tm, body, 0)

    def step(cur):
        wait_tile(cur)
        for r in range(tm):
            for k in range(2):
                row_copy(pos_nxt, r, k, 1 - cur).start(priority=k)
        info = rinfo_ref[...]
        w1 = info[:, RINFO_W1:RINFO_W1 + 1]
        w2 = info[:, RINFO_W2:RINFO_W2 + 1]
        moe = w1 * ybuf[cur, 0] + w2 * ybuf[cur, 1]
        y = x1_ref[...] + g2_ref[...] * moe
        ms = jnp.mean(y * y, axis=-1, keepdims=True)
        o_ref[...] = y * lax.rsqrt(ms + NORM_EPS) * nfg_ref[...]

        @pl.when(i == pl.num_programs(0) - 1)
        def _():
            wait_tile(1 - cur)

    for cur in range(2):
        pl.when(slot == cur)(functools.partial(step, cur))


def _combine_call(pos, x1, mod3, normf_g, rinfo, y_sorted, *, seq, tm):
    m = x1.shape[0]
    n_tiles = m // tm
    tiles_per_batch = seq // tm
    return pl.pallas_call(
        _combine_kernel,
        grid=(n_tiles,),
        in_specs=[
            pl.BlockSpec((None, 1, 2 * tm), lambda i: (i, 0, 0), memory_space=pltpu.SMEM),
            pl.BlockSpec((None, 1, 2 * tm), lambda i: (jnp.minimum(i + 1, n_tiles - 1), 0, 0),
                         memory_space=pltpu.SMEM),
            pl.BlockSpec((tm, D_MODEL), lambda i: (i, 0)),
            pl.BlockSpec((None, 1, D_MODEL), lambda i: (i // tiles_per_batch, 0, 5)),
            pl.BlockSpec((1, D_MODEL), lambda i: (0, 0)),
            pl.BlockSpec((tm, LANES), lambda i: (i, 0)),
            pl.BlockSpec(memory_space=pl.ANY),
        ],
        out_specs=pl.BlockSpec((tm, D_MODEL), lambda i: (i, 0)),
        out_shape=jax.ShapeDtypeStruct((m, D_MODEL), F32),
        scratch_shapes=[
            pltpu.VMEM((2, 2, tm, D_MODEL), F32),
            pltpu.SemaphoreType.DMA((2,)),
        ],
        compiler_params=_params("arbitrary"),
        name="combine",
    )(pos, pos, x1, mod3, normf_g, rinfo, y_sorted)


def _dispatch_plan(rinfo, counts, *, tm):
    n_tok = rinfo.shape[0]
    n_tiles = (2 * n_tok + N_EXPERTS * (tm - 1)) // tm + 1
    expert = rinfo[:, RINFO_E1:RINFO_E2 + 1].astype(jnp.int32)
    rank = rinfo[:, RINFO_R1:RINFO_R2 + 1].astype(jnp.int32)
    cnt = counts[0, :N_EXPERTS].astype(jnp.int32)
    tiles_e = (cnt + tm - 1) // tm
    end_tile = jnp.cumsum(tiles_e)
    start_row = (end_tile - tiles_e) * tm
    is_e = expert[:, :, None] == jnp.arange(N_EXPERTS, dtype=jnp.int32)
    pos = jnp.sum(jnp.where(is_e, start_row, 0), axis=-1) + rank
    n_used = end_tile[-1:]
    tile_ids = jnp.minimum(jnp.arange(n_tiles, dtype=jnp.int32), n_used - 1)
    tile_expert = jnp.sum(tile_ids[:, None] >= end_tile[None, :], axis=1).astype(jnp.int32)
    pad_rows = ((start_row + cnt).astype(jnp.int32), (tiles_e * tm - cnt).astype(jnp.int32))
    used = (tiles_e > 0).astype(jnp.int32)
    ord_e = jnp.cumsum(used) - used
    ids = jnp.arange(N_EXPERTS, dtype=jnp.int32)
    later = jnp.where((ids[None, :] > ids[:, None]) & (used[None, :] > 0), ids[None, :], N_EXPERTS)
    nxt_e = jnp.min(later, axis=1)
    nxt_e = jnp.where(nxt_e < N_EXPERTS, nxt_e, -1).astype(jnp.int32)
    is_te = tile_expert[:, None] == ids[None, :]
    tile_ord = jnp.sum(jnp.where(is_te, ord_e, 0), axis=1).astype(jnp.int32)
    next_expert = jnp.sum(jnp.where(is_te, nxt_e, 0), axis=1).astype(jnp.int32)
    return pos, (tile_expert, n_used.astype(jnp.int32), tile_ord, next_expert), pad_rows, n_tiles


def kernel(x, c, ctx, c_ctx, w_mod, b_mod, norm1_g, w_in, conv_w, conv_b, dt_bias_f, dt_bias_b, a_log_f, a_log_b,
           d_skip, ssd_norm_g, cm_ln_g, cm_ln_b, w_spatial, b_spatial, w_out, norm2_g, w_router_group,
           b_router_group, w_router_expert, b_router_expert, w_exp_gate, w_exp_up, w_exp_down, normf_g):
    bsz, seq, _ = x.shape
    ctx_len = ctx.shape[1]
    i = 0

    cc = jnp.concatenate([c, c_ctx[None, :], jnp.zeros((MOD_ROWS - bsz - 1, D_MODEL), F32)], axis=0)
    mod = _modulation(cc, w_mod[i], b_mod[i][None, :])
    mod3 = mod.reshape(MOD_ROWS, 1, N_MOD * D_MODEL)

    w_main, w_dt = _w_in_prep(jnp.swapaxes(w_in[i], 0, 1))
    g1row = norm1_g[i][None, :]

    x2 = x.reshape(bsz * seq, D_MODEL)
    ctx2 = ctx.reshape(bsz * ctx_len, D_MODEL)
    tm_in = 1024
    tn_in = 1024
    xs_col0, bc_col0 = SSD_WIDTH, 2 * SSD_WIDTH + 2 * CM_WIDTH
    pm_x, dt_x = _in_proj(x2, mod3, lambda t: t // (seq // tm_in), g1row, w_main, w_dt, tm_in, tn_in)
    ctx_blocks = tuple(range(xs_col0 // tn_in, 2 * SSD_WIDTH // tn_in)) + (bc_col0 // tn_in,)
    pm_c, dt_c = _in_proj(ctx2, mod3, lambda t: bsz, g1row, w_main, w_dt, bsz * ctx_len, tn_in, ctx_blocks)

    def dt_lanes(fwd, bwd):
        both = jnp.stack([fwd.reshape(SSD_GROUPS, HEADS_PER_GROUP), bwd.reshape(SSD_GROUPS, HEADS_PER_GROUP)], axis=1)
        return jnp.pad(both.reshape(-1).astype(F32), (0, LANES - 2 * SSD_HEADS))[None, :]

    dt_bias_row = dt_lanes(dt_bias_f[i], dt_bias_b[i])
    a_row = dt_lanes(-jnp.exp(a_log_f[i].astype(F32)), -jnp.exp(a_log_b[i].astype(F32)))
    dskip_g = jnp.repeat(d_skip[i], SSD_HEAD_DIM).reshape(SSD_GROUPS, 1, GROUP_WIDTH)
    cw = conv_w[i]
    cb = conv_b[i][None, :]
    h_zero = jnp.zeros((bsz, SSD_GROUPS, SSD_STATE, GROUP_WIDTH), F32)

    dt_terms_c = _ssd_dt(dt_c, dt_bias_row, a_row, rows_per_step=bsz * ctx_len)
    dt_terms_x = _ssd_dt(dt_x, dt_bias_row, a_row, rows_per_step=1024)
    hc_f, hc_b = _ssd(pm_c, 0, SSD_WIDTH, dt_terms_c, cw, cb, dskip_g, h_zero, h_zero,
                      batch=bsz, seq=ctx_len, conv_tile=ctx_len, conv_period=ctx_len, with_output=False)
    yd = _ssd(pm_x, xs_col0, bc_col0, dt_terms_x, cw, cb, dskip_g, hc_f, hc_b,
              batch=bsz, seq=seq, conv_tile=CHUNK, conv_period=GRID_W, with_output=True)

    x1 = _out_proj(yd, pm_x, x2, mod3, ssd_norm_g[i][None, :], cm_ln_g[i][None, :], cm_ln_b[i][None, :],
                   w_spatial[i].astype(BF16), b_spatial[i].T, w_out[i].astype(BF16), seq=seq, tm=256)

    w_re = jnp.transpose(w_router_expert[i], (1, 0, 2)).reshape(D_MODEL, N_EXPERTS)
    pad = LANES - N_EXPERTS - N_GROUPS
    w_router = jnp.pad(jnp.concatenate([w_re, w_router_group[i]], axis=1), ((0, 0), (0, pad)))
    w_router_hi = w_router.astype(BF16)
    w_router = jnp.concatenate([w_router_hi, (w_router - w_router_hi.astype(F32)).astype(BF16)], axis=1)
    b_router = jnp.pad(jnp.concatenate([b_router_expert[i].reshape(-1), b_router_group[i]]), (0, pad))[None, :]
    h2, rinfo, counts = _route_call(x1, mod3, norm2_g[i][None, :], w_router, b_router, seq=seq, tm=512)

    tm_e = 256
    pos, tile_tables, pad_rows, n_tiles = _dispatch_plan(rinfo, counts, tm=tm_e)
    tm_d = 512
    x_sorted = _dispatch_call(*pad_rows, tile_tables[1], pos.reshape(bsz * seq // tm_d, 1, 2 * tm_d), h2,
                              tm=tm_d, tile_rows=tm_e, n_tiles=n_tiles)
    wg = w_exp_gate[i].reshape(N_EXPERTS, D_MODEL, EXPERT_FF)
    wu = w_exp_up[i].reshape(N_EXPERTS, D_MODEL, EXPERT_FF)
    wd = w_exp_down[i].reshape(N_EXPERTS, EXPERT_FF, D_MODEL)
    y_sorted = _experts_call(*tile_tables, x_sorted, wg, wu, wd, tm=tm_e)
    tm_c = 256
    out = _combine_call(pos.reshape(bsz * seq // tm_c, 1, 2 * tm_c), x1, mod3, normf_g[None, :], rinfo, y_sorted,
                        seq=seq, tm=tm_c)
    return out.reshape(bsz, seq, D_MODEL)
```

```python
import functools

import jax
import jax.numpy as jnp
from jax import lax
from jax.experimental import pallas as pl
from jax.experimental.pallas import tpu as pltpu

F32 = jnp.float32
BF16 = jnp.bfloat16
HIGHEST = lax.Precision.HIGHEST

D_MODEL = 2048
GRID_W = 64
SSD_WIDTH = 2048
CM_WIDTH = 2048
SSD_HEADS = 32
SSD_HEAD_DIM = 64
SSD_GROUPS = 4
HEADS_PER_GROUP = SSD_HEADS // SSD_GROUPS
GROUP_WIDTH = HEADS_PER_GROUP * SSD_HEAD_DIM
SSD_STATE = 128
CHUNK = 128
BC_WIDTH = SSD_GROUPS * SSD_STATE
CM_HEADS = 8
CM_HEAD_DIM = CM_WIDTH // CM_HEADS
N_GROUPS = 4
N_EXPERTS = 32
EXPERTS_PER_GROUP = 8
EXPERT_FF = 512
N_MOD = 6
NORM_EPS = 1e-6
LANES = 128
NEG_BIG = -1e30
MOD_ROWS = 8
VMEM_LIMIT = 56 * 1024 * 1024


def _params(*sem):
    return pltpu.CompilerParams(dimension_semantics=sem, vmem_limit_bytes=VMEM_LIMIT)


LOG2_E = 1.4426950408889634
GELU_C = 0.7978845608028654
GELU_K1 = -2.0 * GELU_C * LOG2_E
GELU_K3 = GELU_K1 * 0.044715


def _silu(v):
    return v / (1.0 + jnp.exp2(v * (-LOG2_E)))


def _gelu_tanh(v):
    return v / (1.0 + jnp.exp2(v * (GELU_K1 + GELU_K3 * (v * v))))


def _dot(a, b):
    return jnp.dot(a, b, preferred_element_type=F32)


def _dot_nt(a, b):
    return lax.dot_general(a, b, (((1,), (1,)), ((), ())), preferred_element_type=F32)


def _mod_kernel(cc_ref, w_ref, b_ref, o_ref):
    a = _silu(cc_ref[...])
    hi = a.astype(BF16).astype(F32)
    lhs = jnp.concatenate([hi, a - hi], axis=0).astype(BF16)
    r = _dot(lhs, w_ref[...].astype(BF16))
    o_ref[...] = r[:MOD_ROWS] + r[MOD_ROWS:] + b_ref[...]


def _modulation(cc, w_mod, b_mod):
    n = w_mod.shape[1]
    tn = 1024
    return pl.pallas_call(
        _mod_kernel,
        grid=(n // tn,),
        in_specs=[
            pl.BlockSpec((MOD_ROWS, D_MODEL), lambda j: (0, 0)),
            pl.BlockSpec((D_MODEL, tn), lambda j: (0, j)),
            pl.BlockSpec((1, tn), lambda j: (0, j)),
        ],
        out_specs=pl.BlockSpec((MOD_ROWS, tn), lambda j: (0, j)),
        out_shape=jax.ShapeDtypeStruct((MOD_ROWS, n), F32),
        compiler_params=_params("arbitrary"),
        name="modulation",
    )(cc, w_mod, b_mod)


PREP_ROWS = 512
PREP_UV_BLK0 = 2 * SSD_WIDTH // PREP_ROWS
PREP_BC_BLK0 = (2 * SSD_WIDTH + 2 * CM_WIDTH) // PREP_ROWS
PREP_N_BLKS = PREP_BC_BLK0 + 2 * BC_WIDTH // PREP_ROWS
DT_COL0 = 2 * SSD_WIDTH + 2 * BC_WIDTH
UV_COL0 = DT_COL0 + 2 * SSD_HEADS


def _w_in_prep_kernel(a_ref, dt_ref, o_ref, wdt_ref):
    @pl.when(pl.program_id(0) == 0)
    def _():
        hpg = HEADS_PER_GROUP
        parts = []
        for g in range(SSD_GROUPS):
            parts += [dt_ref[hpg * g:hpg * (g + 1), :], dt_ref[SSD_HEADS + hpg * g:SSD_HEADS + hpg * (g + 1), :]]
        parts.append(dt_ref[2 * SSD_HEADS:, :])
        wdt_ref[...] = jnp.concatenate(parts, axis=0).astype(BF16)

    o_ref[...] = a_ref[...].astype(BF16)


def _w_in_prep(w_in_t):
    def src_row(j):
        uv = UV_COL0 + (j - PREP_UV_BLK0) * PREP_ROWS
        bc = 2 * SSD_WIDTH + (j - PREP_BC_BLK0) * PREP_ROWS
        return pl.multiple_of(jnp.where(j < PREP_UV_BLK0, j * PREP_ROWS, jnp.where(j < PREP_BC_BLK0, uv, bc)), 8)

    return pl.pallas_call(
        _w_in_prep_kernel,
        grid=(PREP_N_BLKS,),
        in_specs=[
            pl.BlockSpec((pl.Element(PREP_ROWS), pl.Element(D_MODEL)), lambda j: (src_row(j), 0)),
            pl.BlockSpec((LANES, D_MODEL), lambda j: (DT_COL0 // LANES, 0)),
        ],
        out_specs=[
            pl.BlockSpec((PREP_ROWS, D_MODEL), lambda j: (j, 0)),
            pl.BlockSpec((LANES, D_MODEL), lambda j: (0, 0)),
        ],
        out_shape=[
            jax.ShapeDtypeStruct((PREP_N_BLKS * PREP_ROWS, D_MODEL), BF16),
            jax.ShapeDtypeStruct((LANES, D_MODEL), BF16),
        ],
        compiler_params=_params("arbitrary"),
        name="w_in_prep",
    )(w_in_t, w_in_t)


ACT_NONE, ACT_SILU, ACT_GELU = 0, 1, 2


def _in_proj_kernel(x_ref, g_ref, sh_ref, sc_ref, w_ref, wdt_ref, o_ref, dt_ref, h_scr, *, block_acts):
    tm = x_ref.shape[0]
    rc = 256

    @pl.when(pl.program_id(1) == 0)
    def _():
        gain = g_ref[...]
        scale = 1.0 + sc_ref[...]
        shift = sh_ref[...]

        def body(k, carry):
            r = pl.ds(pl.multiple_of(k * rc, rc), rc)
            xv = x_ref[r, :]
            ms = jnp.mean(xv * xv, axis=-1, keepdims=True)
            h = (xv * lax.rsqrt(ms + NORM_EPS) * gain) * scale + shift
            h_scr[r, :] = h.astype(BF16)
            return carry

        lax.fori_loop(0, tm // rc, body, 0)
        dt_ref[...] = _dot_nt(h_scr[...], wdt_ref[...])

    j = pl.program_id(1)
    for act, fn in ((ACT_NONE, lambda v: v), (ACT_SILU, _silu), (ACT_GELU, _gelu_tanh)):
        blocks = [q for q, a in enumerate(block_acts) if a == act]
        if blocks:
            is_act = functools.reduce(jnp.logical_or, [j == q for q in blocks])

            @pl.when(is_act)
            def _():
                o_ref[...] = fn(_dot_nt(h_scr[...], w_ref[...])).astype(o_ref.dtype)


def _in_proj(x2, mod3, mod_row_of_tile, norm_g, w_main, w_dt, tm, tn, w_blocks=None):
    m = x2.shape[0]
    if w_blocks is None:
        w_blocks = tuple(range(w_main.shape[0] // tn))
    n = len(w_blocks) * tn
    z_end, uv0, uv_end = SSD_WIDTH // tn, 2 * SSD_WIDTH // tn, (2 * SSD_WIDTH + 2 * CM_WIDTH) // tn
    block_acts = tuple(ACT_SILU if b < z_end else ACT_GELU if uv0 <= b < uv_end else ACT_NONE for b in w_blocks)

    def w_blk(j):
        if w_blocks == tuple(range(len(w_blocks))):
            return j
        blk = jnp.int32(w_blocks[-1])
        for q in range(len(w_blocks) - 2, -1, -1):
            blk = jnp.where(j == q, w_blocks[q], blk)
        return blk

    return pl.pallas_call(
        functools.partial(_in_proj_kernel, block_acts=block_acts),
        grid=(m // tm, n // tn),
        in_specs=[
            pl.BlockSpec((tm, D_MODEL), lambda i, j: (i, 0)),
            pl.BlockSpec((1, D_MODEL), lambda i, j: (0, 0)),
            pl.BlockSpec((None, 1, D_MODEL), lambda i, j: (mod_row_of_tile(i), 0, 0)),
            pl.BlockSpec((None, 1, D_MODEL), lambda i, j: (mod_row_of_tile(i), 0, 1)),
            pl.BlockSpec((tn, D_MODEL), lambda i, j: (w_blk(j), 0)),
            pl.BlockSpec((LANES, D_MODEL), lambda i, j: (0, 0)),
        ],
        out_specs=[
            pl.BlockSpec((tm, tn), lambda i, j: (i, j)),
            pl.BlockSpec((tm, LANES), lambda i, j: (i, 0)),
        ],
        out_shape=[
            jax.ShapeDtypeStruct((m, n), BF16),
            jax.ShapeDtypeStruct((m, LANES), F32),
        ],
        scratch_shapes=[pltpu.VMEM((tm, D_MODEL), BF16)],
        compiler_params=_params("parallel", "arbitrary"),
        name="in_proj",
    )(x2, norm_g, mod3, mod3, w_main, w_dt)


def _split3(v):
    hi = v.astype(BF16)
    rem = v - hi.astype(F32)
    mid = rem.astype(BF16)
    lo = (rem - mid.astype(F32)).astype(BF16)
    return hi, mid, lo


GROUP_LANES = 2 * HEADS_PER_GROUP


def _stack_split3(v):
    hi, mid, lo = _split3(v)
    stacked = (hi.astype(F32) + pltpu.roll(mid.astype(F32), GROUP_LANES, 1)
               + pltpu.roll(lo.astype(F32), 2 * GROUP_LANES, 1))
    return stacked.astype(BF16)


def _ssd_dt_kernel(dt_ref, bias_ref, a_ref, dts_ref, cum_ref, src_t_ref):
    hpg = HEADS_PER_GROUP
    ii = lax.broadcasted_iota(jnp.int32, (CHUNK, CHUNK), 0)
    jj = lax.broadcasted_iota(jnp.int32, (CHUNK, CHUNK), 1)
    tri = (ii >= jj).astype(BF16)
    tri3 = jnp.concatenate([tri, tri, tri], axis=1)
    bias = bias_ref[...]
    a_row = a_ref[...]

    def body(k, carry):
        r = pl.ds(pl.multiple_of(k * CHUNK, CHUNK), CHUNK)
        raw = dt_ref[r, :] + bias
        dts = jnp.where(jj < 2 * SSD_HEADS, jnp.maximum(raw, 0.0) + jnp.log1p(jnp.exp(-jnp.abs(raw))), 0.0)
        adt = dts * a_row
        hi, mid, lo = _split3(adt)
        cf = _dot(tri3, jnp.concatenate([hi, mid, lo], axis=0))
        cr = cf[CHUNK - 1:CHUNK, :] - cf + adt
        valid = jj < 2 * SSD_HEADS
        cum = jnp.where(jj % GROUP_LANES < hpg, cf, cr)
        src_t = ((cum - jnp.log(jnp.where(valid, dts, 1.0))) * LOG2_E).T
        rt = pl.ds(pl.multiple_of(k * GROUP_LANES, GROUP_LANES), GROUP_LANES)
        for g in range(SSD_GROUPS):
            g0 = g * GROUP_LANES
            for src, dst in ((cum * LOG2_E, cum_ref), (dts, dts_ref)):
                lanes = src if g == 0 else pltpu.roll(src, LANES - g0, 1)
                dst[g, r, :] = jnp.where(jj < GROUP_LANES, lanes, 0.0)
            src_t_ref[g, rt, :] = src_t[g0:g0 + GROUP_LANES, :]
        return carry

    lax.fori_loop(0, dt_ref.shape[0] // CHUNK, body, 0)


def _ssd_dt(dt, bias_row, a_row, *, rows_per_step):
    m = dt.shape[0]
    row = pl.BlockSpec((1, LANES), lambda i: (0, 0))
    full = pl.BlockSpec((SSD_GROUPS, rows_per_step, LANES), lambda i: (0, i, 0))
    tr = pl.BlockSpec((SSD_GROUPS, rows_per_step // CHUNK * GROUP_LANES, LANES), lambda i: (0, i, 0))
    full_shape = jax.ShapeDtypeStruct((SSD_GROUPS, m, LANES), F32)
    tr_shape = jax.ShapeDtypeStruct((SSD_GROUPS, m // CHUNK * GROUP_LANES, LANES), F32)
    return pl.pallas_call(
        _ssd_dt_kernel,
        grid=(m // rows_per_step,),
        in_specs=[pl.BlockSpec((rows_per_step, LANES), lambda i: (i, 0)), row, row],
        out_specs=[full, full, tr],
        out_shape=[full_shape, full_shape, tr_shape],
        compiler_params=_params("parallel"),
        name="ssd_dt",
    )(dt, bias_row, a_row)


def _ssd_kernel(xs_ref, b_ref, c_ref, dts_ref, cum_ref, src_t_ref, cwx_ref, cbx_ref, cwb_ref, cbb_ref,
                cwc_ref, cbc_ref, dskip_ref, h0f_ref, h0b_ref, *rest,
                n_chunks, conv_tile, conv_period, with_output):
    if with_output:
        y_ref, xs_s, b_s, c_s, hf_s, hb_s, y_s, yb_s = rest
    else:
        hf_out, hb_out, xs_s, b_s, c_s, hf_s, hb_s = rest
    seq = n_chunks * CHUNK
    hpg = HEADS_PER_GROUP

    def conv_tile_fn(src_ref, dst_ref, w_ref, bias_ref):
        cols = src_ref.shape[1]
        w1 = w_ref[1:2, :]
        bias = bias_ref[...]
        row = lax.broadcasted_iota(jnp.int32, (conv_tile, cols), 0) % conv_period
        w0 = jnp.where(row == 0, 0.0, w_ref[0:1, :])
        w2 = jnp.where(row == conv_period - 1, 0.0, w_ref[2:3, :])

        def one_tile(k):
            r = pl.ds(pl.multiple_of(k * conv_tile, conv_tile), conv_tile)
            v = src_ref[r, :].astype(F32)
            y = bias + pltpu.roll(v, 1, 0) * w0 + v * w1 + pltpu.roll(v, conv_tile - 1, 0) * w2
            dst_ref[r, :] = _silu(y).astype(BF16)

        return one_tile

    conv_fns = (conv_tile_fn(xs_ref, xs_s, cwx_ref, cbx_ref), conv_tile_fn(b_ref, b_s, cwb_ref, cbb_ref),
                conv_tile_fn(c_ref, c_s, cwc_ref, cbc_ref))
    lazy_conv = conv_tile == CHUNK
    if not lazy_conv:
        for fn in conv_fns:
            lax.fori_loop(0, seq // conv_tile, lambda k, carry, fn=fn: (fn(k), carry)[1], 0)

    ii = lax.broadcasted_iota(jnp.int32, (CHUNK, CHUNK), 0)
    jj = lax.broadcasted_iota(jnp.int32, (CHUNK, CHUNK), 1)
    lower = ii >= jj
    upper = jj >= ii

    hf_s[...] = h0f_ref[...]
    hb_s[...] = h0b_ref[...]

    head_of_col = lax.broadcasted_iota(jnp.int32, (LANES, GROUP_WIDTH), 1) // SSD_HEAD_DIM
    sel_row = lax.broadcasted_iota(jnp.int32, (LANES, GROUP_WIDTH), 0)
    sel_valid = sel_row < 3 * GROUP_LANES
    sel_f = (sel_valid & (sel_row % GROUP_LANES == head_of_col)).astype(BF16)
    sel_b = (sel_valid & (sel_row % GROUP_LANES == head_of_col + hpg)).astype(BF16)
    low_half = jj < SSD_HEAD_DIM
    group_lane = jj < GROUP_LANES

    def direction_terms(r, sel):
        cum = cum_ref[r, :]
        dts = dts_ref[r, :]
        tot = jnp.where(jj[0:1, :] < hpg, cum[CHUNK - 1:CHUNK, :], cum[0:1, :])
        wst = dts * jnp.exp2(tot - cum)
        eoff = jnp.where(group_lane, jnp.exp2(cum), 0.0)
        edec = jnp.broadcast_to(jnp.where(group_lane[0:1, :], jnp.exp2(tot), 0.0), (8, LANES))
        ex = _dot(_stack_split3(jnp.concatenate([wst, eoff, edec], axis=0)), sel)
        return cum, ex[:CHUNK], ex[CHUNK:2 * CHUNK], ex[2 * CHUNK:2 * CHUNK + 1]

    def state_step(h_s, r, sel):
        cum, w_state, e_off, e_dec = direction_terms(r, sel)
        xc = xs_s[r, :]
        xw = (xc.astype(F32) * w_state).astype(BF16)
        s_new = lax.dot_general(b_s[r, :], xw, (((0,), (0,)), ((), ())), preferred_element_type=F32)
        h = h_s[...]
        y_off = _dot(c_s[r, :], h.astype(BF16)) * e_off
        h_s[...] = h * e_dec + s_new
        return cum, xc, y_off

    def scan_body(k, carry, *, second_half):
        if lazy_conv and not second_half:
            for fn in conv_fns:
                fn(k)
                fn(n_chunks - 1 - k)
        r = pl.ds(pl.multiple_of(k * CHUNK, CHUNK), CHUNK)
        cum, xc, y_off = state_step(hf_s, r, sel_f)
        if with_output:
            g = _dot_nt(c_s[r, :], b_s[r, :])
            rt = pl.ds(pl.multiple_of(k * GROUP_LANES, GROUP_LANES), GROUP_LANES)
            src_t = src_t_ref[rt, :]
            pieces = []
            for pair in range(hpg // 2):
                ms = []
                for h in (2 * pair, 2 * pair + 1):
                    hb = h + hpg
                    mf = jnp.exp2(jnp.where(lower, cum[:, h:h + 1] - src_t[h:h + 1, :], NEG_BIG))
                    mb = jnp.exp2(jnp.where(upper, cum[:, hb:hb + 1] - src_t[hb:hb + 1, :], NEG_BIG))
                    ms.append((g * (mf + mb)).astype(BF16))
                lhs = jnp.concatenate(ms, axis=1)
                xp = xc[:, pair * LANES:(pair + 1) * LANES]
                zero = jnp.zeros_like(xp)
                rhs = jnp.concatenate([jnp.where(low_half, xp, zero), jnp.where(low_half, zero, xp)], axis=0)
                pieces.append(_dot(lhs, rhs))
            y_fwd = jnp.concatenate(pieces, axis=1) + y_off + xc.astype(F32) * dskip_ref[...]
            if second_half:
                y_ref[r, :] = (y_fwd + yb_s[r, :]).astype(y_ref.dtype)
            else:
                y_s[r, :] = y_fwd
        rb = pl.ds(pl.multiple_of((n_chunks - 1 - k) * CHUNK, CHUNK), CHUNK)
        _, _, y_off_b = state_step(hb_s, rb, sel_b)
        if with_output:
            if second_half:
                y_ref[rb, :] = (y_s[rb, :] + y_off_b).astype(y_ref.dtype)
            else:
                yb_s[rb, :] = y_off_b
        return carry

    half = n_chunks // 2
    unroll = 8 if half % 8 == 0 else 1
    lax.fori_loop(0, half, functools.partial(scan_body, second_half=False), 0, unroll=unroll)
    lax.fori_loop(half, n_chunks, functools.partial(scan_body, second_half=True), 0, unroll=unroll)

    if not with_output:
        hf_out[...] = hf_s[...]
        hb_out[...] = hb_s[...]


def _ssd(pm, xs_col0, bc_col0, dt_terms, conv_w, conv_b, dskip_g, h0f, h0b, *, batch, seq, conv_tile, conv_period,
         with_output):
    n_chunks = seq // CHUNK
    dts_g, cum_g, src_t_g = dt_terms
    dt_spec = pl.BlockSpec((None, seq, LANES), lambda b, g: (g, b, 0))
    dt_t_spec = pl.BlockSpec((None, n_chunks * GROUP_LANES, LANES), lambda b, g: (g, b, 0))
    xs_blk0 = xs_col0 // GROUP_WIDTH
    b_blk0 = bc_col0 // SSD_STATE
    c_blk0 = b_blk0 + SSD_GROUPS
    cw_b0 = SSD_WIDTH // SSD_STATE
    cw_c0 = cw_b0 + SSD_GROUPS
    state_spec = pl.BlockSpec((None, None, SSD_STATE, GROUP_WIDTH), lambda b, g: (b, g, 0, 0))
    in_specs = [
        pl.BlockSpec((seq, GROUP_WIDTH), lambda b, g: (b, xs_blk0 + g)),
        pl.BlockSpec((seq, SSD_STATE), lambda b, g: (b, b_blk0 + g)),
        pl.BlockSpec((seq, SSD_STATE), lambda b, g: (b, c_blk0 + g)),
        dt_spec,
        dt_spec,
        dt_t_spec,
        pl.BlockSpec((3, GROUP_WIDTH), lambda b, g: (0, g)),
        pl.BlockSpec((1, GROUP_WIDTH), lambda b, g: (0, g)),
        pl.BlockSpec((3, SSD_STATE), lambda b, g: (0, cw_b0 + g)),
        pl.BlockSpec((1, SSD_STATE), lambda b, g: (0, cw_b0 + g)),
        pl.BlockSpec((3, SSD_STATE), lambda b, g: (0, cw_c0 + g)),
        pl.BlockSpec((1, SSD_STATE), lambda b, g: (0, cw_c0 + g)),
        pl.BlockSpec((None, 1, GROUP_WIDTH), lambda b, g: (g, 0, 0)),
        state_spec,
        state_spec,
    ]
    scratch = [
        pltpu.VMEM((seq, GROUP_WIDTH), BF16),
        pltpu.VMEM((seq, SSD_STATE), BF16),
        pltpu.VMEM((seq, SSD_STATE), BF16),
        pltpu.VMEM((SSD_STATE, GROUP_WIDTH), F32),
        pltpu.VMEM((SSD_STATE, GROUP_WIDTH), F32),
    ]
    if with_output:
        out_specs = pl.BlockSpec((seq, GROUP_WIDTH), lambda b, g: (b, g))
        out_shape = jax.ShapeDtypeStruct((batch * seq, SSD_WIDTH), BF16)
        scratch += [pltpu.VMEM((seq, GROUP_WIDTH), F32)] * 2
    else:
        out_specs = [state_spec, state_spec]
        out_shape = [jax.ShapeDtypeStruct((batch, SSD_GROUPS, SSD_STATE, GROUP_WIDTH), F32)] * 2
    return pl.pallas_call(
        functools.partial(_ssd_kernel, n_chunks=n_chunks, conv_tile=conv_tile, conv_period=conv_period,
                          with_output=with_output),
        grid=(batch, SSD_GROUPS),
        in_specs=in_specs,
        out_specs=out_specs,
        out_shape=out_shape,
        scratch_shapes=scratch,
        compiler_params=_params("parallel", "parallel"),
        name="ssd_out" if with_output else "ssd_ctx",
    )(pm, pm, pm, dts_g, cum_g, src_t_g, conv_w, conv_b, conv_w, conv_b, conv_w, conv_b, dskip_g, h0f, h0b)


def _out_proj_kernel(yd_ref, z_ref, u_ref, v_ref, x_ref, g1_ref, ng_ref, lng_ref, lnb_ref, ws_ref, bst_ref, w_ref,
                     o_ref, mix_cur, mix_nxt):
    tm = yd_ref.shape[0]
    s = pl.program_id(0)
    n = pl.num_programs(0) - 1

    def build_steps():
        steps = []
        for k in range(tm // CHUNK):
            r = pl.ds(k * CHUNK, CHUNK)
            shared = {}

            def ssd_part(r=r):
                a = yd_ref[r, :].astype(F32) * z_ref[r, :].astype(F32)
                ms = jnp.mean(a * a, axis=-1, keepdims=True)
                mix_nxt[r, 0:SSD_WIDTH] = (a * lax.rsqrt(ms + NORM_EPS) * ng_ref[...]).astype(BF16)

            def ln_part(r=r, shared=shared):
                gv = v_ref[r, :].astype(F32)
                mu = jnp.mean(gv, axis=-1, keepdims=True)
                xc = gv - mu
                var = jnp.mean(xc * xc, axis=-1, keepdims=True)
                shared["ln"] = ((xc * lax.rsqrt(var + NORM_EPS)) * lng_ref[...] + lnb_ref[...]).astype(BF16)

            def gate_part(h0, h1, r=r, shared=shared):
                for h in range(h0, h1):
                    c0, c1 = h * CM_HEAD_DIM, (h + 1) * CM_HEAD_DIM
                    sp = _dot(ws_ref[h], shared["ln"][:, c0:c1]) + bst_ref[:, h:h + 1]
                    gu = u_ref[r, c0:c1].astype(F32)
                    mix_nxt[r, SSD_WIDTH + c0:SSD_WIDTH + c1] = (gu * sp).astype(BF16)

            steps += [ssd_part, ln_part, functools.partial(gate_part, 0, CM_HEADS // 2),
                      functools.partial(gate_part, CM_HEADS // 2, CM_HEADS)]
        return steps

    n_proj = 8
    pw = D_MODEL // n_proj

    def project_steps():
        def piece(q):
            c = slice(q * pw, (q + 1) * pw)
            o_ref[:, c] = x_ref[:, c] + g1_ref[:, c] * _dot(mix_cur[...], w_ref[:, c])

        return [functools.partial(piece, q) for q in range(n_proj)]

    @pl.when(s == 0)
    def _():
        for step in build_steps():
            step()

    @pl.when((s > 0) & (s < n))
    def _():
        for proj, build in zip(project_steps(), build_steps()):
            proj()
            build()

    @pl.when(s == n)
    def _():
        for step in project_steps():
            step()

    @pl.when(s < n)
    def _():
        mix_cur[...] = mix_nxt[...]


def _out_proj(yd, pm, x2, mod3, ssd_norm_g, ln_g, ln_b, ws, bst, w_out, *, seq, tm):
    m = x2.shape[0]
    n = m // tm
    tiles_per_batch = seq // tm
    z_blk, u_blk, v_blk = 0, 2 * SSD_WIDTH // CM_WIDTH, 2 * SSD_WIDTH // CM_WIDTH + 1
    g1_blk = 2
    row = lambda shape: pl.BlockSpec(shape, lambda s: (0, 0))
    build_tile = lambda s: jnp.minimum(s, n - 1)
    proj_tile = lambda s: jnp.maximum(s - 1, 0)
    return pl.pallas_call(
        _out_proj_kernel,
        grid=(n + 1,),
        in_specs=[
            pl.BlockSpec((tm, SSD_WIDTH), lambda s: (build_tile(s), 0)),
            pl.BlockSpec((tm, SSD_WIDTH), lambda s: (build_tile(s), z_blk)),
            pl.BlockSpec((tm, CM_WIDTH), lambda s: (build_tile(s), u_blk)),
            pl.BlockSpec((tm, CM_WIDTH), lambda s: (build_tile(s), v_blk)),
            pl.BlockSpec((tm, D_MODEL), lambda s: (proj_tile(s), 0)),
            pl.BlockSpec((None, 1, D_MODEL), lambda s: (proj_tile(s) // tiles_per_batch, 0, g1_blk)),
            row((1, SSD_WIDTH)),
            row((1, CM_WIDTH)),
            row((1, CM_WIDTH)),
            pl.BlockSpec((CM_HEADS, CHUNK, CHUNK), lambda s: (0, 0, 0)),
            row((CHUNK, CM_HEADS)),
            pl.BlockSpec((SSD_WIDTH + CM_WIDTH, D_MODEL), lambda s: (0, 0), pipeline_mode=pl.Buffered(1)),
        ],
        out_specs=pl.BlockSpec((tm, D_MODEL), lambda s: (proj_tile(s), 0)),
        out_shape=jax.ShapeDtypeStruct((m, D_MODEL), F32),
        scratch_shapes=[pltpu.VMEM((tm, SSD_WIDTH + CM_WIDTH), BF16)] * 2,
        compiler_params=_params("arbitrary"),
        name="out_proj",
    )(yd, pm, pm, pm, x2, mod3, ssd_norm_g, ln_g, ln_b, ws, bst, w_out)


def _route(logits):
    lane = lax.broadcasted_iota(jnp.int32, logits.shape, 1)
    lane_f = lane.astype(F32)
    is_group = (lane >= N_EXPERTS) & (lane < N_EXPERTS + N_GROUPS)
    lg = jnp.where(is_group, logits, NEG_BIG)
    mg = jnp.max(lg, axis=1, keepdims=True)
    top_pg = 1.0 / jnp.sum(jnp.exp(lg - mg), axis=1, keepdims=True)
    gi = jnp.min(jnp.where(lg == mg, lane_f, 1e9), axis=1, keepdims=True) - N_EXPERTS
    in_group = (lane < N_EXPERTS) & ((lane // EXPERTS_PER_GROUP).astype(F32) == gi)
    le = jnp.where(in_group, logits, NEG_BIG)
    m1 = jnp.max(le, axis=1, keepdims=True)
    i1 = jnp.min(jnp.where(le == m1, lane_f, 1e9), axis=1, keepdims=True)
    le2 = jnp.where(lane_f == i1, NEG_BIG, le)
    m2 = jnp.max(le2, axis=1, keepdims=True)
    i2 = jnp.min(jnp.where(le2 == m2, lane_f, 1e9), axis=1, keepdims=True)
    e2 = jnp.exp(m2 - m1)
    p1 = 1.0 / (1.0 + e2)
    p2 = e2 * p1
    return i1, i2, p1 * top_pg, p2 * top_pg


RINFO_E1, RINFO_E2, RINFO_W1, RINFO_W2, RINFO_R1, RINFO_R2 = range(6)


def _route_kernel(x1_ref, n2g_ref, sh2_ref, sc2_ref, wr_ref, br_ref, h2_ref, rinfo_ref, counts_ref, cnt_scr):
    tm = x1_ref.shape[0]

    @pl.when(pl.program_id(0) == 0)
    def _():
        cnt_scr[...] = jnp.zeros_like(cnt_scr)

    x1 = x1_ref[...]
    ms = jnp.mean(x1 * x1, axis=-1, keepdims=True)
    h2 = (x1 * lax.rsqrt(ms + NORM_EPS) * n2g_ref[...]) * (1.0 + sc2_ref[...]) + sh2_ref[...]
    h2_ref[...] = h2
    hi, mid, _ = _split3(h2)
    p = _dot(hi, wr_ref[...])
    logits = p[:, :LANES] + p[:, LANES:] + _dot(mid, wr_ref[:, :LANES]) + br_ref[...]
    i1, i2, w1, w2 = _route(logits)

    lane = lax.broadcasted_iota(jnp.int32, (tm, LANES), 1)
    lane_f = lane.astype(F32)
    oh1 = jnp.where(lane_f == i1, 1.0, 0.0)
    oh2 = jnp.where(lane_f == i2, 1.0, 0.0)
    before = (lax.broadcasted_iota(jnp.int32, (tm, tm), 0) > lax.broadcasted_iota(jnp.int32, (tm, tm), 1)).astype(BF16)
    carried = cnt_scr[...]
    tot1 = jnp.sum(oh1, axis=0, keepdims=True)
    r1 = jnp.sum(oh1 * (_dot(before, oh1.astype(BF16)) + carried), axis=1, keepdims=True)
    r2 = jnp.sum(oh2 * (_dot(before, oh2.astype(BF16)) + (carried + tot1)), axis=1, keepdims=True)
    counts = carried + tot1 + jnp.sum(oh2, axis=0, keepdims=True)
    cnt_scr[...] = counts
    counts_ref[...] = jnp.broadcast_to(counts, counts_ref.shape)

    info = jnp.zeros((tm, LANES), F32)
    for k, v in ((RINFO_E1, i1), (RINFO_E2, i2), (RINFO_W1, w1), (RINFO_W2, w2), (RINFO_R1, r1), (RINFO_R2, r2)):
        info = jnp.where(lane == k, v, info)
    rinfo_ref[...] = info


def _route_call(x1, mod3, norm2_g, w_router, b_router, *, seq, tm):
    m = x1.shape[0]
    tiles_per_batch = seq // tm
    row = lambda shape: pl.BlockSpec(shape, lambda i: (0, 0))
    modrow = lambda k: pl.BlockSpec((None, 1, D_MODEL), lambda i: (i // tiles_per_batch, 0, k))
    return pl.pallas_call(
        _route_kernel,
        grid=(m // tm,),
        in_specs=[
            pl.BlockSpec((tm, D_MODEL), lambda i: (i, 0)),
            row((1, D_MODEL)),
            modrow(3),
            modrow(4),
            row((D_MODEL, 2 * LANES)),
            row((1, LANES)),
        ],
        out_specs=[
            pl.BlockSpec((tm, D_MODEL), lambda i: (i, 0)),
            pl.BlockSpec((tm, LANES), lambda i: (i, 0)),
            row((8, LANES)),
        ],
        out_shape=[
            jax.ShapeDtypeStruct((m, D_MODEL), F32),
            jax.ShapeDtypeStruct((m, LANES), F32),
            jax.ShapeDtypeStruct((8, LANES), F32),
        ],
        scratch_shapes=[pltpu.VMEM((1, LANES), F32)],
        compiler_params=_params("arbitrary"),
        name="route",
    )(x1, norm2_g, mod3, mod3, w_router, b_router)


SUBLANES = 8


def _dispatch_kernel(ps_ref, pl_ref, nu_ref, pos_ref, h2_ref, x_hbm, zbuf, zsem, sem, *, tile_rows, n_tiles):
    tm = h2_ref.shape[0]
    n_used = nu_ref[0]

    def zero_fill(phase):
        def go(rows, start):
            getattr(pltpu.make_async_copy(zbuf.at[pl.ds(0, rows), :], x_hbm.at[pl.ds(start, rows), :], zsem), phase)()

        for e in range(N_EXPERTS):
            start, length = ps_ref[e], pl_ref[e]
            head = (SUBLANES - (start & (SUBLANES - 1))) & (SUBLANES - 1)
            for q in range(SUBLANES - 1):
                pl.when(q < head)(functools.partial(go, 1, start + q))
            rem = length - head
            blk = tile_rows // 2
            while blk >= SUBLANES:
                off = pl.multiple_of(start + head + (rem & ~(2 * blk - 1)), SUBLANES)
                pl.when((rem & blk) != 0)(functools.partial(go, blk, off))
                blk //= 2

        def unused(t, carry):
            go(tile_rows, pl.multiple_of(t * tile_rows, tile_rows))
            return carry

        lax.fori_loop(n_used, n_tiles, unused, 0)

    @pl.when(pl.program_id(0) == 0)
    def _():
        zbuf[...] = jnp.zeros_like(zbuf)
        zero_fill("start")

    def body(r, carry):
        for k in range(2):
            pltpu.make_async_copy(h2_ref.at[pl.ds(r, 1), :], x_hbm.at[pl.ds(pos_ref[0, 2 * r + k], 1), :],
                                  sem).start(priority=k)
        return carry

    lax.fori_loop(0, tm, body, 0, unroll=8)
    for k in range(2):
        pltpu.make_async_copy(h2_ref, x_hbm.at[pl.ds(0, tm), :], sem).wait()

    @pl.when(pl.program_id(0) == pl.num_programs(0) - 1)
    def _():
        zero_fill("wait")


def _dispatch_call(pad_start, pad_len, n_used, pos, h2, *, tm, tile_rows, n_tiles):
    m = h2.shape[0]
    grid_spec = pltpu.PrefetchScalarGridSpec(
        num_scalar_prefetch=3,
        grid=(m // tm,),
        in_specs=[
            pl.BlockSpec((None, 1, 2 * tm), lambda i, ps, pn, nu: (i, 0, 0), memory_space=pltpu.SMEM),
            pl.BlockSpec((tm, D_MODEL), lambda i, ps, pn, nu: (i, 0)),
        ],
        out_specs=pl.BlockSpec(memory_space=pl.ANY),
        scratch_shapes=[
            pltpu.VMEM((tile_rows, D_MODEL), F32),
            pltpu.SemaphoreType.DMA(()),
            pltpu.SemaphoreType.DMA(()),
        ],
    )
    return pl.pallas_call(
        functools.partial(_dispatch_kernel, tile_rows=tile_rows, n_tiles=n_tiles),
        grid_spec=grid_spec,
        out_shape=jax.ShapeDtypeStruct((n_tiles * tile_rows, D_MODEL), F32),
        compiler_params=_params("arbitrary"),
        name="dispatch",
    )(pad_start, pad_len, n_used, pos, h2)


def _experts_kernel(te_ref, nu_ref, ord_ref, nxt_ref, x_ref, wg_hbm, wu_hbm, wd_hbm, y_ref,
                    wg_f, wu_f, wd_f, wg_b, wu_b, wd_b, sem):
    j = pl.program_id(0)
    n_used = nu_ref[0]

    def weight_copies(e, slot):
        return [pltpu.make_async_copy(src.at[e], dst.at[slot], sem.at[slot])
                for src, dst in ((wg_hbm, wg_f), (wu_hbm, wu_f), (wd_hbm, wd_f))]

    @pl.when(j == 0)
    def _():
        for cp in weight_copies(te_ref[0], 0):
            cp.start()

    @pl.when((j < n_used) & ((j == 0) | (te_ref[j] != te_ref[jnp.maximum(j - 1, 0)])))
    def _():
        slot = ord_ref[j] % 2
        for cp in weight_copies(te_ref[j], slot):
            cp.wait()

        @pl.when(nxt_ref[j] >= 0)
        def _():
            for cp in weight_copies(nxt_ref[j], 1 - slot):
                cp.start(priority=1)

        wg_b[...] = wg_f[slot].astype(BF16)
        wu_b[...] = wu_f[slot].astype(BF16)
        wd_b[...] = wd_f[slot].astype(BF16)

    @pl.when(j < n_used)
    def _():
        xt = x_ref[...].astype(BF16)
        hid = (_silu(_dot(xt, wg_b[...])) * _dot(xt, wu_b[...])).astype(BF16)
        y_ref[...] = _dot(hid, wd_b[...])

    @pl.when(j >= n_used)
    def _():
        y_ref[...] = jnp.zeros_like(y_ref)


def _experts_call(tile_expert, n_used, tile_ord, next_expert, x_sorted, wg, wu, wd, *, tm):
    n_tiles = tile_expert.shape[0]
    hbm = pl.BlockSpec(memory_space=pl.ANY)
    grid_spec = pltpu.PrefetchScalarGridSpec(
        num_scalar_prefetch=4,
        grid=(n_tiles,),
        in_specs=[
            pl.BlockSpec((tm, D_MODEL), lambda j, te, nu, od, nx: (jnp.minimum(j, nu[0] - 1), 0)),
            hbm,
            hbm,
            hbm,
        ],
        out_specs=pl.BlockSpec((tm, D_MODEL), lambda j, te, nu, od, nx: (j, 0)),
        scratch_shapes=[
            pltpu.VMEM((2, D_MODEL, EXPERT_FF), F32),
            pltpu.VMEM((2, D_MODEL, EXPERT_FF), F32),
            pltpu.VMEM((2, EXPERT_FF, D_MODEL), F32),
            pltpu.VMEM((D_MODEL, EXPERT_FF), BF16),
            pltpu.VMEM((D_MODEL, EXPERT_FF), BF16),
            pltpu.VMEM((EXPERT_FF, D_MODEL), BF16),
            pltpu.SemaphoreType.DMA((2,)),
        ],
    )
    return pl.pallas_call(
        _experts_kernel,
        grid_spec=grid_spec,
        out_shape=jax.ShapeDtypeStruct((n_tiles * tm, D_MODEL), F32),
        compiler_params=_params("arbitrary"),
        name="experts",
    )(tile_expert, n_used, tile_ord, next_expert, x_sorted, wg, wu, wd)


def _combine_kernel(pos_cur, pos_nxt, x1_ref, g2_ref, nfg_ref, rinfo_ref, y_hbm, o_ref, ybuf, sem):
    i = pl.program_id(0)
    tm = x1_ref.shape[0]
    slot = i % 2

    def row_copy(pos_ref, r, k, s):
        return pltpu.make_async_copy(y_hbm.at[pl.ds(pos_ref[0, 2 * r + k], 1), :], ybuf.at[s, k, pl.ds(r, 1), :],
                                     sem.at[s])

    def wait_tile(s):
        for k in range(2):
            pltpu.make_async_copy(y_hbm.at[pl.ds(0, tm), :], ybuf.at[s, k], sem.at[s]).wait()

    @pl.when(i == 0)
    def _():
        def body(r, carry):
            for k in range(2):
                row_copy(pos_cur, r, k, 0).start()
            return carry

        lax.fori_loop(0, tm, body, 0)

    def step(cur):
        wait_tile(cur)
        for r in range(tm):
            for k in range(2):
                row_copy(pos_nxt, r, k, 1 - cur).start(priority=k)
        info = rinfo_ref[...]
        w1 = info[:, RINFO_W1:RINFO_W1 + 1]
        w2 = info[:, RINFO_W2:RINFO_W2 + 1]
        moe = w1 * ybuf[cur, 0] + w2 * ybuf[cur, 1]
        y = x1_ref[...] + g2_ref[...] * moe
        ms = jnp.mean(y * y, axis=-1, keepdims=True)
        o_ref[...] = y * lax.rsqrt(ms + NORM_EPS) * nfg_ref[...]

        @pl.when(i == pl.num_programs(0) - 1)
        def _():
            wait_tile(1 - cur)

    for cur in range(2):
        pl.when(slot == cur)(functools.partial(step, cur))


def _combine_call(pos, x1, mod3, normf_g, rinfo, y_sorted, *, seq, tm):
    m = x1.shape[0]
    n_tiles = m // tm
    tiles_per_batch = seq // tm
    return pl.pallas_call(
        _combine_kernel,
        grid=(n_tiles,),
        in_specs=[
            pl.BlockSpec((None, 1, 2 * tm), lambda i: (i, 0, 0), memory_space=pltpu.SMEM),
            pl.BlockSpec((None, 1, 2 * tm), lambda i: (jnp.minimum(i + 1, n_tiles - 1), 0, 0),
                         memory_space=pltpu.SMEM),
            pl.BlockSpec((tm, D_MODEL), lambda i: (i, 0)),
            pl.BlockSpec((None, 1, D_MODEL), lambda i: (i // tiles_per_batch, 0, 5)),
            pl.BlockSpec((1, D_MODEL), lambda i: (0, 0)),
            pl.BlockSpec((tm, LANES), lambda i: (i, 0)),
            pl.BlockSpec(memory_space=pl.ANY),
        ],
        out_specs=pl.BlockSpec((tm, D_MODEL), lambda i: (i, 0)),
        out_shape=jax.ShapeDtypeStruct((m, D_MODEL), F32),
        scratch_shapes=[
            pltpu.VMEM((2, 2, tm, D_MODEL), F32),
            pltpu.SemaphoreType.DMA((2,)),
        ],
        compiler_params=_params("arbitrary"),
        name="combine",
    )(pos, pos, x1, mod3, normf_g, rinfo, y_sorted)


def _dispatch_plan(rinfo, counts, *, tm):
    n_tok = rinfo.shape[0]
    n_tiles = (2 * n_tok + N_EXPERTS * (tm - 1)) // tm + 1
    expert = rinfo[:, RINFO_E1:RINFO_E2 + 1].astype(jnp.int32)
    rank = rinfo[:, RINFO_R1:RINFO_R2 + 1].astype(jnp.int32)
    cnt = counts[0, :N_EXPERTS].astype(jnp.int32)
    tiles_e = (cnt + tm - 1) // tm
    end_tile = jnp.cumsum(tiles_e)
    start_row = (end_tile - tiles_e) * tm
    is_e = expert[:, :, None] == jnp.arange(N_EXPERTS, dtype=jnp.int32)
    pos = jnp.sum(jnp.where(is_e, start_row, 0), axis=-1) + rank
    n_used = end_tile[-1:]
    tile_ids = jnp.minimum(jnp.arange(n_tiles, dtype=jnp.int32), n_used - 1)
    tile_expert = jnp.sum(tile_ids[:, None] >= end_tile[None, :], axis=1).astype(jnp.int32)
    pad_rows = ((start_row + cnt).astype(jnp.int32), (tiles_e * tm - cnt).astype(jnp.int32))
    used = (tiles_e > 0).astype(jnp.int32)
    ord_e = jnp.cumsum(used) - used
    ids = jnp.arange(N_EXPERTS, dtype=jnp.int32)
    later = jnp.where((ids[None, :] > ids[:, None]) & (used[None, :] > 0), ids[None, :], N_EXPERTS)
    nxt_e = jnp.min(later, axis=1)
    nxt_e = jnp.where(nxt_e < N_EXPERTS, nxt_e, -1).astype(jnp.int32)
    is_te = tile_expert[:, None] == ids[None, :]
    tile_ord = jnp.sum(jnp.where(is_te, ord_e, 0), axis=1).astype(jnp.int32)
    next_expert = jnp.sum(jnp.where(is_te, nxt_e, 0), axis=1).astype(jnp.int32)
    return pos, (tile_expert, n_used.astype(jnp.int32), tile_ord, next_expert), pad_rows, n_tiles


def kernel(x, c, ctx, c_ctx, w_mod, b_mod, norm1_g, w_in, conv_w, conv_b, dt_bias_f, dt_bias_b, a_log_f, a_log_b,
           d_skip, ssd_norm_g, cm_ln_g, cm_ln_b, w_spatial, b_spatial, w_out, norm2_g, w_router_group,
           b_router_group, w_router_expert, b_router_expert, w_exp_gate, w_exp_up, w_exp_down, normf_g):
    bsz, seq, _ = x.shape
    ctx_len = ctx.shape[1]
    i = 0

    cc = jnp.concatenate([c, c_ctx[None, :], jnp.zeros((MOD_ROWS - bsz - 1, D_MODEL), F32)], axis=0)
    mod = _modulation(cc, w_mod[i], b_mod[i][None, :])
    mod3 = mod.reshape(MOD_ROWS, 1, N_MOD * D_MODEL)

    w_main, w_dt = _w_in_prep(jnp.swapaxes(w_in[i], 0, 1))
    g1row = norm1_g[i][None, :]

    x2 = x.reshape(bsz * seq, D_MODEL)
    ctx2 = ctx.reshape(bsz * ctx_len, D_MODEL)
    tm_in = 1024
    tn_in = 1024
    xs_col0, bc_col0 = SSD_WIDTH, 2 * SSD_WIDTH + 2 * CM_WIDTH
    pm_x, dt_x = _in_proj(x2, mod3, lambda t: t // (seq // tm_in), g1row, w_main, w_dt, tm_in, tn_in)
    ctx_blocks = tuple(range(xs_col0 // tn_in, 2 * SSD_WIDTH // tn_in)) + (bc_col0 // tn_in,)
    pm_c, dt_c = _in_proj(ctx2, mod3, lambda t: bsz, g1row, w_main, w_dt, bsz * ctx_len, tn_in, ctx_blocks)

    def dt_lanes(fwd, bwd):
        both = jnp.stack([fwd.reshape(SSD_GROUPS, HEADS_PER_GROUP), bwd.reshape(SSD_GROUPS, HEADS_PER_GROUP)], axis=1)
        return jnp.pad(both.reshape(-1).astype(F32), (0, LANES - 2 * SSD_HEADS))[None, :]

    dt_bias_row = dt_lanes(dt_bias_f[i], dt_bias_b[i])
    a_row = dt_lanes(-jnp.exp(a_log_f[i].astype(F32)), -jnp.exp(a_log_b[i].astype(F32)))
    dskip_g = jnp.repeat(d_skip[i], SSD_HEAD_DIM).reshape(SSD_GROUPS, 1, GROUP_WIDTH)
    cw = conv_w[i]
    cb = conv_b[i][None, :]
    h_zero = jnp.zeros((bsz, SSD_GROUPS, SSD_STATE, GROUP_WIDTH), F32)

    dt_terms_c = _ssd_dt(dt_c, dt_bias_row, a_row, rows_per_step=bsz * ctx_len)
    dt_terms_x = _ssd_dt(dt_x, dt_bias_row, a_row, rows_per_step=1024)
    hc_f, hc_b = _ssd(pm_c, 0, SSD_WIDTH, dt_terms_c, cw, cb, dskip_g, h_zero, h_zero,
                      batch=bsz, seq=ctx_len, conv_tile=ctx_len, conv_period=ctx_len, with_output=False)
    yd = _ssd(pm_x, xs_col0, bc_col0, dt_terms_x, cw, cb, dskip_g, hc_f, hc_b,
              batch=bsz, seq=seq, conv_tile=CHUNK, conv_period=GRID_W, with_output=True)

    x1 = _out_proj(yd, pm_x, x2, mod3, ssd_norm_g[i][None, :], cm_ln_g[i][None, :], cm_ln_b[i][None, :],
                   w_spatial[i].astype(BF16), b_spatial[i].T, w_out[i].astype(BF16), seq=seq, tm=256)

    w_re = jnp.transpose(w_router_expert[i], (1, 0, 2)).reshape(D_MODEL, N_EXPERTS)
    pad = LANES - N_EXPERTS - N_GROUPS
    w_router = jnp.pad(jnp.concatenate([w_re, w_router_group[i]], axis=1), ((0, 0), (0, pad)))
    w_router_hi = w_router.astype(BF16)
    w_router = jnp.concatenate([w_router_hi, (w_router - w_router_hi.astype(F32)).astype(BF16)], axis=1)
    b_router = jnp.pad(jnp.concatenate([b_router_expert[i].reshape(-1), b_router_group[i]]), (0, pad))[None, :]
    h2, rinfo, counts = _route_call(x1, mod3, norm2_g[i][None, :], w_router, b_router, seq=seq, tm=512)

    tm_e = 256
    pos, tile_tables, pad_rows, n_tiles = _dispatch_plan(rinfo, counts, tm=tm_e)
    tm_d = 512
    x_sorted = _dispatch_call(*pad_rows, tile_tables[1], pos.reshape(bsz * seq // tm_d, 1, 2 * tm_d), h2,
                              tm=tm_d, tile_rows=tm_e, n_tiles=n_tiles)
    wg = w_exp_gate[i].reshape(N_EXPERTS, D_MODEL, EXPERT_FF)
    wu = w_exp_up[i].reshape(N_EXPERTS, D_MODEL, EXPERT_FF)
    wd = w_exp_down[i].reshape(N_EXPERTS, EXPERT_FF, D_MODEL)
    y_sorted = _experts_call(*tile_tables, x_sorted, wg, wu, wd, tm=tm_e)
    tm_c = 256
    out = _combine_call(pos.reshape(bsz * seq // tm_c, 1, 2 * tm_c), x1, mod3, normf_g[None, :], rinfo, y_sorted,
                        seq=seq, tm=tm_c)
    return out.reshape(bsz, seq, D_MODEL)
```

```python
import functools

import jax
import jax.numpy as jnp
from jax import lax
from jax.experimental import pallas as pl
from jax.experimental.pallas import tpu as pltpu

F32 = jnp.float32
BF16 = jnp.bfloat16
HIGHEST = lax.Precision.HIGHEST

D_MODEL = 2048
GRID_W = 64
SSD_WIDTH = 2048
CM_WIDTH = 2048
SSD_HEADS = 32
SSD_HEAD_DIM = 64
SSD_GROUPS = 4
HEADS_PER_GROUP = SSD_HEADS // SSD_GROUPS
GROUP_WIDTH = HEADS_PER_GROUP * SSD_HEAD_DIM
SSD_STATE = 128
CHUNK = 128
BC_WIDTH = SSD_GROUPS * SSD_STATE
CM_HEADS = 8
CM_HEAD_DIM = CM_WIDTH // CM_HEADS
N_GROUPS = 4
N_EXPERTS = 32
EXPERTS_PER_GROUP = 8
EXPERT_FF = 512
N_MOD = 6
NORM_EPS = 1e-6
LANES = 128
NEG_BIG = -1e30
MOD_ROWS = 8
VMEM_LIMIT = 56 * 1024 * 1024


def _params(*sem):
    return pltpu.CompilerParams(dimension_semantics=sem, vmem_limit_bytes=VMEM_LIMIT)


LOG2_E = 1.4426950408889634
GELU_C = 0.7978845608028654
GELU_K1 = -2.0 * GELU_C * LOG2_E
GELU_K3 = GELU_K1 * 0.044715


def _silu(v):
    return v / (1.0 + jnp.exp2(v * (-LOG2_E)))


def _gelu_tanh(v):
    return v / (1.0 + jnp.exp2(v * (GELU_K1 + GELU_K3 * (v * v))))


def _dot(a, b):
    return jnp.dot(a, b, preferred_element_type=F32)


def _dot_nt(a, b):
    return lax.dot_general(a, b, (((1,), (1,)), ((), ())), preferred_element_type=F32)


def _mod_kernel(cc_ref, w_ref, b_ref, o_ref):
    a = _silu(cc_ref[...])
    hi = a.astype(BF16).astype(F32)
    lhs = jnp.concatenate([hi, a - hi], axis=0).astype(BF16)
    r = _dot(lhs, w_ref[...].astype(BF16))
    o_ref[...] = r[:MOD_ROWS] + r[MOD_ROWS:] + b_ref[...]


def _modulation(cc, w_mod, b_mod):
    n = w_mod.shape[1]
    tn = 1024
    return pl.pallas_call(
        _mod_kernel,
        grid=(n // tn,),
        in_specs=[
            pl.BlockSpec((MOD_ROWS, D_MODEL), lambda j: (0, 0)),
            pl.BlockSpec((D_MODEL, tn), lambda j: (0, j)),
            pl.BlockSpec((1, tn), lambda j: (0, j)),
        ],
        out_specs=pl.BlockSpec((MOD_ROWS, tn), lambda j: (0, j)),
        out_shape=jax.ShapeDtypeStruct((MOD_ROWS, n), F32),
        compiler_params=_params("arbitrary"),
        name="modulation",
    )(cc, w_mod, b_mod)


PREP_ROWS = 512
PREP_UV_BLK0 = 2 * SSD_WIDTH // PREP_ROWS
PREP_BC_BLK0 = (2 * SSD_WIDTH + 2 * CM_WIDTH) // PREP_ROWS
PREP_N_BLKS = PREP_BC_BLK0 + 2 * BC_WIDTH // PREP_ROWS
DT_COL0 = 2 * SSD_WIDTH + 2 * BC_WIDTH
UV_COL0 = DT_COL0 + 2 * SSD_HEADS


def _w_in_prep_kernel(a_ref, dt_ref, o_ref, wdt_ref):
    @pl.when(pl.program_id(0) == 0)
    def _():
        hpg = HEADS_PER_GROUP
        parts = []
        for g in range(SSD_GROUPS):
            parts += [dt_ref[hpg * g:hpg * (g + 1), :], dt_ref[SSD_HEADS + hpg * g:SSD_HEADS + hpg * (g + 1), :]]
        parts.append(dt_ref[2 * SSD_HEADS:, :])
        wdt_ref[...] = jnp.concatenate(parts, axis=0).astype(BF16)

    o_ref[...] = a_ref[...].astype(BF16)


def _w_in_prep(w_in_t):
    def src_row(j):
        uv = UV_COL0 + (j - PREP_UV_BLK0) * PREP_ROWS
        bc = 2 * SSD_WIDTH + (j - PREP_BC_BLK0) * PREP_ROWS
        return pl.multiple_of(jnp.where(j < PREP_UV_BLK0, j * PREP_ROWS, jnp.where(j < PREP_BC_BLK0, uv, bc)), 8)

    return pl.pallas_call(
        _w_in_prep_kernel,
        grid=(PREP_N_BLKS,),
        in_specs=[
            pl.BlockSpec((pl.Element(PREP_ROWS), pl.Element(D_MODEL)), lambda j: (src_row(j), 0)),
            pl.BlockSpec((LANES, D_MODEL), lambda j: (DT_COL0 // LANES, 0)),
        ],
        out_specs=[
            pl.BlockSpec((PREP_ROWS, D_MODEL), lambda j: (j, 0)),
            pl.BlockSpec((LANES, D_MODEL), lambda j: (0, 0)),
        ],
        out_shape=[
            jax.ShapeDtypeStruct((PREP_N_BLKS * PREP_ROWS, D_MODEL), BF16),
            jax.ShapeDtypeStruct((LANES, D_MODEL), BF16),
        ],
        compiler_params=_params("arbitrary"),
        name="w_in_prep",
    )(w_in_t, w_in_t)


ACT_NONE, ACT_SILU, ACT_GELU = 0, 1, 2


def _in_proj_kernel(x_ref, g_ref, sh_ref, sc_ref, w_ref, wdt_ref, o_ref, dt_ref, h_scr, *, block_acts):
    tm = x_ref.shape[0]
    rc = 256

    @pl.when(pl.program_id(1) == 0)
    def _():
        gain = g_ref[...]
        scale = 1.0 + sc_ref[...]
        shift = sh_ref[...]

        def body(k, carry):
            r = pl.ds(pl.multiple_of(k * rc, rc), rc)
            xv = x_ref[r, :]
            ms = jnp.mean(xv * xv, axis=-1, keepdims=True)
            h = (xv * lax.rsqrt(ms + NORM_EPS) * gain) * scale + shift
            h_scr[r, :] = h.astype(BF16)
            return carry

        lax.fori_loop(0, tm // rc, body, 0)
        dt_ref[...] = _dot_nt(h_scr[...], wdt_ref[...])

    j = pl.program_id(1)
    for act, fn in ((ACT_NONE, lambda v: v), (ACT_SILU, _silu), (ACT_GELU, _gelu_tanh)):
        blocks = [q for q, a in enumerate(block_acts) if a == act]
        if blocks:
            is_act = functools.reduce(jnp.logical_or, [j == q for q in blocks])

            @pl.when(is_act)
            def _():
                o_ref[...] = fn(_dot_nt(h_scr[...], w_ref[...])).astype(o_ref.dtype)


def _in_proj(x2, mod3, mod_row_of_tile, norm_g, w_main, w_dt, tm, tn, w_blocks=None):
    m = x2.shape[0]
    if w_blocks is None:
        w_blocks = tuple(range(w_main.shape[0] // tn))
    n = len(w_blocks) * tn
    z_end, uv0, uv_end = SSD_WIDTH // tn, 2 * SSD_WIDTH // tn, (2 * SSD_WIDTH + 2 * CM_WIDTH) // tn
    block_acts = tuple(ACT_SILU if b < z_end else ACT_GELU if uv0 <= b < uv_end else ACT_NONE for b in w_blocks)

    def w_blk(j):
        if w_blocks == tuple(range(len(w_blocks))):
            return j
        blk = jnp.int32(w_blocks[-1])
        for q in range(len(w_blocks) - 2, -1, -1):
            blk = jnp.where(j == q, w_blocks[q], blk)
        return blk

    return pl.pallas_call(
        functools.partial(_in_proj_kernel, block_acts=block_acts),
        grid=(m // tm, n // tn),
        in_specs=[
            pl.BlockSpec((tm, D_MODEL), lambda i, j: (i, 0)),
            pl.BlockSpec((1, D_MODEL), lambda i, j: (0, 0)),
            pl.BlockSpec((None, 1, D_MODEL), lambda i, j: (mod_row_of_tile(i), 0, 0)),
            pl.BlockSpec((None, 1, D_MODEL), lambda i, j: (mod_row_of_tile(i), 0, 1)),
            pl.BlockSpec((tn, D_MODEL), lambda i, j: (w_blk(j), 0)),
            pl.BlockSpec((LANES, D_MODEL), lambda i, j: (0, 0)),
        ],
        out_specs=[
            pl.BlockSpec((tm, tn), lambda i, j: (i, j)),
            pl.BlockSpec((tm, LANES), lambda i, j: (i, 0)),
        ],
        out_shape=[
            jax.ShapeDtypeStruct((m, n), BF16),
            jax.ShapeDtypeStruct((m, LANES), F32),
        ],
        scratch_shapes=[pltpu.VMEM((tm, D_MODEL), BF16)],
        compiler_params=_params("parallel", "arbitrary"),
        name="in_proj",
    )(x2, norm_g, mod3, mod3, w_main, w_dt)


def _split3(v):
    hi = v.astype(BF16)
    rem = v - hi.astype(F32)
    mid = rem.astype(BF16)
    lo = (rem - mid.astype(F32)).astype(BF16)
    return hi, mid, lo


GROUP_LANES = 2 * HEADS_PER_GROUP


def _stack_split3(v):
    hi, mid, lo = _split3(v)
    stacked = (hi.astype(F32) + pltpu.roll(mid.astype(F32), GROUP_LANES, 1)
               + pltpu.roll(lo.astype(F32), 2 * GROUP_LANES, 1))
    return stacked.astype(BF16)


def _ssd_dt_kernel(dt_ref, bias_ref, a_ref, dts_ref, cum_ref, src_t_ref):
    hpg = HEADS_PER_GROUP
    ii = lax.broadcasted_iota(jnp.int32, (CHUNK, CHUNK), 0)
    jj = lax.broadcasted_iota(jnp.int32, (CHUNK, CHUNK), 1)
    tri = (ii >= jj).astype(BF16)
    tri3 = jnp.concatenate([tri, tri, tri], axis=1)
    bias = bias_ref[...]
    a_row = a_ref[...]

    def body(k, carry):
        r = pl.ds(pl.multiple_of(k * CHUNK, CHUNK), CHUNK)
        raw = dt_ref[r, :] + bias
        dts = jnp.where(jj < 2 * SSD_HEADS, jnp.maximum(raw, 0.0) + jnp.log1p(jnp.exp(-jnp.abs(raw))), 0.0)
        adt = dts * a_row
        hi, mid, lo = _split3(adt)
        cf = _dot(tri3, jnp.concatenate([hi, mid, lo], axis=0))
        cr = cf[CHUNK - 1:CHUNK, :] - cf + adt
        valid = jj < 2 * SSD_HEADS
        cum = jnp.where(jj % GROUP_LANES < hpg, cf, cr)
        src_t = ((cum - jnp.log(jnp.where(valid, dts, 1.0))) * LOG2_E).T
        rt = pl.ds(pl.multiple_of(k * GROUP_LANES, GROUP_LANES), GROUP_LANES)
        for g in range(SSD_GROUPS):
            g0 = g * GROUP_LANES
            for src, dst in ((cum * LOG2_E, cum_ref), (dts, dts_ref)):
                lanes = src if g == 0 else pltpu.roll(src, LANES - g0, 1)
                dst[g, r, :] = jnp.where(jj < GROUP_LANES, lanes, 0.0)
            src_t_ref[g, rt, :] = src_t[g0:g0 + GROUP_LANES, :]
        return carry

    lax.fori_loop(0, dt_ref.shape[0] // CHUNK, body, 0)


def _ssd_dt(dt, bias_row, a_row, *, rows_per_step):
    m = dt.shape[0]
    row = pl.BlockSpec((1, LANES), lambda i: (0, 0))
    full = pl.BlockSpec((SSD_GROUPS, rows_per_step, LANES), lambda i: (0, i, 0))
    tr = pl.BlockSpec((SSD_GROUPS, rows_per_step // CHUNK * GROUP_LANES, LANES), lambda i: (0, i, 0))
    full_shape = jax.ShapeDtypeStruct((SSD_GROUPS, m, LANES), F32)
    tr_shape = jax.ShapeDtypeStruct((SSD_GROUPS, m // CHUNK * GROUP_LANES, LANES), F32)
    return pl.pallas_call(
        _ssd_dt_kernel,
        grid=(m // rows_per_step,),
        in_specs=[pl.BlockSpec((rows_per_step, LANES), lambda i: (i, 0)), row, row],
        out_specs=[full, full, tr],
        out_shape=[full_shape, full_shape, tr_shape],
        compiler_params=_params("parallel"),
        name="ssd_dt",
    )(dt, bias_row, a_row)


def _ssd_kernel(xs_ref, b_ref, c_ref, dts_ref, cum_ref, src_t_ref, cwx_ref, cbx_ref, cwb_ref, cbb_ref,
                cwc_ref, cbc_ref, dskip_ref, h0f_ref, h0b_ref, *rest,
                n_chunks, conv_tile, conv_period, with_output):
    if with_output:
        y_ref, xs_s, b_s, c_s, hf_s, hb_s, y_s, yb_s = rest
    else:
        hf_out, hb_out, xs_s, b_s, c_s, hf_s, hb_s = rest
    seq = n_chunks * CHUNK
    hpg = HEADS_PER_GROUP

    def conv_tile_fn(src_ref, dst_ref, w_ref, bias_ref):
        cols = src_ref.shape[1]
        w1 = w_ref[1:2, :]
        bias = bias_ref[...]
        row = lax.broadcasted_iota(jnp.int32, (conv_tile, cols), 0) % conv_period
        w0 = jnp.where(row == 0, 0.0, w_ref[0:1, :])
        w2 = jnp.where(row == conv_period - 1, 0.0, w_ref[2:3, :])

        def one_tile(k):
            r = pl.ds(pl.multiple_of(k * conv_tile, conv_tile), conv_tile)
            v = src_ref[r, :].astype(F32)
            y = bias + pltpu.roll(v, 1, 0) * w0 + v * w1 + pltpu.roll(v, conv_tile - 1, 0) * w2
            dst_ref[r, :] = _silu(y).astype(BF16)

        return one_tile

    conv_fns = (conv_tile_fn(xs_ref, xs_s, cwx_ref, cbx_ref), conv_tile_fn(b_ref, b_s, cwb_ref, cbb_ref),
                conv_tile_fn(c_ref, c_s, cwc_ref, cbc_ref))
    lazy_conv = conv_tile == CHUNK
    if not lazy_conv:
        for fn in conv_fns:
            lax.fori_loop(0, seq // conv_tile, lambda k, carry, fn=fn: (fn(k), carry)[1], 0)

    ii = lax.broadcasted_iota(jnp.int32, (CHUNK, CHUNK), 0)
    jj = lax.broadcasted_iota(jnp.int32, (CHUNK, CHUNK), 1)
    lower = ii >= jj
    upper = jj >= ii

    hf_s[...] = h0f_ref[...]
    hb_s[...] = h0b_ref[...]

    head_of_col = lax.broadcasted_iota(jnp.int32, (LANES, GROUP_WIDTH), 1) // SSD_HEAD_DIM
    sel_row = lax.broadcasted_iota(jnp.int32, (LANES, GROUP_WIDTH), 0)
    sel_valid = sel_row < 3 * GROUP_LANES
    sel_f = (sel_valid & (sel_row % GROUP_LANES == head_of_col)).astype(BF16)
    sel_b = (sel_valid & (sel_row % GROUP_LANES == head_of_col + hpg)).astype(BF16)
    low_half = jj < SSD_HEAD_DIM
    group_lane = jj < GROUP_LANES

    def direction_terms(r, sel):
        cum = cum_ref[r, :]
        dts = dts_ref[r, :]
        tot = jnp.where(jj[0:1, :] < hpg, cum[CHUNK - 1:CHUNK, :], cum[0:1, :])
        wst = dts * jnp.exp2(tot - cum)
        eoff = jnp.where(group_lane, jnp.exp2(cum), 0.0)
        edec = jnp.broadcast_to(jnp.where(group_lane[0:1, :], jnp.exp2(tot), 0.0), (8, LANES))
        ex = _dot(_stack_split3(jnp.concatenate([wst, eoff, edec], axis=0)), sel)
        return cum, ex[:CHUNK], ex[CHUNK:2 * CHUNK], ex[2 * CHUNK:2 * CHUNK + 1]

    def state_step(h_s, r, sel):
        cum, w_state, e_off, e_dec = direction_terms(r, sel)
        xc = xs_s[r, :]
        xw = (xc.astype(F32) * w_state).astype(BF16)
        s_new = lax.dot_general(b_s[r, :], xw, (((0,), (0,)), ((), ())), preferred_element_type=F32)
        h = h_s[...]
        y_off = _dot(c_s[r, :], h.astype(BF16)) * e_off
        h_s[...] = h * e_dec + s_new
        return cum, xc, y_off

    def scan_body(k, carry, *, second_half):
        if lazy_conv and not second_half:
            for fn in conv_fns:
                fn(k)
                fn(n_chunks - 1 - k)
        r = pl.ds(pl.multiple_of(k * CHUNK, CHUNK), CHUNK)
        cum, xc, y_off = state_step(hf_s, r, sel_f)
        if with_output:
            g = _dot_nt(c_s[r, :], b_s[r, :])
            rt = pl.ds(pl.multiple_of(k * GROUP_LANES, GROUP_LANES), GROUP_LANES)
            src_t = src_t_ref[rt, :]
            pieces = []
            for pair in range(hpg // 2):
                ms = []
                for h in (2 * pair, 2 * pair + 1):
                    hb = h + hpg
                    mf = jnp.exp2(jnp.where(lower, cum[:, h:h + 1] - src_t[h:h + 1, :], NEG_BIG))
                    mb = jnp.exp2(jnp.where(upper, cum[:, hb:hb + 1] - src_t[hb:hb + 1, :], NEG_BIG))
                    ms.append((g * (mf + mb)).astype(BF16))
                lhs = jnp.concatenate(ms, axis=1)
                xp = xc[:, pair * LANES:(pair + 1) * LANES]
                zero = jnp.zeros_like(xp)
                rhs = jnp.concatenate([jnp.where(low_half, xp, zero), jnp.where(low_half, zero, xp)], axis=0)
                pieces.append(_dot(lhs, rhs))
            y_fwd = jnp.concatenate(pieces, axis=1) + y_off + xc.astype(F32) * dskip_ref[...]
            if second_half:
                y_ref[r, :] = (y_fwd + yb_s[r, :]).astype(y_ref.dtype)
            else:
                y_s[r, :] = y_fwd
        rb = pl.ds(pl.multiple_of((n_chunks - 1 - k) * CHUNK, CHUNK), CHUNK)
        _, _, y_off_b = state_step(hb_s, rb, sel_b)
        if with_output:
            if second_half:
                y_ref[rb, :] = (y_s[rb, :] + y_off_b).astype(y_ref.dtype)
            else:
                yb_s[rb, :] = y_off_b
        return carry

    half = n_chunks // 2
    unroll = 8 if half % 8 == 0 else 1
    lax.fori_loop(0, half, functools.partial(scan_body, second_half=False), 0, unroll=unroll)
    lax.fori_loop(half, n_chunks, functools.partial(scan_body, second_half=True), 0, unroll=unroll)

    if not with_output:
        hf_out[...] = hf_s[...]
        hb_out[...] = hb_s[...]


def _ssd(pm, xs_col0, bc_col0, dt_terms, conv_w, conv_b, dskip_g, h0f, h0b, *, batch, seq, conv_tile, conv_period,
         with_output):
    n_chunks = seq // CHUNK
    dts_g, cum_g, src_t_g = dt_terms
    dt_spec = pl.BlockSpec((None, seq, LANES), lambda b, g: (g, b, 0))
    dt_t_spec = pl.BlockSpec((None, n_chunks * GROUP_LANES, LANES), lambda b, g: (g, b, 0))
    xs_blk0 = xs_col0 // GROUP_WIDTH
    b_blk0 = bc_col0 // SSD_STATE
    c_blk0 = b_blk0 + SSD_GROUPS
    cw_b0 = SSD_WIDTH // SSD_STATE
    cw_c0 = cw_b0 + SSD_GROUPS
    state_spec = pl.BlockSpec((None, None, SSD_STATE, GROUP_WIDTH), lambda b, g: (b, g, 0, 0))
    in_specs = [
        pl.BlockSpec((seq, GROUP_WIDTH), lambda b, g: (b, xs_blk0 + g)),
        pl.BlockSpec((seq, SSD_STATE), lambda b, g: (b, b_blk0 + g)),
        pl.BlockSpec((seq, SSD_STATE), lambda b, g: (b, c_blk0 + g)),
        dt_spec,
        dt_spec,
        dt_t_spec,
        pl.BlockSpec((3, GROUP_WIDTH), lambda b, g: (0, g)),
        pl.BlockSpec((1, GROUP_WIDTH), lambda b, g: (0, g)),
        pl.BlockSpec((3, SSD_STATE), lambda b, g: (0, cw_b0 + g)),
        pl.BlockSpec((1, SSD_STATE), lambda b, g: (0, cw_b0 + g)),
        pl.BlockSpec((3, SSD_STATE), lambda b, g: (0, cw_c0 + g)),
        pl.BlockSpec((1, SSD_STATE), lambda b, g: (0, cw_c0 + g)),
        pl.BlockSpec((None, 1, GROUP_WIDTH), lambda b, g: (g, 0, 0)),
        state_spec,
        state_spec,
    ]
    scratch = [
        pltpu.VMEM((seq, GROUP_WIDTH), BF16),
        pltpu.VMEM((seq, SSD_STATE), BF16),
        pltpu.VMEM((seq, SSD_STATE), BF16),
        pltpu.VMEM((SSD_STATE, GROUP_WIDTH), F32),
        pltpu.VMEM((SSD_STATE, GROUP_WIDTH), F32),
    ]
    if with_output:
        out_specs = pl.BlockSpec((seq, GROUP_WIDTH), lambda b, g: (b, g))
        out_shape = jax.ShapeDtypeStruct((batch * seq, SSD_WIDTH), BF16)
        scratch += [pltpu.VMEM((seq, GROUP_WIDTH), F32)] * 2
    else:
        out_specs = [state_spec, state_spec]
        out_shape = [jax.ShapeDtypeStruct((batch, SSD_GROUPS, SSD_STATE, GROUP_WIDTH), F32)] * 2
    return pl.pallas_call(
        functools.partial(_ssd_kernel, n_chunks=n_chunks, conv_tile=conv_tile, conv_period=conv_period,
                          with_output=with_output),
        grid=(batch, SSD_GROUPS),
        in_specs=in_specs,
        out_specs=out_specs,
        out_shape=out_shape,
        scratch_shapes=scratch,
        compiler_params=_params("parallel", "parallel"),
        name="ssd_out" if with_output else "ssd_ctx",
    )(pm, pm, pm, dts_g, cum_g, src_t_g, conv_w, conv_b, conv_w, conv_b, conv_w, conv_b, dskip_g, h0f, h0b)


def _out_proj_kernel(yd_ref, z_ref, u_ref, v_ref, x_ref, g1_ref, ng_ref, lng_ref, lnb_ref, ws_ref, bst_ref, w_ref,
                     o_ref, mix_cur, mix_nxt):
    tm = yd_ref.shape[0]
    s = pl.program_id(0)
    n = pl.num_programs(0) - 1

    def build_steps():
        steps = []
        for k in range(tm // CHUNK):
            r = pl.ds(k * CHUNK, CHUNK)
            shared = {}

            def ssd_part(r=r):
                a = yd_ref[r, :].astype(F32) * z_ref[r, :].astype(F32)
                ms = jnp.mean(a * a, axis=-1, keepdims=True)
                mix_nxt[r, 0:SSD_WIDTH] = (a * lax.rsqrt(ms + NORM_EPS) * ng_ref[...]).astype(BF16)

            def ln_part(r=r, shared=shared):
                gv = v_ref[r, :].astype(F32)
                mu = jnp.mean(gv, axis=-1, keepdims=True)
                xc = gv - mu
                var = jnp.mean(xc * xc, axis=-1, keepdims=True)
                shared["ln"] = ((xc * lax.rsqrt(var + NORM_EPS)) * lng_ref[...] + lnb_ref[...]).astype(BF16)

            def gate_part(h0, h1, r=r, shared=shared):
                for h in range(h0, h1):
                    c0, c1 = h * CM_HEAD_DIM, (h + 1) * CM_HEAD_DIM
                    sp = _dot(ws_ref[h], shared["ln"][:, c0:c1]) + bst_ref[:, h:h + 1]
                    gu = u_ref[r, c0:c1].astype(F32)
                    mix_nxt[r, SSD_WIDTH + c0:SSD_WIDTH + c1] = (gu * sp).astype(BF16)

            steps += [ssd_part, ln_part, functools.partial(gate_part, 0, CM_HEADS // 2),
                      functools.partial(gate_part, CM_HEADS // 2, CM_HEADS)]
        return steps

    n_proj = 8
    pw = D_MODEL // n_proj

    def project_steps():
        def piece(q):
            c = slice(q * pw, (q + 1) * pw)
            o_ref[:, c] = x_ref[:, c] + g1_ref[:, c] * _dot(mix_cur[...], w_ref[:, c])

        return [functools.partial(piece, q) for q in range(n_proj)]

    @pl.when(s == 0)
    def _():
        for step in build_steps():
            step()

    @pl.when((s > 0) & (s < n))
    def _():
        for proj, build in zip(project_steps(), build_steps()):
            proj()
            build()

    @pl.when(s == n)
    def _():
        for step in project_steps():
            step()

    @pl.when(s < n)
    def _():
        mix_cur[...] = mix_nxt[...]


def _out_proj(yd, pm, x2, mod3, ssd_norm_g, ln_g, ln_b, ws, bst, w_out, *, seq, tm):
    m = x2.shape[0]
    n = m // tm
    tiles_per_batch = seq // tm
    z_blk, u_blk, v_blk = 0, 2 * SSD_WIDTH // CM_WIDTH, 2 * SSD_WIDTH // CM_WIDTH + 1
    g1_blk = 2
    row = lambda shape: pl.BlockSpec(shape, lambda s: (0, 0))
    build_tile = lambda s: jnp.minimum(s, n - 1)
    proj_tile = lambda s: jnp.maximum(s - 1, 0)
    return pl.pallas_call(
        _out_proj_kernel,
        grid=(n + 1,),
        in_specs=[
            pl.BlockSpec((tm, SSD_WIDTH), lambda s: (build_tile(s), 0)),
            pl.BlockSpec((tm, SSD_WIDTH), lambda s: (build_tile(s), z_blk)),
            pl.BlockSpec((tm, CM_WIDTH), lambda s: (build_tile(s), u_blk)),
            pl.BlockSpec((tm, CM_WIDTH), lambda s: (build_tile(s), v_blk)),
            pl.BlockSpec((tm, D_MODEL), lambda s: (proj_tile(s), 0)),
            pl.BlockSpec((None, 1, D_MODEL), lambda s: (proj_tile(s) // tiles_per_batch, 0, g1_blk)),
            row((1, SSD_WIDTH)),
            row((1, CM_WIDTH)),
            row((1, CM_WIDTH)),
            pl.BlockSpec((CM_HEADS, CHUNK, CHUNK), lambda s: (0, 0, 0)),
            row((CHUNK, CM_HEADS)),
            pl.BlockSpec((SSD_WIDTH + CM_WIDTH, D_MODEL), lambda s: (0, 0), pipeline_mode=pl.Buffered(1)),
        ],
        out_specs=pl.BlockSpec((tm, D_MODEL), lambda s: (proj_tile(s), 0)),
        out_shape=jax.ShapeDtypeStruct((m, D_MODEL), F32),
        scratch_shapes=[pltpu.VMEM((tm, SSD_WIDTH + CM_WIDTH), BF16)] * 2,
        compiler_params=_params("arbitrary"),
        name="out_proj",
    )(yd, pm, pm, pm, x2, mod3, ssd_norm_g, ln_g, ln_b, ws, bst, w_out)


def _route(logits):
    lane = lax.broadcasted_iota(jnp.int32, logits.shape, 1)
    lane_f = lane.astype(F32)
    is_group = (lane >= N_EXPERTS) & (lane < N_EXPERTS + N_GROUPS)
    lg = jnp.where(is_group, logits, NEG_BIG)
    mg = jnp.max(lg, axis=1, keepdims=True)
    top_pg = 1.0 / jnp.sum(jnp.exp(lg - mg), axis=1, keepdims=True)
    gi = jnp.min(jnp.where(lg == mg, lane_f, 1e9), axis=1, keepdims=True) - N_EXPERTS
    in_group = (lane < N_EXPERTS) & ((lane // EXPERTS_PER_GROUP).astype(F32) == gi)
    le = jnp.where(in_group, logits, NEG_BIG)
    m1 = jnp.max(le, axis=1, keepdims=True)
    i1 = jnp.min(jnp.where(le == m1, lane_f, 1e9), axis=1, keepdims=True)
    le2 = jnp.where(lane_f == i1, NEG_BIG, le)
    m2 = jnp.max(le2, axis=1, keepdims=True)
    i2 = jnp.min(jnp.where(le2 == m2, lane_f, 1e9), axis=1, keepdims=True)
    e2 = jnp.exp(m2 - m1)
    p1 = 1.0 / (1.0 + e2)
    p2 = e2 * p1
    return i1, i2, p1 * top_pg, p2 * top_pg


RINFO_E1, RINFO_E2, RINFO_W1, RINFO_W2, RINFO_R1, RINFO_R2 = range(6)


def _route_kernel(x1_ref, n2g_ref, sh2_ref, sc2_ref, wr_ref, br_ref, h2_ref, rinfo_ref, counts_ref, cnt_scr):
    tm = x1_ref.shape[0]

    @pl.when(pl.program_id(0) == 0)
    def _():
        cnt_scr[...] = jnp.zeros_like(cnt_scr)

    x1 = x1_ref[...]
    ms = jnp.mean(x1 * x1, axis=-1, keepdims=True)
    h2 = (x1 * lax.rsqrt(ms + NORM_EPS) * n2g_ref[...]) * (1.0 + sc2_ref[...]) + sh2_ref[...]
    h2_ref[...] = h2
    hi, mid, _ = _split3(h2)
    p = _dot(hi, wr_ref[...])
    logits = p[:, :LANES] + p[:, LANES:] + _dot(mid, wr_ref[:, :LANES]) + br_ref[...]
    i1, i2, w1, w2 = _route(logits)

    lane = lax.broadcasted_iota(jnp.int32, (tm, LANES), 1)
    lane_f = lane.astype(F32)
    oh1 = jnp.where(lane_f == i1, 1.0, 0.0)
    oh2 = jnp.where(lane_f == i2, 1.0, 0.0)
    before = (lax.broadcasted_iota(jnp.int32, (tm, tm), 0) > lax.broadcasted_iota(jnp.int32, (tm, tm), 1)).astype(BF16)
    carried = cnt_scr[...]
    tot1 = jnp.sum(oh1, axis=0, keepdims=True)
    r1 = jnp.sum(oh1 * (_dot(before, oh1.astype(BF16)) + carried), axis=1, keepdims=True)
    r2 = jnp.sum(oh2 * (_dot(before, oh2.astype(BF16)) + (carried + tot1)), axis=1, keepdims=True)
    counts = carried + tot1 + jnp.sum(oh2, axis=0, keepdims=True)
    cnt_scr[...] = counts
    counts_ref[...] = jnp.broadcast_to(counts, counts_ref.shape)

    info = jnp.zeros((tm, LANES), F32)
    for k, v in ((RINFO_E1, i1), (RINFO_E2, i2), (RINFO_W1, w1), (RINFO_W2, w2), (RINFO_R1, r1), (RINFO_R2, r2)):
        info = jnp.where(lane == k, v, info)
    rinfo_ref[...] = info


def _route_call(x1, mod3, norm2_g, w_router, b_router, *, seq, tm):
    m = x1.shape[0]
    tiles_per_batch = seq // tm
    row = lambda shape: pl.BlockSpec(shape, lambda i: (0, 0))
    modrow = lambda k: pl.BlockSpec((None, 1, D_MODEL), lambda i: (i // tiles_per_batch, 0, k))
    return pl.pallas_call(
        _route_kernel,
        grid=(m // tm,),
        in_specs=[
            pl.BlockSpec((tm, D_MODEL), lambda i: (i, 0)),
            row((1, D_MODEL)),
            modrow(3),
            modrow(4),
            row((D_MODEL, 2 * LANES)),
            row((1, LANES)),
        ],
        out_specs=[
            pl.BlockSpec((tm, D_MODEL), lambda i: (i, 0)),
            pl.BlockSpec((tm, LANES), lambda i: (i, 0)),
            row((8, LANES)),
        ],
        out_shape=[
            jax.ShapeDtypeStruct((m, D_MODEL), F32),
            jax.ShapeDtypeStruct((m, LANES), F32),
            jax.ShapeDtypeStruct((8, LANES), F32),
        ],
        scratch_shapes=[pltpu.VMEM((1, LANES), F32)],
        compiler_params=_params("arbitrary"),
        name="route",
    )(x1, norm2_g, mod3, mod3, w_router, b_router)


SUBLANES = 8


def _dispatch_kernel(ps_ref, pl_ref, nu_ref, pos_ref, h2_ref, x_hbm, zbuf, zsem, sem, *, tile_rows, n_tiles):
    tm = h2_ref.shape[0]
    n_used = nu_ref[0]

    def zero_fill(phase):
        def go(rows, start):
            getattr(pltpu.make_async_copy(zbuf.at[pl.ds(0, rows), :], x_hbm.at[pl.ds(start, rows), :], zsem), phase)()

        for e in range(N_EXPERTS):
            start, length = ps_ref[e], pl_ref[e]
            head = (SUBLANES - (start & (SUBLANES - 1))) & (SUBLANES - 1)
            for q in range(SUBLANES - 1):
                pl.when(q < head)(functools.partial(go, 1, start + q))
            rem = length - head
            blk = tile_rows // 2
            while blk >= SUBLANES:
                off = pl.multiple_of(start + head + (rem & ~(2 * blk - 1)), SUBLANES)
                pl.when((rem & blk) != 0)(functools.partial(go, blk, off))
                blk //= 2

        def unused(t, carry):
            go(tile_rows, pl.multiple_of(t * tile_rows, tile_rows))
            return carry

        lax.fori_loop(n_used, n_tiles, unused, 0)

    @pl.when(pl.program_id(0) == 0)
    def _():
        zbuf[...] = jnp.zeros_like(zbuf)
        zero_fill("start")

    def body(r, carry):
        for k in range(2):
            pltpu.make_async_copy(h2_ref.at[pl.ds(r, 1), :], x_hbm.at[pl.ds(pos_ref[0, 2 * r + k], 1), :],
                                  sem).start(priority=1)
        return carry

    lax.fori_loop(0, tm, body, 0, unroll=8)
    for k in range(2):
        pltpu.make_async_copy(h2_ref, x_hbm.at[pl.ds(0, tm), :], sem).wait()

    @pl.when(pl.program_id(0) == pl.num_programs(0) - 1)
    def _():
        zero_fill("wait")


def _dispatch_call(pad_start, pad_len, n_used, pos, h2, *, tm, tile_rows, n_tiles):
    m = h2.shape[0]
    grid_spec = pltpu.PrefetchScalarGridSpec(
        num_scalar_prefetch=3,
        grid=(m // tm,),
        in_specs=[
            pl.BlockSpec((None, 1, 2 * tm), lambda i, ps, pn, nu: (i, 0, 0), memory_space=pltpu.SMEM),
            pl.BlockSpec((tm, D_MODEL), lambda i, ps, pn, nu: (i, 0)),
        ],
        out_specs=pl.BlockSpec(memory_space=pl.ANY),
        scratch_shapes=[
            pltpu.VMEM((tile_rows, D_MODEL), F32),
            pltpu.SemaphoreType.DMA(()),
            pltpu.SemaphoreType.DMA(()),
        ],
    )
    return pl.pallas_call(
        functools.partial(_dispatch_kernel, tile_rows=tile_rows, n_tiles=n_tiles),
        grid_spec=grid_spec,
        out_shape=jax.ShapeDtypeStruct((n_tiles * tile_rows, D_MODEL), F32),
        compiler_params=_params("arbitrary"),
        name="dispatch",
    )(pad_start, pad_len, n_used, pos, h2)


def _experts_kernel(te_ref, nu_ref, ord_ref, nxt_ref, x_ref, wg_hbm, wu_hbm, wd_hbm, y_ref,
                    wg_f, wu_f, wd_f, wg_b, wu_b, wd_b, sem):
    j = pl.program_id(0)
    n_used = nu_ref[0]

    def weight_copies(e, slot):
        return [pltpu.make_async_copy(src.at[e], dst.at[slot], sem.at[slot])
                for src, dst in ((wg_hbm, wg_f), (wu_hbm, wu_f), (wd_hbm, wd_f))]

    @pl.when(j == 0)
    def _():
        for cp in weight_copies(te_ref[0], 0):
            cp.start()

    @pl.when((j < n_used) & ((j == 0) | (te_ref[j] != te_ref[jnp.maximum(j - 1, 0)])))
    def _():
        slot = ord_ref[j] % 2
        for cp in weight_copies(te_ref[j], slot):
            cp.wait()

        @pl.when(nxt_ref[j] >= 0)
        def _():
            for cp in weight_copies(nxt_ref[j], 1 - slot):
                cp.start(priority=1)

        wg_b[...] = wg_f[slot].astype(BF16)
        wu_b[...] = wu_f[slot].astype(BF16)
        wd_b[...] = wd_f[slot].astype(BF16)

    @pl.when(j < n_used)
    def _():
        xt = x_ref[...].astype(BF16)
        hid = (_silu(_dot(xt, wg_b[...])) * _dot(xt, wu_b[...])).astype(BF16)
        y_ref[...] = _dot(hid, wd_b[...])

    @pl.when(j >= n_used)
    def _():
        y_ref[...] = jnp.zeros_like(y_ref)


def _experts_call(tile_expert, n_used, tile_ord, next_expert, x_sorted, wg, wu, wd, *, tm):
    n_tiles = tile_expert.shape[0]
    hbm = pl.BlockSpec(memory_space=pl.ANY)
    grid_spec = pltpu.PrefetchScalarGridSpec(
        num_scalar_prefetch=4,
        grid=(n_tiles,),
        in_specs=[
            pl.BlockSpec((tm, D_MODEL), lambda j, te, nu, od, nx: (jnp.minimum(j, nu[0] - 1), 0)),
            hbm,
            hbm,
            hbm,
        ],
        out_specs=pl.BlockSpec((tm, D_MODEL), lambda j, te, nu, od, nx: (j, 0)),
        scratch_shapes=[
            pltpu.VMEM((2, D_MODEL, EXPERT_FF), F32),
            pltpu.VMEM((2, D_MODEL, EXPERT_FF), F32),
            pltpu.VMEM((2, EXPERT_FF, D_MODEL), F32),
            pltpu.VMEM((D_MODEL, EXPERT_FF), BF16),
            pltpu.VMEM((D_MODEL, EXPERT_FF), BF16),
            pltpu.VMEM((EXPERT_FF, D_MODEL), BF16),
            pltpu.SemaphoreType.DMA((2,)),
        ],
    )
    return pl.pallas_call(
        _experts_kernel,
        grid_spec=grid_spec,
        out_shape=jax.ShapeDtypeStruct((n_tiles * tm, D_MODEL), F32),
        compiler_params=_params("arbitrary"),
        name="experts",
    )(tile_expert, n_used, tile_ord, next_expert, x_sorted, wg, wu, wd)


def _combine_kernel(pos_cur, pos_nxt, x1_ref, g2_ref, nfg_ref, rinfo_ref, y_hbm, o_ref, ybuf, sem):
    i = pl.program_id(0)
    tm = x1_ref.shape[0]
    slot = i % 2

    def row_copy(pos_ref, r, k, s):
        return pltpu.make_async_copy(y_hbm.at[pl.ds(pos_ref[0, 2 * r + k], 1), :], ybuf.at[s, k, pl.ds(r, 1), :],
                                     sem.at[s])

    def wait_tile(s):
        for k in range(2):
            pltpu.make_async_copy(y_hbm.at[pl.ds(0, tm), :], ybuf.at[s, k], sem.at[s]).wait()

    @pl.when(i == 0)
    def _():
        def body(r, carry):
            for k in range(2):
                row_copy(pos_cur, r, k, 0).start()
            return carry

        lax.fori_loop(0, tm, body, 0)

    def step(cur):
        wait_tile(cur)
        for r in range(tm):
            for k in range(2):
                row_copy(pos_nxt, r, k, 1 - cur).start(priority=1)
        info = rinfo_ref[...]
        w1 = info[:, RINFO_W1:RINFO_W1 + 1]
        w2 = info[:, RINFO_W2:RINFO_W2 + 1]
        moe = w1 * ybuf[cur, 0] + w2 * ybuf[cur, 1]
        y = x1_ref[...] + g2_ref[...] * moe
        ms = jnp.mean(y * y, axis=-1, keepdims=True)
        o_ref[...] = y * lax.rsqrt(ms + NORM_EPS) * nfg_ref[...]

        @pl.when(i == pl.num_programs(0) - 1)
        def _():
            wait_tile(1 - cur)

    for cur in range(2):
        pl.when(slot == cur)(functools.partial(step, cur))


def _combine_call(pos, x1, mod3, normf_g, rinfo, y_sorted, *, seq, tm):
    m = x1.shape[0]
    n_tiles = m // tm
    tiles_per_batch = seq // tm
    return pl.pallas_call(
        _combine_kernel,
        grid=(n_tiles,),
        in_specs=[
            pl.BlockSpec((None, 1, 2 * tm), lambda i: (i, 0, 0), memory_space=pltpu.SMEM),
            pl.BlockSpec((None, 1, 2 * tm), lambda i: (jnp.minimum(i + 1, n_tiles - 1), 0, 0),
                         memory_space=pltpu.SMEM),
            pl.BlockSpec((tm, D_MODEL), lambda i: (i, 0)),
            pl.BlockSpec((None, 1, D_MODEL), lambda i: (i // tiles_per_batch, 0, 5)),
            pl.BlockSpec((1, D_MODEL), lambda i: (0, 0)),
            pl.BlockSpec((tm, LANES), lambda i: (i, 0)),
            pl.BlockSpec(memory_space=pl.ANY),
        ],
        out_specs=pl.BlockSpec((tm, D_MODEL), lambda i: (i, 0)),
        out_shape=jax.ShapeDtypeStruct((m, D_MODEL), F32),
        scratch_shapes=[
            pltpu.VMEM((2, 2, tm, D_MODEL), F32),
            pltpu.SemaphoreType.DMA((2,)),
        ],
        compiler_params=_params("arbitrary"),
        name="combine",
    )(pos, pos, x1, mod3, normf_g, rinfo, y_sorted)


def _dispatch_plan(rinfo, counts, *, tm):
    n_tok = rinfo.shape[0]
    n_tiles = (2 * n_tok + N_EXPERTS * (tm - 1)) // tm + 1
    expert = rinfo[:, RINFO_E1:RINFO_E2 + 1].astype(jnp.int32)
    rank = rinfo[:, RINFO_R1:RINFO_R2 + 1].astype(jnp.int32)
    cnt = counts[0, :N_EXPERTS].astype(jnp.int32)
    tiles_e = (cnt + tm - 1) // tm
    end_tile = jnp.cumsum(tiles_e)
    start_row = (end_tile - tiles_e) * tm
    is_e = expert[:, :, None] == jnp.arange(N_EXPERTS, dtype=jnp.int32)
    pos = jnp.sum(jnp.where(is_e, start_row, 0), axis=-1) + rank
    n_used = end_tile[-1:]
    tile_ids = jnp.minimum(jnp.arange(n_tiles, dtype=jnp.int32), n_used - 1)
    tile_expert = jnp.sum(tile_ids[:, None] >= end_tile[None, :], axis=1).astype(jnp.int32)
    pad_rows = ((start_row + cnt).astype(jnp.int32), (tiles_e * tm - cnt).astype(jnp.int32))
    used = (tiles_e > 0).astype(jnp.int32)
    ord_e = jnp.cumsum(used) - used
    ids = jnp.arange(N_EXPERTS, dtype=jnp.int32)
    later = jnp.where((ids[None, :] > ids[:, None]) & (used[None, :] > 0), ids[None, :], N_EXPERTS)
    nxt_e = jnp.min(later, axis=1)
    nxt_e = jnp.where(nxt_e < N_EXPERTS, nxt_e, -1).astype(jnp.int32)
    is_te = tile_expert[:, None] == ids[None, :]
    tile_ord = jnp.sum(jnp.where(is_te, ord_e, 0), axis=1).astype(jnp.int32)
    next_expert = jnp.sum(jnp.where(is_te, nxt_e, 0), axis=1).astype(jnp.int32)
    return pos, (tile_expert, n_used.astype(jnp.int32), tile_ord, next_expert), pad_rows, n_tiles


def kernel(x, c, ctx, c_ctx, w_mod, b_mod, norm1_g, w_in, conv_w, conv_b, dt_bias_f, dt_bias_b, a_log_f, a_log_b,
           d_skip, ssd_norm_g, cm_ln_g, cm_ln_b, w_spatial, b_spatial, w_out, norm2_g, w_router_group,
           b_router_group, w_router_expert, b_router_expert, w_exp_gate, w_exp_up, w_exp_down, normf_g):
    bsz, seq, _ = x.shape
    ctx_len = ctx.shape[1]
    i = 0

    cc = jnp.concatenate([c, c_ctx[None, :], jnp.zeros((MOD_ROWS - bsz - 1, D_MODEL), F32)], axis=0)
    mod = _modulation(cc, w_mod[i], b_mod[i][None, :])
    mod3 = mod.reshape(MOD_ROWS, 1, N_MOD * D_MODEL)

    w_main, w_dt = _w_in_prep(jnp.swapaxes(w_in[i], 0, 1))
    g1row = norm1_g[i][None, :]

    x2 = x.reshape(bsz * seq, D_MODEL)
    ctx2 = ctx.reshape(bsz * ctx_len, D_MODEL)
    tm_in = 1024
    tn_in = 1024
    xs_col0, bc_col0 = SSD_WIDTH, 2 * SSD_WIDTH + 2 * CM_WIDTH
    pm_x, dt_x = _in_proj(x2, mod3, lambda t: t // (seq // tm_in), g1row, w_main, w_dt, tm_in, tn_in)
    ctx_blocks = tuple(range(xs_col0 // tn_in, 2 * SSD_WIDTH // tn_in)) + (bc_col0 // tn_in,)
    pm_c, dt_c = _in_proj(ctx2, mod3, lambda t: bsz, g1row, w_main, w_dt, bsz * ctx_len, tn_in, ctx_blocks)

    def dt_lanes(fwd, bwd):
        both = jnp.stack([fwd.reshape(SSD_GROUPS, HEADS_PER_GROUP), bwd.reshape(SSD_GROUPS, HEADS_PER_GROUP)], axis=1)
        return jnp.pad(both.reshape(-1).astype(F32), (0, LANES - 2 * SSD_HEADS))[None, :]

    dt_bias_row = dt_lanes(dt_bias_f[i], dt_bias_b[i])
    a_row = dt_lanes(-jnp.exp(a_log_f[i].astype(F32)), -jnp.exp(a_log_b[i].astype(F32)))
    dskip_g = jnp.repeat(d_skip[i], SSD_HEAD_DIM).reshape(SSD_GROUPS, 1, GROUP_WIDTH)
    cw = conv_w[i]
    cb = conv_b[i][None, :]
    h_zero = jnp.zeros((bsz, SSD_GROUPS, SSD_STATE, GROUP_WIDTH), F32)

    dt_terms_c = _ssd_dt(dt_c, dt_bias_row, a_row, rows_per_step=bsz * ctx_len)
    dt_terms_x = _ssd_dt(dt_x, dt_bias_row, a_row, rows_per_step=1024)
    hc_f, hc_b = _ssd(pm_c, 0, SSD_WIDTH, dt_terms_c, cw, cb, dskip_g, h_zero, h_zero,
                      batch=bsz, seq=ctx_len, conv_tile=ctx_len, conv_period=ctx_len, with_output=False)
    yd = _ssd(pm_x, xs_col0, bc_col0, dt_terms_x, cw, cb, dskip_g, hc_f, hc_b,
              batch=bsz, seq=seq, conv_tile=CHUNK, conv_period=GRID_W, with_output=True)

    x1 = _out_proj(yd, pm_x, x2, mod3, ssd_norm_g[i][None, :], cm_ln_g[i][None, :], cm_ln_b[i][None, :],
                   w_spatial[i].astype(BF16), b_spatial[i].T, w_out[i].astype(BF16), seq=seq, tm=256)

    w_re = jnp.transpose(w_router_expert[i], (1, 0, 2)).reshape(D_MODEL, N_EXPERTS)
    pad = LANES - N_EXPERTS - N_GROUPS
    w_router = jnp.pad(jnp.concatenate([w_re, w_router_group[i]], axis=1), ((0, 0), (0, pad)))
    w_router_hi = w_router.astype(BF16)
    w_router = jnp.concatenate([w_router_hi, (w_router - w_router_hi.astype(F32)).astype(BF16)], axis=1)
    b_router = jnp.pad(jnp.concatenate([b_router_expert[i].reshape(-1), b_router_group[i]]), (0, pad))[None, :]
    h2, rinfo, counts = _route_call(x1, mod3, norm2_g[i][None, :], w_router, b_router, seq=seq, tm=512)

    tm_e = 256
    pos, tile_tables, pad_rows, n_tiles = _dispatch_plan(rinfo, counts, tm=tm_e)
    tm_d = 512
    x_sorted = _dispatch_call(*pad_rows, tile_tables[1], pos.reshape(bsz * seq // tm_d, 1, 2 * tm_d), h2,
                              tm=tm_d, tile_rows=tm_e, n_tiles=n_tiles)
    wg = w_exp_gate[i].reshape(N_EXPERTS, D_MODEL, EXPERT_FF)
    wu = w_exp_up[i].reshape(N_EXPERTS, D_MODEL, EXPERT_FF)
    wd = w_exp_down[i].reshape(N_EXPERTS, EXPERT_FF, D_MODEL)
    y_sorted = _experts_call(*tile_tables, x_sorted, wg, wu, wd, tm=tm_e)
    tm_c = 256
    out = _combine_call(pos.reshape(bsz * seq // tm_c, 1, 2 * tm_c), x1, mod3, normf_g[None, :], rinfo, y_sorted,
                        seq=seq, tm=tm_c)
    return out.reshape(bsz, seq, D_MODEL)
```
